```python
import jax, jax.numpy as jnp
from jax import lax
import numpy as np

D_MODEL = 1024
BATCH = 8
SEQ = 4096
DEPTH = 2

N_HEADS = 16
HEAD_DIM = 64
ATTN_WIDTH = N_HEADS * HEAD_DIM
DILATED_PATTERNS = ((128, 1), (512, 4), (2048, 16))
N_GROUPS = len(DILATED_PATTERNS)
FOX_Q_BLOCK = 128
ROT_DIM = HEAD_DIM // 4
ROPE_THETA = 500000.0
D_FF = -(-8 * D_MODEL // (3 * 256)) * 256
RMS_EPS = 1e-6
NEG_INF = -1e30
N_MIXERS = 2
N_A_LAYERS = (DEPTH + 1) // 2
N_B_LAYERS = DEPTH // 2

kernel_name = "hybrid_dilated_fox_swiglu"


def rmsnorm(x, g):
    xf = x.astype(jnp.float32)
    y = xf * lax.rsqrt(jnp.mean(xf * xf, axis=-1, keepdims=True) + RMS_EPS) * g.astype(jnp.float32)
    return y.astype(x.dtype)


def partial_rotary(t, pos):
    half = ROT_DIM // 2
    inv_freq = ROPE_THETA ** (-jnp.arange(half, dtype=jnp.float32) * 2.0 / ROT_DIM)
    ang = pos[:, None] * inv_freq[None, :]
    cos = jnp.cos(ang)[None, :, None, :]
    sin = jnp.sin(ang)[None, :, None, :]
    t1 = t[..., :half].astype(jnp.float32)
    t2 = t[..., half:ROT_DIM].astype(jnp.float32)
    rot = jnp.concatenate([t1 * cos - t2 * sin, t2 * cos + t1 * sin], axis=-1).astype(t.dtype)
    return jnp.concatenate([rot, t[..., ROT_DIM:]], axis=-1)


def dilated_band_attention(q, k, v, dil, steps):
    B, S, H, D = q.shape
    L = S // dil
    blk = steps
    nb = -(-L // blk)
    Lp = nb * blk

    def by_stride(t):
        t = t.reshape(B, L, dil, H, D).transpose(0, 2, 1, 3, 4)
        return jnp.pad(t, ((0, 0), (0, 0), (0, Lp - L), (0, 0), (0, 0)))

    def band(t):
        tp = jnp.pad(t, ((0, 0), (0, 0), (blk, 0), (0, 0), (0, 0))).reshape(B, dil, nb + 1, blk, H, D)
        return jnp.concatenate([tp[:, :, :-1], tp[:, :, 1:]], axis=3)

    qb = by_stride(q).reshape(B, dil, nb, blk, H, D)
    kb = band(by_stride(k))
    vb = band(by_stride(v))

    s = jnp.einsum('bcnihd,bcnjhd->bcnhij', qb, kb).astype(jnp.float32) * (D ** -0.5)
    i = jnp.arange(blk)[:, None]
    j = jnp.arange(2 * blk)[None, :]
    diff = i + blk - j
    key_step = (jnp.arange(nb)[:, None] - 1) * blk + jnp.arange(2 * blk)[None, :]
    valid = ((diff >= 0) & (diff <= steps))[None, :, :] & (key_step >= 0)[:, None, :]
    s = jnp.where(valid[None, None, :, None, :, :], s, NEG_INF)

    m = jnp.max(s, axis=-1, keepdims=True)
    p = jnp.exp(s - m)
    den = jnp.sum(p, axis=-1, keepdims=True)
    o = jnp.einsum('bcnhij,bcnjhd->bcnihd', p / den, vb.astype(jnp.float32))
    lse = (m + jnp.log(den))[..., 0]

    o = o.reshape(B, dil, Lp, H, D)[:, :, :L].transpose(0, 2, 1, 3, 4).reshape(B, S, H, D)
    lse = lse.transpose(0, 1, 2, 4, 3).reshape(B, dil, Lp, H)[:, :, :L].transpose(0, 2, 1, 3).reshape(B, S, H)
    return o, lse


def dilated_mixer(h, w_in, w_out):
    B, S, _ = h.shape
    proj = (h @ w_in).reshape(B, S, N_GROUPS, 3, N_HEADS, HEAD_DIM)
    pos = jnp.arange(S, dtype=jnp.float32)
    outs, lses = [], []
    for g, (window, dil) in enumerate(DILATED_PATTERNS):
        q = partial_rotary(proj[:, :, g, 0], pos)
        k = partial_rotary(proj[:, :, g, 1], pos)
        v = proj[:, :, g, 2]
        o, lse = dilated_band_attention(q, k, v, dil, window // dil)
        outs.append(o)
        lses.append(lse)
    wts = jax.nn.softmax(jnp.stack(lses, axis=0), axis=0)
    o = jnp.einsum('gbsh,gbshd->bshd', wts, jnp.stack(outs, axis=0))
    return o.reshape(B, S, ATTN_WIDTH).astype(h.dtype) @ w_out


def forgetting_mixer(h, w_in, b_f, w_out):
    B, S, _ = h.shape
    proj = h @ w_in
    qkv = proj[..., :3 * ATTN_WIDTH].reshape(B, S, 3, N_HEADS, HEAD_DIM)
    q, k, v = qkv[:, :, 0], qkv[:, :, 1], qkv[:, :, 2]
    log_f = jax.nn.log_sigmoid(proj[..., 3 * ATTN_WIDTH:].astype(jnp.float32) + b_f.astype(jnp.float32))
    c = lax.cumsum(log_f, axis=1)
    nq = S // FOX_Q_BLOCK
    qb = q.reshape(B, nq, FOX_Q_BLOCK, N_HEADS, HEAD_DIM).transpose(1, 0, 2, 3, 4)
    cqb = c.reshape(B, nq, FOX_Q_BLOCK, N_HEADS).transpose(1, 0, 3, 2)
    ck = c.transpose(0, 2, 1)
    key_pos = jnp.arange(S)
    vf = v.astype(jnp.float32)
    scale = HEAD_DIM ** -0.5

    def block(args):
        qi, ci, n = args
        s = jnp.einsum('bihd,bjhd->bhij', qi, k).astype(jnp.float32) * scale
        s = s + ci[..., None] - ck[:, :, None, :]
        qpos = n * FOX_Q_BLOCK + jnp.arange(FOX_Q_BLOCK)
        s = jnp.where(key_pos[None, :] <= qpos[:, None], s, NEG_INF)
        p = jax.nn.softmax(s, axis=-1)
        return jnp.einsum('bhij,bjhd->bihd', p, vf)

    o = lax.map(block, (qb, cqb, jnp.arange(nq)))
    o = o.transpose(1, 0, 2, 3, 4).reshape(B, S, ATTN_WIDTH).astype(h.dtype)
    return o @ w_out


def swiglu(h, w_gu, w_down):
    gu = h @ w_gu
    g, u = gu[..., :D_FF], gu[..., D_FF:]
    return (jax.nn.silu(g) * u) @ w_down


def _fwd_setup_inputs(seed: int = 0) -> dict:
    key = jax.random.key(seed)
    ks = jax.random.split(key, 13)
    f32 = jnp.float32
    x = jax.random.normal(ks[0], (BATCH, SEQ, D_MODEL), f32)
    a_norm = 1.0 + 0.02 * jax.random.normal(ks[1], (N_A_LAYERS, D_MODEL), f32)
    a_w_in = jax.random.normal(ks[2], (N_A_LAYERS, D_MODEL, N_GROUPS * 3 * ATTN_WIDTH), f32) * D_MODEL ** -0.5
    a_w_out = jax.random.normal(ks[3], (N_A_LAYERS, ATTN_WIDTH, D_MODEL), f32) * ATTN_WIDTH ** -0.5
    b_norm = 1.0 + 0.02 * jax.random.normal(ks[4], (N_B_LAYERS, D_MODEL), f32)
    b_w_in = jax.random.normal(ks[5], (N_B_LAYERS, D_MODEL, 3 * ATTN_WIDTH + N_HEADS), f32) * D_MODEL ** -0.5
    b_f = jnp.linspace(1.0, 6.0, N_HEADS, dtype=f32)[None, :] + 0.1 * jax.random.normal(ks[6], (N_B_LAYERS, N_HEADS), f32)
    b_w_out = jax.random.normal(ks[7], (N_B_LAYERS, ATTN_WIDTH, D_MODEL), f32) * ATTN_WIDTH ** -0.5
    ffn_norm = 1.0 + 0.02 * jax.random.normal(ks[8], (DEPTH, D_MODEL), f32)
    ffn_w_gu = jax.random.normal(ks[9], (DEPTH, D_MODEL, 2 * D_FF), f32) * D_MODEL ** -0.5
    ffn_w_down = jax.random.normal(ks[10], (DEPTH, D_FF, D_MODEL), f32) * D_FF ** -0.5
    final_norm = 1.0 + 0.02 * jax.random.normal(ks[11], (D_MODEL,), f32)
    return {"x": x, "a_norm": a_norm, "a_w_in": a_w_in, "a_w_out": a_w_out,
            "b_norm": b_norm, "b_w_in": b_w_in, "b_f": b_f, "b_w_out": b_w_out,
            "ffn_norm": ffn_norm, "ffn_w_gu": ffn_w_gu, "ffn_w_down": ffn_w_down,
            "final_norm": final_norm}


def _fwd_reference(x, a_norm, a_w_in, a_w_out, b_norm, b_w_in, b_f, b_w_out,
              ffn_norm, ffn_w_gu, ffn_w_down, final_norm):
    h = x
    for i in range(DEPTH):
        j = i // N_MIXERS
        if i % N_MIXERS == 0:
            h = h + dilated_mixer(rmsnorm(h, a_norm[j]), a_w_in[j], a_w_out[j])
        else:
            h = h + forgetting_mixer(rmsnorm(h, b_norm[j]), b_w_in[j], b_f[j], b_w_out[j])
        h = h + swiglu(rmsnorm(h, ffn_norm[i]), ffn_w_gu[i], ffn_w_down[i])
    return rmsnorm(h, final_norm)


import jax as _jax
import jax.numpy as _jnp

TWIN_FORMAT = 'train_step'
FWD_PARAMS = ['x', 'a_norm', 'a_w_in', 'a_w_out', 'b_norm', 'b_w_in', 'b_f', 'b_w_out', 'ffn_norm', 'ffn_w_gu', 'ffn_w_down', 'final_norm']
TWIN_WEIGHTS = ['a_norm', 'a_w_in', 'a_w_out', 'b_norm', 'b_w_in', 'b_f', 'b_w_out', 'ffn_norm', 'ffn_w_gu', 'ffn_w_down', 'final_norm']
TWIN_DIFF_INPUT = 'x'
TWIN_INPUTS = ['x', 'a_norm', 'a_w_in', 'a_w_out', 'b_norm', 'b_w_in', 'b_f', 'b_w_out', 'ffn_norm', 'ffn_w_gu', 'ffn_w_down', 'final_norm', 'loss_target', 'm_a_norm', 'm_a_w_in', 'm_a_w_out', 'm_b_norm', 'm_b_w_in', 'm_b_f', 'm_b_w_out', 'm_ffn_norm', 'm_ffn_w_gu', 'm_ffn_w_down', 'm_final_norm', 'v_a_norm', 'v_a_w_in', 'v_a_w_out', 'v_b_norm', 'v_b_w_in', 'v_b_f', 'v_b_w_out', 'v_ffn_norm', 'v_ffn_w_gu', 'v_ffn_w_down', 'v_final_norm']
TWIN_OUTPUTS = ['loss', 'grad_x', 'grad_a_norm', 'grad_a_w_in', 'grad_a_w_out', 'grad_b_norm', 'grad_b_w_in', 'grad_b_f', 'grad_b_w_out', 'grad_ffn_norm', 'grad_ffn_w_gu', 'grad_ffn_w_down', 'grad_final_norm', 'delta_a_norm', 'delta_a_w_in', 'delta_a_w_out', 'delta_b_norm', 'delta_b_w_in', 'delta_b_f', 'delta_b_w_out', 'delta_ffn_norm', 'delta_ffn_w_gu', 'delta_ffn_w_down', 'delta_final_norm', 'new_m_a_norm', 'new_m_a_w_in', 'new_m_a_w_out', 'new_m_b_norm', 'new_m_b_w_in', 'new_m_b_f', 'new_m_b_w_out', 'new_m_ffn_norm', 'new_m_ffn_w_gu', 'new_m_ffn_w_down', 'new_m_final_norm', 'new_v_a_norm', 'new_v_a_w_in', 'new_v_a_w_out', 'new_v_b_norm', 'new_v_b_w_in', 'new_v_b_f', 'new_v_b_w_out', 'new_v_ffn_norm', 'new_v_ffn_w_gu', 'new_v_ffn_w_down', 'new_v_final_norm']
TWIN_LEAF_KINDS = {'loss': 'loss', 'grad_x': 'grad_x', 'grad_a_norm': 'grad_w', 'grad_a_w_in': 'grad_w', 'grad_a_w_out': 'grad_w', 'grad_b_norm': 'grad_w', 'grad_b_w_in': 'grad_w', 'grad_b_f': 'grad_w', 'grad_b_w_out': 'grad_w', 'grad_ffn_norm': 'grad_w', 'grad_ffn_w_gu': 'grad_w', 'grad_ffn_w_down': 'grad_w', 'grad_final_norm': 'grad_w', 'delta_a_norm': 'delta_w', 'delta_a_w_in': 'delta_w', 'delta_a_w_out': 'delta_w', 'delta_b_norm': 'delta_w', 'delta_b_w_in': 'delta_w', 'delta_b_f': 'delta_w', 'delta_b_w_out': 'delta_w', 'delta_ffn_norm': 'delta_w', 'delta_ffn_w_gu': 'delta_w', 'delta_ffn_w_down': 'delta_w', 'delta_final_norm': 'delta_w', 'new_m_a_norm': 'new_m', 'new_m_a_w_in': 'new_m', 'new_m_a_w_out': 'new_m', 'new_m_b_norm': 'new_m', 'new_m_b_w_in': 'new_m', 'new_m_b_f': 'new_m', 'new_m_b_w_out': 'new_m', 'new_m_ffn_norm': 'new_m', 'new_m_ffn_w_gu': 'new_m', 'new_m_ffn_w_down': 'new_m', 'new_m_final_norm': 'new_m', 'new_v_a_norm': 'new_v', 'new_v_a_w_in': 'new_v', 'new_v_a_w_out': 'new_v', 'new_v_b_norm': 'new_v', 'new_v_b_w_in': 'new_v', 'new_v_b_f': 'new_v', 'new_v_b_w_out': 'new_v', 'new_v_ffn_norm': 'new_v', 'new_v_ffn_w_gu': 'new_v', 'new_v_ffn_w_down': 'new_v', 'new_v_final_norm': 'new_v'}


def _forward(args):
    return _fwd_reference(*[args[k] for k in FWD_PARAMS])


def _output_shape():
    out = _jax.eval_shape(lambda: _forward(_fwd_setup_inputs(0)))
    return out.shape, out.dtype

N_MICROBATCH = 1
ADAM_LR = 0.001
ADAM_B1 = 0.9
ADAM_B2 = 0.999
ADAM_EPS = 1e-08
ADAM_WD = 0.01
ADAM_STEP = 10
PER_EXAMPLE_BATCH_AXIS = {'x': 0, 'loss_target': 0}
SHARED_INPUTS = []
_WEIGHT_DTYPES = {'a_norm': _jnp.float32, 'a_w_in': _jnp.float32, 'a_w_out': _jnp.float32, 'b_norm': _jnp.float32, 'b_w_in': _jnp.float32, 'b_f': _jnp.float32, 'b_w_out': _jnp.float32, 'ffn_norm': _jnp.float32, 'ffn_w_gu': _jnp.float32, 'ffn_w_down': _jnp.float32, 'final_norm': _jnp.float32}
MOMENT_SCALE = {'a_norm': 6.897195e-02, 'a_w_in': 2.254001e-02, 'a_w_out': 4.316106e-02, 'b_norm': 7.765384e-02, 'b_w_in': 4.512618e-02, 'b_f': 5.401738e-01, 'b_w_out': 5.372569e-02, 'ffn_norm': 1.419234e-01, 'ffn_w_gu': 5.653475e-02, 'ffn_w_down': 9.244710e-02, 'final_norm': 3.199976e+01}


def _to_microbatches(a, axis):
    t = _jnp.moveaxis(a, axis, 0)
    t = t.reshape((N_MICROBATCH, t.shape[0] // N_MICROBATCH) + t.shape[1:])
    return _jnp.moveaxis(t, 1, axis + 1)


def setup_inputs(seed: int = 0) -> dict:
    inp = _fwd_setup_inputs(seed)
    key = _jax.random.fold_in(_jax.random.key(seed), 7919)
    shape, _ = _output_shape()
    out = dict(inp)
    out["loss_target"] = _jax.random.normal(_jax.random.fold_in(key, 0), shape, _jnp.float32)
    for i, name in enumerate(TWIN_WEIGHTS):
        w = inp[name].astype(_jnp.float32)
        if MOMENT_SCALE is None:
            s = _jnp.sqrt(_jnp.mean(_jnp.square(w)) + 1e-30)
        else:
            s = MOMENT_SCALE[name]
        km, kv = _jax.random.split(_jax.random.fold_in(key, i + 1))
        out[name] = w
        out["m_" + name] = s * _jax.random.normal(km, w.shape, _jnp.float32)
        out["v_" + name] = (s * s) * _jax.random.uniform(kv, w.shape, _jnp.float32, 0.5, 1.5)
    if N_MICROBATCH > 1:
        for name, axis in PER_EXAMPLE_BATCH_AXIS.items():
            out[name] = _to_microbatches(out[name], axis)
    return {'x': out['x'], 'a_norm': out['a_norm'], 'a_w_in': out['a_w_in'], 'a_w_out': out['a_w_out'], 'b_norm': out['b_norm'], 'b_w_in': out['b_w_in'], 'b_f': out['b_f'], 'b_w_out': out['b_w_out'], 'ffn_norm': out['ffn_norm'], 'ffn_w_gu': out['ffn_w_gu'], 'ffn_w_down': out['ffn_w_down'], 'final_norm': out['final_norm'], 'loss_target': out['loss_target'], 'm_a_norm': out['m_a_norm'], 'm_a_w_in': out['m_a_w_in'], 'm_a_w_out': out['m_a_w_out'], 'm_b_norm': out['m_b_norm'], 'm_b_w_in': out['m_b_w_in'], 'm_b_f': out['m_b_f'], 'm_b_w_out': out['m_b_w_out'], 'm_ffn_norm': out['m_ffn_norm'], 'm_ffn_w_gu': out['m_ffn_w_gu'], 'm_ffn_w_down': out['m_ffn_w_down'], 'm_final_norm': out['m_final_norm'], 'v_a_norm': out['v_a_norm'], 'v_a_w_in': out['v_a_w_in'], 'v_a_w_out': out['v_a_w_out'], 'v_b_norm': out['v_b_norm'], 'v_b_w_in': out['v_b_w_in'], 'v_b_f': out['v_b_f'], 'v_b_w_out': out['v_b_w_out'], 'v_ffn_norm': out['v_ffn_norm'], 'v_ffn_w_gu': out['v_ffn_w_gu'], 'v_ffn_w_down': out['v_ffn_w_down'], 'v_final_norm': out['v_final_norm']}


def _loss(weights, diff, rest, loss_target):
    with _jax.named_scope("forward"):
        args = {**rest, TWIN_DIFF_INPUT: diff, **{k: w.astype(_WEIGHT_DTYPES[k]) for k, w in weights.items()}}
        y = _forward(args)
    with _jax.named_scope("loss_head"):
        err = _jnp.square(y.astype(_jnp.float32) - loss_target)
        return 0.5 * _jnp.sum(_jnp.mean(err, axis=-1)) if err.ndim else 0.5 * err


def _adamw(w, g, m, v):
    m = ADAM_B1 * m + (1.0 - ADAM_B1) * g
    v = ADAM_B2 * v + (1.0 - ADAM_B2) * _jnp.square(g)
    m_hat = m / (1.0 - ADAM_B1 ** ADAM_STEP)
    v_hat = v / (1.0 - ADAM_B2 ** ADAM_STEP)
    delta = -ADAM_LR * (m_hat / (_jnp.sqrt(v_hat) + ADAM_EPS) + ADAM_WD * w)
    return delta, m, v


def reference(x, a_norm, a_w_in, a_w_out, b_norm, b_w_in, b_f, b_w_out, ffn_norm, ffn_w_gu, ffn_w_down, final_norm, loss_target, m_a_norm, m_a_w_in, m_a_w_out, m_b_norm, m_b_w_in, m_b_f, m_b_w_out, m_ffn_norm, m_ffn_w_gu, m_ffn_w_down, m_final_norm, v_a_norm, v_a_w_in, v_a_w_out, v_b_norm, v_b_w_in, v_b_f, v_b_w_out, v_ffn_norm, v_ffn_w_gu, v_ffn_w_down, v_final_norm):
    given = dict(x=x, a_norm=a_norm, a_w_in=a_w_in, a_w_out=a_w_out, b_norm=b_norm, b_w_in=b_w_in, b_f=b_f, b_w_out=b_w_out, ffn_norm=ffn_norm, ffn_w_gu=ffn_w_gu, ffn_w_down=ffn_w_down, final_norm=final_norm, loss_target=loss_target, m_a_norm=m_a_norm, m_a_w_in=m_a_w_in, m_a_w_out=m_a_w_out, m_b_norm=m_b_norm, m_b_w_in=m_b_w_in, m_b_f=m_b_f, m_b_w_out=m_b_w_out, m_ffn_norm=m_ffn_norm, m_ffn_w_gu=m_ffn_w_gu, m_ffn_w_down=m_ffn_w_down, m_final_norm=m_final_norm, v_a_norm=v_a_norm, v_a_w_in=v_a_w_in, v_a_w_out=v_a_w_out, v_b_norm=v_b_norm, v_b_w_in=v_b_w_in, v_b_f=v_b_f, v_b_w_out=v_b_w_out, v_ffn_norm=v_ffn_norm, v_ffn_w_gu=v_ffn_w_gu, v_ffn_w_down=v_ffn_w_down, v_final_norm=v_final_norm)
    weights = {n: given[n] for n in TWIN_WEIGHTS}
    shared = {n: given[n] for n in SHARED_INPUTS}
    per_example = {n: given[n] for n in ['x']}
    grad_fn = _jax.value_and_grad(_loss, argnums=(0, 1))

    def one_microbatch(ex, loss_target):
        ex = dict(ex)
        diff = ex.pop(TWIN_DIFF_INPUT)
        return grad_fn(weights, diff, {**shared, **ex}, loss_target)

    if N_MICROBATCH == 1:
        loss, (grad_w, grad_x) = one_microbatch(per_example, given["loss_target"])
    else:
        def body(carry, xs):
            loss_sum, grad_sum = carry
            l_k, (gw_k, gx_k) = one_microbatch(xs[0], xs[1])
            with _jax.named_scope("update"):
                return (loss_sum + l_k, _jax.tree.map(_jnp.add, grad_sum, gw_k)), gx_k

        init = (_jnp.zeros((), _jnp.float32), _jax.tree.map(_jnp.zeros_like, weights))
        (loss, grad_w), grad_x = _jax.lax.scan(body, init, (per_example, given["loss_target"]))
    with _jax.named_scope("update"):
        delta_w, new_m, new_v = {}, {}, {}
        for n in TWIN_WEIGHTS:
            delta_w[n], new_m[n], new_v[n] = _adamw(weights[n], grad_w[n], given["m_" + n], given["v_" + n])
    return (loss, grad_x, *[grad_w[n] for n in TWIN_WEIGHTS], *[delta_w[n] for n in TWIN_WEIGHTS],
            *[new_m[n] for n in TWIN_WEIGHTS], *[new_v[n] for n in TWIN_WEIGHTS])
```

```python
import functools

import jax
import jax.numpy as jnp
from jax import lax
from jax.experimental import pallas as pl
from jax.experimental.pallas import tpu as pltpu

F32 = jnp.float32
BF16 = jnp.bfloat16

D_MODEL = 1024
N_HEADS = 16
HEAD_DIM = 64
N_PAIRS = N_HEADS // 2
LANES = 128
DILATED_PATTERNS = ((128, 1), (512, 4), (2048, 16))
BAND_STEPS = 128
ROT_DIM = HEAD_DIM // 4
ROPE_THETA = 500000.0
D_FF = 2816
RMS_EPS = 1e-6
NEG_INF = -1e30
N_DEV = 8
FF_BLK = 2 * D_FF // N_DEV
ADAM_LR, ADAM_B1, ADAM_B2, ADAM_EPS, ADAM_WD, ADAM_STEP = 0.001, 0.9, 0.999, 1e-08, 0.01, 10
VMEM_LIMIT = 52 * 1024 * 1024
MESH = pl.DeviceIdType.MESH

NN = (((1,), (0,)), ((), ()))
NT = (((1,), (1,)), ((), ()))
TN = (((0,), (0,)), ((), ()))


def _params(*sem):
    return pltpu.CompilerParams(dimension_semantics=sem, vmem_limit_bytes=VMEM_LIMIT)


def _dot(a, b, dims):
    return lax.dot_general(a, b, dims, preferred_element_type=F32)


def _mm_call(name, grid, a, a_spec, b, b_spec, dims, out_shapes, out_specs, acc_shape, epilogue=None,
             extras=(), extra_specs=()):
    nk = grid[2]
    n_extra = len(extras)
    n_out = len(out_shapes)

    def body(*refs):
        a_ref, b_ref = refs[0], refs[1]
        ex = refs[2:2 + n_extra]
        outs = refs[2 + n_extra:2 + n_extra + n_out]
        acc = refs[-1]
        j, k = pl.program_id(1), pl.program_id(2)

        @pl.when(k == 0)
        def _():
            acc[...] = jnp.zeros_like(acc)

        acc[...] += _dot(a_ref[...].astype(BF16), b_ref[...].astype(BF16), dims)

        @pl.when(k == nk - 1)
        def _():
            if epilogue is None:
                outs[0][...] = acc[...].astype(outs[0].dtype)
            else:
                epilogue(acc[...], ex, outs, j)

    return pl.pallas_call(
        body, grid=grid, in_specs=[a_spec, b_spec, *extra_specs], out_specs=out_specs, out_shape=out_shapes,
        scratch_shapes=[pltpu.VMEM(acc_shape, F32)], compiler_params=_params("parallel", "parallel", "arbitrary"),
        name=name)(a, b, *extras)


def _matmul(name, a, b, mode, out_dtype, tm, tn, tk, resid=None):
    if mode == "nn":
        (M, K), N = a.shape, b.shape[1]
        a_spec = pl.BlockSpec((tm, tk), lambda i, j, k: (i, k))
        b_spec = pl.BlockSpec((tk, tn), lambda i, j, k: (k, j))
        dims = NN
    elif mode == "nt":
        (M, K), N = a.shape, b.shape[0]
        a_spec = pl.BlockSpec((tm, tk), lambda i, j, k: (i, k))
        b_spec = pl.BlockSpec((tn, tk), lambda i, j, k: (j, k))
        dims = NT
    else:
        (K, M), N = a.shape, b.shape[1]
        a_spec = pl.BlockSpec((tk, tm), lambda i, j, k: (k, i))
        b_spec = pl.BlockSpec((tk, tn), lambda i, j, k: (k, j))
        dims = TN
    assert M % tm == 0 and N % tn == 0 and K % tk == 0, (name, M, N, K, tm, tn, tk)
    o_spec = pl.BlockSpec((tm, tn), lambda i, j, k: (i, j))
    extras, extra_specs, epilogue = (), (), None
    if resid is not None:
        extras, extra_specs = (resid,), (o_spec,)

        def epilogue(acc, ex, outs, j):
            outs[0][...] = (acc + ex[0][...]).astype(outs[0].dtype)

    return _mm_call(name, (M // tm, N // tn, K // tk), a, a_spec, b, b_spec, dims,
                    [jax.ShapeDtypeStruct((M, N), out_dtype)], [o_spec], (tm, tn), epilogue, extras, extra_specs)[0]


def _rms_fwd(name, h, gain, tm=512):
    S, D = h.shape

    def body(h_ref, g_ref, n_ref):
        x = h_ref[...]
        rstd = lax.rsqrt(jnp.mean(x * x, axis=-1, keepdims=True) + RMS_EPS)
        n_ref[...] = (x * rstd * g_ref[...]).astype(BF16)

    return pl.pallas_call(
        body, grid=(S // tm,), in_specs=[pl.BlockSpec((tm, D), lambda i: (i, 0)), pl.BlockSpec((1, D), lambda i: (0, 0))],
        out_specs=pl.BlockSpec((tm, D), lambda i: (i, 0)), out_shape=jax.ShapeDtypeStruct((S, D), BF16),
        compiler_params=_params("parallel"), name=name)(h, gain.reshape(1, D))


def _rms_bwd(name, dn, h, gain, dres, tm=512):
    S, D = h.shape

    def body(dn_ref, h_ref, g_ref, r_ref, dh_ref, dg_ref):
        x = h_ref[...]
        rstd = lax.rsqrt(jnp.mean(x * x, axis=-1, keepdims=True) + RMS_EPS)
        xhat = x * rstd
        d = dn_ref[...]
        dxhat = d * g_ref[...]
        dh_ref[...] = rstd * (dxhat - xhat * jnp.mean(dxhat * xhat, axis=-1, keepdims=True)) + r_ref[...]

        @pl.when(pl.program_id(0) == 0)
        def _():
            dg_ref[...] = jnp.zeros_like(dg_ref)

        dg_ref[...] += jnp.sum(d * xhat, axis=0, keepdims=True)

    row = pl.BlockSpec((tm, D), lambda i: (i, 0))
    vec = pl.BlockSpec((1, D), lambda i: (0, 0))
    return pl.pallas_call(
        body, grid=(S // tm,), in_specs=[row, row, vec, row], out_specs=[row, vec],
        out_shape=[jax.ShapeDtypeStruct((S, D), F32), jax.ShapeDtypeStruct((1, D), F32)],
        compiler_params=_params("arbitrary"), name=name)(dn, h, gain.reshape(1, D), dres)


def _loss_head(name, h, gain, target, tm=512):
    S, D = h.shape

    def body(h_ref, g_ref, t_ref, dh_ref, dg_ref, loss_ref):
        x = h_ref[...]
        rstd = lax.rsqrt(jnp.mean(x * x, axis=-1, keepdims=True) + RMS_EPS)
        xhat = x * rstd
        err = xhat * g_ref[...] - t_ref[...]
        dy = err * (1.0 / D)
        dxhat = dy * g_ref[...]
        dh_ref[...] = rstd * (dxhat - xhat * jnp.mean(dxhat * xhat, axis=-1, keepdims=True))

        @pl.when(pl.program_id(0) == 0)
        def _():
            dg_ref[...] = jnp.zeros_like(dg_ref)
            loss_ref[...] = jnp.zeros_like(loss_ref)

        dg_ref[...] += jnp.sum(dy * xhat, axis=0, keepdims=True)
        part = 0.5 * jnp.sum(jnp.mean(err * err, axis=-1, keepdims=True), axis=0, keepdims=True)
        loss_ref[...] += jnp.broadcast_to(part, loss_ref.shape)

    row = pl.BlockSpec((tm, D), lambda i: (i, 0))
    vec = pl.BlockSpec((1, D), lambda i: (0, 0))
    return pl.pallas_call(
        body, grid=(S // tm,), in_specs=[row, vec, row], out_specs=[row, vec, pl.BlockSpec((1, LANES), lambda i: (0, 0))],
        out_shape=[jax.ShapeDtypeStruct((S, D), F32), jax.ShapeDtypeStruct((1, D), F32),
                   jax.ShapeDtypeStruct((1, LANES), F32)],
        compiler_params=_params("arbitrary"), name=name)(h, gain.reshape(1, D), target)


def _rope_tables(S):
    half = ROT_DIM // 2
    inv_freq = ROPE_THETA ** (-jnp.arange(half, dtype=F32) * 2.0 / ROT_DIM)
    ang = jnp.arange(S, dtype=F32)[:, None] * inv_freq[None, :]
    cos, sin = jnp.cos(ang), jnp.sin(ang)
    one = jnp.ones((S, HEAD_DIM - ROT_DIM), F32)
    zero = jnp.zeros((S, HEAD_DIM - ROT_DIM), F32)
    zh = jnp.zeros((S, half), F32)
    c = jnp.concatenate([cos, cos, one], axis=1)
    sa = jnp.concatenate([-sin, zh, zero], axis=1)
    sb = jnp.concatenate([zh, sin, zero], axis=1)
    return tuple(jnp.concatenate([t, t], axis=1) for t in (c, sa, sb))


def _rotate(x, c, sa, sb, sign):
    return x * c + sign * (pltpu.roll(x, LANES - ROT_DIM // 2, 1) * sa + pltpu.roll(x, ROT_DIM // 2, 1) * sb)


def _a_proj(name, n, w, tabs, tm=512):
    S, D = n.shape
    N = w.shape[1]
    tn = D_MODEL

    def epilogue(acc, ex, outs, j):
        @pl.when(j % 3 == 2)
        def _():
            outs[0][...] = acc.astype(BF16)

        @pl.when(j % 3 != 2)
        def _():
            c, sa, sb = ex[0][...], ex[1][...], ex[2][...]
            for p in range(N_PAIRS):
                cols = slice(p * LANES, (p + 1) * LANES)
                outs[0][:, cols] = _rotate(acc[:, cols], c, sa, sb, 1.0).astype(BF16)

    tab = pl.BlockSpec((tm, LANES), lambda i, j, k: (i, 0))
    return _mm_call(name, (S // tm, N // tn, 1), n, pl.BlockSpec((tm, D), lambda i, j, k: (i, 0)),
                    w, pl.BlockSpec((D, tn), lambda i, j, k: (0, j)), NN,
                    [jax.ShapeDtypeStruct((S, N), BF16)], [pl.BlockSpec((tm, tn), lambda i, j, k: (i, j))], (tm, tn),
                    epilogue, tabs, (tab, tab, tab))[0]


def _attn_geometry(L, t, band):
    nq = L // t
    nkv = 2 if band else nq
    if band:
        kv_of = lambda qi, j: jnp.maximum(qi - 1 + j, 0)
        q_of = lambda kb, j: jnp.minimum(kb + j, nq - 1)
    else:
        kv_of = lambda qi, j: jnp.minimum(j, qi)
        q_of = lambda kb, j: jnp.maximum(j, kb)
    return nq, nkv, kv_of, q_of


def _valid_mask(qb, kb, t, band):
    diff = (qb * t + lax.broadcasted_iota(jnp.int32, (t, t), 0)) - (kb * t + lax.broadcasted_iota(jnp.int32, (t, t), 1))
    valid = diff >= 0
    if band:
        valid = valid & (diff <= BAND_STEPS)
    return valid


def _lo_lanes():
    return lax.broadcasted_iota(jnp.int32, (1, LANES), 1) < HEAD_DIM


def _attn_fwd(name, q, k, v, qcol, kcol, vcol, R, L, t, band, o_dtype, bias=None):
    nq, nkv, kv_of, _ = _attn_geometry(L, t, band)
    scale = HEAD_DIM ** -0.5
    W = D_MODEL
    has_bias = bias is not None

    def body(*refs):
        q_ref, k_ref, v_ref = refs[:3]
        if has_bias:
            cb_ref, ct_ref = refs[3:5]
        o_ref, lse_ref, m_scr, l_scr, acc_scr = refs[-5:]
        qi, j = pl.program_id(1), pl.program_id(2)
        kb = qi - 1 + j if band else j
        active = kb >= 0 if band else j <= qi
        lo = _lo_lanes()

        @pl.when(j == 0)
        def _():
            m_scr[...] = jnp.full_like(m_scr, NEG_INF)
            l_scr[...] = jnp.zeros_like(l_scr)
            acc_scr[...] = jnp.zeros_like(acc_scr)

        @pl.when(active)
        def _():
            valid = _valid_mask(qi, kb, t, band)

            def pair(p, carry):
                c0 = pl.multiple_of(p * LANES, LANES)
                q2, k2, v2 = q_ref[:, pl.ds(c0, LANES)], k_ref[:, pl.ds(c0, LANES)], v_ref[:, pl.ds(c0, LANES)]
                if has_bias:
                    cb2 = cb_ref[:, pl.ds(c0, LANES)]
                    ct2 = ct_ref[p]
                pv, alphas = [], []
                for hh in range(2):
                    hm = lo if hh == 0 else jnp.logical_not(lo)
                    s = _dot(jnp.where(hm, q2, 0), k2, NT) * scale
                    if has_bias:
                        s = s + cb2[:, hh * HEAD_DIM:hh * HEAD_DIM + 1] - ct2[hh:hh + 1, :]
                    s = jnp.where(valid, s, NEG_INF)
                    h = 2 * p + hh
                    m_prev = m_scr[h]
                    m_new = jnp.maximum(m_prev, jnp.max(s, axis=1, keepdims=True))
                    pe = jnp.where(valid, jnp.exp(s - m_new), 0.0)
                    alpha = jnp.exp(m_prev - m_new)
                    l_scr[h] = alpha * l_scr[h] + jnp.sum(pe, axis=1, keepdims=True)
                    m_scr[h] = m_new
                    pv.append(_dot(pe.astype(BF16), jnp.where(hm, v2, 0), NN))
                    alphas.append(alpha)
                alpha2 = jnp.where(lo, alphas[0], alphas[1])
                acc_scr[:, pl.ds(c0, LANES)] = acc_scr[:, pl.ds(c0, LANES)] * alpha2 + pv[0] + pv[1]
                return carry

            lax.fori_loop(0, N_PAIRS, pair, 0)

        @pl.when(j == nkv - 1)
        def _():
            for p in range(N_PAIRS):
                cols = slice(p * LANES, (p + 1) * LANES)
                l2 = jnp.where(lo, l_scr[2 * p], l_scr[2 * p + 1])
                m2 = jnp.where(lo, m_scr[2 * p], m_scr[2 * p + 1])
                o_ref[:, cols] = (acc_scr[:, cols] / l2).astype(o_dtype)
                lse_ref[:, cols] = m2 + jnp.log(l2)

    in_specs = [pl.BlockSpec((t, W), lambda r, qi, j: (qi, qcol(r))),
                pl.BlockSpec((t, W), lambda r, qi, j: (kv_of(qi, j), kcol(r))),
                pl.BlockSpec((t, W), lambda r, qi, j: (kv_of(qi, j), vcol(r)))]
    args = [q, k, v]
    if has_bias:
        in_specs += [pl.BlockSpec((t, W), lambda r, qi, j: (qi, 0)),
                     pl.BlockSpec((N_PAIRS, 2, t), lambda r, qi, j: (0, 0, kv_of(qi, j)))]
        args += list(bias)
    o_spec = pl.BlockSpec((t, W), lambda r, qi, j: (qi, r))
    return pl.pallas_call(
        body, grid=(R, nq, nkv), in_specs=in_specs, out_specs=[o_spec, o_spec],
        out_shape=[jax.ShapeDtypeStruct((L, R * W), o_dtype), jax.ShapeDtypeStruct((L, R * W), F32)],
        scratch_shapes=[pltpu.VMEM((N_HEADS, t, 1), F32), pltpu.VMEM((N_HEADS, t, 1), F32), pltpu.VMEM((t, W), F32)],
        compiler_params=_params("parallel", "parallel", "arbitrary"), name=name)(*args)


def _attn_probs(q2, k2, v2, do2, lse2, dd, hm, hh, valid, scale, bias2):
    s = _dot(jnp.where(hm, q2, 0), k2, NT) * scale
    if bias2 is not None:
        cb2, ct2 = bias2
        s = s + cb2[:, hh * HEAD_DIM:hh * HEAD_DIM + 1] - ct2[hh:hh + 1, :]
    p = jnp.where(valid, jnp.exp(s - lse2[:, hh * HEAD_DIM:hh * HEAD_DIM + 1]), 0.0)
    dp = _dot(jnp.where(hm, do2, 0), v2, NT)
    d_h = jnp.sum(jnp.where(hm, dd, 0.0), axis=1, keepdims=True)
    return p, p * (dp - d_h)


def _attn_dq(name, q, k, v, do, o, lse, qcol, kcol, vcol, R, L, t, band, tabs=None, bias=None):
    nq, nkv, kv_of, _ = _attn_geometry(L, t, band)
    scale = HEAD_DIM ** -0.5
    W = D_MODEL
    has_bias, has_rot = bias is not None, tabs is not None

    def body(*refs):
        q_ref, k_ref, v_ref, do_ref, o_ref, lse_ref = refs[:6]
        pos = 6
        if has_bias:
            cb_ref, ct_ref = refs[pos:pos + 2]
            pos += 2
        if has_rot:
            c_ref, sa_ref, sb_ref = refs[pos:pos + 3]
            pos += 3
        dq_ref = refs[pos]
        pos += 1
        if has_bias:
            dcr_ref = refs[pos]
            dcr_scr = refs[-1]
            dq_scr = refs[-2]
        else:
            dq_scr = refs[-1]
        qi, j = pl.program_id(1), pl.program_id(2)
        kb = qi - 1 + j if band else j
        active = kb >= 0 if band else j <= qi
        lo = _lo_lanes()

        @pl.when(j == 0)
        def _():
            dq_scr[...] = jnp.zeros_like(dq_scr)
            if has_bias:
                dcr_scr[...] = jnp.zeros_like(dcr_scr)

        @pl.when(active)
        def _():
            valid = _valid_mask(qi, kb, t, band)

            def pair(p, carry):
                c0 = pl.multiple_of(p * LANES, LANES)
                cs = pl.ds(c0, LANES)
                q2, k2, v2, do2 = q_ref[:, cs], k_ref[:, cs], v_ref[:, cs], do_ref[:, cs]
                dd = do2.astype(F32) * o_ref[:, cs].astype(F32)
                lse2 = lse_ref[:, cs]
                bias2 = (cb_ref[:, cs], ct_ref[p]) if has_bias else None
                dq2 = jnp.zeros((t, LANES), F32)
                for hh in range(2):
                    hm = lo if hh == 0 else jnp.logical_not(lo)
                    _, ds = _attn_probs(q2, k2, v2, do2, lse2, dd, hm, hh, valid, scale, bias2)
                    if has_bias:
                        dcr_scr[2 * p + hh] += jnp.sum(ds, axis=1, keepdims=True)
                    dq2 = dq2 + _dot(ds.astype(BF16), jnp.where(hm, k2, 0), NN)
                dq_scr[:, cs] += dq2 * scale
                return carry

            lax.fori_loop(0, N_PAIRS, pair, 0)

        @pl.when(j == nkv - 1)
        def _():
            for p in range(N_PAIRS):
                cols = slice(p * LANES, (p + 1) * LANES)
                x = dq_scr[:, cols]
                if has_rot:
                    x = _rotate(x, c_ref[...], sa_ref[...], sb_ref[...], -1.0)
                dq_ref[:, cols] = x.astype(BF16)
                if has_bias:
                    dcr_ref[:, cols] = jnp.broadcast_to(jnp.where(lo, dcr_scr[2 * p], dcr_scr[2 * p + 1]), (t, LANES))

    qrow = lambda col: pl.BlockSpec((t, W), lambda r, qi, j: (qi, col(r)))
    krow = lambda col: pl.BlockSpec((t, W), lambda r, qi, j: (kv_of(qi, j), col(r)))
    own = pl.BlockSpec((t, W), lambda r, qi, j: (qi, r))
    in_specs = [qrow(qcol), krow(kcol), krow(vcol), own, own, own]
    args = [q, k, v, do, o, lse]
    if has_bias:
        in_specs += [pl.BlockSpec((t, W), lambda r, qi, j: (qi, 0)),
                     pl.BlockSpec((N_PAIRS, 2, t), lambda r, qi, j: (0, 0, kv_of(qi, j)))]
        args += list(bias)
    if has_rot:
        in_specs += [pl.BlockSpec((t, LANES), lambda r, qi, j: (qi, r))] * 3
        args += list(tabs)
    out_specs, out_shape = [own], [jax.ShapeDtypeStruct((L, R * W), BF16)]
    scratch = [pltpu.VMEM((t, W), F32)]
    if has_bias:
        out_specs.append(own)
        out_shape.append(jax.ShapeDtypeStruct((L, R * W), F32))
        scratch.append(pltpu.VMEM((N_HEADS, t, 1), F32))
    return pl.pallas_call(
        body, grid=(R, nq, nkv), in_specs=in_specs, out_specs=out_specs, out_shape=out_shape, scratch_shapes=scratch,
        compiler_params=_params("parallel", "parallel", "arbitrary"), name=name)(*args)


def _attn_dkv(name, q, k, v, do, o, lse, qcol, kcol, vcol, R, L, t, band, tabs=None, bias=None):
    nq, nkv, _, q_of = _attn_geometry(L, t, band)
    scale = HEAD_DIM ** -0.5
    W = D_MODEL
    has_bias, has_rot = bias is not None, tabs is not None

    def body(*refs):
        q_ref, k_ref, v_ref, do_ref, o_ref, lse_ref = refs[:6]
        pos = 6
        if has_bias:
            cb_ref, ct_ref = refs[pos:pos + 2]
            pos += 2
        if has_rot:
            c_ref, sa_ref, sb_ref = refs[pos:pos + 3]
            pos += 3
        dk_ref, dv_ref = refs[pos:pos + 2]
        pos += 2
        if has_bias:
            dcc_ref = refs[pos]
        dk_scr, dv_scr = refs[-2:]
        kb, j = pl.program_id(1), pl.program_id(2)
        qb = kb + j if band else j
        active = qb <= nq - 1 if band else j >= kb
        lo = _lo_lanes()

        @pl.when(j == 0)
        def _():
            dk_scr[...] = jnp.zeros_like(dk_scr)
            dv_scr[...] = jnp.zeros_like(dv_scr)
            if has_bias:
                dcc_ref[...] = jnp.zeros_like(dcc_ref)

        @pl.when(active)
        def _():
            valid = _valid_mask(qb, kb, t, band)

            def pair(p, carry):
                c0 = pl.multiple_of(p * LANES, LANES)
                cs = pl.ds(c0, LANES)
                q2, k2, v2, do2 = q_ref[:, cs], k_ref[:, cs], v_ref[:, cs], do_ref[:, cs]
                dd = do2.astype(F32) * o_ref[:, cs].astype(F32)
                lse2 = lse_ref[:, cs]
                bias2 = (cb_ref[:, cs], ct_ref[p]) if has_bias else None
                dk2 = jnp.zeros((t, LANES), F32)
                dv2 = jnp.zeros((t, LANES), F32)
                for hh in range(2):
                    hm = lo if hh == 0 else jnp.logical_not(lo)
                    pr, ds = _attn_probs(q2, k2, v2, do2, lse2, dd, hm, hh, valid, scale, bias2)
                    if has_bias:
                        dcc_ref[p, hh:hh + 1, :] += jnp.sum(ds, axis=0, keepdims=True)
                    dv2 = dv2 + _dot(pr.astype(BF16), jnp.where(hm, do2, 0), TN)
                    dk2 = dk2 + _dot(ds.astype(BF16), jnp.where(hm, q2, 0), TN)
                dk_scr[:, cs] += dk2 * scale
                dv_scr[:, cs] += dv2
                return carry

            lax.fori_loop(0, N_PAIRS, pair, 0)

        @pl.when(j == nkv - 1)
        def _():
            dv_ref[...] = dv_scr[...].astype(BF16)
            for p in range(N_PAIRS):
                cols = slice(p * LANES, (p + 1) * LANES)
                x = dk_scr[:, cols]
                if has_rot:
                    x = _rotate(x, c_ref[...], sa_ref[...], sb_ref[...], -1.0)
                dk_ref[:, cols] = x.astype(BF16)

    qrow = lambda col: pl.BlockSpec((t, W), lambda r, kb, j: (q_of(kb, j), col(r)))
    krow = lambda col: pl.BlockSpec((t, W), lambda r, kb, j: (kb, col(r)))
    qown = pl.BlockSpec((t, W), lambda r, kb, j: (q_of(kb, j), r))
    kown = pl.BlockSpec((t, W), lambda r, kb, j: (kb, r))
    in_specs = [qrow(qcol), krow(kcol), krow(vcol), qown, qown, qown]
    args = [q, k, v, do, o, lse]
    if has_bias:
        in_specs += [pl.BlockSpec((t, W), lambda r, kb, j: (q_of(kb, j), 0)),
                     pl.BlockSpec((N_PAIRS, 2, t), lambda r, kb, j: (0, 0, kb))]
        args += list(bias)
    if has_rot:
        in_specs += [pl.BlockSpec((t, LANES), lambda r, kb, j: (kb, r))] * 3
        args += list(tabs)
    out_specs = [kown, kown]
    out_shape = [jax.ShapeDtypeStruct((L, R * W), BF16), jax.ShapeDtypeStruct((L, R * W), BF16)]
    if has_bias:
        out_specs.append(pl.BlockSpec((N_PAIRS, 2, t), lambda r, kb, j: (0, 0, kb)))
        out_shape.append(jax.ShapeDtypeStruct((N_PAIRS, 2, L), F32))
    return pl.pallas_call(
        body, grid=(R, nq, nkv), in_specs=in_specs, out_specs=out_specs, out_shape=out_shape,
        scratch_shapes=[pltpu.VMEM((t, W), F32), pltpu.VMEM((t, W), F32)],
        compiler_params=_params("parallel", "parallel", "arbitrary"), name=name)(*args)


def _combine(name, os_, lses, tm=256):
    S, W = os_[0].shape
    G = len(os_)

    def body(*refs):
        o_refs, l_refs = refs[:G], refs[G:2 * G]
        o_ref, lse_ref = refs[2 * G:]
        ls = [r[...] for r in l_refs]
        m = functools.reduce(jnp.maximum, ls)
        ws = [jnp.exp(l - m) for l in ls]
        den = functools.reduce(jnp.add, ws)
        num = functools.reduce(jnp.add, [w * r[...] for w, r in zip(ws, o_refs)])
        o_ref[...] = (num / den).astype(BF16)
        lse_ref[...] = m + jnp.log(den)

    row = pl.BlockSpec((tm, W), lambda i: (i, 0))
    return pl.pallas_call(
        body, grid=(S // tm,), in_specs=[row] * (2 * G), out_specs=[row, row],
        out_shape=[jax.ShapeDtypeStruct((S, W), BF16), jax.ShapeDtypeStruct((S, W), F32)],
        compiler_params=_params("parallel"), name=name)(*os_, *lses)


def _tri_matmul(tri, x):
    hi = x.astype(BF16)
    r1 = x - hi.astype(F32)
    mid = r1.astype(BF16)
    lo = (r1 - mid.astype(F32)).astype(BF16)
    return _dot(tri, hi, NN) + _dot(tri, mid, NN) + _dot(tri, lo, NN)


def _gate_fwd(name, z, bf, tb=512):
    S = z.shape[0]

    def body(z_ref, b_ref, c_ref, carry):
        @pl.when(pl.program_id(0) == 0)
        def _():
            carry[...] = jnp.zeros_like(carry)

        lf = jax.nn.log_sigmoid(z_ref[...] + b_ref[...])
        ri = lax.broadcasted_iota(jnp.int32, (tb, tb), 0)
        ci = lax.broadcasted_iota(jnp.int32, (tb, tb), 1)
        tri = (ci <= ri).astype(BF16)
        c = _tri_matmul(tri, lf) + carry[...]
        c_ref[...] = c
        carry[...] = c[tb - 1:tb, :]

    row = pl.BlockSpec((tb, LANES), lambda i: (i, 0))
    return pl.pallas_call(
        body, grid=(S // tb,), in_specs=[row, pl.BlockSpec((1, LANES), lambda i: (0, 0))], out_specs=row,
        out_shape=jax.ShapeDtypeStruct((S, LANES), F32), scratch_shapes=[pltpu.VMEM((1, LANES), F32)],
        compiler_params=_params("arbitrary"), name=name)(z, bf)


def _gate_bwd(name, dc, z, bf, tb=512):
    S = z.shape[0]
    nb = S // tb

    def body(dc_ref, z_ref, b_ref, dz_ref, db_ref, carry):
        @pl.when(pl.program_id(0) == 0)
        def _():
            carry[...] = jnp.zeros_like(carry)
            db_ref[...] = jnp.zeros_like(db_ref)

        ri = lax.broadcasted_iota(jnp.int32, (tb, tb), 0)
        ci = lax.broadcasted_iota(jnp.int32, (tb, tb), 1)
        tri = (ci >= ri).astype(BF16)
        dlf = _tri_matmul(tri, dc_ref[...]) + carry[...]
        carry[...] = dlf[0:1, :]
        dz = dlf * jax.nn.sigmoid(-(z_ref[...] + b_ref[...]))
        dz_ref[...] = dz
        db_ref[...] += jnp.sum(dz, axis=0, keepdims=True)

    row = pl.BlockSpec((tb, LANES), lambda i: (nb - 1 - i, 0))
    vec = pl.BlockSpec((1, LANES), lambda i: (0, 0))
    return pl.pallas_call(
        body, grid=(nb,), in_specs=[row, row, vec], out_specs=[row, vec],
        out_shape=[jax.ShapeDtypeStruct((S, LANES), F32), jax.ShapeDtypeStruct((1, LANES), F32)],
        scratch_shapes=[pltpu.VMEM((1, LANES), F32)], compiler_params=_params("arbitrary"), name=name)(dc, z, bf)


def _ffn_gu(name, n, wgu, tm=512):
    S, D = n.shape
    nb = N_DEV // 2

    def body(n_ref, wg_ref, wu_ref, gu_ref, act_ref):
        x = n_ref[...]
        g = _dot(x, wg_ref[...], NN)
        u = _dot(x, wu_ref[...], NN)
        gu_ref[0] = g.astype(BF16)
        gu_ref[1] = u.astype(BF16)
        act_ref[...] = (g * jax.nn.sigmoid(g) * u).astype(BF16)

    return pl.pallas_call(
        body, grid=(S // tm, nb),
        in_specs=[pl.BlockSpec((tm, D), lambda i, j: (i, 0)), pl.BlockSpec((None, D, FF_BLK), lambda i, j: (j, 0, 0)),
                  pl.BlockSpec((None, D, FF_BLK), lambda i, j: (j + nb, 0, 0))],
        out_specs=[pl.BlockSpec((2, None, tm, FF_BLK), lambda i, j: (0, j, i, 0)),
                   pl.BlockSpec((None, tm, FF_BLK), lambda i, j: (j, i, 0))],
        out_shape=[jax.ShapeDtypeStruct((2, nb, S, FF_BLK), BF16), jax.ShapeDtypeStruct((nb, S, FF_BLK), BF16)],
        compiler_params=_params("parallel", "parallel"), name=name)(n, wgu, wgu)


def _ffn_down(name, act, wd, resid, tm=512):
    nb, S, _ = act.shape
    D = wd.shape[1]

    def epilogue(acc, ex, outs, j):
        outs[0][...] = acc + ex[0][...]

    o_spec = pl.BlockSpec((tm, D), lambda i, j, k: (i, 0))
    return _mm_call(name, (S // tm, 1, nb), act, pl.BlockSpec((None, tm, FF_BLK), lambda i, j, k: (k, i, 0)),
                    wd, pl.BlockSpec((FF_BLK, D), lambda i, j, k: (k, 0)), NN,
                    [jax.ShapeDtypeStruct((S, D), F32)], [o_spec], (tm, D), epilogue, (resid,), (o_spec,))[0]


def _ffn_dact(name, dh, wd, gu, tm=512):
    S, D = dh.shape
    nb = N_DEV // 2

    def epilogue(acc, ex, outs, j):
        g = ex[0][0].astype(F32)
        u = ex[0][1].astype(F32)
        sig = jax.nn.sigmoid(g)
        outs[0][0] = (acc * u * (sig * (1.0 + g * (1.0 - sig)))).astype(BF16)
        outs[0][1] = (acc * (g * sig)).astype(BF16)

    gu_spec = pl.BlockSpec((2, None, tm, FF_BLK), lambda i, j, k: (0, j, i, 0))
    return _mm_call(name, (S // tm, nb, 1), dh, pl.BlockSpec((tm, D), lambda i, j, k: (i, 0)),
                    wd, pl.BlockSpec((FF_BLK, D), lambda i, j, k: (j, 0)), NT,
                    [jax.ShapeDtypeStruct((2, nb, S, FF_BLK), BF16)], [gu_spec], (tm, FF_BLK), epilogue, (gu,), (gu_spec,))[0]


def _ffn_dwgu(name, n, dgu, tm=512, tk=1024):
    S, D = n.shape
    dgu8 = dgu.reshape(N_DEV, S, FF_BLK)
    return _mm_call(name, (N_DEV, D // tm, S // tk), n, pl.BlockSpec((tk, tm), lambda d, i, k: (k, i)),
                    dgu8, pl.BlockSpec((None, tk, FF_BLK), lambda d, i, k: (d, k, 0)), TN,
                    [jax.ShapeDtypeStruct((N_DEV, D, FF_BLK), F32)],
                    [pl.BlockSpec((None, tm, FF_BLK), lambda d, i, k: (d, i, 0))], (tm, FF_BLK))[0]


def _ffn_dwd(name, act, dh, tk=1024):
    nb, S, _ = act.shape
    D = dh.shape[1]
    out = _mm_call(name, (nb, 1, S // tk), act, pl.BlockSpec((None, tk, FF_BLK), lambda b, j, k: (b, k, 0)),
                   dh, pl.BlockSpec((tk, D), lambda b, j, k: (k, 0)), TN,
                   [jax.ShapeDtypeStruct((nb, FF_BLK, D), F32)],
                   [pl.BlockSpec((None, FF_BLK, D), lambda b, j, k: (b, 0, 0))], (FF_BLK, D))[0]
    return out.reshape(nb * FF_BLK, D)


def _ffn_dn(name, dgu, wgu, tm=512):
    S = dgu.shape[2]
    D = wgu.shape[1]
    dgu8 = dgu.reshape(N_DEV, S, FF_BLK)
    return _mm_call(name, (S // tm, 1, N_DEV), dgu8, pl.BlockSpec((None, tm, FF_BLK), lambda i, j, k: (k, i, 0)),
                    wgu, pl.BlockSpec((None, D, FF_BLK), lambda i, j, k: (k, 0, 0)), NT,
                    [jax.ShapeDtypeStruct((S, D), F32)], [pl.BlockSpec((tm, D), lambda i, j, k: (i, 0))], (tm, D))[0]


def _adamw(name, parts, w, m, v, tr):
    rows, cols = w.shape
    n_parts = len(parts)
    c1 = 1.0 - ADAM_B1 ** ADAM_STEP
    c2 = 1.0 - ADAM_B2 ** ADAM_STEP

    def body(*refs):
        p_refs = refs[:n_parts]
        w_ref, m_ref, v_ref, g_ref, d_ref, nm_ref, nv_ref = refs[n_parts:]
        g = p_refs[0][...].astype(F32)
        for r in p_refs[1:]:
            g = g + r[...].astype(F32)
        mm = ADAM_B1 * m_ref[...] + (1.0 - ADAM_B1) * g
        vv = ADAM_B2 * v_ref[...] + (1.0 - ADAM_B2) * (g * g)
        g_ref[...] = g
        nm_ref[...] = mm
        nv_ref[...] = vv
        d_ref[...] = -ADAM_LR * ((mm / c1) / (jnp.sqrt(vv / c2) + ADAM_EPS) + ADAM_WD * w_ref[...])

    blk = pl.BlockSpec((tr, cols), lambda i: (i, 0))
    out = jax.ShapeDtypeStruct((rows, cols), F32)
    return pl.pallas_call(
        body, grid=(rows // tr,), in_specs=[blk] * (n_parts + 3), out_specs=[blk] * 4, out_shape=[out] * 4,
        compiler_params=_params("parallel"), name=name)(*parts, w, m, v)


def _position():
    return lax.axis_index("x"), lax.axis_index("y"), lax.axis_index("c")


def _all_gather(name, block):
    shape, dtype = block.shape, block.dtype

    def body(x_ref, out_ref, send_sems, recv_sems, local_sem):
        x, y, c = _position()
        me, sibling = (x, y, c), (x, y, 1 - c)
        chips = [(1 - x, y), (x, 1 - y), (1 - x, 1 - y)]

        def slot(px, py, pc):
            return out_ref.at[4 * px + 2 * py + pc]

        def copy(k, blk, to, src=None):
            return pltpu.make_async_remote_copy(
                src_ref=slot(*blk) if src is None else src, dst_ref=slot(*blk), send_sem=send_sems.at[k],
                recv_sem=recv_sems.at[k], device_id=to, device_id_type=MESH)

        mine = pltpu.make_async_copy(x_ref, slot(*me), local_sem)
        mine.start()
        first = [copy(0, me, sibling, src=x_ref)]
        first += [copy(1 + j, me, (*chip, c), src=x_ref) for j, chip in enumerate(chips)]
        for cp in first:
            cp.start()
        passed = [copy(4 + j, (*chip, c), sibling) for j, chip in enumerate(chips)]
        for j, chip in enumerate(chips):
            copy(1 + j, (*chip, c), me).wait_recv()
            passed[j].start()
        copy(0, sibling, me).wait_recv()
        for j, chip in enumerate(chips):
            copy(4 + j, (*chip, 1 - c), me).wait_recv()
        for cp in first + passed:
            cp.wait_send()
        mine.wait()

    hbm = pl.BlockSpec(memory_space=pltpu.HBM)
    return pl.pallas_call(
        body, out_shape=jax.ShapeDtypeStruct((N_DEV,) + shape, dtype), in_specs=[hbm], out_specs=hbm,
        scratch_shapes=[pltpu.SemaphoreType.DMA((7,)), pltpu.SemaphoreType.DMA((7,)), pltpu.SemaphoreType.DMA],
        name=name)(block)


def _swap_halves(name, packed):
    _, _, rows, cols = packed.shape

    def body(p_ref, got_ref, send_sem, recv_sem):
        x, y, c = _position()
        cp = pltpu.make_async_remote_copy(
            src_ref=p_ref.at[:, 1 - c], dst_ref=got_ref, send_sem=send_sem, recv_sem=recv_sem,
            device_id=(x, y, 1 - c), device_id_type=MESH)
        cp.start()
        cp.wait()

    hbm = pl.BlockSpec(memory_space=pltpu.HBM)
    return pl.pallas_call(
        body, out_shape=jax.ShapeDtypeStruct((4, rows, cols), packed.dtype), in_specs=[hbm], out_specs=hbm,
        scratch_shapes=[pltpu.SemaphoreType.DMA, pltpu.SemaphoreType.DMA], name=name)(packed)


def _pair_sum(name, packed, got, core, tr=560):
    _, _, rows, cols = packed.shape

    def body(c_ref, a_ref, b_ref, o_ref):
        o_ref[...] = (a_ref[...].astype(F32) + b_ref[...].astype(F32)).astype(o_ref.dtype)

    grid_spec = pltpu.PrefetchScalarGridSpec(
        num_scalar_prefetch=1, grid=(4, rows // tr),
        in_specs=[pl.BlockSpec((None, None, tr, cols), lambda q, i, c: (q, c[0], i, 0)),
                  pl.BlockSpec((None, tr, cols), lambda q, i, c: (q, i, 0))],
        out_specs=pl.BlockSpec((None, tr, cols), lambda q, i, c: (q, i, 0)))
    return pl.pallas_call(
        body, grid_spec=grid_spec, out_shape=jax.ShapeDtypeStruct((4, rows, cols), packed.dtype),
        compiler_params=_params("parallel", "parallel"), name=name)(core, packed, got)


def _exchange_chips(name, sums):
    _, rows, cols = sums.shape

    def body(s_ref, got_ref, send_sems, recv_sems):
        x, y, c = _position()
        chips = [(1 - x, y), (x, 1 - y), (1 - x, 1 - y)]
        cps = [pltpu.make_async_remote_copy(
            src_ref=s_ref.at[2 * px + py], dst_ref=got_ref.at[k], send_sem=send_sems.at[k], recv_sem=recv_sems.at[k],
            device_id=(px, py, c), device_id_type=MESH) for k, (px, py) in enumerate(chips)]
        for cp in cps:
            cp.start()
        for cp in cps:
            cp.wait()

    hbm = pl.BlockSpec(memory_space=pltpu.HBM)
    return pl.pallas_call(
        body, out_shape=jax.ShapeDtypeStruct((3, rows, cols), sums.dtype), in_specs=[hbm], out_specs=hbm,
        scratch_shapes=[pltpu.SemaphoreType.DMA((3,)), pltpu.SemaphoreType.DMA((3,))], name=name)(sums)


PACK = (("a_w_in", 1152, 1152), ("a_w_out", 128, 128), ("b_w_in", 386, 400), ("b_w_out", 128, 128),
        ("gu0", 704, 704), ("gu1", 704, 704), ("dn0", 352, 352), ("dn1", 352, 352))
PACK_ROWS = sum(p[2] for p in PACK)
PACK_OFF = {p[0]: sum(q[2] for q in PACK[:i]) for i, p in enumerate(PACK)}
PACK_LEN = {p[0]: p[1] for p in PACK}
ADAM_TILE = {"a_w_in": 128, "a_w_out": 128, "b_w_in": 386, "b_w_out": 128, "gu0": 176, "gu1": 176, "dn0": 176, "dn1": 176}
B_IN = 3 * D_MODEL + N_HEADS
B_IN_PAD = 3 * D_MODEL + LANES


def _pack_rows(pieces, dtype):
    out = []
    for name, rows, padded in PACK:
        a = pieces[name].astype(dtype)
        if padded != rows:
            a = jnp.pad(a, [(0, 0)] * (a.ndim - 2) + [(0, padded - rows), (0, 0)])
        out.append(a)
    return jnp.concatenate(out, axis=-2)


def _unpack(packed, name):
    off = PACK_OFF[name]
    return packed[..., off:off + PACK_LEN[name], :]


def _gathered_weights(wg):
    blocks = lambda name, shape: _unpack(wg, name).reshape((N_DEV,) + shape)
    a_w_in = blocks("a_w_in", (D_MODEL, 1152)).transpose(1, 0, 2).reshape(D_MODEL, 9 * D_MODEL)
    b_w_in = blocks("b_w_in", (D_MODEL, 386)).transpose(1, 0, 2).reshape(D_MODEL, B_IN)
    b_gate = jnp.pad(b_w_in[:, 3 * D_MODEL:], ((0, 0), (0, LANES - N_HEADS)))
    return dict(
        a_w_in=a_w_in, a_w_out=_unpack(wg, "a_w_out").reshape(D_MODEL, D_MODEL),
        b_w_qkv=b_w_in[:, :3 * D_MODEL], b_w_gate=b_gate, b_w_cat=jnp.concatenate([b_w_in[:, :3 * D_MODEL], b_gate], axis=1),
        b_w_out=_unpack(wg, "b_w_out").reshape(D_MODEL, D_MODEL),
        gu=[blocks("gu%d" % l, (D_MODEL, FF_BLK)) for l in range(2)],
        dn=[_unpack(wg, "dn%d" % l).reshape(D_FF, D_MODEL) for l in range(2)])


def _local_step(h0, target, W, norms):
    S = h0.shape[0]
    tabs = _rope_tables(S)

    n0 = _rms_fwd("rms_a", h0, norms["a_norm"])
    proj_a = _a_proj("proj_a", n0, W["a_w_in"], tabs)
    groups = []
    for g, (window, dil) in enumerate(DILATED_PATTERNS):
        L = S // dil
        view = proj_a.reshape(L, dil * 9 * D_MODEL)
        cols = [(lambda r, g=g, tq=tq: r * 9 + g * 3 + tq) for tq in range(3)]
        groups.append((g, dil, L, view, cols))
    outs, lses = [], []
    for g, dil, L, view, cols in groups:
        o_g, lse_g = _attn_fwd("dil_fwd%d" % g, view, view, view, *cols, dil, L, min(L, BAND_STEPS), True, F32)
        outs.append(o_g.reshape(S, D_MODEL))
        lses.append(lse_g.reshape(S, D_MODEL))
    o_a, lse_a = _combine("dil_combine", outs, lses)
    h1 = _matmul("out_a", o_a, W["a_w_out"], "nn", F32, 512, 1024, 1024, resid=h0)

    n1 = _rms_fwd("rms_f0", h1, norms["ffn_norm0"])
    gu0, act0 = _ffn_gu("gu_f0", n1, W["gu"][0])
    h2 = _ffn_down("down_f0", act0, W["dn"][0], h1)

    n2 = _rms_fwd("rms_b", h2, norms["b_norm"])
    qkv = _matmul("proj_b", n2, W["b_w_qkv"], "nn", BF16, 512, 1024, 1024)
    z = _matmul("gate_b", n2, W["b_w_gate"], "nn", F32, 512, LANES, 1024)
    c = _gate_fwd("gate_cumsum", z, norms["b_f"])
    c16 = c[:, :N_HEADS]
    bias = (jnp.repeat(c16, HEAD_DIM, axis=1), c16.T.reshape(N_PAIRS, 2, S))
    fcols = [(lambda r, tq=tq: tq) for tq in range(3)]
    tf = min(S, 512)
    o_b, lse_b = _attn_fwd("fox_fwd", qkv, qkv, qkv, *fcols, 1, S, tf, False, BF16, bias=bias)
    h3 = _matmul("out_b", o_b, W["b_w_out"], "nn", F32, 512, 1024, 1024, resid=h2)

    n3 = _rms_fwd("rms_f1", h3, norms["ffn_norm1"])
    gu1, act1 = _ffn_gu("gu_f1", n3, W["gu"][1])
    h4 = _ffn_down("down_f1", act1, W["dn"][1], h3)

    dh4, d_final, loss = _loss_head("loss_head", h4, norms["final_norm"], target)

    def ffn_bwd(tag, dh_out, h_in, n, gu, act, wgu, wd, gain):
        dgu = _ffn_dact("dact_" + tag, dh_out, wd, gu)
        dwd = _ffn_dwd("dwd_" + tag, act, dh_out)
        dwgu = _ffn_dwgu("dwgu_" + tag, n, dgu)
        dn = _ffn_dn("dn_" + tag, dgu, wgu)
        dh_in, dgain = _rms_bwd("rmsb_" + tag, dn, h_in, gain, dh_out)
        return dh_in, dgain, dwgu, dwd

    dh3, d_ffn1, dwgu1, dwd1 = ffn_bwd("f1", dh4, h3, n3, gu1, act1, W["gu"][1], W["dn"][1], norms["ffn_norm1"])

    do_b = _matmul("dout_b", dh3, W["b_w_out"], "nt", BF16, 512, 1024, 1024)
    dw_b_out = _matmul("dwout_b", o_b, dh3, "tn", F32, 512, 1024, 1024)
    dq_b, dcrow = _attn_dq("fox_dq", qkv, qkv, qkv, do_b, o_b, lse_b, *fcols, 1, S, tf, False, bias=bias)
    dk_b, dv_b, dccol = _attn_dkv("fox_dkv", qkv, qkv, qkv, do_b, o_b, lse_b, *fcols, 1, S, tf, False, bias=bias)
    dc = dcrow[:, ::HEAD_DIM] - dccol.reshape(N_HEADS, S).T
    dz, d_bf = _gate_bwd("gate_bwd", jnp.pad(dc, ((0, 0), (0, LANES - N_HEADS))), z, norms["b_f"])
    dproj_b = jnp.concatenate([dq_b, dk_b, dv_b, dz.astype(BF16)], axis=1)
    dw_b_in = _matmul("dwin_b", n2, dproj_b, "tn", F32, 512, B_IN_PAD // 5, 1024)
    dn2 = _matmul("dn_b", dproj_b, W["b_w_cat"], "nt", F32, 512, 1024, B_IN_PAD // 5)
    dh2, d_bnorm = _rms_bwd("rmsb_b", dn2, h2, norms["b_norm"], dh3)

    dh1, d_ffn0, dwgu0, dwd0 = ffn_bwd("f0", dh2, h1, n1, gu0, act0, W["gu"][0], W["dn"][0], norms["ffn_norm0"])

    do_a = _matmul("dout_a", dh1, W["a_w_out"], "nt", BF16, 512, 1024, 1024)
    dw_a_out = _matmul("dwout_a", o_a, dh1, "tn", F32, 512, 1024, 1024)
    pieces = []
    for g, dil, L, view, cols in groups:
        t = min(L, BAND_STEPS)
        sv = lambda a: a.reshape(L, dil * a.shape[1])
        rot = tuple(sv(tb) for tb in tabs)
        args = (view, view, view, sv(do_a), sv(o_a), sv(lse_a), *cols, dil, L, t, True)
        dq_g = _attn_dq("dil_dq%d" % g, *args, tabs=rot)[0]
        dk_g, dv_g = _attn_dkv("dil_dkv%d" % g, *args, tabs=rot)
        pieces += [a.reshape(S, D_MODEL) for a in (dq_g, dk_g, dv_g)]
    dproj_a = jnp.concatenate(pieces, axis=1)
    dw_a_in = _mm_call("dwin_a", (N_DEV, D_MODEL // 512, S // 1024), n0, pl.BlockSpec((1024, 512), lambda d, i, k: (k, i)),
                       dproj_a, pl.BlockSpec((1024, 1152), lambda d, i, k: (k, d)), TN,
                       [jax.ShapeDtypeStruct((N_DEV, D_MODEL, 1152), F32)],
                       [pl.BlockSpec((None, 512, 1152), lambda d, i, k: (d, i, 0))], (512, 1152))[0]
    dn0 = _matmul("dn_a", dproj_a, W["a_w_in"], "nt", F32, 512, 1024, 1024)
    dx, d_anorm = _rms_bwd("rmsb_a", dn0, h0, norms["a_norm"], dh1)

    by_dest = lambda a, cols: a.reshape(D_MODEL, N_DEV, cols).transpose(1, 0, 2)
    wgrads = {
        "a_w_in": dw_a_in.reshape(N_DEV, 1152, D_MODEL),
        "a_w_out": dw_a_out.reshape(N_DEV, 128, D_MODEL),
        "b_w_in": by_dest(dw_b_in[:, :B_IN], 386).reshape(N_DEV, 386, D_MODEL),
        "b_w_out": dw_b_out.reshape(N_DEV, 128, D_MODEL),
        "gu0": dwgu0.reshape(N_DEV, FF_BLK, D_MODEL), "gu1": dwgu1.reshape(N_DEV, FF_BLK, D_MODEL),
        "dn0": dwd0.reshape(N_DEV, 352, D_MODEL), "dn1": dwd1.reshape(N_DEV, 352, D_MODEL),
    }
    vgrads = dict(a_norm=d_anorm, ffn_norm0=d_ffn0, ffn_norm1=d_ffn1, final_norm=d_final, b_norm=d_bnorm, b_f=d_bf)
    return loss, dx, wgrads, vgrads


def kernel(x, a_norm, a_w_in, a_w_out, b_norm, b_w_in, b_f, b_w_out, ffn_norm, ffn_w_gu, ffn_w_down, final_norm, loss_target, m_a_norm, m_a_w_in, m_a_w_out, m_b_norm, m_b_w_in, m_b_f, m_b_w_out, m_ffn_norm, m_ffn_w_gu, m_ffn_w_down, m_final_norm, v_a_norm, v_a_w_in, v_a_w_out, v_b_norm, v_b_w_in, v_b_f, v_b_w_out, v_ffn_norm, v_ffn_w_gu, v_ffn_w_down, v_final_norm):
    S = x.shape[1]
    xi, yi, ci = _position()
    dev = 4 * xi + 2 * yi + ci
    rows = lambda a: a.reshape(-1, D_MODEL)

    def shards(a_in, a_out, b_in, b_out, gu, dn):
        return {"a_w_in": rows(a_in), "a_w_out": rows(a_out), "b_w_in": rows(b_in), "b_w_out": rows(b_out),
                "gu0": rows(gu[0]), "gu1": rows(gu[1]), "dn0": rows(dn[0]), "dn1": rows(dn[1])}

    w_sh = shards(a_w_in, a_w_out, b_w_in, b_w_out, ffn_w_gu, ffn_w_down)
    m_sh = shards(m_a_w_in, m_a_w_out, m_b_w_in, m_b_w_out, m_ffn_w_gu, m_ffn_w_down)
    v_sh = shards(v_a_w_in, v_a_w_out, v_b_w_in, v_b_w_out, v_ffn_w_gu, v_ffn_w_down)

    wg = _all_gather("gather_weights", _pack_rows(w_sh, BF16))
    b_norm_full = _all_gather("gather_b_norm", jnp.pad(b_norm, ((0, 7), (0, 0)))).reshape(N_DEV, 8, LANES)[:, 0].reshape(1, D_MODEL)
    W = _gathered_weights(wg)
    bf_pad = jnp.pad(b_f, ((0, 0), (0, LANES - N_HEADS)))
    norms = dict(a_norm=a_norm[0], ffn_norm0=ffn_norm[0], ffn_norm1=ffn_norm[1], final_norm=final_norm,
                 b_norm=b_norm_full[0], b_f=bf_pad)

    loss, dx, wgrads, vgrads = _local_step(x.reshape(S, D_MODEL), loss_target.reshape(S, D_MODEL), W, norms)

    packed = _pack_rows(wgrads, BF16).reshape(4, 2, PACK_ROWS, D_MODEL)
    got = _swap_halves("rs_sibling", packed)
    sums = _pair_sum("rs_pair_sum", packed, got, ci.reshape(1).astype(jnp.int32))
    others = _exchange_chips("rs_chips", sums)
    mine = lax.dynamic_index_in_dim(sums, 2 * xi + yi, axis=0, keepdims=False)

    outs = {}
    for name, n_rows, _ in PACK:
        parts = [_unpack(mine, name)] + [_unpack(others[k], name) for k in range(3)]
        outs[name] = _adamw("adamw_" + name, parts, w_sh[name], m_sh[name], v_sh[name], ADAM_TILE[name])

    misc = jnp.concatenate([vgrads["b_f"][:, :N_HEADS], loss[:, :1], jnp.zeros((1, D_MODEL - N_HEADS - 1), F32)], axis=1)
    small = jnp.concatenate([vgrads["a_norm"], vgrads["ffn_norm0"], vgrads["ffn_norm1"], vgrads["final_norm"],
                             vgrads["b_norm"], misc, jnp.zeros((2, D_MODEL), F32)], axis=0)
    small_all = _all_gather("gather_small", small)
    pad_vec = lambda a: jnp.pad(a, ((0, 0), (0, D_MODEL - a.shape[1])))

    def small_pack(an, fn, fin, bf):
        return jnp.concatenate([an, fn, fin.reshape(1, D_MODEL), jnp.zeros((1, D_MODEL), F32), pad_vec(bf),
                                jnp.zeros((2, D_MODEL), F32)], axis=0)

    sg, sd, sm, sv = _adamw("adamw_small", [small_all[d] for d in range(N_DEV)], small_pack(a_norm, ffn_norm, final_norm, b_f),
                            small_pack(m_a_norm, m_ffn_norm, m_final_norm, m_b_f),
                            small_pack(v_a_norm, v_ffn_norm, v_final_norm, v_b_f), 8)
    g_bn = lax.dynamic_slice(sg[4:5], (0, dev * LANES), (1, LANES))
    bn = _adamw("adamw_b_norm", [g_bn], b_norm, m_b_norm, v_b_norm, 1)

    def tree(i):
        full = lambda name, ref: outs[name][i].reshape(ref.shape)
        sml = (sg, sd, sm, sv)[i]
        return dict(
            a_norm=sml[0:1], a_w_in=full("a_w_in", a_w_in), a_w_out=full("a_w_out", a_w_out), b_norm=bn[i],
            b_w_in=full("b_w_in", b_w_in), b_f=sml[5:6, :N_HEADS], b_w_out=full("b_w_out", b_w_out), ffn_norm=sml[1:3],
            ffn_w_gu=jnp.stack([outs["gu0"][i], outs["gu1"][i]]).reshape(ffn_w_gu.shape),
            ffn_w_down=jnp.stack([outs["dn0"][i], outs["dn1"][i]]).reshape(ffn_w_down.shape), final_norm=sml[3])

    order = ("a_norm", "a_w_in", "a_w_out", "b_norm", "b_w_in", "b_f", "b_w_out", "ffn_norm", "ffn_w_gu", "ffn_w_down", "final_norm")
    result = [sg[5, N_HEADS], dx.reshape(x.shape)]
    for i in range(4):
        t = tree(i)
        result += [t[n] for n in order]
    return tuple(result)
```

```python
import functools

import jax
import jax.numpy as jnp
from jax import lax
from jax.experimental import pallas as pl
from jax.experimental.pallas import tpu as pltpu

F32 = jnp.float32
BF16 = jnp.bfloat16

D_MODEL = 1024
N_HEADS = 16
HEAD_DIM = 64
N_PAIRS = N_HEADS // 2
LANES = 128
DILATED_PATTERNS = ((128, 1), (512, 4), (2048, 16))
BAND_STEPS = 128
ROT_DIM = HEAD_DIM // 4
ROPE_THETA = 500000.0
D_FF = 2816
RMS_EPS = 1e-6
NEG_INF = -1e30
SOFTMAX_SCALE = HEAD_DIM ** -0.5
N_DEV = 8
FF_BLK = 2 * D_FF // N_DEV
ADAM_LR, ADAM_B1, ADAM_B2, ADAM_EPS, ADAM_WD, ADAM_STEP = 0.001, 0.9, 0.999, 1e-08, 0.01, 10
VMEM_LIMIT = 52 * 1024 * 1024
MESH = pl.DeviceIdType.MESH

NN = (((1,), (0,)), ((), ()))
NT = (((1,), (1,)), ((), ()))
TN = (((0,), (0,)), ((), ()))


def _params(*sem):
    return pltpu.CompilerParams(dimension_semantics=sem, vmem_limit_bytes=VMEM_LIMIT)


def _dot(a, b, dims):
    return lax.dot_general(a, b, dims, preferred_element_type=F32)


def _mm_call(name, grid, a, a_spec, b, b_spec, dims, out_shapes, out_specs, acc_shape, epilogue=None,
             extras=(), extra_specs=()):
    nk = grid[2]
    n_extra = len(extras)
    n_out = len(out_shapes)

    def body(*refs):
        a_ref, b_ref = refs[0], refs[1]
        ex = refs[2:2 + n_extra]
        outs = refs[2 + n_extra:2 + n_extra + n_out]
        acc = refs[-1]
        j, k = pl.program_id(1), pl.program_id(2)

        @pl.when(k == 0)
        def _():
            acc[...] = jnp.zeros_like(acc)

        acc[...] += _dot(a_ref[...].astype(BF16), b_ref[...].astype(BF16), dims)

        @pl.when(k == nk - 1)
        def _():
            if epilogue is None:
                outs[0][...] = acc[...].astype(outs[0].dtype)
            else:
                epilogue(acc[...], ex, outs, j)

    return pl.pallas_call(
        body, grid=grid, in_specs=[a_spec, b_spec, *extra_specs], out_specs=out_specs, out_shape=out_shapes,
        scratch_shapes=[pltpu.VMEM(acc_shape, F32)], compiler_params=_params("parallel", "parallel", "arbitrary"),
        name=name)(a, b, *extras)


def _matmul(name, a, b, mode, out_dtype, tm, tn, tk, resid=None, col0_scale=None):
    if mode == "nn":
        (M, K), N = a.shape, b.shape[1]
        a_spec = pl.BlockSpec((tm, tk), lambda i, j, k: (i, k))
        b_spec = pl.BlockSpec((tk, tn), lambda i, j, k: (k, j))
        dims = NN
    elif mode == "nt":
        (M, K), N = a.shape, b.shape[0]
        a_spec = pl.BlockSpec((tm, tk), lambda i, j, k: (i, k))
        b_spec = pl.BlockSpec((tn, tk), lambda i, j, k: (j, k))
        dims = NT
    else:
        (K, M), N = a.shape, b.shape[1]
        a_spec = pl.BlockSpec((tk, tm), lambda i, j, k: (k, i))
        b_spec = pl.BlockSpec((tk, tn), lambda i, j, k: (k, j))
        dims = TN
    assert M % tm == 0 and N % tn == 0 and K % tk == 0, (name, M, N, K, tm, tn, tk)
    o_spec = pl.BlockSpec((tm, tn), lambda i, j, k: (i, j))
    extras, extra_specs, epilogue = (), (), None
    if resid is not None:
        extras, extra_specs = (resid,), (o_spec,)

        def epilogue(acc, ex, outs, j):
            outs[0][...] = (acc + ex[0][...]).astype(outs[0].dtype)

    elif col0_scale is not None:

        def epilogue(acc, ex, outs, j):
            outs[0][...] = (acc * jnp.where(j == 0, col0_scale, 1.0)).astype(outs[0].dtype)

    return _mm_call(name, (M // tm, N // tn, K // tk), a, a_spec, b, b_spec, dims,
                    [jax.ShapeDtypeStruct((M, N), out_dtype)], [o_spec], (tm, tn), epilogue, extras, extra_specs)[0]


def _rms_fwd(name, h, gain, tm=512):
    S, D = h.shape

    def body(h_ref, g_ref, n_ref):
        x = h_ref[...]
        rstd = lax.rsqrt(jnp.mean(x * x, axis=-1, keepdims=True) + RMS_EPS)
        n_ref[...] = (x * rstd * g_ref[...]).astype(BF16)

    return pl.pallas_call(
        body, grid=(S // tm,), in_specs=[pl.BlockSpec((tm, D), lambda i: (i, 0)), pl.BlockSpec((1, D), lambda i: (0, 0))],
        out_specs=pl.BlockSpec((tm, D), lambda i: (i, 0)), out_shape=jax.ShapeDtypeStruct((S, D), BF16),
        compiler_params=_params("parallel"), name=name)(h, gain.reshape(1, D))


def _rms_bwd(name, dn, h, gain, dres, tm=512):
    S, D = h.shape

    def body(dn_ref, h_ref, g_ref, r_ref, dh_ref, dg_ref):
        x = h_ref[...]
        rstd = lax.rsqrt(jnp.mean(x * x, axis=-1, keepdims=True) + RMS_EPS)
        xhat = x * rstd
        d = dn_ref[...]
        dxhat = d * g_ref[...]
        dh_ref[...] = rstd * (dxhat - xhat * jnp.mean(dxhat * xhat, axis=-1, keepdims=True)) + r_ref[...]

        @pl.when(pl.program_id(0) == 0)
        def _():
            dg_ref[...] = jnp.zeros_like(dg_ref)

        dg_ref[...] += jnp.sum(d * xhat, axis=0, keepdims=True)

    row = pl.BlockSpec((tm, D), lambda i: (i, 0))
    vec = pl.BlockSpec((1, D), lambda i: (0, 0))
    return pl.pallas_call(
        body, grid=(S // tm,), in_specs=[row, row, vec, row], out_specs=[row, vec],
        out_shape=[jax.ShapeDtypeStruct((S, D), F32), jax.ShapeDtypeStruct((1, D), F32)],
        compiler_params=_params("arbitrary"), name=name)(dn, h, gain.reshape(1, D), dres)


def _loss_head(name, h, gain, target, tm=512):
    S, D = h.shape

    def body(h_ref, g_ref, t_ref, dh_ref, dg_ref, loss_ref):
        x = h_ref[...]
        rstd = lax.rsqrt(jnp.mean(x * x, axis=-1, keepdims=True) + RMS_EPS)
        xhat = x * rstd
        err = xhat * g_ref[...] - t_ref[...]
        dy = err * (1.0 / D)
        dxhat = dy * g_ref[...]
        dh_ref[...] = rstd * (dxhat - xhat * jnp.mean(dxhat * xhat, axis=-1, keepdims=True))

        @pl.when(pl.program_id(0) == 0)
        def _():
            dg_ref[...] = jnp.zeros_like(dg_ref)
            loss_ref[...] = jnp.zeros_like(loss_ref)

        dg_ref[...] += jnp.sum(dy * xhat, axis=0, keepdims=True)
        part = 0.5 * jnp.sum(jnp.mean(err * err, axis=-1, keepdims=True), axis=0, keepdims=True)
        loss_ref[...] += jnp.broadcast_to(part, loss_ref.shape)

    row = pl.BlockSpec((tm, D), lambda i: (i, 0))
    vec = pl.BlockSpec((1, D), lambda i: (0, 0))
    return pl.pallas_call(
        body, grid=(S // tm,), in_specs=[row, vec, row], out_specs=[row, vec, pl.BlockSpec((1, LANES), lambda i: (0, 0))],
        out_shape=[jax.ShapeDtypeStruct((S, D), F32), jax.ShapeDtypeStruct((1, D), F32),
                   jax.ShapeDtypeStruct((1, LANES), F32)],
        compiler_params=_params("arbitrary"), name=name)(h, gain.reshape(1, D), target)


def _rope_tables(S):
    half = ROT_DIM // 2
    inv_freq = ROPE_THETA ** (-jnp.arange(half, dtype=F32) * 2.0 / ROT_DIM)
    ang = jnp.arange(S, dtype=F32)[:, None] * inv_freq[None, :]
    cos, sin = jnp.cos(ang), jnp.sin(ang)
    one = jnp.ones((S, HEAD_DIM - ROT_DIM), F32)
    zero = jnp.zeros((S, HEAD_DIM - ROT_DIM), F32)
    zh = jnp.zeros((S, half), F32)
    c = jnp.concatenate([cos, cos, one], axis=1)
    sa = jnp.concatenate([-sin, zh, zero], axis=1)
    sb = jnp.concatenate([zh, sin, zero], axis=1)
    return tuple(jnp.concatenate([t, t], axis=1) for t in (c, sa, sb))


def _rotate(x, c, sa, sb, sign):
    return x * c + sign * (pltpu.roll(x, LANES - ROT_DIM // 2, 1) * sa + pltpu.roll(x, ROT_DIM // 2, 1) * sb)


def _a_proj(name, n, w, tabs, tm=512):
    S, D = n.shape
    N = w.shape[1]
    tn = D_MODEL

    def epilogue(acc, ex, outs, j):
        @pl.when(j % 3 == 2)
        def _():
            outs[0][...] = acc.astype(BF16)

        @pl.when(j % 3 != 2)
        def _():
            c, sa, sb = ex[0][...], ex[1][...], ex[2][...]
            scale = jnp.where(j % 3 == 0, SOFTMAX_SCALE, 1.0)
            for p in range(N_PAIRS):
                cols = slice(p * LANES, (p + 1) * LANES)
                outs[0][:, cols] = (_rotate(acc[:, cols], c, sa, sb, 1.0) * scale).astype(BF16)

    tab = pl.BlockSpec((tm, LANES), lambda i, j, k: (i, 0))
    return _mm_call(name, (S // tm, N // tn, 1), n, pl.BlockSpec((tm, D), lambda i, j, k: (i, 0)),
                    w, pl.BlockSpec((D, tn), lambda i, j, k: (0, j)), NN,
                    [jax.ShapeDtypeStruct((S, N), BF16)], [pl.BlockSpec((tm, tn), lambda i, j, k: (i, j))], (tm, tn),
                    epilogue, tabs, (tab, tab, tab))[0]


def _lo_lanes():
    return lax.broadcasted_iota(jnp.int32, (1, LANES), 1) < HEAD_DIM


def _attn_probs(q2, k2, v2, do2, lse2, dd, hm, hh, valid, scale, bias2):
    s = _dot(jnp.where(hm, q2, 0), k2, NT) * scale
    if bias2 is not None:
        cb2, ct2 = bias2
        s = s + cb2[:, hh * HEAD_DIM:hh * HEAD_DIM + 1] - ct2[hh:hh + 1, :]
    p = jnp.where(valid, jnp.exp(s - lse2[:, hh * HEAD_DIM:hh * HEAD_DIM + 1]), 0.0)
    dp = _dot(jnp.where(hm, do2, 0), v2, NT)
    d_h = jnp.sum(jnp.where(hm, dd, 0.0), axis=1, keepdims=True)
    return p, p * (dp - d_h)


def _attn_dq(name, q, k, v, do, o, lse, qcol, kcol, vcol, R, L, t, band, tabs=None, bias=None):
    nq, nkv, kv_of, _ = _attn_geometry(L, t, band)
    scale = HEAD_DIM ** -0.5
    W = D_MODEL
    has_bias, has_rot = bias is not None, tabs is not None

    def body(*refs):
        q_ref, k_ref, v_ref, do_ref, o_ref, lse_ref = refs[:6]
        pos = 6
        if has_bias:
            cb_ref, ct_ref = refs[pos:pos + 2]
            pos += 2
        if has_rot:
            c_ref, sa_ref, sb_ref = refs[pos:pos + 3]
            pos += 3
        dq_ref = refs[pos]
        pos += 1
        if has_bias:
            dcr_ref = refs[pos]
            dcr_scr = refs[-1]
            dq_scr = refs[-2]
        else:
            dq_scr = refs[-1]
        qi, j = pl.program_id(1), pl.program_id(2)
        kb = qi - 1 + j if band else j
        active = kb >= 0 if band else j <= qi
        lo = _lo_lanes()

        @pl.when(j == 0)
        def _():
            dq_scr[...] = jnp.zeros_like(dq_scr)
            if has_bias:
                dcr_scr[...] = jnp.zeros_like(dcr_scr)

        @pl.when(active)
        def _():
            valid = _valid_mask(qi, kb, t, band)

            def pair(p, carry):
                c0 = pl.multiple_of(p * LANES, LANES)
                cs = pl.ds(c0, LANES)
                q2, k2, v2, do2 = q_ref[:, cs], k_ref[:, cs], v_ref[:, cs], do_ref[:, cs]
                dd = do2.astype(F32) * o_ref[:, cs].astype(F32)
                lse2 = lse_ref[:, cs]
                bias2 = (cb_ref[:, cs], ct_ref[p]) if has_bias else None
                dq2 = jnp.zeros((t, LANES), F32)
                for hh in range(2):
                    hm = lo if hh == 0 else jnp.logical_not(lo)
                    _, ds = _attn_probs(q2, k2, v2, do2, lse2, dd, hm, hh, valid, scale, bias2)
                    if has_bias:
                        dcr_scr[2 * p + hh] += jnp.sum(ds, axis=1, keepdims=True)
                    dq2 = dq2 + _dot(ds.astype(BF16), jnp.where(hm, k2, 0), NN)
                dq_scr[:, cs] += dq2 * scale
                return carry

            lax.fori_loop(0, N_PAIRS, pair, 0)

        @pl.when(j == nkv - 1)
        def _():
            for p in range(N_PAIRS):
                cols = slice(p * LANES, (p + 1) * LANES)
                x = dq_scr[:, cols]
                if has_rot:
                    x = _rotate(x, c_ref[...], sa_ref[...], sb_ref[...], -1.0)
                dq_ref[:, cols] = x.astype(BF16)
                if has_bias:
                    dcr_ref[:, cols] = jnp.broadcast_to(jnp.where(lo, dcr_scr[2 * p], dcr_scr[2 * p + 1]), (t, LANES))

    qrow = lambda col: pl.BlockSpec((t, W), lambda r, qi, j: (qi, col(r)))
    krow = lambda col: pl.BlockSpec((t, W), lambda r, qi, j: (kv_of(qi, j), col(r)))
    own = pl.BlockSpec((t, W), lambda r, qi, j: (qi, r))
    in_specs = [qrow(qcol), krow(kcol), krow(vcol), own, own, own]
    args = [q, k, v, do, o, lse]
    if has_bias:
        in_specs += [pl.BlockSpec((t, W), lambda r, qi, j: (qi, 0)),
                     pl.BlockSpec((N_PAIRS, 2, t), lambda r, qi, j: (0, 0, kv_of(qi, j)))]
        args += list(bias)
    if has_rot:
        in_specs += [pl.BlockSpec((t, LANES), lambda r, qi, j: (qi, r))] * 3
        args += list(tabs)
    out_specs, out_shape = [own], [jax.ShapeDtypeStruct((L, R * W), BF16)]
    scratch = [pltpu.VMEM((t, W), F32)]
    if has_bias:
        out_specs.append(own)
        out_shape.append(jax.ShapeDtypeStruct((L, R * W), F32))
        scratch.append(pltpu.VMEM((N_HEADS, t, 1), F32))
    return pl.pallas_call(
        body, grid=(R, nq, nkv), in_specs=in_specs, out_specs=out_specs, out_shape=out_shape, scratch_shapes=scratch,
        compiler_params=_params("parallel", "parallel", "arbitrary"), name=name)(*args)


def _attn_dkv(name, q, k, v, do, o, lse, qcol, kcol, vcol, R, L, t, band, tabs=None, bias=None):
    nq, nkv, _, q_of = _attn_geometry(L, t, band)
    scale = HEAD_DIM ** -0.5
    W = D_MODEL
    has_bias, has_rot = bias is not None, tabs is not None

    def body(*refs):
        q_ref, k_ref, v_ref, do_ref, o_ref, lse_ref = refs[:6]
        pos = 6
        if has_bias:
            cb_ref, ct_ref = refs[pos:pos + 2]
            pos += 2
        if has_rot:
            c_ref, sa_ref, sb_ref = refs[pos:pos + 3]
            pos += 3
        dk_ref, dv_ref = refs[pos:pos + 2]
        pos += 2
        if has_bias:
            dcc_ref = refs[pos]
        dk_scr, dv_scr = refs[-2:]
        kb, j = pl.program_id(1), pl.program_id(2)
        qb = kb + j if band else j
        active = qb <= nq - 1 if band else j >= kb
        lo = _lo_lanes()

        @pl.when(j == 0)
        def _():
            dk_scr[...] = jnp.zeros_like(dk_scr)
            dv_scr[...] = jnp.zeros_like(dv_scr)
            if has_bias:
                dcc_ref[...] = jnp.zeros_like(dcc_ref)

        @pl.when(active)
        def _():
            valid = _valid_mask(qb, kb, t, band)

            def pair(p, carry):
                c0 = pl.multiple_of(p * LANES, LANES)
                cs = pl.ds(c0, LANES)
                q2, k2, v2, do2 = q_ref[:, cs], k_ref[:, cs], v_ref[:, cs], do_ref[:, cs]
                dd = do2.astype(F32) * o_ref[:, cs].astype(F32)
                lse2 = lse_ref[:, cs]
                bias2 = (cb_ref[:, cs], ct_ref[p]) if has_bias else None
                dk2 = jnp.zeros((t, LANES), F32)
                dv2 = jnp.zeros((t, LANES), F32)
                for hh in range(2):
                    hm = lo if hh == 0 else jnp.logical_not(lo)
                    pr, ds = _attn_probs(q2, k2, v2, do2, lse2, dd, hm, hh, valid, scale, bias2)
                    if has_bias:
                        dcc_ref[p, hh:hh + 1, :] += jnp.sum(ds, axis=0, keepdims=True)
                    dv2 = dv2 + _dot(pr.astype(BF16), jnp.where(hm, do2, 0), TN)
                    dk2 = dk2 + _dot(ds.astype(BF16), jnp.where(hm, q2, 0), TN)
                dk_scr[:, cs] += dk2 * scale
                dv_scr[:, cs] += dv2
                return carry

            lax.fori_loop(0, N_PAIRS, pair, 0)

        @pl.when(j == nkv - 1)
        def _():
            dv_ref[...] = dv_scr[...].astype(BF16)
            for p in range(N_PAIRS):
                cols = slice(p * LANES, (p + 1) * LANES)
                x = dk_scr[:, cols]
                if has_rot:
                    x = _rotate(x, c_ref[...], sa_ref[...], sb_ref[...], -1.0)
                dk_ref[:, cols] = x.astype(BF16)

    qrow = lambda col: pl.BlockSpec((t, W), lambda r, kb, j: (q_of(kb, j), col(r)))
    krow = lambda col: pl.BlockSpec((t, W), lambda r, kb, j: (kb, col(r)))
    qown = pl.BlockSpec((t, W), lambda r, kb, j: (q_of(kb, j), r))
    kown = pl.BlockSpec((t, W), lambda r, kb, j: (kb, r))
    in_specs = [qrow(qcol), krow(kcol), krow(vcol), qown, qown, qown]
    args = [q, k, v, do, o, lse]
    if has_bias:
        in_specs += [pl.BlockSpec((t, W), lambda r, kb, j: (q_of(kb, j), 0)),
                     pl.BlockSpec((N_PAIRS, 2, t), lambda r, kb, j: (0, 0, kb))]
        args += list(bias)
    if has_rot:
        in_specs += [pl.BlockSpec((t, LANES), lambda r, kb, j: (kb, r))] * 3
        args += list(tabs)
    out_specs = [kown, kown]
    out_shape = [jax.ShapeDtypeStruct((L, R * W), BF16), jax.ShapeDtypeStruct((L, R * W), BF16)]
    if has_bias:
        out_specs.append(pl.BlockSpec((N_PAIRS, 2, t), lambda r, kb, j: (0, 0, kb)))
        out_shape.append(jax.ShapeDtypeStruct((N_PAIRS, 2, L), F32))
    return pl.pallas_call(
        body, grid=(R, nq, nkv), in_specs=in_specs, out_specs=out_specs, out_shape=out_shape,
        scratch_shapes=[pltpu.VMEM((t, W), F32), pltpu.VMEM((t, W), F32)],
        compiler_params=_params("parallel", "parallel", "arbitrary"), name=name)(*args)


def _rep_rows(x2, lo):
    sw = pltpu.roll(x2, HEAD_DIM, 1)
    return jnp.where(lo, x2, sw), jnp.where(lo, sw, x2)


def _band_masks(t, first):
    ri = lax.broadcasted_iota(jnp.int32, (t, t), 0)
    ci = lax.broadcasted_iota(jnp.int32, (t, t), 1)
    neg_prev = jnp.where((ci >= ri) & jnp.logical_not(first), 0.0, NEG_INF)
    neg_cur = jnp.where(ci <= ri, 0.0, NEG_INF)
    return neg_prev, neg_cur


def _dil_specs(L, R, t, qcol, kcol, vcol):
    W = D_MODEL
    prev = lambda qi: jnp.maximum(qi - 1, 0)
    return dict(
        q=pl.BlockSpec((t, W), lambda r, qi: (qi, qcol(r))),
        kp=pl.BlockSpec((t, W), lambda r, qi: (prev(qi), kcol(r))), kc=pl.BlockSpec((t, W), lambda r, qi: (qi, kcol(r))),
        vp=pl.BlockSpec((t, W), lambda r, qi: (prev(qi), vcol(r))), vc=pl.BlockSpec((t, W), lambda r, qi: (qi, vcol(r))),
        own=pl.BlockSpec((t, W), lambda r, qi: (qi, r)), tab=pl.BlockSpec((t, LANES), lambda r, qi: (qi, r)))


def _dil_fwd(name, x, qcol, kcol, vcol, R, L):
    t = BAND_STEPS
    W = D_MODEL
    sp = _dil_specs(L, R, t, qcol, kcol, vcol)

    def body(q_ref, kp_ref, kc_ref, vp_ref, vc_ref, o_ref, lse_ref):
        lo = _lo_lanes()
        neg_p, neg_c = _band_masks(t, pl.program_id(1) == 0)
        for p in range(N_PAIRS):
            cols = slice(p * LANES, (p + 1) * LANES)
            q2, kp2, kc2, vp2, vc2 = q_ref[:, cols], kp_ref[:, cols], kc_ref[:, cols], vp_ref[:, cols], vc_ref[:, cols]
            o2 = jnp.zeros((t, LANES), F32)
            lses = []
            for hh in range(2):
                hm = lo if hh == 0 else jnp.logical_not(lo)
                qh = jnp.where(hm, q2, 0)
                s_p = _dot(qh, kp2, NT) + neg_p
                s_c = _dot(qh, kc2, NT) + neg_c
                m = jnp.maximum(jnp.max(s_p, axis=1, keepdims=True), jnp.max(s_c, axis=1, keepdims=True))
                p_p, p_c = jnp.exp(s_p - m), jnp.exp(s_c - m)
                l = jnp.sum(p_p, axis=1, keepdims=True) + jnp.sum(p_c, axis=1, keepdims=True)
                pv = _dot(p_p.astype(BF16), jnp.where(hm, vp2, 0), NN) + _dot(p_c.astype(BF16), jnp.where(hm, vc2, 0), NN)
                o2 = o2 + pv * (1.0 / l)
                lses.append(m + jnp.log(l))
            o_ref[:, cols] = o2
            lse_ref[:, cols] = jnp.where(lo, lses[0], lses[1])

    return pl.pallas_call(
        body, grid=(R, L // t), in_specs=[sp["q"], sp["kp"], sp["kc"], sp["vp"], sp["vc"]], out_specs=[sp["own"], sp["own"]],
        out_shape=[jax.ShapeDtypeStruct((L, R * W), F32), jax.ShapeDtypeStruct((L, R * W), F32)],
        compiler_params=_params("parallel", "parallel"), name=name)(x, x, x, x, x)


def _dil_head_grads(qh, k2, v2, doh, neg, lse_h, d_h):
    p = jnp.exp(_dot(qh, k2, NT) + neg - lse_h)
    return p, p * (_dot(doh, v2, NT) - d_h)


def _dil_dq(name, x, do, o, lse, tabs, qcol, kcol, vcol, R, L):
    t = BAND_STEPS
    W = D_MODEL
    sp = _dil_specs(L, R, t, qcol, kcol, vcol)

    def body(q_ref, kp_ref, kc_ref, vp_ref, vc_ref, do_ref, o_ref, lse_ref, c_ref, sa_ref, sb_ref, dq_ref):
        lo = _lo_lanes()
        neg_p, neg_c = _band_masks(t, pl.program_id(1) == 0)
        for p in range(N_PAIRS):
            cols = slice(p * LANES, (p + 1) * LANES)
            q2, kp2, kc2, vp2, vc2 = q_ref[:, cols], kp_ref[:, cols], kc_ref[:, cols], vp_ref[:, cols], vc_ref[:, cols]
            do2 = do_ref[:, cols]
            dd = do2.astype(F32) * o_ref[:, cols].astype(F32)
            lse_h = _rep_rows(lse_ref[:, cols], lo)
            dq2 = jnp.zeros((t, LANES), F32)
            for hh in range(2):
                hm = lo if hh == 0 else jnp.logical_not(lo)
                qh, doh = jnp.where(hm, q2, 0), jnp.where(hm, do2, 0)
                d_h = jnp.sum(jnp.where(hm, dd, 0.0), axis=1, keepdims=True)
                _, ds_p = _dil_head_grads(qh, kp2, vp2, doh, neg_p, lse_h[hh], d_h)
                _, ds_c = _dil_head_grads(qh, kc2, vc2, doh, neg_c, lse_h[hh], d_h)
                dq2 = dq2 + _dot(ds_p.astype(BF16), jnp.where(hm, kp2, 0), NN) + _dot(ds_c.astype(BF16), jnp.where(hm, kc2, 0), NN)
            dq_ref[:, cols] = _rotate(dq2 * SOFTMAX_SCALE, c_ref[...], sa_ref[...], sb_ref[...], -1.0).astype(BF16)

    return pl.pallas_call(
        body, grid=(R, L // t),
        in_specs=[sp["q"], sp["kp"], sp["kc"], sp["vp"], sp["vc"], sp["own"], sp["own"], sp["own"], sp["tab"], sp["tab"], sp["tab"]],
        out_specs=sp["own"], out_shape=jax.ShapeDtypeStruct((L, R * W), BF16),
        compiler_params=_params("parallel", "parallel"), name=name)(x, x, x, x, x, do, o, lse, *tabs)


def _dil_dkv(name, x, do, o, lse, tabs, qcol, kcol, vcol, R, L):
    t = BAND_STEPS
    W = D_MODEL
    nq = L // t
    nxt = lambda kb: jnp.minimum(kb + 1, nq - 1)
    cur_q = pl.BlockSpec((t, W), lambda r, kb: (kb, qcol(r)))
    nxt_q = pl.BlockSpec((t, W), lambda r, kb: (nxt(kb), qcol(r)))
    cur_o = pl.BlockSpec((t, W), lambda r, kb: (kb, r))
    nxt_o = pl.BlockSpec((t, W), lambda r, kb: (nxt(kb), r))
    tab = pl.BlockSpec((t, LANES), lambda r, kb: (kb, r))

    def body(k_ref, v_ref, qc_ref, qn_ref, doc_ref, don_ref, oc_ref, on_ref, lc_ref, ln_ref, c_ref, sa_ref, sb_ref,
             dk_ref, dv_ref):
        lo = _lo_lanes()
        ri = lax.broadcasted_iota(jnp.int32, (t, t), 0)
        ci = lax.broadcasted_iota(jnp.int32, (t, t), 1)
        neg_c = jnp.where(ci <= ri, 0.0, NEG_INF)
        neg_n = jnp.where((ci >= ri) & (pl.program_id(1) + 1 < nq), 0.0, NEG_INF)
        for p in range(N_PAIRS):
            cols = slice(p * LANES, (p + 1) * LANES)
            k2, v2 = k_ref[:, cols], v_ref[:, cols]
            dk2 = jnp.zeros((t, LANES), F32)
            dv2 = jnp.zeros((t, LANES), F32)
            for q_ref, do_ref, o_ref, l_ref, neg in ((qc_ref, doc_ref, oc_ref, lc_ref, neg_c), (qn_ref, don_ref, on_ref, ln_ref, neg_n)):
                q2, do2 = q_ref[:, cols], do_ref[:, cols]
                dd = do2.astype(F32) * o_ref[:, cols].astype(F32)
                lse_h = _rep_rows(l_ref[:, cols], lo)
                for hh in range(2):
                    hm = lo if hh == 0 else jnp.logical_not(lo)
                    qh, doh = jnp.where(hm, q2, 0), jnp.where(hm, do2, 0)
                    d_h = jnp.sum(jnp.where(hm, dd, 0.0), axis=1, keepdims=True)
                    pr, ds = _dil_head_grads(qh, k2, v2, doh, neg, lse_h[hh], d_h)
                    dv2 = dv2 + _dot(pr.astype(BF16), doh, TN)
                    dk2 = dk2 + _dot(ds.astype(BF16), qh, TN)
            dk_ref[:, cols] = _rotate(dk2, c_ref[...], sa_ref[...], sb_ref[...], -1.0).astype(BF16)
            dv_ref[:, cols] = dv2.astype(BF16)

    kcur = pl.BlockSpec((t, W), lambda r, kb: (kb, kcol(r)))
    vcur = pl.BlockSpec((t, W), lambda r, kb: (kb, vcol(r)))
    return pl.pallas_call(
        body, grid=(R, nq),
        in_specs=[kcur, vcur, cur_q, nxt_q, cur_o, nxt_o, cur_o, nxt_o, cur_o, nxt_o, tab, tab, tab],
        out_specs=[cur_o, cur_o], out_shape=[jax.ShapeDtypeStruct((L, R * W), BF16)] * 2,
        compiler_params=_params("parallel", "parallel"), name=name)(x, x, x, x, do, do, o, o, lse, lse, *tabs)


def _fox_operands(q2, k2, kb2, lo, hh):
    lane = lax.broadcasted_iota(jnp.int32, (1, LANES), 1)
    if hh == 0:
        ones = ((lane >= HEAD_DIM) & (lane < HEAD_DIM + 3)).astype(BF16)
        return jnp.where(lo, q2, ones), jnp.where(lo, k2, kb2)
    ones = (lane < 3).astype(BF16)
    return jnp.where(lo, ones, q2), jnp.where(lo, kb2, k2)


def _causal_neg(t):
    ri = lax.broadcasted_iota(jnp.int32, (t, t), 0)
    ci = lax.broadcasted_iota(jnp.int32, (t, t), 1)
    return jnp.where(ci <= ri, 0.0, NEG_INF)


def _fox_fwd(name, qkv, kbias, t):
    S = qkv.shape[0]
    W = D_MODEL
    nq = S // t
    rep = t // LANES

    def body(q_ref, k_ref, v_ref, kb_ref, o_ref, lse_ref, m_scr, l_scr, acc_scr):
        qi, j = pl.program_id(0), pl.program_id(1)
        lo = _lo_lanes()

        @pl.when(j == 0)
        def _():
            m_scr[...] = jnp.full_like(m_scr, NEG_INF)
            l_scr[...] = jnp.zeros_like(l_scr)
            acc_scr[...] = jnp.zeros_like(acc_scr)

        def step(masked):
            neg = _causal_neg(t) if masked else None

            def pair(p, carry):
                cs = pl.ds(pl.multiple_of(p * LANES, LANES), LANES)
                q2, k2, v2, kb2 = q_ref[:, cs], k_ref[:, cs], v_ref[:, cs], kb_ref[:, cs]
                pvs, alphas = [], []
                for hh in range(2):
                    hm = lo if hh == 0 else jnp.logical_not(lo)
                    qh, kh = _fox_operands(q2, k2, kb2, lo, hh)
                    s = _dot(qh, kh, NT)
                    if masked:
                        s = s + neg
                    h = 2 * p + hh
                    m_prev = m_scr[h]
                    m_new = jnp.maximum(m_prev, jnp.max(s, axis=1, keepdims=True))
                    pe = jnp.exp(s - jnp.tile(m_new, (1, rep)))
                    alpha = jnp.exp(m_prev - m_new)
                    l_scr[h] = alpha * l_scr[h] + jnp.sum(pe, axis=1, keepdims=True)
                    m_scr[h] = m_new
                    pvs.append(_dot(pe.astype(BF16), jnp.where(hm, v2, 0), NN))
                    alphas.append(alpha)
                acc_scr[:, cs] = acc_scr[:, cs] * jnp.where(lo, alphas[0], alphas[1]) + pvs[0] + pvs[1]
                return carry

            lax.fori_loop(0, N_PAIRS, pair, 0)

        @pl.when(j < qi)
        def _():
            step(False)

        @pl.when(j == qi)
        def _():
            step(True)

        @pl.when(j == nq - 1)
        def _():
            for p in range(N_PAIRS):
                cols = slice(p * LANES, (p + 1) * LANES)
                l2 = jnp.where(lo, l_scr[2 * p], l_scr[2 * p + 1])
                m2 = jnp.where(lo, m_scr[2 * p], m_scr[2 * p + 1])
                o_ref[:, cols] = (acc_scr[:, cols] / l2).astype(BF16)
                lse_ref[:, cols] = m2 + jnp.log(l2)

    kv = lambda col: pl.BlockSpec((t, W), lambda qi, j: (jnp.minimum(j, qi), col))
    own = pl.BlockSpec((t, W), lambda qi, j: (qi, 0))
    return pl.pallas_call(
        body, grid=(nq, nq), in_specs=[own, kv(1), kv(2), kv(0)], out_specs=[own, own],
        out_shape=[jax.ShapeDtypeStruct((S, W), BF16), jax.ShapeDtypeStruct((S, W), F32)],
        scratch_shapes=[pltpu.VMEM((N_HEADS, t, LANES), F32), pltpu.VMEM((N_HEADS, t, LANES), F32), pltpu.VMEM((t, W), F32)],
        compiler_params=_params("parallel", "arbitrary"), name=name)(qkv, qkv, qkv, kbias)


def _fox_head_grads(qh, kh, v2, doh, neg, lse_h, d_h, rep):
    s = _dot(qh, kh, NT)
    if neg is not None:
        s = s + neg
    p = jnp.exp(s - jnp.tile(lse_h, (1, rep)))
    return p, p * (_dot(doh, v2, NT) - d_h)


def _fox_dq(name, qkv, kbias, do, o, lse, t):
    S = qkv.shape[0]
    W = D_MODEL
    nq = S // t
    rep = t // LANES

    def body(q_ref, k_ref, v_ref, kb_ref, do_ref, o_ref, lse_ref, dq_ref, rs_ref, dq_scr, rs_scr):
        qi, j = pl.program_id(0), pl.program_id(1)
        lo = _lo_lanes()

        @pl.when(j == 0)
        def _():
            dq_scr[...] = jnp.zeros_like(dq_scr)
            rs_scr[...] = jnp.zeros_like(rs_scr)

        def step(masked):
            neg = _causal_neg(t) if masked else None

            def pair(p, carry):
                cs = pl.ds(pl.multiple_of(p * LANES, LANES), LANES)
                q2, k2, v2, kb2, do2 = q_ref[:, cs], k_ref[:, cs], v_ref[:, cs], kb_ref[:, cs], do_ref[:, cs]
                dd = do2.astype(F32) * o_ref[:, cs].astype(F32)
                lse_h = _rep_rows(lse_ref[:, cs], lo)
                dq2 = jnp.zeros((t, LANES), F32)
                for hh in range(2):
                    hm = lo if hh == 0 else jnp.logical_not(lo)
                    qh, kh = _fox_operands(q2, k2, kb2, lo, hh)
                    d_h = jnp.sum(jnp.where(hm, dd, 0.0), axis=1, keepdims=True)
                    _, ds = _fox_head_grads(qh, kh, v2, jnp.where(hm, do2, 0), neg, lse_h[hh], d_h, rep)
                    rs_scr[2 * p + hh] += jnp.broadcast_to(jnp.sum(ds, axis=1, keepdims=True), (t, LANES))
                    dq2 = dq2 + _dot(ds.astype(BF16), jnp.where(hm, k2, 0), NN)
                dq_scr[:, cs] += dq2
                return carry

            lax.fori_loop(0, N_PAIRS, pair, 0)

        @pl.when(j < qi)
        def _():
            step(False)

        @pl.when(j == qi)
        def _():
            step(True)

        @pl.when(j == nq - 1)
        def _():
            dq_ref[...] = (dq_scr[...] * SOFTMAX_SCALE).astype(BF16)
            for p in range(N_PAIRS):
                rs_ref[:, p * LANES:(p + 1) * LANES] = jnp.where(lo, rs_scr[2 * p], rs_scr[2 * p + 1])

    kv = lambda col: pl.BlockSpec((t, W), lambda qi, j: (jnp.minimum(j, qi), col))
    own = pl.BlockSpec((t, W), lambda qi, j: (qi, 0))
    return pl.pallas_call(
        body, grid=(nq, nq), in_specs=[own, kv(1), kv(2), kv(0), own, own, own], out_specs=[own, own],
        out_shape=[jax.ShapeDtypeStruct((S, W), BF16), jax.ShapeDtypeStruct((S, W), F32)],
        scratch_shapes=[pltpu.VMEM((t, W), F32), pltpu.VMEM((N_HEADS, t, LANES), F32)],
        compiler_params=_params("parallel", "arbitrary"), name=name)(qkv, qkv, qkv, kbias, do, o, lse)


def _fox_dkv(name, qkv, kbias, do, o, lse, t):
    S = qkv.shape[0]
    W = D_MODEL
    nq = S // t
    rep = t // LANES

    def body(q_ref, k_ref, v_ref, kb_ref, do_ref, o_ref, lse_ref, dk_ref, dv_ref, dc_ref, dk_scr, dv_scr):
        kb, j = pl.program_id(0), pl.program_id(1)
        lo = _lo_lanes()

        @pl.when(j == 0)
        def _():
            dk_scr[...] = jnp.zeros_like(dk_scr)
            dv_scr[...] = jnp.zeros_like(dv_scr)
            dc_ref[...] = jnp.zeros_like(dc_ref)

        def step(masked):
            neg = _causal_neg(t) if masked else None

            def pair(p, carry):
                cs = pl.ds(pl.multiple_of(p * LANES, LANES), LANES)
                q2, k2, v2, kb2, do2 = q_ref[:, cs], k_ref[:, cs], v_ref[:, cs], kb_ref[:, cs], do_ref[:, cs]
                dd = do2.astype(F32) * o_ref[:, cs].astype(F32)
                lse_h = _rep_rows(lse_ref[:, cs], lo)
                dv2 = jnp.zeros((t, LANES), F32)
                dk2 = jnp.zeros((t, LANES), F32)
                for hh in range(2):
                    hm = lo if hh == 0 else jnp.logical_not(lo)
                    qh, kh = _fox_operands(q2, k2, kb2, lo, hh)
                    doh = jnp.where(hm, do2, 0)
                    d_h = jnp.sum(jnp.where(hm, dd, 0.0), axis=1, keepdims=True)
                    pr, ds = _fox_head_grads(qh, kh, v2, doh, neg, lse_h[hh], d_h, rep)
                    dc_ref[p, hh:hh + 1, :] += jnp.sum(ds, axis=0, keepdims=True)
                    dv2 = dv2 + _dot(pr.astype(BF16), doh, TN)
                    dk2 = dk2 + _dot(ds.astype(BF16), jnp.where(hm, q2, 0), TN)
                dv_scr[:, cs] += dv2
                dk_scr[:, cs] += dk2
                return carry

            lax.fori_loop(0, N_PAIRS, pair, 0)

        @pl.when(j > kb)
        def _():
            step(False)

        @pl.when(j == kb)
        def _():
            step(True)

        @pl.when(j == nq - 1)
        def _():
            dv_ref[...] = dv_scr[...].astype(BF16)
            dk_ref[...] = dk_scr[...].astype(BF16)

    qrow = pl.BlockSpec((t, W), lambda kb, j: (jnp.maximum(j, kb), 0))
    krow = lambda col: pl.BlockSpec((t, W), lambda kb, j: (kb, col))
    own = pl.BlockSpec((t, W), lambda kb, j: (kb, 0))
    return pl.pallas_call(
        body, grid=(nq, nq), in_specs=[qrow, krow(1), krow(2), krow(0), qrow, qrow, qrow],
        out_specs=[own, own, pl.BlockSpec((N_PAIRS, 2, t), lambda kb, j: (0, 0, kb))],
        out_shape=[jax.ShapeDtypeStruct((S, W), BF16), jax.ShapeDtypeStruct((S, W), BF16), jax.ShapeDtypeStruct((N_PAIRS, 2, S), F32)],
        scratch_shapes=[pltpu.VMEM((t, W), F32), pltpu.VMEM((t, W), F32)],
        compiler_params=_params("parallel", "arbitrary"), name=name)(qkv, qkv, qkv, kbias, do, o, lse)


def _combine(name, os_, lses, tm=256):
    S, W = os_[0].shape
    G = len(os_)

    def body(*refs):
        o_refs, l_refs = refs[:G], refs[G:2 * G]
        o_ref, lse_ref = refs[2 * G:]
        ls = [r[...] for r in l_refs]
        m = functools.reduce(jnp.maximum, ls)
        ws = [jnp.exp(l - m) for l in ls]
        den = functools.reduce(jnp.add, ws)
        num = functools.reduce(jnp.add, [w * r[...] for w, r in zip(ws, o_refs)])
        o_ref[...] = (num / den).astype(BF16)
        lse_ref[...] = m + jnp.log(den)

    row = pl.BlockSpec((tm, W), lambda i: (i, 0))
    return pl.pallas_call(
        body, grid=(S // tm,), in_specs=[row] * (2 * G), out_specs=[row, row],
        out_shape=[jax.ShapeDtypeStruct((S, W), BF16), jax.ShapeDtypeStruct((S, W), F32)],
        compiler_params=_params("parallel"), name=name)(*os_, *lses)


def _tri_matmul(tri, x):
    hi, mid, lo = _split3(x)
    return _dot(tri, hi, NN) + _dot(tri, mid, NN) + _dot(tri, lo, NN)


def _split3(x):
    hi = x.astype(BF16)
    r1 = x - hi.astype(F32)
    mid = r1.astype(BF16)
    return hi, mid, (r1 - mid.astype(F32)).astype(BF16)


def _gate_fwd(name, z, bf, tb=512):
    S = z.shape[0]

    def body(z_ref, b_ref, kb_ref, carry):
        @pl.when(pl.program_id(0) == 0)
        def _():
            carry[...] = jnp.zeros_like(carry)

        lf = jax.nn.log_sigmoid(z_ref[...] + b_ref[...])
        ri = lax.broadcasted_iota(jnp.int32, (tb, tb), 0)
        ci = lax.broadcasted_iota(jnp.int32, (tb, tb), 1)
        tri = (ci <= ri).astype(BF16)
        c = _tri_matmul(tri, lf) + carry[...]
        carry[...] = c[tb - 1:tb, :]
        head = lax.broadcasted_iota(jnp.int32, (LANES, D_MODEL), 0)
        col = lax.broadcasted_iota(jnp.int32, (LANES, D_MODEL), 1)
        base = (head >> 1) * LANES + jnp.where((head & 1) == 0, HEAD_DIM, 0)
        kb = jnp.zeros((tb, D_MODEL), F32)
        for i, piece in enumerate(_split3(-c)):
            place = ((col == base + i) & (head < N_HEADS)).astype(BF16)
            kb = kb + _dot(piece, place, NN)
        kb_ref[...] = kb.astype(BF16)

    row = pl.BlockSpec((tb, LANES), lambda i: (i, 0))
    return pl.pallas_call(
        body, grid=(S // tb,), in_specs=[row, pl.BlockSpec((1, LANES), lambda i: (0, 0))],
        out_specs=pl.BlockSpec((tb, D_MODEL), lambda i: (i, 0)), out_shape=jax.ShapeDtypeStruct((S, D_MODEL), BF16),
        scratch_shapes=[pltpu.VMEM((1, LANES), F32)], compiler_params=_params("arbitrary"), name=name)(z, bf)


def _gate_bwd(name, dc, z, bf, tb=512):
    S = z.shape[0]
    nb = S // tb

    def body(dc_ref, z_ref, b_ref, dz_ref, db_ref, carry):
        @pl.when(pl.program_id(0) == 0)
        def _():
            carry[...] = jnp.zeros_like(carry)
            db_ref[...] = jnp.zeros_like(db_ref)

        ri = lax.broadcasted_iota(jnp.int32, (tb, tb), 0)
        ci = lax.broadcasted_iota(jnp.int32, (tb, tb), 1)
        tri = (ci >= ri).astype(BF16)
        dlf = _tri_matmul(tri, dc_ref[...]) + carry[...]
        carry[...] = dlf[0:1, :]
        dz = dlf * jax.nn.sigmoid(-(z_ref[...] + b_ref[...]))
        dz_ref[...] = dz
        db_ref[...] += jnp.sum(dz, axis=0, keepdims=True)

    row = pl.BlockSpec((tb, LANES), lambda i: (nb - 1 - i, 0))
    vec = pl.BlockSpec((1, LANES), lambda i: (0, 0))
    return pl.pallas_call(
        body, grid=(nb,), in_specs=[row, row, vec], out_specs=[row, vec],
        out_shape=[jax.ShapeDtypeStruct((S, LANES), F32), jax.ShapeDtypeStruct((1, LANES), F32)],
        scratch_shapes=[pltpu.VMEM((1, LANES), F32)], compiler_params=_params("arbitrary"), name=name)(dc, z, bf)


def _ffn_gu(name, n, wgu, tm=512):
    S, D = n.shape
    nb = N_DEV // 2

    def body(n_ref, wg_ref, wu_ref, gu_ref, act_ref):
        x = n_ref[...]
        g = _dot(x, wg_ref[...], NN)
        u = _dot(x, wu_ref[...], NN)
        gu_ref[0] = g.astype(BF16)
        gu_ref[1] = u.astype(BF16)
        act_ref[...] = (g * jax.nn.sigmoid(g) * u).astype(BF16)

    return pl.pallas_call(
        body, grid=(S // tm, nb),
        in_specs=[pl.BlockSpec((tm, D), lambda i, j: (i, 0)), pl.BlockSpec((None, D, FF_BLK), lambda i, j: (j, 0, 0)),
                  pl.BlockSpec((None, D, FF_BLK), lambda i, j: (j + nb, 0, 0))],
        out_specs=[pl.BlockSpec((2, None, tm, FF_BLK), lambda i, j: (0, j, i, 0)),
                   pl.BlockSpec((None, tm, FF_BLK), lambda i, j: (j, i, 0))],
        out_shape=[jax.ShapeDtypeStruct((2, nb, S, FF_BLK), BF16), jax.ShapeDtypeStruct((nb, S, FF_BLK), BF16)],
        compiler_params=_params("parallel", "parallel"), name=name)(n, wgu, wgu)


def _ffn_down(name, act, wd, resid, tm=512):
    nb, S, _ = act.shape
    D = wd.shape[1]

    def epilogue(acc, ex, outs, j):
        outs[0][...] = acc + ex[0][...]

    o_spec = pl.BlockSpec((tm, D), lambda i, j, k: (i, 0))
    return _mm_call(name, (S // tm, 1, nb), act, pl.BlockSpec((None, tm, FF_BLK), lambda i, j, k: (k, i, 0)),
                    wd, pl.BlockSpec((FF_BLK, D), lambda i, j, k: (k, 0)), NN,
                    [jax.ShapeDtypeStruct((S, D), F32)], [o_spec], (tm, D), epilogue, (resid,), (o_spec,))[0]


def _ffn_dact(name, dh, wd, gu, tm=512):
    S, D = dh.shape
    nb = N_DEV // 2

    def epilogue(acc, ex, outs, j):
        g = ex[0][0].astype(F32)
        u = ex[0][1].astype(F32)
        sig = jax.nn.sigmoid(g)
        outs[0][0] = (acc * u * (sig * (1.0 + g * (1.0 - sig)))).astype(BF16)
        outs[0][1] = (acc * (g * sig)).astype(BF16)

    gu_spec = pl.BlockSpec((2, None, tm, FF_BLK), lambda i, j, k: (0, j, i, 0))
    return _mm_call(name, (S // tm, nb, 1), dh, pl.BlockSpec((tm, D), lambda i, j, k: (i, 0)),
                    wd, pl.BlockSpec((FF_BLK, D), lambda i, j, k: (j, 0)), NT,
                    [jax.ShapeDtypeStruct((2, nb, S, FF_BLK), BF16)], [gu_spec], (tm, FF_BLK), epilogue, (gu,), (gu_spec,))[0]


def _ffn_dwgu(name, n, dgu, tm=512, tk=1024):
    S, D = n.shape
    dgu8 = dgu.reshape(N_DEV, S, FF_BLK)
    return _mm_call(name, (N_DEV, D // tm, S // tk), n, pl.BlockSpec((tk, tm), lambda d, i, k: (k, i)),
                    dgu8, pl.BlockSpec((None, tk, FF_BLK), lambda d, i, k: (d, k, 0)), TN,
                    [jax.ShapeDtypeStruct((N_DEV, D, FF_BLK), F32)],
                    [pl.BlockSpec((None, tm, FF_BLK), lambda d, i, k: (d, i, 0))], (tm, FF_BLK))[0]


def _ffn_dwd(name, act, dh, tk=1024):
    nb, S, _ = act.shape
    D = dh.shape[1]
    out = _mm_call(name, (nb, 1, S // tk), act, pl.BlockSpec((None, tk, FF_BLK), lambda b, j, k: (b, k, 0)),
                   dh, pl.BlockSpec((tk, D), lambda b, j, k: (k, 0)), TN,
                   [jax.ShapeDtypeStruct((nb, FF_BLK, D), F32)],
                   [pl.BlockSpec((None, FF_BLK, D), lambda b, j, k: (b, 0, 0))], (FF_BLK, D))[0]
    return out.reshape(nb * FF_BLK, D)


def _ffn_dn(name, dgu, wgu, tm=512):
    S = dgu.shape[2]
    D = wgu.shape[1]
    dgu8 = dgu.reshape(N_DEV, S, FF_BLK)
    return _mm_call(name, (S // tm, 1, N_DEV), dgu8, pl.BlockSpec((None, tm, FF_BLK), lambda i, j, k: (k, i, 0)),
                    wgu, pl.BlockSpec((None, D, FF_BLK), lambda i, j, k: (k, 0, 0)), NT,
                    [jax.ShapeDtypeStruct((S, D), F32)], [pl.BlockSpec((tm, D), lambda i, j, k: (i, 0))], (tm, D))[0]


def _adamw(name, parts, w, m, v, tr):
    rows, cols = w.shape
    n_parts = len(parts)
    c1 = 1.0 - ADAM_B1 ** ADAM_STEP
    c2 = 1.0 - ADAM_B2 ** ADAM_STEP

    def body(*refs):
        p_refs = refs[:n_parts]
        w_ref, m_ref, v_ref, g_ref, d_ref, nm_ref, nv_ref = refs[n_parts:]
        g = p_refs[0][...].astype(F32)
        for r in p_refs[1:]:
            g = g + r[...].astype(F32)
        mm = ADAM_B1 * m_ref[...] + (1.0 - ADAM_B1) * g
        vv = ADAM_B2 * v_ref[...] + (1.0 - ADAM_B2) * (g * g)
        g_ref[...] = g
        nm_ref[...] = mm
        nv_ref[...] = vv
        d_ref[...] = -ADAM_LR * ((mm / c1) / (jnp.sqrt(vv / c2) + ADAM_EPS) + ADAM_WD * w_ref[...])

    blk = pl.BlockSpec((tr, cols), lambda i: (i, 0))
    out = jax.ShapeDtypeStruct((rows, cols), F32)
    return pl.pallas_call(
        body, grid=(rows // tr,), in_specs=[blk] * (n_parts + 3), out_specs=[blk] * 4, out_shape=[out] * 4,
        compiler_params=_params("parallel"), name=name)(*parts, w, m, v)


def _position():
    return lax.axis_index("x"), lax.axis_index("y"), lax.axis_index("c")


def _all_gather(name, block):
    shape, dtype = block.shape, block.dtype

    def body(x_ref, out_ref, send_sems, recv_sems, local_sem):
        x, y, c = _position()
        me, sibling = (x, y, c), (x, y, 1 - c)
        chips = [(1 - x, y), (x, 1 - y), (1 - x, 1 - y)]

        def slot(px, py, pc):
            return out_ref.at[4 * px + 2 * py + pc]

        def copy(k, blk, to, src=None):
            return pltpu.make_async_remote_copy(
                src_ref=slot(*blk) if src is None else src, dst_ref=slot(*blk), send_sem=send_sems.at[k],
                recv_sem=recv_sems.at[k], device_id=to, device_id_type=MESH)

        mine = pltpu.make_async_copy(x_ref, slot(*me), local_sem)
        mine.start()
        first = [copy(0, me, sibling, src=x_ref)]
        first += [copy(1 + j, me, (*chip, c), src=x_ref) for j, chip in enumerate(chips)]
        for cp in first:
            cp.start()
        passed = [copy(4 + j, (*chip, c), sibling) for j, chip in enumerate(chips)]
        for j, chip in enumerate(chips):
            copy(1 + j, (*chip, c), me).wait_recv()
            passed[j].start()
        copy(0, sibling, me).wait_recv()
        for j, chip in enumerate(chips):
            copy(4 + j, (*chip, 1 - c), me).wait_recv()
        for cp in first + passed:
            cp.wait_send()
        mine.wait()

    hbm = pl.BlockSpec(memory_space=pltpu.HBM)
    return pl.pallas_call(
        body, out_shape=jax.ShapeDtypeStruct((N_DEV,) + shape, dtype), in_specs=[hbm], out_specs=hbm,
        scratch_shapes=[pltpu.SemaphoreType.DMA((7,)), pltpu.SemaphoreType.DMA((7,)), pltpu.SemaphoreType.DMA],
        name=name)(block)


def _swap_halves(name, packed):
    _, _, rows, cols = packed.shape

    def body(p_ref, got_ref, send_sem, recv_sem):
        x, y, c = _position()
        cp = pltpu.make_async_remote_copy(
            src_ref=p_ref.at[:, 1 - c], dst_ref=got_ref, send_sem=send_sem, recv_sem=recv_sem,
            device_id=(x, y, 1 - c), device_id_type=MESH)
        cp.start()
        cp.wait()

    hbm = pl.BlockSpec(memory_space=pltpu.HBM)
    return pl.pallas_call(
        body, out_shape=jax.ShapeDtypeStruct((4, rows, cols), packed.dtype), in_specs=[hbm], out_specs=hbm,
        scratch_shapes=[pltpu.SemaphoreType.DMA, pltpu.SemaphoreType.DMA], name=name)(packed)


def _pair_sum(name, packed, got, core, tr=560):
    _, _, rows, cols = packed.shape

    def body(c_ref, a_ref, b_ref, o_ref):
        o_ref[...] = (a_ref[...].astype(F32) + b_ref[...].astype(F32)).astype(o_ref.dtype)

    grid_spec = pltpu.PrefetchScalarGridSpec(
        num_scalar_prefetch=1, grid=(4, rows // tr),
        in_specs=[pl.BlockSpec((None, None, tr, cols), lambda q, i, c: (q, c[0], i, 0)),
                  pl.BlockSpec((None, tr, cols), lambda q, i, c: (q, i, 0))],
        out_specs=pl.BlockSpec((None, tr, cols), lambda q, i, c: (q, i, 0)))
    return pl.pallas_call(
        body, grid_spec=grid_spec, out_shape=jax.ShapeDtypeStruct((4, rows, cols), packed.dtype),
        compiler_params=_params("parallel", "parallel"), name=name)(core, packed, got)


def _exchange_chips(name, sums):
    _, rows, cols = sums.shape

    def body(s_ref, got_ref, send_sems, recv_sems):
        x, y, c = _position()
        chips = [(1 - x, y), (x, 1 - y), (1 - x, 1 - y)]
        cps = [pltpu.make_async_remote_copy(
            src_ref=s_ref.at[2 * px + py], dst_ref=got_ref.at[k], send_sem=send_sems.at[k], recv_sem=recv_sems.at[k],
            device_id=(px, py, c), device_id_type=MESH) for k, (px, py) in enumerate(chips)]
        for cp in cps:
            cp.start()
        for cp in cps:
            cp.wait()

    hbm = pl.BlockSpec(memory_space=pltpu.HBM)
    return pl.pallas_call(
        body, out_shape=jax.ShapeDtypeStruct((3, rows, cols), sums.dtype), in_specs=[hbm], out_specs=hbm,
        scratch_shapes=[pltpu.SemaphoreType.DMA((3,)), pltpu.SemaphoreType.DMA((3,))], name=name)(sums)


PACK = (("a_w_in", 1152, 1152), ("a_w_out", 128, 128), ("b_w_in", 386, 400), ("b_w_out", 128, 128),
        ("gu0", 704, 704), ("gu1", 704, 704), ("dn0", 352, 352), ("dn1", 352, 352))
PACK_ROWS = sum(p[2] for p in PACK)
PACK_OFF = {p[0]: sum(q[2] for q in PACK[:i]) for i, p in enumerate(PACK)}
PACK_LEN = {p[0]: p[1] for p in PACK}
ADAM_TILE = {"a_w_in": 128, "a_w_out": 128, "b_w_in": 386, "b_w_out": 128, "gu0": 176, "gu1": 176, "dn0": 176, "dn1": 176}
B_IN = 3 * D_MODEL + N_HEADS
B_IN_PAD = 3 * D_MODEL + LANES


def _pack_rows(pieces, dtype):
    out = []
    for name, rows, padded in PACK:
        a = pieces[name].astype(dtype)
        if padded != rows:
            a = jnp.pad(a, [(0, 0)] * (a.ndim - 2) + [(0, padded - rows), (0, 0)])
        out.append(a)
    return jnp.concatenate(out, axis=-2)


def _unpack(packed, name):
    off = PACK_OFF[name]
    return packed[..., off:off + PACK_LEN[name], :]


def _gathered_weights(wg):
    blocks = lambda name, shape: _unpack(wg, name).reshape((N_DEV,) + shape)
    a_w_in = blocks("a_w_in", (D_MODEL, 1152)).transpose(1, 0, 2).reshape(D_MODEL, 9 * D_MODEL)
    b_w_in = blocks("b_w_in", (D_MODEL, 386)).transpose(1, 0, 2).reshape(D_MODEL, B_IN)
    b_gate = jnp.pad(b_w_in[:, 3 * D_MODEL:], ((0, 0), (0, LANES - N_HEADS)))
    return dict(
        a_w_in=a_w_in, a_w_out=_unpack(wg, "a_w_out").reshape(D_MODEL, D_MODEL),
        b_w_qkv=b_w_in[:, :3 * D_MODEL], b_w_gate=b_gate, b_w_cat=jnp.concatenate([b_w_in[:, :3 * D_MODEL], b_gate], axis=1),
        b_w_out=_unpack(wg, "b_w_out").reshape(D_MODEL, D_MODEL),
        gu=[blocks("gu%d" % l, (D_MODEL, FF_BLK)) for l in range(2)],
        dn=[_unpack(wg, "dn%d" % l).reshape(D_FF, D_MODEL) for l in range(2)])


def _local_step(h0, target, W, norms):
    S = h0.shape[0]
    tabs = _rope_tables(S)

    n0 = _rms_fwd("rms_a", h0, norms["a_norm"])
    proj_a = _a_proj("proj_a", n0, W["a_w_in"], tabs)
    groups = []
    for g, (window, dil) in enumerate(DILATED_PATTERNS):
        L = S // dil
        view = proj_a.reshape(L, dil * 9 * D_MODEL)
        cols = [(lambda r, g=g, tq=tq: r * 9 + g * 3 + tq) for tq in range(3)]
        groups.append((g, dil, L, view, cols))
    outs, lses = [], []
    for g, dil, L, view, cols in groups:
        o_g, lse_g = _dil_fwd("dil_fwd%d" % g, view, *cols, dil, L)
        outs.append(o_g.reshape(S, D_MODEL))
        lses.append(lse_g.reshape(S, D_MODEL))
    o_a, lse_a = _combine("dil_combine", outs, lses)
    h1 = _matmul("out_a", o_a, W["a_w_out"], "nn", F32, 512, 1024, 1024, resid=h0)

    n1 = _rms_fwd("rms_f0", h1, norms["ffn_norm0"])
    gu0, act0 = _ffn_gu("gu_f0", n1, W["gu"][0])
    h2 = _ffn_down("down_f0", act0, W["dn"][0], h1)

    n2 = _rms_fwd("rms_b", h2, norms["b_norm"])
    qkv = _matmul("proj_b", n2, W["b_w_qkv"], "nn", BF16, 512, 1024, 1024, col0_scale=SOFTMAX_SCALE)
    z = _matmul("gate_b", n2, W["b_w_gate"], "nn", F32, 512, LANES, 1024)
    kbias = _gate_fwd("gate_cumsum", z, norms["b_f"])
    tf = min(S, 512)
    o_b, lse_b = _fox_fwd("fox_fwd", qkv, kbias, tf)
    h3 = _matmul("out_b", o_b, W["b_w_out"], "nn", F32, 512, 1024, 1024, resid=h2)

    n3 = _rms_fwd("rms_f1", h3, norms["ffn_norm1"])
    gu1, act1 = _ffn_gu("gu_f1", n3, W["gu"][1])
    h4 = _ffn_down("down_f1", act1, W["dn"][1], h3)

    dh4, d_final, loss = _loss_head("loss_head", h4, norms["final_norm"], target)

    def ffn_bwd(tag, dh_out, h_in, n, gu, act, wgu, wd, gain):
        dgu = _ffn_dact("dact_" + tag, dh_out, wd, gu)
        dwd = _ffn_dwd("dwd_" + tag, act, dh_out)
        dwgu = _ffn_dwgu("dwgu_" + tag, n, dgu)
        dn = _ffn_dn("dn_" + tag, dgu, wgu)
        dh_in, dgain = _rms_bwd("rmsb_" + tag, dn, h_in, gain, dh_out)
        return dh_in, dgain, dwgu, dwd

    dh3, d_ffn1, dwgu1, dwd1 = ffn_bwd("f1", dh4, h3, n3, gu1, act1, W["gu"][1], W["dn"][1], norms["ffn_norm1"])

    do_b = _matmul("dout_b", dh3, W["b_w_out"], "nt", BF16, 512, 1024, 1024)
    dw_b_out = _matmul("dwout_b", o_b, dh3, "tn", F32, 512, 1024, 1024)
    dq_b, ds_rowsum = _fox_dq("fox_dq", qkv, kbias, do_b, o_b, lse_b, tf)
    dk_b, dv_b, ds_colsum = _fox_dkv("fox_dkv", qkv, kbias, do_b, o_b, lse_b, tf)
    dc = ds_rowsum[:, ::HEAD_DIM] - ds_colsum.reshape(N_HEADS, S).T
    dz, d_bf = _gate_bwd("gate_bwd", jnp.pad(dc, ((0, 0), (0, LANES - N_HEADS))), z, norms["b_f"])
    dproj_b = jnp.concatenate([dq_b, dk_b, dv_b, dz.astype(BF16)], axis=1)
    dw_b_in = _matmul("dwin_b", n2, dproj_b, "tn", F32, 512, B_IN_PAD // 5, 1024)
    dn2 = _matmul("dn_b", dproj_b, W["b_w_cat"], "nt", F32, 512, 1024, B_IN_PAD // 5)
    dh2, d_bnorm = _rms_bwd("rmsb_b", dn2, h2, norms["b_norm"], dh3)

    dh1, d_ffn0, dwgu0, dwd0 = ffn_bwd("f0", dh2, h1, n1, gu0, act0, W["gu"][0], W["dn"][0], norms["ffn_norm0"])

    do_a = _matmul("dout_a", dh1, W["a_w_out"], "nt", BF16, 512, 1024, 1024)
    dw_a_out = _matmul("dwout_a", o_a, dh1, "tn", F32, 512, 1024, 1024)
    pieces = []
    for g, dil, L, view, cols in groups:
        sv = lambda a: a.reshape(L, dil * a.shape[1])
        args = (view, sv(do_a), sv(o_a), sv(lse_a), tuple(sv(tb) for tb in tabs), *cols, dil, L)
        dq_g = _dil_dq("dil_dq%d" % g, *args)
        dk_g, dv_g = _dil_dkv("dil_dkv%d" % g, *args)
        pieces += [a.reshape(S, D_MODEL) for a in (dq_g, dk_g, dv_g)]
    dproj_a = jnp.concatenate(pieces, axis=1)
    dw_a_in = _mm_call("dwin_a", (N_DEV, D_MODEL // 512, S // 1024), n0, pl.BlockSpec((1024, 512), lambda d, i, k: (k, i)),
                       dproj_a, pl.BlockSpec((1024, 1152), lambda d, i, k: (k, d)), TN,
                       [jax.ShapeDtypeStruct((N_DEV, D_MODEL, 1152), F32)],
                       [pl.BlockSpec((None, 512, 1152), lambda d, i, k: (d, i, 0))], (512, 1152))[0]
    dn0 = _matmul("dn_a", dproj_a, W["a_w_in"], "nt", F32, 512, 1024, 1024)
    dx, d_anorm = _rms_bwd("rmsb_a", dn0, h0, norms["a_norm"], dh1)

    by_dest = lambda a, cols: a.reshape(D_MODEL, N_DEV, cols).transpose(1, 0, 2)
    wgrads = {
        "a_w_in": dw_a_in.reshape(N_DEV, 1152, D_MODEL),
        "a_w_out": dw_a_out.reshape(N_DEV, 128, D_MODEL),
        "b_w_in": by_dest(dw_b_in[:, :B_IN], 386).reshape(N_DEV, 386, D_MODEL),
        "b_w_out": dw_b_out.reshape(N_DEV, 128, D_MODEL),
        "gu0": dwgu0.reshape(N_DEV, FF_BLK, D_MODEL), "gu1": dwgu1.reshape(N_DEV, FF_BLK, D_MODEL),
        "dn0": dwd0.reshape(N_DEV, 352, D_MODEL), "dn1": dwd1.reshape(N_DEV, 352, D_MODEL),
    }
    vgrads = dict(a_norm=d_anorm, ffn_norm0=d_ffn0, ffn_norm1=d_ffn1, final_norm=d_final, b_norm=d_bnorm, b_f=d_bf)
    return loss, dx, wgrads, vgrads


def kernel(x, a_norm, a_w_in, a_w_out, b_norm, b_w_in, b_f, b_w_out, ffn_norm, ffn_w_gu, ffn_w_down, final_norm, loss_target, m_a_norm, m_a_w_in, m_a_w_out, m_b_norm, m_b_w_in, m_b_f, m_b_w_out, m_ffn_norm, m_ffn_w_gu, m_ffn_w_down, m_final_norm, v_a_norm, v_a_w_in, v_a_w_out, v_b_norm, v_b_w_in, v_b_f, v_b_w_out, v_ffn_norm, v_ffn_w_gu, v_ffn_w_down, v_final_norm):
    S = x.shape[1]
    xi, yi, ci = _position()
    dev = 4 * xi + 2 * yi + ci
    rows = lambda a: a.reshape(-1, D_MODEL)

    def shards(a_in, a_out, b_in, b_out, gu, dn):
        return {"a_w_in": rows(a_in), "a_w_out": rows(a_out), "b_w_in": rows(b_in), "b_w_out": rows(b_out),
                "gu0": rows(gu[0]), "gu1": rows(gu[1]), "dn0": rows(dn[0]), "dn1": rows(dn[1])}

    w_sh = shards(a_w_in, a_w_out, b_w_in, b_w_out, ffn_w_gu, ffn_w_down)
    m_sh = shards(m_a_w_in, m_a_w_out, m_b_w_in, m_b_w_out, m_ffn_w_gu, m_ffn_w_down)
    v_sh = shards(v_a_w_in, v_a_w_out, v_b_w_in, v_b_w_out, v_ffn_w_gu, v_ffn_w_down)

    wg = _all_gather("gather_weights", _pack_rows(w_sh, BF16))
    b_norm_full = _all_gather("gather_b_norm", jnp.pad(b_norm, ((0, 7), (0, 0)))).reshape(N_DEV, 8, LANES)[:, 0].reshape(1, D_MODEL)
    W = _gathered_weights(wg)
    bf_pad = jnp.pad(b_f, ((0, 0), (0, LANES - N_HEADS)))
    norms = dict(a_norm=a_norm[0], ffn_norm0=ffn_norm[0], ffn_norm1=ffn_norm[1], final_norm=final_norm,
                 b_norm=b_norm_full[0], b_f=bf_pad)

    loss, dx, wgrads, vgrads = _local_step(x.reshape(S, D_MODEL), loss_target.reshape(S, D_MODEL), W, norms)

    packed = _pack_rows(wgrads, BF16).reshape(4, 2, PACK_ROWS, D_MODEL)
    got = _swap_halves("rs_sibling", packed)
    sums = _pair_sum("rs_pair_sum", packed, got, ci.reshape(1).astype(jnp.int32))
    others = _exchange_chips("rs_chips", sums)
    mine = lax.dynamic_index_in_dim(sums, 2 * xi + yi, axis=0, keepdims=False)

    outs = {}
    for name, n_rows, _ in PACK:
        parts = [_unpack(mine, name)] + [_unpack(others[k], name) for k in range(3)]
        outs[name] = _adamw("adamw_" + name, parts, w_sh[name], m_sh[name], v_sh[name], ADAM_TILE[name])

    misc = jnp.concatenate([vgrads["b_f"][:, :N_HEADS], loss[:, :1], jnp.zeros((1, D_MODEL - N_HEADS - 1), F32)], axis=1)
    small = jnp.concatenate([vgrads["a_norm"], vgrads["ffn_norm0"], vgrads["ffn_norm1"], vgrads["final_norm"],
                             vgrads["b_norm"], misc, jnp.zeros((2, D_MODEL), F32)], axis=0)
    small_all = _all_gather("gather_small", small)
    pad_vec = lambda a: jnp.pad(a, ((0, 0), (0, D_MODEL - a.shape[1])))

    def small_pack(an, fn, fin, bf):
        return jnp.concatenate([an, fn, fin.reshape(1, D_MODEL), jnp.zeros((1, D_MODEL), F32), pad_vec(bf),
                                jnp.zeros((2, D_MODEL), F32)], axis=0)

    sg, sd, sm, sv = _adamw("adamw_small", [small_all[d] for d in range(N_DEV)], small_pack(a_norm, ffn_norm, final_norm, b_f),
                            small_pack(m_a_norm, m_ffn_norm, m_final_norm, m_b_f),
                            small_pack(v_a_norm, v_ffn_norm, v_final_norm, v_b_f), 8)
    g_bn = lax.dynamic_slice(sg[4:5], (0, dev * LANES), (1, LANES))
    bn = _adamw("adamw_b_norm", [g_bn], b_norm, m_b_norm, v_b_norm, 1)

    def tree(i):
        full = lambda name, ref: outs[name][i].reshape(ref.shape)
        sml = (sg, sd, sm, sv)[i]
        return dict(
            a_norm=sml[0:1], a_w_in=full("a_w_in", a_w_in), a_w_out=full("a_w_out", a_w_out), b_norm=bn[i],
            b_w_in=full("b_w_in", b_w_in), b_f=sml[5:6, :N_HEADS], b_w_out=full("b_w_out", b_w_out), ffn_norm=sml[1:3],
            ffn_w_gu=jnp.stack([outs["gu0"][i], outs["gu1"][i]]).reshape(ffn_w_gu.shape),
            ffn_w_down=jnp.stack([outs["dn0"][i], outs["dn1"][i]]).reshape(ffn_w_down.shape), final_norm=sml[3])

    order = ("a_norm", "a_w_in", "a_w_out", "b_norm", "b_w_in", "b_f", "b_w_out", "ffn_norm", "ffn_w_gu", "ffn_w_down", "final_norm")
    result = [sg[5, N_HEADS], dx.reshape(x.shape)]
    for i in range(4):
        t = tree(i)
        result += [t[n] for n in order]
    return tuple(result)
```

```python
import functools

import jax
import jax.numpy as jnp
from jax import lax
from jax.experimental import pallas as pl
from jax.experimental.pallas import tpu as pltpu

F32 = jnp.float32
BF16 = jnp.bfloat16

D_MODEL = 1024
N_HEADS = 16
HEAD_DIM = 64
N_PAIRS = N_HEADS // 2
LANES = 128
DILATED_PATTERNS = ((128, 1), (512, 4), (2048, 16))
BAND_STEPS = 128
ROT_DIM = HEAD_DIM // 4
ROPE_THETA = 500000.0
D_FF = 2816
RMS_EPS = 1e-6
NEG_INF = -1e30
SOFTMAX_SCALE = HEAD_DIM ** -0.5
N_DEV = 8
FF_BLK = 2 * D_FF // N_DEV
ADAM_LR, ADAM_B1, ADAM_B2, ADAM_EPS, ADAM_WD, ADAM_STEP = 0.001, 0.9, 0.999, 1e-08, 0.01, 10
VMEM_LIMIT = 52 * 1024 * 1024
TM = 1024
MESH = pl.DeviceIdType.MESH

NN = (((1,), (0,)), ((), ()))
NT = (((1,), (1,)), ((), ()))
TN = (((0,), (0,)), ((), ()))


def _params(*sem):
    return pltpu.CompilerParams(dimension_semantics=sem, vmem_limit_bytes=VMEM_LIMIT)


def _dot(a, b, dims):
    return lax.dot_general(a, b, dims, preferred_element_type=F32)


def _mm_call(name, grid, a, a_spec, b, b_spec, dims, out_shapes, out_specs, acc_shape, epilogue=None,
             extras=(), extra_specs=(), col_axis=1):
    nk = grid[2]
    n_extra = len(extras)
    n_out = len(out_shapes)

    def finish(res, ex, outs, j):
        if epilogue is None:
            outs[0][...] = res.astype(outs[0].dtype)
        else:
            epilogue(res, ex, outs, j)

    def body(*refs):
        a_ref, b_ref = refs[0], refs[1]
        ex = refs[2:2 + n_extra]
        outs = refs[2 + n_extra:2 + n_extra + n_out]
        j, k = pl.program_id(col_axis), pl.program_id(2)
        part = _dot(a_ref[...].astype(BF16), b_ref[...].astype(BF16), dims)
        if nk == 1:
            finish(part, ex, outs, j)
            return
        acc = refs[-1]

        @pl.when(k == 0)
        def _():
            acc[...] = part

        @pl.when((k > 0) & (k < nk - 1))
        def _():
            acc[...] += part

        @pl.when(k == nk - 1)
        def _():
            finish(acc[...] + part, ex, outs, j)

    return pl.pallas_call(
        body, grid=grid, in_specs=[a_spec, b_spec, *extra_specs], out_specs=out_specs, out_shape=out_shapes,
        scratch_shapes=[] if nk == 1 else [pltpu.VMEM(acc_shape, F32)],
        compiler_params=_params("parallel", "parallel", "arbitrary"), name=name)(a, b, *extras)


def _matmul(name, a, b, mode, out_dtype, tm, tn, tk, resid=None, col0_scale=None):
    if mode == "nn":
        (M, K), N = a.shape, b.shape[1]
        a_spec = pl.BlockSpec((tm, tk), lambda j, i, k: (i, k))
        b_spec = pl.BlockSpec((tk, tn), lambda j, i, k: (k, j))
        dims = NN
    elif mode == "nt":
        (M, K), N = a.shape, b.shape[0]
        a_spec = pl.BlockSpec((tm, tk), lambda j, i, k: (i, k))
        b_spec = pl.BlockSpec((tn, tk), lambda j, i, k: (j, k))
        dims = NT
    else:
        (K, M), N = a.shape, b.shape[1]
        a_spec = pl.BlockSpec((tk, tm), lambda j, i, k: (k, i))
        b_spec = pl.BlockSpec((tk, tn), lambda j, i, k: (k, j))
        dims = TN
    assert M % tm == 0 and N % tn == 0 and K % tk == 0, (name, M, N, K, tm, tn, tk)
    o_spec = pl.BlockSpec((tm, tn), lambda j, i, k: (i, j))
    extras, extra_specs, epilogue = (), (), None
    if resid is not None:
        extras, extra_specs = (resid,), (o_spec,)

        def epilogue(acc, ex, outs, j):
            outs[0][...] = (acc + ex[0][...]).astype(outs[0].dtype)

    elif col0_scale is not None:

        def epilogue(acc, ex, outs, j):
            outs[0][...] = (acc * jnp.where(j == 0, col0_scale, 1.0)).astype(outs[0].dtype)

    return _mm_call(name, (N // tn, M // tm, K // tk), a, a_spec, b, b_spec, dims,
                    [jax.ShapeDtypeStruct((M, N), out_dtype)], [o_spec], (tm, tn), epilogue, extras, extra_specs, col_axis=0)[0]


def _rms_fwd(name, h, gain, tm=512):
    S, D = h.shape

    def body(h_ref, g_ref, n_ref):
        x = h_ref[...]
        rstd = lax.rsqrt(jnp.mean(x * x, axis=-1, keepdims=True) + RMS_EPS)
        n_ref[...] = (x * rstd * g_ref[...]).astype(BF16)

    return pl.pallas_call(
        body, grid=(S // tm,), in_specs=[pl.BlockSpec((tm, D), lambda i: (i, 0)), pl.BlockSpec((1, D), lambda i: (0, 0))],
        out_specs=pl.BlockSpec((tm, D), lambda i: (i, 0)), out_shape=jax.ShapeDtypeStruct((S, D), BF16),
        compiler_params=_params("parallel"), name=name)(h, gain.reshape(1, D))


def _rms_bwd(name, dn, h, gain, dres, tm=512):
    S, D = h.shape

    def body(dn_ref, h_ref, g_ref, r_ref, dh_ref, dg_ref):
        x = h_ref[...]
        rstd = lax.rsqrt(jnp.mean(x * x, axis=-1, keepdims=True) + RMS_EPS)
        xhat = x * rstd
        d = dn_ref[...]
        dxhat = d * g_ref[...]
        dh_ref[...] = rstd * (dxhat - xhat * jnp.mean(dxhat * xhat, axis=-1, keepdims=True)) + r_ref[...]

        @pl.when(pl.program_id(0) == 0)
        def _():
            dg_ref[...] = jnp.zeros_like(dg_ref)

        dg_ref[...] += jnp.sum(d * xhat, axis=0, keepdims=True)

    row = pl.BlockSpec((tm, D), lambda i: (i, 0))
    vec = pl.BlockSpec((1, D), lambda i: (0, 0))
    return pl.pallas_call(
        body, grid=(S // tm,), in_specs=[row, row, vec, row], out_specs=[row, vec],
        out_shape=[jax.ShapeDtypeStruct((S, D), F32), jax.ShapeDtypeStruct((1, D), F32)],
        compiler_params=_params("arbitrary"), name=name)(dn, h, gain.reshape(1, D), dres)


def _loss_head(name, h, gain, target, tm=512):
    S, D = h.shape

    def body(h_ref, g_ref, t_ref, dh_ref, dg_ref, loss_ref):
        x = h_ref[...]
        rstd = lax.rsqrt(jnp.mean(x * x, axis=-1, keepdims=True) + RMS_EPS)
        xhat = x * rstd
        err = xhat * g_ref[...] - t_ref[...]
        dy = err * (1.0 / D)
        dxhat = dy * g_ref[...]
        dh_ref[...] = rstd * (dxhat - xhat * jnp.mean(dxhat * xhat, axis=-1, keepdims=True))

        @pl.when(pl.program_id(0) == 0)
        def _():
            dg_ref[...] = jnp.zeros_like(dg_ref)
            loss_ref[...] = jnp.zeros_like(loss_ref)

        dg_ref[...] += jnp.sum(dy * xhat, axis=0, keepdims=True)
        part = 0.5 * jnp.sum(jnp.mean(err * err, axis=-1, keepdims=True), axis=0, keepdims=True)
        loss_ref[...] += jnp.broadcast_to(part, loss_ref.shape)

    row = pl.BlockSpec((tm, D), lambda i: (i, 0))
    vec = pl.BlockSpec((1, D), lambda i: (0, 0))
    return pl.pallas_call(
        body, grid=(S // tm,), in_specs=[row, vec, row], out_specs=[row, vec, pl.BlockSpec((1, LANES), lambda i: (0, 0))],
        out_shape=[jax.ShapeDtypeStruct((S, D), F32), jax.ShapeDtypeStruct((1, D), F32),
                   jax.ShapeDtypeStruct((1, LANES), F32)],
        compiler_params=_params("arbitrary"), name=name)(h, gain.reshape(1, D), target)


def _rope_tables(S):
    half = ROT_DIM // 2
    inv_freq = ROPE_THETA ** (-jnp.arange(half, dtype=F32) * 2.0 / ROT_DIM)
    ang = jnp.arange(S, dtype=F32)[:, None] * inv_freq[None, :]
    cos, sin = jnp.cos(ang), jnp.sin(ang)
    one = jnp.ones((S, HEAD_DIM - ROT_DIM), F32)
    zero = jnp.zeros((S, HEAD_DIM - ROT_DIM), F32)
    zh = jnp.zeros((S, half), F32)
    c = jnp.concatenate([cos, cos, one], axis=1)
    sa = jnp.concatenate([-sin, zh, zero], axis=1)
    sb = jnp.concatenate([zh, sin, zero], axis=1)
    return tuple(jnp.concatenate([t, t], axis=1) for t in (c, sa, sb))


def _rotate(x, c, sa, sb, sign):
    return x * c + sign * (pltpu.roll(x, LANES - ROT_DIM // 2, 1) * sa + pltpu.roll(x, ROT_DIM // 2, 1) * sb)


def _a_proj(name, n, w, tabs, tm=1024):
    S, D = n.shape
    N = w.shape[1]
    tn = D_MODEL

    def epilogue(acc, ex, outs, j):
        @pl.when(j % 3 == 2)
        def _():
            outs[0][...] = acc.astype(BF16)

        @pl.when(j % 3 != 2)
        def _():
            c, sa, sb = ex[0][...], ex[1][...], ex[2][...]
            scale = jnp.where(j % 3 == 0, SOFTMAX_SCALE, 1.0)
            for p in range(N_PAIRS):
                cols = slice(p * LANES, (p + 1) * LANES)
                outs[0][:, cols] = (_rotate(acc[:, cols], c, sa, sb, 1.0) * scale).astype(BF16)

    tab = pl.BlockSpec((tm, LANES), lambda j, i, k: (i, 0))
    return _mm_call(name, (N // tn, S // tm, 1), n, pl.BlockSpec((tm, D), lambda j, i, k: (i, 0)),
                    w, pl.BlockSpec((D, tn), lambda j, i, k: (0, j)), NN,
                    [jax.ShapeDtypeStruct((S, N), BF16)], [pl.BlockSpec((tm, tn), lambda j, i, k: (i, j))], (tm, tn),
                    epilogue, tabs, (tab, tab, tab), col_axis=0)[0]


def _lo_lanes():
    return lax.broadcasted_iota(jnp.int32, (1, LANES), 1) < HEAD_DIM


def _rep_rows(x2, lo):
    sw = pltpu.roll(x2, HEAD_DIM, 1)
    return jnp.where(lo, x2, sw), jnp.where(lo, sw, x2)


def _pair_cols(h):
    return slice((h // 2) * LANES, (h // 2 + 1) * LANES)


def _head_lanes(lo, h):
    return lo if h % 2 == 0 else jnp.logical_not(lo)


def _band_masks(t, first):
    ri = lax.broadcasted_iota(jnp.int32, (t, t), 0)
    ci = lax.broadcasted_iota(jnp.int32, (t, t), 1)
    neg_prev = jnp.where((ci >= ri) & jnp.logical_not(first), 0.0, NEG_INF)
    neg_cur = jnp.where(ci <= ri, 0.0, NEG_INF)
    return neg_prev, neg_cur


def _dil_specs(L, R, t, qcol, kcol, vcol):
    W = D_MODEL
    prev = lambda qi: jnp.maximum(qi - 1, 0)
    return dict(
        q=pl.BlockSpec((t, W), lambda r, qi: (qi, qcol(r))),
        kp=pl.BlockSpec((t, W), lambda r, qi: (prev(qi), kcol(r))), kc=pl.BlockSpec((t, W), lambda r, qi: (qi, kcol(r))),
        vp=pl.BlockSpec((t, W), lambda r, qi: (prev(qi), vcol(r))), vc=pl.BlockSpec((t, W), lambda r, qi: (qi, vcol(r))),
        own=pl.BlockSpec((t, W), lambda r, qi: (qi, r)), tab=pl.BlockSpec((t, LANES), lambda r, qi: (qi, r)))


def _dil_fwd(name, x, qcol, kcol, vcol, R, L):
    t = BAND_STEPS
    W = D_MODEL
    sp = _dil_specs(L, R, t, qcol, kcol, vcol)

    def body(q_ref, kp_ref, kc_ref, vp_ref, vc_ref, o_ref, lse_ref):
        lo = _lo_lanes()
        neg_p, neg_c = _band_masks(t, pl.program_id(1) == 0)
        s_p, s_c = [], []
        for h in range(N_HEADS):
            cols = _pair_cols(h)
            qh = jnp.where(_head_lanes(lo, h), q_ref[:, cols], 0)
            s_p.append(_dot(qh, kp_ref[:, cols], NT))
            s_c.append(_dot(qh, kc_ref[:, cols], NT))
        s_p = jnp.stack(s_p) + neg_p[None]
        s_c = jnp.stack(s_c) + neg_c[None]
        m = jnp.maximum(jnp.max(s_p, axis=2, keepdims=True), jnp.max(s_c, axis=2, keepdims=True))
        p_p, p_c = jnp.exp(s_p - m), jnp.exp(s_c - m)
        l = jnp.sum(p_p, axis=2, keepdims=True) + jnp.sum(p_c, axis=2, keepdims=True)
        inv, lse = 1.0 / l, m + jnp.log(l)
        p_p, p_c = p_p.astype(BF16), p_c.astype(BF16)
        for p in range(N_PAIRS):
            cols = _pair_cols(2 * p)
            o2 = jnp.zeros((t, LANES), F32)
            for h in (2 * p, 2 * p + 1):
                hm = _head_lanes(lo, h)
                pv = _dot(p_p[h], jnp.where(hm, vp_ref[:, cols], 0), NN) + _dot(p_c[h], jnp.where(hm, vc_ref[:, cols], 0), NN)
                o2 = o2 + pv * inv[h]
            o_ref[:, cols] = o2
            lse_ref[:, cols] = jnp.where(lo, lse[2 * p], lse[2 * p + 1])

    return pl.pallas_call(
        body, grid=(R, L // t), in_specs=[sp["q"], sp["kp"], sp["kc"], sp["vp"], sp["vc"]], out_specs=[sp["own"], sp["own"]],
        out_shape=[jax.ShapeDtypeStruct((L, R * W), F32), jax.ShapeDtypeStruct((L, R * W), F32)],
        compiler_params=_params("parallel", "parallel"), name=name)(x, x, x, x, x)


def _dil_scores(lo, q_ref, do_ref, o_ref, lse_ref, kv_refs):
    s = [[] for _ in kv_refs]
    dp = [[] for _ in kv_refs]
    lse, d = [], []
    for h in range(N_HEADS):
        cols = _pair_cols(h)
        hm = _head_lanes(lo, h)
        qh, doh = jnp.where(hm, q_ref[:, cols], 0), jnp.where(hm, do_ref[:, cols], 0)
        for i, (k_ref, v_ref) in enumerate(kv_refs):
            s[i].append(_dot(qh, k_ref[:, cols], NT))
            dp[i].append(_dot(doh, v_ref[:, cols], NT))
        lse.append(_rep_rows(lse_ref[:, cols], lo)[h % 2])
        dd = do_ref[:, cols].astype(F32) * o_ref[:, cols].astype(F32)
        d.append(jnp.sum(jnp.where(hm, dd, 0.0), axis=1, keepdims=True))
    return (*[jnp.stack(x) for x in s], *[jnp.stack(x) for x in dp], jnp.stack(lse), jnp.stack(d))


def _dil_dq(name, x, do, o, lse, tabs, qcol, kcol, vcol, R, L):
    t = BAND_STEPS
    W = D_MODEL
    sp = _dil_specs(L, R, t, qcol, kcol, vcol)

    def body(q_ref, kp_ref, kc_ref, vp_ref, vc_ref, do_ref, o_ref, lse_ref, c_ref, sa_ref, sb_ref, dq_ref):
        lo = _lo_lanes()
        neg_p, neg_c = _band_masks(t, pl.program_id(1) == 0)
        s_p, s_c, dp_p, dp_c, lse, d = _dil_scores(lo, q_ref, do_ref, o_ref, lse_ref, ((kp_ref, vp_ref), (kc_ref, vc_ref)))
        ds_p = (jnp.exp(s_p + neg_p[None] - lse) * (dp_p - d)).astype(BF16)
        ds_c = (jnp.exp(s_c + neg_c[None] - lse) * (dp_c - d)).astype(BF16)
        for p in range(N_PAIRS):
            cols = _pair_cols(2 * p)
            dq2 = jnp.zeros((t, LANES), F32)
            for h in (2 * p, 2 * p + 1):
                hm = _head_lanes(lo, h)
                dq2 = dq2 + _dot(ds_p[h], jnp.where(hm, kp_ref[:, cols], 0), NN) + _dot(ds_c[h], jnp.where(hm, kc_ref[:, cols], 0), NN)
            dq_ref[:, cols] = _rotate(dq2 * SOFTMAX_SCALE, c_ref[...], sa_ref[...], sb_ref[...], -1.0).astype(BF16)

    return pl.pallas_call(
        body, grid=(R, L // t),
        in_specs=[sp["q"], sp["kp"], sp["kc"], sp["vp"], sp["vc"], sp["own"], sp["own"], sp["own"], sp["tab"], sp["tab"], sp["tab"]],
        out_specs=sp["own"], out_shape=jax.ShapeDtypeStruct((L, R * W), BF16),
        compiler_params=_params("parallel", "parallel"), name=name)(x, x, x, x, x, do, o, lse, *tabs)


def _dil_dkv(name, x, do, o, lse, tabs, qcol, kcol, vcol, R, L):
    t = BAND_STEPS
    W = D_MODEL
    nq = L // t
    nxt = lambda kb: jnp.minimum(kb + 1, nq - 1)
    cur_q = pl.BlockSpec((t, W), lambda r, kb: (kb, qcol(r)))
    nxt_q = pl.BlockSpec((t, W), lambda r, kb: (nxt(kb), qcol(r)))
    cur_o = pl.BlockSpec((t, W), lambda r, kb: (kb, r))
    nxt_o = pl.BlockSpec((t, W), lambda r, kb: (nxt(kb), r))
    tab = pl.BlockSpec((t, LANES), lambda r, kb: (kb, r))

    def body(k_ref, v_ref, qc_ref, qn_ref, doc_ref, don_ref, oc_ref, on_ref, lc_ref, ln_ref, c_ref, sa_ref, sb_ref,
             dk_ref, dv_ref):
        lo = _lo_lanes()
        ri = lax.broadcasted_iota(jnp.int32, (t, t), 0)
        ci = lax.broadcasted_iota(jnp.int32, (t, t), 1)
        neg_c = jnp.where(ci <= ri, 0.0, NEG_INF)
        neg_n = jnp.where((ci >= ri) & (pl.program_id(1) + 1 < nq), 0.0, NEG_INF)
        blocks = []
        for q_ref, do_ref, o_ref, l_ref, neg in ((qc_ref, doc_ref, oc_ref, lc_ref, neg_c), (qn_ref, don_ref, on_ref, ln_ref, neg_n)):
            s, dp, lse, d = _dil_scores(lo, q_ref, do_ref, o_ref, l_ref, ((k_ref, v_ref),))
            pr = jnp.exp(s + neg[None] - lse)
            blocks.append((q_ref, do_ref, pr.astype(BF16), (pr * (dp - d)).astype(BF16)))
        for p in range(N_PAIRS):
            cols = _pair_cols(2 * p)
            dk2 = jnp.zeros((t, LANES), F32)
            dv2 = jnp.zeros((t, LANES), F32)
            for q_ref, do_ref, pr, ds in blocks:
                for h in (2 * p, 2 * p + 1):
                    hm = _head_lanes(lo, h)
                    dv2 = dv2 + _dot(pr[h], jnp.where(hm, do_ref[:, cols], 0), TN)
                    dk2 = dk2 + _dot(ds[h], jnp.where(hm, q_ref[:, cols], 0), TN)
            dk_ref[:, cols] = _rotate(dk2, c_ref[...], sa_ref[...], sb_ref[...], -1.0).astype(BF16)
            dv_ref[:, cols] = dv2.astype(BF16)

    kcur = pl.BlockSpec((t, W), lambda r, kb: (kb, kcol(r)))
    vcur = pl.BlockSpec((t, W), lambda r, kb: (kb, vcol(r)))
    return pl.pallas_call(
        body, grid=(R, nq),
        in_specs=[kcur, vcur, cur_q, nxt_q, cur_o, nxt_o, cur_o, nxt_o, cur_o, nxt_o, tab, tab, tab],
        out_specs=[cur_o, cur_o], out_shape=[jax.ShapeDtypeStruct((L, R * W), BF16)] * 2,
        compiler_params=_params("parallel", "parallel"), name=name)(x, x, x, x, do, do, o, o, lse, lse, *tabs)


def _fox_operands(q2, k2, kb2, lo, hh):
    lane = lax.broadcasted_iota(jnp.int32, (1, LANES), 1)
    if hh == 0:
        ones = ((lane >= HEAD_DIM) & (lane < HEAD_DIM + 3)).astype(BF16)
        return jnp.where(lo, q2, ones), jnp.where(lo, k2, kb2)
    ones = (lane < 3).astype(BF16)
    return jnp.where(lo, ones, q2), jnp.where(lo, kb2, k2)


def _causal_neg(t):
    ri = lax.broadcasted_iota(jnp.int32, (t, t), 0)
    ci = lax.broadcasted_iota(jnp.int32, (t, t), 1)
    return jnp.where(ci <= ri, 0.0, NEG_INF)


def _fox_fwd(name, qkv, kbias, t):
    S = qkv.shape[0]
    W = D_MODEL
    nq = S // t
    rep = t // LANES

    def body(q_ref, k_ref, v_ref, kb_ref, o_ref, lse_ref, m_scr, l_scr, acc_scr):
        qi, j = pl.program_id(0), pl.program_id(1)
        lo = _lo_lanes()

        @pl.when(j == 0)
        def _():
            m_scr[...] = jnp.full_like(m_scr, NEG_INF)
            l_scr[...] = jnp.zeros_like(l_scr)
            acc_scr[...] = jnp.zeros_like(acc_scr)

        def step(masked):
            neg = _causal_neg(t) if masked else None

            def pair(p, carry):
                cs = pl.ds(pl.multiple_of(p * LANES, LANES), LANES)
                q2, k2, v2, kb2 = q_ref[:, cs], k_ref[:, cs], v_ref[:, cs], kb_ref[:, cs]
                pvs, alphas = [], []
                for hh in range(2):
                    hm = lo if hh == 0 else jnp.logical_not(lo)
                    qh, kh = _fox_operands(q2, k2, kb2, lo, hh)
                    s = _dot(qh, kh, NT)
                    if masked:
                        s = s + neg
                    h = 2 * p + hh
                    m_prev = m_scr[h]
                    m_new = jnp.maximum(m_prev, jnp.max(s, axis=1, keepdims=True))
                    pe = jnp.exp(s - jnp.tile(m_new, (1, rep)))
                    alpha = jnp.exp(m_prev - m_new)
                    l_scr[h] = alpha * l_scr[h] + jnp.sum(pe, axis=1, keepdims=True)
                    m_scr[h] = m_new
                    pvs.append(_dot(pe.astype(BF16), jnp.where(hm, v2, 0), NN))
                    alphas.append(alpha)
                acc_scr[:, cs] = acc_scr[:, cs] * jnp.where(lo, alphas[0], alphas[1]) + pvs[0] + pvs[1]
                return carry

            lax.fori_loop(0, N_PAIRS, pair, 0)

        @pl.when(j < qi)
        def _():
            step(False)

        @pl.when(j == qi)
        def _():
            step(True)

        @pl.when(j == nq - 1)
        def _():
            for p in range(N_PAIRS):
                cols = slice(p * LANES, (p + 1) * LANES)
                l2 = jnp.where(lo, l_scr[2 * p], l_scr[2 * p + 1])
                m2 = jnp.where(lo, m_scr[2 * p], m_scr[2 * p + 1])
                o_ref[:, cols] = (acc_scr[:, cols] / l2).astype(BF16)
                lse_ref[:, cols] = m2 + jnp.log(l2)

    kv = lambda col: pl.BlockSpec((t, W), lambda qi, j: (jnp.minimum(j, qi), col))
    own = pl.BlockSpec((t, W), lambda qi, j: (qi, 0))
    return pl.pallas_call(
        body, grid=(nq, nq), in_specs=[own, kv(1), kv(2), kv(0)], out_specs=[own, own],
        out_shape=[jax.ShapeDtypeStruct((S, W), BF16), jax.ShapeDtypeStruct((S, W), F32)],
        scratch_shapes=[pltpu.VMEM((N_HEADS, t, LANES), F32), pltpu.VMEM((N_HEADS, t, LANES), F32), pltpu.VMEM((t, W), F32)],
        compiler_params=_params("parallel", "arbitrary"), name=name)(qkv, qkv, qkv, kbias)


def _fox_head_grads(qh, kh, v2, doh, neg, lse_h, d_h, rep):
    s = _dot(qh, kh, NT)
    if neg is not None:
        s = s + neg
    p = jnp.exp(s - jnp.tile(lse_h, (1, rep)))
    return p, p * (_dot(doh, v2, NT) - d_h)


def _fox_dq(name, qkv, kbias, do, o, lse, t):
    S = qkv.shape[0]
    W = D_MODEL
    nq = S // t
    rep = t // LANES

    def body(q_ref, k_ref, v_ref, kb_ref, do_ref, o_ref, lse_ref, dq_ref, rs_ref, dq_scr, rs_scr):
        qi, j = pl.program_id(0), pl.program_id(1)
        lo = _lo_lanes()

        @pl.when(j == 0)
        def _():
            dq_scr[...] = jnp.zeros_like(dq_scr)
            rs_scr[...] = jnp.zeros_like(rs_scr)

        def step(masked):
            neg = _causal_neg(t) if masked else None

            def pair(p, carry):
                cs = pl.ds(pl.multiple_of(p * LANES, LANES), LANES)
                q2, k2, v2, kb2, do2 = q_ref[:, cs], k_ref[:, cs], v_ref[:, cs], kb_ref[:, cs], do_ref[:, cs]
                dd = do2.astype(F32) * o_ref[:, cs].astype(F32)
                lse_h = _rep_rows(lse_ref[:, cs], lo)
                dq2 = jnp.zeros((t, LANES), F32)
                for hh in range(2):
                    hm = lo if hh == 0 else jnp.logical_not(lo)
                    qh, kh = _fox_operands(q2, k2, kb2, lo, hh)
                    d_h = jnp.sum(jnp.where(hm, dd, 0.0), axis=1, keepdims=True)
                    _, ds = _fox_head_grads(qh, kh, v2, jnp.where(hm, do2, 0), neg, lse_h[hh], d_h, rep)
                    rs_scr[2 * p + hh] += jnp.broadcast_to(jnp.sum(ds, axis=1, keepdims=True), (t, LANES))
                    dq2 = dq2 + _dot(ds.astype(BF16), jnp.where(hm, k2, 0), NN)
                dq_scr[:, cs] += dq2
                return carry

            lax.fori_loop(0, N_PAIRS, pair, 0)

        @pl.when(j < qi)
        def _():
            step(False)

        @pl.when(j == qi)
        def _():
            step(True)

        @pl.when(j == nq - 1)
        def _():
            dq_ref[...] = (dq_scr[...] * SOFTMAX_SCALE).astype(BF16)
            for p in range(N_PAIRS):
                rs_ref[:, p * LANES:(p + 1) * LANES] = jnp.where(lo, rs_scr[2 * p], rs_scr[2 * p + 1])

    kv = lambda col: pl.BlockSpec((t, W), lambda qi, j: (jnp.minimum(j, qi), col))
    own = pl.BlockSpec((t, W), lambda qi, j: (qi, 0))
    return pl.pallas_call(
        body, grid=(nq, nq), in_specs=[own, kv(1), kv(2), kv(0), own, own, own], out_specs=[own, own],
        out_shape=[jax.ShapeDtypeStruct((S, W), BF16), jax.ShapeDtypeStruct((S, W), F32)],
        scratch_shapes=[pltpu.VMEM((t, W), F32), pltpu.VMEM((N_HEADS, t, LANES), F32)],
        compiler_params=_params("parallel", "arbitrary"), name=name)(qkv, qkv, qkv, kbias, do, o, lse)


def _fox_dkv(name, qkv, kbias, do, o, lse, t):
    S = qkv.shape[0]
    W = D_MODEL
    nq = S // t
    rep = t // LANES

    def body(q_ref, k_ref, v_ref, kb_ref, do_ref, o_ref, lse_ref, dk_ref, dv_ref, dc_ref, dk_scr, dv_scr):
        kb, j = pl.program_id(0), pl.program_id(1)
        lo = _lo_lanes()

        @pl.when(j == 0)
        def _():
            dk_scr[...] = jnp.zeros_like(dk_scr)
            dv_scr[...] = jnp.zeros_like(dv_scr)
            dc_ref[...] = jnp.zeros_like(dc_ref)

        def step(masked):
            neg = _causal_neg(t) if masked else None

            def pair(p, carry):
                cs = pl.ds(pl.multiple_of(p * LANES, LANES), LANES)
                q2, k2, v2, kb2, do2 = q_ref[:, cs], k_ref[:, cs], v_ref[:, cs], kb_ref[:, cs], do_ref[:, cs]
                dd = do2.astype(F32) * o_ref[:, cs].astype(F32)
                lse_h = _rep_rows(lse_ref[:, cs], lo)
                dv2 = jnp.zeros((t, LANES), F32)
                dk2 = jnp.zeros((t, LANES), F32)
                for hh in range(2):
                    hm = lo if hh == 0 else jnp.logical_not(lo)
                    qh, kh = _fox_operands(q2, k2, kb2, lo, hh)
                    doh = jnp.where(hm, do2, 0)
                    d_h = jnp.sum(jnp.where(hm, dd, 0.0), axis=1, keepdims=True)
                    pr, ds = _fox_head_grads(qh, kh, v2, doh, neg, lse_h[hh], d_h, rep)
                    dc_ref[p, hh:hh + 1, :] += jnp.sum(ds, axis=0, keepdims=True)
                    dv2 = dv2 + _dot(pr.astype(BF16), doh, TN)
                    dk2 = dk2 + _dot(ds.astype(BF16), jnp.where(hm, q2, 0), TN)
                dv_scr[:, cs] += dv2
                dk_scr[:, cs] += dk2
                return carry

            lax.fori_loop(0, N_PAIRS, pair, 0)

        @pl.when(j > kb)
        def _():
            step(False)

        @pl.when(j == kb)
        def _():
            step(True)

        @pl.when(j == nq - 1)
        def _():
            dv_ref[...] = dv_scr[...].astype(BF16)
            dk_ref[...] = dk_scr[...].astype(BF16)

    qrow = pl.BlockSpec((t, W), lambda kb, j: (jnp.maximum(j, kb), 0))
    krow = lambda col: pl.BlockSpec((t, W), lambda kb, j: (kb, col))
    own = pl.BlockSpec((t, W), lambda kb, j: (kb, 0))
    return pl.pallas_call(
        body, grid=(nq, nq), in_specs=[qrow, krow(1), krow(2), krow(0), qrow, qrow, qrow],
        out_specs=[own, own, pl.BlockSpec((N_PAIRS, 2, t), lambda kb, j: (0, 0, kb))],
        out_shape=[jax.ShapeDtypeStruct((S, W), BF16), jax.ShapeDtypeStruct((S, W), BF16), jax.ShapeDtypeStruct((N_PAIRS, 2, S), F32)],
        scratch_shapes=[pltpu.VMEM((t, W), F32), pltpu.VMEM((t, W), F32)],
        compiler_params=_params("parallel", "arbitrary"), name=name)(qkv, qkv, qkv, kbias, do, o, lse)


def _combine(name, os_, lses, tm=256):
    S, W = os_[0].shape
    G = len(os_)

    def body(*refs):
        o_refs, l_refs = refs[:G], refs[G:2 * G]
        o_ref, lse_ref = refs[2 * G:]
        ls = [r[...] for r in l_refs]
        m = functools.reduce(jnp.maximum, ls)
        ws = [jnp.exp(l - m) for l in ls]
        den = functools.reduce(jnp.add, ws)
        num = functools.reduce(jnp.add, [w * r[...] for w, r in zip(ws, o_refs)])
        o_ref[...] = (num / den).astype(BF16)
        lse_ref[...] = m + jnp.log(den)

    row = pl.BlockSpec((tm, W), lambda i: (i, 0))
    return pl.pallas_call(
        body, grid=(S // tm,), in_specs=[row] * (2 * G), out_specs=[row, row],
        out_shape=[jax.ShapeDtypeStruct((S, W), BF16), jax.ShapeDtypeStruct((S, W), F32)],
        compiler_params=_params("parallel"), name=name)(*os_, *lses)


def _tri_matmul(tri, x):
    hi, mid, lo = _split3(x)
    return _dot(tri, hi, NN) + _dot(tri, mid, NN) + _dot(tri, lo, NN)


def _split3(x):
    hi = x.astype(BF16)
    r1 = x - hi.astype(F32)
    mid = r1.astype(BF16)
    return hi, mid, (r1 - mid.astype(F32)).astype(BF16)


def _gate_fwd(name, z, bf, tb=512):
    S = z.shape[0]

    def body(z_ref, b_ref, kb_ref, carry):
        @pl.when(pl.program_id(0) == 0)
        def _():
            carry[...] = jnp.zeros_like(carry)

        lf = jax.nn.log_sigmoid(z_ref[...] + b_ref[...])
        ri = lax.broadcasted_iota(jnp.int32, (tb, tb), 0)
        ci = lax.broadcasted_iota(jnp.int32, (tb, tb), 1)
        tri = (ci <= ri).astype(BF16)
        c = _tri_matmul(tri, lf) + carry[...]
        carry[...] = c[tb - 1:tb, :]
        head = lax.broadcasted_iota(jnp.int32, (LANES, D_MODEL), 0)
        col = lax.broadcasted_iota(jnp.int32, (LANES, D_MODEL), 1)
        base = (head >> 1) * LANES + jnp.where((head & 1) == 0, HEAD_DIM, 0)
        kb = jnp.zeros((tb, D_MODEL), F32)
        for i, piece in enumerate(_split3(-c)):
            place = ((col == base + i) & (head < N_HEADS)).astype(BF16)
            kb = kb + _dot(piece, place, NN)
        kb_ref[...] = kb.astype(BF16)

    row = pl.BlockSpec((tb, LANES), lambda i: (i, 0))
    return pl.pallas_call(
        body, grid=(S // tb,), in_specs=[row, pl.BlockSpec((1, LANES), lambda i: (0, 0))],
        out_specs=pl.BlockSpec((tb, D_MODEL), lambda i: (i, 0)), out_shape=jax.ShapeDtypeStruct((S, D_MODEL), BF16),
        scratch_shapes=[pltpu.VMEM((1, LANES), F32)], compiler_params=_params("arbitrary"), name=name)(z, bf)


def _gate_bwd(name, dc, z, bf, tb=512):
    S = z.shape[0]
    nb = S // tb

    def body(dc_ref, z_ref, b_ref, dz_ref, db_ref, carry):
        @pl.when(pl.program_id(0) == 0)
        def _():
            carry[...] = jnp.zeros_like(carry)
            db_ref[...] = jnp.zeros_like(db_ref)

        ri = lax.broadcasted_iota(jnp.int32, (tb, tb), 0)
        ci = lax.broadcasted_iota(jnp.int32, (tb, tb), 1)
        tri = (ci >= ri).astype(BF16)
        dlf = _tri_matmul(tri, dc_ref[...]) + carry[...]
        carry[...] = dlf[0:1, :]
        dz = dlf * jax.nn.sigmoid(-(z_ref[...] + b_ref[...]))
        dz_ref[...] = dz
        db_ref[...] += jnp.sum(dz, axis=0, keepdims=True)

    row = pl.BlockSpec((tb, LANES), lambda i: (nb - 1 - i, 0))
    vec = pl.BlockSpec((1, LANES), lambda i: (0, 0))
    return pl.pallas_call(
        body, grid=(nb,), in_specs=[row, row, vec], out_specs=[row, vec],
        out_shape=[jax.ShapeDtypeStruct((S, LANES), F32), jax.ShapeDtypeStruct((1, LANES), F32)],
        scratch_shapes=[pltpu.VMEM((1, LANES), F32)], compiler_params=_params("arbitrary"), name=name)(dc, z, bf)


def _ffn_gu(name, n, wgu, tm=1024):
    S, D = n.shape
    nb = N_DEV // 2

    def body(n_ref, wg_ref, wu_ref, gu_ref, act_ref):
        x = n_ref[...]
        g = _dot(x, wg_ref[...], NN)
        u = _dot(x, wu_ref[...], NN)
        gu_ref[0] = g.astype(BF16)
        gu_ref[1] = u.astype(BF16)
        act_ref[...] = (g * jax.nn.sigmoid(g) * u).astype(BF16)

    return pl.pallas_call(
        body, grid=(nb, S // tm),
        in_specs=[pl.BlockSpec((tm, D), lambda j, i: (i, 0)), pl.BlockSpec((None, D, FF_BLK), lambda j, i: (j, 0, 0)),
                  pl.BlockSpec((None, D, FF_BLK), lambda j, i: (j + nb, 0, 0))],
        out_specs=[pl.BlockSpec((2, None, tm, FF_BLK), lambda j, i: (0, j, i, 0)),
                   pl.BlockSpec((None, tm, FF_BLK), lambda j, i: (j, i, 0))],
        out_shape=[jax.ShapeDtypeStruct((2, nb, S, FF_BLK), BF16), jax.ShapeDtypeStruct((nb, S, FF_BLK), BF16)],
        compiler_params=_params("parallel", "parallel"), name=name)(n, wgu, wgu)


def _ffn_down(name, act, wd, resid, tm=1024):
    nb, S, _ = act.shape
    D = wd.shape[1]

    def epilogue(acc, ex, outs, j):
        outs[0][...] = acc + ex[0][...]

    o_spec = pl.BlockSpec((tm, D), lambda i, j, k: (i, 0))
    return _mm_call(name, (S // tm, 1, nb), act, pl.BlockSpec((None, tm, FF_BLK), lambda i, j, k: (k, i, 0)),
                    wd, pl.BlockSpec((FF_BLK, D), lambda i, j, k: (k, 0)), NN,
                    [jax.ShapeDtypeStruct((S, D), F32)], [o_spec], (tm, D), epilogue, (resid,), (o_spec,))[0]


def _ffn_dact(name, dh, wd, gu, tm=512):
    S, D = dh.shape
    nb = N_DEV // 2

    def epilogue(acc, ex, outs, j):
        g = ex[0][0].astype(F32)
        u = ex[0][1].astype(F32)
        sig = jax.nn.sigmoid(g)
        outs[0][0] = (acc * u * (sig * (1.0 + g * (1.0 - sig)))).astype(BF16)
        outs[0][1] = (acc * (g * sig)).astype(BF16)

    gu_spec = pl.BlockSpec((2, None, tm, FF_BLK), lambda j, i, k: (0, j, i, 0))
    return _mm_call(name, (nb, S // tm, 1), dh, pl.BlockSpec((tm, D), lambda j, i, k: (i, 0)),
                    wd, pl.BlockSpec((FF_BLK, D), lambda j, i, k: (j, 0)), NT,
                    [jax.ShapeDtypeStruct((2, nb, S, FF_BLK), BF16)], [gu_spec], (tm, FF_BLK), epilogue, (gu,), (gu_spec,),
                    col_axis=0)[0]


def _ffn_dwgu(name, n, dgu, tm=1024, tk=1024):
    S, D = n.shape
    dgu8 = dgu.reshape(N_DEV, S, FF_BLK)
    return _mm_call(name, (N_DEV, D // tm, S // tk), n, pl.BlockSpec((tk, tm), lambda d, i, k: (k, i)),
                    dgu8, pl.BlockSpec((None, tk, FF_BLK), lambda d, i, k: (d, k, 0)), TN,
                    [jax.ShapeDtypeStruct((N_DEV, D, FF_BLK), F32)],
                    [pl.BlockSpec((None, tm, FF_BLK), lambda d, i, k: (d, i, 0))], (tm, FF_BLK))[0]


def _ffn_dwd(name, act, dh, tk=1024):
    nb, S, _ = act.shape
    D = dh.shape[1]
    out = _mm_call(name, (nb, 1, S // tk), act, pl.BlockSpec((None, tk, FF_BLK), lambda b, j, k: (b, k, 0)),
                   dh, pl.BlockSpec((tk, D), lambda b, j, k: (k, 0)), TN,
                   [jax.ShapeDtypeStruct((nb, FF_BLK, D), F32)],
                   [pl.BlockSpec((None, FF_BLK, D), lambda b, j, k: (b, 0, 0))], (FF_BLK, D))[0]
    return out.reshape(nb * FF_BLK, D)


def _ffn_dn(name, dgu, wgu, tm=1024):
    S = dgu.shape[2]
    D = wgu.shape[1]
    dgu8 = dgu.reshape(N_DEV, S, FF_BLK)
    return _mm_call(name, (S // tm, 1, N_DEV), dgu8, pl.BlockSpec((None, tm, FF_BLK), lambda i, j, k: (k, i, 0)),
                    wgu, pl.BlockSpec((None, D, FF_BLK), lambda i, j, k: (k, 0, 0)), NT,
                    [jax.ShapeDtypeStruct((S, D), F32)], [pl.BlockSpec((tm, D), lambda i, j, k: (i, 0))], (tm, D))[0]


def _adamw(name, parts, w, m, v, tr):
    rows, cols = w.shape
    n_parts = len(parts)
    c1 = 1.0 - ADAM_B1 ** ADAM_STEP
    c2 = 1.0 - ADAM_B2 ** ADAM_STEP

    def body(*refs):
        p_refs = refs[:n_parts]
        w_ref, m_ref, v_ref, g_ref, d_ref, nm_ref, nv_ref = refs[n_parts:]
        g = p_refs[0][...].astype(F32)
        for r in p_refs[1:]:
            g = g + r[...].astype(F32)
        mm = ADAM_B1 * m_ref[...] + (1.0 - ADAM_B1) * g
        vv = ADAM_B2 * v_ref[...] + (1.0 - ADAM_B2) * (g * g)
        g_ref[...] = g
        nm_ref[...] = mm
        nv_ref[...] = vv
        d_ref[...] = -ADAM_LR * ((mm / c1) / (jnp.sqrt(vv / c2) + ADAM_EPS) + ADAM_WD * w_ref[...])

    blk = pl.BlockSpec((tr, cols), lambda i: (i, 0))
    out = jax.ShapeDtypeStruct((rows, cols), F32)
    return pl.pallas_call(
        body, grid=(rows // tr,), in_specs=[blk] * (n_parts + 3), out_specs=[blk] * 4, out_shape=[out] * 4,
        compiler_params=_params("parallel"), name=name)(*parts, w, m, v)


def _position():
    return lax.axis_index("x"), lax.axis_index("y"), lax.axis_index("c")


def _all_gather(name, block):
    shape, dtype = block.shape, block.dtype

    def body(x_ref, out_ref, send_sems, recv_sems, local_sem):
        x, y, c = _position()
        me, sibling = (x, y, c), (x, y, 1 - c)
        chips = [(1 - x, y), (x, 1 - y), (1 - x, 1 - y)]

        def slot(px, py, pc):
            return out_ref.at[4 * px + 2 * py + pc]

        def copy(k, blk, to, src=None):
            return pltpu.make_async_remote_copy(
                src_ref=slot(*blk) if src is None else src, dst_ref=slot(*blk), send_sem=send_sems.at[k],
                recv_sem=recv_sems.at[k], device_id=to, device_id_type=MESH)

        mine = pltpu.make_async_copy(x_ref, slot(*me), local_sem)
        mine.start()
        first = [copy(0, me, sibling, src=x_ref)]
        first += [copy(1 + j, me, (*chip, c), src=x_ref) for j, chip in enumerate(chips)]
        for cp in first:
            cp.start()
        passed = [copy(4 + j, (*chip, c), sibling) for j, chip in enumerate(chips)]
        for j, chip in enumerate(chips):
            copy(1 + j, (*chip, c), me).wait_recv()
            passed[j].start()
        copy(0, sibling, me).wait_recv()
        for j, chip in enumerate(chips):
            copy(4 + j, (*chip, 1 - c), me).wait_recv()
        for cp in first + passed:
            cp.wait_send()
        mine.wait()

    hbm = pl.BlockSpec(memory_space=pltpu.HBM)
    return pl.pallas_call(
        body, out_shape=jax.ShapeDtypeStruct((N_DEV,) + shape, dtype), in_specs=[hbm], out_specs=hbm,
        scratch_shapes=[pltpu.SemaphoreType.DMA((7,)), pltpu.SemaphoreType.DMA((7,)), pltpu.SemaphoreType.DMA],
        name=name)(block)


def _swap_halves(name, packed):
    _, _, rows, cols = packed.shape

    def body(p_ref, got_ref, send_sem, recv_sem):
        x, y, c = _position()
        cp = pltpu.make_async_remote_copy(
            src_ref=p_ref.at[:, 1 - c], dst_ref=got_ref, send_sem=send_sem, recv_sem=recv_sem,
            device_id=(x, y, 1 - c), device_id_type=MESH)
        cp.start()
        cp.wait()

    hbm = pl.BlockSpec(memory_space=pltpu.HBM)
    return pl.pallas_call(
        body, out_shape=jax.ShapeDtypeStruct((4, rows, cols), packed.dtype), in_specs=[hbm], out_specs=hbm,
        scratch_shapes=[pltpu.SemaphoreType.DMA, pltpu.SemaphoreType.DMA], name=name)(packed)


def _pair_sum(name, packed, got, core, tr=560):
    _, _, rows, cols = packed.shape

    def body(c_ref, a_ref, b_ref, o_ref):
        o_ref[...] = (a_ref[...].astype(F32) + b_ref[...].astype(F32)).astype(o_ref.dtype)

    grid_spec = pltpu.PrefetchScalarGridSpec(
        num_scalar_prefetch=1, grid=(4, rows // tr),
        in_specs=[pl.BlockSpec((None, None, tr, cols), lambda q, i, c: (q, c[0], i, 0)),
                  pl.BlockSpec((None, tr, cols), lambda q, i, c: (q, i, 0))],
        out_specs=pl.BlockSpec((None, tr, cols), lambda q, i, c: (q, i, 0)))
    return pl.pallas_call(
        body, grid_spec=grid_spec, out_shape=jax.ShapeDtypeStruct((4, rows, cols), packed.dtype),
        compiler_params=_params("parallel", "parallel"), name=name)(core, packed, got)


def _exchange_chips(name, sums):
    _, rows, cols = sums.shape

    def body(s_ref, got_ref, send_sems, recv_sems):
        x, y, c = _position()
        chips = [(1 - x, y), (x, 1 - y), (1 - x, 1 - y)]
        cps = [pltpu.make_async_remote_copy(
            src_ref=s_ref.at[2 * px + py], dst_ref=got_ref.at[k], send_sem=send_sems.at[k], recv_sem=recv_sems.at[k],
            device_id=(px, py, c), device_id_type=MESH) for k, (px, py) in enumerate(chips)]
        for cp in cps:
            cp.start()
        for cp in cps:
            cp.wait()

    hbm = pl.BlockSpec(memory_space=pltpu.HBM)
    return pl.pallas_call(
        body, out_shape=jax.ShapeDtypeStruct((3, rows, cols), sums.dtype), in_specs=[hbm], out_specs=hbm,
        scratch_shapes=[pltpu.SemaphoreType.DMA((3,)), pltpu.SemaphoreType.DMA((3,))], name=name)(sums)


PACK = (("a_w_in", 1152, 1152), ("a_w_out", 128, 128), ("b_w_in", 386, 400), ("b_w_out", 128, 128),
        ("gu0", 704, 704), ("gu1", 704, 704), ("dn0", 352, 352), ("dn1", 352, 352))
PACK_ROWS = sum(p[2] for p in PACK)
PACK_OFF = {p[0]: sum(q[2] for q in PACK[:i]) for i, p in enumerate(PACK)}
PACK_LEN = {p[0]: p[1] for p in PACK}
ADAM_TILE = {"a_w_in": 128, "a_w_out": 128, "b_w_in": 386, "b_w_out": 128, "gu0": 176, "gu1": 176, "dn0": 176, "dn1": 176}
B_IN = 3 * D_MODEL + N_HEADS
B_IN_PAD = 3 * D_MODEL + LANES


def _pack_rows(pieces, dtype):
    out = []
    for name, rows, padded in PACK:
        a = pieces[name].astype(dtype)
        if padded != rows:
            a = jnp.pad(a, [(0, 0)] * (a.ndim - 2) + [(0, padded - rows), (0, 0)])
        out.append(a)
    return jnp.concatenate(out, axis=-2)


def _unpack(packed, name):
    off = PACK_OFF[name]
    return packed[..., off:off + PACK_LEN[name], :]


def _gathered_weights(wg):
    blocks = lambda name, shape: _unpack(wg, name).reshape((N_DEV,) + shape)
    a_w_in = blocks("a_w_in", (D_MODEL, 1152)).transpose(1, 0, 2).reshape(D_MODEL, 9 * D_MODEL)
    b_w_in = blocks("b_w_in", (D_MODEL, 386)).transpose(1, 0, 2).reshape(D_MODEL, B_IN)
    b_gate = jnp.pad(b_w_in[:, 3 * D_MODEL:], ((0, 0), (0, LANES - N_HEADS)))
    return dict(
        a_w_in=a_w_in, a_w_out=_unpack(wg, "a_w_out").reshape(D_MODEL, D_MODEL),
        b_w_qkv=b_w_in[:, :3 * D_MODEL], b_w_gate=b_gate, b_w_cat=jnp.concatenate([b_w_in[:, :3 * D_MODEL], b_gate], axis=1),
        b_w_out=_unpack(wg, "b_w_out").reshape(D_MODEL, D_MODEL),
        gu=[blocks("gu%d" % l, (D_MODEL, FF_BLK)) for l in range(2)],
        dn=[_unpack(wg, "dn%d" % l).reshape(D_FF, D_MODEL) for l in range(2)])


def _local_step(h0, target, W, norms):
    S = h0.shape[0]
    tabs = _rope_tables(S)

    n0 = _rms_fwd("rms_a", h0, norms["a_norm"])
    proj_a = _a_proj("proj_a", n0, W["a_w_in"], tabs)
    groups = []
    for g, (window, dil) in enumerate(DILATED_PATTERNS):
        L = S // dil
        view = proj_a.reshape(L, dil * 9 * D_MODEL)
        cols = [(lambda r, g=g, tq=tq: r * 9 + g * 3 + tq) for tq in range(3)]
        groups.append((g, dil, L, view, cols))
    outs, lses = [], []
    for g, dil, L, view, cols in groups:
        o_g, lse_g = _dil_fwd("dil_fwd%d" % g, view, *cols, dil, L)
        outs.append(o_g.reshape(S, D_MODEL))
        lses.append(lse_g.reshape(S, D_MODEL))
    o_a, lse_a = _combine("dil_combine", outs, lses)
    h1 = _matmul("out_a", o_a, W["a_w_out"], "nn", F32, TM, 1024, 1024, resid=h0)

    n1 = _rms_fwd("rms_f0", h1, norms["ffn_norm0"])
    gu0, act0 = _ffn_gu("gu_f0", n1, W["gu"][0])
    h2 = _ffn_down("down_f0", act0, W["dn"][0], h1)

    n2 = _rms_fwd("rms_b", h2, norms["b_norm"])
    qkv = _matmul("proj_b", n2, W["b_w_qkv"], "nn", BF16, TM, 1024, 1024, col0_scale=SOFTMAX_SCALE)
    z = _matmul("gate_b", n2, W["b_w_gate"], "nn", F32, TM, LANES, 1024)
    kbias = _gate_fwd("gate_cumsum", z, norms["b_f"])
    tf = min(S, 512)
    o_b, lse_b = _fox_fwd("fox_fwd", qkv, kbias, tf)
    h3 = _matmul("out_b", o_b, W["b_w_out"], "nn", F32, TM, 1024, 1024, resid=h2)

    n3 = _rms_fwd("rms_f1", h3, norms["ffn_norm1"])
    gu1, act1 = _ffn_gu("gu_f1", n3, W["gu"][1])
    h4 = _ffn_down("down_f1", act1, W["dn"][1], h3)

    dh4, d_final, loss = _loss_head("loss_head", h4, norms["final_norm"], target)

    def ffn_bwd(tag, dh_out, h_in, n, gu, act, wgu, wd, gain):
        dgu = _ffn_dact("dact_" + tag, dh_out, wd, gu)
        dwd = _ffn_dwd("dwd_" + tag, act, dh_out)
        dwgu = _ffn_dwgu("dwgu_" + tag, n, dgu)
        dn = _ffn_dn("dn_" + tag, dgu, wgu)
        dh_in, dgain = _rms_bwd("rmsb_" + tag, dn, h_in, gain, dh_out)
        return dh_in, dgain, dwgu, dwd

    dh3, d_ffn1, dwgu1, dwd1 = ffn_bwd("f1", dh4, h3, n3, gu1, act1, W["gu"][1], W["dn"][1], norms["ffn_norm1"])

    do_b = _matmul("dout_b", dh3, W["b_w_out"], "nt", BF16, TM, 1024, 1024)
    dw_b_out = _matmul("dwout_b", o_b, dh3, "tn", F32, TM, 1024, 1024)
    dq_b, ds_rowsum = _fox_dq("fox_dq", qkv, kbias, do_b, o_b, lse_b, tf)
    dk_b, dv_b, ds_colsum = _fox_dkv("fox_dkv", qkv, kbias, do_b, o_b, lse_b, tf)
    dc = ds_rowsum[:, ::HEAD_DIM] - ds_colsum.reshape(N_HEADS, S).T
    dz, d_bf = _gate_bwd("gate_bwd", jnp.pad(dc, ((0, 0), (0, LANES - N_HEADS))), z, norms["b_f"])
    dproj_b = jnp.concatenate([dq_b, dk_b, dv_b, dz.astype(BF16)], axis=1)
    dw_b_in = _matmul("dwin_b", n2, dproj_b, "tn", F32, TM, B_IN_PAD // 5, 1024)
    dn2 = _matmul("dn_b", dproj_b, W["b_w_cat"], "nt", F32, TM, 1024, B_IN_PAD // 5)
    dh2, d_bnorm = _rms_bwd("rmsb_b", dn2, h2, norms["b_norm"], dh3)

    dh1, d_ffn0, dwgu0, dwd0 = ffn_bwd("f0", dh2, h1, n1, gu0, act0, W["gu"][0], W["dn"][0], norms["ffn_norm0"])

    do_a = _matmul("dout_a", dh1, W["a_w_out"], "nt", BF16, TM, 1024, 1024)
    dw_a_out = _matmul("dwout_a", o_a, dh1, "tn", F32, TM, 1024, 1024)
    pieces = []
    for g, dil, L, view, cols in groups:
        sv = lambda a: a.reshape(L, dil * a.shape[1])
        args = (view, sv(do_a), sv(o_a), sv(lse_a), tuple(sv(tb) for tb in tabs), *cols, dil, L)
        dq_g = _dil_dq("dil_dq%d" % g, *args)
        dk_g, dv_g = _dil_dkv("dil_dkv%d" % g, *args)
        pieces += [a.reshape(S, D_MODEL) for a in (dq_g, dk_g, dv_g)]
    dproj_a = jnp.concatenate(pieces, axis=1)
    dw_a_in = _mm_call("dwin_a", (N_DEV, 1, S // 1024), n0, pl.BlockSpec((1024, D_MODEL), lambda d, i, k: (k, 0)),
                       dproj_a, pl.BlockSpec((1024, 1152), lambda d, i, k: (k, d)), TN,
                       [jax.ShapeDtypeStruct((N_DEV, D_MODEL, 1152), F32)],
                       [pl.BlockSpec((None, D_MODEL, 1152), lambda d, i, k: (d, 0, 0))], (D_MODEL, 1152))[0]
    dn0 = _matmul("dn_a", dproj_a, W["a_w_in"], "nt", F32, TM, 1024, 1024)
    dx, d_anorm = _rms_bwd("rmsb_a", dn0, h0, norms["a_norm"], dh1)

    by_dest = lambda a, cols: a.reshape(D_MODEL, N_DEV, cols).transpose(1, 0, 2)
    wgrads = {
        "a_w_in": dw_a_in.reshape(N_DEV, 1152, D_MODEL),
        "a_w_out": dw_a_out.reshape(N_DEV, 128, D_MODEL),
        "b_w_in": by_dest(dw_b_in[:, :B_IN], 386).reshape(N_DEV, 386, D_MODEL),
        "b_w_out": dw_b_out.reshape(N_DEV, 128, D_MODEL),
        "gu0": dwgu0.reshape(N_DEV, FF_BLK, D_MODEL), "gu1": dwgu1.reshape(N_DEV, FF_BLK, D_MODEL),
        "dn0": dwd0.reshape(N_DEV, 352, D_MODEL), "dn1": dwd1.reshape(N_DEV, 352, D_MODEL),
    }
    vgrads = dict(a_norm=d_anorm, ffn_norm0=d_ffn0, ffn_norm1=d_ffn1, final_norm=d_final, b_norm=d_bnorm, b_f=d_bf)
    return loss, dx, wgrads, vgrads


def kernel(x, a_norm, a_w_in, a_w_out, b_norm, b_w_in, b_f, b_w_out, ffn_norm, ffn_w_gu, ffn_w_down, final_norm, loss_target, m_a_norm, m_a_w_in, m_a_w_out, m_b_norm, m_b_w_in, m_b_f, m_b_w_out, m_ffn_norm, m_ffn_w_gu, m_ffn_w_down, m_final_norm, v_a_norm, v_a_w_in, v_a_w_out, v_b_norm, v_b_w_in, v_b_f, v_b_w_out, v_ffn_norm, v_ffn_w_gu, v_ffn_w_down, v_final_norm):
    S = x.shape[1]
    xi, yi, ci = _position()
    dev = 4 * xi + 2 * yi + ci
    rows = lambda a: a.reshape(-1, D_MODEL)

    def shards(a_in, a_out, b_in, b_out, gu, dn):
        return {"a_w_in": rows(a_in), "a_w_out": rows(a_out), "b_w_in": rows(b_in), "b_w_out": rows(b_out),
                "gu0": rows(gu[0]), "gu1": rows(gu[1]), "dn0": rows(dn[0]), "dn1": rows(dn[1])}

    w_sh = shards(a_w_in, a_w_out, b_w_in, b_w_out, ffn_w_gu, ffn_w_down)
    m_sh = shards(m_a_w_in, m_a_w_out, m_b_w_in, m_b_w_out, m_ffn_w_gu, m_ffn_w_down)
    v_sh = shards(v_a_w_in, v_a_w_out, v_b_w_in, v_b_w_out, v_ffn_w_gu, v_ffn_w_down)

    wg = _all_gather("gather_weights", _pack_rows(w_sh, BF16))
    b_norm_full = _all_gather("gather_b_norm", jnp.pad(b_norm, ((0, 7), (0, 0)))).reshape(N_DEV, 8, LANES)[:, 0].reshape(1, D_MODEL)
    W = _gathered_weights(wg)
    bf_pad = jnp.pad(b_f, ((0, 0), (0, LANES - N_HEADS)))
    norms = dict(a_norm=a_norm[0], ffn_norm0=ffn_norm[0], ffn_norm1=ffn_norm[1], final_norm=final_norm,
                 b_norm=b_norm_full[0], b_f=bf_pad)

    loss, dx, wgrads, vgrads = _local_step(x.reshape(S, D_MODEL), loss_target.reshape(S, D_MODEL), W, norms)

    packed = _pack_rows(wgrads, BF16).reshape(4, 2, PACK_ROWS, D_MODEL)
    got = _swap_halves("rs_sibling", packed)
    sums = _pair_sum("rs_pair_sum", packed, got, ci.reshape(1).astype(jnp.int32))
    others = _exchange_chips("rs_chips", sums)
    mine = lax.dynamic_index_in_dim(sums, 2 * xi + yi, axis=0, keepdims=False)

    outs = {}
    for name, n_rows, _ in PACK:
        parts = [_unpack(mine, name)] + [_unpack(others[k], name) for k in range(3)]
        outs[name] = _adamw("adamw_" + name, parts, w_sh[name], m_sh[name], v_sh[name], ADAM_TILE[name])

    misc = jnp.concatenate([vgrads["b_f"][:, :N_HEADS], loss[:, :1], jnp.zeros((1, D_MODEL - N_HEADS - 1), F32)], axis=1)
    small = jnp.concatenate([vgrads["a_norm"], vgrads["ffn_norm0"], vgrads["ffn_norm1"], vgrads["final_norm"],
                             vgrads["b_norm"], misc, jnp.zeros((2, D_MODEL), F32)], axis=0)
    small_all = _all_gather("gather_small", small)
    pad_vec = lambda a: jnp.pad(a, ((0, 0), (0, D_MODEL - a.shape[1])))

    def small_pack(an, fn, fin, bf):
        return jnp.concatenate([an, fn, fin.reshape(1, D_MODEL), jnp.zeros((1, D_MODEL), F32), pad_vec(bf),
                                jnp.zeros((2, D_MODEL), F32)], axis=0)

    sg, sd, sm, sv = _adamw("adamw_small", [small_all[d] for d in range(N_DEV)], small_pack(a_norm, ffn_norm, final_norm, b_f),
                            small_pack(m_a_norm, m_ffn_norm, m_final_norm, m_b_f),
                            small_pack(v_a_norm, v_ffn_norm, v_final_norm, v_b_f), 8)
    g_bn = lax.dynamic_slice(sg[4:5], (0, dev * LANES), (1, LANES))
    bn = _adamw("adamw_b_norm", [g_bn], b_norm, m_b_norm, v_b_norm, 1)

    def tree(i):
        full = lambda name, ref: outs[name][i].reshape(ref.shape)
        sml = (sg, sd, sm, sv)[i]
        return dict(
            a_norm=sml[0:1], a_w_in=full("a_w_in", a_w_in), a_w_out=full("a_w_out", a_w_out), b_norm=bn[i],
            b_w_in=full("b_w_in", b_w_in), b_f=sml[5:6, :N_HEADS], b_w_out=full("b_w_out", b_w_out), ffn_norm=sml[1:3],
            ffn_w_gu=jnp.stack([outs["gu0"][i], outs["gu1"][i]]).reshape(ffn_w_gu.shape),
            ffn_w_down=jnp.stack([outs["dn0"][i], outs["dn1"][i]]).reshape(ffn_w_down.shape), final_norm=sml[3])

    order = ("a_norm", "a_w_in", "a_w_out", "b_norm", "b_w_in", "b_f", "b_w_out", "ffn_norm", "ffn_w_gu", "ffn_w_down", "final_norm")
    result = [sg[5, N_HEADS], dx.reshape(x.shape)]
    for i in range(4):
        t = tree(i)
        result += [t[n] for n in order]
    return tuple(result)
```

```python
import functools
from typing import Callable, NamedTuple

import jax
import jax.numpy as jnp
from jax import lax
from jax.experimental import pallas as pl
from jax.experimental.pallas import tpu as pltpu

F32 = jnp.float32
BF16 = jnp.bfloat16

D_MODEL = 1024
N_HEADS = 16
HEAD_DIM = 64
N_PAIRS = N_HEADS // 2
LANES = 128
DILATED_PATTERNS = ((128, 1), (512, 4), (2048, 16))
BAND_STEPS = 128
ROT_DIM = HEAD_DIM // 4
ROPE_THETA = 500000.0
D_FF = 2816
RMS_EPS = 1e-6
NEG_INF = -1e30
SOFTMAX_SCALE = HEAD_DIM ** -0.5
N_DEV = 8
FF_BLK = 2 * D_FF // N_DEV
ADAM_LR, ADAM_B1, ADAM_B2, ADAM_EPS, ADAM_WD, ADAM_STEP = 0.001, 0.9, 0.999, 1e-08, 0.01, 10
VMEM_LIMIT = 52 * 1024 * 1024
TM = 1024
MESH = pl.DeviceIdType.MESH

NN = (((1,), (0,)), ((), ()))
NT = (((1,), (1,)), ((), ()))
TN = (((0,), (0,)), ((), ()))


def _params(*sem):
    return pltpu.CompilerParams(dimension_semantics=sem, vmem_limit_bytes=VMEM_LIMIT)


def _dot(a, b, dims):
    return lax.dot_general(a, b, dims, preferred_element_type=F32)


class _Comm(NamedTuple):
    ins: tuple
    outs: tuple
    aliases: dict
    copies: Callable
    n_remote: int
    n_local: int


def _call(name, body, grid, in_specs, out_specs, out_shape, scratch, args, sem, comm=None):
    if comm is None:
        return pl.pallas_call(body, grid=grid, in_specs=in_specs, out_specs=out_specs, out_shape=out_shape,
                              scratch_shapes=scratch, compiler_params=_params(*sem), name=name)(*args)
    n_in, n_out = len(in_specs), len(out_specs)
    n_ci, n_co = len(comm.ins), len(comm.outs)
    o0 = n_in + n_ci

    def hosted(*refs):
        c_ins, c_outs = refs[n_in:o0], refs[o0 + n_out:o0 + n_out + n_co]
        sems = refs[-3:]

        def start():
            for cp in comm.copies(c_ins, c_outs, *sems):
                cp.start()

        def wait():
            for cp in comm.copies(c_ins, c_outs, *sems):
                cp.wait()

        if not grid:
            start()
            body()
            wait()
            return
        ids = [pl.program_id(ax) for ax in range(len(grid))]
        pl.when(functools.reduce(jnp.logical_and, [i == 0 for i in ids]))(start)
        body(*refs[:n_in], *refs[o0:o0 + n_out], *refs[o0 + n_out + n_co:-3])
        pl.when(functools.reduce(jnp.logical_and, [i == g - 1 for i, g in zip(ids, grid)]))(wait)

    hbm = pl.BlockSpec(memory_space=pltpu.HBM)
    dma = pltpu.SemaphoreType.DMA
    return pl.pallas_call(
        hosted, grid=grid, in_specs=[*in_specs, *[hbm] * n_ci], out_specs=[*out_specs, *[hbm] * n_co],
        out_shape=[*out_shape, *comm.outs], input_output_aliases={n_in + i: n_out + o for i, o in comm.aliases.items()},
        scratch_shapes=[*scratch, dma((comm.n_remote,)), dma((comm.n_remote,)), dma((max(comm.n_local, 1),))],
        compiler_params=_params(*["arbitrary"] * len(grid)), name=name)(*args, *comm.ins)


def _mm_call(name, grid, a, a_spec, b, b_spec, dims, out_shapes, out_specs, acc_shape, epilogue=None,
             extras=(), extra_specs=(), col_axis=1, comm=None):
    nk = grid[2]
    n_extra = len(extras)
    n_out = len(out_shapes)

    def finish(res, ex, outs, j):
        if epilogue is None:
            outs[0][...] = res.astype(outs[0].dtype)
        else:
            epilogue(res, ex, outs, j)

    def body(*refs):
        a_ref, b_ref = refs[0], refs[1]
        ex = refs[2:2 + n_extra]
        outs = refs[2 + n_extra:2 + n_extra + n_out]
        j, k = pl.program_id(col_axis), pl.program_id(2)
        part = _dot(a_ref[...].astype(BF16), b_ref[...].astype(BF16), dims)
        if nk == 1:
            finish(part, ex, outs, j)
            return
        acc = refs[-1]

        @pl.when(k == 0)
        def _():
            acc[...] = part

        @pl.when((k > 0) & (k < nk - 1))
        def _():
            acc[...] += part

        @pl.when(k == nk - 1)
        def _():
            finish(acc[...] + part, ex, outs, j)

    return _call(name, body, grid, [a_spec, b_spec, *extra_specs], out_specs, out_shapes,
                 [] if nk == 1 else [pltpu.VMEM(acc_shape, F32)], (a, b, *extras), ("parallel", "parallel", "arbitrary"), comm)


def _matmul(name, a, b, mode, out_dtype, tm, tn, tk, resid=None, col0_scale=None, comm=None):
    if mode == "nn":
        (M, K), N = a.shape, b.shape[1]
        a_spec = pl.BlockSpec((tm, tk), lambda j, i, k: (i, k))
        b_spec = pl.BlockSpec((tk, tn), lambda j, i, k: (k, j))
        dims = NN
    elif mode == "nt":
        (M, K), N = a.shape, b.shape[0]
        a_spec = pl.BlockSpec((tm, tk), lambda j, i, k: (i, k))
        b_spec = pl.BlockSpec((tn, tk), lambda j, i, k: (j, k))
        dims = NT
    else:
        (K, M), N = a.shape, b.shape[1]
        a_spec = pl.BlockSpec((tk, tm), lambda j, i, k: (k, i))
        b_spec = pl.BlockSpec((tk, tn), lambda j, i, k: (k, j))
        dims = TN
    assert M % tm == 0 and N % tn == 0 and K % tk == 0, (name, M, N, K, tm, tn, tk)
    o_spec = pl.BlockSpec((tm, tn), lambda j, i, k: (i, j))
    extras, extra_specs, epilogue = (), (), None
    if resid is not None:
        extras, extra_specs = (resid,), (o_spec,)

        def epilogue(acc, ex, outs, j):
            outs[0][...] = (acc + ex[0][...]).astype(outs[0].dtype)

    elif col0_scale is not None:

        def epilogue(acc, ex, outs, j):
            outs[0][...] = (acc * jnp.where(j == 0, col0_scale, 1.0)).astype(outs[0].dtype)

    res = _mm_call(name, (N // tn, M // tm, K // tk), a, a_spec, b, b_spec, dims, [jax.ShapeDtypeStruct((M, N), out_dtype)],
                   [o_spec], (tm, tn), epilogue, extras, extra_specs, col_axis=0, comm=comm)
    return res[0] if comm is None else (res[0], res[1:])


def _rms_fwd(name, h, gain, tm=512):
    S, D = h.shape

    def body(h_ref, g_ref, n_ref):
        x = h_ref[...]
        rstd = lax.rsqrt(jnp.mean(x * x, axis=-1, keepdims=True) + RMS_EPS)
        n_ref[...] = (x * rstd * g_ref[...]).astype(BF16)

    return pl.pallas_call(
        body, grid=(S // tm,), in_specs=[pl.BlockSpec((tm, D), lambda i: (i, 0)), pl.BlockSpec((1, D), lambda i: (0, 0))],
        out_specs=pl.BlockSpec((tm, D), lambda i: (i, 0)), out_shape=jax.ShapeDtypeStruct((S, D), BF16),
        compiler_params=_params("parallel"), name=name)(h, gain.reshape(1, D))


def _rms_bwd(name, dn, h, gain, dres, tm=512):
    S, D = h.shape

    def body(dn_ref, h_ref, g_ref, r_ref, dh_ref, dg_ref):
        x = h_ref[...]
        rstd = lax.rsqrt(jnp.mean(x * x, axis=-1, keepdims=True) + RMS_EPS)
        xhat = x * rstd
        d = dn_ref[...]
        dxhat = d * g_ref[...]
        dh_ref[...] = rstd * (dxhat - xhat * jnp.mean(dxhat * xhat, axis=-1, keepdims=True)) + r_ref[...]

        @pl.when(pl.program_id(0) == 0)
        def _():
            dg_ref[...] = jnp.zeros_like(dg_ref)

        dg_ref[...] += jnp.sum(d * xhat, axis=0, keepdims=True)

    row = pl.BlockSpec((tm, D), lambda i: (i, 0))
    vec = pl.BlockSpec((1, D), lambda i: (0, 0))
    return pl.pallas_call(
        body, grid=(S // tm,), in_specs=[row, row, vec, row], out_specs=[row, vec],
        out_shape=[jax.ShapeDtypeStruct((S, D), F32), jax.ShapeDtypeStruct((1, D), F32)],
        compiler_params=_params("arbitrary"), name=name)(dn, h, gain.reshape(1, D), dres)


def _loss_head(name, h, gain, target, tm=512):
    S, D = h.shape

    def body(h_ref, g_ref, t_ref, dh_ref, dg_ref, loss_ref):
        x = h_ref[...]
        rstd = lax.rsqrt(jnp.mean(x * x, axis=-1, keepdims=True) + RMS_EPS)
        xhat = x * rstd
        err = xhat * g_ref[...] - t_ref[...]
        dy = err * (1.0 / D)
        dxhat = dy * g_ref[...]
        dh_ref[...] = rstd * (dxhat - xhat * jnp.mean(dxhat * xhat, axis=-1, keepdims=True))

        @pl.when(pl.program_id(0) == 0)
        def _():
            dg_ref[...] = jnp.zeros_like(dg_ref)
            loss_ref[...] = jnp.zeros_like(loss_ref)

        dg_ref[...] += jnp.sum(dy * xhat, axis=0, keepdims=True)
        part = 0.5 * jnp.sum(jnp.mean(err * err, axis=-1, keepdims=True), axis=0, keepdims=True)
        loss_ref[...] += jnp.broadcast_to(part, loss_ref.shape)

    row = pl.BlockSpec((tm, D), lambda i: (i, 0))
    vec = pl.BlockSpec((1, D), lambda i: (0, 0))
    return pl.pallas_call(
        body, grid=(S // tm,), in_specs=[row, vec, row], out_specs=[row, vec, pl.BlockSpec((1, LANES), lambda i: (0, 0))],
        out_shape=[jax.ShapeDtypeStruct((S, D), F32), jax.ShapeDtypeStruct((1, D), F32),
                   jax.ShapeDtypeStruct((1, LANES), F32)],
        compiler_params=_params("arbitrary"), name=name)(h, gain.reshape(1, D), target)


def _rope_tables(S):
    half = ROT_DIM // 2
    inv_freq = ROPE_THETA ** (-jnp.arange(half, dtype=F32) * 2.0 / ROT_DIM)
    ang = jnp.arange(S, dtype=F32)[:, None] * inv_freq[None, :]
    cos, sin = jnp.cos(ang), jnp.sin(ang)
    one = jnp.ones((S, HEAD_DIM - ROT_DIM), F32)
    zero = jnp.zeros((S, HEAD_DIM - ROT_DIM), F32)
    zh = jnp.zeros((S, half), F32)
    c = jnp.concatenate([cos, cos, one], axis=1)
    sa = jnp.concatenate([-sin, zh, zero], axis=1)
    sb = jnp.concatenate([zh, sin, zero], axis=1)
    return tuple(jnp.concatenate([t, t], axis=1) for t in (c, sa, sb))


def _rotate(x, c, sa, sb, sign):
    return x * c + sign * (pltpu.roll(x, LANES - ROT_DIM // 2, 1) * sa + pltpu.roll(x, ROT_DIM // 2, 1) * sb)


def _a_proj(name, n, w, tabs, comm, tm=1024):
    S, D = n.shape
    tn = w.shape[2]

    def epilogue(acc, ex, outs, d):
        c, sa, sb = ex[0][...], ex[1][...], ex[2][...]
        for b in range(tn // LANES):
            cols = slice(b * LANES, (b + 1) * LANES)
            kind = ((d * (tn // LANES) + b) // N_PAIRS) % 3
            x = acc[:, cols]
            rot = _rotate(x, c, sa, sb, 1.0) * jnp.where(kind == 0, SOFTMAX_SCALE, 1.0)
            outs[0][:, cols] = jnp.where(kind == 2, x, rot).astype(BF16)

    tab = pl.BlockSpec((tm, LANES), lambda d, i, k: (i, 0))
    res = _mm_call(name, (N_DEV, S // tm, 1), n, pl.BlockSpec((tm, D), lambda d, i, k: (i, 0)),
                   w, pl.BlockSpec((None, D, tn), lambda d, i, k: (d, 0, 0)), NN,
                   [jax.ShapeDtypeStruct((S, N_DEV * tn), BF16)], [pl.BlockSpec((tm, tn), lambda d, i, k: (i, d))], (tm, tn),
                   epilogue, tabs, (tab, tab, tab), col_axis=0, comm=comm)
    return res[0], res[1:]


def _lo_lanes():
    return lax.broadcasted_iota(jnp.int32, (1, LANES), 1) < HEAD_DIM


def _rep_rows(x2, lo):
    sw = pltpu.roll(x2, HEAD_DIM, 1)
    return jnp.where(lo, x2, sw), jnp.where(lo, sw, x2)


def _pair_cols(h):
    return slice((h // 2) * LANES, (h // 2 + 1) * LANES)


def _head_lanes(lo, h):
    return lo if h % 2 == 0 else jnp.logical_not(lo)


def _band_masks(t, first):
    ri = lax.broadcasted_iota(jnp.int32, (t, t), 0)
    ci = lax.broadcasted_iota(jnp.int32, (t, t), 1)
    neg_prev = jnp.where((ci >= ri) & jnp.logical_not(first), 0.0, NEG_INF)
    neg_cur = jnp.where(ci <= ri, 0.0, NEG_INF)
    return neg_prev, neg_cur


def _dil_specs(L, R, t, qcol, kcol, vcol):
    W = D_MODEL
    prev = lambda qi: jnp.maximum(qi - 1, 0)
    return dict(
        q=pl.BlockSpec((t, W), lambda r, qi: (qi, qcol(r))),
        kp=pl.BlockSpec((t, W), lambda r, qi: (prev(qi), kcol(r))), kc=pl.BlockSpec((t, W), lambda r, qi: (qi, kcol(r))),
        vp=pl.BlockSpec((t, W), lambda r, qi: (prev(qi), vcol(r))), vc=pl.BlockSpec((t, W), lambda r, qi: (qi, vcol(r))),
        own=pl.BlockSpec((t, W), lambda r, qi: (qi, r)), tab=pl.BlockSpec((t, LANES), lambda r, qi: (qi, r)))


def _dil_fwd(name, x, qcol, kcol, vcol, R, L):
    t = BAND_STEPS
    W = D_MODEL
    sp = _dil_specs(L, R, t, qcol, kcol, vcol)

    def body(q_ref, kp_ref, kc_ref, vp_ref, vc_ref, o_ref, lse_ref):
        lo = _lo_lanes()
        neg_p, neg_c = _band_masks(t, pl.program_id(1) == 0)
        s_p, s_c = [], []
        for h in range(N_HEADS):
            cols = _pair_cols(h)
            qh = jnp.where(_head_lanes(lo, h), q_ref[:, cols], 0)
            s_p.append(_dot(qh, kp_ref[:, cols], NT))
            s_c.append(_dot(qh, kc_ref[:, cols], NT))
        s_p = jnp.stack(s_p) + neg_p[None]
        s_c = jnp.stack(s_c) + neg_c[None]
        m = jnp.maximum(jnp.max(s_p, axis=2, keepdims=True), jnp.max(s_c, axis=2, keepdims=True))
        p_p, p_c = jnp.exp(s_p - m), jnp.exp(s_c - m)
        l = jnp.sum(p_p, axis=2, keepdims=True) + jnp.sum(p_c, axis=2, keepdims=True)
        inv, lse = 1.0 / l, m + jnp.log(l)
        p_p, p_c = p_p.astype(BF16), p_c.astype(BF16)
        for p in range(N_PAIRS):
            cols = _pair_cols(2 * p)
            o2 = jnp.zeros((t, LANES), F32)
            for h in (2 * p, 2 * p + 1):
                hm = _head_lanes(lo, h)
                pv = _dot(p_p[h], jnp.where(hm, vp_ref[:, cols], 0), NN) + _dot(p_c[h], jnp.where(hm, vc_ref[:, cols], 0), NN)
                o2 = o2 + pv * inv[h]
            o_ref[:, cols] = o2
            lse_ref[:, cols] = jnp.where(lo, lse[2 * p], lse[2 * p + 1])

    return pl.pallas_call(
        body, grid=(R, L // t), in_specs=[sp["q"], sp["kp"], sp["kc"], sp["vp"], sp["vc"]], out_specs=[sp["own"], sp["own"]],
        out_shape=[jax.ShapeDtypeStruct((L, R * W), F32), jax.ShapeDtypeStruct((L, R * W), F32)],
        compiler_params=_params("parallel", "parallel"), name=name)(x, x, x, x, x)


def _dil_scores(lo, q_ref, do_ref, o_ref, lse_ref, kv_refs):
    s = [[] for _ in kv_refs]
    dp = [[] for _ in kv_refs]
    lse, d = [], []
    for h in range(N_HEADS):
        cols = _pair_cols(h)
        hm = _head_lanes(lo, h)
        qh, doh = jnp.where(hm, q_ref[:, cols], 0), jnp.where(hm, do_ref[:, cols], 0)
        for i, (k_ref, v_ref) in enumerate(kv_refs):
            s[i].append(_dot(qh, k_ref[:, cols], NT))
            dp[i].append(_dot(doh, v_ref[:, cols], NT))
        lse.append(_rep_rows(lse_ref[:, cols], lo)[h % 2])
        dd = do_ref[:, cols].astype(F32) * o_ref[:, cols].astype(F32)
        d.append(jnp.sum(jnp.where(hm, dd, 0.0), axis=1, keepdims=True))
    return (*[jnp.stack(x) for x in s], *[jnp.stack(x) for x in dp], jnp.stack(lse), jnp.stack(d))


def _dil_dq(name, x, do, o, lse, tabs, qcol, kcol, vcol, R, L):
    t = BAND_STEPS
    W = D_MODEL
    sp = _dil_specs(L, R, t, qcol, kcol, vcol)

    def body(q_ref, kp_ref, kc_ref, vp_ref, vc_ref, do_ref, o_ref, lse_ref, c_ref, sa_ref, sb_ref, dq_ref):
        lo = _lo_lanes()
        neg_p, neg_c = _band_masks(t, pl.program_id(1) == 0)
        s_p, s_c, dp_p, dp_c, lse, d = _dil_scores(lo, q_ref, do_ref, o_ref, lse_ref, ((kp_ref, vp_ref), (kc_ref, vc_ref)))
        ds_p = (jnp.exp(s_p + neg_p[None] - lse) * (dp_p - d)).astype(BF16)
        ds_c = (jnp.exp(s_c + neg_c[None] - lse) * (dp_c - d)).astype(BF16)
        for p in range(N_PAIRS):
            cols = _pair_cols(2 * p)
            dq2 = jnp.zeros((t, LANES), F32)
            for h in (2 * p, 2 * p + 1):
                hm = _head_lanes(lo, h)
                dq2 = dq2 + _dot(ds_p[h], jnp.where(hm, kp_ref[:, cols], 0), NN) + _dot(ds_c[h], jnp.where(hm, kc_ref[:, cols], 0), NN)
            dq_ref[:, cols] = _rotate(dq2 * SOFTMAX_SCALE, c_ref[...], sa_ref[...], sb_ref[...], -1.0).astype(BF16)

    return pl.pallas_call(
        body, grid=(R, L // t),
        in_specs=[sp["q"], sp["kp"], sp["kc"], sp["vp"], sp["vc"], sp["own"], sp["own"], sp["own"], sp["tab"], sp["tab"], sp["tab"]],
        out_specs=sp["own"], out_shape=jax.ShapeDtypeStruct((L, R * W), BF16),
        compiler_params=_params("parallel", "parallel"), name=name)(x, x, x, x, x, do, o, lse, *tabs)


def _dil_dkv(name, x, do, o, lse, tabs, qcol, kcol, vcol, R, L):
    t = BAND_STEPS
    W = D_MODEL
    nq = L // t
    nxt = lambda kb: jnp.minimum(kb + 1, nq - 1)
    cur_q = pl.BlockSpec((t, W), lambda r, kb: (kb, qcol(r)))
    nxt_q = pl.BlockSpec((t, W), lambda r, kb: (nxt(kb), qcol(r)))
    cur_o = pl.BlockSpec((t, W), lambda r, kb: (kb, r))
    nxt_o = pl.BlockSpec((t, W), lambda r, kb: (nxt(kb), r))
    tab = pl.BlockSpec((t, LANES), lambda r, kb: (kb, r))

    def body(k_ref, v_ref, qc_ref, qn_ref, doc_ref, don_ref, oc_ref, on_ref, lc_ref, ln_ref, c_ref, sa_ref, sb_ref,
             dk_ref, dv_ref):
        lo = _lo_lanes()
        ri = lax.broadcasted_iota(jnp.int32, (t, t), 0)
        ci = lax.broadcasted_iota(jnp.int32, (t, t), 1)
        neg_c = jnp.where(ci <= ri, 0.0, NEG_INF)
        neg_n = jnp.where((ci >= ri) & (pl.program_id(1) + 1 < nq), 0.0, NEG_INF)
        blocks = []
        for q_ref, do_ref, o_ref, l_ref, neg in ((qc_ref, doc_ref, oc_ref, lc_ref, neg_c), (qn_ref, don_ref, on_ref, ln_ref, neg_n)):
            s, dp, lse, d = _dil_scores(lo, q_ref, do_ref, o_ref, l_ref, ((k_ref, v_ref),))
            pr = jnp.exp(s + neg[None] - lse)
            blocks.append((q_ref, do_ref, pr.astype(BF16), (pr * (dp - d)).astype(BF16)))
        for p in range(N_PAIRS):
            cols = _pair_cols(2 * p)
            dk2 = jnp.zeros((t, LANES), F32)
            dv2 = jnp.zeros((t, LANES), F32)
            for q_ref, do_ref, pr, ds in blocks:
                for h in (2 * p, 2 * p + 1):
                    hm = _head_lanes(lo, h)
                    dv2 = dv2 + _dot(pr[h], jnp.where(hm, do_ref[:, cols], 0), TN)
                    dk2 = dk2 + _dot(ds[h], jnp.where(hm, q_ref[:, cols], 0), TN)
            dk_ref[:, cols] = _rotate(dk2, c_ref[...], sa_ref[...], sb_ref[...], -1.0).astype(BF16)
            dv_ref[:, cols] = dv2.astype(BF16)

    kcur = pl.BlockSpec((t, W), lambda r, kb: (kb, kcol(r)))
    vcur = pl.BlockSpec((t, W), lambda r, kb: (kb, vcol(r)))
    return pl.pallas_call(
        body, grid=(R, nq),
        in_specs=[kcur, vcur, cur_q, nxt_q, cur_o, nxt_o, cur_o, nxt_o, cur_o, nxt_o, tab, tab, tab],
        out_specs=[cur_o, cur_o], out_shape=[jax.ShapeDtypeStruct((L, R * W), BF16)] * 2,
        compiler_params=_params("parallel", "parallel"), name=name)(x, x, x, x, do, do, o, o, lse, lse, *tabs)


def _fox_operands(q2, k2, kb2, lo, hh):
    lane = lax.broadcasted_iota(jnp.int32, (1, LANES), 1)
    if hh == 0:
        ones = ((lane >= HEAD_DIM) & (lane < HEAD_DIM + 3)).astype(BF16)
        return jnp.where(lo, q2, ones), jnp.where(lo, k2, kb2)
    ones = (lane < 3).astype(BF16)
    return jnp.where(lo, ones, q2), jnp.where(lo, kb2, k2)


def _causal_neg(t):
    ri = lax.broadcasted_iota(jnp.int32, (t, t), 0)
    ci = lax.broadcasted_iota(jnp.int32, (t, t), 1)
    return jnp.where(ci <= ri, 0.0, NEG_INF)


def _fox_fwd(name, qkv, kbias, t):
    S = qkv.shape[0]
    W = D_MODEL
    nq = S // t
    rep = t // LANES

    def body(q_ref, k_ref, v_ref, kb_ref, o_ref, lse_ref, m_scr, l_scr, acc_scr):
        qi, j = pl.program_id(0), pl.program_id(1)
        lo = _lo_lanes()

        @pl.when(j == 0)
        def _():
            m_scr[...] = jnp.full_like(m_scr, NEG_INF)
            l_scr[...] = jnp.zeros_like(l_scr)
            acc_scr[...] = jnp.zeros_like(acc_scr)

        def step(masked):
            neg = _causal_neg(t) if masked else None

            def pair(p, carry):
                cs = pl.ds(pl.multiple_of(p * LANES, LANES), LANES)
                q2, k2, v2, kb2 = q_ref[:, cs], k_ref[:, cs], v_ref[:, cs], kb_ref[:, cs]
                pvs, alphas = [], []
                for hh in range(2):
                    hm = lo if hh == 0 else jnp.logical_not(lo)
                    qh, kh = _fox_operands(q2, k2, kb2, lo, hh)
                    s = _dot(qh, kh, NT)
                    if masked:
                        s = s + neg
                    h = 2 * p + hh
                    m_prev = m_scr[h]
                    m_new = jnp.maximum(m_prev, jnp.max(s, axis=1, keepdims=True))
                    pe = jnp.exp(s - jnp.tile(m_new, (1, rep)))
                    alpha = jnp.exp(m_prev - m_new)
                    l_scr[h] = alpha * l_scr[h] + jnp.sum(pe, axis=1, keepdims=True)
                    m_scr[h] = m_new
                    pvs.append(_dot(pe.astype(BF16), jnp.where(hm, v2, 0), NN))
                    alphas.append(alpha)
                acc_scr[:, cs] = acc_scr[:, cs] * jnp.where(lo, alphas[0], alphas[1]) + pvs[0] + pvs[1]
                return carry

            lax.fori_loop(0, N_PAIRS, pair, 0)

        @pl.when(j < qi)
        def _():
            step(False)

        @pl.when(j == qi)
        def _():
            step(True)

        @pl.when(j == nq - 1)
        def _():
            for p in range(N_PAIRS):
                cols = slice(p * LANES, (p + 1) * LANES)
                l2 = jnp.where(lo, l_scr[2 * p], l_scr[2 * p + 1])
                m2 = jnp.where(lo, m_scr[2 * p], m_scr[2 * p + 1])
                o_ref[:, cols] = (acc_scr[:, cols] / l2).astype(BF16)
                lse_ref[:, cols] = m2 + jnp.log(l2)

    kv = lambda col: pl.BlockSpec((t, W), lambda qi, j: (jnp.minimum(j, qi), col))
    own = pl.BlockSpec((t, W), lambda qi, j: (qi, 0))
    return pl.pallas_call(
        body, grid=(nq, nq), in_specs=[own, kv(1), kv(2), kv(0)], out_specs=[own, own],
        out_shape=[jax.ShapeDtypeStruct((S, W), BF16), jax.ShapeDtypeStruct((S, W), F32)],
        scratch_shapes=[pltpu.VMEM((N_HEADS, t, LANES), F32), pltpu.VMEM((N_HEADS, t, LANES), F32), pltpu.VMEM((t, W), F32)],
        compiler_params=_params("parallel", "arbitrary"), name=name)(qkv, qkv, qkv, kbias)


def _fox_head_grads(qh, kh, v2, doh, neg, lse_h, d_h, rep):
    s = _dot(qh, kh, NT)
    if neg is not None:
        s = s + neg
    p = jnp.exp(s - jnp.tile(lse_h, (1, rep)))
    return p, p * (_dot(doh, v2, NT) - d_h)


def _fox_dq(name, qkv, kbias, do, o, lse, t):
    S = qkv.shape[0]
    W = D_MODEL
    nq = S // t
    rep = t // LANES

    def body(q_ref, k_ref, v_ref, kb_ref, do_ref, o_ref, lse_ref, dq_ref, rs_ref, dq_scr, rs_scr):
        qi, j = pl.program_id(0), pl.program_id(1)
        lo = _lo_lanes()

        @pl.when(j == 0)
        def _():
            dq_scr[...] = jnp.zeros_like(dq_scr)
            rs_scr[...] = jnp.zeros_like(rs_scr)

        def step(masked):
            neg = _causal_neg(t) if masked else None

            def pair(p, carry):
                cs = pl.ds(pl.multiple_of(p * LANES, LANES), LANES)
                q2, k2, v2, kb2, do2 = q_ref[:, cs], k_ref[:, cs], v_ref[:, cs], kb_ref[:, cs], do_ref[:, cs]
                dd = do2.astype(F32) * o_ref[:, cs].astype(F32)
                lse_h = _rep_rows(lse_ref[:, cs], lo)
                dq2 = jnp.zeros((t, LANES), F32)
                for hh in range(2):
                    hm = lo if hh == 0 else jnp.logical_not(lo)
                    qh, kh = _fox_operands(q2, k2, kb2, lo, hh)
                    d_h = jnp.sum(jnp.where(hm, dd, 0.0), axis=1, keepdims=True)
                    _, ds = _fox_head_grads(qh, kh, v2, jnp.where(hm, do2, 0), neg, lse_h[hh], d_h, rep)
                    rs_scr[2 * p + hh] += jnp.broadcast_to(jnp.sum(ds, axis=1, keepdims=True), (t, LANES))
                    dq2 = dq2 + _dot(ds.astype(BF16), jnp.where(hm, k2, 0), NN)
                dq_scr[:, cs] += dq2
                return carry

            lax.fori_loop(0, N_PAIRS, pair, 0)

        @pl.when(j < qi)
        def _():
            step(False)

        @pl.when(j == qi)
        def _():
            step(True)

        @pl.when(j == nq - 1)
        def _():
            dq_ref[...] = (dq_scr[...] * SOFTMAX_SCALE).astype(BF16)
            for p in range(N_PAIRS):
                rs_ref[:, p * LANES:(p + 1) * LANES] = jnp.where(lo, rs_scr[2 * p], rs_scr[2 * p + 1])

    kv = lambda col: pl.BlockSpec((t, W), lambda qi, j: (jnp.minimum(j, qi), col))
    own = pl.BlockSpec((t, W), lambda qi, j: (qi, 0))
    return pl.pallas_call(
        body, grid=(nq, nq), in_specs=[own, kv(1), kv(2), kv(0), own, own, own], out_specs=[own, own],
        out_shape=[jax.ShapeDtypeStruct((S, W), BF16), jax.ShapeDtypeStruct((S, W), F32)],
        scratch_shapes=[pltpu.VMEM((t, W), F32), pltpu.VMEM((N_HEADS, t, LANES), F32)],
        compiler_params=_params("parallel", "arbitrary"), name=name)(qkv, qkv, qkv, kbias, do, o, lse)


def _fox_dkv(name, qkv, kbias, do, o, lse, t):
    S = qkv.shape[0]
    W = D_MODEL
    nq = S // t
    rep = t // LANES

    def body(q_ref, k_ref, v_ref, kb_ref, do_ref, o_ref, lse_ref, dk_ref, dv_ref, dc_ref, dk_scr, dv_scr):
        kb, j = pl.program_id(0), pl.program_id(1)
        lo = _lo_lanes()

        @pl.when(j == 0)
        def _():
            dk_scr[...] = jnp.zeros_like(dk_scr)
            dv_scr[...] = jnp.zeros_like(dv_scr)
            dc_ref[...] = jnp.zeros_like(dc_ref)

        def step(masked):
            neg = _causal_neg(t) if masked else None

            def pair(p, carry):
                cs = pl.ds(pl.multiple_of(p * LANES, LANES), LANES)
                q2, k2, v2, kb2, do2 = q_ref[:, cs], k_ref[:, cs], v_ref[:, cs], kb_ref[:, cs], do_ref[:, cs]
                dd = do2.astype(F32) * o_ref[:, cs].astype(F32)
                lse_h = _rep_rows(lse_ref[:, cs], lo)
                dv2 = jnp.zeros((t, LANES), F32)
                dk2 = jnp.zeros((t, LANES), F32)
                for hh in range(2):
                    hm = lo if hh == 0 else jnp.logical_not(lo)
                    qh, kh = _fox_operands(q2, k2, kb2, lo, hh)
                    doh = jnp.where(hm, do2, 0)
                    d_h = jnp.sum(jnp.where(hm, dd, 0.0), axis=1, keepdims=True)
                    pr, ds = _fox_head_grads(qh, kh, v2, doh, neg, lse_h[hh], d_h, rep)
                    dc_ref[p, hh:hh + 1, :] += jnp.sum(ds, axis=0, keepdims=True)
                    dv2 = dv2 + _dot(pr.astype(BF16), doh, TN)
                    dk2 = dk2 + _dot(ds.astype(BF16), jnp.where(hm, q2, 0), TN)
                dv_scr[:, cs] += dv2
                dk_scr[:, cs] += dk2
                return carry

            lax.fori_loop(0, N_PAIRS, pair, 0)

        @pl.when(j > kb)
        def _():
            step(False)

        @pl.when(j == kb)
        def _():
            step(True)

        @pl.when(j == nq - 1)
        def _():
            dv_ref[...] = dv_scr[...].astype(BF16)
            dk_ref[...] = dk_scr[...].astype(BF16)

    qrow = pl.BlockSpec((t, W), lambda kb, j: (jnp.maximum(j, kb), 0))
    krow = lambda col: pl.BlockSpec((t, W), lambda kb, j: (kb, col))
    own = pl.BlockSpec((t, W), lambda kb, j: (kb, 0))
    return pl.pallas_call(
        body, grid=(nq, nq), in_specs=[qrow, krow(1), krow(2), krow(0), qrow, qrow, qrow],
        out_specs=[own, own, pl.BlockSpec((N_PAIRS, 2, t), lambda kb, j: (0, 0, kb))],
        out_shape=[jax.ShapeDtypeStruct((S, W), BF16), jax.ShapeDtypeStruct((S, W), BF16), jax.ShapeDtypeStruct((N_PAIRS, 2, S), F32)],
        scratch_shapes=[pltpu.VMEM((t, W), F32), pltpu.VMEM((t, W), F32)],
        compiler_params=_params("parallel", "arbitrary"), name=name)(qkv, qkv, qkv, kbias, do, o, lse)


def _combine(name, os_, lses, tm=256):
    S, W = os_[0].shape
    G = len(os_)

    def body(*refs):
        o_refs, l_refs = refs[:G], refs[G:2 * G]
        o_ref, lse_ref = refs[2 * G:]
        ls = [r[...] for r in l_refs]
        m = functools.reduce(jnp.maximum, ls)
        ws = [jnp.exp(l - m) for l in ls]
        den = functools.reduce(jnp.add, ws)
        num = functools.reduce(jnp.add, [w * r[...] for w, r in zip(ws, o_refs)])
        o_ref[...] = (num / den).astype(BF16)
        lse_ref[...] = m + jnp.log(den)

    row = pl.BlockSpec((tm, W), lambda i: (i, 0))
    return pl.pallas_call(
        body, grid=(S // tm,), in_specs=[row] * (2 * G), out_specs=[row, row],
        out_shape=[jax.ShapeDtypeStruct((S, W), BF16), jax.ShapeDtypeStruct((S, W), F32)],
        compiler_params=_params("parallel"), name=name)(*os_, *lses)


def _tri_matmul(tri, x):
    hi, mid, lo = _split3(x)
    return _dot(tri, hi, NN) + _dot(tri, mid, NN) + _dot(tri, lo, NN)


def _split3(x):
    hi = x.astype(BF16)
    r1 = x - hi.astype(F32)
    mid = r1.astype(BF16)
    return hi, mid, (r1 - mid.astype(F32)).astype(BF16)


def _gate_fwd(name, z, bf, tb=512):
    S = z.shape[0]

    def body(z_ref, b_ref, kb_ref, carry):
        @pl.when(pl.program_id(0) == 0)
        def _():
            carry[...] = jnp.zeros_like(carry)

        lf = jax.nn.log_sigmoid(z_ref[...] + b_ref[...])
        ri = lax.broadcasted_iota(jnp.int32, (tb, tb), 0)
        ci = lax.broadcasted_iota(jnp.int32, (tb, tb), 1)
        tri = (ci <= ri).astype(BF16)
        c = _tri_matmul(tri, lf) + carry[...]
        carry[...] = c[tb - 1:tb, :]
        head = lax.broadcasted_iota(jnp.int32, (LANES, D_MODEL), 0)
        col = lax.broadcasted_iota(jnp.int32, (LANES, D_MODEL), 1)
        base = (head >> 1) * LANES + jnp.where((head & 1) == 0, HEAD_DIM, 0)
        kb = jnp.zeros((tb, D_MODEL), F32)
        for i, piece in enumerate(_split3(-c)):
            place = ((col == base + i) & (head < N_HEADS)).astype(BF16)
            kb = kb + _dot(piece, place, NN)
        kb_ref[...] = kb.astype(BF16)

    row = pl.BlockSpec((tb, LANES), lambda i: (i, 0))
    return pl.pallas_call(
        body, grid=(S // tb,), in_specs=[row, pl.BlockSpec((1, LANES), lambda i: (0, 0))],
        out_specs=pl.BlockSpec((tb, D_MODEL), lambda i: (i, 0)), out_shape=jax.ShapeDtypeStruct((S, D_MODEL), BF16),
        scratch_shapes=[pltpu.VMEM((1, LANES), F32)], compiler_params=_params("arbitrary"), name=name)(z, bf)


def _gate_bwd(name, dc, z, bf, tb=512):
    S = z.shape[0]
    nb = S // tb

    def body(dc_ref, z_ref, b_ref, dz_ref, db_ref, carry):
        @pl.when(pl.program_id(0) == 0)
        def _():
            carry[...] = jnp.zeros_like(carry)
            db_ref[...] = jnp.zeros_like(db_ref)

        ri = lax.broadcasted_iota(jnp.int32, (tb, tb), 0)
        ci = lax.broadcasted_iota(jnp.int32, (tb, tb), 1)
        tri = (ci >= ri).astype(BF16)
        dlf = _tri_matmul(tri, dc_ref[...]) + carry[...]
        carry[...] = dlf[0:1, :]
        dz = dlf * jax.nn.sigmoid(-(z_ref[...] + b_ref[...]))
        dz_ref[...] = dz
        db_ref[...] += jnp.sum(dz, axis=0, keepdims=True)

    row = pl.BlockSpec((tb, LANES), lambda i: (nb - 1 - i, 0))
    vec = pl.BlockSpec((1, LANES), lambda i: (0, 0))
    return pl.pallas_call(
        body, grid=(nb,), in_specs=[row, row, vec], out_specs=[row, vec],
        out_shape=[jax.ShapeDtypeStruct((S, LANES), F32), jax.ShapeDtypeStruct((1, LANES), F32)],
        scratch_shapes=[pltpu.VMEM((1, LANES), F32)], compiler_params=_params("arbitrary"), name=name)(dc, z, bf)


def _ffn_gu(name, n, wgu, comm=None, tm=1024):
    S, D = n.shape
    nb = N_DEV // 2

    def body(n_ref, wg_ref, wu_ref, gu_ref, act_ref):
        x = n_ref[...]
        g = _dot(x, wg_ref[...], NN)
        u = _dot(x, wu_ref[...], NN)
        gu_ref[0] = g.astype(BF16)
        gu_ref[1] = u.astype(BF16)
        act_ref[...] = (g * jax.nn.sigmoid(g) * u).astype(BF16)

    return _call(
        name, body, (nb, S // tm),
        [pl.BlockSpec((tm, D), lambda j, i: (i, 0)), pl.BlockSpec((None, D, FF_BLK), lambda j, i: (j, 0, 0)),
         pl.BlockSpec((None, D, FF_BLK), lambda j, i: (j + nb, 0, 0))],
        [pl.BlockSpec((2, None, tm, FF_BLK), lambda j, i: (0, j, i, 0)), pl.BlockSpec((None, tm, FF_BLK), lambda j, i: (j, i, 0))],
        [jax.ShapeDtypeStruct((2, nb, S, FF_BLK), BF16), jax.ShapeDtypeStruct((nb, S, FF_BLK), BF16)], [],
        (n, wgu, wgu), ("parallel", "parallel"), comm)


def _ffn_down(name, act, wd, resid, comm=None, tm=1024):
    nb, S, _ = act.shape
    D = wd.shape[1]

    def epilogue(acc, ex, outs, j):
        outs[0][...] = acc + ex[0][...]

    o_spec = pl.BlockSpec((tm, D), lambda i, j, k: (i, 0))
    return _mm_call(name, (S // tm, 1, nb), act, pl.BlockSpec((None, tm, FF_BLK), lambda i, j, k: (k, i, 0)),
                    wd, pl.BlockSpec((FF_BLK, D), lambda i, j, k: (k, 0)), NN,
                    [jax.ShapeDtypeStruct((S, D), F32)], [o_spec], (tm, D), epilogue, (resid,), (o_spec,), comm=comm)


def _ffn_dact(name, dh, wd, gu, comm=None, tm=512):
    S, D = dh.shape
    nb = N_DEV // 2

    def epilogue(acc, ex, outs, j):
        g = ex[0][0].astype(F32)
        u = ex[0][1].astype(F32)
        sig = jax.nn.sigmoid(g)
        outs[0][0] = (acc * u * (sig * (1.0 + g * (1.0 - sig)))).astype(BF16)
        outs[0][1] = (acc * (g * sig)).astype(BF16)

    gu_spec = pl.BlockSpec((2, None, tm, FF_BLK), lambda j, i, k: (0, j, i, 0))
    return _mm_call(name, (nb, S // tm, 1), dh, pl.BlockSpec((tm, D), lambda j, i, k: (i, 0)),
                    wd, pl.BlockSpec((FF_BLK, D), lambda j, i, k: (j, 0)), NT,
                    [jax.ShapeDtypeStruct((2, nb, S, FF_BLK), BF16)], [gu_spec], (tm, FF_BLK), epilogue, (gu,), (gu_spec,),
                    col_axis=0, comm=comm)


def _ffn_dwgu(name, n, dgu, comm=None, tm=1024, tk=1024):
    S, D = n.shape
    dgu8 = dgu.reshape(N_DEV, S, FF_BLK)
    return _mm_call(name, (N_DEV, D // tm, S // tk), n, pl.BlockSpec((tk, tm), lambda d, i, k: (k, i)),
                    dgu8, pl.BlockSpec((None, tk, FF_BLK), lambda d, i, k: (d, k, 0)), TN,
                    [jax.ShapeDtypeStruct((N_DEV, D, FF_BLK), BF16)],
                    [pl.BlockSpec((None, tm, FF_BLK), lambda d, i, k: (d, i, 0))], (tm, FF_BLK), comm=comm)


def _ffn_dwd(name, act, dh, tk=1024):
    nb, S, _ = act.shape
    D = dh.shape[1]
    out = _mm_call(name, (nb, 1, S // tk), act, pl.BlockSpec((None, tk, FF_BLK), lambda b, j, k: (b, k, 0)),
                   dh, pl.BlockSpec((tk, D), lambda b, j, k: (k, 0)), TN,
                   [jax.ShapeDtypeStruct((nb, FF_BLK, D), BF16)],
                   [pl.BlockSpec((None, FF_BLK, D), lambda b, j, k: (b, 0, 0))], (FF_BLK, D))[0]
    return out.reshape(N_DEV, FF_BLK // 2, D)


def _ffn_dn(name, dgu, wgu, comm=None, tm=1024):
    S = dgu.shape[2]
    D = wgu.shape[1]
    dgu8 = dgu.reshape(N_DEV, S, FF_BLK)
    return _mm_call(name, (S // tm, 1, N_DEV), dgu8, pl.BlockSpec((None, tm, FF_BLK), lambda i, j, k: (k, i, 0)),
                    wgu, pl.BlockSpec((None, D, FF_BLK), lambda i, j, k: (k, 0, 0)), NT,
                    [jax.ShapeDtypeStruct((S, D), F32)], [pl.BlockSpec((tm, D), lambda i, j, k: (i, 0))], (tm, D), comm=comm)


def _adamw(name, parts, w, m, v, tr):
    rows, cols = w.shape
    n_parts = len(parts)
    c1 = 1.0 - ADAM_B1 ** ADAM_STEP
    c2 = 1.0 - ADAM_B2 ** ADAM_STEP

    def body(*refs):
        p_refs = refs[:n_parts]
        w_ref, m_ref, v_ref, g_ref, d_ref, nm_ref, nv_ref = refs[n_parts:]
        g = p_refs[0][...].astype(F32)
        for r in p_refs[1:]:
            g = g + r[...].astype(F32)
        mm = ADAM_B1 * m_ref[...] + (1.0 - ADAM_B1) * g
        vv = ADAM_B2 * v_ref[...] + (1.0 - ADAM_B2) * (g * g)
        g_ref[...] = g
        nm_ref[...] = mm
        nv_ref[...] = vv
        d_ref[...] = -ADAM_LR * ((mm / c1) / (jnp.sqrt(vv / c2) + ADAM_EPS) + ADAM_WD * w_ref[...])

    blk = pl.BlockSpec((tr, cols), lambda i: (i, 0))
    out = jax.ShapeDtypeStruct((rows, cols), F32)
    return pl.pallas_call(
        body, grid=(rows // tr,), in_specs=[blk] * (n_parts + 3), out_specs=[blk] * 4, out_shape=[out] * 4,
        compiler_params=_params("parallel"), name=name)(*parts, w, m, v)


def _position():
    return lax.axis_index("x"), lax.axis_index("y"), lax.axis_index("c")


def _other_chips():
    x, y, _ = _position()
    return [(1 - x, y), (x, 1 - y), (1 - x, 1 - y)]


def _remote(src, dst, send, recv, k, to):
    return pltpu.make_async_remote_copy(src_ref=src, dst_ref=dst, send_sem=send.at[k], recv_sem=recv.at[k],
                                        device_id=to, device_id_type=MESH)


def _ag_send(blocks, direct=False):
    n_peer = 7 if direct else 4

    def copies(ins, outs, send, recv, local, r0=0, l0=0):
        x, y, c = _position()
        me = 4 * x + 2 * y + c
        peers = [(x, y, 1 - c)] + [(px, py, c) for px, py in _other_chips()]
        if direct:
            peers += [(px, py, 1 - c) for px, py in _other_chips()]
        cps = []
        for t, (src, dst) in enumerate(zip(ins, outs)):
            cps.append(pltpu.make_async_copy(src, dst.at[me], local.at[l0 + t]))
            cps += [_remote(src, dst.at[me], send, recv, r0 + n_peer * t + k, to) for k, to in enumerate(peers)]
        return cps

    outs = tuple(jax.ShapeDtypeStruct((N_DEV,) + b.shape, b.dtype) for b in blocks)
    return _Comm(tuple(blocks), outs, {}, copies, n_peer * len(blocks), len(blocks))


def _ag_forward(bufs):
    def copies(ins, outs, send, recv, local, r0=0, l0=0):
        x, y, c = _position()
        cps = []
        for t, buf in enumerate(outs):
            for k, (px, py) in enumerate(_other_chips()):
                slot = buf.at[4 * px + 2 * py + c]
                cps.append(_remote(slot, slot, send, recv, r0 + 3 * t + k, (x, y, 1 - c)))
        return cps

    outs = tuple(jax.ShapeDtypeStruct(b.shape, b.dtype) for b in bufs)
    return _Comm(tuple(bufs), outs, {t: t for t in range(len(bufs))}, copies, 3 * len(bufs), 0)


def _rs_swap(shares):
    def copies(ins, outs, send, recv, local, r0=0, l0=0):
        x, y, c = _position()
        return [_remote(src.at[:, 1 - c], dst, send, recv, r0 + t, (x, y, 1 - c)) for t, (src, dst) in enumerate(zip(ins, outs))]

    ins = tuple(s.reshape((4, 2) + s.shape[1:]) for s in shares)
    outs = tuple(jax.ShapeDtypeStruct((4,) + s.shape[1:], s.dtype) for s in shares)
    return _Comm(ins, outs, {}, copies, len(shares), 0)


def _rs_exchange(sums):
    def copies(ins, outs, send, recv, local, r0=0, l0=0):
        _, _, c = _position()
        return [_remote(src.at[2 * px + py], dst.at[k], send, recv, r0 + 3 * t + k, (px, py, c))
                for t, (src, dst) in enumerate(zip(ins, outs)) for k, (px, py) in enumerate(_other_chips())]

    outs = tuple(jax.ShapeDtypeStruct((3,) + s.shape[1:], s.dtype) for s in sums)
    return _Comm(tuple(sums), outs, {}, copies, 3 * len(sums), 0)


def _join(a, b):
    def copies(ins, outs, send, recv, local, r0=0, l0=0):
        return (a.copies(ins[:len(a.ins)], outs[:len(a.outs)], send, recv, local, r0, l0)
                + b.copies(ins[len(a.ins):], outs[len(a.outs):], send, recv, local, r0 + a.n_remote, l0 + a.n_local))

    aliases = {**a.aliases, **{len(a.ins) + i: len(a.outs) + o for i, o in b.aliases.items()}}
    return _Comm(a.ins + b.ins, a.outs + b.outs, aliases, copies, a.n_remote + b.n_remote, a.n_local + b.n_local)


def _comm_call(name, comm):
    return _call(name, lambda: None, (), [], [], [], [], (), (), comm)


def _pair_sum(name, share, got, core, tr):
    _, rows, cols = share.shape

    def body(c_ref, a_ref, b_ref, o_ref):
        o_ref[...] = (a_ref[...].astype(F32) + b_ref[...].astype(F32)).astype(o_ref.dtype)

    grid_spec = pltpu.PrefetchScalarGridSpec(
        num_scalar_prefetch=1, grid=(4, rows // tr),
        in_specs=[pl.BlockSpec((None, None, tr, cols), lambda q, i, c: (q, c[0], i, 0)),
                  pl.BlockSpec((None, tr, cols), lambda q, i, c: (q, i, 0))],
        out_specs=pl.BlockSpec((None, tr, cols), lambda q, i, c: (q, i, 0)))
    return pl.pallas_call(
        body, grid_spec=grid_spec, out_shape=jax.ShapeDtypeStruct((4, rows, cols), share.dtype),
        compiler_params=_params("parallel", "parallel"), name=name)(core, share.reshape(4, 2, rows, cols), got)


TENSORS = ("a_w_in", "a_w_out", "b_w_in", "b_w_out", "gu0", "gu1", "dn0", "dn1")
ROW_TILE = {"a_w_in": 256, "a_w_out": 128, "b_w_in": 256, "b_w_out": 128, "gu0": 256, "gu1": 256, "dn0": 176, "dn1": 176}
A_BLK = 9 * D_MODEL // N_DEV
B_BLK = 386
B_IN = 3 * D_MODEL + N_HEADS
B_IN_PAD = 3 * D_MODEL + LANES


def kernel(x, a_norm, a_w_in, a_w_out, b_norm, b_w_in, b_f, b_w_out, ffn_norm, ffn_w_gu, ffn_w_down, final_norm, loss_target, m_a_norm, m_a_w_in, m_a_w_out, m_b_norm, m_b_w_in, m_b_f, m_b_w_out, m_ffn_norm, m_ffn_w_gu, m_ffn_w_down, m_final_norm, v_a_norm, v_a_w_in, v_a_w_out, v_b_norm, v_b_w_in, v_b_f, v_b_w_out, v_ffn_norm, v_ffn_w_gu, v_ffn_w_down, v_final_norm):
    S = x.shape[1]
    xi, yi, ci = _position()
    dev = 4 * xi + 2 * yi + ci
    core = ci.reshape(1).astype(jnp.int32)
    h0, target = x.reshape(S, D_MODEL), loss_target.reshape(S, D_MODEL)

    def shards(a_in, a_out, b_in, b_out, gu, dn):
        return {"a_w_in": a_in[0], "a_w_out": a_out[0], "b_w_in": b_in[0], "b_w_out": b_out[0],
                "gu0": gu[0], "gu1": gu[1], "dn0": dn[0], "dn1": dn[1]}

    w_sh = shards(a_w_in, a_w_out, b_w_in, b_w_out, ffn_w_gu, ffn_w_down)
    m_sh = shards(m_a_w_in, m_a_w_out, m_b_w_in, m_b_w_out, m_ffn_w_gu, m_ffn_w_down)
    v_sh = shards(v_a_w_in, v_a_w_out, v_b_w_in, v_b_w_out, v_ffn_w_gu, v_ffn_w_down)
    wb = {n: w_sh[n].astype(BF16) for n in TENSORS}
    bf_pad = jnp.pad(b_f, ((0, 0), (0, LANES - N_HEADS)))
    tabs = _rope_tables(S)

    g_ain, g_aout = _comm_call("gather_a", _ag_send([wb["a_w_in"], wb["a_w_out"]]))
    g_ain, g_aout = _comm_call("forward_a", _ag_forward([g_ain, g_aout]))
    n0 = _rms_fwd("rms_a", h0, a_norm[0])
    later = [wb["b_w_in"], wb["b_w_out"], wb["gu0"], wb["dn0"], jnp.pad(b_norm, ((0, 7), (0, 0)))]
    proj_a, later = _a_proj("proj_a", n0, g_ain, tabs, _ag_send(later))
    groups = []
    for g, (window, dil) in enumerate(DILATED_PATTERNS):
        L = S // dil
        view = proj_a.reshape(L, dil * 9 * D_MODEL)
        cols = [(lambda r, g=g, tq=tq: r * 9 + g * 3 + tq) for tq in range(3)]
        groups.append((g, dil, L, view, cols))
    outs, lses = [], []
    for g, dil, L, view, cols in groups:
        o_g, lse_g = _dil_fwd("dil_fwd%d" % g, view, *cols, dil, L)
        outs.append(o_g.reshape(S, D_MODEL))
        lses.append(lse_g.reshape(S, D_MODEL))
    o_a, lse_a = _combine("dil_combine", outs, lses)
    w_a_out = g_aout.reshape(D_MODEL, D_MODEL)
    h1, (g_bin, g_bout, g_gu0, g_dn0, g_bnorm) = _matmul("out_a", o_a, w_a_out, "nn", F32, TM, 1024, 1024, resid=h0,
                                                         comm=_ag_forward(later))

    n1 = _rms_fwd("rms_f0", h1, ffn_norm[0])
    gu0, act0, g_gu1 = _ffn_gu("gu_f0", n1, g_gu0, _ag_send([wb["gu1"]]))
    w_dn0 = g_dn0.reshape(D_FF, D_MODEL)
    h2, g_dn1 = _ffn_down("down_f0", act0, w_dn0, h1, _ag_send([wb["dn1"]]))

    b_norm_full = g_bnorm[:, 0].reshape(D_MODEL)
    w_b_in = g_bin.transpose(1, 0, 2).reshape(D_MODEL, B_IN)
    w_b_gate = jnp.pad(w_b_in[:, 3 * D_MODEL:], ((0, 0), (0, LANES - N_HEADS)))
    w_b_cat = jnp.concatenate([w_b_in[:, :3 * D_MODEL], w_b_gate], axis=1)
    w_b_out = g_bout.reshape(D_MODEL, D_MODEL)
    n2 = _rms_fwd("rms_b", h2, b_norm_full)
    qkv, (g_gu1, g_dn1) = _matmul("proj_b", n2, w_b_in[:, :3 * D_MODEL], "nn", BF16, TM, 1024, 1024, col0_scale=SOFTMAX_SCALE,
                                  comm=_ag_forward([g_gu1, g_dn1]))
    z = _matmul("gate_b", n2, w_b_gate, "nn", F32, TM, LANES, 1024)
    kbias = _gate_fwd("gate_cumsum", z, bf_pad)
    tf = min(S, 512)
    o_b, lse_b = _fox_fwd("fox_fwd", qkv, kbias, tf)
    h3 = _matmul("out_b", o_b, w_b_out, "nn", F32, TM, 1024, 1024, resid=h2)

    w_dn1 = g_dn1.reshape(D_FF, D_MODEL)
    n3 = _rms_fwd("rms_f1", h3, ffn_norm[1])
    gu1, act1 = _ffn_gu("gu_f1", n3, g_gu1)
    h4 = _ffn_down("down_f1", act1, w_dn1, h3)[0]

    dh4, d_final, loss = _loss_head("loss_head", h4, final_norm, target)

    share, got, sums, others = {}, {}, {}, {}

    def pair_sums(*names):
        for n in names:
            sums[n] = _pair_sum("pair_" + n, share[n], got[n], core, ROW_TILE[n])

    dgu1 = _ffn_dact("dact_f1", dh4, w_dn1, gu1)[0]
    share["dn1"] = _ffn_dwd("dwd_f1", act1, dh4)
    share["gu1"] = _ffn_dwgu("dwgu_f1", n3, dgu1)[0]
    dn3, got["gu1"], got["dn1"] = _ffn_dn("dn_f1", dgu1, g_gu1, _rs_swap([share["gu1"], share["dn1"]]))
    dh3, d_ffn1 = _rms_bwd("rmsb_f1", dn3, h3, ffn_norm[1], dh4)
    pair_sums("gu1", "dn1")

    do_b = _matmul("dout_b", dh3, w_b_out, "nt", BF16, TM, 1024, 1024)
    share["b_w_out"] = _matmul("dwout_b", o_b, dh3, "tn", BF16, TM, 1024, 1024).reshape(N_DEV, 128, D_MODEL)
    dq_b, ds_rowsum = _fox_dq("fox_dq", qkv, kbias, do_b, o_b, lse_b, tf)
    dk_b, dv_b, ds_colsum = _fox_dkv("fox_dkv", qkv, kbias, do_b, o_b, lse_b, tf)
    dc = ds_rowsum[:, ::HEAD_DIM] - ds_colsum.reshape(N_HEADS, S).T
    dz, d_bf = _gate_bwd("gate_bwd", jnp.pad(dc, ((0, 0), (0, LANES - N_HEADS))), z, bf_pad)
    dproj_b = jnp.concatenate([dq_b, dk_b, dv_b, dz.astype(BF16)], axis=1)
    dw_b_in, (others["gu1"],) = _matmul("dwin_b", n2, dproj_b, "tn", BF16, TM, B_IN_PAD // 5, 1024, comm=_rs_exchange([sums["gu1"]]))
    dn2, (others["dn1"],) = _matmul("dn_b", dproj_b, w_b_cat, "nt", F32, TM, 1024, B_IN_PAD // 5, comm=_rs_exchange([sums["dn1"]]))
    dh2, d_bnorm = _rms_bwd("rmsb_b", dn2, h2, b_norm_full, dh3)
    share["b_w_in"] = dw_b_in[:, :B_IN].reshape(D_MODEL, N_DEV, B_BLK).transpose(1, 0, 2)

    dgu0, got["b_w_in"], got["b_w_out"] = _ffn_dact("dact_f0", dh2, w_dn0, gu0, _rs_swap([share["b_w_in"], share["b_w_out"]]))
    share["dn0"] = _ffn_dwd("dwd_f0", act0, dh2)
    pair_sums("b_w_in", "b_w_out")
    share["gu0"], others["b_w_in"], others["b_w_out"] = _ffn_dwgu(
        "dwgu_f0", n1, dgu0, _rs_exchange([sums["b_w_in"], sums["b_w_out"]]))
    dn1, got["gu0"], got["dn0"] = _ffn_dn("dn_f0", dgu0, g_gu0, _rs_swap([share["gu0"], share["dn0"]]))
    dh1, d_ffn0 = _rms_bwd("rmsb_f0", dn1, h1, ffn_norm[0], dh2)
    pair_sums("gu0", "dn0")

    do_a = _matmul("dout_a", dh1, w_a_out, "nt", BF16, TM, 1024, 1024)
    share["a_w_out"] = _matmul("dwout_a", o_a, dh1, "tn", BF16, TM, 1024, 1024).reshape(N_DEV, 128, D_MODEL)
    pieces = []
    for g, dil, L, view, cols in groups:
        sv = lambda a: a.reshape(L, dil * a.shape[1])
        args = (view, sv(do_a), sv(o_a), sv(lse_a), tuple(sv(tb) for tb in tabs), *cols, dil, L)
        dq_g = _dil_dq("dil_dq%d" % g, *args)
        dk_g, dv_g = _dil_dkv("dil_dkv%d" % g, *args)
        pieces += [a.reshape(S, D_MODEL) for a in (dq_g, dk_g, dv_g)]
    dproj_a = jnp.concatenate(pieces, axis=1)
    share["a_w_in"], others["gu0"], others["dn0"] = _mm_call(
        "dwin_a", (N_DEV, 1, S // 1024), n0, pl.BlockSpec((1024, D_MODEL), lambda d, i, k: (k, 0)),
        dproj_a, pl.BlockSpec((1024, A_BLK), lambda d, i, k: (k, d)), TN, [jax.ShapeDtypeStruct((N_DEV, D_MODEL, A_BLK), BF16)],
        [pl.BlockSpec((None, D_MODEL, A_BLK), lambda d, i, k: (d, 0, 0))], (D_MODEL, A_BLK),
        comm=_rs_exchange([sums["gu0"], sums["dn0"]]))
    dn0, got["a_w_in"], got["a_w_out"] = _mm_call(
        "dn_a", (S // TM, 1, N_DEV), dproj_a, pl.BlockSpec((TM, A_BLK), lambda i, j, k: (i, k)),
        g_ain, pl.BlockSpec((None, D_MODEL, A_BLK), lambda i, j, k: (k, 0, 0)), NT, [jax.ShapeDtypeStruct((S, D_MODEL), F32)],
        [pl.BlockSpec((TM, D_MODEL), lambda i, j, k: (i, 0))], (TM, D_MODEL), comm=_rs_swap([share["a_w_in"], share["a_w_out"]]))
    dx, d_anorm = _rms_bwd("rmsb_a", dn0, h0, a_norm[0], dh1)
    pair_sums("a_w_in", "a_w_out")

    misc = jnp.concatenate([d_bf[:, :N_HEADS], loss[:, :1], jnp.zeros((1, D_MODEL - N_HEADS - 1), F32)], axis=1)
    small = jnp.concatenate([d_anorm, d_ffn0, d_ffn1, d_final, d_bnorm, misc, jnp.zeros((2, D_MODEL), F32)], axis=0)
    others["a_w_in"], others["a_w_out"], small_all = _comm_call(
        "exchange_a", _join(_rs_exchange([sums["a_w_in"], sums["a_w_out"]]), _ag_send([small], direct=True)))

    outs = {}
    for n in TENSORS:
        mine = lax.dynamic_index_in_dim(sums[n], 2 * xi + yi, axis=0, keepdims=False)
        outs[n] = _adamw("adamw_" + n, [mine] + [others[n][k] for k in range(3)], w_sh[n], m_sh[n], v_sh[n], ROW_TILE[n])

    pad_vec = lambda a: jnp.pad(a, ((0, 0), (0, D_MODEL - a.shape[1])))

    def small_pack(an, fn, fin, bf):
        return jnp.concatenate([an, fn, fin.reshape(1, D_MODEL), jnp.zeros((1, D_MODEL), F32), pad_vec(bf),
                                jnp.zeros((2, D_MODEL), F32)], axis=0)

    sg, sd, sm, sv = _adamw("adamw_small", [small_all[d] for d in range(N_DEV)], small_pack(a_norm, ffn_norm, final_norm, b_f),
                            small_pack(m_a_norm, m_ffn_norm, m_final_norm, m_b_f),
                            small_pack(v_a_norm, v_ffn_norm, v_final_norm, v_b_f), 8)
    g_bn = lax.dynamic_slice(sg[4:5], (0, dev * LANES), (1, LANES))
    bn = _adamw("adamw_b_norm", [g_bn], b_norm, m_b_norm, v_b_norm, 1)

    def tree(i):
        full = lambda name, ref: outs[name][i].reshape(ref.shape)
        sml = (sg, sd, sm, sv)[i]
        return dict(
            a_norm=sml[0:1], a_w_in=full("a_w_in", a_w_in), a_w_out=full("a_w_out", a_w_out), b_norm=bn[i],
            b_w_in=full("b_w_in", b_w_in), b_f=sml[5:6, :N_HEADS], b_w_out=full("b_w_out", b_w_out), ffn_norm=sml[1:3],
            ffn_w_gu=jnp.stack([outs["gu0"][i], outs["gu1"][i]]).reshape(ffn_w_gu.shape),
            ffn_w_down=jnp.stack([outs["dn0"][i], outs["dn1"][i]]).reshape(ffn_w_down.shape), final_norm=sml[3])

    order = ("a_norm", "a_w_in", "a_w_out", "b_norm", "b_w_in", "b_f", "b_w_out", "ffn_norm", "ffn_w_gu", "ffn_w_down", "final_norm")
    result = [sg[5, N_HEADS], dx.reshape(x.shape)]
    for i in range(4):
        t = tree(i)
        result += [t[n] for n in order]
    return tuple(result)
```

```python
import functools
from typing import Callable, NamedTuple

import jax
import jax.numpy as jnp
from jax import lax
from jax.experimental import pallas as pl
from jax.experimental.pallas import tpu as pltpu

F32 = jnp.float32
BF16 = jnp.bfloat16

D_MODEL = 1024
N_HEADS = 16
HEAD_DIM = 64
N_PAIRS = N_HEADS // 2
LANES = 128
DILATED_PATTERNS = ((128, 1), (512, 4), (2048, 16))
BAND_STEPS = 128
ROT_DIM = HEAD_DIM // 4
ROPE_THETA = 500000.0
D_FF = 2816
RMS_EPS = 1e-6
NEG_INF = -1e30
SOFTMAX_SCALE = HEAD_DIM ** -0.5
N_DEV = 8
FF_BLK = 2 * D_FF // N_DEV
ADAM_LR, ADAM_B1, ADAM_B2, ADAM_EPS, ADAM_WD, ADAM_STEP = 0.001, 0.9, 0.999, 1e-08, 0.01, 10
VMEM_LIMIT = 52 * 1024 * 1024
FOX_BWD_VMEM = 60 * 1024 * 1024
TM = 1024
MESH = pl.DeviceIdType.MESH

NN = (((1,), (0,)), ((), ()))
NT = (((1,), (1,)), ((), ()))
TN = (((0,), (0,)), ((), ()))


def _params(*sem):
    return pltpu.CompilerParams(dimension_semantics=sem, vmem_limit_bytes=VMEM_LIMIT)


def _dot(a, b, dims):
    return lax.dot_general(a, b, dims, preferred_element_type=F32)


class _Comm(NamedTuple):
    ins: tuple
    outs: tuple
    aliases: dict
    copies: Callable
    n_remote: int
    n_local: int


def _call(name, body, grid, in_specs, out_specs, out_shape, scratch, args, sem, comm=None):
    if comm is None:
        return pl.pallas_call(body, grid=grid, in_specs=in_specs, out_specs=out_specs, out_shape=out_shape,
                              scratch_shapes=scratch, compiler_params=_params(*sem), name=name)(*args)
    n_in, n_out = len(in_specs), len(out_specs)
    n_ci, n_co = len(comm.ins), len(comm.outs)
    o0 = n_in + n_ci

    def hosted(*refs):
        c_ins, c_outs = refs[n_in:o0], refs[o0 + n_out:o0 + n_out + n_co]
        sems = refs[-3:]

        def start():
            for cp in comm.copies(c_ins, c_outs, *sems):
                cp.start()

        def wait():
            for cp in comm.copies(c_ins, c_outs, *sems):
                cp.wait()

        if not grid:
            start()
            body()
            wait()
            return
        ids = [pl.program_id(ax) for ax in range(len(grid))]
        pl.when(functools.reduce(jnp.logical_and, [i == 0 for i in ids]))(start)
        body(*refs[:n_in], *refs[o0:o0 + n_out], *refs[o0 + n_out + n_co:-3])
        pl.when(functools.reduce(jnp.logical_and, [i == g - 1 for i, g in zip(ids, grid)]))(wait)

    hbm = pl.BlockSpec(memory_space=pltpu.HBM)
    dma = pltpu.SemaphoreType.DMA
    return pl.pallas_call(
        hosted, grid=grid, in_specs=[*in_specs, *[hbm] * n_ci], out_specs=[*out_specs, *[hbm] * n_co],
        out_shape=[*out_shape, *comm.outs], input_output_aliases={n_in + i: n_out + o for i, o in comm.aliases.items()},
        scratch_shapes=[*scratch, dma((comm.n_remote,)), dma((comm.n_remote,)), dma((max(comm.n_local, 1),))],
        compiler_params=_params(*["arbitrary"] * len(grid)), name=name)(*args, *comm.ins)


def _mm_call(name, grid, a, a_spec, b, b_spec, dims, out_shapes, out_specs, acc_shape, epilogue=None,
             extras=(), extra_specs=(), col_axis=1, comm=None):
    nk = grid[2]
    n_extra = len(extras)
    n_out = len(out_shapes)

    def finish(res, ex, outs, j):
        if epilogue is None:
            outs[0][...] = res.astype(outs[0].dtype)
        else:
            epilogue(res, ex, outs, j)

    def body(*refs):
        a_ref, b_ref = refs[0], refs[1]
        ex = refs[2:2 + n_extra]
        outs = refs[2 + n_extra:2 + n_extra + n_out]
        j, k = pl.program_id(col_axis), pl.program_id(2)
        part = _dot(a_ref[...].astype(BF16), b_ref[...].astype(BF16), dims)
        if nk == 1:
            finish(part, ex, outs, j)
            return
        acc = refs[-1]

        @pl.when(k == 0)
        def _():
            acc[...] = part

        @pl.when((k > 0) & (k < nk - 1))
        def _():
            acc[...] += part

        @pl.when(k == nk - 1)
        def _():
            finish(acc[...] + part, ex, outs, j)

    return _call(name, body, grid, [a_spec, b_spec, *extra_specs], out_specs, out_shapes,
                 [] if nk == 1 else [pltpu.VMEM(acc_shape, F32)], (a, b, *extras), ("parallel", "parallel", "arbitrary"), comm)


def _matmul(name, a, b, mode, out_dtype, tm, tn, tk, resid=None, col0_scale=None, comm=None):
    if mode == "nn":
        (M, K), N = a.shape, b.shape[1]
        a_spec = pl.BlockSpec((tm, tk), lambda j, i, k: (i, k))
        b_spec = pl.BlockSpec((tk, tn), lambda j, i, k: (k, j))
        dims = NN
    elif mode == "nt":
        (M, K), N = a.shape, b.shape[0]
        a_spec = pl.BlockSpec((tm, tk), lambda j, i, k: (i, k))
        b_spec = pl.BlockSpec((tn, tk), lambda j, i, k: (j, k))
        dims = NT
    else:
        (K, M), N = a.shape, b.shape[1]
        a_spec = pl.BlockSpec((tk, tm), lambda j, i, k: (k, i))
        b_spec = pl.BlockSpec((tk, tn), lambda j, i, k: (k, j))
        dims = TN
    assert M % tm == 0 and N % tn == 0 and K % tk == 0, (name, M, N, K, tm, tn, tk)
    o_spec = pl.BlockSpec((tm, tn), lambda j, i, k: (i, j))
    extras, extra_specs, epilogue = (), (), None
    if resid is not None:
        extras, extra_specs = (resid,), (o_spec,)

        def epilogue(acc, ex, outs, j):
            outs[0][...] = (acc + ex[0][...]).astype(outs[0].dtype)

    elif col0_scale is not None:

        def epilogue(acc, ex, outs, j):
            outs[0][...] = (acc * jnp.where(j == 0, col0_scale, 1.0)).astype(outs[0].dtype)

    res = _mm_call(name, (N // tn, M // tm, K // tk), a, a_spec, b, b_spec, dims, [jax.ShapeDtypeStruct((M, N), out_dtype)],
                   [o_spec], (tm, tn), epilogue, extras, extra_specs, col_axis=0, comm=comm)
    return res[0] if comm is None else (res[0], res[1:])


def _rms_fwd(name, h, gain, tm=512):
    S, D = h.shape

    def body(h_ref, g_ref, n_ref):
        x = h_ref[...]
        rstd = lax.rsqrt(jnp.mean(x * x, axis=-1, keepdims=True) + RMS_EPS)
        n_ref[...] = (x * rstd * g_ref[...]).astype(BF16)

    return pl.pallas_call(
        body, grid=(S // tm,), in_specs=[pl.BlockSpec((tm, D), lambda i: (i, 0)), pl.BlockSpec((1, D), lambda i: (0, 0))],
        out_specs=pl.BlockSpec((tm, D), lambda i: (i, 0)), out_shape=jax.ShapeDtypeStruct((S, D), BF16),
        compiler_params=_params("parallel"), name=name)(h, gain.reshape(1, D))


def _rms_bwd(name, dn, h, gain, dres, tm=512):
    S, D = h.shape

    def body(dn_ref, h_ref, g_ref, r_ref, dh_ref, dg_ref):
        x = h_ref[...]
        rstd = lax.rsqrt(jnp.mean(x * x, axis=-1, keepdims=True) + RMS_EPS)
        xhat = x * rstd
        d = dn_ref[...]
        dxhat = d * g_ref[...]
        dh_ref[...] = rstd * (dxhat - xhat * jnp.mean(dxhat * xhat, axis=-1, keepdims=True)) + r_ref[...]

        @pl.when(pl.program_id(0) == 0)
        def _():
            dg_ref[...] = jnp.zeros_like(dg_ref)

        dg_ref[...] += jnp.sum(d * xhat, axis=0, keepdims=True)

    row = pl.BlockSpec((tm, D), lambda i: (i, 0))
    vec = pl.BlockSpec((1, D), lambda i: (0, 0))
    return pl.pallas_call(
        body, grid=(S // tm,), in_specs=[row, row, vec, row], out_specs=[row, vec],
        out_shape=[jax.ShapeDtypeStruct((S, D), F32), jax.ShapeDtypeStruct((1, D), F32)],
        compiler_params=_params("arbitrary"), name=name)(dn, h, gain.reshape(1, D), dres)


def _loss_head(name, h, gain, target, tm=512):
    S, D = h.shape

    def body(h_ref, g_ref, t_ref, dh_ref, dg_ref, loss_ref):
        x = h_ref[...]
        rstd = lax.rsqrt(jnp.mean(x * x, axis=-1, keepdims=True) + RMS_EPS)
        xhat = x * rstd
        err = xhat * g_ref[...] - t_ref[...]
        dy = err * (1.0 / D)
        dxhat = dy * g_ref[...]
        dh_ref[...] = rstd * (dxhat - xhat * jnp.mean(dxhat * xhat, axis=-1, keepdims=True))

        @pl.when(pl.program_id(0) == 0)
        def _():
            dg_ref[...] = jnp.zeros_like(dg_ref)
            loss_ref[...] = jnp.zeros_like(loss_ref)

        dg_ref[...] += jnp.sum(dy * xhat, axis=0, keepdims=True)
        part = 0.5 * jnp.sum(jnp.mean(err * err, axis=-1, keepdims=True), axis=0, keepdims=True)
        loss_ref[...] += jnp.broadcast_to(part, loss_ref.shape)

    row = pl.BlockSpec((tm, D), lambda i: (i, 0))
    vec = pl.BlockSpec((1, D), lambda i: (0, 0))
    return pl.pallas_call(
        body, grid=(S // tm,), in_specs=[row, vec, row], out_specs=[row, vec, pl.BlockSpec((1, LANES), lambda i: (0, 0))],
        out_shape=[jax.ShapeDtypeStruct((S, D), F32), jax.ShapeDtypeStruct((1, D), F32),
                   jax.ShapeDtypeStruct((1, LANES), F32)],
        compiler_params=_params("arbitrary"), name=name)(h, gain.reshape(1, D), target)


def _rope_tables(S):
    half = ROT_DIM // 2
    inv_freq = ROPE_THETA ** (-jnp.arange(half, dtype=F32) * 2.0 / ROT_DIM)
    ang = jnp.arange(S, dtype=F32)[:, None] * inv_freq[None, :]
    cos, sin = jnp.cos(ang), jnp.sin(ang)
    one = jnp.ones((S, HEAD_DIM - ROT_DIM), F32)
    zero = jnp.zeros((S, HEAD_DIM - ROT_DIM), F32)
    zh = jnp.zeros((S, half), F32)
    c = jnp.concatenate([cos, cos, one], axis=1)
    sa = jnp.concatenate([-sin, zh, zero], axis=1)
    sb = jnp.concatenate([zh, sin, zero], axis=1)
    return tuple(jnp.concatenate([t, t], axis=1) for t in (c, sa, sb))


def _rotate(x, c, sa, sb, sign):
    return x * c + sign * (pltpu.roll(x, LANES - ROT_DIM // 2, 1) * sa + pltpu.roll(x, ROT_DIM // 2, 1) * sb)


def _a_proj(name, n, w, tabs, comm, tm=1024):
    S, D = n.shape
    tn = w.shape[2]

    def epilogue(acc, ex, outs, d):
        c, sa, sb = ex[0][...], ex[1][...], ex[2][...]
        for b in range(tn // LANES):
            cols = slice(b * LANES, (b + 1) * LANES)
            kind = ((d * (tn // LANES) + b) // N_PAIRS) % 3
            x = acc[:, cols]
            rot = _rotate(x, c, sa, sb, 1.0) * jnp.where(kind == 0, SOFTMAX_SCALE, 1.0)
            outs[0][:, cols] = jnp.where(kind == 2, x, rot).astype(BF16)

    tab = pl.BlockSpec((tm, LANES), lambda d, i, k: (i, 0))
    res = _mm_call(name, (N_DEV, S // tm, 1), n, pl.BlockSpec((tm, D), lambda d, i, k: (i, 0)),
                   w, pl.BlockSpec((None, D, tn), lambda d, i, k: (d, 0, 0)), NN,
                   [jax.ShapeDtypeStruct((S, N_DEV * tn), BF16)], [pl.BlockSpec((tm, tn), lambda d, i, k: (i, d))], (tm, tn),
                   epilogue, tabs, (tab, tab, tab), col_axis=0, comm=comm)
    return res[0], res[1:]


def _lo_lanes():
    return lax.broadcasted_iota(jnp.int32, (1, LANES), 1) < HEAD_DIM


def _rep_rows(x2, lo):
    sw = pltpu.roll(x2, HEAD_DIM, 1)
    return jnp.where(lo, x2, sw), jnp.where(lo, sw, x2)


def _pair_cols(h):
    return slice((h // 2) * LANES, (h // 2 + 1) * LANES)


def _head_lanes(lo, h):
    return lo if h % 2 == 0 else jnp.logical_not(lo)


def _band_masks(t, first):
    ri = lax.broadcasted_iota(jnp.int32, (t, t), 0)
    ci = lax.broadcasted_iota(jnp.int32, (t, t), 1)
    neg_prev = jnp.where((ci >= ri) & jnp.logical_not(first), 0.0, NEG_INF)
    neg_cur = jnp.where(ci <= ri, 0.0, NEG_INF)
    return neg_prev, neg_cur


def _dil_specs(L, R, t, qcol, kcol, vcol):
    W = D_MODEL
    prev = lambda qi: jnp.maximum(qi - 1, 0)
    return dict(
        q=pl.BlockSpec((t, W), lambda r, qi: (qi, qcol(r))),
        kp=pl.BlockSpec((t, W), lambda r, qi: (prev(qi), kcol(r))), kc=pl.BlockSpec((t, W), lambda r, qi: (qi, kcol(r))),
        vp=pl.BlockSpec((t, W), lambda r, qi: (prev(qi), vcol(r))), vc=pl.BlockSpec((t, W), lambda r, qi: (qi, vcol(r))),
        own=pl.BlockSpec((t, W), lambda r, qi: (qi, r)), tab=pl.BlockSpec((t, LANES), lambda r, qi: (qi, r)))


def _dil_fwd(name, x, qcol, kcol, vcol, R, L):
    t = BAND_STEPS
    W = D_MODEL
    sp = _dil_specs(L, R, t, qcol, kcol, vcol)

    def body(q_ref, kp_ref, kc_ref, vp_ref, vc_ref, o_ref, lse_ref):
        lo = _lo_lanes()
        neg_p, neg_c = _band_masks(t, pl.program_id(1) == 0)
        s_p, s_c = [], []
        for h in range(N_HEADS):
            cols = _pair_cols(h)
            qh = jnp.where(_head_lanes(lo, h), q_ref[:, cols], 0)
            s_p.append(_dot(qh, kp_ref[:, cols], NT))
            s_c.append(_dot(qh, kc_ref[:, cols], NT))
        s_p = jnp.stack(s_p) + neg_p[None]
        s_c = jnp.stack(s_c) + neg_c[None]
        m = jnp.maximum(jnp.max(s_p, axis=2, keepdims=True), jnp.max(s_c, axis=2, keepdims=True))
        p_p, p_c = jnp.exp(s_p - m), jnp.exp(s_c - m)
        l = jnp.sum(p_p, axis=2, keepdims=True) + jnp.sum(p_c, axis=2, keepdims=True)
        inv, lse = 1.0 / l, m + jnp.log(l)
        p_p, p_c = p_p.astype(BF16), p_c.astype(BF16)
        for p in range(N_PAIRS):
            cols = _pair_cols(2 * p)
            o2 = jnp.zeros((t, LANES), F32)
            for h in (2 * p, 2 * p + 1):
                hm = _head_lanes(lo, h)
                pv = _dot(p_p[h], jnp.where(hm, vp_ref[:, cols], 0), NN) + _dot(p_c[h], jnp.where(hm, vc_ref[:, cols], 0), NN)
                o2 = o2 + pv * inv[h]
            o_ref[:, cols] = o2
            lse_ref[:, cols] = jnp.where(lo, lse[2 * p], lse[2 * p + 1])

    return pl.pallas_call(
        body, grid=(R, L // t), in_specs=[sp["q"], sp["kp"], sp["kc"], sp["vp"], sp["vc"]], out_specs=[sp["own"], sp["own"]],
        out_shape=[jax.ShapeDtypeStruct((L, R * W), F32), jax.ShapeDtypeStruct((L, R * W), F32)],
        compiler_params=_params("parallel", "parallel"), name=name)(x, x, x, x, x)


def _dil_scores(lo, q_ref, do_ref, o_ref, lse_ref, kv_refs):
    s = [[] for _ in kv_refs]
    dp = [[] for _ in kv_refs]
    lse, d = [], []
    for h in range(N_HEADS):
        cols = _pair_cols(h)
        hm = _head_lanes(lo, h)
        qh, doh = jnp.where(hm, q_ref[:, cols], 0), jnp.where(hm, do_ref[:, cols], 0)
        for i, (k_ref, v_ref) in enumerate(kv_refs):
            s[i].append(_dot(qh, k_ref[:, cols], NT))
            dp[i].append(_dot(doh, v_ref[:, cols], NT))
        lse.append(_rep_rows(lse_ref[:, cols], lo)[h % 2])
        dd = do_ref[:, cols].astype(F32) * o_ref[:, cols].astype(F32)
        d.append(jnp.sum(jnp.where(hm, dd, 0.0), axis=1, keepdims=True))
    return (*[jnp.stack(x) for x in s], *[jnp.stack(x) for x in dp], jnp.stack(lse), jnp.stack(d))


def _dil_dq(name, x, do, o, lse, tabs, qcol, kcol, vcol, R, L):
    t = BAND_STEPS
    W = D_MODEL
    sp = _dil_specs(L, R, t, qcol, kcol, vcol)

    def body(q_ref, kp_ref, kc_ref, vp_ref, vc_ref, do_ref, o_ref, lse_ref, c_ref, sa_ref, sb_ref, dq_ref):
        lo = _lo_lanes()
        neg_p, neg_c = _band_masks(t, pl.program_id(1) == 0)
        s_p, s_c, dp_p, dp_c, lse, d = _dil_scores(lo, q_ref, do_ref, o_ref, lse_ref, ((kp_ref, vp_ref), (kc_ref, vc_ref)))
        ds_p = (jnp.exp(s_p + neg_p[None] - lse) * (dp_p - d)).astype(BF16)
        ds_c = (jnp.exp(s_c + neg_c[None] - lse) * (dp_c - d)).astype(BF16)
        for p in range(N_PAIRS):
            cols = _pair_cols(2 * p)
            dq2 = jnp.zeros((t, LANES), F32)
            for h in (2 * p, 2 * p + 1):
                hm = _head_lanes(lo, h)
                dq2 = dq2 + _dot(ds_p[h], jnp.where(hm, kp_ref[:, cols], 0), NN) + _dot(ds_c[h], jnp.where(hm, kc_ref[:, cols], 0), NN)
            dq_ref[:, cols] = _rotate(dq2 * SOFTMAX_SCALE, c_ref[...], sa_ref[...], sb_ref[...], -1.0).astype(BF16)

    return pl.pallas_call(
        body, grid=(R, L // t),
        in_specs=[sp["q"], sp["kp"], sp["kc"], sp["vp"], sp["vc"], sp["own"], sp["own"], sp["own"], sp["tab"], sp["tab"], sp["tab"]],
        out_specs=sp["own"], out_shape=jax.ShapeDtypeStruct((L, R * W), BF16),
        compiler_params=_params("parallel", "parallel"), name=name)(x, x, x, x, x, do, o, lse, *tabs)


def _dil_dkv(name, x, do, o, lse, tabs, qcol, kcol, vcol, R, L):
    t = BAND_STEPS
    W = D_MODEL
    nq = L // t
    nxt = lambda kb: jnp.minimum(kb + 1, nq - 1)
    cur_q = pl.BlockSpec((t, W), lambda r, kb: (kb, qcol(r)))
    nxt_q = pl.BlockSpec((t, W), lambda r, kb: (nxt(kb), qcol(r)))
    cur_o = pl.BlockSpec((t, W), lambda r, kb: (kb, r))
    nxt_o = pl.BlockSpec((t, W), lambda r, kb: (nxt(kb), r))
    tab = pl.BlockSpec((t, LANES), lambda r, kb: (kb, r))

    def body(k_ref, v_ref, qc_ref, qn_ref, doc_ref, don_ref, oc_ref, on_ref, lc_ref, ln_ref, c_ref, sa_ref, sb_ref,
             dk_ref, dv_ref):
        lo = _lo_lanes()
        ri = lax.broadcasted_iota(jnp.int32, (t, t), 0)
        ci = lax.broadcasted_iota(jnp.int32, (t, t), 1)
        neg_c = jnp.where(ci <= ri, 0.0, NEG_INF)
        neg_n = jnp.where((ci >= ri) & (pl.program_id(1) + 1 < nq), 0.0, NEG_INF)
        blocks = []
        for q_ref, do_ref, o_ref, l_ref, neg in ((qc_ref, doc_ref, oc_ref, lc_ref, neg_c), (qn_ref, don_ref, on_ref, ln_ref, neg_n)):
            s, dp, lse, d = _dil_scores(lo, q_ref, do_ref, o_ref, l_ref, ((k_ref, v_ref),))
            pr = jnp.exp(s + neg[None] - lse)
            blocks.append((q_ref, do_ref, pr.astype(BF16), (pr * (dp - d)).astype(BF16)))
        for p in range(N_PAIRS):
            cols = _pair_cols(2 * p)
            dk2 = jnp.zeros((t, LANES), F32)
            dv2 = jnp.zeros((t, LANES), F32)
            for q_ref, do_ref, pr, ds in blocks:
                for h in (2 * p, 2 * p + 1):
                    hm = _head_lanes(lo, h)
                    dv2 = dv2 + _dot(pr[h], jnp.where(hm, do_ref[:, cols], 0), TN)
                    dk2 = dk2 + _dot(ds[h], jnp.where(hm, q_ref[:, cols], 0), TN)
            dk_ref[:, cols] = _rotate(dk2, c_ref[...], sa_ref[...], sb_ref[...], -1.0).astype(BF16)
            dv_ref[:, cols] = dv2.astype(BF16)

    kcur = pl.BlockSpec((t, W), lambda r, kb: (kb, kcol(r)))
    vcur = pl.BlockSpec((t, W), lambda r, kb: (kb, vcol(r)))
    return pl.pallas_call(
        body, grid=(R, nq),
        in_specs=[kcur, vcur, cur_q, nxt_q, cur_o, nxt_o, cur_o, nxt_o, cur_o, nxt_o, tab, tab, tab],
        out_specs=[cur_o, cur_o], out_shape=[jax.ShapeDtypeStruct((L, R * W), BF16)] * 2,
        compiler_params=_params("parallel", "parallel"), name=name)(x, x, x, x, do, do, o, o, lse, lse, *tabs)


def _fox_operands(q2, k2, kb2, lo, hh):
    lane = lax.broadcasted_iota(jnp.int32, (1, LANES), 1)
    if hh == 0:
        ones = ((lane >= HEAD_DIM) & (lane < HEAD_DIM + 3)).astype(BF16)
        return jnp.where(lo, q2, ones), jnp.where(lo, k2, kb2)
    ones = (lane < 3).astype(BF16)
    return jnp.where(lo, ones, q2), jnp.where(lo, kb2, k2)


def _causal_neg(t):
    ri = lax.broadcasted_iota(jnp.int32, (t, t), 0)
    ci = lax.broadcasted_iota(jnp.int32, (t, t), 1)
    return jnp.where(ci <= ri, 0.0, NEG_INF)


def _fox_fwd(name, qkv, kbias, t):
    S = qkv.shape[0]
    W = D_MODEL
    nq = S // t
    rep = t // LANES

    def body(q_ref, k_ref, v_ref, kb_ref, o_ref, lse_ref, m_scr, l_scr, acc_scr):
        qi, j = pl.program_id(0), pl.program_id(1)
        lo = _lo_lanes()

        @pl.when(j == 0)
        def _():
            m_scr[...] = jnp.full_like(m_scr, NEG_INF)
            l_scr[...] = jnp.zeros_like(l_scr)
            acc_scr[...] = jnp.zeros_like(acc_scr)

        def step(masked):
            neg = _causal_neg(t) if masked else None

            def pair(p, carry):
                cs = pl.ds(pl.multiple_of(p * LANES, LANES), LANES)
                q2, k2, v2, kb2 = q_ref[:, cs], k_ref[:, cs], v_ref[:, cs], kb_ref[:, cs]
                pvs, alphas = [], []
                for hh in range(2):
                    hm = lo if hh == 0 else jnp.logical_not(lo)
                    qh, kh = _fox_operands(q2, k2, kb2, lo, hh)
                    s = _dot(qh, kh, NT)
                    if masked:
                        s = s + neg
                    h = 2 * p + hh
                    m_prev = m_scr[h]
                    m_new = jnp.maximum(m_prev, jnp.max(s, axis=1, keepdims=True))
                    pe = jnp.exp(s - jnp.tile(m_new, (1, rep)))
                    alpha = jnp.exp(m_prev - m_new)
                    l_scr[h] = alpha * l_scr[h] + jnp.sum(pe, axis=1, keepdims=True)
                    m_scr[h] = m_new
                    pvs.append(_dot(pe.astype(BF16), jnp.where(hm, v2, 0), NN))
                    alphas.append(alpha)
                acc_scr[:, cs] = acc_scr[:, cs] * jnp.where(lo, alphas[0], alphas[1]) + pvs[0] + pvs[1]
                return carry

            lax.fori_loop(0, N_PAIRS, pair, 0)

        @pl.when(j < qi)
        def _():
            step(False)

        @pl.when(j == qi)
        def _():
            step(True)

        @pl.when(j == nq - 1)
        def _():
            for p in range(N_PAIRS):
                cols = slice(p * LANES, (p + 1) * LANES)
                l2 = jnp.where(lo, l_scr[2 * p], l_scr[2 * p + 1])
                m2 = jnp.where(lo, m_scr[2 * p], m_scr[2 * p + 1])
                o_ref[:, cols] = (acc_scr[:, cols] / l2).astype(BF16)
                lse_ref[:, cols] = m2 + jnp.log(l2)

    kv = lambda col: pl.BlockSpec((t, W), lambda qi, j: (jnp.minimum(j, qi), col))
    own = pl.BlockSpec((t, W), lambda qi, j: (qi, 0))
    return pl.pallas_call(
        body, grid=(nq, nq), in_specs=[own, kv(1), kv(2), kv(0)], out_specs=[own, own],
        out_shape=[jax.ShapeDtypeStruct((S, W), BF16), jax.ShapeDtypeStruct((S, W), F32)],
        scratch_shapes=[pltpu.VMEM((N_HEADS, t, LANES), F32), pltpu.VMEM((N_HEADS, t, LANES), F32), pltpu.VMEM((t, W), F32)],
        compiler_params=_params("parallel", "arbitrary"), name=name)(qkv, qkv, qkv, kbias)


def _fox_head_grads(qh, kh, v2, doh, neg, lse_h, d_h, rep):
    s = _dot(qh, kh, NT)
    if neg is not None:
        s = s + neg
    p = jnp.exp(s - jnp.tile(lse_h, (1, rep)))
    return p, p * (_dot(doh, v2, NT) - d_h)


def _fox_bwd(name, qkv, kbias, do, o, lse, t):
    S = qkv.shape[0]
    W = D_MODEL
    nq = S // t
    rep = t // LANES

    def body(q_ref, k_ref, v_ref, kb_ref, do_ref, o_ref, lse_ref, dq_ref, dk_ref, dv_ref, rs_ref, dc_ref, dq_scr, dk_scr, dv_scr):
        kb, j = pl.program_id(0), pl.program_id(1)
        lo = _lo_lanes()
        lane = lax.broadcasted_iota(jnp.int32, (1, LANES), 1)
        rows = pl.ds(pl.multiple_of(j * t, t), t)

        @pl.when((kb == 0) & (j == 0))
        def _():
            dq_scr[...] = jnp.zeros_like(dq_scr)
            rs_ref[...] = jnp.zeros_like(rs_ref)

        @pl.when(j == 0)
        def _():
            dk_scr[...] = jnp.zeros_like(dk_scr)
            dv_scr[...] = jnp.zeros_like(dv_scr)
            dc_ref[...] = jnp.zeros_like(dc_ref)

        def step(masked):
            neg = _causal_neg(t) if masked else None

            def pair(p, carry):
                cs = pl.ds(pl.multiple_of(p * LANES, LANES), LANES)
                q2, k2, v2, kb2, do2 = q_ref[:, cs], k_ref[:, cs], v_ref[:, cs], kb_ref[:, cs], do_ref[:, cs]
                dd = do2.astype(F32) * o_ref[:, cs].astype(F32)
                lse_h = _rep_rows(lse_ref[:, cs], lo)
                dq2 = jnp.zeros((t, LANES), F32)
                dv2 = jnp.zeros((t, LANES), F32)
                dk2 = jnp.zeros((t, LANES), F32)
                for hh in range(2):
                    hm = lo if hh == 0 else jnp.logical_not(lo)
                    qh, kh = _fox_operands(q2, k2, kb2, lo, hh)
                    doh = jnp.where(hm, do2, 0)
                    d_h = jnp.sum(jnp.where(hm, dd, 0.0), axis=1, keepdims=True)
                    pr, ds = _fox_head_grads(qh, kh, v2, doh, neg, lse_h[hh], d_h, rep)
                    rs_ref[rows, :] += jnp.where(lane == 2 * p + hh, jnp.sum(ds, axis=1, keepdims=True), 0.0)
                    dc_ref[p, hh:hh + 1, :] += jnp.sum(ds, axis=0, keepdims=True)
                    dsb = ds.astype(BF16)
                    dv2 = dv2 + _dot(pr.astype(BF16), doh, TN)
                    dk2 = dk2 + _dot(dsb, jnp.where(hm, q2, 0), TN)
                    dq2 = dq2 + _dot(dsb, jnp.where(hm, k2, 0), NN)
                dv_scr[:, cs] += dv2
                dk_scr[:, cs] += dk2
                dq_scr[rows, cs] += dq2
                return carry

            lax.fori_loop(0, N_PAIRS, pair, 0)
            if masked:
                dq_ref[...] = (dq_scr[rows, :] * SOFTMAX_SCALE).astype(BF16)

        @pl.when(j > kb)
        def _():
            step(False)

        @pl.when(j == kb)
        def _():
            step(True)

        @pl.when(j == nq - 1)
        def _():
            dv_ref[...] = dv_scr[...].astype(BF16)
            dk_ref[...] = dk_scr[...].astype(BF16)

    qrow = pl.BlockSpec((t, W), lambda kb, j: (jnp.maximum(j, kb), 0))
    krow = lambda col: pl.BlockSpec((t, W), lambda kb, j: (kb, col))
    own = pl.BlockSpec((t, W), lambda kb, j: (kb, 0))
    wide = jax.ShapeDtypeStruct((S, W), BF16)
    return pl.pallas_call(
        body, grid=(nq, nq), in_specs=[qrow, krow(1), krow(2), krow(0), qrow, qrow, qrow],
        out_specs=[own, own, own, pl.BlockSpec((S, LANES), lambda kb, j: (0, 0)), pl.BlockSpec((N_PAIRS, 2, t), lambda kb, j: (0, 0, kb))],
        out_shape=[wide, wide, wide, jax.ShapeDtypeStruct((S, LANES), F32), jax.ShapeDtypeStruct((N_PAIRS, 2, S), F32)],
        scratch_shapes=[pltpu.VMEM((S, W), F32), pltpu.VMEM((t, W), F32), pltpu.VMEM((t, W), F32)],
        compiler_params=pltpu.CompilerParams(dimension_semantics=("arbitrary", "arbitrary"), vmem_limit_bytes=FOX_BWD_VMEM),
        name=name)(qkv, qkv, qkv, kbias, do, o, lse)


def _combine(name, os_, lses, tm=256):
    S, W = os_[0].shape
    G = len(os_)

    def body(*refs):
        o_refs, l_refs = refs[:G], refs[G:2 * G]
        o_ref, lse_ref = refs[2 * G:]
        ls = [r[...] for r in l_refs]
        m = functools.reduce(jnp.maximum, ls)
        ws = [jnp.exp(l - m) for l in ls]
        den = functools.reduce(jnp.add, ws)
        num = functools.reduce(jnp.add, [w * r[...] for w, r in zip(ws, o_refs)])
        o_ref[...] = (num / den).astype(BF16)
        lse_ref[...] = m + jnp.log(den)

    row = pl.BlockSpec((tm, W), lambda i: (i, 0))
    return pl.pallas_call(
        body, grid=(S // tm,), in_specs=[row] * (2 * G), out_specs=[row, row],
        out_shape=[jax.ShapeDtypeStruct((S, W), BF16), jax.ShapeDtypeStruct((S, W), F32)],
        compiler_params=_params("parallel"), name=name)(*os_, *lses)


def _tri_matmul(tri, x):
    hi, mid, lo = _split3(x)
    return _dot(tri, hi, NN) + _dot(tri, mid, NN) + _dot(tri, lo, NN)


def _split3(x):
    hi = x.astype(BF16)
    r1 = x - hi.astype(F32)
    mid = r1.astype(BF16)
    return hi, mid, (r1 - mid.astype(F32)).astype(BF16)


def _gate_fwd(name, z, bf, tb=512):
    S = z.shape[0]

    def body(z_ref, b_ref, kb_ref, carry):
        @pl.when(pl.program_id(0) == 0)
        def _():
            carry[...] = jnp.zeros_like(carry)

        lf = jax.nn.log_sigmoid(z_ref[...] + b_ref[...])
        ri = lax.broadcasted_iota(jnp.int32, (tb, tb), 0)
        ci = lax.broadcasted_iota(jnp.int32, (tb, tb), 1)
        tri = (ci <= ri).astype(BF16)
        c = _tri_matmul(tri, lf) + carry[...]
        carry[...] = c[tb - 1:tb, :]
        head = lax.broadcasted_iota(jnp.int32, (LANES, D_MODEL), 0)
        col = lax.broadcasted_iota(jnp.int32, (LANES, D_MODEL), 1)
        base = (head >> 1) * LANES + jnp.where((head & 1) == 0, HEAD_DIM, 0)
        kb = jnp.zeros((tb, D_MODEL), F32)
        for i, piece in enumerate(_split3(-c)):
            place = ((col == base + i) & (head < N_HEADS)).astype(BF16)
            kb = kb + _dot(piece, place, NN)
        kb_ref[...] = kb.astype(BF16)

    row = pl.BlockSpec((tb, LANES), lambda i: (i, 0))
    return pl.pallas_call(
        body, grid=(S // tb,), in_specs=[row, pl.BlockSpec((1, LANES), lambda i: (0, 0))],
        out_specs=pl.BlockSpec((tb, D_MODEL), lambda i: (i, 0)), out_shape=jax.ShapeDtypeStruct((S, D_MODEL), BF16),
        scratch_shapes=[pltpu.VMEM((1, LANES), F32)], compiler_params=_params("arbitrary"), name=name)(z, bf)


def _gate_bwd(name, dc, z, bf, tb=512):
    S = z.shape[0]
    nb = S // tb

    def body(dc_ref, z_ref, b_ref, dz_ref, db_ref, carry):
        @pl.when(pl.program_id(0) == 0)
        def _():
            carry[...] = jnp.zeros_like(carry)
            db_ref[...] = jnp.zeros_like(db_ref)

        ri = lax.broadcasted_iota(jnp.int32, (tb, tb), 0)
        ci = lax.broadcasted_iota(jnp.int32, (tb, tb), 1)
        tri = (ci >= ri).astype(BF16)
        dlf = _tri_matmul(tri, dc_ref[...]) + carry[...]
        carry[...] = dlf[0:1, :]
        dz = dlf * jax.nn.sigmoid(-(z_ref[...] + b_ref[...]))
        dz_ref[...] = dz
        db_ref[...] += jnp.sum(dz, axis=0, keepdims=True)

    row = pl.BlockSpec((tb, LANES), lambda i: (nb - 1 - i, 0))
    vec = pl.BlockSpec((1, LANES), lambda i: (0, 0))
    return pl.pallas_call(
        body, grid=(nb,), in_specs=[row, row, vec], out_specs=[row, vec],
        out_shape=[jax.ShapeDtypeStruct((S, LANES), F32), jax.ShapeDtypeStruct((1, LANES), F32)],
        scratch_shapes=[pltpu.VMEM((1, LANES), F32)], compiler_params=_params("arbitrary"), name=name)(dc, z, bf)


def _ffn_gu(name, n, wgu, comm=None, tm=1024):
    S, D = n.shape
    nb = N_DEV // 2

    def body(n_ref, wg_ref, wu_ref, gu_ref, act_ref):
        x = n_ref[...]
        g = _dot(x, wg_ref[...], NN)
        u = _dot(x, wu_ref[...], NN)
        gu_ref[0] = g.astype(BF16)
        gu_ref[1] = u.astype(BF16)
        act_ref[...] = (g * jax.nn.sigmoid(g) * u).astype(BF16)

    return _call(
        name, body, (nb, S // tm),
        [pl.BlockSpec((tm, D), lambda j, i: (i, 0)), pl.BlockSpec((None, D, FF_BLK), lambda j, i: (j, 0, 0)),
         pl.BlockSpec((None, D, FF_BLK), lambda j, i: (j + nb, 0, 0))],
        [pl.BlockSpec((2, None, tm, FF_BLK), lambda j, i: (0, j, i, 0)), pl.BlockSpec((None, tm, FF_BLK), lambda j, i: (j, i, 0))],
        [jax.ShapeDtypeStruct((2, nb, S, FF_BLK), BF16), jax.ShapeDtypeStruct((nb, S, FF_BLK), BF16)], [],
        (n, wgu, wgu), ("parallel", "parallel"), comm)


def _ffn_down(name, act, wd, resid, comm=None, tm=1024):
    nb, S, _ = act.shape
    D = wd.shape[1]

    def epilogue(acc, ex, outs, j):
        outs[0][...] = acc + ex[0][...]

    o_spec = pl.BlockSpec((tm, D), lambda i, j, k: (i, 0))
    return _mm_call(name, (S // tm, 1, nb), act, pl.BlockSpec((None, tm, FF_BLK), lambda i, j, k: (k, i, 0)),
                    wd, pl.BlockSpec((FF_BLK, D), lambda i, j, k: (k, 0)), NN,
                    [jax.ShapeDtypeStruct((S, D), F32)], [o_spec], (tm, D), epilogue, (resid,), (o_spec,), comm=comm)


def _ffn_dact(name, dh, wd, gu, comm=None, tm=512):
    S, D = dh.shape
    nb = N_DEV // 2

    def epilogue(acc, ex, outs, j):
        g = ex[0][0].astype(F32)
        u = ex[0][1].astype(F32)
        sig = jax.nn.sigmoid(g)
        outs[0][0] = (acc * u * (sig * (1.0 + g * (1.0 - sig)))).astype(BF16)
        outs[0][1] = (acc * (g * sig)).astype(BF16)

    gu_spec = pl.BlockSpec((2, None, tm, FF_BLK), lambda j, i, k: (0, j, i, 0))
    return _mm_call(name, (nb, S // tm, 1), dh, pl.BlockSpec((tm, D), lambda j, i, k: (i, 0)),
                    wd, pl.BlockSpec((FF_BLK, D), lambda j, i, k: (j, 0)), NT,
                    [jax.ShapeDtypeStruct((2, nb, S, FF_BLK), BF16)], [gu_spec], (tm, FF_BLK), epilogue, (gu,), (gu_spec,),
                    col_axis=0, comm=comm)


def _ffn_dwgu(name, n, dgu, comm=None, tm=1024, tk=1024):
    S, D = n.shape
    dgu8 = dgu.reshape(N_DEV, S, FF_BLK)
    return _mm_call(name, (N_DEV, D // tm, S // tk), n, pl.BlockSpec((tk, tm), lambda d, i, k: (k, i)),
                    dgu8, pl.BlockSpec((None, tk, FF_BLK), lambda d, i, k: (d, k, 0)), TN,
                    [jax.ShapeDtypeStruct((N_DEV, D, FF_BLK), BF16)],
                    [pl.BlockSpec((None, tm, FF_BLK), lambda d, i, k: (d, i, 0))], (tm, FF_BLK), comm=comm)


def _ffn_dwd(name, act, dh, tk=1024):
    nb, S, _ = act.shape
    D = dh.shape[1]
    out = _mm_call(name, (nb, 1, S // tk), act, pl.BlockSpec((None, tk, FF_BLK), lambda b, j, k: (b, k, 0)),
                   dh, pl.BlockSpec((tk, D), lambda b, j, k: (k, 0)), TN,
                   [jax.ShapeDtypeStruct((nb, FF_BLK, D), BF16)],
                   [pl.BlockSpec((None, FF_BLK, D), lambda b, j, k: (b, 0, 0))], (FF_BLK, D))[0]
    return out.reshape(N_DEV, FF_BLK // 2, D)


def _ffn_dn(name, dgu, wgu, comm=None, tm=1024):
    S = dgu.shape[2]
    D = wgu.shape[1]
    dgu8 = dgu.reshape(N_DEV, S, FF_BLK)
    return _mm_call(name, (S // tm, 1, N_DEV), dgu8, pl.BlockSpec((None, tm, FF_BLK), lambda i, j, k: (k, i, 0)),
                    wgu, pl.BlockSpec((None, D, FF_BLK), lambda i, j, k: (k, 0, 0)), NT,
                    [jax.ShapeDtypeStruct((S, D), F32)], [pl.BlockSpec((tm, D), lambda i, j, k: (i, 0))], (tm, D), comm=comm)


def _adamw(name, parts, w, m, v, tr):
    rows, cols = w.shape
    n_parts = len(parts)
    c1 = 1.0 - ADAM_B1 ** ADAM_STEP
    c2 = 1.0 - ADAM_B2 ** ADAM_STEP

    def body(*refs):
        p_refs = refs[:n_parts]
        w_ref, m_ref, v_ref, g_ref, d_ref, nm_ref, nv_ref = refs[n_parts:]
        g = p_refs[0][...].astype(F32)
        for r in p_refs[1:]:
            g = g + r[...].astype(F32)
        mm = ADAM_B1 * m_ref[...] + (1.0 - ADAM_B1) * g
        vv = ADAM_B2 * v_ref[...] + (1.0 - ADAM_B2) * (g * g)
        g_ref[...] = g
        nm_ref[...] = mm
        nv_ref[...] = vv
        d_ref[...] = -ADAM_LR * ((mm / c1) / (jnp.sqrt(vv / c2) + ADAM_EPS) + ADAM_WD * w_ref[...])

    blk = pl.BlockSpec((tr, cols), lambda i: (i, 0))
    out = jax.ShapeDtypeStruct((rows, cols), F32)
    return pl.pallas_call(
        body, grid=(rows // tr,), in_specs=[blk] * (n_parts + 3), out_specs=[blk] * 4, out_shape=[out] * 4,
        compiler_params=_params("parallel"), name=name)(*parts, w, m, v)


def _position():
    return lax.axis_index("x"), lax.axis_index("y"), lax.axis_index("c")


def _other_chips():
    x, y, _ = _position()
    return [(1 - x, y), (x, 1 - y), (1 - x, 1 - y)]


def _remote(src, dst, send, recv, k, to):
    return pltpu.make_async_remote_copy(src_ref=src, dst_ref=dst, send_sem=send.at[k], recv_sem=recv.at[k],
                                        device_id=to, device_id_type=MESH)


def _ag_send(blocks, direct=False):
    n_peer = 7 if direct else 4

    def copies(ins, outs, send, recv, local, r0=0, l0=0):
        x, y, c = _position()
        me = 4 * x + 2 * y + c
        peers = [(x, y, 1 - c)] + [(px, py, c) for px, py in _other_chips()]
        if direct:
            peers += [(px, py, 1 - c) for px, py in _other_chips()]
        cps = []
        for t, (src, dst) in enumerate(zip(ins, outs)):
            cps.append(pltpu.make_async_copy(src, dst.at[me], local.at[l0 + t]))
            cps += [_remote(src, dst.at[me], send, recv, r0 + n_peer * t + k, to) for k, to in enumerate(peers)]
        return cps

    outs = tuple(jax.ShapeDtypeStruct((N_DEV,) + b.shape, b.dtype) for b in blocks)
    return _Comm(tuple(blocks), outs, {}, copies, n_peer * len(blocks), len(blocks))


def _ag_forward(bufs):
    def copies(ins, outs, send, recv, local, r0=0, l0=0):
        x, y, c = _position()
        cps = []
        for t, buf in enumerate(outs):
            for k, (px, py) in enumerate(_other_chips()):
                slot = buf.at[4 * px + 2 * py + c]
                cps.append(_remote(slot, slot, send, recv, r0 + 3 * t + k, (x, y, 1 - c)))
        return cps

    outs = tuple(jax.ShapeDtypeStruct(b.shape, b.dtype) for b in bufs)
    return _Comm(tuple(bufs), outs, {t: t for t in range(len(bufs))}, copies, 3 * len(bufs), 0)


def _rs_swap(shares):
    def copies(ins, outs, send, recv, local, r0=0, l0=0):
        x, y, c = _position()
        return [_remote(src.at[:, 1 - c], dst, send, recv, r0 + t, (x, y, 1 - c)) for t, (src, dst) in enumerate(zip(ins, outs))]

    ins = tuple(s.reshape((4, 2) + s.shape[1:]) for s in shares)
    outs = tuple(jax.ShapeDtypeStruct((4,) + s.shape[1:], s.dtype) for s in shares)
    return _Comm(ins, outs, {}, copies, len(shares), 0)


def _rs_exchange(sums):
    def copies(ins, outs, send, recv, local, r0=0, l0=0):
        _, _, c = _position()
        return [_remote(src.at[2 * px + py], dst.at[k], send, recv, r0 + 3 * t + k, (px, py, c))
                for t, (src, dst) in enumerate(zip(ins, outs)) for k, (px, py) in enumerate(_other_chips())]

    outs = tuple(jax.ShapeDtypeStruct((3,) + s.shape[1:], s.dtype) for s in sums)
    return _Comm(tuple(sums), outs, {}, copies, 3 * len(sums), 0)


def _join(a, b):
    def copies(ins, outs, send, recv, local, r0=0, l0=0):
        return (a.copies(ins[:len(a.ins)], outs[:len(a.outs)], send, recv, local, r0, l0)
                + b.copies(ins[len(a.ins):], outs[len(a.outs):], send, recv, local, r0 + a.n_remote, l0 + a.n_local))

    aliases = {**a.aliases, **{len(a.ins) + i: len(a.outs) + o for i, o in b.aliases.items()}}
    return _Comm(a.ins + b.ins, a.outs + b.outs, aliases, copies, a.n_remote + b.n_remote, a.n_local + b.n_local)


def _comm_call(name, comm):
    return _call(name, lambda: None, (), [], [], [], [], (), (), comm)


def _pair_sum(name, share, got, core, tr):
    _, rows, cols = share.shape

    def body(c_ref, a_ref, b_ref, o_ref):
        o_ref[...] = (a_ref[...].astype(F32) + b_ref[...].astype(F32)).astype(o_ref.dtype)

    grid_spec = pltpu.PrefetchScalarGridSpec(
        num_scalar_prefetch=1, grid=(4, rows // tr),
        in_specs=[pl.BlockSpec((None, None, tr, cols), lambda q, i, c: (q, c[0], i, 0)),
                  pl.BlockSpec((None, tr, cols), lambda q, i, c: (q, i, 0))],
        out_specs=pl.BlockSpec((None, tr, cols), lambda q, i, c: (q, i, 0)))
    return pl.pallas_call(
        body, grid_spec=grid_spec, out_shape=jax.ShapeDtypeStruct((4, rows, cols), share.dtype),
        compiler_params=_params("parallel", "parallel"), name=name)(core, share.reshape(4, 2, rows, cols), got)


TENSORS = ("a_w_in", "a_w_out", "b_w_in", "b_w_out", "gu0", "gu1", "dn0", "dn1")
ROW_TILE = {"a_w_in": 256, "a_w_out": 128, "b_w_in": 256, "b_w_out": 128, "gu0": 256, "gu1": 256, "dn0": 176, "dn1": 176}
A_BLK = 9 * D_MODEL // N_DEV
B_BLK = 386
B_IN = 3 * D_MODEL + N_HEADS
B_IN_PAD = 3 * D_MODEL + LANES


def kernel(x, a_norm, a_w_in, a_w_out, b_norm, b_w_in, b_f, b_w_out, ffn_norm, ffn_w_gu, ffn_w_down, final_norm, loss_target, m_a_norm, m_a_w_in, m_a_w_out, m_b_norm, m_b_w_in, m_b_f, m_b_w_out, m_ffn_norm, m_ffn_w_gu, m_ffn_w_down, m_final_norm, v_a_norm, v_a_w_in, v_a_w_out, v_b_norm, v_b_w_in, v_b_f, v_b_w_out, v_ffn_norm, v_ffn_w_gu, v_ffn_w_down, v_final_norm):
    S = x.shape[1]
    xi, yi, ci = _position()
    dev = 4 * xi + 2 * yi + ci
    core = ci.reshape(1).astype(jnp.int32)
    h0, target = x.reshape(S, D_MODEL), loss_target.reshape(S, D_MODEL)

    def shards(a_in, a_out, b_in, b_out, gu, dn):
        return {"a_w_in": a_in[0], "a_w_out": a_out[0], "b_w_in": b_in[0], "b_w_out": b_out[0],
                "gu0": gu[0], "gu1": gu[1], "dn0": dn[0], "dn1": dn[1]}

    w_sh = shards(a_w_in, a_w_out, b_w_in, b_w_out, ffn_w_gu, ffn_w_down)
    m_sh = shards(m_a_w_in, m_a_w_out, m_b_w_in, m_b_w_out, m_ffn_w_gu, m_ffn_w_down)
    v_sh = shards(v_a_w_in, v_a_w_out, v_b_w_in, v_b_w_out, v_ffn_w_gu, v_ffn_w_down)
    wb = {n: w_sh[n].astype(BF16) for n in TENSORS}
    bf_pad = jnp.pad(b_f, ((0, 0), (0, LANES - N_HEADS)))
    tabs = _rope_tables(S)

    g_ain, g_aout = _comm_call("gather_a", _ag_send([wb["a_w_in"], wb["a_w_out"]]))
    g_ain, g_aout = _comm_call("forward_a", _ag_forward([g_ain, g_aout]))
    n0 = _rms_fwd("rms_a", h0, a_norm[0])
    later = [wb["b_w_in"], wb["b_w_out"], wb["gu0"], wb["dn0"], jnp.pad(b_norm, ((0, 7), (0, 0)))]
    proj_a, later = _a_proj("proj_a", n0, g_ain, tabs, _ag_send(later))
    groups = []
    for g, (window, dil) in enumerate(DILATED_PATTERNS):
        L = S // dil
        view = proj_a.reshape(L, dil * 9 * D_MODEL)
        cols = [(lambda r, g=g, tq=tq: r * 9 + g * 3 + tq) for tq in range(3)]
        groups.append((g, dil, L, view, cols))
    outs, lses = [], []
    for g, dil, L, view, cols in groups:
        o_g, lse_g = _dil_fwd("dil_fwd%d" % g, view, *cols, dil, L)
        outs.append(o_g.reshape(S, D_MODEL))
        lses.append(lse_g.reshape(S, D_MODEL))
    o_a, lse_a = _combine("dil_combine", outs, lses)
    w_a_out = g_aout.reshape(D_MODEL, D_MODEL)
    h1, (g_bin, g_bout, g_gu0, g_dn0, g_bnorm) = _matmul("out_a", o_a, w_a_out, "nn", F32, TM, 1024, 1024, resid=h0,
                                                         comm=_ag_forward(later))

    n1 = _rms_fwd("rms_f0", h1, ffn_norm[0])
    gu0, act0, g_gu1 = _ffn_gu("gu_f0", n1, g_gu0, _ag_send([wb["gu1"]]))
    w_dn0 = g_dn0.reshape(D_FF, D_MODEL)
    h2, g_dn1 = _ffn_down("down_f0", act0, w_dn0, h1, _ag_send([wb["dn1"]]))

    b_norm_full = g_bnorm[:, 0].reshape(D_MODEL)
    w_b_in = g_bin.transpose(1, 0, 2).reshape(D_MODEL, B_IN)
    w_b_gate = jnp.pad(w_b_in[:, 3 * D_MODEL:], ((0, 0), (0, LANES - N_HEADS)))
    w_b_cat = jnp.concatenate([w_b_in[:, :3 * D_MODEL], w_b_gate], axis=1)
    w_b_out = g_bout.reshape(D_MODEL, D_MODEL)
    n2 = _rms_fwd("rms_b", h2, b_norm_full)
    qkv, (g_gu1, g_dn1) = _matmul("proj_b", n2, w_b_in[:, :3 * D_MODEL], "nn", BF16, TM, 1024, 1024, col0_scale=SOFTMAX_SCALE,
                                  comm=_ag_forward([g_gu1, g_dn1]))
    z = _matmul("gate_b", n2, w_b_gate, "nn", F32, TM, LANES, 1024)
    kbias = _gate_fwd("gate_cumsum", z, bf_pad)
    tf = min(S, 512)
    o_b, lse_b = _fox_fwd("fox_fwd", qkv, kbias, tf)
    h3 = _matmul("out_b", o_b, w_b_out, "nn", F32, TM, 1024, 1024, resid=h2)

    w_dn1 = g_dn1.reshape(D_FF, D_MODEL)
    n3 = _rms_fwd("rms_f1", h3, ffn_norm[1])
    gu1, act1 = _ffn_gu("gu_f1", n3, g_gu1)
    h4 = _ffn_down("down_f1", act1, w_dn1, h3)[0]

    dh4, d_final, loss = _loss_head("loss_head", h4, final_norm, target)

    share, got, sums, others = {}, {}, {}, {}

    def pair_sums(*names):
        for n in names:
            sums[n] = _pair_sum("pair_" + n, share[n], got[n], core, ROW_TILE[n])

    dgu1 = _ffn_dact("dact_f1", dh4, w_dn1, gu1)[0]
    share["dn1"] = _ffn_dwd("dwd_f1", act1, dh4)
    share["gu1"] = _ffn_dwgu("dwgu_f1", n3, dgu1)[0]
    dn3, got["gu1"], got["dn1"] = _ffn_dn("dn_f1", dgu1, g_gu1, _rs_swap([share["gu1"], share["dn1"]]))
    dh3, d_ffn1 = _rms_bwd("rmsb_f1", dn3, h3, ffn_norm[1], dh4)
    pair_sums("gu1", "dn1")

    do_b = _matmul("dout_b", dh3, w_b_out, "nt", BF16, TM, 1024, 1024)
    share["b_w_out"] = _matmul("dwout_b", o_b, dh3, "tn", BF16, TM, 1024, 1024).reshape(N_DEV, 128, D_MODEL)
    dq_b, dk_b, dv_b, ds_rowsum, ds_colsum = _fox_bwd("fox_bwd", qkv, kbias, do_b, o_b, lse_b, tf)
    dc = ds_rowsum[:, :N_HEADS] - ds_colsum.reshape(N_HEADS, S).T
    dz, d_bf = _gate_bwd("gate_bwd", jnp.pad(dc, ((0, 0), (0, LANES - N_HEADS))), z, bf_pad)
    dproj_b = jnp.concatenate([dq_b, dk_b, dv_b, dz.astype(BF16)], axis=1)
    dw_b_in, (others["gu1"],) = _matmul("dwin_b", n2, dproj_b, "tn", BF16, TM, B_IN_PAD // 5, 1024, comm=_rs_exchange([sums["gu1"]]))
    dn2, (others["dn1"],) = _matmul("dn_b", dproj_b, w_b_cat, "nt", F32, TM, 1024, B_IN_PAD // 5, comm=_rs_exchange([sums["dn1"]]))
    dh2, d_bnorm = _rms_bwd("rmsb_b", dn2, h2, b_norm_full, dh3)
    share["b_w_in"] = dw_b_in[:, :B_IN].reshape(D_MODEL, N_DEV, B_BLK).transpose(1, 0, 2)

    dgu0, got["b_w_in"], got["b_w_out"] = _ffn_dact("dact_f0", dh2, w_dn0, gu0, _rs_swap([share["b_w_in"], share["b_w_out"]]))
    share["dn0"] = _ffn_dwd("dwd_f0", act0, dh2)
    pair_sums("b_w_in", "b_w_out")
    share["gu0"], others["b_w_in"], others["b_w_out"] = _ffn_dwgu(
        "dwgu_f0", n1, dgu0, _rs_exchange([sums["b_w_in"], sums["b_w_out"]]))
    dn1, got["gu0"], got["dn0"] = _ffn_dn("dn_f0", dgu0, g_gu0, _rs_swap([share["gu0"], share["dn0"]]))
    dh1, d_ffn0 = _rms_bwd("rmsb_f0", dn1, h1, ffn_norm[0], dh2)
    pair_sums("gu0", "dn0")

    do_a = _matmul("dout_a", dh1, w_a_out, "nt", BF16, TM, 1024, 1024)
    share["a_w_out"] = _matmul("dwout_a", o_a, dh1, "tn", BF16, TM, 1024, 1024).reshape(N_DEV, 128, D_MODEL)
    pieces = []
    for g, dil, L, view, cols in groups:
        sv = lambda a: a.reshape(L, dil * a.shape[1])
        args = (view, sv(do_a), sv(o_a), sv(lse_a), tuple(sv(tb) for tb in tabs), *cols, dil, L)
        dq_g = _dil_dq("dil_dq%d" % g, *args)
        dk_g, dv_g = _dil_dkv("dil_dkv%d" % g, *args)
        pieces += [a.reshape(S, D_MODEL) for a in (dq_g, dk_g, dv_g)]
    dproj_a = jnp.concatenate(pieces, axis=1)
    share["a_w_in"], others["gu0"], others["dn0"] = _mm_call(
        "dwin_a", (N_DEV, 1, S // 1024), n0, pl.BlockSpec((1024, D_MODEL), lambda d, i, k: (k, 0)),
        dproj_a, pl.BlockSpec((1024, A_BLK), lambda d, i, k: (k, d)), TN, [jax.ShapeDtypeStruct((N_DEV, D_MODEL, A_BLK), BF16)],
        [pl.BlockSpec((None, D_MODEL, A_BLK), lambda d, i, k: (d, 0, 0))], (D_MODEL, A_BLK),
        comm=_rs_exchange([sums["gu0"], sums["dn0"]]))
    dn0, got["a_w_in"], got["a_w_out"] = _mm_call(
        "dn_a", (S // TM, 1, N_DEV), dproj_a, pl.BlockSpec((TM, A_BLK), lambda i, j, k: (i, k)),
        g_ain, pl.BlockSpec((None, D_MODEL, A_BLK), lambda i, j, k: (k, 0, 0)), NT, [jax.ShapeDtypeStruct((S, D_MODEL), F32)],
        [pl.BlockSpec((TM, D_MODEL), lambda i, j, k: (i, 0))], (TM, D_MODEL), comm=_rs_swap([share["a_w_in"], share["a_w_out"]]))
    dx, d_anorm = _rms_bwd("rmsb_a", dn0, h0, a_norm[0], dh1)
    pair_sums("a_w_in", "a_w_out")

    misc = jnp.concatenate([d_bf[:, :N_HEADS], loss[:, :1], jnp.zeros((1, D_MODEL - N_HEADS - 1), F32)], axis=1)
    small = jnp.concatenate([d_anorm, d_ffn0, d_ffn1, d_final, d_bnorm, misc, jnp.zeros((2, D_MODEL), F32)], axis=0)
    others["a_w_in"], others["a_w_out"], small_all = _comm_call(
        "exchange_a", _join(_rs_exchange([sums["a_w_in"], sums["a_w_out"]]), _ag_send([small], direct=True)))

    outs = {}
    for n in TENSORS:
        mine = lax.dynamic_index_in_dim(sums[n], 2 * xi + yi, axis=0, keepdims=False)
        outs[n] = _adamw("adamw_" + n, [mine] + [others[n][k] for k in range(3)], w_sh[n], m_sh[n], v_sh[n], ROW_TILE[n])

    pad_vec = lambda a: jnp.pad(a, ((0, 0), (0, D_MODEL - a.shape[1])))

    def small_pack(an, fn, fin, bf):
        return jnp.concatenate([an, fn, fin.reshape(1, D_MODEL), jnp.zeros((1, D_MODEL), F32), pad_vec(bf),
                                jnp.zeros((2, D_MODEL), F32)], axis=0)

    sg, sd, sm, sv = _adamw("adamw_small", [small_all[d] for d in range(N_DEV)], small_pack(a_norm, ffn_norm, final_norm, b_f),
                            small_pack(m_a_norm, m_ffn_norm, m_final_norm, m_b_f),
                            small_pack(v_a_norm, v_ffn_norm, v_final_norm, v_b_f), 8)
    g_bn = lax.dynamic_slice(sg[4:5], (0, dev * LANES), (1, LANES))
    bn = _adamw("adamw_b_norm", [g_bn], b_norm, m_b_norm, v_b_norm, 1)

    def tree(i):
        full = lambda name, ref: outs[name][i].reshape(ref.shape)
        sml = (sg, sd, sm, sv)[i]
        return dict(
            a_norm=sml[0:1], a_w_in=full("a_w_in", a_w_in), a_w_out=full("a_w_out", a_w_out), b_norm=bn[i],
            b_w_in=full("b_w_in", b_w_in), b_f=sml[5:6, :N_HEADS], b_w_out=full("b_w_out", b_w_out), ffn_norm=sml[1:3],
            ffn_w_gu=jnp.stack([outs["gu0"][i], outs["gu1"][i]]).reshape(ffn_w_gu.shape),
            ffn_w_down=jnp.stack([outs["dn0"][i], outs["dn1"][i]]).reshape(ffn_w_down.shape), final_norm=sml[3])

    order = ("a_norm", "a_w_in", "a_w_out", "b_norm", "b_w_in", "b_f", "b_w_out", "ffn_norm", "ffn_w_gu", "ffn_w_down", "final_norm")
    result = [sg[5, N_HEADS], dx.reshape(x.shape)]
    for i in range(4):
        t = tree(i)
        result += [t[n] for n in order]
    return tuple(result)
```

```python
import functools
from typing import Callable, NamedTuple

import jax
import jax.numpy as jnp
from jax import lax
from jax.experimental import pallas as pl
from jax.experimental.pallas import tpu as pltpu

F32 = jnp.float32
BF16 = jnp.bfloat16

D_MODEL = 1024
N_HEADS = 16
HEAD_DIM = 64
N_PAIRS = N_HEADS // 2
LANES = 128
DILATED_PATTERNS = ((128, 1), (512, 4), (2048, 16))
BAND_STEPS = 128
ROT_DIM = HEAD_DIM // 4
ROPE_THETA = 500000.0
D_FF = 2816
RMS_EPS = 1e-6
NEG_INF = -1e30
SOFTMAX_SCALE = HEAD_DIM ** -0.5
N_DEV = 8
FF_BLK = 2 * D_FF // N_DEV
ADAM_LR, ADAM_B1, ADAM_B2, ADAM_EPS, ADAM_WD, ADAM_STEP = 0.001, 0.9, 0.999, 1e-08, 0.01, 10
VMEM_LIMIT = 52 * 1024 * 1024
FOX_BWD_VMEM = 60 * 1024 * 1024
TM = 1024
MESH = pl.DeviceIdType.MESH

NN = (((1,), (0,)), ((), ()))
NT = (((1,), (1,)), ((), ()))
TN = (((0,), (0,)), ((), ()))


def _params(*sem):
    return pltpu.CompilerParams(dimension_semantics=sem, vmem_limit_bytes=VMEM_LIMIT)


def _dot(a, b, dims):
    return lax.dot_general(a, b, dims, preferred_element_type=F32)


class _Comm(NamedTuple):
    ins: tuple
    outs: tuple
    aliases: dict
    copies: Callable
    n_remote: int
    n_local: int


def _call(name, body, grid, in_specs, out_specs, out_shape, scratch, args, sem, comm=None):
    if comm is None:
        return pl.pallas_call(body, grid=grid, in_specs=in_specs, out_specs=out_specs, out_shape=out_shape,
                              scratch_shapes=scratch, compiler_params=_params(*sem), name=name)(*args)
    n_in, n_out = len(in_specs), len(out_specs)
    n_ci, n_co = len(comm.ins), len(comm.outs)
    o0 = n_in + n_ci

    def hosted(*refs):
        c_ins, c_outs = refs[n_in:o0], refs[o0 + n_out:o0 + n_out + n_co]
        sems = refs[-3:]

        def start():
            for cp in comm.copies(c_ins, c_outs, *sems):
                cp.start()

        def wait():
            for cp in comm.copies(c_ins, c_outs, *sems):
                cp.wait()

        if not grid:
            start()
            body()
            wait()
            return
        ids = [pl.program_id(ax) for ax in range(len(grid))]
        pl.when(functools.reduce(jnp.logical_and, [i == 0 for i in ids]))(start)
        body(*refs[:n_in], *refs[o0:o0 + n_out], *refs[o0 + n_out + n_co:-3])
        pl.when(functools.reduce(jnp.logical_and, [i == g - 1 for i, g in zip(ids, grid)]))(wait)

    hbm = pl.BlockSpec(memory_space=pltpu.HBM)
    dma = pltpu.SemaphoreType.DMA
    return pl.pallas_call(
        hosted, grid=grid, in_specs=[*in_specs, *[hbm] * n_ci], out_specs=[*out_specs, *[hbm] * n_co],
        out_shape=[*out_shape, *comm.outs], input_output_aliases={n_in + i: n_out + o for i, o in comm.aliases.items()},
        scratch_shapes=[*scratch, dma((comm.n_remote,)), dma((comm.n_remote,)), dma((max(comm.n_local, 1),))],
        compiler_params=_params(*["arbitrary"] * len(grid)), name=name)(*args, *comm.ins)


def _mm_call(name, grid, a, a_spec, b, b_spec, dims, out_shapes, out_specs, acc_shape, epilogue=None,
             extras=(), extra_specs=(), col_axis=1, comm=None):
    nk = grid[2]
    n_extra = len(extras)
    n_out = len(out_shapes)

    def finish(res, ex, outs, j):
        if epilogue is None:
            outs[0][...] = res.astype(outs[0].dtype)
        else:
            epilogue(res, ex, outs, j)

    def body(*refs):
        a_ref, b_ref = refs[0], refs[1]
        ex = refs[2:2 + n_extra]
        outs = refs[2 + n_extra:2 + n_extra + n_out]
        j, k = pl.program_id(col_axis), pl.program_id(2)
        part = _dot(a_ref[...].astype(BF16), b_ref[...].astype(BF16), dims)
        if nk == 1:
            finish(part, ex, outs, j)
            return
        acc = refs[-1]

        @pl.when(k == 0)
        def _():
            acc[...] = part

        @pl.when((k > 0) & (k < nk - 1))
        def _():
            acc[...] += part

        @pl.when(k == nk - 1)
        def _():
            finish(acc[...] + part, ex, outs, j)

    return _call(name, body, grid, [a_spec, b_spec, *extra_specs], out_specs, out_shapes,
                 [] if nk == 1 else [pltpu.VMEM(acc_shape, F32)], (a, b, *extras), ("parallel", "parallel", "arbitrary"), comm)


def _matmul(name, a, b, mode, out_dtype, tm, tn, tk, resid=None, col0_scale=None, comm=None):
    if mode == "nn":
        (M, K), N = a.shape, b.shape[1]
        a_spec = pl.BlockSpec((tm, tk), lambda j, i, k: (i, k))
        b_spec = pl.BlockSpec((tk, tn), lambda j, i, k: (k, j))
        dims = NN
    elif mode == "nt":
        (M, K), N = a.shape, b.shape[0]
        a_spec = pl.BlockSpec((tm, tk), lambda j, i, k: (i, k))
        b_spec = pl.BlockSpec((tn, tk), lambda j, i, k: (j, k))
        dims = NT
    else:
        (K, M), N = a.shape, b.shape[1]
        a_spec = pl.BlockSpec((tk, tm), lambda j, i, k: (k, i))
        b_spec = pl.BlockSpec((tk, tn), lambda j, i, k: (k, j))
        dims = TN
    assert M % tm == 0 and N % tn == 0 and K % tk == 0, (name, M, N, K, tm, tn, tk)
    o_spec = pl.BlockSpec((tm, tn), lambda j, i, k: (i, j))
    extras, extra_specs, epilogue = (), (), None
    if resid is not None:
        extras, extra_specs = (resid,), (o_spec,)

        def epilogue(acc, ex, outs, j):
            outs[0][...] = (acc + ex[0][...]).astype(outs[0].dtype)

    elif col0_scale is not None:

        def epilogue(acc, ex, outs, j):
            outs[0][...] = (acc * jnp.where(j == 0, col0_scale, 1.0)).astype(outs[0].dtype)

    res = _mm_call(name, (N // tn, M // tm, K // tk), a, a_spec, b, b_spec, dims, [jax.ShapeDtypeStruct((M, N), out_dtype)],
                   [o_spec], (tm, tn), epilogue, extras, extra_specs, col_axis=0, comm=comm)
    return res[0] if comm is None else (res[0], res[1:])


def _rms_fwd(name, h, gain, tm=512):
    S, D = h.shape

    def body(h_ref, g_ref, n_ref):
        x = h_ref[...]
        rstd = lax.rsqrt(jnp.mean(x * x, axis=-1, keepdims=True) + RMS_EPS)
        n_ref[...] = (x * rstd * g_ref[...]).astype(BF16)

    return pl.pallas_call(
        body, grid=(S // tm,), in_specs=[pl.BlockSpec((tm, D), lambda i: (i, 0)), pl.BlockSpec((1, D), lambda i: (0, 0))],
        out_specs=pl.BlockSpec((tm, D), lambda i: (i, 0)), out_shape=jax.ShapeDtypeStruct((S, D), BF16),
        compiler_params=_params("parallel"), name=name)(h, gain.reshape(1, D))


def _rms_bwd(name, dn, h, gain, dres, tm=512):
    S, D = h.shape

    def body(dn_ref, h_ref, g_ref, r_ref, dh_ref, dg_ref):
        x = h_ref[...]
        rstd = lax.rsqrt(jnp.mean(x * x, axis=-1, keepdims=True) + RMS_EPS)
        xhat = x * rstd
        d = dn_ref[...]
        dxhat = d * g_ref[...]
        dh_ref[...] = rstd * (dxhat - xhat * jnp.mean(dxhat * xhat, axis=-1, keepdims=True)) + r_ref[...]

        @pl.when(pl.program_id(0) == 0)
        def _():
            dg_ref[...] = jnp.zeros_like(dg_ref)

        dg_ref[...] += jnp.sum(d * xhat, axis=0, keepdims=True)

    row = pl.BlockSpec((tm, D), lambda i: (i, 0))
    vec = pl.BlockSpec((1, D), lambda i: (0, 0))
    return pl.pallas_call(
        body, grid=(S // tm,), in_specs=[row, row, vec, row], out_specs=[row, vec],
        out_shape=[jax.ShapeDtypeStruct((S, D), F32), jax.ShapeDtypeStruct((1, D), F32)],
        compiler_params=_params("arbitrary"), name=name)(dn, h, gain.reshape(1, D), dres)


def _loss_head(name, h, gain, target, tm=512):
    S, D = h.shape

    def body(h_ref, g_ref, t_ref, dh_ref, dg_ref, loss_ref):
        x = h_ref[...]
        rstd = lax.rsqrt(jnp.mean(x * x, axis=-1, keepdims=True) + RMS_EPS)
        xhat = x * rstd
        err = xhat * g_ref[...] - t_ref[...]
        dy = err * (1.0 / D)
        dxhat = dy * g_ref[...]
        dh_ref[...] = rstd * (dxhat - xhat * jnp.mean(dxhat * xhat, axis=-1, keepdims=True))

        @pl.when(pl.program_id(0) == 0)
        def _():
            dg_ref[...] = jnp.zeros_like(dg_ref)
            loss_ref[...] = jnp.zeros_like(loss_ref)

        dg_ref[...] += jnp.sum(dy * xhat, axis=0, keepdims=True)
        part = 0.5 * jnp.sum(jnp.mean(err * err, axis=-1, keepdims=True), axis=0, keepdims=True)
        loss_ref[...] += jnp.broadcast_to(part, loss_ref.shape)

    row = pl.BlockSpec((tm, D), lambda i: (i, 0))
    vec = pl.BlockSpec((1, D), lambda i: (0, 0))
    return pl.pallas_call(
        body, grid=(S // tm,), in_specs=[row, vec, row], out_specs=[row, vec, pl.BlockSpec((1, LANES), lambda i: (0, 0))],
        out_shape=[jax.ShapeDtypeStruct((S, D), F32), jax.ShapeDtypeStruct((1, D), F32),
                   jax.ShapeDtypeStruct((1, LANES), F32)],
        compiler_params=_params("arbitrary"), name=name)(h, gain.reshape(1, D), target)


def _rope_tables(S):
    half = ROT_DIM // 2
    inv_freq = ROPE_THETA ** (-jnp.arange(half, dtype=F32) * 2.0 / ROT_DIM)
    ang = jnp.arange(S, dtype=F32)[:, None] * inv_freq[None, :]
    cos, sin = jnp.cos(ang), jnp.sin(ang)
    one = jnp.ones((S, HEAD_DIM - ROT_DIM), F32)
    zero = jnp.zeros((S, HEAD_DIM - ROT_DIM), F32)
    zh = jnp.zeros((S, half), F32)
    c = jnp.concatenate([cos, cos, one], axis=1)
    sa = jnp.concatenate([-sin, zh, zero], axis=1)
    sb = jnp.concatenate([zh, sin, zero], axis=1)
    return tuple(jnp.concatenate([t, t], axis=1) for t in (c, sa, sb))


def _rotate(x, c, sa, sb, sign):
    return x * c + sign * (pltpu.roll(x, LANES - ROT_DIM // 2, 1) * sa + pltpu.roll(x, ROT_DIM // 2, 1) * sb)


def _stage_chunks(scr, chunks):
    for c, x in enumerate(chunks):
        scr[c] = x


def _strided_rows(scr, c, r, n, R):
    return scr.at[c][pl.ds(r, n, stride=R), :]


def _a_proj(name, n, w, tabs, comm, tm=512):
    S, D = n.shape
    n_i = S // tm
    dils = [dil for _, dil in DILATED_PATTERNS]
    n_out = 3 * len(dils)

    def body(n_ref, w_ref, c_ref, sa_ref, sb_ref, *rest):
        outs, scr = rest[:n_out], rest[n_out]
        j = pl.program_id(0)
        acc = _dot(n_ref[...], w_ref[...], NN)
        c, sa, sb = c_ref[...], sa_ref[...], sb_ref[...]
        for J in range(n_out):
            R, kind = dils[J // 3], J % 3

            @pl.when(j == J)
            def _(J=J, R=R, kind=kind):
                chunks = [acc[:, b * LANES:(b + 1) * LANES] for b in range(N_PAIRS)]
                if kind < 2:
                    chunks = [_rotate(x, c, sa, sb, 1.0) * (SOFTMAX_SCALE if kind == 0 else 1.0) for x in chunks]
                if R == 1:
                    for b, x in enumerate(chunks):
                        outs[J][:, b * LANES:(b + 1) * LANES] = x.astype(BF16)
                    return
                _stage_chunks(scr, chunks)
                for r in range(R):
                    for b in range(N_PAIRS):
                        col = r * D_MODEL + b * LANES
                        outs[J][:, col:col + LANES] = _strided_rows(scr, b, r, tm // R, R).astype(BF16)

    def out_spec(J, R):
        return pl.BlockSpec((tm // R, R * D_MODEL), lambda j, i: (jnp.where(j == J, i, jnp.where(j < J, 0, n_i - 1)), 0))

    tab = pl.BlockSpec((tm, LANES), lambda j, i: (i, 0))
    res = _call(name, body, (n_out, n_i),
                [pl.BlockSpec((tm, D), lambda j, i: (i, 0)), pl.BlockSpec((D, D_MODEL), lambda j, i: (0, j)), tab, tab, tab],
                [out_spec(J, dils[J // 3]) for J in range(n_out)],
                [jax.ShapeDtypeStruct((S // dils[J // 3], dils[J // 3] * D_MODEL), BF16) for J in range(n_out)],
                [pltpu.VMEM((N_PAIRS, tm, LANES), F32)], (n, w, *tabs), ("arbitrary", "arbitrary"), comm)
    return [res[3 * g:3 * g + 3] for g in range(len(dils))], res[n_out:]


def _relayout(name, x, R, to_view, tm=512):
    rows, cols = x.shape
    S = rows if to_view else rows * R

    def body(x_ref, o_ref, scr):
        if to_view:
            _stage_chunks(scr, [x_ref[:, b * LANES:(b + 1) * LANES].astype(F32) for b in range(N_PAIRS)])
            for r in range(R):
                for b in range(N_PAIRS):
                    col = r * D_MODEL + b * LANES
                    o_ref[:, col:col + LANES] = _strided_rows(scr, b, r, tm // R, R).astype(o_ref.dtype)
        else:
            for r in range(R):
                for b in range(N_PAIRS):
                    col = r * D_MODEL + b * LANES
                    scr.at[b][pl.ds(r, tm // R, stride=R), :] = x_ref[:, col:col + LANES].astype(F32)
            for b in range(N_PAIRS):
                o_ref[:, b * LANES:(b + 1) * LANES] = scr[b].astype(o_ref.dtype)

    nat = pl.BlockSpec((tm, D_MODEL), lambda i: (i, 0))
    view = pl.BlockSpec((tm // R, R * D_MODEL), lambda i: (i, 0))
    out_shape = (S // R, R * D_MODEL) if to_view else (S, D_MODEL)
    return pl.pallas_call(
        body, grid=(S // tm,), in_specs=[nat if to_view else view], out_specs=view if to_view else nat,
        out_shape=jax.ShapeDtypeStruct(out_shape, x.dtype), scratch_shapes=[pltpu.VMEM((N_PAIRS, tm, LANES), F32)],
        compiler_params=_params("parallel"), name=name)(x)


def _lo_lanes():
    return lax.broadcasted_iota(jnp.int32, (1, LANES), 1) < HEAD_DIM


def _rep_rows(x2, lo):
    sw = pltpu.roll(x2, HEAD_DIM, 1)
    return jnp.where(lo, x2, sw), jnp.where(lo, sw, x2)


def _pair_cols(h):
    return slice((h // 2) * LANES, (h // 2 + 1) * LANES)


def _head_lanes(lo, h):
    return lo if h % 2 == 0 else jnp.logical_not(lo)


def _band_masks(t, first):
    ri = lax.broadcasted_iota(jnp.int32, (t, t), 0)
    ci = lax.broadcasted_iota(jnp.int32, (t, t), 1)
    neg_prev = jnp.where((ci >= ri) & jnp.logical_not(first), 0.0, NEG_INF)
    neg_cur = jnp.where(ci <= ri, 0.0, NEG_INF)
    return neg_prev, neg_cur


def _dil_specs(L, R, t, qcol, kcol, vcol):
    W = D_MODEL
    prev = lambda qi: jnp.maximum(qi - 1, 0)
    return dict(
        q=pl.BlockSpec((t, W), lambda r, qi: (qi, qcol(r))),
        kp=pl.BlockSpec((t, W), lambda r, qi: (prev(qi), kcol(r))), kc=pl.BlockSpec((t, W), lambda r, qi: (qi, kcol(r))),
        vp=pl.BlockSpec((t, W), lambda r, qi: (prev(qi), vcol(r))), vc=pl.BlockSpec((t, W), lambda r, qi: (qi, vcol(r))),
        own=pl.BlockSpec((t, W), lambda r, qi: (qi, r)), tab=pl.BlockSpec((t, LANES), lambda r, qi: (qi, r)))


def _dil_fwd(name, x, qcol, kcol, vcol, R, L):
    t = BAND_STEPS
    W = D_MODEL
    sp = _dil_specs(L, R, t, qcol, kcol, vcol)

    def body(q_ref, kp_ref, kc_ref, vp_ref, vc_ref, o_ref, lse_ref):
        lo = _lo_lanes()
        neg_p, neg_c = _band_masks(t, pl.program_id(1) == 0)
        s_p, s_c = [], []
        for h in range(N_HEADS):
            cols = _pair_cols(h)
            qh = jnp.where(_head_lanes(lo, h), q_ref[:, cols], 0)
            s_p.append(_dot(qh, kp_ref[:, cols], NT))
            s_c.append(_dot(qh, kc_ref[:, cols], NT))
        s_p = jnp.stack(s_p) + neg_p[None]
        s_c = jnp.stack(s_c) + neg_c[None]
        m = jnp.maximum(jnp.max(s_p, axis=2, keepdims=True), jnp.max(s_c, axis=2, keepdims=True))
        p_p, p_c = jnp.exp(s_p - m), jnp.exp(s_c - m)
        l = jnp.sum(p_p, axis=2, keepdims=True) + jnp.sum(p_c, axis=2, keepdims=True)
        inv, lse = 1.0 / l, m + jnp.log(l)
        p_p, p_c = p_p.astype(BF16), p_c.astype(BF16)
        for p in range(N_PAIRS):
            cols = _pair_cols(2 * p)
            o2 = jnp.zeros((t, LANES), F32)
            for h in (2 * p, 2 * p + 1):
                hm = _head_lanes(lo, h)
                pv = _dot(p_p[h], jnp.where(hm, vp_ref[:, cols], 0), NN) + _dot(p_c[h], jnp.where(hm, vc_ref[:, cols], 0), NN)
                o2 = o2 + pv * inv[h]
            o_ref[:, cols] = o2
            lse_ref[:, cols] = jnp.where(lo, lse[2 * p], lse[2 * p + 1])

    return pl.pallas_call(
        body, grid=(R, L // t), in_specs=[sp["q"], sp["kp"], sp["kc"], sp["vp"], sp["vc"]], out_specs=[sp["own"], sp["own"]],
        out_shape=[jax.ShapeDtypeStruct((L, R * W), F32), jax.ShapeDtypeStruct((L, R * W), F32)],
        compiler_params=_params("parallel", "parallel"), name=name)(x[0], x[1], x[1], x[2], x[2])


def _dil_scores(lo, q_ref, do_ref, o_ref, lse_ref, kv_refs):
    s = [[] for _ in kv_refs]
    dp = [[] for _ in kv_refs]
    lse, d = [], []
    for h in range(N_HEADS):
        cols = _pair_cols(h)
        hm = _head_lanes(lo, h)
        qh, doh = jnp.where(hm, q_ref[:, cols], 0), jnp.where(hm, do_ref[:, cols], 0)
        for i, (k_ref, v_ref) in enumerate(kv_refs):
            s[i].append(_dot(qh, k_ref[:, cols], NT))
            dp[i].append(_dot(doh, v_ref[:, cols], NT))
        lse.append(_rep_rows(lse_ref[:, cols], lo)[h % 2])
        dd = do_ref[:, cols].astype(F32) * o_ref[:, cols].astype(F32)
        d.append(jnp.sum(jnp.where(hm, dd, 0.0), axis=1, keepdims=True))
    return (*[jnp.stack(x) for x in s], *[jnp.stack(x) for x in dp], jnp.stack(lse), jnp.stack(d))


def _dil_dq(name, x, do, o, lse, tabs, qcol, kcol, vcol, R, L):
    t = BAND_STEPS
    W = D_MODEL
    sp = _dil_specs(L, R, t, qcol, kcol, vcol)

    def body(q_ref, kp_ref, kc_ref, vp_ref, vc_ref, do_ref, o_ref, lse_ref, c_ref, sa_ref, sb_ref, dq_ref):
        lo = _lo_lanes()
        neg_p, neg_c = _band_masks(t, pl.program_id(1) == 0)
        s_p, s_c, dp_p, dp_c, lse, d = _dil_scores(lo, q_ref, do_ref, o_ref, lse_ref, ((kp_ref, vp_ref), (kc_ref, vc_ref)))
        ds_p = (jnp.exp(s_p + neg_p[None] - lse) * (dp_p - d)).astype(BF16)
        ds_c = (jnp.exp(s_c + neg_c[None] - lse) * (dp_c - d)).astype(BF16)
        for p in range(N_PAIRS):
            cols = _pair_cols(2 * p)
            dq2 = jnp.zeros((t, LANES), F32)
            for h in (2 * p, 2 * p + 1):
                hm = _head_lanes(lo, h)
                dq2 = dq2 + _dot(ds_p[h], jnp.where(hm, kp_ref[:, cols], 0), NN) + _dot(ds_c[h], jnp.where(hm, kc_ref[:, cols], 0), NN)
            dq_ref[:, cols] = _rotate(dq2 * SOFTMAX_SCALE, c_ref[...], sa_ref[...], sb_ref[...], -1.0).astype(BF16)

    return pl.pallas_call(
        body, grid=(R, L // t),
        in_specs=[sp["q"], sp["kp"], sp["kc"], sp["vp"], sp["vc"], sp["own"], sp["own"], sp["own"], sp["tab"], sp["tab"], sp["tab"]],
        out_specs=sp["own"], out_shape=jax.ShapeDtypeStruct((L, R * W), BF16),
        compiler_params=_params("parallel", "parallel"), name=name)(x[0], x[1], x[1], x[2], x[2], do, o, lse, *tabs)


def _dil_dkv(name, x, do, o, lse, tabs, qcol, kcol, vcol, R, L):
    t = BAND_STEPS
    W = D_MODEL
    nq = L // t
    nxt = lambda kb: jnp.minimum(kb + 1, nq - 1)
    cur_q = pl.BlockSpec((t, W), lambda r, kb: (kb, qcol(r)))
    nxt_q = pl.BlockSpec((t, W), lambda r, kb: (nxt(kb), qcol(r)))
    cur_o = pl.BlockSpec((t, W), lambda r, kb: (kb, r))
    nxt_o = pl.BlockSpec((t, W), lambda r, kb: (nxt(kb), r))
    tab = pl.BlockSpec((t, LANES), lambda r, kb: (kb, r))

    def body(k_ref, v_ref, qc_ref, qn_ref, doc_ref, don_ref, oc_ref, on_ref, lc_ref, ln_ref, c_ref, sa_ref, sb_ref,
             dk_ref, dv_ref):
        lo = _lo_lanes()
        ri = lax.broadcasted_iota(jnp.int32, (t, t), 0)
        ci = lax.broadcasted_iota(jnp.int32, (t, t), 1)
        neg_c = jnp.where(ci <= ri, 0.0, NEG_INF)
        neg_n = jnp.where((ci >= ri) & (pl.program_id(1) + 1 < nq), 0.0, NEG_INF)
        blocks = []
        for q_ref, do_ref, o_ref, l_ref, neg in ((qc_ref, doc_ref, oc_ref, lc_ref, neg_c), (qn_ref, don_ref, on_ref, ln_ref, neg_n)):
            s, dp, lse, d = _dil_scores(lo, q_ref, do_ref, o_ref, l_ref, ((k_ref, v_ref),))
            pr = jnp.exp(s + neg[None] - lse)
            blocks.append((q_ref, do_ref, pr.astype(BF16), (pr * (dp - d)).astype(BF16)))
        for p in range(N_PAIRS):
            cols = _pair_cols(2 * p)
            dk2 = jnp.zeros((t, LANES), F32)
            dv2 = jnp.zeros((t, LANES), F32)
            for q_ref, do_ref, pr, ds in blocks:
                for h in (2 * p, 2 * p + 1):
                    hm = _head_lanes(lo, h)
                    dv2 = dv2 + _dot(pr[h], jnp.where(hm, do_ref[:, cols], 0), TN)
                    dk2 = dk2 + _dot(ds[h], jnp.where(hm, q_ref[:, cols], 0), TN)
            dk_ref[:, cols] = _rotate(dk2, c_ref[...], sa_ref[...], sb_ref[...], -1.0).astype(BF16)
            dv_ref[:, cols] = dv2.astype(BF16)

    kcur = pl.BlockSpec((t, W), lambda r, kb: (kb, kcol(r)))
    vcur = pl.BlockSpec((t, W), lambda r, kb: (kb, vcol(r)))
    return pl.pallas_call(
        body, grid=(R, nq),
        in_specs=[kcur, vcur, cur_q, nxt_q, cur_o, nxt_o, cur_o, nxt_o, cur_o, nxt_o, tab, tab, tab],
        out_specs=[cur_o, cur_o], out_shape=[jax.ShapeDtypeStruct((L, R * W), BF16)] * 2,
        compiler_params=_params("parallel", "parallel"), name=name)(x[1], x[2], x[0], x[0], do, do, o, o, lse, lse, *tabs)


def _fox_operands(q2, k2, kb2, lo, hh):
    lane = lax.broadcasted_iota(jnp.int32, (1, LANES), 1)
    if hh == 0:
        ones = ((lane >= HEAD_DIM) & (lane < HEAD_DIM + 3)).astype(BF16)
        return jnp.where(lo, q2, ones), jnp.where(lo, k2, kb2)
    ones = (lane < 3).astype(BF16)
    return jnp.where(lo, ones, q2), jnp.where(lo, kb2, k2)


def _causal_neg(t):
    ri = lax.broadcasted_iota(jnp.int32, (t, t), 0)
    ci = lax.broadcasted_iota(jnp.int32, (t, t), 1)
    return jnp.where(ci <= ri, 0.0, NEG_INF)


def _fox_fwd(name, qkv, kbias, t):
    S = qkv.shape[0]
    W = D_MODEL
    nq = S // t
    rep = t // LANES

    def body(q_ref, k_ref, v_ref, kb_ref, o_ref, lse_ref, m_scr, l_scr, acc_scr):
        qi, j = pl.program_id(0), pl.program_id(1)
        lo = _lo_lanes()

        @pl.when(j == 0)
        def _():
            m_scr[...] = jnp.full_like(m_scr, NEG_INF)
            l_scr[...] = jnp.zeros_like(l_scr)
            acc_scr[...] = jnp.zeros_like(acc_scr)

        def step(masked):
            neg = _causal_neg(t) if masked else None

            def pair(p, carry):
                cs = pl.ds(pl.multiple_of(p * LANES, LANES), LANES)
                q2, k2, v2, kb2 = q_ref[:, cs], k_ref[:, cs], v_ref[:, cs], kb_ref[:, cs]
                pvs, alphas = [], []
                for hh in range(2):
                    hm = lo if hh == 0 else jnp.logical_not(lo)
                    qh, kh = _fox_operands(q2, k2, kb2, lo, hh)
                    s = _dot(qh, kh, NT)
                    if masked:
                        s = s + neg
                    h = 2 * p + hh
                    m_prev = m_scr[h]
                    m_new = jnp.maximum(m_prev, jnp.max(s, axis=1, keepdims=True))
                    pe = jnp.exp(s - jnp.tile(m_new, (1, rep)))
                    alpha = jnp.exp(m_prev - m_new)
                    l_scr[h] = alpha * l_scr[h] + jnp.sum(pe, axis=1, keepdims=True)
                    m_scr[h] = m_new
                    pvs.append(_dot(pe.astype(BF16), jnp.where(hm, v2, 0), NN))
                    alphas.append(alpha)
                acc_scr[:, cs] = acc_scr[:, cs] * jnp.where(lo, alphas[0], alphas[1]) + pvs[0] + pvs[1]
                return carry

            lax.fori_loop(0, N_PAIRS, pair, 0)

        @pl.when(j < qi)
        def _():
            step(False)

        @pl.when(j == qi)
        def _():
            step(True)

        @pl.when(j == nq - 1)
        def _():
            for p in range(N_PAIRS):
                cols = slice(p * LANES, (p + 1) * LANES)
                l2 = jnp.where(lo, l_scr[2 * p], l_scr[2 * p + 1])
                m2 = jnp.where(lo, m_scr[2 * p], m_scr[2 * p + 1])
                o_ref[:, cols] = (acc_scr[:, cols] / l2).astype(BF16)
                lse_ref[:, cols] = m2 + jnp.log(l2)

    kv = lambda col: pl.BlockSpec((t, W), lambda qi, j: (jnp.minimum(j, qi), col))
    own = pl.BlockSpec((t, W), lambda qi, j: (qi, 0))
    return pl.pallas_call(
        body, grid=(nq, nq), in_specs=[own, kv(1), kv(2), kv(0)], out_specs=[own, own],
        out_shape=[jax.ShapeDtypeStruct((S, W), BF16), jax.ShapeDtypeStruct((S, W), F32)],
        scratch_shapes=[pltpu.VMEM((N_HEADS, t, LANES), F32), pltpu.VMEM((N_HEADS, t, LANES), F32), pltpu.VMEM((t, W), F32)],
        compiler_params=_params("parallel", "arbitrary"), name=name)(qkv, qkv, qkv, kbias)


def _fox_head_grads(qh, kh, v2, doh, neg, lse_h, d_h, rep):
    s = _dot(qh, kh, NT)
    if neg is not None:
        s = s + neg
    p = jnp.exp(s - jnp.tile(lse_h, (1, rep)))
    return p, p * (_dot(doh, v2, NT) - d_h)


def _fox_bwd(name, qkv, kbias, do, o, lse, t):
    S = qkv.shape[0]
    W = D_MODEL
    nq = S // t
    rep = t // LANES

    def body(q_ref, k_ref, v_ref, kb_ref, do_ref, o_ref, lse_ref, dq_ref, dk_ref, dv_ref, rs_ref, dc_ref, dq_scr, dk_scr, dv_scr):
        kb, j = pl.program_id(0), pl.program_id(1)
        lo = _lo_lanes()
        lane = lax.broadcasted_iota(jnp.int32, (1, LANES), 1)
        rows = pl.ds(pl.multiple_of(j * t, t), t)

        @pl.when((kb == 0) & (j == 0))
        def _():
            dq_scr[...] = jnp.zeros_like(dq_scr)
            rs_ref[...] = jnp.zeros_like(rs_ref)

        @pl.when(j == 0)
        def _():
            dk_scr[...] = jnp.zeros_like(dk_scr)
            dv_scr[...] = jnp.zeros_like(dv_scr)
            dc_ref[...] = jnp.zeros_like(dc_ref)

        def step(masked):
            neg = _causal_neg(t) if masked else None

            def pair(p, carry):
                cs = pl.ds(pl.multiple_of(p * LANES, LANES), LANES)
                q2, k2, v2, kb2, do2 = q_ref[:, cs], k_ref[:, cs], v_ref[:, cs], kb_ref[:, cs], do_ref[:, cs]
                dd = do2.astype(F32) * o_ref[:, cs].astype(F32)
                lse_h = _rep_rows(lse_ref[:, cs], lo)
                dq2 = jnp.zeros((t, LANES), F32)
                dv2 = jnp.zeros((t, LANES), F32)
                dk2 = jnp.zeros((t, LANES), F32)
                for hh in range(2):
                    hm = lo if hh == 0 else jnp.logical_not(lo)
                    qh, kh = _fox_operands(q2, k2, kb2, lo, hh)
                    doh = jnp.where(hm, do2, 0)
                    d_h = jnp.sum(jnp.where(hm, dd, 0.0), axis=1, keepdims=True)
                    pr, ds = _fox_head_grads(qh, kh, v2, doh, neg, lse_h[hh], d_h, rep)
                    rs_ref[rows, :] += jnp.where(lane == 2 * p + hh, jnp.sum(ds, axis=1, keepdims=True), 0.0)
                    dc_ref[p, hh:hh + 1, :] += jnp.sum(ds, axis=0, keepdims=True)
                    dsb = ds.astype(BF16)
                    dv2 = dv2 + _dot(pr.astype(BF16), doh, TN)
                    dk2 = dk2 + _dot(dsb, jnp.where(hm, q2, 0), TN)
                    dq2 = dq2 + _dot(dsb, jnp.where(hm, k2, 0), NN)
                dv_scr[:, cs] += dv2
                dk_scr[:, cs] += dk2
                dq_scr[rows, cs] += dq2
                return carry

            lax.fori_loop(0, N_PAIRS, pair, 0)
            if masked:
                dq_ref[...] = (dq_scr[rows, :] * SOFTMAX_SCALE).astype(BF16)

        @pl.when(j > kb)
        def _():
            step(False)

        @pl.when(j == kb)
        def _():
            step(True)

        @pl.when(j == nq - 1)
        def _():
            dv_ref[...] = dv_scr[...].astype(BF16)
            dk_ref[...] = dk_scr[...].astype(BF16)

    qrow = pl.BlockSpec((t, W), lambda kb, j: (jnp.maximum(j, kb), 0))
    krow = lambda col: pl.BlockSpec((t, W), lambda kb, j: (kb, col))
    own = pl.BlockSpec((t, W), lambda kb, j: (kb, 0))
    wide = jax.ShapeDtypeStruct((S, W), BF16)
    return pl.pallas_call(
        body, grid=(nq, nq), in_specs=[qrow, krow(1), krow(2), krow(0), qrow, qrow, qrow],
        out_specs=[own, own, own, pl.BlockSpec((S, LANES), lambda kb, j: (0, 0)), pl.BlockSpec((N_PAIRS, 2, t), lambda kb, j: (0, 0, kb))],
        out_shape=[wide, wide, wide, jax.ShapeDtypeStruct((S, LANES), F32), jax.ShapeDtypeStruct((N_PAIRS, 2, S), F32)],
        scratch_shapes=[pltpu.VMEM((S, W), F32), pltpu.VMEM((t, W), F32), pltpu.VMEM((t, W), F32)],
        compiler_params=pltpu.CompilerParams(dimension_semantics=("arbitrary", "arbitrary"), vmem_limit_bytes=FOX_BWD_VMEM),
        name=name)(qkv, qkv, qkv, kbias, do, o, lse)


def _combine(name, os_, lses, tm=256):
    S, W = os_[0].shape
    G = len(os_)

    def body(*refs):
        o_refs, l_refs = refs[:G], refs[G:2 * G]
        o_ref, lse_ref = refs[2 * G:]
        ls = [r[...] for r in l_refs]
        m = functools.reduce(jnp.maximum, ls)
        ws = [jnp.exp(l - m) for l in ls]
        den = functools.reduce(jnp.add, ws)
        num = functools.reduce(jnp.add, [w * r[...] for w, r in zip(ws, o_refs)])
        o_ref[...] = (num / den).astype(BF16)
        lse_ref[...] = m + jnp.log(den)

    row = pl.BlockSpec((tm, W), lambda i: (i, 0))
    return pl.pallas_call(
        body, grid=(S // tm,), in_specs=[row] * (2 * G), out_specs=[row, row],
        out_shape=[jax.ShapeDtypeStruct((S, W), BF16), jax.ShapeDtypeStruct((S, W), F32)],
        compiler_params=_params("parallel"), name=name)(*os_, *lses)


def _tri_matmul(tri, x):
    hi, mid, lo = _split3(x)
    return _dot(tri, hi, NN) + _dot(tri, mid, NN) + _dot(tri, lo, NN)


def _split3(x):
    hi = x.astype(BF16)
    r1 = x - hi.astype(F32)
    mid = r1.astype(BF16)
    return hi, mid, (r1 - mid.astype(F32)).astype(BF16)


def _gate_fwd(name, z, bf, tb=512):
    S = z.shape[0]

    def body(z_ref, b_ref, kb_ref, carry):
        @pl.when(pl.program_id(0) == 0)
        def _():
            carry[...] = jnp.zeros_like(carry)

        lf = jax.nn.log_sigmoid(z_ref[...] + b_ref[...])
        ri = lax.broadcasted_iota(jnp.int32, (tb, tb), 0)
        ci = lax.broadcasted_iota(jnp.int32, (tb, tb), 1)
        tri = (ci <= ri).astype(BF16)
        c = _tri_matmul(tri, lf) + carry[...]
        carry[...] = c[tb - 1:tb, :]
        head = lax.broadcasted_iota(jnp.int32, (LANES, D_MODEL), 0)
        col = lax.broadcasted_iota(jnp.int32, (LANES, D_MODEL), 1)
        base = (head >> 1) * LANES + jnp.where((head & 1) == 0, HEAD_DIM, 0)
        kb = jnp.zeros((tb, D_MODEL), F32)
        for i, piece in enumerate(_split3(-c)):
            place = ((col == base + i) & (head < N_HEADS)).astype(BF16)
            kb = kb + _dot(piece, place, NN)
        kb_ref[...] = kb.astype(BF16)

    row = pl.BlockSpec((tb, LANES), lambda i: (i, 0))
    return pl.pallas_call(
        body, grid=(S // tb,), in_specs=[row, pl.BlockSpec((1, LANES), lambda i: (0, 0))],
        out_specs=pl.BlockSpec((tb, D_MODEL), lambda i: (i, 0)), out_shape=jax.ShapeDtypeStruct((S, D_MODEL), BF16),
        scratch_shapes=[pltpu.VMEM((1, LANES), F32)], compiler_params=_params("arbitrary"), name=name)(z, bf)


def _gate_bwd(name, dc, z, bf, tb=512):
    S = z.shape[0]
    nb = S // tb

    def body(dc_ref, z_ref, b_ref, dz_ref, db_ref, carry):
        @pl.when(pl.program_id(0) == 0)
        def _():
            carry[...] = jnp.zeros_like(carry)
            db_ref[...] = jnp.zeros_like(db_ref)

        ri = lax.broadcasted_iota(jnp.int32, (tb, tb), 0)
        ci = lax.broadcasted_iota(jnp.int32, (tb, tb), 1)
        tri = (ci >= ri).astype(BF16)
        dlf = _tri_matmul(tri, dc_ref[...]) + carry[...]
        carry[...] = dlf[0:1, :]
        dz = dlf * jax.nn.sigmoid(-(z_ref[...] + b_ref[...]))
        dz_ref[...] = dz
        db_ref[...] += jnp.sum(dz, axis=0, keepdims=True)

    row = pl.BlockSpec((tb, LANES), lambda i: (nb - 1 - i, 0))
    vec = pl.BlockSpec((1, LANES), lambda i: (0, 0))
    return pl.pallas_call(
        body, grid=(nb,), in_specs=[row, row, vec], out_specs=[row, vec],
        out_shape=[jax.ShapeDtypeStruct((S, LANES), F32), jax.ShapeDtypeStruct((1, LANES), F32)],
        scratch_shapes=[pltpu.VMEM((1, LANES), F32)], compiler_params=_params("arbitrary"), name=name)(dc, z, bf)


def _ffn_gu(name, n, wgu, comm=None, tm=1024):
    S, D = n.shape
    nb = N_DEV // 2

    def body(n_ref, wg_ref, wu_ref, gu_ref, act_ref):
        x = n_ref[...]
        g = _dot(x, wg_ref[...], NN)
        u = _dot(x, wu_ref[...], NN)
        gu_ref[0] = g.astype(BF16)
        gu_ref[1] = u.astype(BF16)
        act_ref[...] = (g * jax.nn.sigmoid(g) * u).astype(BF16)

    return _call(
        name, body, (nb, S // tm),
        [pl.BlockSpec((tm, D), lambda j, i: (i, 0)), pl.BlockSpec((None, D, FF_BLK), lambda j, i: (j, 0, 0)),
         pl.BlockSpec((None, D, FF_BLK), lambda j, i: (j + nb, 0, 0))],
        [pl.BlockSpec((2, None, tm, FF_BLK), lambda j, i: (0, j, i, 0)), pl.BlockSpec((None, tm, FF_BLK), lambda j, i: (j, i, 0))],
        [jax.ShapeDtypeStruct((2, nb, S, FF_BLK), BF16), jax.ShapeDtypeStruct((nb, S, FF_BLK), BF16)], [],
        (n, wgu, wgu), ("parallel", "parallel"), comm)


def _ffn_down(name, act, wd, resid, comm=None, tm=1024):
    nb, S, _ = act.shape
    D = wd.shape[1]

    def epilogue(acc, ex, outs, j):
        outs[0][...] = acc + ex[0][...]

    o_spec = pl.BlockSpec((tm, D), lambda i, j, k: (i, 0))
    return _mm_call(name, (S // tm, 1, nb), act, pl.BlockSpec((None, tm, FF_BLK), lambda i, j, k: (k, i, 0)),
                    wd, pl.BlockSpec((FF_BLK, D), lambda i, j, k: (k, 0)), NN,
                    [jax.ShapeDtypeStruct((S, D), F32)], [o_spec], (tm, D), epilogue, (resid,), (o_spec,), comm=comm)


def _ffn_dact(name, dh, wd, gu, comm=None, tm=512):
    S, D = dh.shape
    nb = N_DEV // 2

    def epilogue(acc, ex, outs, j):
        g = ex[0][0].astype(F32)
        u = ex[0][1].astype(F32)
        sig = jax.nn.sigmoid(g)
        outs[0][0] = (acc * u * (sig * (1.0 + g * (1.0 - sig)))).astype(BF16)
        outs[0][1] = (acc * (g * sig)).astype(BF16)

    gu_spec = pl.BlockSpec((2, None, tm, FF_BLK), lambda j, i, k: (0, j, i, 0))
    return _mm_call(name, (nb, S // tm, 1), dh, pl.BlockSpec((tm, D), lambda j, i, k: (i, 0)),
                    wd, pl.BlockSpec((FF_BLK, D), lambda j, i, k: (j, 0)), NT,
                    [jax.ShapeDtypeStruct((2, nb, S, FF_BLK), BF16)], [gu_spec], (tm, FF_BLK), epilogue, (gu,), (gu_spec,),
                    col_axis=0, comm=comm)


def _ffn_dwgu(name, n, dgu, comm=None, tm=1024, tk=1024):
    S, D = n.shape
    dgu8 = dgu.reshape(N_DEV, S, FF_BLK)
    return _mm_call(name, (N_DEV, D // tm, S // tk), n, pl.BlockSpec((tk, tm), lambda d, i, k: (k, i)),
                    dgu8, pl.BlockSpec((None, tk, FF_BLK), lambda d, i, k: (d, k, 0)), TN,
                    [jax.ShapeDtypeStruct((N_DEV, D, FF_BLK), BF16)],
                    [pl.BlockSpec((None, tm, FF_BLK), lambda d, i, k: (d, i, 0))], (tm, FF_BLK), comm=comm)


def _ffn_dwd(name, act, dh, tk=1024):
    nb, S, _ = act.shape
    D = dh.shape[1]
    out = _mm_call(name, (nb, 1, S // tk), act, pl.BlockSpec((None, tk, FF_BLK), lambda b, j, k: (b, k, 0)),
                   dh, pl.BlockSpec((tk, D), lambda b, j, k: (k, 0)), TN,
                   [jax.ShapeDtypeStruct((nb, FF_BLK, D), BF16)],
                   [pl.BlockSpec((None, FF_BLK, D), lambda b, j, k: (b, 0, 0))], (FF_BLK, D))[0]
    return out.reshape(N_DEV, FF_BLK // 2, D)


def _ffn_dn(name, dgu, wgu, comm=None, tm=1024):
    S = dgu.shape[2]
    D = wgu.shape[1]
    dgu8 = dgu.reshape(N_DEV, S, FF_BLK)
    return _mm_call(name, (S // tm, 1, N_DEV), dgu8, pl.BlockSpec((None, tm, FF_BLK), lambda i, j, k: (k, i, 0)),
                    wgu, pl.BlockSpec((None, D, FF_BLK), lambda i, j, k: (k, 0, 0)), NT,
                    [jax.ShapeDtypeStruct((S, D), F32)], [pl.BlockSpec((tm, D), lambda i, j, k: (i, 0))], (tm, D), comm=comm)


def _adamw(name, parts, w, m, v, tr):
    rows, cols = w.shape
    n_parts = len(parts)
    c1 = 1.0 - ADAM_B1 ** ADAM_STEP
    c2 = 1.0 - ADAM_B2 ** ADAM_STEP

    def body(*refs):
        p_refs = refs[:n_parts]
        w_ref, m_ref, v_ref, g_ref, d_ref, nm_ref, nv_ref = refs[n_parts:]
        g = p_refs[0][...].astype(F32)
        for r in p_refs[1:]:
            g = g + r[...].astype(F32)
        mm = ADAM_B1 * m_ref[...] + (1.0 - ADAM_B1) * g
        vv = ADAM_B2 * v_ref[...] + (1.0 - ADAM_B2) * (g * g)
        g_ref[...] = g
        nm_ref[...] = mm
        nv_ref[...] = vv
        d_ref[...] = -ADAM_LR * ((mm / c1) / (jnp.sqrt(vv / c2) + ADAM_EPS) + ADAM_WD * w_ref[...])

    blk = pl.BlockSpec((tr, cols), lambda i: (i, 0))
    out = jax.ShapeDtypeStruct((rows, cols), F32)
    return pl.pallas_call(
        body, grid=(rows // tr,), in_specs=[blk] * (n_parts + 3), out_specs=[blk] * 4, out_shape=[out] * 4,
        compiler_params=_params("parallel"), name=name)(*parts, w, m, v)


def _position():
    return lax.axis_index("x"), lax.axis_index("y"), lax.axis_index("c")


def _other_chips():
    x, y, _ = _position()
    return [(1 - x, y), (x, 1 - y), (1 - x, 1 - y)]


def _remote(src, dst, send, recv, k, to):
    return pltpu.make_async_remote_copy(src_ref=src, dst_ref=dst, send_sem=send.at[k], recv_sem=recv.at[k],
                                        device_id=to, device_id_type=MESH)


def _ag_send(blocks, direct=False):
    n_peer = 7 if direct else 4

    def copies(ins, outs, send, recv, local, r0=0, l0=0):
        x, y, c = _position()
        me = 4 * x + 2 * y + c
        peers = [(x, y, 1 - c)] + [(px, py, c) for px, py in _other_chips()]
        if direct:
            peers += [(px, py, 1 - c) for px, py in _other_chips()]
        cps = []
        for t, (src, dst) in enumerate(zip(ins, outs)):
            cps.append(pltpu.make_async_copy(src, dst.at[me], local.at[l0 + t]))
            cps += [_remote(src, dst.at[me], send, recv, r0 + n_peer * t + k, to) for k, to in enumerate(peers)]
        return cps

    outs = tuple(jax.ShapeDtypeStruct((N_DEV,) + b.shape, b.dtype) for b in blocks)
    return _Comm(tuple(blocks), outs, {}, copies, n_peer * len(blocks), len(blocks))


def _ag_forward(bufs):
    def copies(ins, outs, send, recv, local, r0=0, l0=0):
        x, y, c = _position()
        cps = []
        for t, buf in enumerate(outs):
            for k, (px, py) in enumerate(_other_chips()):
                slot = buf.at[4 * px + 2 * py + c]
                cps.append(_remote(slot, slot, send, recv, r0 + 3 * t + k, (x, y, 1 - c)))
        return cps

    outs = tuple(jax.ShapeDtypeStruct(b.shape, b.dtype) for b in bufs)
    return _Comm(tuple(bufs), outs, {t: t for t in range(len(bufs))}, copies, 3 * len(bufs), 0)


def _rs_swap(shares):
    def copies(ins, outs, send, recv, local, r0=0, l0=0):
        x, y, c = _position()
        return [_remote(src.at[:, 1 - c], dst, send, recv, r0 + t, (x, y, 1 - c)) for t, (src, dst) in enumerate(zip(ins, outs))]

    ins = tuple(s.reshape((4, 2) + s.shape[1:]) for s in shares)
    outs = tuple(jax.ShapeDtypeStruct((4,) + s.shape[1:], s.dtype) for s in shares)
    return _Comm(ins, outs, {}, copies, len(shares), 0)


def _rs_exchange(sums):
    def copies(ins, outs, send, recv, local, r0=0, l0=0):
        _, _, c = _position()
        return [_remote(src.at[2 * px + py], dst.at[k], send, recv, r0 + 3 * t + k, (px, py, c))
                for t, (src, dst) in enumerate(zip(ins, outs)) for k, (px, py) in enumerate(_other_chips())]

    outs = tuple(jax.ShapeDtypeStruct((3,) + s.shape[1:], s.dtype) for s in sums)
    return _Comm(tuple(sums), outs, {}, copies, 3 * len(sums), 0)


def _join(a, b):
    def copies(ins, outs, send, recv, local, r0=0, l0=0):
        return (a.copies(ins[:len(a.ins)], outs[:len(a.outs)], send, recv, local, r0, l0)
                + b.copies(ins[len(a.ins):], outs[len(a.outs):], send, recv, local, r0 + a.n_remote, l0 + a.n_local))

    aliases = {**a.aliases, **{len(a.ins) + i: len(a.outs) + o for i, o in b.aliases.items()}}
    return _Comm(a.ins + b.ins, a.outs + b.outs, aliases, copies, a.n_remote + b.n_remote, a.n_local + b.n_local)


def _comm_call(name, comm):
    return _call(name, lambda: None, (), [], [], [], [], (), (), comm)


def _pair_sum(name, share, got, core, tr):
    _, rows, cols = share.shape

    def body(c_ref, a_ref, b_ref, o_ref):
        o_ref[...] = (a_ref[...].astype(F32) + b_ref[...].astype(F32)).astype(o_ref.dtype)

    grid_spec = pltpu.PrefetchScalarGridSpec(
        num_scalar_prefetch=1, grid=(4, rows // tr),
        in_specs=[pl.BlockSpec((None, None, tr, cols), lambda q, i, c: (q, c[0], i, 0)),
                  pl.BlockSpec((None, tr, cols), lambda q, i, c: (q, i, 0))],
        out_specs=pl.BlockSpec((None, tr, cols), lambda q, i, c: (q, i, 0)))
    return pl.pallas_call(
        body, grid_spec=grid_spec, out_shape=jax.ShapeDtypeStruct((4, rows, cols), share.dtype),
        compiler_params=_params("parallel", "parallel"), name=name)(core, share.reshape(4, 2, rows, cols), got)


TENSORS = ("a_w_in", "a_w_out", "b_w_in", "b_w_out", "gu0", "gu1", "dn0", "dn1")
ROW_TILE = {"a_w_in": 256, "a_w_out": 128, "b_w_in": 256, "b_w_out": 128, "gu0": 256, "gu1": 256, "dn0": 176, "dn1": 176}
A_BLK = 9 * D_MODEL // N_DEV
B_BLK = 386
B_IN = 3 * D_MODEL + N_HEADS
B_IN_PAD = 3 * D_MODEL + LANES


def kernel(x, a_norm, a_w_in, a_w_out, b_norm, b_w_in, b_f, b_w_out, ffn_norm, ffn_w_gu, ffn_w_down, final_norm, loss_target, m_a_norm, m_a_w_in, m_a_w_out, m_b_norm, m_b_w_in, m_b_f, m_b_w_out, m_ffn_norm, m_ffn_w_gu, m_ffn_w_down, m_final_norm, v_a_norm, v_a_w_in, v_a_w_out, v_b_norm, v_b_w_in, v_b_f, v_b_w_out, v_ffn_norm, v_ffn_w_gu, v_ffn_w_down, v_final_norm):
    S = x.shape[1]
    xi, yi, ci = _position()
    dev = 4 * xi + 2 * yi + ci
    core = ci.reshape(1).astype(jnp.int32)
    h0, target = x.reshape(S, D_MODEL), loss_target.reshape(S, D_MODEL)

    def shards(a_in, a_out, b_in, b_out, gu, dn):
        return {"a_w_in": a_in[0], "a_w_out": a_out[0], "b_w_in": b_in[0], "b_w_out": b_out[0],
                "gu0": gu[0], "gu1": gu[1], "dn0": dn[0], "dn1": dn[1]}

    w_sh = shards(a_w_in, a_w_out, b_w_in, b_w_out, ffn_w_gu, ffn_w_down)
    m_sh = shards(m_a_w_in, m_a_w_out, m_b_w_in, m_b_w_out, m_ffn_w_gu, m_ffn_w_down)
    v_sh = shards(v_a_w_in, v_a_w_out, v_b_w_in, v_b_w_out, v_ffn_w_gu, v_ffn_w_down)
    wb = {n: w_sh[n].astype(BF16) for n in TENSORS}
    bf_pad = jnp.pad(b_f, ((0, 0), (0, LANES - N_HEADS)))
    tabs = _rope_tables(S)

    g_ain, g_aout = _comm_call("gather_a", _ag_send([wb["a_w_in"], wb["a_w_out"]]))
    g_ain, g_aout = _comm_call("forward_a", _ag_forward([g_ain, g_aout]))
    n0 = _rms_fwd("rms_a", h0, a_norm[0])
    later = [wb["b_w_in"], wb["b_w_out"], wb["gu0"], wb["dn0"], jnp.pad(b_norm, ((0, 7), (0, 0)))]
    w_a_in = g_ain.transpose(1, 0, 2).reshape(D_MODEL, 9 * D_MODEL)
    qkv_a, later = _a_proj("proj_a", n0, w_a_in, tabs, _ag_send(later))
    cols = [lambda r: r] * 3
    groups = [(g, dil, S // dil, qkv_a[g]) for g, (window, dil) in enumerate(DILATED_PATTERNS)]
    to_view = lambda tag, a, dil: a if dil == 1 else _relayout("view%d_%s" % (dil, tag), a, dil, True)
    from_view = lambda tag, a, dil: a if dil == 1 else _relayout("unview%d_%s" % (dil, tag), a, dil, False)
    outs, lses = [], []
    for g, dil, L, view in groups:
        o_g, lse_g = _dil_fwd("dil_fwd%d" % g, view, *cols, dil, L)
        outs.append(from_view("o", o_g, dil))
        lses.append(from_view("lse", lse_g, dil))
    o_a, lse_a = _combine("dil_combine", outs, lses)
    w_a_out = g_aout.reshape(D_MODEL, D_MODEL)
    h1, (g_bin, g_bout, g_gu0, g_dn0, g_bnorm) = _matmul("out_a", o_a, w_a_out, "nn", F32, TM, 1024, 1024, resid=h0,
                                                         comm=_ag_forward(later))

    n1 = _rms_fwd("rms_f0", h1, ffn_norm[0])
    gu0, act0, g_gu1 = _ffn_gu("gu_f0", n1, g_gu0, _ag_send([wb["gu1"]]))
    w_dn0 = g_dn0.reshape(D_FF, D_MODEL)
    h2, g_dn1 = _ffn_down("down_f0", act0, w_dn0, h1, _ag_send([wb["dn1"]]))

    b_norm_full = g_bnorm[:, 0].reshape(D_MODEL)
    w_b_in = g_bin.transpose(1, 0, 2).reshape(D_MODEL, B_IN)
    w_b_gate = jnp.pad(w_b_in[:, 3 * D_MODEL:], ((0, 0), (0, LANES - N_HEADS)))
    w_b_cat = jnp.concatenate([w_b_in[:, :3 * D_MODEL], w_b_gate], axis=1)
    w_b_out = g_bout.reshape(D_MODEL, D_MODEL)
    n2 = _rms_fwd("rms_b", h2, b_norm_full)
    qkv, (g_gu1, g_dn1) = _matmul("proj_b", n2, w_b_in[:, :3 * D_MODEL], "nn", BF16, TM, 1024, 1024, col0_scale=SOFTMAX_SCALE,
                                  comm=_ag_forward([g_gu1, g_dn1]))
    z = _matmul("gate_b", n2, w_b_gate, "nn", F32, TM, LANES, 1024)
    kbias = _gate_fwd("gate_cumsum", z, bf_pad)
    tf = min(S, 512)
    o_b, lse_b = _fox_fwd("fox_fwd", qkv, kbias, tf)
    h3 = _matmul("out_b", o_b, w_b_out, "nn", F32, TM, 1024, 1024, resid=h2)

    w_dn1 = g_dn1.reshape(D_FF, D_MODEL)
    n3 = _rms_fwd("rms_f1", h3, ffn_norm[1])
    gu1, act1 = _ffn_gu("gu_f1", n3, g_gu1)
    h4 = _ffn_down("down_f1", act1, w_dn1, h3)[0]

    dh4, d_final, loss = _loss_head("loss_head", h4, final_norm, target)

    share, got, sums, others = {}, {}, {}, {}

    def pair_sums(*names):
        for n in names:
            sums[n] = _pair_sum("pair_" + n, share[n], got[n], core, ROW_TILE[n])

    dgu1 = _ffn_dact("dact_f1", dh4, w_dn1, gu1)[0]
    share["dn1"] = _ffn_dwd("dwd_f1", act1, dh4)
    share["gu1"] = _ffn_dwgu("dwgu_f1", n3, dgu1)[0]
    dn3, got["gu1"], got["dn1"] = _ffn_dn("dn_f1", dgu1, g_gu1, _rs_swap([share["gu1"], share["dn1"]]))
    dh3, d_ffn1 = _rms_bwd("rmsb_f1", dn3, h3, ffn_norm[1], dh4)
    pair_sums("gu1", "dn1")

    do_b = _matmul("dout_b", dh3, w_b_out, "nt", BF16, TM, 1024, 1024)
    share["b_w_out"] = _matmul("dwout_b", o_b, dh3, "tn", BF16, TM, 1024, 1024).reshape(N_DEV, 128, D_MODEL)
    dq_b, dk_b, dv_b, ds_rowsum, ds_colsum = _fox_bwd("fox_bwd", qkv, kbias, do_b, o_b, lse_b, tf)
    dc = ds_rowsum[:, :N_HEADS] - ds_colsum.reshape(N_HEADS, S).T
    dz, d_bf = _gate_bwd("gate_bwd", jnp.pad(dc, ((0, 0), (0, LANES - N_HEADS))), z, bf_pad)
    dproj_b = jnp.concatenate([dq_b, dk_b, dv_b, dz.astype(BF16)], axis=1)
    dw_b_in, (others["gu1"],) = _matmul("dwin_b", n2, dproj_b, "tn", BF16, TM, B_IN_PAD // 5, 1024, comm=_rs_exchange([sums["gu1"]]))
    dn2, (others["dn1"],) = _matmul("dn_b", dproj_b, w_b_cat, "nt", F32, TM, 1024, B_IN_PAD // 5, comm=_rs_exchange([sums["dn1"]]))
    dh2, d_bnorm = _rms_bwd("rmsb_b", dn2, h2, b_norm_full, dh3)
    share["b_w_in"] = dw_b_in[:, :B_IN].reshape(D_MODEL, N_DEV, B_BLK).transpose(1, 0, 2)

    dgu0, got["b_w_in"], got["b_w_out"] = _ffn_dact("dact_f0", dh2, w_dn0, gu0, _rs_swap([share["b_w_in"], share["b_w_out"]]))
    share["dn0"] = _ffn_dwd("dwd_f0", act0, dh2)
    pair_sums("b_w_in", "b_w_out")
    share["gu0"], others["b_w_in"], others["b_w_out"] = _ffn_dwgu(
        "dwgu_f0", n1, dgu0, _rs_exchange([sums["b_w_in"], sums["b_w_out"]]))
    dn1, got["gu0"], got["dn0"] = _ffn_dn("dn_f0", dgu0, g_gu0, _rs_swap([share["gu0"], share["dn0"]]))
    dh1, d_ffn0 = _rms_bwd("rmsb_f0", dn1, h1, ffn_norm[0], dh2)
    pair_sums("gu0", "dn0")

    do_a = _matmul("dout_a", dh1, w_a_out, "nt", BF16, TM, 1024, 1024)
    share["a_w_out"] = _matmul("dwout_a", o_a, dh1, "tn", BF16, TM, 1024, 1024).reshape(N_DEV, 128, D_MODEL)
    pieces = []
    for g, dil, L, view in groups:
        rot = tuple(tb.reshape(L, dil * LANES) for tb in tabs)
        args = (view, to_view("do", do_a, dil), to_view("o", o_a, dil), to_view("lse", lse_a, dil), rot, *cols, dil, L)
        dq_g = _dil_dq("dil_dq%d" % g, *args)
        dk_g, dv_g = _dil_dkv("dil_dkv%d" % g, *args)
        pieces += [from_view("d" + tag, a, dil) for tag, a in (("q", dq_g), ("k", dk_g), ("v", dv_g))]
    dproj_a = jnp.concatenate(pieces, axis=1)
    share["a_w_in"], others["gu0"], others["dn0"] = _mm_call(
        "dwin_a", (N_DEV, 1, S // 1024), n0, pl.BlockSpec((1024, D_MODEL), lambda d, i, k: (k, 0)),
        dproj_a, pl.BlockSpec((1024, A_BLK), lambda d, i, k: (k, d)), TN, [jax.ShapeDtypeStruct((N_DEV, D_MODEL, A_BLK), BF16)],
        [pl.BlockSpec((None, D_MODEL, A_BLK), lambda d, i, k: (d, 0, 0))], (D_MODEL, A_BLK),
        comm=_rs_exchange([sums["gu0"], sums["dn0"]]))
    dn0, got["a_w_in"], got["a_w_out"] = _mm_call(
        "dn_a", (S // TM, 1, N_DEV), dproj_a, pl.BlockSpec((TM, A_BLK), lambda i, j, k: (i, k)),
        g_ain, pl.BlockSpec((None, D_MODEL, A_BLK), lambda i, j, k: (k, 0, 0)), NT, [jax.ShapeDtypeStruct((S, D_MODEL), F32)],
        [pl.BlockSpec((TM, D_MODEL), lambda i, j, k: (i, 0))], (TM, D_MODEL), comm=_rs_swap([share["a_w_in"], share["a_w_out"]]))
    dx, d_anorm = _rms_bwd("rmsb_a", dn0, h0, a_norm[0], dh1)
    pair_sums("a_w_in", "a_w_out")

    misc = jnp.concatenate([d_bf[:, :N_HEADS], loss[:, :1], jnp.zeros((1, D_MODEL - N_HEADS - 1), F32)], axis=1)
    small = jnp.concatenate([d_anorm, d_ffn0, d_ffn1, d_final, d_bnorm, misc, jnp.zeros((2, D_MODEL), F32)], axis=0)
    others["a_w_in"], others["a_w_out"], small_all = _comm_call(
        "exchange_a", _join(_rs_exchange([sums["a_w_in"], sums["a_w_out"]]), _ag_send([small], direct=True)))

    outs = {}
    for n in TENSORS:
        mine = lax.dynamic_index_in_dim(sums[n], 2 * xi + yi, axis=0, keepdims=False)
        outs[n] = _adamw("adamw_" + n, [mine] + [others[n][k] for k in range(3)], w_sh[n], m_sh[n], v_sh[n], ROW_TILE[n])

    pad_vec = lambda a: jnp.pad(a, ((0, 0), (0, D_MODEL - a.shape[1])))

    def small_pack(an, fn, fin, bf):
        return jnp.concatenate([an, fn, fin.reshape(1, D_MODEL), jnp.zeros((1, D_MODEL), F32), pad_vec(bf),
                                jnp.zeros((2, D_MODEL), F32)], axis=0)

    sg, sd, sm, sv = _adamw("adamw_small", [small_all[d] for d in range(N_DEV)], small_pack(a_norm, ffn_norm, final_norm, b_f),
                            small_pack(m_a_norm, m_ffn_norm, m_final_norm, m_b_f),
                            small_pack(v_a_norm, v_ffn_norm, v_final_norm, v_b_f), 8)
    g_bn = lax.dynamic_slice(sg[4:5], (0, dev * LANES), (1, LANES))
    bn = _adamw("adamw_b_norm", [g_bn], b_norm, m_b_norm, v_b_norm, 1)

    def tree(i):
        full = lambda name, ref: outs[name][i].reshape(ref.shape)
        sml = (sg, sd, sm, sv)[i]
        return dict(
            a_norm=sml[0:1], a_w_in=full("a_w_in", a_w_in), a_w_out=full("a_w_out", a_w_out), b_norm=bn[i],
            b_w_in=full("b_w_in", b_w_in), b_f=sml[5:6, :N_HEADS], b_w_out=full("b_w_out", b_w_out), ffn_norm=sml[1:3],
            ffn_w_gu=jnp.stack([outs["gu0"][i], outs["gu1"][i]]).reshape(ffn_w_gu.shape),
            ffn_w_down=jnp.stack([outs["dn0"][i], outs["dn1"][i]]).reshape(ffn_w_down.shape), final_norm=sml[3])

    order = ("a_norm", "a_w_in", "a_w_out", "b_norm", "b_w_in", "b_f", "b_w_out", "ffn_norm", "ffn_w_gu", "ffn_w_down", "final_norm")
    result = [sg[5, N_HEADS], dx.reshape(x.shape)]
    for i in range(4):
        t = tree(i)
        result += [t[n] for n in order]
    return tuple(result)
```

```python
import functools
from typing import Callable, NamedTuple

import jax
import jax.numpy as jnp
from jax import lax
from jax.experimental import pallas as pl
from jax.experimental.pallas import tpu as pltpu

F32 = jnp.float32
BF16 = jnp.bfloat16

D_MODEL = 1024
N_HEADS = 16
HEAD_DIM = 64
N_PAIRS = N_HEADS // 2
LANES = 128
DILATED_PATTERNS = ((128, 1), (512, 4), (2048, 16))
BAND_STEPS = 128
ROT_DIM = HEAD_DIM // 4
ROPE_THETA = 500000.0
D_FF = 2816
RMS_EPS = 1e-6
NEG_INF = -1e30
SOFTMAX_SCALE = HEAD_DIM ** -0.5
N_DEV = 8
FF_BLK = 2 * D_FF // N_DEV
ADAM_LR, ADAM_B1, ADAM_B2, ADAM_EPS, ADAM_WD, ADAM_STEP = 0.001, 0.9, 0.999, 1e-08, 0.01, 10
VMEM_LIMIT = 52 * 1024 * 1024
FOX_BWD_VMEM = 60 * 1024 * 1024
TM = 1024
MESH = pl.DeviceIdType.MESH

NN = (((1,), (0,)), ((), ()))
NT = (((1,), (1,)), ((), ()))
TN = (((0,), (0,)), ((), ()))


def _params(*sem):
    return pltpu.CompilerParams(dimension_semantics=sem, vmem_limit_bytes=VMEM_LIMIT)


def _dot(a, b, dims):
    return lax.dot_general(a, b, dims, preferred_element_type=F32)


class _Comm(NamedTuple):
    ins: tuple
    outs: tuple
    aliases: dict
    copies: Callable
    n_remote: int
    n_local: int


def _call(name, body, grid, in_specs, out_specs, out_shape, scratch, args, sem, comm=None):
    if comm is None:
        return pl.pallas_call(body, grid=grid, in_specs=in_specs, out_specs=out_specs, out_shape=out_shape,
                              scratch_shapes=scratch, compiler_params=_params(*sem), name=name)(*args)
    n_in, n_out = len(in_specs), len(out_specs)
    n_ci, n_co = len(comm.ins), len(comm.outs)
    o0 = n_in + n_ci

    def hosted(*refs):
        c_ins, c_outs = refs[n_in:o0], refs[o0 + n_out:o0 + n_out + n_co]
        sems = refs[-3:]

        def start():
            for cp in comm.copies(c_ins, c_outs, *sems):
                cp.start()

        def wait():
            for cp in comm.copies(c_ins, c_outs, *sems):
                cp.wait()

        if not grid:
            start()
            body()
            wait()
            return
        ids = [pl.program_id(ax) for ax in range(len(grid))]
        pl.when(functools.reduce(jnp.logical_and, [i == 0 for i in ids]))(start)
        body(*refs[:n_in], *refs[o0:o0 + n_out], *refs[o0 + n_out + n_co:-3])
        pl.when(functools.reduce(jnp.logical_and, [i == g - 1 for i, g in zip(ids, grid)]))(wait)

    hbm = pl.BlockSpec(memory_space=pltpu.HBM)
    dma = pltpu.SemaphoreType.DMA
    return pl.pallas_call(
        hosted, grid=grid, in_specs=[*in_specs, *[hbm] * n_ci], out_specs=[*out_specs, *[hbm] * n_co],
        out_shape=[*out_shape, *comm.outs], input_output_aliases={n_in + i: n_out + o for i, o in comm.aliases.items()},
        scratch_shapes=[*scratch, dma((comm.n_remote,)), dma((comm.n_remote,)), dma((max(comm.n_local, 1),))],
        compiler_params=_params(*["arbitrary"] * len(grid)), name=name)(*args, *comm.ins)


def _mm_call(name, grid, a, a_spec, b, b_spec, dims, out_shapes, out_specs, acc_shape, epilogue=None,
             extras=(), extra_specs=(), col_axis=1, comm=None):
    nk = grid[2]
    n_extra = len(extras)
    n_out = len(out_shapes)

    def finish(res, ex, outs, j):
        if epilogue is None:
            outs[0][...] = res.astype(outs[0].dtype)
        else:
            epilogue(res, ex, outs, j)

    def body(*refs):
        a_ref, b_ref = refs[0], refs[1]
        ex = refs[2:2 + n_extra]
        outs = refs[2 + n_extra:2 + n_extra + n_out]
        j, k = pl.program_id(col_axis), pl.program_id(2)
        part = _dot(a_ref[...].astype(BF16), b_ref[...].astype(BF16), dims)
        if nk == 1:
            finish(part, ex, outs, j)
            return
        acc = refs[-1]

        @pl.when(k == 0)
        def _():
            acc[...] = part

        @pl.when((k > 0) & (k < nk - 1))
        def _():
            acc[...] += part

        @pl.when(k == nk - 1)
        def _():
            finish(acc[...] + part, ex, outs, j)

    return _call(name, body, grid, [a_spec, b_spec, *extra_specs], out_specs, out_shapes,
                 [] if nk == 1 else [pltpu.VMEM(acc_shape, F32)], (a, b, *extras), ("parallel", "parallel", "arbitrary"), comm)


def _matmul(name, a, b, mode, out_dtype, tm, tn, tk, resid=None, col0_scale=None, comm=None):
    if mode == "nn":
        (M, K), N = a.shape, b.shape[1]
        a_spec = pl.BlockSpec((tm, tk), lambda j, i, k: (i, k))
        b_spec = pl.BlockSpec((tk, tn), lambda j, i, k: (k, j))
        dims = NN
    elif mode == "nt":
        (M, K), N = a.shape, b.shape[0]
        a_spec = pl.BlockSpec((tm, tk), lambda j, i, k: (i, k))
        b_spec = pl.BlockSpec((tn, tk), lambda j, i, k: (j, k))
        dims = NT
    else:
        (K, M), N = a.shape, b.shape[1]
        a_spec = pl.BlockSpec((tk, tm), lambda j, i, k: (k, i))
        b_spec = pl.BlockSpec((tk, tn), lambda j, i, k: (k, j))
        dims = TN
    assert M % tm == 0 and N % tn == 0 and K % tk == 0, (name, M, N, K, tm, tn, tk)
    o_spec = pl.BlockSpec((tm, tn), lambda j, i, k: (i, j))
    extras, extra_specs, epilogue = (), (), None
    if resid is not None:
        extras, extra_specs = (resid,), (o_spec,)

        def epilogue(acc, ex, outs, j):
            outs[0][...] = (acc + ex[0][...]).astype(outs[0].dtype)

    elif col0_scale is not None:

        def epilogue(acc, ex, outs, j):
            outs[0][...] = (acc * jnp.where(j == 0, col0_scale, 1.0)).astype(outs[0].dtype)

    res = _mm_call(name, (N // tn, M // tm, K // tk), a, a_spec, b, b_spec, dims, [jax.ShapeDtypeStruct((M, N), out_dtype)],
                   [o_spec], (tm, tn), epilogue, extras, extra_specs, col_axis=0, comm=comm)
    return res[0] if comm is None else (res[0], res[1:])


def _rms_fwd(name, h, gain, tm=512):
    S, D = h.shape

    def body(h_ref, g_ref, n_ref):
        x = h_ref[...]
        rstd = lax.rsqrt(jnp.mean(x * x, axis=-1, keepdims=True) + RMS_EPS)
        n_ref[...] = (x * rstd * g_ref[...]).astype(BF16)

    return pl.pallas_call(
        body, grid=(S // tm,), in_specs=[pl.BlockSpec((tm, D), lambda i: (i, 0)), pl.BlockSpec((1, D), lambda i: (0, 0))],
        out_specs=pl.BlockSpec((tm, D), lambda i: (i, 0)), out_shape=jax.ShapeDtypeStruct((S, D), BF16),
        compiler_params=_params("parallel"), name=name)(h, gain.reshape(1, D))


def _rms_bwd(name, dn, h, gain, dres, tm=512):
    S, D = h.shape

    def body(dn_ref, h_ref, g_ref, r_ref, dh_ref, dg_ref):
        x = h_ref[...]
        rstd = lax.rsqrt(jnp.mean(x * x, axis=-1, keepdims=True) + RMS_EPS)
        xhat = x * rstd
        d = dn_ref[...]
        dxhat = d * g_ref[...]
        dh_ref[...] = rstd * (dxhat - xhat * jnp.mean(dxhat * xhat, axis=-1, keepdims=True)) + r_ref[...]

        @pl.when(pl.program_id(0) == 0)
        def _():
            dg_ref[...] = jnp.zeros_like(dg_ref)

        dg_ref[...] += jnp.sum(d * xhat, axis=0, keepdims=True)

    row = pl.BlockSpec((tm, D), lambda i: (i, 0))
    vec = pl.BlockSpec((1, D), lambda i: (0, 0))
    return pl.pallas_call(
        body, grid=(S // tm,), in_specs=[row, row, vec, row], out_specs=[row, vec],
        out_shape=[jax.ShapeDtypeStruct((S, D), F32), jax.ShapeDtypeStruct((1, D), F32)],
        compiler_params=_params("arbitrary"), name=name)(dn, h, gain.reshape(1, D), dres)


def _loss_head(name, h, gain, target, tm=512):
    S, D = h.shape

    def body(h_ref, g_ref, t_ref, dh_ref, dg_ref, loss_ref):
        x = h_ref[...]
        rstd = lax.rsqrt(jnp.mean(x * x, axis=-1, keepdims=True) + RMS_EPS)
        xhat = x * rstd
        err = xhat * g_ref[...] - t_ref[...]
        dy = err * (1.0 / D)
        dxhat = dy * g_ref[...]
        dh_ref[...] = rstd * (dxhat - xhat * jnp.mean(dxhat * xhat, axis=-1, keepdims=True))

        @pl.when(pl.program_id(0) == 0)
        def _():
            dg_ref[...] = jnp.zeros_like(dg_ref)
            loss_ref[...] = jnp.zeros_like(loss_ref)

        dg_ref[...] += jnp.sum(dy * xhat, axis=0, keepdims=True)
        part = 0.5 * jnp.sum(jnp.mean(err * err, axis=-1, keepdims=True), axis=0, keepdims=True)
        loss_ref[...] += jnp.broadcast_to(part, loss_ref.shape)

    row = pl.BlockSpec((tm, D), lambda i: (i, 0))
    vec = pl.BlockSpec((1, D), lambda i: (0, 0))
    return pl.pallas_call(
        body, grid=(S // tm,), in_specs=[row, vec, row], out_specs=[row, vec, pl.BlockSpec((1, LANES), lambda i: (0, 0))],
        out_shape=[jax.ShapeDtypeStruct((S, D), F32), jax.ShapeDtypeStruct((1, D), F32),
                   jax.ShapeDtypeStruct((1, LANES), F32)],
        compiler_params=_params("arbitrary"), name=name)(h, gain.reshape(1, D), target)


def _rope_tables(S):
    half = ROT_DIM // 2
    inv_freq = ROPE_THETA ** (-jnp.arange(half, dtype=F32) * 2.0 / ROT_DIM)
    ang = jnp.arange(S, dtype=F32)[:, None] * inv_freq[None, :]
    cos, sin = jnp.cos(ang), jnp.sin(ang)
    one = jnp.ones((S, HEAD_DIM - ROT_DIM), F32)
    zero = jnp.zeros((S, HEAD_DIM - ROT_DIM), F32)
    zh = jnp.zeros((S, half), F32)
    c = jnp.concatenate([cos, cos, one], axis=1)
    sa = jnp.concatenate([-sin, zh, zero], axis=1)
    sb = jnp.concatenate([zh, sin, zero], axis=1)
    return tuple(jnp.concatenate([t, t], axis=1) for t in (c, sa, sb))


def _rotate(x, c, sa, sb, sign):
    return x * c + sign * (pltpu.roll(x, LANES - ROT_DIM // 2, 1) * sa + pltpu.roll(x, ROT_DIM // 2, 1) * sb)


def _stage_chunks(scr, chunks):
    for c, x in enumerate(chunks):
        scr[c] = x


def _strided_rows(scr, c, r, n, R):
    return scr.at[c][pl.ds(r, n, stride=R), :]


def _a_proj(name, n, w, tabs, comm, tm=512):
    S, D = n.shape
    n_i = S // tm
    dils = [dil for _, dil in DILATED_PATTERNS]
    n_out = 3 * len(dils)

    def body(n_ref, w_ref, c_ref, sa_ref, sb_ref, *rest):
        outs, scr = rest[:n_out], rest[n_out]
        j = pl.program_id(0)
        acc = _dot(n_ref[...], w_ref[...], NN)
        c, sa, sb = c_ref[...], sa_ref[...], sb_ref[...]
        for J in range(n_out):
            R, kind = dils[J // 3], J % 3

            @pl.when(j == J)
            def _(J=J, R=R, kind=kind):
                chunks = [acc[:, b * LANES:(b + 1) * LANES] for b in range(N_PAIRS)]
                if kind < 2:
                    chunks = [_rotate(x, c, sa, sb, 1.0) * (SOFTMAX_SCALE if kind == 0 else 1.0) for x in chunks]
                if R == 1:
                    for b, x in enumerate(chunks):
                        outs[J][:, b * LANES:(b + 1) * LANES] = x.astype(BF16)
                    return
                _stage_chunks(scr, chunks)
                for r in range(R):
                    for b in range(N_PAIRS):
                        col = r * D_MODEL + b * LANES
                        outs[J][:, col:col + LANES] = _strided_rows(scr, b, r, tm // R, R).astype(BF16)

    def out_spec(J, R):
        return pl.BlockSpec((tm // R, R * D_MODEL), lambda j, i: (jnp.where(j == J, i, jnp.where(j < J, 0, n_i - 1)), 0))

    tab = pl.BlockSpec((tm, LANES), lambda j, i: (i, 0))
    res = _call(name, body, (n_out, n_i),
                [pl.BlockSpec((tm, D), lambda j, i: (i, 0)), pl.BlockSpec((D, D_MODEL), lambda j, i: (0, j)), tab, tab, tab],
                [out_spec(J, dils[J // 3]) for J in range(n_out)],
                [jax.ShapeDtypeStruct((S // dils[J // 3], dils[J // 3] * D_MODEL), BF16) for J in range(n_out)],
                [pltpu.VMEM((N_PAIRS, tm, LANES), F32)], (n, w, *tabs), ("arbitrary", "arbitrary"), comm)
    return [res[3 * g:3 * g + 3] for g in range(len(dils))], res[n_out:]


def _relayout(name, x, R, to_view, tm=512):
    rows, cols = x.shape
    S = rows if to_view else rows * R

    def body(x_ref, o_ref, scr):
        if to_view:
            _stage_chunks(scr, [x_ref[:, b * LANES:(b + 1) * LANES].astype(F32) for b in range(N_PAIRS)])
            for r in range(R):
                for b in range(N_PAIRS):
                    col = r * D_MODEL + b * LANES
                    o_ref[:, col:col + LANES] = _strided_rows(scr, b, r, tm // R, R).astype(o_ref.dtype)
        else:
            for r in range(R):
                for b in range(N_PAIRS):
                    col = r * D_MODEL + b * LANES
                    scr.at[b][pl.ds(r, tm // R, stride=R), :] = x_ref[:, col:col + LANES].astype(F32)
            for b in range(N_PAIRS):
                o_ref[:, b * LANES:(b + 1) * LANES] = scr[b].astype(o_ref.dtype)

    nat = pl.BlockSpec((tm, D_MODEL), lambda i: (i, 0))
    view = pl.BlockSpec((tm // R, R * D_MODEL), lambda i: (i, 0))
    out_shape = (S // R, R * D_MODEL) if to_view else (S, D_MODEL)
    return pl.pallas_call(
        body, grid=(S // tm,), in_specs=[nat if to_view else view], out_specs=view if to_view else nat,
        out_shape=jax.ShapeDtypeStruct(out_shape, x.dtype), scratch_shapes=[pltpu.VMEM((N_PAIRS, tm, LANES), F32)],
        compiler_params=_params("parallel"), name=name)(x)


def _lo_lanes():
    return lax.broadcasted_iota(jnp.int32, (1, LANES), 1) < HEAD_DIM


def _rep_rows(x2, lo):
    sw = pltpu.roll(x2, HEAD_DIM, 1)
    return jnp.where(lo, x2, sw), jnp.where(lo, sw, x2)


def _pair_cols(h):
    return slice((h // 2) * LANES, (h // 2 + 1) * LANES)


def _head_lanes(lo, h):
    return lo if h % 2 == 0 else jnp.logical_not(lo)


def _band_masks(t, first):
    ri = lax.broadcasted_iota(jnp.int32, (t, t), 0)
    ci = lax.broadcasted_iota(jnp.int32, (t, t), 1)
    neg_prev = jnp.where((ci >= ri) & jnp.logical_not(first), 0.0, NEG_INF)
    neg_cur = jnp.where(ci <= ri, 0.0, NEG_INF)
    return neg_prev, neg_cur


def _dil_specs(L, R, t, qcol, kcol, vcol):
    W = D_MODEL
    prev = lambda qi: jnp.maximum(qi - 1, 0)
    return dict(
        q=pl.BlockSpec((t, W), lambda r, qi: (qi, qcol(r))),
        kp=pl.BlockSpec((t, W), lambda r, qi: (prev(qi), kcol(r))), kc=pl.BlockSpec((t, W), lambda r, qi: (qi, kcol(r))),
        vp=pl.BlockSpec((t, W), lambda r, qi: (prev(qi), vcol(r))), vc=pl.BlockSpec((t, W), lambda r, qi: (qi, vcol(r))),
        own=pl.BlockSpec((t, W), lambda r, qi: (qi, r)), tab=pl.BlockSpec((t, LANES), lambda r, qi: (qi, r)))


def _dil_fwd(name, x, qcol, kcol, vcol, R, L):
    t = BAND_STEPS
    W = D_MODEL
    sp = _dil_specs(L, R, t, qcol, kcol, vcol)

    def body(q_ref, kp_ref, kc_ref, vp_ref, vc_ref, o_ref, lse_ref):
        lo = _lo_lanes()
        neg_p, neg_c = _band_masks(t, pl.program_id(1) == 0)
        s_p, s_c = [], []
        for h in range(N_HEADS):
            cols = _pair_cols(h)
            qh = jnp.where(_head_lanes(lo, h), q_ref[:, cols], 0)
            s_p.append(_dot(qh, kp_ref[:, cols], NT))
            s_c.append(_dot(qh, kc_ref[:, cols], NT))
        s_p = jnp.stack(s_p) + neg_p[None]
        s_c = jnp.stack(s_c) + neg_c[None]
        m = jnp.maximum(jnp.max(s_p, axis=2, keepdims=True), jnp.max(s_c, axis=2, keepdims=True))
        p_p, p_c = jnp.exp(s_p - m), jnp.exp(s_c - m)
        l = jnp.sum(p_p, axis=2, keepdims=True) + jnp.sum(p_c, axis=2, keepdims=True)
        inv, lse = 1.0 / l, m + jnp.log(l)
        p_p, p_c = p_p.astype(BF16), p_c.astype(BF16)
        for p in range(N_PAIRS):
            cols = _pair_cols(2 * p)
            o2 = jnp.zeros((t, LANES), F32)
            for h in (2 * p, 2 * p + 1):
                hm = _head_lanes(lo, h)
                pv = _dot(p_p[h], jnp.where(hm, vp_ref[:, cols], 0), NN) + _dot(p_c[h], jnp.where(hm, vc_ref[:, cols], 0), NN)
                o2 = o2 + pv * inv[h]
            o_ref[:, cols] = o2
            lse_ref[:, cols] = jnp.where(lo, lse[2 * p], lse[2 * p + 1])

    return pl.pallas_call(
        body, grid=(R, L // t), in_specs=[sp["q"], sp["kp"], sp["kc"], sp["vp"], sp["vc"]], out_specs=[sp["own"], sp["own"]],
        out_shape=[jax.ShapeDtypeStruct((L, R * W), F32), jax.ShapeDtypeStruct((L, R * W), F32)],
        compiler_params=_params("parallel", "parallel"), name=name)(x[0], x[1], x[1], x[2], x[2])


def _dil_scores(lo, q_ref, do_ref, o_ref, lse_ref, kv_refs):
    s = [[] for _ in kv_refs]
    dp = [[] for _ in kv_refs]
    lse, d = [], []
    for h in range(N_HEADS):
        cols = _pair_cols(h)
        hm = _head_lanes(lo, h)
        qh, doh = jnp.where(hm, q_ref[:, cols], 0), jnp.where(hm, do_ref[:, cols], 0)
        for i, (k_ref, v_ref) in enumerate(kv_refs):
            s[i].append(_dot(qh, k_ref[:, cols], NT))
            dp[i].append(_dot(doh, v_ref[:, cols], NT))
        lse.append(_rep_rows(lse_ref[:, cols], lo)[h % 2])
        dd = do_ref[:, cols].astype(F32) * o_ref[:, cols].astype(F32)
        d.append(jnp.sum(jnp.where(hm, dd, 0.0), axis=1, keepdims=True))
    return (*[jnp.stack(x) for x in s], *[jnp.stack(x) for x in dp], jnp.stack(lse), jnp.stack(d))


def _dil_dq(name, x, do, o, lse, tabs, qcol, kcol, vcol, R, L):
    t = BAND_STEPS
    W = D_MODEL
    sp = _dil_specs(L, R, t, qcol, kcol, vcol)

    def body(q_ref, kp_ref, kc_ref, vp_ref, vc_ref, do_ref, o_ref, lse_ref, c_ref, sa_ref, sb_ref, dq_ref):
        lo = _lo_lanes()
        neg_p, neg_c = _band_masks(t, pl.program_id(1) == 0)
        s_p, s_c, dp_p, dp_c, lse, d = _dil_scores(lo, q_ref, do_ref, o_ref, lse_ref, ((kp_ref, vp_ref), (kc_ref, vc_ref)))
        ds_p = (jnp.exp(s_p + neg_p[None] - lse) * (dp_p - d)).astype(BF16)
        ds_c = (jnp.exp(s_c + neg_c[None] - lse) * (dp_c - d)).astype(BF16)
        for p in range(N_PAIRS):
            cols = _pair_cols(2 * p)
            dq2 = jnp.zeros((t, LANES), F32)
            for h in (2 * p, 2 * p + 1):
                hm = _head_lanes(lo, h)
                dq2 = dq2 + _dot(ds_p[h], jnp.where(hm, kp_ref[:, cols], 0), NN) + _dot(ds_c[h], jnp.where(hm, kc_ref[:, cols], 0), NN)
            dq_ref[:, cols] = _rotate(dq2 * SOFTMAX_SCALE, c_ref[...], sa_ref[...], sb_ref[...], -1.0).astype(BF16)

    return pl.pallas_call(
        body, grid=(R, L // t),
        in_specs=[sp["q"], sp["kp"], sp["kc"], sp["vp"], sp["vc"], sp["own"], sp["own"], sp["own"], sp["tab"], sp["tab"], sp["tab"]],
        out_specs=sp["own"], out_shape=jax.ShapeDtypeStruct((L, R * W), BF16),
        compiler_params=_params("parallel", "parallel"), name=name)(x[0], x[1], x[1], x[2], x[2], do, o, lse, *tabs)


def _dil_dkv(name, x, do, o, lse, tabs, qcol, kcol, vcol, R, L):
    t = BAND_STEPS
    W = D_MODEL
    nq = L // t
    nxt = lambda kb: jnp.minimum(kb + 1, nq - 1)
    cur_q = pl.BlockSpec((t, W), lambda r, kb: (kb, qcol(r)))
    nxt_q = pl.BlockSpec((t, W), lambda r, kb: (nxt(kb), qcol(r)))
    cur_o = pl.BlockSpec((t, W), lambda r, kb: (kb, r))
    nxt_o = pl.BlockSpec((t, W), lambda r, kb: (nxt(kb), r))
    tab = pl.BlockSpec((t, LANES), lambda r, kb: (kb, r))

    def body(k_ref, v_ref, qc_ref, qn_ref, doc_ref, don_ref, oc_ref, on_ref, lc_ref, ln_ref, c_ref, sa_ref, sb_ref,
             dk_ref, dv_ref):
        lo = _lo_lanes()
        ri = lax.broadcasted_iota(jnp.int32, (t, t), 0)
        ci = lax.broadcasted_iota(jnp.int32, (t, t), 1)
        neg_c = jnp.where(ci <= ri, 0.0, NEG_INF)
        neg_n = jnp.where((ci >= ri) & (pl.program_id(1) + 1 < nq), 0.0, NEG_INF)
        blocks = []
        for q_ref, do_ref, o_ref, l_ref, neg in ((qc_ref, doc_ref, oc_ref, lc_ref, neg_c), (qn_ref, don_ref, on_ref, ln_ref, neg_n)):
            s, dp, lse, d = _dil_scores(lo, q_ref, do_ref, o_ref, l_ref, ((k_ref, v_ref),))
            pr = jnp.exp(s + neg[None] - lse)
            blocks.append((q_ref, do_ref, pr.astype(BF16), (pr * (dp - d)).astype(BF16)))
        for p in range(N_PAIRS):
            cols = _pair_cols(2 * p)
            dk2 = jnp.zeros((t, LANES), F32)
            dv2 = jnp.zeros((t, LANES), F32)
            for q_ref, do_ref, pr, ds in blocks:
                for h in (2 * p, 2 * p + 1):
                    hm = _head_lanes(lo, h)
                    dv2 = dv2 + _dot(pr[h], jnp.where(hm, do_ref[:, cols], 0), TN)
                    dk2 = dk2 + _dot(ds[h], jnp.where(hm, q_ref[:, cols], 0), TN)
            dk_ref[:, cols] = _rotate(dk2, c_ref[...], sa_ref[...], sb_ref[...], -1.0).astype(BF16)
            dv_ref[:, cols] = dv2.astype(BF16)

    kcur = pl.BlockSpec((t, W), lambda r, kb: (kb, kcol(r)))
    vcur = pl.BlockSpec((t, W), lambda r, kb: (kb, vcol(r)))
    return pl.pallas_call(
        body, grid=(R, nq),
        in_specs=[kcur, vcur, cur_q, nxt_q, cur_o, nxt_o, cur_o, nxt_o, cur_o, nxt_o, tab, tab, tab],
        out_specs=[cur_o, cur_o], out_shape=[jax.ShapeDtypeStruct((L, R * W), BF16)] * 2,
        compiler_params=_params("parallel", "parallel"), name=name)(x[1], x[2], x[0], x[0], do, do, o, o, lse, lse, *tabs)


def _fox_operands(q2, k2, kb2, lo, hh):
    lane = lax.broadcasted_iota(jnp.int32, (1, LANES), 1)
    if hh == 0:
        ones = ((lane >= HEAD_DIM) & (lane < HEAD_DIM + 3)).astype(BF16)
        return jnp.where(lo, q2, ones), jnp.where(lo, k2, kb2)
    ones = (lane < 3).astype(BF16)
    return jnp.where(lo, ones, q2), jnp.where(lo, kb2, k2)


def _causal_neg(t):
    ri = lax.broadcasted_iota(jnp.int32, (t, t), 0)
    ci = lax.broadcasted_iota(jnp.int32, (t, t), 1)
    return jnp.where(ci <= ri, 0.0, NEG_INF)


def _fox_fwd(name, qkv, kbias, t):
    S = qkv.shape[0]
    W = D_MODEL
    nq = S // t
    rep = t // LANES

    def body(q_ref, k_ref, v_ref, kb_ref, o_ref, lse_ref, m_scr, l_scr, acc_scr):
        qi, j = pl.program_id(0), pl.program_id(1)
        lo = _lo_lanes()

        @pl.when(j == 0)
        def _():
            m_scr[...] = jnp.full_like(m_scr, NEG_INF)
            l_scr[...] = jnp.zeros_like(l_scr)
            acc_scr[...] = jnp.zeros_like(acc_scr)

        def step(masked):
            neg = _causal_neg(t) if masked else None

            def pair(p, carry):
                cs = pl.ds(pl.multiple_of(p * LANES, LANES), LANES)
                q2, k2, v2, kb2 = q_ref[:, cs], k_ref[:, cs], v_ref[:, cs], kb_ref[:, cs]
                pvs, alphas = [], []
                for hh in range(2):
                    hm = lo if hh == 0 else jnp.logical_not(lo)
                    qh, kh = _fox_operands(q2, k2, kb2, lo, hh)
                    s = _dot(qh, kh, NT)
                    if masked:
                        s = s + neg
                    h = 2 * p + hh
                    m_prev = m_scr[h]
                    m_new = jnp.maximum(m_prev, jnp.max(s, axis=1, keepdims=True))
                    pe = jnp.exp(s - jnp.tile(m_new, (1, rep)))
                    alpha = jnp.exp(m_prev - m_new)
                    l_scr[h] = alpha * l_scr[h] + jnp.sum(pe, axis=1, keepdims=True)
                    m_scr[h] = m_new
                    pvs.append(_dot(pe.astype(BF16), jnp.where(hm, v2, 0), NN))
                    alphas.append(alpha)
                acc_scr[:, cs] = acc_scr[:, cs] * jnp.where(lo, alphas[0], alphas[1]) + pvs[0] + pvs[1]
                return carry

            lax.fori_loop(0, N_PAIRS, pair, 0)

        @pl.when(j < qi)
        def _():
            step(False)

        @pl.when(j == qi)
        def _():
            step(True)

        @pl.when(j == nq - 1)
        def _():
            for p in range(N_PAIRS):
                cols = slice(p * LANES, (p + 1) * LANES)
                l2 = jnp.where(lo, l_scr[2 * p], l_scr[2 * p + 1])
                m2 = jnp.where(lo, m_scr[2 * p], m_scr[2 * p + 1])
                o_ref[:, cols] = (acc_scr[:, cols] / l2).astype(BF16)
                lse_ref[:, cols] = m2 + jnp.log(l2)

    kv = lambda col: pl.BlockSpec((t, W), lambda qi, j: (jnp.minimum(j, qi), col))
    own = pl.BlockSpec((t, W), lambda qi, j: (qi, 0))
    return pl.pallas_call(
        body, grid=(nq, nq), in_specs=[own, kv(1), kv(2), kv(0)], out_specs=[own, own],
        out_shape=[jax.ShapeDtypeStruct((S, W), BF16), jax.ShapeDtypeStruct((S, W), F32)],
        scratch_shapes=[pltpu.VMEM((N_HEADS, t, LANES), F32), pltpu.VMEM((N_HEADS, t, LANES), F32), pltpu.VMEM((t, W), F32)],
        compiler_params=_params("parallel", "arbitrary"), name=name)(qkv, qkv, qkv, kbias)


def _fox_head_grads(qh, kh, v2, doh, neg, lse_h, d_h, rep):
    s = _dot(qh, kh, NT)
    if neg is not None:
        s = s + neg
    p = jnp.exp(s - jnp.tile(lse_h, (1, rep)))
    return p, p * (_dot(doh, v2, NT) - d_h)


def _fox_bwd(name, qkv, kbias, do, o, lse, t):
    S = qkv.shape[0]
    W = D_MODEL
    nq = S // t
    rep = t // LANES

    def body(q_ref, k_ref, v_ref, kb_ref, do_ref, o_ref, lse_ref, dq_ref, dk_ref, dv_ref, rs_ref, dc_ref, dq_scr, dk_scr, dv_scr):
        kb, j = pl.program_id(0), pl.program_id(1)
        lo = _lo_lanes()
        lane = lax.broadcasted_iota(jnp.int32, (1, LANES), 1)
        rows = pl.ds(pl.multiple_of(j * t, t), t)

        @pl.when((kb == 0) & (j == 0))
        def _():
            dq_scr[...] = jnp.zeros_like(dq_scr)
            rs_ref[...] = jnp.zeros_like(rs_ref)

        @pl.when(j == 0)
        def _():
            dk_scr[...] = jnp.zeros_like(dk_scr)
            dv_scr[...] = jnp.zeros_like(dv_scr)
            dc_ref[...] = jnp.zeros_like(dc_ref)

        def step(masked):
            neg = _causal_neg(t) if masked else None

            def pair(p, carry):
                cs = pl.ds(pl.multiple_of(p * LANES, LANES), LANES)
                q2, k2, v2, kb2, do2 = q_ref[:, cs], k_ref[:, cs], v_ref[:, cs], kb_ref[:, cs], do_ref[:, cs]
                dd = do2.astype(F32) * o_ref[:, cs].astype(F32)
                lse_h = _rep_rows(lse_ref[:, cs], lo)
                dq2 = jnp.zeros((t, LANES), F32)
                dv2 = jnp.zeros((t, LANES), F32)
                dk2 = jnp.zeros((t, LANES), F32)
                for hh in range(2):
                    hm = lo if hh == 0 else jnp.logical_not(lo)
                    qh, kh = _fox_operands(q2, k2, kb2, lo, hh)
                    doh = jnp.where(hm, do2, 0)
                    d_h = jnp.sum(jnp.where(hm, dd, 0.0), axis=1, keepdims=True)
                    pr, ds = _fox_head_grads(qh, kh, v2, doh, neg, lse_h[hh], d_h, rep)
                    rs_ref[rows, :] += jnp.where(lane == 2 * p + hh, jnp.sum(ds, axis=1, keepdims=True), 0.0)
                    dc_ref[p, hh:hh + 1, :] += jnp.sum(ds, axis=0, keepdims=True)
                    dsb = ds.astype(BF16)
                    dv2 = dv2 + _dot(pr.astype(BF16), doh, TN)
                    dk2 = dk2 + _dot(dsb, jnp.where(hm, q2, 0), TN)
                    dq2 = dq2 + _dot(dsb, jnp.where(hm, k2, 0), NN)
                dv_scr[:, cs] += dv2
                dk_scr[:, cs] += dk2
                dq_scr[rows, cs] += dq2
                return carry

            lax.fori_loop(0, N_PAIRS, pair, 0)
            if masked:
                dq_ref[...] = (dq_scr[rows, :] * SOFTMAX_SCALE).astype(BF16)

        @pl.when(j > kb)
        def _():
            step(False)

        @pl.when(j == kb)
        def _():
            step(True)

        @pl.when(j == nq - 1)
        def _():
            dv_ref[...] = dv_scr[...].astype(BF16)
            dk_ref[...] = dk_scr[...].astype(BF16)

    qrow = pl.BlockSpec((t, W), lambda kb, j: (jnp.maximum(j, kb), 0))
    krow = lambda col: pl.BlockSpec((t, W), lambda kb, j: (kb, col))
    own = pl.BlockSpec((t, W), lambda kb, j: (kb, 0))
    wide = jax.ShapeDtypeStruct((S, W), BF16)
    return pl.pallas_call(
        body, grid=(nq, nq), in_specs=[qrow, krow(1), krow(2), krow(0), qrow, qrow, qrow],
        out_specs=[own, own, own, pl.BlockSpec((S, LANES), lambda kb, j: (0, 0)), pl.BlockSpec((N_PAIRS, 2, t), lambda kb, j: (0, 0, kb))],
        out_shape=[wide, wide, wide, jax.ShapeDtypeStruct((S, LANES), F32), jax.ShapeDtypeStruct((N_PAIRS, 2, S), F32)],
        scratch_shapes=[pltpu.VMEM((S, W), F32), pltpu.VMEM((t, W), F32), pltpu.VMEM((t, W), F32)],
        compiler_params=pltpu.CompilerParams(dimension_semantics=("arbitrary", "arbitrary"), vmem_limit_bytes=FOX_BWD_VMEM),
        name=name)(qkv, qkv, qkv, kbias, do, o, lse)


def _view_spec(tm, R, index=lambda i: (i, 0)):
    return pl.BlockSpec((tm // R, R * D_MODEL), index)


def _matmul_nt_views(name, a, w, dils, tm=512):
    S, K = a.shape

    def body(a_ref, w_ref, *rest):
        res = _dot(a_ref[...].astype(BF16), w_ref[...], NT)
        _write_views([res[:, b * LANES:(b + 1) * LANES] for b in range(N_PAIRS)], rest[-1], rest[:-1], dils, tm)

    return pl.pallas_call(
        body, grid=(S // tm,), in_specs=[pl.BlockSpec((tm, K), lambda i: (i, 0)), pl.BlockSpec((D_MODEL, K), lambda i: (0, 0))],
        out_specs=[_view_spec(tm, R) for R in dils],
        out_shape=[jax.ShapeDtypeStruct((S // R, R * D_MODEL), BF16) for R in dils],
        scratch_shapes=[pltpu.VMEM((N_PAIRS, tm, LANES), F32)], compiler_params=_params("parallel"), name=name)(a, w)


def _write_views(chunks, scr, out_refs, dils, tm):
    if any(R > 1 for R in dils):
        _stage_chunks(scr, chunks)
    for ref, R in zip(out_refs, dils):
        for b, x in enumerate(chunks):
            if R == 1:
                ref[:, b * LANES:(b + 1) * LANES] = x.astype(ref.dtype)
                continue
            for r in range(R):
                col = r * D_MODEL + b * LANES
                ref[:, col:col + LANES] = _strided_rows(scr, b, r, tm // R, R).astype(ref.dtype)


def _combine(name, os_, lses, dils, tm=256):
    S = os_[0].shape[0] * dils[0]
    G = len(dils)

    def body(*refs):
        o_refs, l_refs = refs[:G], refs[G:2 * G]
        o_outs, l_outs = refs[2 * G:3 * G], refs[3 * G:4 * G]
        stage = refs[4 * G:]
        for g, R in enumerate(dils):
            if R == 1:
                continue
            for src, dst in ((o_refs[g], stage[2 * g]), (l_refs[g], stage[2 * g + 1])):
                for r in range(R):
                    for b in range(N_PAIRS):
                        col = r * D_MODEL + b * LANES
                        dst.at[b][pl.ds(r, tm // R, stride=R), :] = src[:, col:col + LANES]
        o_chunks, l_chunks = [], []
        for b in range(N_PAIRS):
            cols = slice(b * LANES, (b + 1) * LANES)
            os_b = [o_refs[g][:, cols] if R == 1 else stage[2 * g][b] for g, R in enumerate(dils)]
            ls = [l_refs[g][:, cols] if R == 1 else stage[2 * g + 1][b] for g, R in enumerate(dils)]
            m = functools.reduce(jnp.maximum, ls)
            ws = [jnp.exp(l - m) for l in ls]
            den = functools.reduce(jnp.add, ws)
            o_chunks.append(functools.reduce(jnp.add, [w * o for w, o in zip(ws, os_b)]) / den)
            l_chunks.append(m + jnp.log(den))
        _write_views(o_chunks, stage[0], o_outs, dils, tm)
        _write_views(l_chunks, stage[1], l_outs, dils, tm)

    specs = [_view_spec(tm, R) for R in dils]
    shapes = lambda dt: [jax.ShapeDtypeStruct((S // R, R * D_MODEL), dt) for R in dils]
    res = pl.pallas_call(
        body, grid=(S // tm,), in_specs=specs * 2, out_specs=specs * 2, out_shape=shapes(BF16) + shapes(F32),
        scratch_shapes=[pltpu.VMEM((N_PAIRS, tm, LANES), F32)] * (2 * G), compiler_params=_params("parallel"),
        name=name)(*os_, *lses)
    return res[:G], res[G:]


def _tri_matmul(tri, x):
    hi, mid, lo = _split3(x)
    return _dot(tri, hi, NN) + _dot(tri, mid, NN) + _dot(tri, lo, NN)


def _split3(x):
    hi = x.astype(BF16)
    r1 = x - hi.astype(F32)
    mid = r1.astype(BF16)
    return hi, mid, (r1 - mid.astype(F32)).astype(BF16)


def _gate_fwd(name, z, bf, tb=512):
    S = z.shape[0]

    def body(z_ref, b_ref, kb_ref, carry):
        @pl.when(pl.program_id(0) == 0)
        def _():
            carry[...] = jnp.zeros_like(carry)

        lf = jax.nn.log_sigmoid(z_ref[...] + b_ref[...])
        ri = lax.broadcasted_iota(jnp.int32, (tb, tb), 0)
        ci = lax.broadcasted_iota(jnp.int32, (tb, tb), 1)
        tri = (ci <= ri).astype(BF16)
        c = _tri_matmul(tri, lf) + carry[...]
        carry[...] = c[tb - 1:tb, :]
        head = lax.broadcasted_iota(jnp.int32, (LANES, D_MODEL), 0)
        col = lax.broadcasted_iota(jnp.int32, (LANES, D_MODEL), 1)
        base = (head >> 1) * LANES + jnp.where((head & 1) == 0, HEAD_DIM, 0)
        kb = jnp.zeros((tb, D_MODEL), F32)
        for i, piece in enumerate(_split3(-c)):
            place = ((col == base + i) & (head < N_HEADS)).astype(BF16)
            kb = kb + _dot(piece, place, NN)
        kb_ref[...] = kb.astype(BF16)

    row = pl.BlockSpec((tb, LANES), lambda i: (i, 0))
    return pl.pallas_call(
        body, grid=(S // tb,), in_specs=[row, pl.BlockSpec((1, LANES), lambda i: (0, 0))],
        out_specs=pl.BlockSpec((tb, D_MODEL), lambda i: (i, 0)), out_shape=jax.ShapeDtypeStruct((S, D_MODEL), BF16),
        scratch_shapes=[pltpu.VMEM((1, LANES), F32)], compiler_params=_params("arbitrary"), name=name)(z, bf)


def _gate_bwd(name, dc, z, bf, tb=512):
    S = z.shape[0]
    nb = S // tb

    def body(dc_ref, z_ref, b_ref, dz_ref, db_ref, carry):
        @pl.when(pl.program_id(0) == 0)
        def _():
            carry[...] = jnp.zeros_like(carry)
            db_ref[...] = jnp.zeros_like(db_ref)

        ri = lax.broadcasted_iota(jnp.int32, (tb, tb), 0)
        ci = lax.broadcasted_iota(jnp.int32, (tb, tb), 1)
        tri = (ci >= ri).astype(BF16)
        dlf = _tri_matmul(tri, dc_ref[...]) + carry[...]
        carry[...] = dlf[0:1, :]
        dz = dlf * jax.nn.sigmoid(-(z_ref[...] + b_ref[...]))
        dz_ref[...] = dz
        db_ref[...] += jnp.sum(dz, axis=0, keepdims=True)

    row = pl.BlockSpec((tb, LANES), lambda i: (nb - 1 - i, 0))
    vec = pl.BlockSpec((1, LANES), lambda i: (0, 0))
    return pl.pallas_call(
        body, grid=(nb,), in_specs=[row, row, vec], out_specs=[row, vec],
        out_shape=[jax.ShapeDtypeStruct((S, LANES), F32), jax.ShapeDtypeStruct((1, LANES), F32)],
        scratch_shapes=[pltpu.VMEM((1, LANES), F32)], compiler_params=_params("arbitrary"), name=name)(dc, z, bf)


def _ffn_gu(name, n, wgu, comm=None, tm=1024):
    S, D = n.shape
    nb = N_DEV // 2

    def body(n_ref, wg_ref, wu_ref, gu_ref, act_ref):
        x = n_ref[...]
        g = _dot(x, wg_ref[...], NN)
        u = _dot(x, wu_ref[...], NN)
        gu_ref[0] = g.astype(BF16)
        gu_ref[1] = u.astype(BF16)
        act_ref[...] = (g * jax.nn.sigmoid(g) * u).astype(BF16)

    return _call(
        name, body, (nb, S // tm),
        [pl.BlockSpec((tm, D), lambda j, i: (i, 0)), pl.BlockSpec((None, D, FF_BLK), lambda j, i: (j, 0, 0)),
         pl.BlockSpec((None, D, FF_BLK), lambda j, i: (j + nb, 0, 0))],
        [pl.BlockSpec((2, None, tm, FF_BLK), lambda j, i: (0, j, i, 0)), pl.BlockSpec((None, tm, FF_BLK), lambda j, i: (j, i, 0))],
        [jax.ShapeDtypeStruct((2, nb, S, FF_BLK), BF16), jax.ShapeDtypeStruct((nb, S, FF_BLK), BF16)], [],
        (n, wgu, wgu), ("parallel", "parallel"), comm)


def _ffn_down(name, act, wd, resid, comm=None, tm=1024):
    nb, S, _ = act.shape
    D = wd.shape[1]

    def epilogue(acc, ex, outs, j):
        outs[0][...] = acc + ex[0][...]

    o_spec = pl.BlockSpec((tm, D), lambda i, j, k: (i, 0))
    return _mm_call(name, (S // tm, 1, nb), act, pl.BlockSpec((None, tm, FF_BLK), lambda i, j, k: (k, i, 0)),
                    wd, pl.BlockSpec((FF_BLK, D), lambda i, j, k: (k, 0)), NN,
                    [jax.ShapeDtypeStruct((S, D), F32)], [o_spec], (tm, D), epilogue, (resid,), (o_spec,), comm=comm)


def _ffn_dact(name, dh, wd, gu, comm=None, tm=512):
    S, D = dh.shape
    nb = N_DEV // 2

    def epilogue(acc, ex, outs, j):
        g = ex[0][0].astype(F32)
        u = ex[0][1].astype(F32)
        sig = jax.nn.sigmoid(g)
        outs[0][0] = (acc * u * (sig * (1.0 + g * (1.0 - sig)))).astype(BF16)
        outs[0][1] = (acc * (g * sig)).astype(BF16)

    gu_spec = pl.BlockSpec((2, None, tm, FF_BLK), lambda j, i, k: (0, j, i, 0))
    return _mm_call(name, (nb, S // tm, 1), dh, pl.BlockSpec((tm, D), lambda j, i, k: (i, 0)),
                    wd, pl.BlockSpec((FF_BLK, D), lambda j, i, k: (j, 0)), NT,
                    [jax.ShapeDtypeStruct((2, nb, S, FF_BLK), BF16)], [gu_spec], (tm, FF_BLK), epilogue, (gu,), (gu_spec,),
                    col_axis=0, comm=comm)


def _ffn_dwgu(name, n, dgu, comm=None, tm=1024, tk=1024):
    S, D = n.shape
    dgu8 = dgu.reshape(N_DEV, S, FF_BLK)
    return _mm_call(name, (N_DEV, D // tm, S // tk), n, pl.BlockSpec((tk, tm), lambda d, i, k: (k, i)),
                    dgu8, pl.BlockSpec((None, tk, FF_BLK), lambda d, i, k: (d, k, 0)), TN,
                    [jax.ShapeDtypeStruct((N_DEV, D, FF_BLK), BF16)],
                    [pl.BlockSpec((None, tm, FF_BLK), lambda d, i, k: (d, i, 0))], (tm, FF_BLK), comm=comm)


def _ffn_dwd(name, act, dh, tk=1024):
    nb, S, _ = act.shape
    D = dh.shape[1]
    out = _mm_call(name, (nb, 1, S // tk), act, pl.BlockSpec((None, tk, FF_BLK), lambda b, j, k: (b, k, 0)),
                   dh, pl.BlockSpec((tk, D), lambda b, j, k: (k, 0)), TN,
                   [jax.ShapeDtypeStruct((nb, FF_BLK, D), BF16)],
                   [pl.BlockSpec((None, FF_BLK, D), lambda b, j, k: (b, 0, 0))], (FF_BLK, D))[0]
    return out.reshape(N_DEV, FF_BLK // 2, D)


def _ffn_dn(name, dgu, wgu, comm=None, tm=1024):
    S = dgu.shape[2]
    D = wgu.shape[1]
    dgu8 = dgu.reshape(N_DEV, S, FF_BLK)
    return _mm_call(name, (S // tm, 1, N_DEV), dgu8, pl.BlockSpec((None, tm, FF_BLK), lambda i, j, k: (k, i, 0)),
                    wgu, pl.BlockSpec((None, D, FF_BLK), lambda i, j, k: (k, 0, 0)), NT,
                    [jax.ShapeDtypeStruct((S, D), F32)], [pl.BlockSpec((tm, D), lambda i, j, k: (i, 0))], (tm, D), comm=comm)


def _adamw(name, parts, w, m, v, tr):
    rows, cols = w.shape
    n_parts = len(parts)
    c1 = 1.0 - ADAM_B1 ** ADAM_STEP
    c2 = 1.0 - ADAM_B2 ** ADAM_STEP

    def body(*refs):
        p_refs = refs[:n_parts]
        w_ref, m_ref, v_ref, g_ref, d_ref, nm_ref, nv_ref = refs[n_parts:]
        g = p_refs[0][...].astype(F32)
        for r in p_refs[1:]:
            g = g + r[...].astype(F32)
        mm = ADAM_B1 * m_ref[...] + (1.0 - ADAM_B1) * g
        vv = ADAM_B2 * v_ref[...] + (1.0 - ADAM_B2) * (g * g)
        g_ref[...] = g
        nm_ref[...] = mm
        nv_ref[...] = vv
        d_ref[...] = -ADAM_LR * ((mm / c1) / (jnp.sqrt(vv / c2) + ADAM_EPS) + ADAM_WD * w_ref[...])

    blk = pl.BlockSpec((tr, cols), lambda i: (i, 0))
    out = jax.ShapeDtypeStruct((rows, cols), F32)
    return pl.pallas_call(
        body, grid=(rows // tr,), in_specs=[blk] * (n_parts + 3), out_specs=[blk] * 4, out_shape=[out] * 4,
        compiler_params=_params("parallel"), name=name)(*parts, w, m, v)


def _position():
    return lax.axis_index("x"), lax.axis_index("y"), lax.axis_index("c")


def _other_chips():
    x, y, _ = _position()
    return [(1 - x, y), (x, 1 - y), (1 - x, 1 - y)]


def _remote(src, dst, send, recv, k, to):
    return pltpu.make_async_remote_copy(src_ref=src, dst_ref=dst, send_sem=send.at[k], recv_sem=recv.at[k],
                                        device_id=to, device_id_type=MESH)


def _ag_send(blocks, direct=False):
    n_peer = 7 if direct else 4

    def copies(ins, outs, send, recv, local, r0=0, l0=0):
        x, y, c = _position()
        me = 4 * x + 2 * y + c
        peers = [(x, y, 1 - c)] + [(px, py, c) for px, py in _other_chips()]
        if direct:
            peers += [(px, py, 1 - c) for px, py in _other_chips()]
        cps = []
        for t, (src, dst) in enumerate(zip(ins, outs)):
            cps.append(pltpu.make_async_copy(src, dst.at[me], local.at[l0 + t]))
            cps += [_remote(src, dst.at[me], send, recv, r0 + n_peer * t + k, to) for k, to in enumerate(peers)]
        return cps

    outs = tuple(jax.ShapeDtypeStruct((N_DEV,) + b.shape, b.dtype) for b in blocks)
    return _Comm(tuple(blocks), outs, {}, copies, n_peer * len(blocks), len(blocks))


def _ag_forward(bufs):
    def copies(ins, outs, send, recv, local, r0=0, l0=0):
        x, y, c = _position()
        cps = []
        for t, buf in enumerate(outs):
            for k, (px, py) in enumerate(_other_chips()):
                slot = buf.at[4 * px + 2 * py + c]
                cps.append(_remote(slot, slot, send, recv, r0 + 3 * t + k, (x, y, 1 - c)))
        return cps

    outs = tuple(jax.ShapeDtypeStruct(b.shape, b.dtype) for b in bufs)
    return _Comm(tuple(bufs), outs, {t: t for t in range(len(bufs))}, copies, 3 * len(bufs), 0)


def _rs_swap(shares):
    def copies(ins, outs, send, recv, local, r0=0, l0=0):
        x, y, c = _position()
        return [_remote(src.at[:, 1 - c], dst, send, recv, r0 + t, (x, y, 1 - c)) for t, (src, dst) in enumerate(zip(ins, outs))]

    ins = tuple(s.reshape((4, 2) + s.shape[1:]) for s in shares)
    outs = tuple(jax.ShapeDtypeStruct((4,) + s.shape[1:], s.dtype) for s in shares)
    return _Comm(ins, outs, {}, copies, len(shares), 0)


def _rs_exchange(sums):
    def copies(ins, outs, send, recv, local, r0=0, l0=0):
        _, _, c = _position()
        return [_remote(src.at[2 * px + py], dst.at[k], send, recv, r0 + 3 * t + k, (px, py, c))
                for t, (src, dst) in enumerate(zip(ins, outs)) for k, (px, py) in enumerate(_other_chips())]

    outs = tuple(jax.ShapeDtypeStruct((3,) + s.shape[1:], s.dtype) for s in sums)
    return _Comm(tuple(sums), outs, {}, copies, 3 * len(sums), 0)


def _comm_call(name, comm):
    return _call(name, lambda: None, (), [], [], [], [], (), (), comm)


def _pair_sum(name, share, got, core, tr):
    _, rows, cols = share.shape

    def body(c_ref, a_ref, b_ref, o_ref):
        o_ref[...] = (a_ref[...].astype(F32) + b_ref[...].astype(F32)).astype(o_ref.dtype)

    grid_spec = pltpu.PrefetchScalarGridSpec(
        num_scalar_prefetch=1, grid=(4, rows // tr),
        in_specs=[pl.BlockSpec((None, None, tr, cols), lambda q, i, c: (q, c[0], i, 0)),
                  pl.BlockSpec((None, tr, cols), lambda q, i, c: (q, i, 0))],
        out_specs=pl.BlockSpec((None, tr, cols), lambda q, i, c: (q, i, 0)))
    return pl.pallas_call(
        body, grid_spec=grid_spec, out_shape=jax.ShapeDtypeStruct((4, rows, cols), share.dtype),
        compiler_params=_params("parallel", "parallel"), name=name)(core, share.reshape(4, 2, rows, cols), got)


TENSORS = ("a_w_in", "a_w_out", "b_w_in", "b_w_out", "gu0", "gu1", "dn0", "dn1")
ROW_TILE = {"a_w_in": 256, "a_w_out": 128, "b_w_in": 256, "b_w_out": 128, "gu0": 256, "gu1": 256, "dn0": 176, "dn1": 176}
A_BLK = 9 * D_MODEL // N_DEV
B_BLK = 386
B_IN = 3 * D_MODEL + N_HEADS
B_IN_PAD = 3 * D_MODEL + LANES


def kernel(x, a_norm, a_w_in, a_w_out, b_norm, b_w_in, b_f, b_w_out, ffn_norm, ffn_w_gu, ffn_w_down, final_norm, loss_target, m_a_norm, m_a_w_in, m_a_w_out, m_b_norm, m_b_w_in, m_b_f, m_b_w_out, m_ffn_norm, m_ffn_w_gu, m_ffn_w_down, m_final_norm, v_a_norm, v_a_w_in, v_a_w_out, v_b_norm, v_b_w_in, v_b_f, v_b_w_out, v_ffn_norm, v_ffn_w_gu, v_ffn_w_down, v_final_norm):
    S = x.shape[1]
    xi, yi, ci = _position()
    dev = 4 * xi + 2 * yi + ci
    core = ci.reshape(1).astype(jnp.int32)
    h0, target = x.reshape(S, D_MODEL), loss_target.reshape(S, D_MODEL)

    def shards(a_in, a_out, b_in, b_out, gu, dn):
        return {"a_w_in": a_in[0], "a_w_out": a_out[0], "b_w_in": b_in[0], "b_w_out": b_out[0],
                "gu0": gu[0], "gu1": gu[1], "dn0": dn[0], "dn1": dn[1]}

    w_sh = shards(a_w_in, a_w_out, b_w_in, b_w_out, ffn_w_gu, ffn_w_down)
    m_sh = shards(m_a_w_in, m_a_w_out, m_b_w_in, m_b_w_out, m_ffn_w_gu, m_ffn_w_down)
    v_sh = shards(v_a_w_in, v_a_w_out, v_b_w_in, v_b_w_out, v_ffn_w_gu, v_ffn_w_down)
    wb = {n: w_sh[n].astype(BF16) for n in TENSORS}
    bf_pad = jnp.pad(b_f, ((0, 0), (0, LANES - N_HEADS)))
    tabs = _rope_tables(S)

    g_ain, g_aout = _comm_call("gather_a", _ag_send([wb["a_w_in"], wb["a_w_out"]]))
    g_ain, g_aout = _comm_call("forward_a", _ag_forward([g_ain, g_aout]))
    n0 = _rms_fwd("rms_a", h0, a_norm[0])
    later = [wb["b_w_in"], wb["b_w_out"], wb["gu0"], wb["dn0"], jnp.pad(b_norm, ((0, 7), (0, 0)))]
    w_a_in = g_ain.transpose(1, 0, 2).reshape(D_MODEL, 9 * D_MODEL)
    qkv_a, later = _a_proj("proj_a", n0, w_a_in, tabs, _ag_send(later))
    cols = [lambda r: r] * 3
    groups = [(g, dil, S // dil, qkv_a[g]) for g, (window, dil) in enumerate(DILATED_PATTERNS)]
    dils = [dil for _, dil in DILATED_PATTERNS]
    from_view = lambda tag, a, dil: a if dil == 1 else _relayout("unview%d_%s" % (dil, tag), a, dil, False)
    fwd = [_dil_fwd("dil_fwd%d" % g, view, *cols, dil, L) for g, dil, L, view in groups]
    o_views, lse_views = _combine("dil_combine", [f[0] for f in fwd], [f[1] for f in fwd], dils)
    o_a = o_views[0]
    w_a_out = g_aout.reshape(D_MODEL, D_MODEL)
    h1, (g_bin, g_bout, g_gu0, g_dn0, g_bnorm) = _matmul("out_a", o_a, w_a_out, "nn", F32, TM, 1024, 1024, resid=h0,
                                                         comm=_ag_forward(later))

    n1 = _rms_fwd("rms_f0", h1, ffn_norm[0])
    gu0, act0, g_gu1 = _ffn_gu("gu_f0", n1, g_gu0, _ag_send([wb["gu1"]]))
    w_dn0 = g_dn0.reshape(D_FF, D_MODEL)
    h2, g_dn1 = _ffn_down("down_f0", act0, w_dn0, h1, _ag_send([wb["dn1"]]))

    b_norm_full = g_bnorm[:, 0].reshape(D_MODEL)
    w_b_in = g_bin.transpose(1, 0, 2).reshape(D_MODEL, B_IN)
    w_b_gate = jnp.pad(w_b_in[:, 3 * D_MODEL:], ((0, 0), (0, LANES - N_HEADS)))
    w_b_cat = jnp.concatenate([w_b_in[:, :3 * D_MODEL], w_b_gate], axis=1)
    w_b_out = g_bout.reshape(D_MODEL, D_MODEL)
    n2 = _rms_fwd("rms_b", h2, b_norm_full)
    qkv, (g_gu1, g_dn1) = _matmul("proj_b", n2, w_b_in[:, :3 * D_MODEL], "nn", BF16, TM, 1024, 1024, col0_scale=SOFTMAX_SCALE,
                                  comm=_ag_forward([g_gu1, g_dn1]))
    z = _matmul("gate_b", n2, w_b_gate, "nn", F32, TM, LANES, 1024)
    kbias = _gate_fwd("gate_cumsum", z, bf_pad)
    tf = min(S, 512)
    o_b, lse_b = _fox_fwd("fox_fwd", qkv, kbias, tf)
    h3 = _matmul("out_b", o_b, w_b_out, "nn", F32, TM, 1024, 1024, resid=h2)

    w_dn1 = g_dn1.reshape(D_FF, D_MODEL)
    n3 = _rms_fwd("rms_f1", h3, ffn_norm[1])
    gu1, act1 = _ffn_gu("gu_f1", n3, g_gu1)
    h4 = _ffn_down("down_f1", act1, w_dn1, h3)[0]

    dh4, d_final, loss = _loss_head("loss_head", h4, final_norm, target)

    share, got, sums, others = {}, {}, {}, {}

    def pair_sums(*names):
        for n in names:
            sums[n] = _pair_sum("pair_" + n, share[n], got[n], core, ROW_TILE[n])

    dgu1 = _ffn_dact("dact_f1", dh4, w_dn1, gu1)[0]
    share["dn1"] = _ffn_dwd("dwd_f1", act1, dh4)
    share["gu1"] = _ffn_dwgu("dwgu_f1", n3, dgu1)[0]
    dn3, got["gu1"], got["dn1"] = _ffn_dn("dn_f1", dgu1, g_gu1, _rs_swap([share["gu1"], share["dn1"]]))
    dh3, d_ffn1 = _rms_bwd("rmsb_f1", dn3, h3, ffn_norm[1], dh4)
    pair_sums("gu1", "dn1")

    do_b = _matmul("dout_b", dh3, w_b_out, "nt", BF16, TM, 1024, 1024)
    share["b_w_out"] = _matmul("dwout_b", o_b, dh3, "tn", BF16, TM, 1024, 1024).reshape(N_DEV, 128, D_MODEL)
    dq_b, dk_b, dv_b, ds_rowsum, ds_colsum = _fox_bwd("fox_bwd", qkv, kbias, do_b, o_b, lse_b, tf)
    dc = ds_rowsum[:, :N_HEADS] - ds_colsum.reshape(N_HEADS, S).T
    dz, d_bf = _gate_bwd("gate_bwd", jnp.pad(dc, ((0, 0), (0, LANES - N_HEADS))), z, bf_pad)
    dproj_b = jnp.concatenate([dq_b, dk_b, dv_b, dz.astype(BF16)], axis=1)
    dw_b_in, (others["gu1"],) = _matmul("dwin_b", n2, dproj_b, "tn", BF16, TM, B_IN_PAD // 5, 1024, comm=_rs_exchange([sums["gu1"]]))
    dn2, (others["dn1"],) = _matmul("dn_b", dproj_b, w_b_cat, "nt", F32, TM, 1024, B_IN_PAD // 5, comm=_rs_exchange([sums["dn1"]]))
    dh2, d_bnorm = _rms_bwd("rmsb_b", dn2, h2, b_norm_full, dh3)
    share["b_w_in"] = dw_b_in[:, :B_IN].reshape(D_MODEL, N_DEV, B_BLK).transpose(1, 0, 2)

    dgu0, got["b_w_in"], got["b_w_out"] = _ffn_dact("dact_f0", dh2, w_dn0, gu0, _rs_swap([share["b_w_in"], share["b_w_out"]]))
    share["dn0"] = _ffn_dwd("dwd_f0", act0, dh2)
    pair_sums("b_w_in", "b_w_out")
    share["gu0"], others["b_w_in"], others["b_w_out"] = _ffn_dwgu(
        "dwgu_f0", n1, dgu0, _rs_exchange([sums["b_w_in"], sums["b_w_out"]]))
    dn1, got["gu0"], got["dn0"] = _ffn_dn("dn_f0", dgu0, g_gu0, _rs_swap([share["gu0"], share["dn0"]]))
    dh1, d_ffn0 = _rms_bwd("rmsb_f0", dn1, h1, ffn_norm[0], dh2)
    pair_sums("gu0", "dn0")

    do_views = _matmul_nt_views("dout_a", dh1, w_a_out, dils)
    share["a_w_out"] = _matmul("dwout_a", o_a, dh1, "tn", BF16, TM, 1024, 1024).reshape(N_DEV, 128, D_MODEL)
    pieces = []
    for g, dil, L, view in groups:
        rot = tuple(tb.reshape(L, dil * LANES) for tb in tabs)
        args = (view, do_views[g], o_views[g], lse_views[g], rot, *cols, dil, L)
        dq_g = _dil_dq("dil_dq%d" % g, *args)
        dk_g, dv_g = _dil_dkv("dil_dkv%d" % g, *args)
        pieces += [from_view("d" + tag, a, dil) for tag, a in (("q", dq_g), ("k", dk_g), ("v", dv_g))]
    dproj_a = jnp.concatenate(pieces, axis=1)
    share["a_w_in"], others["gu0"], others["dn0"] = _mm_call(
        "dwin_a", (N_DEV, 1, S // 1024), n0, pl.BlockSpec((1024, D_MODEL), lambda d, i, k: (k, 0)),
        dproj_a, pl.BlockSpec((1024, A_BLK), lambda d, i, k: (k, d)), TN, [jax.ShapeDtypeStruct((N_DEV, D_MODEL, A_BLK), BF16)],
        [pl.BlockSpec((None, D_MODEL, A_BLK), lambda d, i, k: (d, 0, 0))], (D_MODEL, A_BLK),
        comm=_rs_exchange([sums["gu0"], sums["dn0"]]))
    got["a_w_in"], got["a_w_out"] = _comm_call("swap_a", _rs_swap([share["a_w_in"], share["a_w_out"]]))
    pair_sums("a_w_in", "a_w_out")
    dn0, others["a_w_in"], others["a_w_out"] = _mm_call(
        "dn_a", (S // TM, 1, N_DEV), dproj_a, pl.BlockSpec((TM, A_BLK), lambda i, j, k: (i, k)),
        g_ain, pl.BlockSpec((None, D_MODEL, A_BLK), lambda i, j, k: (k, 0, 0)), NT, [jax.ShapeDtypeStruct((S, D_MODEL), F32)],
        [pl.BlockSpec((TM, D_MODEL), lambda i, j, k: (i, 0))], (TM, D_MODEL),
        comm=_rs_exchange([sums["a_w_in"], sums["a_w_out"]]))
    dx, d_anorm = _rms_bwd("rmsb_a", dn0, h0, a_norm[0], dh1)

    misc = jnp.concatenate([d_bf[:, :N_HEADS], loss[:, :1], jnp.zeros((1, D_MODEL - N_HEADS - 1), F32)], axis=1)
    small = jnp.concatenate([d_anorm, d_ffn0, d_ffn1, d_final, d_bnorm, misc, jnp.zeros((2, D_MODEL), F32)], axis=0)
    small_all, = _comm_call("gather_small", _ag_send([small], direct=True))

    outs = {}
    for n in TENSORS:
        mine = lax.dynamic_index_in_dim(sums[n], 2 * xi + yi, axis=0, keepdims=False)
        outs[n] = _adamw("adamw_" + n, [mine] + [others[n][k] for k in range(3)], w_sh[n], m_sh[n], v_sh[n], ROW_TILE[n])

    pad_vec = lambda a: jnp.pad(a, ((0, 0), (0, D_MODEL - a.shape[1])))

    def small_pack(an, fn, fin, bf):
        return jnp.concatenate([an, fn, fin.reshape(1, D_MODEL), jnp.zeros((1, D_MODEL), F32), pad_vec(bf),
                                jnp.zeros((2, D_MODEL), F32)], axis=0)

    sg, sd, sm, sv = _adamw("adamw_small", [small_all[d] for d in range(N_DEV)], small_pack(a_norm, ffn_norm, final_norm, b_f),
                            small_pack(m_a_norm, m_ffn_norm, m_final_norm, m_b_f),
                            small_pack(v_a_norm, v_ffn_norm, v_final_norm, v_b_f), 8)
    g_bn = lax.dynamic_slice(sg[4:5], (0, dev * LANES), (1, LANES))
    bn = _adamw("adamw_b_norm", [g_bn], b_norm, m_b_norm, v_b_norm, 1)

    def tree(i):
        full = lambda name, ref: outs[name][i].reshape(ref.shape)
        sml = (sg, sd, sm, sv)[i]
        return dict(
            a_norm=sml[0:1], a_w_in=full("a_w_in", a_w_in), a_w_out=full("a_w_out", a_w_out), b_norm=bn[i],
            b_w_in=full("b_w_in", b_w_in), b_f=sml[5:6, :N_HEADS], b_w_out=full("b_w_out", b_w_out), ffn_norm=sml[1:3],
            ffn_w_gu=jnp.stack([outs["gu0"][i], outs["gu1"][i]]).reshape(ffn_w_gu.shape),
            ffn_w_down=jnp.stack([outs["dn0"][i], outs["dn1"][i]]).reshape(ffn_w_down.shape), final_norm=sml[3])

    order = ("a_norm", "a_w_in", "a_w_out", "b_norm", "b_w_in", "b_f", "b_w_out", "ffn_norm", "ffn_w_gu", "ffn_w_down", "final_norm")
    result = [sg[5, N_HEADS], dx.reshape(x.shape)]
    for i in range(4):
        t = tree(i)
        result += [t[n] for n in order]
    return tuple(result)
```

```python
import functools
from typing import Callable, NamedTuple

import jax
import jax.numpy as jnp
from jax import lax
from jax.experimental import pallas as pl
from jax.experimental.pallas import tpu as pltpu

F32 = jnp.float32
BF16 = jnp.bfloat16

D_MODEL = 1024
N_HEADS = 16
HEAD_DIM = 64
N_PAIRS = N_HEADS // 2
LANES = 128
DILATED_PATTERNS = ((128, 1), (512, 4), (2048, 16))
BAND_STEPS = 128
ROT_DIM = HEAD_DIM // 4
ROPE_THETA = 500000.0
D_FF = 2816
RMS_EPS = 1e-6
NEG_INF = -1e30
SOFTMAX_SCALE = HEAD_DIM ** -0.5
N_DEV = 8
FF_BLK = 2 * D_FF // N_DEV
ADAM_LR, ADAM_B1, ADAM_B2, ADAM_EPS, ADAM_WD, ADAM_STEP = 0.001, 0.9, 0.999, 1e-08, 0.01, 10
VMEM_LIMIT = 52 * 1024 * 1024
FOX_BWD_VMEM = 60 * 1024 * 1024
TM = 1024
MESH = pl.DeviceIdType.MESH

NN = (((1,), (0,)), ((), ()))
NT = (((1,), (1,)), ((), ()))
TN = (((0,), (0,)), ((), ()))


def _params(*sem):
    return pltpu.CompilerParams(dimension_semantics=sem, vmem_limit_bytes=VMEM_LIMIT)


def _dot(a, b, dims):
    return lax.dot_general(a, b, dims, preferred_element_type=F32)


class _Comm(NamedTuple):
    ins: tuple
    outs: tuple
    aliases: dict
    copies: Callable
    n_remote: int
    n_local: int


def _call(name, body, grid, in_specs, out_specs, out_shape, scratch, args, sem, comm=None):
    if comm is None:
        return pl.pallas_call(body, grid=grid, in_specs=in_specs, out_specs=out_specs, out_shape=out_shape,
                              scratch_shapes=scratch, compiler_params=_params(*sem), name=name)(*args)
    n_in, n_out = len(in_specs), len(out_specs)
    n_ci, n_co = len(comm.ins), len(comm.outs)
    o0 = n_in + n_ci

    def hosted(*refs):
        c_ins, c_outs = refs[n_in:o0], refs[o0 + n_out:o0 + n_out + n_co]
        sems = refs[-3:]

        def start():
            for cp in comm.copies(c_ins, c_outs, *sems):
                cp.start()

        def wait():
            for cp in comm.copies(c_ins, c_outs, *sems):
                cp.wait()

        if not grid:
            start()
            body()
            wait()
            return
        ids = [pl.program_id(ax) for ax in range(len(grid))]
        pl.when(functools.reduce(jnp.logical_and, [i == 0 for i in ids]))(start)
        body(*refs[:n_in], *refs[o0:o0 + n_out], *refs[o0 + n_out + n_co:-3])
        pl.when(functools.reduce(jnp.logical_and, [i == g - 1 for i, g in zip(ids, grid)]))(wait)

    hbm = pl.BlockSpec(memory_space=pltpu.HBM)
    dma = pltpu.SemaphoreType.DMA
    return pl.pallas_call(
        hosted, grid=grid, in_specs=[*in_specs, *[hbm] * n_ci], out_specs=[*out_specs, *[hbm] * n_co],
        out_shape=[*out_shape, *comm.outs], input_output_aliases={n_in + i: n_out + o for i, o in comm.aliases.items()},
        scratch_shapes=[*scratch, dma((comm.n_remote,)), dma((comm.n_remote,)), dma((max(comm.n_local, 1),))],
        compiler_params=_params(*["arbitrary"] * len(grid)), name=name)(*args, *comm.ins)


def _mm_call(name, grid, a, a_spec, b, b_spec, dims, out_shapes, out_specs, acc_shape, epilogue=None,
             extras=(), extra_specs=(), col_axis=1, comm=None):
    nk = grid[2]
    n_extra = len(extras)
    n_out = len(out_shapes)

    def finish(res, ex, outs, j):
        if epilogue is None:
            outs[0][...] = res.astype(outs[0].dtype)
        else:
            epilogue(res, ex, outs, j)

    def body(*refs):
        a_ref, b_ref = refs[0], refs[1]
        ex = refs[2:2 + n_extra]
        outs = refs[2 + n_extra:2 + n_extra + n_out]
        j, k = pl.program_id(col_axis), pl.program_id(2)
        part = _dot(a_ref[...].astype(BF16), b_ref[...].astype(BF16), dims)
        if nk == 1:
            finish(part, ex, outs, j)
            return
        acc = refs[-1]

        @pl.when(k == 0)
        def _():
            acc[...] = part

        @pl.when((k > 0) & (k < nk - 1))
        def _():
            acc[...] += part

        @pl.when(k == nk - 1)
        def _():
            finish(acc[...] + part, ex, outs, j)

    return _call(name, body, grid, [a_spec, b_spec, *extra_specs], out_specs, out_shapes,
                 [] if nk == 1 else [pltpu.VMEM(acc_shape, F32)], (a, b, *extras), ("parallel", "parallel", "arbitrary"), comm)


def _matmul(name, a, b, mode, out_dtype, tm, tn, tk, resid=None, col0_scale=None, comm=None):
    if mode == "nn":
        (M, K), N = a.shape, b.shape[1]
        a_spec = pl.BlockSpec((tm, tk), lambda j, i, k: (i, k))
        b_spec = pl.BlockSpec((tk, tn), lambda j, i, k: (k, j))
        dims = NN
    elif mode == "nt":
        (M, K), N = a.shape, b.shape[0]
        a_spec = pl.BlockSpec((tm, tk), lambda j, i, k: (i, k))
        b_spec = pl.BlockSpec((tn, tk), lambda j, i, k: (j, k))
        dims = NT
    else:
        (K, M), N = a.shape, b.shape[1]
        a_spec = pl.BlockSpec((tk, tm), lambda j, i, k: (k, i))
        b_spec = pl.BlockSpec((tk, tn), lambda j, i, k: (k, j))
        dims = TN
    assert M % tm == 0 and N % tn == 0 and K % tk == 0, (name, M, N, K, tm, tn, tk)
    o_spec = pl.BlockSpec((tm, tn), lambda j, i, k: (i, j))
    extras, extra_specs, epilogue = (), (), None
    if resid is not None:
        extras, extra_specs = (resid,), (o_spec,)

        def epilogue(acc, ex, outs, j):
            outs[0][...] = (acc + ex[0][...]).astype(outs[0].dtype)

    elif col0_scale is not None:

        def epilogue(acc, ex, outs, j):
            outs[0][...] = (acc * jnp.where(j == 0, col0_scale, 1.0)).astype(outs[0].dtype)

    res = _mm_call(name, (N // tn, M // tm, K // tk), a, a_spec, b, b_spec, dims, [jax.ShapeDtypeStruct((M, N), out_dtype)],
                   [o_spec], (tm, tn), epilogue, extras, extra_specs, col_axis=0, comm=comm)
    return res[0] if comm is None else (res[0], res[1:])


def _rms_fwd(name, h, gain, tm=512):
    S, D = h.shape

    def body(h_ref, g_ref, n_ref):
        x = h_ref[...]
        rstd = lax.rsqrt(jnp.mean(x * x, axis=-1, keepdims=True) + RMS_EPS)
        n_ref[...] = (x * rstd * g_ref[...]).astype(BF16)

    return pl.pallas_call(
        body, grid=(S // tm,), in_specs=[pl.BlockSpec((tm, D), lambda i: (i, 0)), pl.BlockSpec((1, D), lambda i: (0, 0))],
        out_specs=pl.BlockSpec((tm, D), lambda i: (i, 0)), out_shape=jax.ShapeDtypeStruct((S, D), BF16),
        compiler_params=_params("parallel"), name=name)(h, gain.reshape(1, D))


def _rms_bwd(name, dn, h, gain, dres, tm=512):
    S, D = h.shape

    def body(dn_ref, h_ref, g_ref, r_ref, dh_ref, dg_ref):
        x = h_ref[...]
        rstd = lax.rsqrt(jnp.mean(x * x, axis=-1, keepdims=True) + RMS_EPS)
        xhat = x * rstd
        d = dn_ref[...]
        dxhat = d * g_ref[...]
        dh_ref[...] = rstd * (dxhat - xhat * jnp.mean(dxhat * xhat, axis=-1, keepdims=True)) + r_ref[...]

        @pl.when(pl.program_id(0) == 0)
        def _():
            dg_ref[...] = jnp.zeros_like(dg_ref)

        dg_ref[...] += jnp.sum(d * xhat, axis=0, keepdims=True)

    row = pl.BlockSpec((tm, D), lambda i: (i, 0))
    vec = pl.BlockSpec((1, D), lambda i: (0, 0))
    return pl.pallas_call(
        body, grid=(S // tm,), in_specs=[row, row, vec, row], out_specs=[row, vec],
        out_shape=[jax.ShapeDtypeStruct((S, D), F32), jax.ShapeDtypeStruct((1, D), F32)],
        compiler_params=_params("arbitrary"), name=name)(dn, h, gain.reshape(1, D), dres)


def _loss_head(name, h, gain, target, tm=512):
    S, D = h.shape

    def body(h_ref, g_ref, t_ref, dh_ref, dg_ref, loss_ref):
        x = h_ref[...]
        rstd = lax.rsqrt(jnp.mean(x * x, axis=-1, keepdims=True) + RMS_EPS)
        xhat = x * rstd
        err = xhat * g_ref[...] - t_ref[...]
        dy = err * (1.0 / D)
        dxhat = dy * g_ref[...]
        dh_ref[...] = rstd * (dxhat - xhat * jnp.mean(dxhat * xhat, axis=-1, keepdims=True))

        @pl.when(pl.program_id(0) == 0)
        def _():
            dg_ref[...] = jnp.zeros_like(dg_ref)
            loss_ref[...] = jnp.zeros_like(loss_ref)

        dg_ref[...] += jnp.sum(dy * xhat, axis=0, keepdims=True)
        part = 0.5 * jnp.sum(jnp.mean(err * err, axis=-1, keepdims=True), axis=0, keepdims=True)
        loss_ref[...] += jnp.broadcast_to(part, loss_ref.shape)

    row = pl.BlockSpec((tm, D), lambda i: (i, 0))
    vec = pl.BlockSpec((1, D), lambda i: (0, 0))
    return pl.pallas_call(
        body, grid=(S // tm,), in_specs=[row, vec, row], out_specs=[row, vec, pl.BlockSpec((1, LANES), lambda i: (0, 0))],
        out_shape=[jax.ShapeDtypeStruct((S, D), F32), jax.ShapeDtypeStruct((1, D), F32),
                   jax.ShapeDtypeStruct((1, LANES), F32)],
        compiler_params=_params("arbitrary"), name=name)(h, gain.reshape(1, D), target)


def _rope_tables(S):
    half = ROT_DIM // 2
    inv_freq = ROPE_THETA ** (-jnp.arange(half, dtype=F32) * 2.0 / ROT_DIM)
    ang = jnp.arange(S, dtype=F32)[:, None] * inv_freq[None, :]
    cos, sin = jnp.cos(ang), jnp.sin(ang)
    one = jnp.ones((S, HEAD_DIM - ROT_DIM), F32)
    zero = jnp.zeros((S, HEAD_DIM - ROT_DIM), F32)
    zh = jnp.zeros((S, half), F32)
    c = jnp.concatenate([cos, cos, one], axis=1)
    sa = jnp.concatenate([-sin, zh, zero], axis=1)
    sb = jnp.concatenate([zh, sin, zero], axis=1)
    return tuple(jnp.concatenate([t, t], axis=1) for t in (c, sa, sb))


def _rotate(x, c, sa, sb, sign):
    return x * c + sign * (pltpu.roll(x, LANES - ROT_DIM // 2, 1) * sa + pltpu.roll(x, ROT_DIM // 2, 1) * sb)


def _stage_chunks(scr, chunks):
    for c, x in enumerate(chunks):
        scr[c] = x


def _strided_rows(scr, c, r, n, R):
    return scr.at[c][pl.ds(r, n, stride=R), :]


def _a_proj(name, n, w, tabs, comm, tm=512):
    S, D = n.shape
    n_i = S // tm
    dils = [dil for _, dil in DILATED_PATTERNS]
    n_out = 3 * len(dils)

    def body(n_ref, w_ref, c_ref, sa_ref, sb_ref, *rest):
        outs, scr = rest[:n_out], rest[n_out]
        j = pl.program_id(0)
        acc = _dot(n_ref[...], w_ref[...], NN)
        c, sa, sb = c_ref[...], sa_ref[...], sb_ref[...]
        for J in range(n_out):
            R, kind = dils[J // 3], J % 3

            @pl.when(j == J)
            def _(J=J, R=R, kind=kind):
                chunks = [acc[:, b * LANES:(b + 1) * LANES] for b in range(N_PAIRS)]
                if kind < 2:
                    chunks = [_rotate(x, c, sa, sb, 1.0) * (SOFTMAX_SCALE if kind == 0 else 1.0) for x in chunks]
                if R == 1:
                    for b, x in enumerate(chunks):
                        outs[J][:, b * LANES:(b + 1) * LANES] = x.astype(BF16)
                    return
                _stage_chunks(scr, chunks)
                for r in range(R):
                    for b in range(N_PAIRS):
                        col = r * D_MODEL + b * LANES
                        outs[J][:, col:col + LANES] = _strided_rows(scr, b, r, tm // R, R).astype(BF16)

    def out_spec(J, R):
        return pl.BlockSpec((tm // R, R * D_MODEL), lambda j, i: (jnp.where(j == J, i, jnp.where(j < J, 0, n_i - 1)), 0))

    tab = pl.BlockSpec((tm, LANES), lambda j, i: (i, 0))
    res = _call(name, body, (n_out, n_i),
                [pl.BlockSpec((tm, D), lambda j, i: (i, 0)), pl.BlockSpec((D, D_MODEL), lambda j, i: (0, j)), tab, tab, tab],
                [out_spec(J, dils[J // 3]) for J in range(n_out)],
                [jax.ShapeDtypeStruct((S // dils[J // 3], dils[J // 3] * D_MODEL), BF16) for J in range(n_out)],
                [pltpu.VMEM((N_PAIRS, tm, LANES), F32)], (n, w, *tabs), ("arbitrary", "arbitrary"), comm)
    return [res[3 * g:3 * g + 3] for g in range(len(dils))], res[n_out:]


def _relayout(name, x, R, to_view, tm=512):
    rows, cols = x.shape
    S = rows if to_view else rows * R

    def body(x_ref, o_ref, scr):
        if to_view:
            _stage_chunks(scr, [x_ref[:, b * LANES:(b + 1) * LANES].astype(F32) for b in range(N_PAIRS)])
            for r in range(R):
                for b in range(N_PAIRS):
                    col = r * D_MODEL + b * LANES
                    o_ref[:, col:col + LANES] = _strided_rows(scr, b, r, tm // R, R).astype(o_ref.dtype)
        else:
            for r in range(R):
                for b in range(N_PAIRS):
                    col = r * D_MODEL + b * LANES
                    scr.at[b][pl.ds(r, tm // R, stride=R), :] = x_ref[:, col:col + LANES].astype(F32)
            for b in range(N_PAIRS):
                o_ref[:, b * LANES:(b + 1) * LANES] = scr[b].astype(o_ref.dtype)

    nat = pl.BlockSpec((tm, D_MODEL), lambda i: (i, 0))
    view = pl.BlockSpec((tm // R, R * D_MODEL), lambda i: (i, 0))
    out_shape = (S // R, R * D_MODEL) if to_view else (S, D_MODEL)
    return pl.pallas_call(
        body, grid=(S // tm,), in_specs=[nat if to_view else view], out_specs=view if to_view else nat,
        out_shape=jax.ShapeDtypeStruct(out_shape, x.dtype), scratch_shapes=[pltpu.VMEM((N_PAIRS, tm, LANES), F32)],
        compiler_params=_params("parallel"), name=name)(x)


def _lo_lanes():
    return lax.broadcasted_iota(jnp.int32, (1, LANES), 1) < HEAD_DIM


def _rep_rows(x2, lo):
    sw = pltpu.roll(x2, HEAD_DIM, 1)
    return jnp.where(lo, x2, sw), jnp.where(lo, sw, x2)


def _pair_cols(h):
    return slice((h // 2) * LANES, (h // 2 + 1) * LANES)


def _head_lanes(lo, h):
    return lo if h % 2 == 0 else jnp.logical_not(lo)


def _band_masks(t, first):
    ri = lax.broadcasted_iota(jnp.int32, (t, t), 0)
    ci = lax.broadcasted_iota(jnp.int32, (t, t), 1)
    neg_prev = jnp.where((ci >= ri) & jnp.logical_not(first), 0.0, NEG_INF)
    neg_cur = jnp.where(ci <= ri, 0.0, NEG_INF)
    return neg_prev, neg_cur


def _dil_specs(L, R, t, qcol, kcol, vcol):
    W = D_MODEL
    prev = lambda qi: jnp.maximum(qi - 1, 0)
    return dict(
        q=pl.BlockSpec((t, W), lambda r, qi: (qi, qcol(r))),
        kp=pl.BlockSpec((t, W), lambda r, qi: (prev(qi), kcol(r))), kc=pl.BlockSpec((t, W), lambda r, qi: (qi, kcol(r))),
        vp=pl.BlockSpec((t, W), lambda r, qi: (prev(qi), vcol(r))), vc=pl.BlockSpec((t, W), lambda r, qi: (qi, vcol(r))),
        own=pl.BlockSpec((t, W), lambda r, qi: (qi, r)), tab=pl.BlockSpec((t, LANES), lambda r, qi: (qi, r)))


def _dil_fwd(name, x, qcol, kcol, vcol, R, L):
    t = BAND_STEPS
    W = D_MODEL
    sp = _dil_specs(L, R, t, qcol, kcol, vcol)

    def body(q_ref, kp_ref, kc_ref, vp_ref, vc_ref, o_ref, lse_ref):
        lo = _lo_lanes()
        neg_p, neg_c = _band_masks(t, pl.program_id(1) == 0)
        s_p, s_c = [], []
        for h in range(N_HEADS):
            cols = _pair_cols(h)
            qh = jnp.where(_head_lanes(lo, h), q_ref[:, cols], 0)
            s_p.append(_dot(qh, kp_ref[:, cols], NT))
            s_c.append(_dot(qh, kc_ref[:, cols], NT))
        s_p = jnp.stack(s_p) + neg_p[None]
        s_c = jnp.stack(s_c) + neg_c[None]
        m = jnp.maximum(jnp.max(s_p, axis=2, keepdims=True), jnp.max(s_c, axis=2, keepdims=True))
        p_p, p_c = jnp.exp(s_p - m), jnp.exp(s_c - m)
        l = jnp.sum(p_p, axis=2, keepdims=True) + jnp.sum(p_c, axis=2, keepdims=True)
        inv, lse = 1.0 / l, m + jnp.log(l)
        p_p, p_c = p_p.astype(BF16), p_c.astype(BF16)
        for p in range(N_PAIRS):
            cols = _pair_cols(2 * p)
            o2 = jnp.zeros((t, LANES), F32)
            for h in (2 * p, 2 * p + 1):
                hm = _head_lanes(lo, h)
                pv = _dot(p_p[h], jnp.where(hm, vp_ref[:, cols], 0), NN) + _dot(p_c[h], jnp.where(hm, vc_ref[:, cols], 0), NN)
                o2 = o2 + pv * inv[h]
            o_ref[:, cols] = o2
            lse_ref[:, cols] = jnp.where(lo, lse[2 * p], lse[2 * p + 1])

    return pl.pallas_call(
        body, grid=(R, L // t), in_specs=[sp["q"], sp["kp"], sp["kc"], sp["vp"], sp["vc"]], out_specs=[sp["own"], sp["own"]],
        out_shape=[jax.ShapeDtypeStruct((L, R * W), F32), jax.ShapeDtypeStruct((L, R * W), F32)],
        compiler_params=_params("parallel", "parallel"), name=name)(x[0], x[1], x[1], x[2], x[2])


def _dil_scores(lo, q_ref, do_ref, o_ref, lse_ref, kv_refs):
    s = [[] for _ in kv_refs]
    dp = [[] for _ in kv_refs]
    lse, d = [], []
    for h in range(N_HEADS):
        cols = _pair_cols(h)
        hm = _head_lanes(lo, h)
        qh, doh = jnp.where(hm, q_ref[:, cols], 0), jnp.where(hm, do_ref[:, cols], 0)
        for i, (k_ref, v_ref) in enumerate(kv_refs):
            s[i].append(_dot(qh, k_ref[:, cols], NT))
            dp[i].append(_dot(doh, v_ref[:, cols], NT))
        lse.append(_rep_rows(lse_ref[:, cols], lo)[h % 2])
        dd = do_ref[:, cols].astype(F32) * o_ref[:, cols].astype(F32)
        d.append(jnp.sum(jnp.where(hm, dd, 0.0), axis=1, keepdims=True))
    return (*[jnp.stack(x) for x in s], *[jnp.stack(x) for x in dp], jnp.stack(lse), jnp.stack(d))


def _dil_bwd(name, x, do, o, lse, tabs, R, L):
    t = BAND_STEPS
    W = D_MODEL
    nq = L // t
    qb = lambda qi: jnp.minimum(qi, nq - 1)
    kb = lambda qi: jnp.maximum(qb(qi) - 1, 0)
    done = lambda qi: jnp.maximum(qi - 1, 0)
    at = lambda f, width: pl.BlockSpec((t, width), lambda r, qi: (f(qi), r))

    def body(q_ref, kp_ref, kc_ref, vp_ref, vc_ref, do_ref, o_ref, lse_ref, cq_ref, saq_ref, sbq_ref, ck_ref, sak_ref, sbk_ref,
             dq_ref, dk_ref, dv_ref, dk_scr, dv_scr):
        qi = pl.program_id(1)
        lo = _lo_lanes()
        rot_k = lambda x: _rotate(x, ck_ref[...], sak_ref[...], sbk_ref[...], -1.0).astype(BF16)

        @pl.when(qi == 0)
        def _():
            dk_scr[...] = jnp.zeros_like(dk_scr)
            dv_scr[...] = jnp.zeros_like(dv_scr)

        @pl.when(qi < nq)
        def _():
            neg_p, neg_c = _band_masks(t, qi == 0)
            s_p, s_c, dp_p, dp_c, lse_h, d = _dil_scores(lo, q_ref, do_ref, o_ref, lse_ref, ((kp_ref, vp_ref), (kc_ref, vc_ref)))
            p_p, p_c = jnp.exp(s_p + neg_p[None] - lse_h), jnp.exp(s_c + neg_c[None] - lse_h)
            ds_p, ds_c = (p_p * (dp_p - d)).astype(BF16), (p_c * (dp_c - d)).astype(BF16)
            p_p, p_c = p_p.astype(BF16), p_c.astype(BF16)
            for p in range(N_PAIRS):
                cols = _pair_cols(2 * p)
                dq2 = jnp.zeros((t, LANES), F32)
                dk_prev, dv_prev = dk_scr[:, cols], dv_scr[:, cols]
                dk_cur, dv_cur = jnp.zeros((t, LANES), F32), jnp.zeros((t, LANES), F32)
                for h in (2 * p, 2 * p + 1):
                    hm = _head_lanes(lo, h)
                    qh, doh = jnp.where(hm, q_ref[:, cols], 0), jnp.where(hm, do_ref[:, cols], 0)
                    dq2 = dq2 + _dot(ds_p[h], jnp.where(hm, kp_ref[:, cols], 0), NN) + _dot(ds_c[h], jnp.where(hm, kc_ref[:, cols], 0), NN)
                    dk_prev, dv_prev = dk_prev + _dot(ds_p[h], qh, TN), dv_prev + _dot(p_p[h], doh, TN)
                    dk_cur, dv_cur = dk_cur + _dot(ds_c[h], qh, TN), dv_cur + _dot(p_c[h], doh, TN)
                dq_ref[:, cols] = _rotate(dq2 * SOFTMAX_SCALE, cq_ref[...], saq_ref[...], sbq_ref[...], -1.0).astype(BF16)
                dk_ref[:, cols] = rot_k(dk_prev)
                dv_ref[:, cols] = dv_prev.astype(BF16)
                dk_scr[:, cols] = dk_cur
                dv_scr[:, cols] = dv_cur

        @pl.when(qi == nq)
        def _():
            for p in range(N_PAIRS):
                cols = _pair_cols(2 * p)
                dk_ref[:, cols] = rot_k(dk_scr[:, cols])
            dv_ref[...] = dv_scr[...].astype(BF16)

    wide = jax.ShapeDtypeStruct((L, R * W), BF16)
    return pl.pallas_call(
        body, grid=(R, nq + 1),
        in_specs=[at(qb, W), at(kb, W), at(qb, W), at(kb, W), at(qb, W), at(qb, W), at(qb, W), at(qb, W),
                  at(qb, LANES), at(qb, LANES), at(qb, LANES), at(done, LANES), at(done, LANES), at(done, LANES)],
        out_specs=[at(qb, W), at(done, W), at(done, W)], out_shape=[wide, wide, wide],
        scratch_shapes=[pltpu.VMEM((t, W), F32), pltpu.VMEM((t, W), F32)],
        compiler_params=_params("parallel", "arbitrary"), name=name)(x[0], x[1], x[1], x[2], x[2], do, o, lse, *tabs, *tabs)


def _fox_operands(q2, k2, kb2, lo, hh):
    lane = lax.broadcasted_iota(jnp.int32, (1, LANES), 1)
    if hh == 0:
        ones = ((lane >= HEAD_DIM) & (lane < HEAD_DIM + 3)).astype(BF16)
        return jnp.where(lo, q2, ones), jnp.where(lo, k2, kb2)
    ones = (lane < 3).astype(BF16)
    return jnp.where(lo, ones, q2), jnp.where(lo, kb2, k2)


def _causal_neg(t):
    ri = lax.broadcasted_iota(jnp.int32, (t, t), 0)
    ci = lax.broadcasted_iota(jnp.int32, (t, t), 1)
    return jnp.where(ci <= ri, 0.0, NEG_INF)


def _fox_fwd(name, qkv, kbias, t):
    S = qkv.shape[0]
    W = D_MODEL
    nq = S // t
    rep = t // LANES

    def body(q_ref, k_ref, v_ref, kb_ref, o_ref, lse_ref, m_scr, l_scr, acc_scr):
        qi, j = pl.program_id(0), pl.program_id(1)
        lo = _lo_lanes()

        @pl.when(j == 0)
        def _():
            m_scr[...] = jnp.full_like(m_scr, NEG_INF)
            l_scr[...] = jnp.zeros_like(l_scr)
            acc_scr[...] = jnp.zeros_like(acc_scr)

        def step(masked):
            neg = _causal_neg(t) if masked else None

            def pair(p, carry):
                cs = pl.ds(pl.multiple_of(p * LANES, LANES), LANES)
                q2, k2, v2, kb2 = q_ref[:, cs], k_ref[:, cs], v_ref[:, cs], kb_ref[:, cs]
                pvs, alphas = [], []
                for hh in range(2):
                    hm = lo if hh == 0 else jnp.logical_not(lo)
                    qh, kh = _fox_operands(q2, k2, kb2, lo, hh)
                    s = _dot(qh, kh, NT)
                    if masked:
                        s = s + neg
                    h = 2 * p + hh
                    m_prev = m_scr[h]
                    m_new = jnp.maximum(m_prev, jnp.max(s, axis=1, keepdims=True))
                    pe = jnp.exp(s - jnp.tile(m_new, (1, rep)))
                    alpha = jnp.exp(m_prev - m_new)
                    l_scr[h] = alpha * l_scr[h] + jnp.sum(pe, axis=1, keepdims=True)
                    m_scr[h] = m_new
                    pvs.append(_dot(pe.astype(BF16), jnp.where(hm, v2, 0), NN))
                    alphas.append(alpha)
                acc_scr[:, cs] = acc_scr[:, cs] * jnp.where(lo, alphas[0], alphas[1]) + pvs[0] + pvs[1]
                return carry

            lax.fori_loop(0, N_PAIRS, pair, 0)

        @pl.when(j < qi)
        def _():
            step(False)

        @pl.when(j == qi)
        def _():
            step(True)

        @pl.when(j == nq - 1)
        def _():
            for p in range(N_PAIRS):
                cols = slice(p * LANES, (p + 1) * LANES)
                l2 = jnp.where(lo, l_scr[2 * p], l_scr[2 * p + 1])
                m2 = jnp.where(lo, m_scr[2 * p], m_scr[2 * p + 1])
                o_ref[:, cols] = (acc_scr[:, cols] / l2).astype(BF16)
                lse_ref[:, cols] = m2 + jnp.log(l2)

    kv = lambda col: pl.BlockSpec((t, W), lambda qi, j: (jnp.minimum(j, qi), col))
    own = pl.BlockSpec((t, W), lambda qi, j: (qi, 0))
    return pl.pallas_call(
        body, grid=(nq, nq), in_specs=[own, kv(1), kv(2), kv(0)], out_specs=[own, own],
        out_shape=[jax.ShapeDtypeStruct((S, W), BF16), jax.ShapeDtypeStruct((S, W), F32)],
        scratch_shapes=[pltpu.VMEM((N_HEADS, t, LANES), F32), pltpu.VMEM((N_HEADS, t, LANES), F32), pltpu.VMEM((t, W), F32)],
        compiler_params=_params("parallel", "arbitrary"), name=name)(qkv, qkv, qkv, kbias)


def _fox_head_grads(qh, kh, v2, doh, neg, lse_h, d_h, rep):
    s = _dot(qh, kh, NT)
    if neg is not None:
        s = s + neg
    p = jnp.exp(s - jnp.tile(lse_h, (1, rep)))
    return p, p * (_dot(doh, v2, NT) - d_h)


def _fox_bwd(name, qkv, kbias, do, o, lse, t):
    S = qkv.shape[0]
    W = D_MODEL
    nq = S // t
    rep = t // LANES

    def body(q_ref, k_ref, v_ref, kb_ref, do_ref, o_ref, lse_ref, dq_ref, dk_ref, dv_ref, rs_ref, dc_ref, dq_scr, dk_scr, dv_scr):
        kb, j = pl.program_id(0), pl.program_id(1)
        lo = _lo_lanes()
        lane = lax.broadcasted_iota(jnp.int32, (1, LANES), 1)
        rows = pl.ds(pl.multiple_of(j * t, t), t)

        @pl.when((kb == 0) & (j == 0))
        def _():
            dq_scr[...] = jnp.zeros_like(dq_scr)
            rs_ref[...] = jnp.zeros_like(rs_ref)

        @pl.when(j == 0)
        def _():
            dk_scr[...] = jnp.zeros_like(dk_scr)
            dv_scr[...] = jnp.zeros_like(dv_scr)
            dc_ref[...] = jnp.zeros_like(dc_ref)

        def step(masked):
            neg = _causal_neg(t) if masked else None

            def pair(p, carry):
                cs = pl.ds(pl.multiple_of(p * LANES, LANES), LANES)
                q2, k2, v2, kb2, do2 = q_ref[:, cs], k_ref[:, cs], v_ref[:, cs], kb_ref[:, cs], do_ref[:, cs]
                dd = do2.astype(F32) * o_ref[:, cs].astype(F32)
                lse_h = _rep_rows(lse_ref[:, cs], lo)
                dq2 = jnp.zeros((t, LANES), F32)
                dv2 = jnp.zeros((t, LANES), F32)
                dk2 = jnp.zeros((t, LANES), F32)
                for hh in range(2):
                    hm = lo if hh == 0 else jnp.logical_not(lo)
                    qh, kh = _fox_operands(q2, k2, kb2, lo, hh)
                    doh = jnp.where(hm, do2, 0)
                    d_h = jnp.sum(jnp.where(hm, dd, 0.0), axis=1, keepdims=True)
                    pr, ds = _fox_head_grads(qh, kh, v2, doh, neg, lse_h[hh], d_h, rep)
                    rs_ref[rows, :] += jnp.where(lane == 2 * p + hh, jnp.sum(ds, axis=1, keepdims=True), 0.0)
                    dc_ref[p, hh:hh + 1, :] += jnp.sum(ds, axis=0, keepdims=True)
                    dsb = ds.astype(BF16)
                    dv2 = dv2 + _dot(pr.astype(BF16), doh, TN)
                    dk2 = dk2 + _dot(dsb, jnp.where(hm, q2, 0), TN)
                    dq2 = dq2 + _dot(dsb, jnp.where(hm, k2, 0), NN)
                dv_scr[:, cs] += dv2
                dk_scr[:, cs] += dk2
                dq_scr[rows, cs] += dq2
                return carry

            lax.fori_loop(0, N_PAIRS, pair, 0)
            if masked:
                dq_ref[...] = (dq_scr[rows, :] * SOFTMAX_SCALE).astype(BF16)

        @pl.when(j > kb)
        def _():
            step(False)

        @pl.when(j == kb)
        def _():
            step(True)

        @pl.when(j == nq - 1)
        def _():
            dv_ref[...] = dv_scr[...].astype(BF16)
            dk_ref[...] = dk_scr[...].astype(BF16)

    qrow = pl.BlockSpec((t, W), lambda kb, j: (jnp.maximum(j, kb), 0))
    krow = lambda col: pl.BlockSpec((t, W), lambda kb, j: (kb, col))
    own = pl.BlockSpec((t, W), lambda kb, j: (kb, 0))
    wide = jax.ShapeDtypeStruct((S, W), BF16)
    return pl.pallas_call(
        body, grid=(nq, nq), in_specs=[qrow, krow(1), krow(2), krow(0), qrow, qrow, qrow],
        out_specs=[own, own, own, pl.BlockSpec((S, LANES), lambda kb, j: (0, 0)), pl.BlockSpec((N_PAIRS, 2, t), lambda kb, j: (0, 0, kb))],
        out_shape=[wide, wide, wide, jax.ShapeDtypeStruct((S, LANES), F32), jax.ShapeDtypeStruct((N_PAIRS, 2, S), F32)],
        scratch_shapes=[pltpu.VMEM((S, W), F32), pltpu.VMEM((t, W), F32), pltpu.VMEM((t, W), F32)],
        compiler_params=pltpu.CompilerParams(dimension_semantics=("arbitrary", "arbitrary"), vmem_limit_bytes=FOX_BWD_VMEM),
        name=name)(qkv, qkv, qkv, kbias, do, o, lse)


def _view_spec(tm, R, index=lambda i: (i, 0)):
    return pl.BlockSpec((tm // R, R * D_MODEL), index)


def _matmul_nt_views(name, a, w, dils, tm=512):
    S, K = a.shape

    def body(a_ref, w_ref, *rest):
        res = _dot(a_ref[...].astype(BF16), w_ref[...], NT)
        _write_views([res[:, b * LANES:(b + 1) * LANES] for b in range(N_PAIRS)], rest[-1], rest[:-1], dils, tm)

    return pl.pallas_call(
        body, grid=(S // tm,), in_specs=[pl.BlockSpec((tm, K), lambda i: (i, 0)), pl.BlockSpec((D_MODEL, K), lambda i: (0, 0))],
        out_specs=[_view_spec(tm, R) for R in dils],
        out_shape=[jax.ShapeDtypeStruct((S // R, R * D_MODEL), BF16) for R in dils],
        scratch_shapes=[pltpu.VMEM((N_PAIRS, tm, LANES), F32)], compiler_params=_params("parallel"), name=name)(a, w)


def _write_views(chunks, scr, out_refs, dils, tm):
    if any(R > 1 for R in dils):
        _stage_chunks(scr, chunks)
    for ref, R in zip(out_refs, dils):
        for b, x in enumerate(chunks):
            if R == 1:
                ref[:, b * LANES:(b + 1) * LANES] = x.astype(ref.dtype)
                continue
            for r in range(R):
                col = r * D_MODEL + b * LANES
                ref[:, col:col + LANES] = _strided_rows(scr, b, r, tm // R, R).astype(ref.dtype)


def _combine(name, os_, lses, dils, tm=256):
    S = os_[0].shape[0] * dils[0]
    G = len(dils)

    def body(*refs):
        o_refs, l_refs = refs[:G], refs[G:2 * G]
        o_outs, l_outs = refs[2 * G:3 * G], refs[3 * G:4 * G]
        stage = refs[4 * G:]
        for g, R in enumerate(dils):
            if R == 1:
                continue
            for src, dst in ((o_refs[g], stage[2 * g]), (l_refs[g], stage[2 * g + 1])):
                for r in range(R):
                    for b in range(N_PAIRS):
                        col = r * D_MODEL + b * LANES
                        dst.at[b][pl.ds(r, tm // R, stride=R), :] = src[:, col:col + LANES]
        o_chunks, l_chunks = [], []
        for b in range(N_PAIRS):
            cols = slice(b * LANES, (b + 1) * LANES)
            os_b = [o_refs[g][:, cols] if R == 1 else stage[2 * g][b] for g, R in enumerate(dils)]
            ls = [l_refs[g][:, cols] if R == 1 else stage[2 * g + 1][b] for g, R in enumerate(dils)]
            m = functools.reduce(jnp.maximum, ls)
            ws = [jnp.exp(l - m) for l in ls]
            den = functools.reduce(jnp.add, ws)
            o_chunks.append(functools.reduce(jnp.add, [w * o for w, o in zip(ws, os_b)]) / den)
            l_chunks.append(m + jnp.log(den))
        _write_views(o_chunks, stage[0], o_outs, dils, tm)
        _write_views(l_chunks, stage[1], l_outs, dils, tm)

    specs = [_view_spec(tm, R) for R in dils]
    shapes = lambda dt: [jax.ShapeDtypeStruct((S // R, R * D_MODEL), dt) for R in dils]
    res = pl.pallas_call(
        body, grid=(S // tm,), in_specs=specs * 2, out_specs=specs * 2, out_shape=shapes(BF16) + shapes(F32),
        scratch_shapes=[pltpu.VMEM((N_PAIRS, tm, LANES), F32)] * (2 * G), compiler_params=_params("parallel"),
        name=name)(*os_, *lses)
    return res[:G], res[G:]


def _tri_matmul(tri, x):
    hi, mid, lo = _split3(x)
    return _dot(tri, hi, NN) + _dot(tri, mid, NN) + _dot(tri, lo, NN)


def _split3(x):
    hi = x.astype(BF16)
    r1 = x - hi.astype(F32)
    mid = r1.astype(BF16)
    return hi, mid, (r1 - mid.astype(F32)).astype(BF16)


def _gate_fwd(name, z, bf, tb=512):
    S = z.shape[0]

    def body(z_ref, b_ref, kb_ref, carry):
        @pl.when(pl.program_id(0) == 0)
        def _():
            carry[...] = jnp.zeros_like(carry)

        lf = jax.nn.log_sigmoid(z_ref[...] + b_ref[...])
        ri = lax.broadcasted_iota(jnp.int32, (tb, tb), 0)
        ci = lax.broadcasted_iota(jnp.int32, (tb, tb), 1)
        tri = (ci <= ri).astype(BF16)
        c = _tri_matmul(tri, lf) + carry[...]
        carry[...] = c[tb - 1:tb, :]
        head = lax.broadcasted_iota(jnp.int32, (LANES, D_MODEL), 0)
        col = lax.broadcasted_iota(jnp.int32, (LANES, D_MODEL), 1)
        base = (head >> 1) * LANES + jnp.where((head & 1) == 0, HEAD_DIM, 0)
        kb = jnp.zeros((tb, D_MODEL), F32)
        for i, piece in enumerate(_split3(-c)):
            place = ((col == base + i) & (head < N_HEADS)).astype(BF16)
            kb = kb + _dot(piece, place, NN)
        kb_ref[...] = kb.astype(BF16)

    row = pl.BlockSpec((tb, LANES), lambda i: (i, 0))
    return pl.pallas_call(
        body, grid=(S // tb,), in_specs=[row, pl.BlockSpec((1, LANES), lambda i: (0, 0))],
        out_specs=pl.BlockSpec((tb, D_MODEL), lambda i: (i, 0)), out_shape=jax.ShapeDtypeStruct((S, D_MODEL), BF16),
        scratch_shapes=[pltpu.VMEM((1, LANES), F32)], compiler_params=_params("arbitrary"), name=name)(z, bf)


def _gate_bwd(name, dc, z, bf, tb=512):
    S = z.shape[0]
    nb = S // tb

    def body(dc_ref, z_ref, b_ref, dz_ref, db_ref, carry):
        @pl.when(pl.program_id(0) == 0)
        def _():
            carry[...] = jnp.zeros_like(carry)
            db_ref[...] = jnp.zeros_like(db_ref)

        ri = lax.broadcasted_iota(jnp.int32, (tb, tb), 0)
        ci = lax.broadcasted_iota(jnp.int32, (tb, tb), 1)
        tri = (ci >= ri).astype(BF16)
        dlf = _tri_matmul(tri, dc_ref[...]) + carry[...]
        carry[...] = dlf[0:1, :]
        dz = dlf * jax.nn.sigmoid(-(z_ref[...] + b_ref[...]))
        dz_ref[...] = dz
        db_ref[...] += jnp.sum(dz, axis=0, keepdims=True)

    row = pl.BlockSpec((tb, LANES), lambda i: (nb - 1 - i, 0))
    vec = pl.BlockSpec((1, LANES), lambda i: (0, 0))
    return pl.pallas_call(
        body, grid=(nb,), in_specs=[row, row, vec], out_specs=[row, vec],
        out_shape=[jax.ShapeDtypeStruct((S, LANES), F32), jax.ShapeDtypeStruct((1, LANES), F32)],
        scratch_shapes=[pltpu.VMEM((1, LANES), F32)], compiler_params=_params("arbitrary"), name=name)(dc, z, bf)


def _ffn_gu(name, n, wgu, comm=None, tm=1024):
    S, D = n.shape
    nb = N_DEV // 2

    def body(n_ref, wg_ref, wu_ref, gu_ref, act_ref):
        x = n_ref[...]
        g = _dot(x, wg_ref[...], NN)
        u = _dot(x, wu_ref[...], NN)
        gu_ref[0] = g.astype(BF16)
        gu_ref[1] = u.astype(BF16)
        act_ref[...] = (g * jax.nn.sigmoid(g) * u).astype(BF16)

    return _call(
        name, body, (nb, S // tm),
        [pl.BlockSpec((tm, D), lambda j, i: (i, 0)), pl.BlockSpec((None, D, FF_BLK), lambda j, i: (j, 0, 0)),
         pl.BlockSpec((None, D, FF_BLK), lambda j, i: (j + nb, 0, 0))],
        [pl.BlockSpec((2, None, tm, FF_BLK), lambda j, i: (0, j, i, 0)), pl.BlockSpec((None, tm, FF_BLK), lambda j, i: (j, i, 0))],
        [jax.ShapeDtypeStruct((2, nb, S, FF_BLK), BF16), jax.ShapeDtypeStruct((nb, S, FF_BLK), BF16)], [],
        (n, wgu, wgu), ("parallel", "parallel"), comm)


def _ffn_down(name, act, wd, resid, comm=None, tm=1024):
    nb, S, _ = act.shape
    D = wd.shape[1]

    def epilogue(acc, ex, outs, j):
        outs[0][...] = acc + ex[0][...]

    o_spec = pl.BlockSpec((tm, D), lambda i, j, k: (i, 0))
    return _mm_call(name, (S // tm, 1, nb), act, pl.BlockSpec((None, tm, FF_BLK), lambda i, j, k: (k, i, 0)),
                    wd, pl.BlockSpec((FF_BLK, D), lambda i, j, k: (k, 0)), NN,
                    [jax.ShapeDtypeStruct((S, D), F32)], [o_spec], (tm, D), epilogue, (resid,), (o_spec,), comm=comm)


def _ffn_dact(name, dh, wd, gu, comm=None, tm=512):
    S, D = dh.shape
    nb = N_DEV // 2

    def epilogue(acc, ex, outs, j):
        g = ex[0][0].astype(F32)
        u = ex[0][1].astype(F32)
        sig = jax.nn.sigmoid(g)
        outs[0][0] = (acc * u * (sig * (1.0 + g * (1.0 - sig)))).astype(BF16)
        outs[0][1] = (acc * (g * sig)).astype(BF16)

    gu_spec = pl.BlockSpec((2, None, tm, FF_BLK), lambda j, i, k: (0, j, i, 0))
    return _mm_call(name, (nb, S // tm, 1), dh, pl.BlockSpec((tm, D), lambda j, i, k: (i, 0)),
                    wd, pl.BlockSpec((FF_BLK, D), lambda j, i, k: (j, 0)), NT,
                    [jax.ShapeDtypeStruct((2, nb, S, FF_BLK), BF16)], [gu_spec], (tm, FF_BLK), epilogue, (gu,), (gu_spec,),
                    col_axis=0, comm=comm)


def _ffn_dwgu(name, n, dgu, comm=None, tm=1024, tk=1024):
    S, D = n.shape
    dgu8 = dgu.reshape(N_DEV, S, FF_BLK)
    return _mm_call(name, (N_DEV, D // tm, S // tk), n, pl.BlockSpec((tk, tm), lambda d, i, k: (k, i)),
                    dgu8, pl.BlockSpec((None, tk, FF_BLK), lambda d, i, k: (d, k, 0)), TN,
                    [jax.ShapeDtypeStruct((N_DEV, D, FF_BLK), BF16)],
                    [pl.BlockSpec((None, tm, FF_BLK), lambda d, i, k: (d, i, 0))], (tm, FF_BLK), comm=comm)


def _ffn_dwd(name, act, dh, tk=1024):
    nb, S, _ = act.shape
    D = dh.shape[1]
    out = _mm_call(name, (nb, 1, S // tk), act, pl.BlockSpec((None, tk, FF_BLK), lambda b, j, k: (b, k, 0)),
                   dh, pl.BlockSpec((tk, D), lambda b, j, k: (k, 0)), TN,
                   [jax.ShapeDtypeStruct((nb, FF_BLK, D), BF16)],
                   [pl.BlockSpec((None, FF_BLK, D), lambda b, j, k: (b, 0, 0))], (FF_BLK, D))[0]
    return out.reshape(N_DEV, FF_BLK // 2, D)


def _ffn_dn(name, dgu, wgu, comm=None, tm=1024):
    S = dgu.shape[2]
    D = wgu.shape[1]
    dgu8 = dgu.reshape(N_DEV, S, FF_BLK)
    return _mm_call(name, (S // tm, 1, N_DEV), dgu8, pl.BlockSpec((None, tm, FF_BLK), lambda i, j, k: (k, i, 0)),
                    wgu, pl.BlockSpec((None, D, FF_BLK), lambda i, j, k: (k, 0, 0)), NT,
                    [jax.ShapeDtypeStruct((S, D), F32)], [pl.BlockSpec((tm, D), lambda i, j, k: (i, 0))], (tm, D), comm=comm)


def _adamw(name, parts, w, m, v, tr):
    rows, cols = w.shape
    n_parts = len(parts)
    c1 = 1.0 - ADAM_B1 ** ADAM_STEP
    c2 = 1.0 - ADAM_B2 ** ADAM_STEP

    def body(*refs):
        p_refs = refs[:n_parts]
        w_ref, m_ref, v_ref, g_ref, d_ref, nm_ref, nv_ref = refs[n_parts:]
        g = p_refs[0][...].astype(F32)
        for r in p_refs[1:]:
            g = g + r[...].astype(F32)
        mm = ADAM_B1 * m_ref[...] + (1.0 - ADAM_B1) * g
        vv = ADAM_B2 * v_ref[...] + (1.0 - ADAM_B2) * (g * g)
        g_ref[...] = g
        nm_ref[...] = mm
        nv_ref[...] = vv
        d_ref[...] = -ADAM_LR * ((mm / c1) / (jnp.sqrt(vv / c2) + ADAM_EPS) + ADAM_WD * w_ref[...])

    blk = pl.BlockSpec((tr, cols), lambda i: (i, 0))
    out = jax.ShapeDtypeStruct((rows, cols), F32)
    return pl.pallas_call(
        body, grid=(rows // tr,), in_specs=[blk] * (n_parts + 3), out_specs=[blk] * 4, out_shape=[out] * 4,
        compiler_params=_params("parallel"), name=name)(*parts, w, m, v)


def _position():
    return lax.axis_index("x"), lax.axis_index("y"), lax.axis_index("c")


def _other_chips():
    x, y, _ = _position()
    return [(1 - x, y), (x, 1 - y), (1 - x, 1 - y)]


def _remote(src, dst, send, recv, k, to):
    return pltpu.make_async_remote_copy(src_ref=src, dst_ref=dst, send_sem=send.at[k], recv_sem=recv.at[k],
                                        device_id=to, device_id_type=MESH)


def _ag_send(blocks, direct=False):
    n_peer = 7 if direct else 4

    def copies(ins, outs, send, recv, local, r0=0, l0=0):
        x, y, c = _position()
        me = 4 * x + 2 * y + c
        peers = [(x, y, 1 - c)] + [(px, py, c) for px, py in _other_chips()]
        if direct:
            peers += [(px, py, 1 - c) for px, py in _other_chips()]
        cps = []
        for t, (src, dst) in enumerate(zip(ins, outs)):
            cps.append(pltpu.make_async_copy(src, dst.at[me], local.at[l0 + t]))
            cps += [_remote(src, dst.at[me], send, recv, r0 + n_peer * t + k, to) for k, to in enumerate(peers)]
        return cps

    outs = tuple(jax.ShapeDtypeStruct((N_DEV,) + b.shape, b.dtype) for b in blocks)
    return _Comm(tuple(blocks), outs, {}, copies, n_peer * len(blocks), len(blocks))


def _ag_forward(bufs):
    def copies(ins, outs, send, recv, local, r0=0, l0=0):
        x, y, c = _position()
        cps = []
        for t, buf in enumerate(outs):
            for k, (px, py) in enumerate(_other_chips()):
                slot = buf.at[4 * px + 2 * py + c]
                cps.append(_remote(slot, slot, send, recv, r0 + 3 * t + k, (x, y, 1 - c)))
        return cps

    outs = tuple(jax.ShapeDtypeStruct(b.shape, b.dtype) for b in bufs)
    return _Comm(tuple(bufs), outs, {t: t for t in range(len(bufs))}, copies, 3 * len(bufs), 0)


def _rs_swap(shares):
    def copies(ins, outs, send, recv, local, r0=0, l0=0):
        x, y, c = _position()
        return [_remote(src.at[:, 1 - c], dst, send, recv, r0 + t, (x, y, 1 - c)) for t, (src, dst) in enumerate(zip(ins, outs))]

    ins = tuple(s.reshape((4, 2) + s.shape[1:]) for s in shares)
    outs = tuple(jax.ShapeDtypeStruct((4,) + s.shape[1:], s.dtype) for s in shares)
    return _Comm(ins, outs, {}, copies, len(shares), 0)


def _rs_exchange(sums):
    def copies(ins, outs, send, recv, local, r0=0, l0=0):
        _, _, c = _position()
        return [_remote(src.at[2 * px + py], dst.at[k], send, recv, r0 + 3 * t + k, (px, py, c))
                for t, (src, dst) in enumerate(zip(ins, outs)) for k, (px, py) in enumerate(_other_chips())]

    outs = tuple(jax.ShapeDtypeStruct((3,) + s.shape[1:], s.dtype) for s in sums)
    return _Comm(tuple(sums), outs, {}, copies, 3 * len(sums), 0)


def _comm_call(name, comm):
    return _call(name, lambda: None, (), [], [], [], [], (), (), comm)


def _pair_sum(name, share, got, core, tr):
    _, rows, cols = share.shape

    def body(c_ref, a_ref, b_ref, o_ref):
        o_ref[...] = (a_ref[...].astype(F32) + b_ref[...].astype(F32)).astype(o_ref.dtype)

    grid_spec = pltpu.PrefetchScalarGridSpec(
        num_scalar_prefetch=1, grid=(4, rows // tr),
        in_specs=[pl.BlockSpec((None, None, tr, cols), lambda q, i, c: (q, c[0], i, 0)),
                  pl.BlockSpec((None, tr, cols), lambda q, i, c: (q, i, 0))],
        out_specs=pl.BlockSpec((None, tr, cols), lambda q, i, c: (q, i, 0)))
    return pl.pallas_call(
        body, grid_spec=grid_spec, out_shape=jax.ShapeDtypeStruct((4, rows, cols), share.dtype),
        compiler_params=_params("parallel", "parallel"), name=name)(core, share.reshape(4, 2, rows, cols), got)


TENSORS = ("a_w_in", "a_w_out", "b_w_in", "b_w_out", "gu0", "gu1", "dn0", "dn1")
ROW_TILE = {"a_w_in": 256, "a_w_out": 128, "b_w_in": 256, "b_w_out": 128, "gu0": 256, "gu1": 256, "dn0": 176, "dn1": 176}
A_BLK = 9 * D_MODEL // N_DEV
B_BLK = 386
B_IN = 3 * D_MODEL + N_HEADS
B_IN_PAD = 3 * D_MODEL + LANES


def kernel(x, a_norm, a_w_in, a_w_out, b_norm, b_w_in, b_f, b_w_out, ffn_norm, ffn_w_gu, ffn_w_down, final_norm, loss_target, m_a_norm, m_a_w_in, m_a_w_out, m_b_norm, m_b_w_in, m_b_f, m_b_w_out, m_ffn_norm, m_ffn_w_gu, m_ffn_w_down, m_final_norm, v_a_norm, v_a_w_in, v_a_w_out, v_b_norm, v_b_w_in, v_b_f, v_b_w_out, v_ffn_norm, v_ffn_w_gu, v_ffn_w_down, v_final_norm):
    S = x.shape[1]
    xi, yi, ci = _position()
    dev = 4 * xi + 2 * yi + ci
    core = ci.reshape(1).astype(jnp.int32)
    h0, target = x.reshape(S, D_MODEL), loss_target.reshape(S, D_MODEL)

    def shards(a_in, a_out, b_in, b_out, gu, dn):
        return {"a_w_in": a_in[0], "a_w_out": a_out[0], "b_w_in": b_in[0], "b_w_out": b_out[0],
                "gu0": gu[0], "gu1": gu[1], "dn0": dn[0], "dn1": dn[1]}

    w_sh = shards(a_w_in, a_w_out, b_w_in, b_w_out, ffn_w_gu, ffn_w_down)
    m_sh = shards(m_a_w_in, m_a_w_out, m_b_w_in, m_b_w_out, m_ffn_w_gu, m_ffn_w_down)
    v_sh = shards(v_a_w_in, v_a_w_out, v_b_w_in, v_b_w_out, v_ffn_w_gu, v_ffn_w_down)
    wb = {n: w_sh[n].astype(BF16) for n in TENSORS}
    bf_pad = jnp.pad(b_f, ((0, 0), (0, LANES - N_HEADS)))
    tabs = _rope_tables(S)

    g_ain, g_aout = _comm_call("gather_a", _ag_send([wb["a_w_in"], wb["a_w_out"]]))
    g_ain, g_aout = _comm_call("forward_a", _ag_forward([g_ain, g_aout]))
    n0 = _rms_fwd("rms_a", h0, a_norm[0])
    later = [wb["b_w_in"], wb["b_w_out"], wb["gu0"], wb["dn0"], jnp.pad(b_norm, ((0, 7), (0, 0)))]
    w_a_in = g_ain.transpose(1, 0, 2).reshape(D_MODEL, 9 * D_MODEL)
    qkv_a, later = _a_proj("proj_a", n0, w_a_in, tabs, _ag_send(later))
    cols = [lambda r: r] * 3
    groups = [(g, dil, S // dil, qkv_a[g]) for g, (window, dil) in enumerate(DILATED_PATTERNS)]
    dils = [dil for _, dil in DILATED_PATTERNS]
    from_view = lambda tag, a, dil: a if dil == 1 else _relayout("unview%d_%s" % (dil, tag), a, dil, False)
    fwd = [_dil_fwd("dil_fwd%d" % g, view, *cols, dil, L) for g, dil, L, view in groups]
    o_views, lse_views = _combine("dil_combine", [f[0] for f in fwd], [f[1] for f in fwd], dils)
    o_a = o_views[0]
    w_a_out = g_aout.reshape(D_MODEL, D_MODEL)
    h1, (g_bin, g_bout, g_gu0, g_dn0, g_bnorm) = _matmul("out_a", o_a, w_a_out, "nn", F32, TM, 1024, 1024, resid=h0,
                                                         comm=_ag_forward(later))

    n1 = _rms_fwd("rms_f0", h1, ffn_norm[0])
    gu0, act0, g_gu1 = _ffn_gu("gu_f0", n1, g_gu0, _ag_send([wb["gu1"]]))
    w_dn0 = g_dn0.reshape(D_FF, D_MODEL)
    h2, g_dn1 = _ffn_down("down_f0", act0, w_dn0, h1, _ag_send([wb["dn1"]]))

    b_norm_full = g_bnorm[:, 0].reshape(D_MODEL)
    w_b_in = g_bin.transpose(1, 0, 2).reshape(D_MODEL, B_IN)
    w_b_gate = jnp.pad(w_b_in[:, 3 * D_MODEL:], ((0, 0), (0, LANES - N_HEADS)))
    w_b_cat = jnp.concatenate([w_b_in[:, :3 * D_MODEL], w_b_gate], axis=1)
    w_b_out = g_bout.reshape(D_MODEL, D_MODEL)
    n2 = _rms_fwd("rms_b", h2, b_norm_full)
    qkv, (g_gu1, g_dn1) = _matmul("proj_b", n2, w_b_in[:, :3 * D_MODEL], "nn", BF16, TM, 1024, 1024, col0_scale=SOFTMAX_SCALE,
                                  comm=_ag_forward([g_gu1, g_dn1]))
    z = _matmul("gate_b", n2, w_b_gate, "nn", F32, TM, LANES, 1024)
    kbias = _gate_fwd("gate_cumsum", z, bf_pad)
    tf = min(S, 512)
    o_b, lse_b = _fox_fwd("fox_fwd", qkv, kbias, tf)
    h3 = _matmul("out_b", o_b, w_b_out, "nn", F32, TM, 1024, 1024, resid=h2)

    w_dn1 = g_dn1.reshape(D_FF, D_MODEL)
    n3 = _rms_fwd("rms_f1", h3, ffn_norm[1])
    gu1, act1 = _ffn_gu("gu_f1", n3, g_gu1)
    h4 = _ffn_down("down_f1", act1, w_dn1, h3)[0]

    dh4, d_final, loss = _loss_head("loss_head", h4, final_norm, target)

    share, got, sums, others = {}, {}, {}, {}

    def pair_sums(*names):
        for n in names:
            sums[n] = _pair_sum("pair_" + n, share[n], got[n], core, ROW_TILE[n])

    dgu1 = _ffn_dact("dact_f1", dh4, w_dn1, gu1)[0]
    share["dn1"] = _ffn_dwd("dwd_f1", act1, dh4)
    share["gu1"] = _ffn_dwgu("dwgu_f1", n3, dgu1)[0]
    dn3, got["gu1"], got["dn1"] = _ffn_dn("dn_f1", dgu1, g_gu1, _rs_swap([share["gu1"], share["dn1"]]))
    dh3, d_ffn1 = _rms_bwd("rmsb_f1", dn3, h3, ffn_norm[1], dh4)
    pair_sums("gu1", "dn1")

    do_b = _matmul("dout_b", dh3, w_b_out, "nt", BF16, TM, 1024, 1024)
    share["b_w_out"] = _matmul("dwout_b", o_b, dh3, "tn", BF16, TM, 1024, 1024).reshape(N_DEV, 128, D_MODEL)
    dq_b, dk_b, dv_b, ds_rowsum, ds_colsum = _fox_bwd("fox_bwd", qkv, kbias, do_b, o_b, lse_b, tf)
    dc = ds_rowsum[:, :N_HEADS] - ds_colsum.reshape(N_HEADS, S).T
    dz, d_bf = _gate_bwd("gate_bwd", jnp.pad(dc, ((0, 0), (0, LANES - N_HEADS))), z, bf_pad)
    dproj_b = jnp.concatenate([dq_b, dk_b, dv_b, dz.astype(BF16)], axis=1)
    dw_b_in, (others["gu1"],) = _matmul("dwin_b", n2, dproj_b, "tn", BF16, TM, B_IN_PAD // 5, 1024, comm=_rs_exchange([sums["gu1"]]))
    dn2, (others["dn1"],) = _matmul("dn_b", dproj_b, w_b_cat, "nt", F32, TM, 1024, B_IN_PAD // 5, comm=_rs_exchange([sums["dn1"]]))
    dh2, d_bnorm = _rms_bwd("rmsb_b", dn2, h2, b_norm_full, dh3)
    share["b_w_in"] = dw_b_in[:, :B_IN].reshape(D_MODEL, N_DEV, B_BLK).transpose(1, 0, 2)

    dgu0, got["b_w_in"], got["b_w_out"] = _ffn_dact("dact_f0", dh2, w_dn0, gu0, _rs_swap([share["b_w_in"], share["b_w_out"]]))
    share["dn0"] = _ffn_dwd("dwd_f0", act0, dh2)
    pair_sums("b_w_in", "b_w_out")
    share["gu0"], others["b_w_in"], others["b_w_out"] = _ffn_dwgu(
        "dwgu_f0", n1, dgu0, _rs_exchange([sums["b_w_in"], sums["b_w_out"]]))
    dn1, got["gu0"], got["dn0"] = _ffn_dn("dn_f0", dgu0, g_gu0, _rs_swap([share["gu0"], share["dn0"]]))
    dh1, d_ffn0 = _rms_bwd("rmsb_f0", dn1, h1, ffn_norm[0], dh2)
    pair_sums("gu0", "dn0")

    do_views = _matmul_nt_views("dout_a", dh1, w_a_out, dils)
    share["a_w_out"] = _matmul("dwout_a", o_a, dh1, "tn", BF16, TM, 1024, 1024).reshape(N_DEV, 128, D_MODEL)
    pieces = []
    for g, dil, L, view in groups:
        rot = tuple(tb.reshape(L, dil * LANES) for tb in tabs)
        grads = _dil_bwd("dil_bwd%d" % g, view, do_views[g], o_views[g], lse_views[g], rot, dil, L)
        pieces += [from_view("d" + tag, a, dil) for tag, a in zip("qkv", grads)]
    dproj_a = jnp.concatenate(pieces, axis=1)
    share["a_w_in"], others["gu0"], others["dn0"] = _mm_call(
        "dwin_a", (N_DEV, 1, S // 1024), n0, pl.BlockSpec((1024, D_MODEL), lambda d, i, k: (k, 0)),
        dproj_a, pl.BlockSpec((1024, A_BLK), lambda d, i, k: (k, d)), TN, [jax.ShapeDtypeStruct((N_DEV, D_MODEL, A_BLK), BF16)],
        [pl.BlockSpec((None, D_MODEL, A_BLK), lambda d, i, k: (d, 0, 0))], (D_MODEL, A_BLK),
        comm=_rs_exchange([sums["gu0"], sums["dn0"]]))
    got["a_w_in"], got["a_w_out"] = _comm_call("swap_a", _rs_swap([share["a_w_in"], share["a_w_out"]]))
    pair_sums("a_w_in", "a_w_out")
    dn0, others["a_w_in"], others["a_w_out"] = _mm_call(
        "dn_a", (S // TM, 1, N_DEV), dproj_a, pl.BlockSpec((TM, A_BLK), lambda i, j, k: (i, k)),
        g_ain, pl.BlockSpec((None, D_MODEL, A_BLK), lambda i, j, k: (k, 0, 0)), NT, [jax.ShapeDtypeStruct((S, D_MODEL), F32)],
        [pl.BlockSpec((TM, D_MODEL), lambda i, j, k: (i, 0))], (TM, D_MODEL),
        comm=_rs_exchange([sums["a_w_in"], sums["a_w_out"]]))
    dx, d_anorm = _rms_bwd("rmsb_a", dn0, h0, a_norm[0], dh1)

    misc = jnp.concatenate([d_bf[:, :N_HEADS], loss[:, :1], jnp.zeros((1, D_MODEL - N_HEADS - 1), F32)], axis=1)
    small = jnp.concatenate([d_anorm, d_ffn0, d_ffn1, d_final, d_bnorm, misc, jnp.zeros((2, D_MODEL), F32)], axis=0)
    small_all, = _comm_call("gather_small", _ag_send([small], direct=True))

    outs = {}
    for n in TENSORS:
        mine = lax.dynamic_index_in_dim(sums[n], 2 * xi + yi, axis=0, keepdims=False)
        outs[n] = _adamw("adamw_" + n, [mine] + [others[n][k] for k in range(3)], w_sh[n], m_sh[n], v_sh[n], ROW_TILE[n])

    pad_vec = lambda a: jnp.pad(a, ((0, 0), (0, D_MODEL - a.shape[1])))

    def small_pack(an, fn, fin, bf):
        return jnp.concatenate([an, fn, fin.reshape(1, D_MODEL), jnp.zeros((1, D_MODEL), F32), pad_vec(bf),
                                jnp.zeros((2, D_MODEL), F32)], axis=0)

    sg, sd, sm, sv = _adamw("adamw_small", [small_all[d] for d in range(N_DEV)], small_pack(a_norm, ffn_norm, final_norm, b_f),
                            small_pack(m_a_norm, m_ffn_norm, m_final_norm, m_b_f),
                            small_pack(v_a_norm, v_ffn_norm, v_final_norm, v_b_f), 8)
    g_bn = lax.dynamic_slice(sg[4:5], (0, dev * LANES), (1, LANES))
    bn = _adamw("adamw_b_norm", [g_bn], b_norm, m_b_norm, v_b_norm, 1)

    def tree(i):
        full = lambda name, ref: outs[name][i].reshape(ref.shape)
        sml = (sg, sd, sm, sv)[i]
        return dict(
            a_norm=sml[0:1], a_w_in=full("a_w_in", a_w_in), a_w_out=full("a_w_out", a_w_out), b_norm=bn[i],
            b_w_in=full("b_w_in", b_w_in), b_f=sml[5:6, :N_HEADS], b_w_out=full("b_w_out", b_w_out), ffn_norm=sml[1:3],
            ffn_w_gu=jnp.stack([outs["gu0"][i], outs["gu1"][i]]).reshape(ffn_w_gu.shape),
            ffn_w_down=jnp.stack([outs["dn0"][i], outs["dn1"][i]]).reshape(ffn_w_down.shape), final_norm=sml[3])

    order = ("a_norm", "a_w_in", "a_w_out", "b_norm", "b_w_in", "b_f", "b_w_out", "ffn_norm", "ffn_w_gu", "ffn_w_down", "final_norm")
    result = [sg[5, N_HEADS], dx.reshape(x.shape)]
    for i in range(4):
        t = tree(i)
        result += [t[n] for n in order]
    return tuple(result)
```

```python
import functools
from typing import Callable, NamedTuple

import jax
import jax.numpy as jnp
from jax import lax
from jax.experimental import pallas as pl
from jax.experimental.pallas import tpu as pltpu

F32 = jnp.float32
BF16 = jnp.bfloat16

D_MODEL = 1024
N_HEADS = 16
HEAD_DIM = 64
N_PAIRS = N_HEADS // 2
LANES = 128
DILATED_PATTERNS = ((128, 1), (512, 4), (2048, 16))
BAND_STEPS = 128
ROT_DIM = HEAD_DIM // 4
ROPE_THETA = 500000.0
D_FF = 2816
RMS_EPS = 1e-6
NEG_INF = -1e30
SOFTMAX_SCALE = HEAD_DIM ** -0.5
N_DEV = 8
FF_BLK = 2 * D_FF // N_DEV
ADAM_LR, ADAM_B1, ADAM_B2, ADAM_EPS, ADAM_WD, ADAM_STEP = 0.001, 0.9, 0.999, 1e-08, 0.01, 10
VMEM_LIMIT = 52 * 1024 * 1024
FOX_BWD_VMEM = 60 * 1024 * 1024
TM = 1024
MESH = pl.DeviceIdType.MESH

NN = (((1,), (0,)), ((), ()))
NT = (((1,), (1,)), ((), ()))
TN = (((0,), (0,)), ((), ()))


def _params(*sem):
    return pltpu.CompilerParams(dimension_semantics=sem, vmem_limit_bytes=VMEM_LIMIT)


def _dot(a, b, dims):
    return lax.dot_general(a, b, dims, preferred_element_type=F32)


class _Comm(NamedTuple):
    ins: tuple
    outs: tuple
    aliases: dict
    copies: Callable
    n_remote: int
    n_local: int


def _call(name, body, grid, in_specs, out_specs, out_shape, scratch, args, sem, comm=None):
    if comm is None:
        return pl.pallas_call(body, grid=grid, in_specs=in_specs, out_specs=out_specs, out_shape=out_shape,
                              scratch_shapes=scratch, compiler_params=_params(*sem), name=name)(*args)
    n_in, n_out = len(in_specs), len(out_specs)
    n_ci, n_co = len(comm.ins), len(comm.outs)
    o0 = n_in + n_ci

    def hosted(*refs):
        c_ins, c_outs = refs[n_in:o0], refs[o0 + n_out:o0 + n_out + n_co]
        sems = refs[-3:]

        def start():
            for cp in comm.copies(c_ins, c_outs, *sems):
                cp.start()

        def wait():
            for cp in comm.copies(c_ins, c_outs, *sems):
                cp.wait()

        if not grid:
            start()
            body()
            wait()
            return
        ids = [pl.program_id(ax) for ax in range(len(grid))]
        pl.when(functools.reduce(jnp.logical_and, [i == 0 for i in ids]))(start)
        body(*refs[:n_in], *refs[o0:o0 + n_out], *refs[o0 + n_out + n_co:-3])
        pl.when(functools.reduce(jnp.logical_and, [i == g - 1 for i, g in zip(ids, grid)]))(wait)

    hbm = pl.BlockSpec(memory_space=pltpu.HBM)
    dma = pltpu.SemaphoreType.DMA
    return pl.pallas_call(
        hosted, grid=grid, in_specs=[*in_specs, *[hbm] * n_ci], out_specs=[*out_specs, *[hbm] * n_co],
        out_shape=[*out_shape, *comm.outs], input_output_aliases={n_in + i: n_out + o for i, o in comm.aliases.items()},
        scratch_shapes=[*scratch, dma((comm.n_remote,)), dma((comm.n_remote,)), dma((max(comm.n_local, 1),))],
        compiler_params=_params(*["arbitrary"] * len(grid)), name=name)(*args, *comm.ins)


def _mm_call(name, grid, a, a_spec, b, b_spec, dims, out_shapes, out_specs, acc_shape, epilogue=None,
             extras=(), extra_specs=(), col_axis=1, comm=None):
    nk = grid[2]
    n_extra = len(extras)
    n_out = len(out_shapes)

    def finish(res, ex, outs, j):
        if epilogue is None:
            outs[0][...] = res.astype(outs[0].dtype)
        else:
            epilogue(res, ex, outs, j)

    def body(*refs):
        a_ref, b_ref = refs[0], refs[1]
        ex = refs[2:2 + n_extra]
        outs = refs[2 + n_extra:2 + n_extra + n_out]
        j, k = pl.program_id(col_axis), pl.program_id(2)
        part = _dot(a_ref[...].astype(BF16), b_ref[...].astype(BF16), dims)
        if nk == 1:
            finish(part, ex, outs, j)
            return
        acc = refs[-1]

        @pl.when(k == 0)
        def _():
            acc[...] = part

        @pl.when((k > 0) & (k < nk - 1))
        def _():
            acc[...] += part

        @pl.when(k == nk - 1)
        def _():
            finish(acc[...] + part, ex, outs, j)

    return _call(name, body, grid, [a_spec, b_spec, *extra_specs], out_specs, out_shapes,
                 [] if nk == 1 else [pltpu.VMEM(acc_shape, F32)], (a, b, *extras), ("parallel", "parallel", "arbitrary"), comm)


def _matmul(name, a, b, mode, out_dtype, tm, tn, tk, resid=None, col0_scale=None, comm=None):
    if mode == "nn":
        (M, K), N = a.shape, b.shape[1]
        a_spec = pl.BlockSpec((tm, tk), lambda j, i, k: (i, k))
        b_spec = pl.BlockSpec((tk, tn), lambda j, i, k: (k, j))
        dims = NN
    elif mode == "nt":
        (M, K), N = a.shape, b.shape[0]
        a_spec = pl.BlockSpec((tm, tk), lambda j, i, k: (i, k))
        b_spec = pl.BlockSpec((tn, tk), lambda j, i, k: (j, k))
        dims = NT
    else:
        (K, M), N = a.shape, b.shape[1]
        a_spec = pl.BlockSpec((tk, tm), lambda j, i, k: (k, i))
        b_spec = pl.BlockSpec((tk, tn), lambda j, i, k: (k, j))
        dims = TN
    assert M % tm == 0 and N % tn == 0 and K % tk == 0, (name, M, N, K, tm, tn, tk)
    o_spec = pl.BlockSpec((tm, tn), lambda j, i, k: (i, j))
    extras, extra_specs, epilogue = (), (), None
    if resid is not None:
        extras, extra_specs = (resid,), (o_spec,)

        def epilogue(acc, ex, outs, j):
            outs[0][...] = (acc + ex[0][...]).astype(outs[0].dtype)

    elif col0_scale is not None:

        def epilogue(acc, ex, outs, j):
            outs[0][...] = (acc * jnp.where(j == 0, col0_scale, 1.0)).astype(outs[0].dtype)

    res = _mm_call(name, (N // tn, M // tm, K // tk), a, a_spec, b, b_spec, dims, [jax.ShapeDtypeStruct((M, N), out_dtype)],
                   [o_spec], (tm, tn), epilogue, extras, extra_specs, col_axis=0, comm=comm)
    return res[0] if comm is None else (res[0], res[1:])


def _rms_fwd(name, h, gain, tm=512):
    S, D = h.shape

    def body(h_ref, g_ref, n_ref):
        x = h_ref[...]
        rstd = lax.rsqrt(jnp.mean(x * x, axis=-1, keepdims=True) + RMS_EPS)
        n_ref[...] = (x * rstd * g_ref[...]).astype(BF16)

    return pl.pallas_call(
        body, grid=(S // tm,), in_specs=[pl.BlockSpec((tm, D), lambda i: (i, 0)), pl.BlockSpec((1, D), lambda i: (0, 0))],
        out_specs=pl.BlockSpec((tm, D), lambda i: (i, 0)), out_shape=jax.ShapeDtypeStruct((S, D), BF16),
        compiler_params=_params("parallel"), name=name)(h, gain.reshape(1, D))


def _rms_bwd(name, dn, h, gain, dres, tm=512):
    S, D = h.shape

    def body(dn_ref, h_ref, g_ref, r_ref, dh_ref, dg_ref):
        x = h_ref[...]
        rstd = lax.rsqrt(jnp.mean(x * x, axis=-1, keepdims=True) + RMS_EPS)
        xhat = x * rstd
        d = dn_ref[...]
        dxhat = d * g_ref[...]
        dh_ref[...] = rstd * (dxhat - xhat * jnp.mean(dxhat * xhat, axis=-1, keepdims=True)) + r_ref[...]

        @pl.when(pl.program_id(0) == 0)
        def _():
            dg_ref[...] = jnp.zeros_like(dg_ref)

        dg_ref[...] += jnp.sum(d * xhat, axis=0, keepdims=True)

    row = pl.BlockSpec((tm, D), lambda i: (i, 0))
    vec = pl.BlockSpec((1, D), lambda i: (0, 0))
    return pl.pallas_call(
        body, grid=(S // tm,), in_specs=[row, row, vec, row], out_specs=[row, vec],
        out_shape=[jax.ShapeDtypeStruct((S, D), F32), jax.ShapeDtypeStruct((1, D), F32)],
        compiler_params=_params("arbitrary"), name=name)(dn, h, gain.reshape(1, D), dres)


def _loss_head(name, h, gain, target, tm=512):
    S, D = h.shape

    def body(h_ref, g_ref, t_ref, dh_ref, dg_ref, loss_ref):
        x = h_ref[...]
        rstd = lax.rsqrt(jnp.mean(x * x, axis=-1, keepdims=True) + RMS_EPS)
        xhat = x * rstd
        err = xhat * g_ref[...] - t_ref[...]
        dy = err * (1.0 / D)
        dxhat = dy * g_ref[...]
        dh_ref[...] = rstd * (dxhat - xhat * jnp.mean(dxhat * xhat, axis=-1, keepdims=True))

        @pl.when(pl.program_id(0) == 0)
        def _():
            dg_ref[...] = jnp.zeros_like(dg_ref)
            loss_ref[...] = jnp.zeros_like(loss_ref)

        dg_ref[...] += jnp.sum(dy * xhat, axis=0, keepdims=True)
        part = 0.5 * jnp.sum(jnp.mean(err * err, axis=-1, keepdims=True), axis=0, keepdims=True)
        loss_ref[...] += jnp.broadcast_to(part, loss_ref.shape)

    row = pl.BlockSpec((tm, D), lambda i: (i, 0))
    vec = pl.BlockSpec((1, D), lambda i: (0, 0))
    return pl.pallas_call(
        body, grid=(S // tm,), in_specs=[row, vec, row], out_specs=[row, vec, pl.BlockSpec((1, LANES), lambda i: (0, 0))],
        out_shape=[jax.ShapeDtypeStruct((S, D), F32), jax.ShapeDtypeStruct((1, D), F32),
                   jax.ShapeDtypeStruct((1, LANES), F32)],
        compiler_params=_params("arbitrary"), name=name)(h, gain.reshape(1, D), target)


def _rope_tables(S):
    half = ROT_DIM // 2
    inv_freq = ROPE_THETA ** (-jnp.arange(half, dtype=F32) * 2.0 / ROT_DIM)
    ang = jnp.arange(S, dtype=F32)[:, None] * inv_freq[None, :]
    cos, sin = jnp.cos(ang), jnp.sin(ang)
    one = jnp.ones((S, HEAD_DIM - ROT_DIM), F32)
    zero = jnp.zeros((S, HEAD_DIM - ROT_DIM), F32)
    zh = jnp.zeros((S, half), F32)
    c = jnp.concatenate([cos, cos, one], axis=1)
    sa = jnp.concatenate([-sin, zh, zero], axis=1)
    sb = jnp.concatenate([zh, sin, zero], axis=1)
    return tuple(jnp.concatenate([t, t], axis=1) for t in (c, sa, sb))


def _rotate(x, c, sa, sb, sign):
    return x * c + sign * (pltpu.roll(x, LANES - ROT_DIM // 2, 1) * sa + pltpu.roll(x, ROT_DIM // 2, 1) * sb)


def _stage_chunks(scr, chunks):
    for c, x in enumerate(chunks):
        scr[c] = x


def _strided_rows(scr, c, r, n, R):
    return scr.at[c][pl.ds(r, n, stride=R), :]


def _a_proj(name, n, w, tabs, comm, tm=512):
    S, D = n.shape
    n_i = S // tm
    dils = [dil for _, dil in DILATED_PATTERNS]
    n_out = 3 * len(dils)

    def body(n_ref, w_ref, c_ref, sa_ref, sb_ref, *rest):
        outs, scr = rest[:n_out], rest[n_out]
        j = pl.program_id(0)
        acc = _dot(n_ref[...], w_ref[...], NN)
        c, sa, sb = c_ref[...], sa_ref[...], sb_ref[...]
        for J in range(n_out):
            R, kind = dils[J // 3], J % 3

            @pl.when(j == J)
            def _(J=J, R=R, kind=kind):
                chunks = [acc[:, b * LANES:(b + 1) * LANES] for b in range(N_PAIRS)]
                if kind < 2:
                    chunks = [_rotate(x, c, sa, sb, 1.0) * (SOFTMAX_SCALE if kind == 0 else 1.0) for x in chunks]
                if R == 1:
                    for b, x in enumerate(chunks):
                        outs[J][:, b * LANES:(b + 1) * LANES] = x.astype(BF16)
                    return
                _stage_chunks(scr, chunks)
                for r in range(R):
                    for b in range(N_PAIRS):
                        col = r * D_MODEL + b * LANES
                        outs[J][:, col:col + LANES] = _strided_rows(scr, b, r, tm // R, R).astype(BF16)

    def out_spec(J, R):
        return pl.BlockSpec((tm // R, R * D_MODEL), lambda j, i: (jnp.where(j == J, i, jnp.where(j < J, 0, n_i - 1)), 0))

    tab = pl.BlockSpec((tm, LANES), lambda j, i: (i, 0))
    res = _call(name, body, (n_out, n_i),
                [pl.BlockSpec((tm, D), lambda j, i: (i, 0)), pl.BlockSpec((D, D_MODEL), lambda j, i: (0, j)), tab, tab, tab],
                [out_spec(J, dils[J // 3]) for J in range(n_out)],
                [jax.ShapeDtypeStruct((S // dils[J // 3], dils[J // 3] * D_MODEL), BF16) for J in range(n_out)],
                [pltpu.VMEM((N_PAIRS, tm, LANES), F32)], (n, w, *tabs), ("arbitrary", "arbitrary"), comm)
    return [res[3 * g:3 * g + 3] for g in range(len(dils))], res[n_out:]


def _token_order(ref, R, stage, rows):
    if R == 1:
        return ref[...]
    for r in range(R):
        for b in range(N_PAIRS):
            col = r * D_MODEL + b * LANES
            stage.at[b][pl.ds(r, rows // R, stride=R), :] = ref[:, col:col + LANES].astype(F32)
    return jnp.concatenate([stage[b] for b in range(N_PAIRS)], axis=1).astype(ref.dtype)


def _a_dw(name, n, pieces, dils, comm, tk=512):
    S, D = n.shape
    n_k = S // tk
    n_p = len(pieces)

    def body(n_ref, *rest):
        p_refs, o_ref, acc, stage = rest[:n_p], rest[n_p], rest[n_p + 1], rest[n_p + 2]
        j, k = pl.program_id(0), pl.program_id(1)
        for J in range(n_p):

            @pl.when(j == J)
            def _(J=J):
                part = _dot(n_ref[...], _token_order(p_refs[J], dils[J // 3], stage, tk), TN)

                @pl.when(k == 0)
                def _():
                    acc[...] = part

                @pl.when(k > 0)
                def _():
                    acc[...] += part

        @pl.when(k == n_k - 1)
        def _():
            o_ref[...] = acc[...].astype(BF16)

    def piece_spec(J):
        R = dils[J // 3]
        return pl.BlockSpec((tk // R, R * D_MODEL), lambda j, k: (jnp.where(j == J, k, jnp.where(j < J, 0, n_k - 1)), 0))

    return _call(name, body, (n_p, n_k), [pl.BlockSpec((tk, D), lambda j, k: (k, 0))] + [piece_spec(J) for J in range(n_p)],
                 [pl.BlockSpec((D, D_MODEL), lambda j, k: (0, j))], [jax.ShapeDtypeStruct((D, n_p * D_MODEL), BF16)],
                 [pltpu.VMEM((D, D_MODEL), F32), pltpu.VMEM((N_PAIRS, tk, LANES), F32)], (n, *pieces), ("arbitrary", "arbitrary"), comm)


def _a_dn(name, pieces, dils, w, comm, tm=512):
    D = w.shape[0]
    S = pieces[0].shape[0] * dils[0]
    n_p = len(pieces)

    def body(*refs):
        p_refs, w_ref, o_ref, acc, stage = refs[:n_p], refs[n_p], refs[n_p + 1], refs[n_p + 2], refs[n_p + 3]
        j = pl.program_id(1)
        for J in range(n_p):

            @pl.when(j == J)
            def _(J=J):
                part = _dot(_token_order(p_refs[J], dils[J // 3], stage, tm), w_ref[...], NT)
                if J == 0:
                    acc[...] = part
                elif J < n_p - 1:
                    acc[...] += part
                else:
                    o_ref[...] = acc[...] + part

    specs = [pl.BlockSpec((tm // dils[J // 3], dils[J // 3] * D_MODEL), lambda i, j: (i, 0)) for J in range(n_p)]
    return _call(name, body, (S // tm, n_p), specs + [pl.BlockSpec((D, D_MODEL), lambda i, j: (0, j))],
                 [pl.BlockSpec((tm, D), lambda i, j: (i, 0))], [jax.ShapeDtypeStruct((S, D), F32)],
                 [pltpu.VMEM((tm, D), F32), pltpu.VMEM((N_PAIRS, tm, LANES), F32)], (*pieces, w), ("arbitrary", "arbitrary"), comm)


def _lo_lanes():
    return lax.broadcasted_iota(jnp.int32, (1, LANES), 1) < HEAD_DIM


def _rep_rows(x2, lo):
    sw = pltpu.roll(x2, HEAD_DIM, 1)
    return jnp.where(lo, x2, sw), jnp.where(lo, sw, x2)


def _pair_cols(h):
    return slice((h // 2) * LANES, (h // 2 + 1) * LANES)


def _head_lanes(lo, h):
    return lo if h % 2 == 0 else jnp.logical_not(lo)


def _band_masks(t, first):
    ri = lax.broadcasted_iota(jnp.int32, (t, t), 0)
    ci = lax.broadcasted_iota(jnp.int32, (t, t), 1)
    neg_prev = jnp.where((ci >= ri) & jnp.logical_not(first), 0.0, NEG_INF)
    neg_cur = jnp.where(ci <= ri, 0.0, NEG_INF)
    return neg_prev, neg_cur


def _dil_specs(L, R, t, qcol, kcol, vcol):
    W = D_MODEL
    prev = lambda qi: jnp.maximum(qi - 1, 0)
    return dict(
        q=pl.BlockSpec((t, W), lambda r, qi: (qi, qcol(r))),
        kp=pl.BlockSpec((t, W), lambda r, qi: (prev(qi), kcol(r))), kc=pl.BlockSpec((t, W), lambda r, qi: (qi, kcol(r))),
        vp=pl.BlockSpec((t, W), lambda r, qi: (prev(qi), vcol(r))), vc=pl.BlockSpec((t, W), lambda r, qi: (qi, vcol(r))),
        own=pl.BlockSpec((t, W), lambda r, qi: (qi, r)), tab=pl.BlockSpec((t, LANES), lambda r, qi: (qi, r)))


def _dil_fwd(name, x, qcol, kcol, vcol, R, L):
    t = BAND_STEPS
    W = D_MODEL
    sp = _dil_specs(L, R, t, qcol, kcol, vcol)

    def body(q_ref, kp_ref, kc_ref, vp_ref, vc_ref, o_ref, lse_ref):
        lo = _lo_lanes()
        neg_p, neg_c = _band_masks(t, pl.program_id(1) == 0)
        s_p, s_c = [], []
        for h in range(N_HEADS):
            cols = _pair_cols(h)
            qh = jnp.where(_head_lanes(lo, h), q_ref[:, cols], 0)
            s_p.append(_dot(qh, kp_ref[:, cols], NT))
            s_c.append(_dot(qh, kc_ref[:, cols], NT))
        s_p = jnp.stack(s_p) + neg_p[None]
        s_c = jnp.stack(s_c) + neg_c[None]
        m = jnp.maximum(jnp.max(s_p, axis=2, keepdims=True), jnp.max(s_c, axis=2, keepdims=True))
        p_p, p_c = jnp.exp(s_p - m), jnp.exp(s_c - m)
        l = jnp.sum(p_p, axis=2, keepdims=True) + jnp.sum(p_c, axis=2, keepdims=True)
        inv, lse = 1.0 / l, m + jnp.log(l)
        p_p, p_c = p_p.astype(BF16), p_c.astype(BF16)
        for p in range(N_PAIRS):
            cols = _pair_cols(2 * p)
            o2 = jnp.zeros((t, LANES), F32)
            for h in (2 * p, 2 * p + 1):
                hm = _head_lanes(lo, h)
                pv = _dot(p_p[h], jnp.where(hm, vp_ref[:, cols], 0), NN) + _dot(p_c[h], jnp.where(hm, vc_ref[:, cols], 0), NN)
                o2 = o2 + pv * inv[h]
            o_ref[:, cols] = o2
            lse_ref[:, cols] = jnp.where(lo, lse[2 * p], lse[2 * p + 1])

    return pl.pallas_call(
        body, grid=(R, L // t), in_specs=[sp["q"], sp["kp"], sp["kc"], sp["vp"], sp["vc"]], out_specs=[sp["own"], sp["own"]],
        out_shape=[jax.ShapeDtypeStruct((L, R * W), F32), jax.ShapeDtypeStruct((L, R * W), F32)],
        compiler_params=_params("parallel", "parallel"), name=name)(x[0], x[1], x[1], x[2], x[2])


def _dil_scores(lo, q_ref, do_ref, o_ref, lse_ref, kv_refs):
    s = [[] for _ in kv_refs]
    dp = [[] for _ in kv_refs]
    lse, d = [], []
    for h in range(N_HEADS):
        cols = _pair_cols(h)
        hm = _head_lanes(lo, h)
        qh, doh = jnp.where(hm, q_ref[:, cols], 0), jnp.where(hm, do_ref[:, cols], 0)
        for i, (k_ref, v_ref) in enumerate(kv_refs):
            s[i].append(_dot(qh, k_ref[:, cols], NT))
            dp[i].append(_dot(doh, v_ref[:, cols], NT))
        lse.append(_rep_rows(lse_ref[:, cols], lo)[h % 2])
        dd = do_ref[:, cols].astype(F32) * o_ref[:, cols].astype(F32)
        d.append(jnp.sum(jnp.where(hm, dd, 0.0), axis=1, keepdims=True))
    return (*[jnp.stack(x) for x in s], *[jnp.stack(x) for x in dp], jnp.stack(lse), jnp.stack(d))


def _dil_bwd(name, x, do, o, lse, tabs, R, L):
    t = BAND_STEPS
    W = D_MODEL
    nq = L // t
    qb = lambda qi: jnp.minimum(qi, nq - 1)
    kb = lambda qi: jnp.maximum(qb(qi) - 1, 0)
    done = lambda qi: jnp.maximum(qi - 1, 0)
    at = lambda f, width: pl.BlockSpec((t, width), lambda r, qi: (f(qi), r))

    def body(q_ref, kp_ref, kc_ref, vp_ref, vc_ref, do_ref, o_ref, lse_ref, cq_ref, saq_ref, sbq_ref, ck_ref, sak_ref, sbk_ref,
             dq_ref, dk_ref, dv_ref, dk_scr, dv_scr):
        qi = pl.program_id(1)
        lo = _lo_lanes()
        rot_k = lambda x: _rotate(x, ck_ref[...], sak_ref[...], sbk_ref[...], -1.0).astype(BF16)

        @pl.when(qi == 0)
        def _():
            dk_scr[...] = jnp.zeros_like(dk_scr)
            dv_scr[...] = jnp.zeros_like(dv_scr)

        @pl.when(qi < nq)
        def _():
            neg_p, neg_c = _band_masks(t, qi == 0)
            s_p, s_c, dp_p, dp_c, lse_h, d = _dil_scores(lo, q_ref, do_ref, o_ref, lse_ref, ((kp_ref, vp_ref), (kc_ref, vc_ref)))
            p_p, p_c = jnp.exp(s_p + neg_p[None] - lse_h), jnp.exp(s_c + neg_c[None] - lse_h)
            ds_p, ds_c = (p_p * (dp_p - d)).astype(BF16), (p_c * (dp_c - d)).astype(BF16)
            p_p, p_c = p_p.astype(BF16), p_c.astype(BF16)
            for p in range(N_PAIRS):
                cols = _pair_cols(2 * p)
                dq2 = jnp.zeros((t, LANES), F32)
                dk_prev, dv_prev = dk_scr[:, cols], dv_scr[:, cols]
                dk_cur, dv_cur = jnp.zeros((t, LANES), F32), jnp.zeros((t, LANES), F32)
                for h in (2 * p, 2 * p + 1):
                    hm = _head_lanes(lo, h)
                    qh, doh = jnp.where(hm, q_ref[:, cols], 0), jnp.where(hm, do_ref[:, cols], 0)
                    dq2 = dq2 + _dot(ds_p[h], jnp.where(hm, kp_ref[:, cols], 0), NN) + _dot(ds_c[h], jnp.where(hm, kc_ref[:, cols], 0), NN)
                    dk_prev, dv_prev = dk_prev + _dot(ds_p[h], qh, TN), dv_prev + _dot(p_p[h], doh, TN)
                    dk_cur, dv_cur = dk_cur + _dot(ds_c[h], qh, TN), dv_cur + _dot(p_c[h], doh, TN)
                dq_ref[:, cols] = _rotate(dq2 * SOFTMAX_SCALE, cq_ref[...], saq_ref[...], sbq_ref[...], -1.0).astype(BF16)
                dk_ref[:, cols] = rot_k(dk_prev)
                dv_ref[:, cols] = dv_prev.astype(BF16)
                dk_scr[:, cols] = dk_cur
                dv_scr[:, cols] = dv_cur

        @pl.when(qi == nq)
        def _():
            for p in range(N_PAIRS):
                cols = _pair_cols(2 * p)
                dk_ref[:, cols] = rot_k(dk_scr[:, cols])
            dv_ref[...] = dv_scr[...].astype(BF16)

    wide = jax.ShapeDtypeStruct((L, R * W), BF16)
    return pl.pallas_call(
        body, grid=(R, nq + 1),
        in_specs=[at(qb, W), at(kb, W), at(qb, W), at(kb, W), at(qb, W), at(qb, W), at(qb, W), at(qb, W),
                  at(qb, LANES), at(qb, LANES), at(qb, LANES), at(done, LANES), at(done, LANES), at(done, LANES)],
        out_specs=[at(qb, W), at(done, W), at(done, W)], out_shape=[wide, wide, wide],
        scratch_shapes=[pltpu.VMEM((t, W), F32), pltpu.VMEM((t, W), F32)],
        compiler_params=_params("parallel", "arbitrary"), name=name)(x[0], x[1], x[1], x[2], x[2], do, o, lse, *tabs, *tabs)


def _fox_operands(q2, k2, kb2, lo, hh):
    lane = lax.broadcasted_iota(jnp.int32, (1, LANES), 1)
    if hh == 0:
        ones = ((lane >= HEAD_DIM) & (lane < HEAD_DIM + 3)).astype(BF16)
        return jnp.where(lo, q2, ones), jnp.where(lo, k2, kb2)
    ones = (lane < 3).astype(BF16)
    return jnp.where(lo, ones, q2), jnp.where(lo, kb2, k2)


def _causal_neg(t):
    ri = lax.broadcasted_iota(jnp.int32, (t, t), 0)
    ci = lax.broadcasted_iota(jnp.int32, (t, t), 1)
    return jnp.where(ci <= ri, 0.0, NEG_INF)


def _fox_fwd(name, qkv, kbias, t):
    S = qkv.shape[0]
    W = D_MODEL
    nq = S // t
    rep = t // LANES

    def body(q_ref, k_ref, v_ref, kb_ref, o_ref, lse_ref, m_scr, l_scr, acc_scr):
        qi, j = pl.program_id(0), pl.program_id(1)
        lo = _lo_lanes()

        @pl.when(j == 0)
        def _():
            m_scr[...] = jnp.full_like(m_scr, NEG_INF)
            l_scr[...] = jnp.zeros_like(l_scr)
            acc_scr[...] = jnp.zeros_like(acc_scr)

        def step(masked):
            neg = _causal_neg(t) if masked else None

            def pair(p, carry):
                cs = pl.ds(pl.multiple_of(p * LANES, LANES), LANES)
                q2, k2, v2, kb2 = q_ref[:, cs], k_ref[:, cs], v_ref[:, cs], kb_ref[:, cs]
                pvs, alphas = [], []
                for hh in range(2):
                    hm = lo if hh == 0 else jnp.logical_not(lo)
                    qh, kh = _fox_operands(q2, k2, kb2, lo, hh)
                    s = _dot(qh, kh, NT)
                    if masked:
                        s = s + neg
                    h = 2 * p + hh
                    m_prev = m_scr[h]
                    m_new = jnp.maximum(m_prev, jnp.max(s, axis=1, keepdims=True))
                    pe = jnp.exp(s - jnp.tile(m_new, (1, rep)))
                    alpha = jnp.exp(m_prev - m_new)
                    l_scr[h] = alpha * l_scr[h] + jnp.sum(pe, axis=1, keepdims=True)
                    m_scr[h] = m_new
                    pvs.append(_dot(pe.astype(BF16), jnp.where(hm, v2, 0), NN))
                    alphas.append(alpha)
                acc_scr[:, cs] = acc_scr[:, cs] * jnp.where(lo, alphas[0], alphas[1]) + pvs[0] + pvs[1]
                return carry

            lax.fori_loop(0, N_PAIRS, pair, 0, unroll=4)

        @pl.when(j < qi)
        def _():
            step(False)

        @pl.when(j == qi)
        def _():
            step(True)

        @pl.when(j == nq - 1)
        def _():
            for p in range(N_PAIRS):
                cols = slice(p * LANES, (p + 1) * LANES)
                l2 = jnp.where(lo, l_scr[2 * p], l_scr[2 * p + 1])
                m2 = jnp.where(lo, m_scr[2 * p], m_scr[2 * p + 1])
                o_ref[:, cols] = (acc_scr[:, cols] / l2).astype(BF16)
                lse_ref[:, cols] = m2 + jnp.log(l2)

    kv = lambda col: pl.BlockSpec((t, W), lambda qi, j: (jnp.minimum(j, qi), col))
    own = pl.BlockSpec((t, W), lambda qi, j: (qi, 0))
    return pl.pallas_call(
        body, grid=(nq, nq), in_specs=[own, kv(1), kv(2), kv(0)], out_specs=[own, own],
        out_shape=[jax.ShapeDtypeStruct((S, W), BF16), jax.ShapeDtypeStruct((S, W), F32)],
        scratch_shapes=[pltpu.VMEM((N_HEADS, t, LANES), F32), pltpu.VMEM((N_HEADS, t, LANES), F32), pltpu.VMEM((t, W), F32)],
        compiler_params=_params("parallel", "arbitrary"), name=name)(qkv, qkv, qkv, kbias)


def _fox_head_grads(qh, kh, v2, doh, neg, lse_h, d_h, rep):
    s = _dot(qh, kh, NT)
    if neg is not None:
        s = s + neg
    p = jnp.exp(s - jnp.tile(lse_h, (1, rep)))
    return p, p * (_dot(doh, v2, NT) - d_h)


def _fox_bwd(name, qkv, kbias, do, o, lse, t):
    S = qkv.shape[0]
    W = D_MODEL
    nq = S // t
    rep = t // LANES

    def body(q_ref, k_ref, v_ref, kb_ref, do_ref, o_ref, lse_ref, dq_ref, dk_ref, dv_ref, rs_ref, dc_ref, dq_scr, dk_scr, dv_scr):
        kb, j = pl.program_id(0), pl.program_id(1)
        lo = _lo_lanes()
        lane = lax.broadcasted_iota(jnp.int32, (1, LANES), 1)
        rows = pl.ds(pl.multiple_of(j * t, t), t)

        @pl.when((kb == 0) & (j == 0))
        def _():
            dq_scr[...] = jnp.zeros_like(dq_scr)
            rs_ref[...] = jnp.zeros_like(rs_ref)

        @pl.when(j == 0)
        def _():
            dk_scr[...] = jnp.zeros_like(dk_scr)
            dv_scr[...] = jnp.zeros_like(dv_scr)
            dc_ref[...] = jnp.zeros_like(dc_ref)

        def step(masked):
            neg = _causal_neg(t) if masked else None

            def pair(p, carry):
                cs = pl.ds(pl.multiple_of(p * LANES, LANES), LANES)
                q2, k2, v2, kb2, do2 = q_ref[:, cs], k_ref[:, cs], v_ref[:, cs], kb_ref[:, cs], do_ref[:, cs]
                dd = do2.astype(F32) * o_ref[:, cs].astype(F32)
                lse_h = _rep_rows(lse_ref[:, cs], lo)
                dq2 = jnp.zeros((t, LANES), F32)
                dv2 = jnp.zeros((t, LANES), F32)
                dk2 = jnp.zeros((t, LANES), F32)
                for hh in range(2):
                    hm = lo if hh == 0 else jnp.logical_not(lo)
                    qh, kh = _fox_operands(q2, k2, kb2, lo, hh)
                    doh = jnp.where(hm, do2, 0)
                    d_h = jnp.sum(jnp.where(hm, dd, 0.0), axis=1, keepdims=True)
                    pr, ds = _fox_head_grads(qh, kh, v2, doh, neg, lse_h[hh], d_h, rep)
                    rs_ref[rows, :] += jnp.where(lane == 2 * p + hh, jnp.sum(ds, axis=1, keepdims=True), 0.0)
                    dc_ref[p, hh:hh + 1, :] += jnp.sum(ds, axis=0, keepdims=True)
                    dsb = ds.astype(BF16)
                    dv2 = dv2 + _dot(pr.astype(BF16), doh, TN)
                    dk2 = dk2 + _dot(dsb, jnp.where(hm, q2, 0), TN)
                    dq2 = dq2 + _dot(dsb, jnp.where(hm, k2, 0), NN)
                dv_scr[:, cs] += dv2
                dk_scr[:, cs] += dk2
                dq_scr[rows, cs] += dq2
                return carry

            lax.fori_loop(0, N_PAIRS, pair, 0, unroll=2)
            if masked:
                dq_ref[...] = (dq_scr[rows, :] * SOFTMAX_SCALE).astype(BF16)

        @pl.when(j > kb)
        def _():
            step(False)

        @pl.when(j == kb)
        def _():
            step(True)

        @pl.when(j == nq - 1)
        def _():
            dv_ref[...] = dv_scr[...].astype(BF16)
            dk_ref[...] = dk_scr[...].astype(BF16)

    qrow = pl.BlockSpec((t, W), lambda kb, j: (jnp.maximum(j, kb), 0))
    krow = lambda col: pl.BlockSpec((t, W), lambda kb, j: (kb, col))
    own = pl.BlockSpec((t, W), lambda kb, j: (kb, 0))
    wide = jax.ShapeDtypeStruct((S, W), BF16)
    return pl.pallas_call(
        body, grid=(nq, nq), in_specs=[qrow, krow(1), krow(2), krow(0), qrow, qrow, qrow],
        out_specs=[own, own, own, pl.BlockSpec((S, LANES), lambda kb, j: (0, 0)), pl.BlockSpec((N_PAIRS, 2, t), lambda kb, j: (0, 0, kb))],
        out_shape=[wide, wide, wide, jax.ShapeDtypeStruct((S, LANES), F32), jax.ShapeDtypeStruct((N_PAIRS, 2, S), F32)],
        scratch_shapes=[pltpu.VMEM((S, W), F32), pltpu.VMEM((t, W), F32), pltpu.VMEM((t, W), F32)],
        compiler_params=pltpu.CompilerParams(dimension_semantics=("arbitrary", "arbitrary"), vmem_limit_bytes=FOX_BWD_VMEM),
        name=name)(qkv, qkv, qkv, kbias, do, o, lse)


def _view_spec(tm, R, index=lambda i: (i, 0)):
    return pl.BlockSpec((tm // R, R * D_MODEL), index)


def _matmul_nt_views(name, a, w, dils, tm=512):
    S, K = a.shape

    def body(a_ref, w_ref, *rest):
        res = _dot(a_ref[...].astype(BF16), w_ref[...], NT)
        _write_views([res[:, b * LANES:(b + 1) * LANES] for b in range(N_PAIRS)], rest[-1], rest[:-1], dils, tm)

    return pl.pallas_call(
        body, grid=(S // tm,), in_specs=[pl.BlockSpec((tm, K), lambda i: (i, 0)), pl.BlockSpec((D_MODEL, K), lambda i: (0, 0))],
        out_specs=[_view_spec(tm, R) for R in dils],
        out_shape=[jax.ShapeDtypeStruct((S // R, R * D_MODEL), BF16) for R in dils],
        scratch_shapes=[pltpu.VMEM((N_PAIRS, tm, LANES), F32)], compiler_params=_params("parallel"), name=name)(a, w)


def _write_views(chunks, scr, out_refs, dils, tm):
    if any(R > 1 for R in dils):
        _stage_chunks(scr, chunks)
    for ref, R in zip(out_refs, dils):
        for b, x in enumerate(chunks):
            if R == 1:
                ref[:, b * LANES:(b + 1) * LANES] = x.astype(ref.dtype)
                continue
            for r in range(R):
                col = r * D_MODEL + b * LANES
                ref[:, col:col + LANES] = _strided_rows(scr, b, r, tm // R, R).astype(ref.dtype)


def _combine(name, os_, lses, dils, tm=256):
    S = os_[0].shape[0] * dils[0]
    G = len(dils)

    def body(*refs):
        o_refs, l_refs = refs[:G], refs[G:2 * G]
        o_outs, l_outs = refs[2 * G:3 * G], refs[3 * G:4 * G]
        stage = refs[4 * G:]
        for g, R in enumerate(dils):
            if R == 1:
                continue
            for src, dst in ((o_refs[g], stage[2 * g]), (l_refs[g], stage[2 * g + 1])):
                for r in range(R):
                    for b in range(N_PAIRS):
                        col = r * D_MODEL + b * LANES
                        dst.at[b][pl.ds(r, tm // R, stride=R), :] = src[:, col:col + LANES]
        o_chunks, l_chunks = [], []
        for b in range(N_PAIRS):
            cols = slice(b * LANES, (b + 1) * LANES)
            os_b = [o_refs[g][:, cols] if R == 1 else stage[2 * g][b] for g, R in enumerate(dils)]
            ls = [l_refs[g][:, cols] if R == 1 else stage[2 * g + 1][b] for g, R in enumerate(dils)]
            m = functools.reduce(jnp.maximum, ls)
            ws = [jnp.exp(l - m) for l in ls]
            den = functools.reduce(jnp.add, ws)
            o_chunks.append(functools.reduce(jnp.add, [w * o for w, o in zip(ws, os_b)]) / den)
            l_chunks.append(m + jnp.log(den))
        _write_views(o_chunks, stage[0], o_outs, dils, tm)
        _write_views(l_chunks, stage[1], l_outs, dils, tm)

    specs = [_view_spec(tm, R) for R in dils]
    shapes = lambda dt: [jax.ShapeDtypeStruct((S // R, R * D_MODEL), dt) for R in dils]
    res = pl.pallas_call(
        body, grid=(S // tm,), in_specs=specs * 2, out_specs=specs * 2, out_shape=shapes(BF16) + shapes(F32),
        scratch_shapes=[pltpu.VMEM((N_PAIRS, tm, LANES), F32)] * (2 * G), compiler_params=_params("parallel"),
        name=name)(*os_, *lses)
    return res[:G], res[G:]


def _tri_matmul(tri, x):
    hi, mid, lo = _split3(x)
    return _dot(tri, hi, NN) + _dot(tri, mid, NN) + _dot(tri, lo, NN)


def _split3(x):
    hi = x.astype(BF16)
    r1 = x - hi.astype(F32)
    mid = r1.astype(BF16)
    return hi, mid, (r1 - mid.astype(F32)).astype(BF16)


def _gate_fwd(name, z, bf, tb=512):
    S = z.shape[0]

    def body(z_ref, b_ref, kb_ref, carry):
        @pl.when(pl.program_id(0) == 0)
        def _():
            carry[...] = jnp.zeros_like(carry)

        lf = jax.nn.log_sigmoid(z_ref[...] + b_ref[...])
        ri = lax.broadcasted_iota(jnp.int32, (tb, tb), 0)
        ci = lax.broadcasted_iota(jnp.int32, (tb, tb), 1)
        tri = (ci <= ri).astype(BF16)
        c = _tri_matmul(tri, lf) + carry[...]
        carry[...] = c[tb - 1:tb, :]
        head = lax.broadcasted_iota(jnp.int32, (LANES, D_MODEL), 0)
        col = lax.broadcasted_iota(jnp.int32, (LANES, D_MODEL), 1)
        base = (head >> 1) * LANES + jnp.where((head & 1) == 0, HEAD_DIM, 0)
        kb = jnp.zeros((tb, D_MODEL), F32)
        for i, piece in enumerate(_split3(-c)):
            place = ((col == base + i) & (head < N_HEADS)).astype(BF16)
            kb = kb + _dot(piece, place, NN)
        kb_ref[...] = kb.astype(BF16)

    row = pl.BlockSpec((tb, LANES), lambda i: (i, 0))
    return pl.pallas_call(
        body, grid=(S // tb,), in_specs=[row, pl.BlockSpec((1, LANES), lambda i: (0, 0))],
        out_specs=pl.BlockSpec((tb, D_MODEL), lambda i: (i, 0)), out_shape=jax.ShapeDtypeStruct((S, D_MODEL), BF16),
        scratch_shapes=[pltpu.VMEM((1, LANES), F32)], compiler_params=_params("arbitrary"), name=name)(z, bf)


def _gate_bwd(name, dc, z, bf, tb=512):
    S = z.shape[0]
    nb = S // tb

    def body(dc_ref, z_ref, b_ref, dz_ref, db_ref, carry):
        @pl.when(pl.program_id(0) == 0)
        def _():
            carry[...] = jnp.zeros_like(carry)
            db_ref[...] = jnp.zeros_like(db_ref)

        ri = lax.broadcasted_iota(jnp.int32, (tb, tb), 0)
        ci = lax.broadcasted_iota(jnp.int32, (tb, tb), 1)
        tri = (ci >= ri).astype(BF16)
        dlf = _tri_matmul(tri, dc_ref[...]) + carry[...]
        carry[...] = dlf[0:1, :]
        dz = dlf * jax.nn.sigmoid(-(z_ref[...] + b_ref[...]))
        dz_ref[...] = dz
        db_ref[...] += jnp.sum(dz, axis=0, keepdims=True)

    row = pl.BlockSpec((tb, LANES), lambda i: (nb - 1 - i, 0))
    vec = pl.BlockSpec((1, LANES), lambda i: (0, 0))
    return pl.pallas_call(
        body, grid=(nb,), in_specs=[row, row, vec], out_specs=[row, vec],
        out_shape=[jax.ShapeDtypeStruct((S, LANES), F32), jax.ShapeDtypeStruct((1, LANES), F32)],
        scratch_shapes=[pltpu.VMEM((1, LANES), F32)], compiler_params=_params("arbitrary"), name=name)(dc, z, bf)


def _ffn_gu(name, n, wgu, comm=None, tm=1024):
    S, D = n.shape
    nb = N_DEV // 2

    def body(n_ref, wg_ref, wu_ref, gu_ref, act_ref):
        x = n_ref[...]
        g = _dot(x, wg_ref[...], NN)
        u = _dot(x, wu_ref[...], NN)
        gu_ref[0] = g.astype(BF16)
        gu_ref[1] = u.astype(BF16)
        act_ref[...] = (g * jax.nn.sigmoid(g) * u).astype(BF16)

    return _call(
        name, body, (nb, S // tm),
        [pl.BlockSpec((tm, D), lambda j, i: (i, 0)), pl.BlockSpec((None, D, FF_BLK), lambda j, i: (j, 0, 0)),
         pl.BlockSpec((None, D, FF_BLK), lambda j, i: (j + nb, 0, 0))],
        [pl.BlockSpec((2, None, tm, FF_BLK), lambda j, i: (0, j, i, 0)), pl.BlockSpec((None, tm, FF_BLK), lambda j, i: (j, i, 0))],
        [jax.ShapeDtypeStruct((2, nb, S, FF_BLK), BF16), jax.ShapeDtypeStruct((nb, S, FF_BLK), BF16)], [],
        (n, wgu, wgu), ("parallel", "parallel"), comm)


def _ffn_down(name, act, wd, resid, comm=None, tm=1024):
    nb, S, _ = act.shape
    D = wd.shape[1]

    def epilogue(acc, ex, outs, j):
        outs[0][...] = acc + ex[0][...]

    o_spec = pl.BlockSpec((tm, D), lambda i, j, k: (i, 0))
    return _mm_call(name, (S // tm, 1, nb), act, pl.BlockSpec((None, tm, FF_BLK), lambda i, j, k: (k, i, 0)),
                    wd, pl.BlockSpec((FF_BLK, D), lambda i, j, k: (k, 0)), NN,
                    [jax.ShapeDtypeStruct((S, D), F32)], [o_spec], (tm, D), epilogue, (resid,), (o_spec,), comm=comm)


def _ffn_dact(name, dh, wd, gu, comm=None, tm=512):
    S, D = dh.shape
    nb = N_DEV // 2

    def epilogue(acc, ex, outs, j):
        g = ex[0][0].astype(F32)
        u = ex[0][1].astype(F32)
        sig = jax.nn.sigmoid(g)
        outs[0][0] = (acc * u * (sig * (1.0 + g * (1.0 - sig)))).astype(BF16)
        outs[0][1] = (acc * (g * sig)).astype(BF16)

    gu_spec = pl.BlockSpec((2, None, tm, FF_BLK), lambda j, i, k: (0, j, i, 0))
    return _mm_call(name, (nb, S // tm, 1), dh, pl.BlockSpec((tm, D), lambda j, i, k: (i, 0)),
                    wd, pl.BlockSpec((FF_BLK, D), lambda j, i, k: (j, 0)), NT,
                    [jax.ShapeDtypeStruct((2, nb, S, FF_BLK), BF16)], [gu_spec], (tm, FF_BLK), epilogue, (gu,), (gu_spec,),
                    col_axis=0, comm=comm)


def _ffn_dwgu(name, n, dgu, comm=None, tm=1024, tk=1024):
    S, D = n.shape
    dgu8 = dgu.reshape(N_DEV, S, FF_BLK)
    return _mm_call(name, (N_DEV, D // tm, S // tk), n, pl.BlockSpec((tk, tm), lambda d, i, k: (k, i)),
                    dgu8, pl.BlockSpec((None, tk, FF_BLK), lambda d, i, k: (d, k, 0)), TN,
                    [jax.ShapeDtypeStruct((N_DEV, D, FF_BLK), BF16)],
                    [pl.BlockSpec((None, tm, FF_BLK), lambda d, i, k: (d, i, 0))], (tm, FF_BLK), comm=comm)


def _ffn_dwd(name, act, dh, tk=1024):
    nb, S, _ = act.shape
    D = dh.shape[1]
    out = _mm_call(name, (nb, 1, S // tk), act, pl.BlockSpec((None, tk, FF_BLK), lambda b, j, k: (b, k, 0)),
                   dh, pl.BlockSpec((tk, D), lambda b, j, k: (k, 0)), TN,
                   [jax.ShapeDtypeStruct((nb, FF_BLK, D), BF16)],
                   [pl.BlockSpec((None, FF_BLK, D), lambda b, j, k: (b, 0, 0))], (FF_BLK, D))[0]
    return out.reshape(N_DEV, FF_BLK // 2, D)


def _ffn_dn(name, dgu, wgu, comm=None, tm=1024):
    S = dgu.shape[2]
    D = wgu.shape[1]
    dgu8 = dgu.reshape(N_DEV, S, FF_BLK)
    return _mm_call(name, (S // tm, 1, N_DEV), dgu8, pl.BlockSpec((None, tm, FF_BLK), lambda i, j, k: (k, i, 0)),
                    wgu, pl.BlockSpec((None, D, FF_BLK), lambda i, j, k: (k, 0, 0)), NT,
                    [jax.ShapeDtypeStruct((S, D), F32)], [pl.BlockSpec((tm, D), lambda i, j, k: (i, 0))], (tm, D), comm=comm)


def _adamw(name, parts, w, m, v, tr):
    rows, cols = w.shape
    n_parts = len(parts)
    c1 = 1.0 - ADAM_B1 ** ADAM_STEP
    c2 = 1.0 - ADAM_B2 ** ADAM_STEP

    def body(*refs):
        p_refs = refs[:n_parts]
        w_ref, m_ref, v_ref, g_ref, d_ref, nm_ref, nv_ref = refs[n_parts:]
        g = p_refs[0][...].astype(F32)
        for r in p_refs[1:]:
            g = g + r[...].astype(F32)
        mm = ADAM_B1 * m_ref[...] + (1.0 - ADAM_B1) * g
        vv = ADAM_B2 * v_ref[...] + (1.0 - ADAM_B2) * (g * g)
        g_ref[...] = g
        nm_ref[...] = mm
        nv_ref[...] = vv
        d_ref[...] = -ADAM_LR * ((mm / c1) / (jnp.sqrt(vv / c2) + ADAM_EPS) + ADAM_WD * w_ref[...])

    blk = pl.BlockSpec((tr, cols), lambda i: (i, 0))
    out = jax.ShapeDtypeStruct((rows, cols), F32)
    return pl.pallas_call(
        body, grid=(rows // tr,), in_specs=[blk] * (n_parts + 3), out_specs=[blk] * 4, out_shape=[out] * 4,
        compiler_params=_params("parallel"), name=name)(*parts, w, m, v)


def _position():
    return lax.axis_index("x"), lax.axis_index("y"), lax.axis_index("c")


def _other_chips():
    x, y, _ = _position()
    return [(1 - x, y), (x, 1 - y), (1 - x, 1 - y)]


def _remote(src, dst, send, recv, k, to):
    return pltpu.make_async_remote_copy(src_ref=src, dst_ref=dst, send_sem=send.at[k], recv_sem=recv.at[k],
                                        device_id=to, device_id_type=MESH)


def _ag_send(blocks, direct=False):
    n_peer = 7 if direct else 4

    def copies(ins, outs, send, recv, local, r0=0, l0=0):
        x, y, c = _position()
        me = 4 * x + 2 * y + c
        peers = [(x, y, 1 - c)] + [(px, py, c) for px, py in _other_chips()]
        if direct:
            peers += [(px, py, 1 - c) for px, py in _other_chips()]
        cps = []
        for t, (src, dst) in enumerate(zip(ins, outs)):
            cps.append(pltpu.make_async_copy(src, dst.at[me], local.at[l0 + t]))
            cps += [_remote(src, dst.at[me], send, recv, r0 + n_peer * t + k, to) for k, to in enumerate(peers)]
        return cps

    outs = tuple(jax.ShapeDtypeStruct((N_DEV,) + b.shape, b.dtype) for b in blocks)
    return _Comm(tuple(blocks), outs, {}, copies, n_peer * len(blocks), len(blocks))


def _ag_forward(bufs):
    def copies(ins, outs, send, recv, local, r0=0, l0=0):
        x, y, c = _position()
        cps = []
        for t, buf in enumerate(outs):
            for k, (px, py) in enumerate(_other_chips()):
                slot = buf.at[4 * px + 2 * py + c]
                cps.append(_remote(slot, slot, send, recv, r0 + 3 * t + k, (x, y, 1 - c)))
        return cps

    outs = tuple(jax.ShapeDtypeStruct(b.shape, b.dtype) for b in bufs)
    return _Comm(tuple(bufs), outs, {t: t for t in range(len(bufs))}, copies, 3 * len(bufs), 0)


def _rs_swap(shares):
    def copies(ins, outs, send, recv, local, r0=0, l0=0):
        x, y, c = _position()
        return [_remote(src.at[:, 1 - c], dst, send, recv, r0 + t, (x, y, 1 - c)) for t, (src, dst) in enumerate(zip(ins, outs))]

    ins = tuple(s.reshape((4, 2) + s.shape[1:]) for s in shares)
    outs = tuple(jax.ShapeDtypeStruct((4,) + s.shape[1:], s.dtype) for s in shares)
    return _Comm(ins, outs, {}, copies, len(shares), 0)


def _rs_exchange(sums):
    def copies(ins, outs, send, recv, local, r0=0, l0=0):
        _, _, c = _position()
        return [_remote(src.at[2 * px + py], dst.at[k], send, recv, r0 + 3 * t + k, (px, py, c))
                for t, (src, dst) in enumerate(zip(ins, outs)) for k, (px, py) in enumerate(_other_chips())]

    outs = tuple(jax.ShapeDtypeStruct((3,) + s.shape[1:], s.dtype) for s in sums)
    return _Comm(tuple(sums), outs, {}, copies, 3 * len(sums), 0)


def _comm_call(name, comm):
    return _call(name, lambda: None, (), [], [], [], [], (), (), comm)


def _pair_sum(name, share, got, core, tr):
    _, rows, cols = share.shape

    def body(c_ref, a_ref, b_ref, o_ref):
        o_ref[...] = (a_ref[...].astype(F32) + b_ref[...].astype(F32)).astype(o_ref.dtype)

    grid_spec = pltpu.PrefetchScalarGridSpec(
        num_scalar_prefetch=1, grid=(4, rows // tr),
        in_specs=[pl.BlockSpec((None, None, tr, cols), lambda q, i, c: (q, c[0], i, 0)),
                  pl.BlockSpec((None, tr, cols), lambda q, i, c: (q, i, 0))],
        out_specs=pl.BlockSpec((None, tr, cols), lambda q, i, c: (q, i, 0)))
    return pl.pallas_call(
        body, grid_spec=grid_spec, out_shape=jax.ShapeDtypeStruct((4, rows, cols), share.dtype),
        compiler_params=_params("parallel", "parallel"), name=name)(core, share.reshape(4, 2, rows, cols), got)


TENSORS = ("a_w_in", "a_w_out", "b_w_in", "b_w_out", "gu0", "gu1", "dn0", "dn1")
ROW_TILE = {"a_w_in": 256, "a_w_out": 128, "b_w_in": 256, "b_w_out": 128, "gu0": 256, "gu1": 256, "dn0": 176, "dn1": 176}
A_BLK = 9 * D_MODEL // N_DEV
B_BLK = 386
B_IN = 3 * D_MODEL + N_HEADS
B_IN_PAD = 3 * D_MODEL + LANES


def kernel(x, a_norm, a_w_in, a_w_out, b_norm, b_w_in, b_f, b_w_out, ffn_norm, ffn_w_gu, ffn_w_down, final_norm, loss_target, m_a_norm, m_a_w_in, m_a_w_out, m_b_norm, m_b_w_in, m_b_f, m_b_w_out, m_ffn_norm, m_ffn_w_gu, m_ffn_w_down, m_final_norm, v_a_norm, v_a_w_in, v_a_w_out, v_b_norm, v_b_w_in, v_b_f, v_b_w_out, v_ffn_norm, v_ffn_w_gu, v_ffn_w_down, v_final_norm):
    S = x.shape[1]
    xi, yi, ci = _position()
    dev = 4 * xi + 2 * yi + ci
    core = ci.reshape(1).astype(jnp.int32)
    h0, target = x.reshape(S, D_MODEL), loss_target.reshape(S, D_MODEL)

    def shards(a_in, a_out, b_in, b_out, gu, dn):
        return {"a_w_in": a_in[0], "a_w_out": a_out[0], "b_w_in": b_in[0], "b_w_out": b_out[0],
                "gu0": gu[0], "gu1": gu[1], "dn0": dn[0], "dn1": dn[1]}

    w_sh = shards(a_w_in, a_w_out, b_w_in, b_w_out, ffn_w_gu, ffn_w_down)
    m_sh = shards(m_a_w_in, m_a_w_out, m_b_w_in, m_b_w_out, m_ffn_w_gu, m_ffn_w_down)
    v_sh = shards(v_a_w_in, v_a_w_out, v_b_w_in, v_b_w_out, v_ffn_w_gu, v_ffn_w_down)
    wb = {n: w_sh[n].astype(BF16) for n in TENSORS}
    bf_pad = jnp.pad(b_f, ((0, 0), (0, LANES - N_HEADS)))
    tabs = _rope_tables(S)

    g_ain, g_aout = _comm_call("gather_a", _ag_send([wb["a_w_in"], wb["a_w_out"]]))
    g_ain, g_aout = _comm_call("forward_a", _ag_forward([g_ain, g_aout]))
    n0 = _rms_fwd("rms_a", h0, a_norm[0])
    later = [wb["b_w_in"], wb["b_w_out"], wb["gu0"], wb["dn0"], jnp.pad(b_norm, ((0, 7), (0, 0)))]
    w_a_in = g_ain.transpose(1, 0, 2).reshape(D_MODEL, 9 * D_MODEL)
    qkv_a, later = _a_proj("proj_a", n0, w_a_in, tabs, _ag_send(later))
    cols = [lambda r: r] * 3
    groups = [(g, dil, S // dil, qkv_a[g]) for g, (window, dil) in enumerate(DILATED_PATTERNS)]
    dils = [dil for _, dil in DILATED_PATTERNS]
    fwd = [_dil_fwd("dil_fwd%d" % g, view, *cols, dil, L) for g, dil, L, view in groups]
    o_views, lse_views = _combine("dil_combine", [f[0] for f in fwd], [f[1] for f in fwd], dils)
    o_a = o_views[0]
    w_a_out = g_aout.reshape(D_MODEL, D_MODEL)
    h1, (g_bin, g_bout, g_gu0, g_dn0, g_bnorm) = _matmul("out_a", o_a, w_a_out, "nn", F32, TM, 1024, 1024, resid=h0,
                                                         comm=_ag_forward(later))

    n1 = _rms_fwd("rms_f0", h1, ffn_norm[0])
    gu0, act0, g_gu1 = _ffn_gu("gu_f0", n1, g_gu0, _ag_send([wb["gu1"]]))
    w_dn0 = g_dn0.reshape(D_FF, D_MODEL)
    h2, g_dn1 = _ffn_down("down_f0", act0, w_dn0, h1, _ag_send([wb["dn1"]]))

    b_norm_full = g_bnorm[:, 0].reshape(D_MODEL)
    w_b_in = g_bin.transpose(1, 0, 2).reshape(D_MODEL, B_IN)
    w_b_gate = jnp.pad(w_b_in[:, 3 * D_MODEL:], ((0, 0), (0, LANES - N_HEADS)))
    w_b_cat = jnp.concatenate([w_b_in[:, :3 * D_MODEL], w_b_gate], axis=1)
    w_b_out = g_bout.reshape(D_MODEL, D_MODEL)
    n2 = _rms_fwd("rms_b", h2, b_norm_full)
    qkv, (g_gu1, g_dn1) = _matmul("proj_b", n2, w_b_in[:, :3 * D_MODEL], "nn", BF16, TM, 1024, 1024, col0_scale=SOFTMAX_SCALE,
                                  comm=_ag_forward([g_gu1, g_dn1]))
    z = _matmul("gate_b", n2, w_b_gate, "nn", F32, TM, LANES, 1024)
    kbias = _gate_fwd("gate_cumsum", z, bf_pad)
    tf = min(S, 512)
    o_b, lse_b = _fox_fwd("fox_fwd", qkv, kbias, tf)
    h3 = _matmul("out_b", o_b, w_b_out, "nn", F32, TM, 1024, 1024, resid=h2)

    w_dn1 = g_dn1.reshape(D_FF, D_MODEL)
    n3 = _rms_fwd("rms_f1", h3, ffn_norm[1])
    gu1, act1 = _ffn_gu("gu_f1", n3, g_gu1)
    h4 = _ffn_down("down_f1", act1, w_dn1, h3)[0]

    dh4, d_final, loss = _loss_head("loss_head", h4, final_norm, target)

    share, got, sums, others = {}, {}, {}, {}

    def pair_sums(*names):
        for n in names:
            sums[n] = _pair_sum("pair_" + n, share[n], got[n], core, ROW_TILE[n])

    dgu1 = _ffn_dact("dact_f1", dh4, w_dn1, gu1)[0]
    share["dn1"] = _ffn_dwd("dwd_f1", act1, dh4)
    share["gu1"] = _ffn_dwgu("dwgu_f1", n3, dgu1)[0]
    dn3, got["gu1"], got["dn1"] = _ffn_dn("dn_f1", dgu1, g_gu1, _rs_swap([share["gu1"], share["dn1"]]))
    dh3, d_ffn1 = _rms_bwd("rmsb_f1", dn3, h3, ffn_norm[1], dh4)
    pair_sums("gu1", "dn1")

    do_b = _matmul("dout_b", dh3, w_b_out, "nt", BF16, TM, 1024, 1024)
    share["b_w_out"] = _matmul("dwout_b", o_b, dh3, "tn", BF16, TM, 1024, 1024).reshape(N_DEV, 128, D_MODEL)
    dq_b, dk_b, dv_b, ds_rowsum, ds_colsum = _fox_bwd("fox_bwd", qkv, kbias, do_b, o_b, lse_b, tf)
    dc = ds_rowsum[:, :N_HEADS] - ds_colsum.reshape(N_HEADS, S).T
    dz, d_bf = _gate_bwd("gate_bwd", jnp.pad(dc, ((0, 0), (0, LANES - N_HEADS))), z, bf_pad)
    dproj_b = jnp.concatenate([dq_b, dk_b, dv_b, dz.astype(BF16)], axis=1)
    dw_b_in, (others["gu1"],) = _matmul("dwin_b", n2, dproj_b, "tn", BF16, TM, B_IN_PAD // 5, 1024, comm=_rs_exchange([sums["gu1"]]))
    dn2, (others["dn1"],) = _matmul("dn_b", dproj_b, w_b_cat, "nt", F32, TM, 1024, B_IN_PAD // 5, comm=_rs_exchange([sums["dn1"]]))
    dh2, d_bnorm = _rms_bwd("rmsb_b", dn2, h2, b_norm_full, dh3)
    share["b_w_in"] = dw_b_in[:, :B_IN].reshape(D_MODEL, N_DEV, B_BLK).transpose(1, 0, 2)

    dgu0, got["b_w_in"], got["b_w_out"] = _ffn_dact("dact_f0", dh2, w_dn0, gu0, _rs_swap([share["b_w_in"], share["b_w_out"]]))
    share["dn0"] = _ffn_dwd("dwd_f0", act0, dh2)
    pair_sums("b_w_in", "b_w_out")
    share["gu0"], others["b_w_in"], others["b_w_out"] = _ffn_dwgu(
        "dwgu_f0", n1, dgu0, _rs_exchange([sums["b_w_in"], sums["b_w_out"]]))
    dn1, got["gu0"], got["dn0"] = _ffn_dn("dn_f0", dgu0, g_gu0, _rs_swap([share["gu0"], share["dn0"]]))
    dh1, d_ffn0 = _rms_bwd("rmsb_f0", dn1, h1, ffn_norm[0], dh2)
    pair_sums("gu0", "dn0")

    do_views = _matmul_nt_views("dout_a", dh1, w_a_out, dils)
    share["a_w_out"] = _matmul("dwout_a", o_a, dh1, "tn", BF16, TM, 1024, 1024).reshape(N_DEV, 128, D_MODEL)
    pieces = []
    for g, dil, L, view in groups:
        rot = tuple(tb.reshape(L, dil * LANES) for tb in tabs)
        grads = _dil_bwd("dil_bwd%d" % g, view, do_views[g], o_views[g], lse_views[g], rot, dil, L)
        pieces += list(grads)
    dw_a_in, others["gu0"], others["dn0"] = _a_dw("dwin_a", n0, pieces, dils, _rs_exchange([sums["gu0"], sums["dn0"]]))
    share["a_w_in"] = dw_a_in.reshape(D_MODEL, N_DEV, A_BLK).transpose(1, 0, 2)
    got["a_w_in"], got["a_w_out"] = _comm_call("swap_a", _rs_swap([share["a_w_in"], share["a_w_out"]]))
    pair_sums("a_w_in", "a_w_out")
    dn0, others["a_w_in"], others["a_w_out"] = _a_dn("dn_a", pieces, dils, w_a_in, _rs_exchange([sums["a_w_in"], sums["a_w_out"]]))
    dx, d_anorm = _rms_bwd("rmsb_a", dn0, h0, a_norm[0], dh1)

    misc = jnp.concatenate([d_bf[:, :N_HEADS], loss[:, :1], jnp.zeros((1, D_MODEL - N_HEADS - 1), F32)], axis=1)
    small = jnp.concatenate([d_anorm, d_ffn0, d_ffn1, d_final, d_bnorm, misc, jnp.zeros((2, D_MODEL), F32)], axis=0)
    small_all, = _comm_call("gather_small", _ag_send([small], direct=True))

    outs = {}
    for n in TENSORS:
        mine = lax.dynamic_index_in_dim(sums[n], 2 * xi + yi, axis=0, keepdims=False)
        outs[n] = _adamw("adamw_" + n, [mine] + [others[n][k] for k in range(3)], w_sh[n], m_sh[n], v_sh[n], ROW_TILE[n])

    pad_vec = lambda a: jnp.pad(a, ((0, 0), (0, D_MODEL - a.shape[1])))

    def small_pack(an, fn, fin, bf):
        return jnp.concatenate([an, fn, fin.reshape(1, D_MODEL), jnp.zeros((1, D_MODEL), F32), pad_vec(bf),
                                jnp.zeros((2, D_MODEL), F32)], axis=0)

    sg, sd, sm, sv = _adamw("adamw_small", [small_all[d] for d in range(N_DEV)], small_pack(a_norm, ffn_norm, final_norm, b_f),
                            small_pack(m_a_norm, m_ffn_norm, m_final_norm, m_b_f),
                            small_pack(v_a_norm, v_ffn_norm, v_final_norm, v_b_f), 8)
    g_bn = lax.dynamic_slice(sg[4:5], (0, dev * LANES), (1, LANES))
    bn = _adamw("adamw_b_norm", [g_bn], b_norm, m_b_norm, v_b_norm, 1)

    def tree(i):
        full = lambda name, ref: outs[name][i].reshape(ref.shape)
        sml = (sg, sd, sm, sv)[i]
        return dict(
            a_norm=sml[0:1], a_w_in=full("a_w_in", a_w_in), a_w_out=full("a_w_out", a_w_out), b_norm=bn[i],
            b_w_in=full("b_w_in", b_w_in), b_f=sml[5:6, :N_HEADS], b_w_out=full("b_w_out", b_w_out), ffn_norm=sml[1:3],
            ffn_w_gu=jnp.stack([outs["gu0"][i], outs["gu1"][i]]).reshape(ffn_w_gu.shape),
            ffn_w_down=jnp.stack([outs["dn0"][i], outs["dn1"][i]]).reshape(ffn_w_down.shape), final_norm=sml[3])

    order = ("a_norm", "a_w_in", "a_w_out", "b_norm", "b_w_in", "b_f", "b_w_out", "ffn_norm", "ffn_w_gu", "ffn_w_down", "final_norm")
    result = [sg[5, N_HEADS], dx.reshape(x.shape)]
    for i in range(4):
        t = tree(i)
        result += [t[n] for n in order]
    return tuple(result)
```

```python
import functools
from typing import Callable, NamedTuple

import jax
import jax.numpy as jnp
from jax import lax
from jax.experimental import pallas as pl
from jax.experimental.pallas import tpu as pltpu

F32 = jnp.float32
BF16 = jnp.bfloat16

D_MODEL = 1024
N_HEADS = 16
HEAD_DIM = 64
N_PAIRS = N_HEADS // 2
LANES = 128
DILATED_PATTERNS = ((128, 1), (512, 4), (2048, 16))
BAND_STEPS = 128
ROT_DIM = HEAD_DIM // 4
ROPE_THETA = 500000.0
D_FF = 2816
RMS_EPS = 1e-6
NEG_INF = -1e30
SOFTMAX_SCALE = HEAD_DIM ** -0.5
N_DEV = 8
FF_BLK = 2 * D_FF // N_DEV
ADAM_LR, ADAM_B1, ADAM_B2, ADAM_EPS, ADAM_WD, ADAM_STEP = 0.001, 0.9, 0.999, 1e-08, 0.01, 10
VMEM_LIMIT = 52 * 1024 * 1024
FOX_BWD_VMEM = 60 * 1024 * 1024
TM = 1024
MESH = pl.DeviceIdType.MESH

NN = (((1,), (0,)), ((), ()))
NT = (((1,), (1,)), ((), ()))
TN = (((0,), (0,)), ((), ()))


def _params(*sem):
    return pltpu.CompilerParams(dimension_semantics=sem, vmem_limit_bytes=VMEM_LIMIT)


def _dot(a, b, dims):
    return lax.dot_general(a, b, dims, preferred_element_type=F32)


class _Comm(NamedTuple):
    ins: tuple
    outs: tuple
    aliases: dict
    copies: Callable
    n_remote: int
    n_local: int


def _call(name, body, grid, in_specs, out_specs, out_shape, scratch, args, sem, comm=None):
    if comm is None:
        return pl.pallas_call(body, grid=grid, in_specs=in_specs, out_specs=out_specs, out_shape=out_shape,
                              scratch_shapes=scratch, compiler_params=_params(*sem), name=name)(*args)
    n_in, n_out = len(in_specs), len(out_specs)
    n_ci, n_co = len(comm.ins), len(comm.outs)
    o0 = n_in + n_ci

    def hosted(*refs):
        c_ins, c_outs = refs[n_in:o0], refs[o0 + n_out:o0 + n_out + n_co]
        sems = refs[-3:]

        def start():
            for cp in comm.copies(c_ins, c_outs, *sems):
                cp.start()

        def wait():
            for cp in comm.copies(c_ins, c_outs, *sems):
                cp.wait()

        if not grid:
            start()
            body()
            wait()
            return
        ids = [pl.program_id(ax) for ax in range(len(grid))]
        pl.when(functools.reduce(jnp.logical_and, [i == 0 for i in ids]))(start)
        body(*refs[:n_in], *refs[o0:o0 + n_out], *refs[o0 + n_out + n_co:-3])
        pl.when(functools.reduce(jnp.logical_and, [i == g - 1 for i, g in zip(ids, grid)]))(wait)

    hbm = pl.BlockSpec(memory_space=pltpu.HBM)
    dma = pltpu.SemaphoreType.DMA
    return pl.pallas_call(
        hosted, grid=grid, in_specs=[*in_specs, *[hbm] * n_ci], out_specs=[*out_specs, *[hbm] * n_co],
        out_shape=[*out_shape, *comm.outs], input_output_aliases={n_in + i: n_out + o for i, o in comm.aliases.items()},
        scratch_shapes=[*scratch, dma((comm.n_remote,)), dma((comm.n_remote,)), dma((max(comm.n_local, 1),))],
        compiler_params=_params(*["arbitrary"] * len(grid)), name=name)(*args, *comm.ins)


def _mm_call(name, grid, a, a_spec, b, b_spec, dims, out_shapes, out_specs, acc_shape, epilogue=None,
             extras=(), extra_specs=(), col_axis=1, comm=None):
    nk = grid[2]
    n_extra = len(extras)
    n_out = len(out_shapes)

    def finish(res, ex, outs, j):
        if epilogue is None:
            outs[0][...] = res.astype(outs[0].dtype)
        else:
            epilogue(res, ex, outs, j)

    def body(*refs):
        a_ref, b_ref = refs[0], refs[1]
        ex = refs[2:2 + n_extra]
        outs = refs[2 + n_extra:2 + n_extra + n_out]
        j, k = pl.program_id(col_axis), pl.program_id(2)
        part = _dot(a_ref[...].astype(BF16), b_ref[...].astype(BF16), dims)
        if nk == 1:
            finish(part, ex, outs, j)
            return
        acc = refs[-1]

        @pl.when(k == 0)
        def _():
            acc[...] = part

        @pl.when((k > 0) & (k < nk - 1))
        def _():
            acc[...] += part

        @pl.when(k == nk - 1)
        def _():
            finish(acc[...] + part, ex, outs, j)

    return _call(name, body, grid, [a_spec, b_spec, *extra_specs], out_specs, out_shapes,
                 [] if nk == 1 else [pltpu.VMEM(acc_shape, F32)], (a, b, *extras), ("parallel", "parallel", "arbitrary"), comm)


def _matmul(name, a, b, mode, out_dtype, tm, tn, tk, resid=None, col0_scale=None, comm=None):
    if mode == "nn":
        (M, K), N = a.shape, b.shape[1]
        a_spec = pl.BlockSpec((tm, tk), lambda j, i, k: (i, k))
        b_spec = pl.BlockSpec((tk, tn), lambda j, i, k: (k, j))
        dims = NN
    elif mode == "nt":
        (M, K), N = a.shape, b.shape[0]
        a_spec = pl.BlockSpec((tm, tk), lambda j, i, k: (i, k))
        b_spec = pl.BlockSpec((tn, tk), lambda j, i, k: (j, k))
        dims = NT
    else:
        (K, M), N = a.shape, b.shape[1]
        a_spec = pl.BlockSpec((tk, tm), lambda j, i, k: (k, i))
        b_spec = pl.BlockSpec((tk, tn), lambda j, i, k: (k, j))
        dims = TN
    assert M % tm == 0 and N % tn == 0 and K % tk == 0, (name, M, N, K, tm, tn, tk)
    o_spec = pl.BlockSpec((tm, tn), lambda j, i, k: (i, j))
    extras, extra_specs, epilogue = (), (), None
    if resid is not None:
        extras, extra_specs = (resid,), (o_spec,)

        def epilogue(acc, ex, outs, j):
            outs[0][...] = (acc + ex[0][...]).astype(outs[0].dtype)

    elif col0_scale is not None:

        def epilogue(acc, ex, outs, j):
            outs[0][...] = (acc * jnp.where(j == 0, col0_scale, 1.0)).astype(outs[0].dtype)

    res = _mm_call(name, (N // tn, M // tm, K // tk), a, a_spec, b, b_spec, dims, [jax.ShapeDtypeStruct((M, N), out_dtype)],
                   [o_spec], (tm, tn), epilogue, extras, extra_specs, col_axis=0, comm=comm)
    return res[0] if comm is None else (res[0], res[1:])


def _rms_fwd(name, h, gain, tm=512):
    S, D = h.shape

    def body(h_ref, g_ref, n_ref):
        x = h_ref[...]
        rstd = lax.rsqrt(jnp.mean(x * x, axis=-1, keepdims=True) + RMS_EPS)
        n_ref[...] = (x * rstd * g_ref[...]).astype(BF16)

    return pl.pallas_call(
        body, grid=(S // tm,), in_specs=[pl.BlockSpec((tm, D), lambda i: (i, 0)), pl.BlockSpec((1, D), lambda i: (0, 0))],
        out_specs=pl.BlockSpec((tm, D), lambda i: (i, 0)), out_shape=jax.ShapeDtypeStruct((S, D), BF16),
        compiler_params=_params("parallel"), name=name)(h, gain.reshape(1, D))


def _rms_bwd(name, dn, h, gain, dres, tm=512):
    S, D = h.shape

    def body(dn_ref, h_ref, g_ref, r_ref, dh_ref, dg_ref):
        x = h_ref[...]
        rstd = lax.rsqrt(jnp.mean(x * x, axis=-1, keepdims=True) + RMS_EPS)
        xhat = x * rstd
        d = dn_ref[...]
        dxhat = d * g_ref[...]
        dh_ref[...] = rstd * (dxhat - xhat * jnp.mean(dxhat * xhat, axis=-1, keepdims=True)) + r_ref[...]

        @pl.when(pl.program_id(0) == 0)
        def _():
            dg_ref[...] = jnp.zeros_like(dg_ref)

        dg_ref[...] += jnp.sum(d * xhat, axis=0, keepdims=True)

    row = pl.BlockSpec((tm, D), lambda i: (i, 0))
    vec = pl.BlockSpec((1, D), lambda i: (0, 0))
    return pl.pallas_call(
        body, grid=(S // tm,), in_specs=[row, row, vec, row], out_specs=[row, vec],
        out_shape=[jax.ShapeDtypeStruct((S, D), F32), jax.ShapeDtypeStruct((1, D), F32)],
        compiler_params=_params("arbitrary"), name=name)(dn, h, gain.reshape(1, D), dres)


def _loss_head(name, h, gain, target, tm=512):
    S, D = h.shape

    def body(h_ref, g_ref, t_ref, dh_ref, dg_ref, loss_ref):
        x = h_ref[...]
        rstd = lax.rsqrt(jnp.mean(x * x, axis=-1, keepdims=True) + RMS_EPS)
        xhat = x * rstd
        err = xhat * g_ref[...] - t_ref[...]
        dy = err * (1.0 / D)
        dxhat = dy * g_ref[...]
        dh_ref[...] = rstd * (dxhat - xhat * jnp.mean(dxhat * xhat, axis=-1, keepdims=True))

        @pl.when(pl.program_id(0) == 0)
        def _():
            dg_ref[...] = jnp.zeros_like(dg_ref)
            loss_ref[...] = jnp.zeros_like(loss_ref)

        dg_ref[...] += jnp.sum(dy * xhat, axis=0, keepdims=True)
        part = 0.5 * jnp.sum(jnp.mean(err * err, axis=-1, keepdims=True), axis=0, keepdims=True)
        loss_ref[...] += jnp.broadcast_to(part, loss_ref.shape)

    row = pl.BlockSpec((tm, D), lambda i: (i, 0))
    vec = pl.BlockSpec((1, D), lambda i: (0, 0))
    return pl.pallas_call(
        body, grid=(S // tm,), in_specs=[row, vec, row], out_specs=[row, vec, pl.BlockSpec((1, LANES), lambda i: (0, 0))],
        out_shape=[jax.ShapeDtypeStruct((S, D), F32), jax.ShapeDtypeStruct((1, D), F32),
                   jax.ShapeDtypeStruct((1, LANES), F32)],
        compiler_params=_params("arbitrary"), name=name)(h, gain.reshape(1, D), target)


def _rope_tables(S):
    half = ROT_DIM // 2
    inv_freq = ROPE_THETA ** (-jnp.arange(half, dtype=F32) * 2.0 / ROT_DIM)
    ang = jnp.arange(S, dtype=F32)[:, None] * inv_freq[None, :]
    cos, sin = jnp.cos(ang), jnp.sin(ang)
    one = jnp.ones((S, HEAD_DIM - ROT_DIM), F32)
    zero = jnp.zeros((S, HEAD_DIM - ROT_DIM), F32)
    zh = jnp.zeros((S, half), F32)
    c = jnp.concatenate([cos, cos, one], axis=1)
    sa = jnp.concatenate([-sin, zh, zero], axis=1)
    sb = jnp.concatenate([zh, sin, zero], axis=1)
    return tuple(jnp.concatenate([t, t], axis=1) for t in (c, sa, sb))


def _rotate(x, c, sa, sb, sign):
    return x * c + sign * (pltpu.roll(x, LANES - ROT_DIM // 2, 1) * sa + pltpu.roll(x, ROT_DIM // 2, 1) * sb)


def _stage_chunks(scr, chunks):
    for c, x in enumerate(chunks):
        scr[c] = x


def _strided_rows(scr, c, r, n, R):
    return scr.at[c][pl.ds(r, n, stride=R), :]


def _a_proj(name, n, w, tabs, comm, tm=512):
    S, D = n.shape
    n_i = S // tm
    dils = [dil for _, dil in DILATED_PATTERNS]
    n_out = 3 * len(dils)

    def body(n_ref, w_ref, c_ref, sa_ref, sb_ref, *rest):
        outs, scr = rest[:n_out], rest[n_out]
        j = pl.program_id(0)
        acc = _dot(n_ref[...], w_ref[...], NN)
        c, sa, sb = c_ref[...], sa_ref[...], sb_ref[...]
        for J in range(n_out):
            R, kind = dils[J // 3], J % 3

            @pl.when(j == J)
            def _(J=J, R=R, kind=kind):
                chunks = [acc[:, b * LANES:(b + 1) * LANES] for b in range(N_PAIRS)]
                if kind < 2:
                    chunks = [_rotate(x, c, sa, sb, 1.0) * (SOFTMAX_SCALE if kind == 0 else 1.0) for x in chunks]
                if R == 1:
                    for b, x in enumerate(chunks):
                        outs[J][:, b * LANES:(b + 1) * LANES] = x.astype(BF16)
                    return
                _stage_chunks(scr, chunks)
                for r in range(R):
                    for b in range(N_PAIRS):
                        col = r * D_MODEL + b * LANES
                        outs[J][:, col:col + LANES] = _strided_rows(scr, b, r, tm // R, R).astype(BF16)

    def out_spec(J, R):
        return pl.BlockSpec((tm // R, R * D_MODEL), lambda j, i: (jnp.where(j == J, i, jnp.where(j < J, 0, n_i - 1)), 0))

    tab = pl.BlockSpec((tm, LANES), lambda j, i: (i, 0))
    res = _call(name, body, (n_out, n_i),
                [pl.BlockSpec((tm, D), lambda j, i: (i, 0)), pl.BlockSpec((D, D_MODEL), lambda j, i: (0, j)), tab, tab, tab],
                [out_spec(J, dils[J // 3]) for J in range(n_out)],
                [jax.ShapeDtypeStruct((S // dils[J // 3], dils[J // 3] * D_MODEL), BF16) for J in range(n_out)],
                [pltpu.VMEM((N_PAIRS, tm, LANES), F32)], (n, w, *tabs), ("arbitrary", "arbitrary"), comm)
    return [res[3 * g:3 * g + 3] for g in range(len(dils))], res[n_out:]


def _unstride(src_chunk, R, stage, tok, rows):
    n = rows // R
    for r in range(R):
        for b in range(N_PAIRS):
            stage.at[b][r * n:(r + 1) * n, :] = src_chunk(r, b).astype(F32)
    for s in range(n):
        for b in range(N_PAIRS):
            tok.at[b][s * R:(s + 1) * R, :] = stage.at[b][pl.ds(s, R, stride=n), :]


def _token_order(ref, R, stage, tok, rows):
    if R == 1:
        return ref[...]
    _unstride(lambda r, b: ref[:, r * D_MODEL + b * LANES:r * D_MODEL + (b + 1) * LANES], R, stage, tok, rows)
    return jnp.concatenate([tok[b] for b in range(N_PAIRS)], axis=1).astype(ref.dtype)


def _a_dw(name, n, pieces, dils, comm, tk=512):
    S, D = n.shape
    n_k = S // tk
    n_p = len(pieces)

    def body(n_ref, *rest):
        p_refs, o_ref, acc, stage, tok = rest[:n_p], rest[n_p], rest[n_p + 1], rest[n_p + 2], rest[n_p + 3]
        j, k = pl.program_id(0), pl.program_id(1)
        for J in range(n_p):

            @pl.when(j == J)
            def _(J=J):
                part = _dot(n_ref[...], _token_order(p_refs[J], dils[J // 3], stage, tok, tk), TN)

                @pl.when(k == 0)
                def _():
                    acc[...] = part

                @pl.when(k > 0)
                def _():
                    acc[...] += part

        @pl.when(k == n_k - 1)
        def _():
            o_ref[...] = acc[...].astype(BF16)

    def piece_spec(J):
        R = dils[J // 3]
        return pl.BlockSpec((tk // R, R * D_MODEL), lambda j, k: (jnp.where(j == J, k, jnp.where(j < J, 0, n_k - 1)), 0))

    return _call(name, body, (n_p, n_k), [pl.BlockSpec((tk, D), lambda j, k: (k, 0))] + [piece_spec(J) for J in range(n_p)],
                 [pl.BlockSpec((D, D_MODEL), lambda j, k: (0, j))], [jax.ShapeDtypeStruct((D, n_p * D_MODEL), BF16)],
                 [pltpu.VMEM((D, D_MODEL), F32)] + [pltpu.VMEM((N_PAIRS, tk, LANES), F32)] * 2, (n, *pieces),
                 ("arbitrary", "arbitrary"), comm)


def _a_dn(name, pieces, dils, w, comm, tm=512):
    D = w.shape[0]
    S = pieces[0].shape[0] * dils[0]
    n_p = len(pieces)

    def body(*refs):
        p_refs, w_ref, o_ref, acc, stage, tok = refs[:n_p], refs[n_p], refs[n_p + 1], refs[n_p + 2], refs[n_p + 3], refs[n_p + 4]
        j = pl.program_id(1)
        for J in range(n_p):

            @pl.when(j == J)
            def _(J=J):
                part = _dot(_token_order(p_refs[J], dils[J // 3], stage, tok, tm), w_ref[...], NT)
                if J == 0:
                    acc[...] = part
                elif J < n_p - 1:
                    acc[...] += part
                else:
                    o_ref[...] = acc[...] + part

    specs = [pl.BlockSpec((tm // dils[J // 3], dils[J // 3] * D_MODEL), lambda i, j: (i, 0)) for J in range(n_p)]
    return _call(name, body, (S // tm, n_p), specs + [pl.BlockSpec((D, D_MODEL), lambda i, j: (0, j))],
                 [pl.BlockSpec((tm, D), lambda i, j: (i, 0))], [jax.ShapeDtypeStruct((S, D), F32)],
                 [pltpu.VMEM((tm, D), F32)] + [pltpu.VMEM((N_PAIRS, tm, LANES), F32)] * 2, (*pieces, w),
                 ("arbitrary", "arbitrary"), comm)


def _lo_lanes():
    return lax.broadcasted_iota(jnp.int32, (1, LANES), 1) < HEAD_DIM


def _rep_rows(x2, lo):
    sw = pltpu.roll(x2, HEAD_DIM, 1)
    return jnp.where(lo, x2, sw), jnp.where(lo, sw, x2)


def _pair_cols(h):
    return slice((h // 2) * LANES, (h // 2 + 1) * LANES)


def _head_lanes(lo, h):
    return lo if h % 2 == 0 else jnp.logical_not(lo)


def _band_masks(t, first):
    ri = lax.broadcasted_iota(jnp.int32, (t, t), 0)
    ci = lax.broadcasted_iota(jnp.int32, (t, t), 1)
    neg_prev = jnp.where((ci >= ri) & jnp.logical_not(first), 0.0, NEG_INF)
    neg_cur = jnp.where(ci <= ri, 0.0, NEG_INF)
    return neg_prev, neg_cur


def _dil_specs(L, R, t, qcol, kcol, vcol):
    W = D_MODEL
    prev = lambda qi: jnp.maximum(qi - 1, 0)
    return dict(
        q=pl.BlockSpec((t, W), lambda r, qi: (qi, qcol(r))),
        kp=pl.BlockSpec((t, W), lambda r, qi: (prev(qi), kcol(r))), kc=pl.BlockSpec((t, W), lambda r, qi: (qi, kcol(r))),
        vp=pl.BlockSpec((t, W), lambda r, qi: (prev(qi), vcol(r))), vc=pl.BlockSpec((t, W), lambda r, qi: (qi, vcol(r))),
        own=pl.BlockSpec((t, W), lambda r, qi: (qi, r)), tab=pl.BlockSpec((t, LANES), lambda r, qi: (qi, r)))


def _dil_fwd(name, x, qcol, kcol, vcol, R, L):
    t = BAND_STEPS
    W = D_MODEL
    sp = _dil_specs(L, R, t, qcol, kcol, vcol)

    def body(q_ref, kp_ref, kc_ref, vp_ref, vc_ref, o_ref, lse_ref):
        lo = _lo_lanes()
        neg_p, neg_c = _band_masks(t, pl.program_id(1) == 0)
        s_p, s_c = [], []
        for h in range(N_HEADS):
            cols = _pair_cols(h)
            qh = jnp.where(_head_lanes(lo, h), q_ref[:, cols], 0)
            s_p.append(_dot(qh, kp_ref[:, cols], NT))
            s_c.append(_dot(qh, kc_ref[:, cols], NT))
        s_p = jnp.stack(s_p) + neg_p[None]
        s_c = jnp.stack(s_c) + neg_c[None]
        m = jnp.maximum(jnp.max(s_p, axis=2, keepdims=True), jnp.max(s_c, axis=2, keepdims=True))
        p_p, p_c = jnp.exp(s_p - m), jnp.exp(s_c - m)
        l = jnp.sum(p_p, axis=2, keepdims=True) + jnp.sum(p_c, axis=2, keepdims=True)
        inv, lse = 1.0 / l, m + jnp.log(l)
        p_p, p_c = p_p.astype(BF16), p_c.astype(BF16)
        for p in range(N_PAIRS):
            cols = _pair_cols(2 * p)
            o2 = jnp.zeros((t, LANES), F32)
            for h in (2 * p, 2 * p + 1):
                hm = _head_lanes(lo, h)
                pv = _dot(p_p[h], jnp.where(hm, vp_ref[:, cols], 0), NN) + _dot(p_c[h], jnp.where(hm, vc_ref[:, cols], 0), NN)
                o2 = o2 + pv * inv[h]
            o_ref[:, cols] = o2
            lse_ref[:, cols] = jnp.where(lo, lse[2 * p], lse[2 * p + 1])

    return pl.pallas_call(
        body, grid=(R, L // t), in_specs=[sp["q"], sp["kp"], sp["kc"], sp["vp"], sp["vc"]], out_specs=[sp["own"], sp["own"]],
        out_shape=[jax.ShapeDtypeStruct((L, R * W), F32), jax.ShapeDtypeStruct((L, R * W), F32)],
        compiler_params=_params("parallel", "parallel"), name=name)(x[0], x[1], x[1], x[2], x[2])


def _dil_scores(lo, q_ref, do_ref, o_ref, lse_ref, kv_refs):
    s = [[] for _ in kv_refs]
    dp = [[] for _ in kv_refs]
    lse, d = [], []
    for h in range(N_HEADS):
        cols = _pair_cols(h)
        hm = _head_lanes(lo, h)
        qh, doh = jnp.where(hm, q_ref[:, cols], 0), jnp.where(hm, do_ref[:, cols], 0)
        for i, (k_ref, v_ref) in enumerate(kv_refs):
            s[i].append(_dot(qh, k_ref[:, cols], NT))
            dp[i].append(_dot(doh, v_ref[:, cols], NT))
        lse.append(_rep_rows(lse_ref[:, cols], lo)[h % 2])
        dd = do_ref[:, cols].astype(F32) * o_ref[:, cols].astype(F32)
        d.append(jnp.sum(jnp.where(hm, dd, 0.0), axis=1, keepdims=True))
    return (*[jnp.stack(x) for x in s], *[jnp.stack(x) for x in dp], jnp.stack(lse), jnp.stack(d))


def _dil_bwd(name, x, do, o, lse, tabs, R, L):
    t = BAND_STEPS
    W = D_MODEL
    nq = L // t
    qb = lambda qi: jnp.minimum(qi, nq - 1)
    kb = lambda qi: jnp.maximum(qb(qi) - 1, 0)
    done = lambda qi: jnp.maximum(qi - 1, 0)
    at = lambda f, width: pl.BlockSpec((t, width), lambda r, qi: (f(qi), r))

    def body(q_ref, kp_ref, kc_ref, vp_ref, vc_ref, do_ref, o_ref, lse_ref, cq_ref, saq_ref, sbq_ref, ck_ref, sak_ref, sbk_ref,
             dq_ref, dk_ref, dv_ref, dk_scr, dv_scr):
        qi = pl.program_id(1)
        lo = _lo_lanes()
        rot_k = lambda x: _rotate(x, ck_ref[...], sak_ref[...], sbk_ref[...], -1.0).astype(BF16)

        @pl.when(qi == 0)
        def _():
            dk_scr[...] = jnp.zeros_like(dk_scr)
            dv_scr[...] = jnp.zeros_like(dv_scr)

        @pl.when(qi < nq)
        def _():
            neg_p, neg_c = _band_masks(t, qi == 0)
            s_p, s_c, dp_p, dp_c, lse_h, d = _dil_scores(lo, q_ref, do_ref, o_ref, lse_ref, ((kp_ref, vp_ref), (kc_ref, vc_ref)))
            p_p, p_c = jnp.exp(s_p + neg_p[None] - lse_h), jnp.exp(s_c + neg_c[None] - lse_h)
            ds_p, ds_c = (p_p * (dp_p - d)).astype(BF16), (p_c * (dp_c - d)).astype(BF16)
            p_p, p_c = p_p.astype(BF16), p_c.astype(BF16)
            for p in range(N_PAIRS):
                cols = _pair_cols(2 * p)
                dq2 = jnp.zeros((t, LANES), F32)
                dk_prev, dv_prev = dk_scr[:, cols], dv_scr[:, cols]
                dk_cur, dv_cur = jnp.zeros((t, LANES), F32), jnp.zeros((t, LANES), F32)
                for h in (2 * p, 2 * p + 1):
                    hm = _head_lanes(lo, h)
                    qh, doh = jnp.where(hm, q_ref[:, cols], 0), jnp.where(hm, do_ref[:, cols], 0)
                    dq2 = dq2 + _dot(ds_p[h], jnp.where(hm, kp_ref[:, cols], 0), NN) + _dot(ds_c[h], jnp.where(hm, kc_ref[:, cols], 0), NN)
                    dk_prev, dv_prev = dk_prev + _dot(ds_p[h], qh, TN), dv_prev + _dot(p_p[h], doh, TN)
                    dk_cur, dv_cur = dk_cur + _dot(ds_c[h], qh, TN), dv_cur + _dot(p_c[h], doh, TN)
                dq_ref[:, cols] = _rotate(dq2 * SOFTMAX_SCALE, cq_ref[...], saq_ref[...], sbq_ref[...], -1.0).astype(BF16)
                dk_ref[:, cols] = rot_k(dk_prev)
                dv_ref[:, cols] = dv_prev.astype(BF16)
                dk_scr[:, cols] = dk_cur
                dv_scr[:, cols] = dv_cur

        @pl.when(qi == nq)
        def _():
            for p in range(N_PAIRS):
                cols = _pair_cols(2 * p)
                dk_ref[:, cols] = rot_k(dk_scr[:, cols])
            dv_ref[...] = dv_scr[...].astype(BF16)

    wide = jax.ShapeDtypeStruct((L, R * W), BF16)
    return pl.pallas_call(
        body, grid=(R, nq + 1),
        in_specs=[at(qb, W), at(kb, W), at(qb, W), at(kb, W), at(qb, W), at(qb, W), at(qb, W), at(qb, W),
                  at(qb, LANES), at(qb, LANES), at(qb, LANES), at(done, LANES), at(done, LANES), at(done, LANES)],
        out_specs=[at(qb, W), at(done, W), at(done, W)], out_shape=[wide, wide, wide],
        scratch_shapes=[pltpu.VMEM((t, W), F32), pltpu.VMEM((t, W), F32)],
        compiler_params=_params("parallel", "arbitrary"), name=name)(x[0], x[1], x[1], x[2], x[2], do, o, lse, *tabs, *tabs)


def _fox_operands(q2, k2, kb2, lo, hh):
    lane = lax.broadcasted_iota(jnp.int32, (1, LANES), 1)
    if hh == 0:
        ones = ((lane >= HEAD_DIM) & (lane < HEAD_DIM + 3)).astype(BF16)
        return jnp.where(lo, q2, ones), jnp.where(lo, k2, kb2)
    ones = (lane < 3).astype(BF16)
    return jnp.where(lo, ones, q2), jnp.where(lo, kb2, k2)


def _causal_neg(t):
    ri = lax.broadcasted_iota(jnp.int32, (t, t), 0)
    ci = lax.broadcasted_iota(jnp.int32, (t, t), 1)
    return jnp.where(ci <= ri, 0.0, NEG_INF)


def _fox_fwd(name, qkv, kbias, t):
    S = qkv.shape[0]
    W = D_MODEL
    nq = S // t
    rep = t // LANES

    def body(q_ref, k_ref, v_ref, kb_ref, o_ref, lse_ref, m_scr, l_scr, acc_scr):
        qi, j = pl.program_id(0), pl.program_id(1)
        lo = _lo_lanes()

        @pl.when(j == 0)
        def _():
            m_scr[...] = jnp.full_like(m_scr, NEG_INF)
            l_scr[...] = jnp.zeros_like(l_scr)
            acc_scr[...] = jnp.zeros_like(acc_scr)

        def step(masked):
            neg = _causal_neg(t) if masked else None

            def pair(p, carry):
                cs = pl.ds(pl.multiple_of(p * LANES, LANES), LANES)
                q2, k2, v2, kb2 = q_ref[:, cs], k_ref[:, cs], v_ref[:, cs], kb_ref[:, cs]
                pvs, alphas = [], []
                for hh in range(2):
                    hm = lo if hh == 0 else jnp.logical_not(lo)
                    qh, kh = _fox_operands(q2, k2, kb2, lo, hh)
                    s = _dot(qh, kh, NT)
                    if masked:
                        s = s + neg
                    h = 2 * p + hh
                    m_prev = m_scr[h]
                    m_new = jnp.maximum(m_prev, jnp.max(s, axis=1, keepdims=True))
                    pe = jnp.exp(s - jnp.tile(m_new, (1, rep)))
                    alpha = jnp.exp(m_prev - m_new)
                    l_scr[h] = alpha * l_scr[h] + jnp.sum(pe, axis=1, keepdims=True)
                    m_scr[h] = m_new
                    pvs.append(_dot(pe.astype(BF16), jnp.where(hm, v2, 0), NN))
                    alphas.append(alpha)
                acc_scr[:, cs] = acc_scr[:, cs] * jnp.where(lo, alphas[0], alphas[1]) + pvs[0] + pvs[1]
                return carry

            lax.fori_loop(0, N_PAIRS, pair, 0, unroll=4)

        @pl.when(j < qi)
        def _():
            step(False)

        @pl.when(j == qi)
        def _():
            step(True)

        @pl.when(j == nq - 1)
        def _():
            for p in range(N_PAIRS):
                cols = slice(p * LANES, (p + 1) * LANES)
                l2 = jnp.where(lo, l_scr[2 * p], l_scr[2 * p + 1])
                m2 = jnp.where(lo, m_scr[2 * p], m_scr[2 * p + 1])
                o_ref[:, cols] = (acc_scr[:, cols] / l2).astype(BF16)
                lse_ref[:, cols] = m2 + jnp.log(l2)

    kv = lambda col: pl.BlockSpec((t, W), lambda qi, j: (jnp.minimum(j, qi), col))
    own = pl.BlockSpec((t, W), lambda qi, j: (qi, 0))
    return pl.pallas_call(
        body, grid=(nq, nq), in_specs=[own, kv(1), kv(2), kv(0)], out_specs=[own, own],
        out_shape=[jax.ShapeDtypeStruct((S, W), BF16), jax.ShapeDtypeStruct((S, W), F32)],
        scratch_shapes=[pltpu.VMEM((N_HEADS, t, LANES), F32), pltpu.VMEM((N_HEADS, t, LANES), F32), pltpu.VMEM((t, W), F32)],
        compiler_params=_params("parallel", "arbitrary"), name=name)(qkv, qkv, qkv, kbias)


def _fox_head_grads(qh, kh, v2, doh, neg, lse_h, d_h, rep):
    s = _dot(qh, kh, NT)
    if neg is not None:
        s = s + neg
    p = jnp.exp(s - jnp.tile(lse_h, (1, rep)))
    return p, p * (_dot(doh, v2, NT) - d_h)


def _fox_bwd(name, qkv, kbias, do, o, lse, t):
    S = qkv.shape[0]
    W = D_MODEL
    nq = S // t
    rep = t // LANES

    def body(q_ref, k_ref, v_ref, kb_ref, do_ref, o_ref, lse_ref, dq_ref, dk_ref, dv_ref, rs_ref, dc_ref, dq_scr, dk_scr, dv_scr):
        kb, j = pl.program_id(0), pl.program_id(1)
        lo = _lo_lanes()
        lane = lax.broadcasted_iota(jnp.int32, (1, LANES), 1)
        rows = pl.ds(pl.multiple_of(j * t, t), t)

        @pl.when((kb == 0) & (j == 0))
        def _():
            dq_scr[...] = jnp.zeros_like(dq_scr)
            rs_ref[...] = jnp.zeros_like(rs_ref)

        @pl.when(j == 0)
        def _():
            dk_scr[...] = jnp.zeros_like(dk_scr)
            dv_scr[...] = jnp.zeros_like(dv_scr)
            dc_ref[...] = jnp.zeros_like(dc_ref)

        def step(masked):
            neg = _causal_neg(t) if masked else None

            def pair(p, carry):
                cs = pl.ds(pl.multiple_of(p * LANES, LANES), LANES)
                q2, k2, v2, kb2, do2 = q_ref[:, cs], k_ref[:, cs], v_ref[:, cs], kb_ref[:, cs], do_ref[:, cs]
                dd = do2.astype(F32) * o_ref[:, cs].astype(F32)
                lse_h = _rep_rows(lse_ref[:, cs], lo)
                dq2 = jnp.zeros((t, LANES), F32)
                dv2 = jnp.zeros((t, LANES), F32)
                dk2 = jnp.zeros((t, LANES), F32)
                for hh in range(2):
                    hm = lo if hh == 0 else jnp.logical_not(lo)
                    qh, kh = _fox_operands(q2, k2, kb2, lo, hh)
                    doh = jnp.where(hm, do2, 0)
                    d_h = jnp.sum(jnp.where(hm, dd, 0.0), axis=1, keepdims=True)
                    pr, ds = _fox_head_grads(qh, kh, v2, doh, neg, lse_h[hh], d_h, rep)
                    rs_ref[rows, :] += jnp.where(lane == 2 * p + hh, jnp.sum(ds, axis=1, keepdims=True), 0.0)
                    dc_ref[p, hh:hh + 1, :] += jnp.sum(ds, axis=0, keepdims=True)
                    dsb = ds.astype(BF16)
                    dv2 = dv2 + _dot(pr.astype(BF16), doh, TN)
                    dk2 = dk2 + _dot(dsb, jnp.where(hm, q2, 0), TN)
                    dq2 = dq2 + _dot(dsb, jnp.where(hm, k2, 0), NN)
                dv_scr[:, cs] += dv2
                dk_scr[:, cs] += dk2
                dq_scr[rows, cs] += dq2
                return carry

            lax.fori_loop(0, N_PAIRS, pair, 0, unroll=4)
            if masked:
                dq_ref[...] = (dq_scr[rows, :] * SOFTMAX_SCALE).astype(BF16)

        @pl.when(j > kb)
        def _():
            step(False)

        @pl.when(j == kb)
        def _():
            step(True)

        @pl.when(j == nq - 1)
        def _():
            dv_ref[...] = dv_scr[...].astype(BF16)
            dk_ref[...] = dk_scr[...].astype(BF16)

    qrow = pl.BlockSpec((t, W), lambda kb, j: (jnp.maximum(j, kb), 0))
    krow = lambda col: pl.BlockSpec((t, W), lambda kb, j: (kb, col))
    own = pl.BlockSpec((t, W), lambda kb, j: (kb, 0))
    wide = jax.ShapeDtypeStruct((S, W), BF16)
    return pl.pallas_call(
        body, grid=(nq, nq), in_specs=[qrow, krow(1), krow(2), krow(0), qrow, qrow, qrow],
        out_specs=[own, own, own, pl.BlockSpec((S, LANES), lambda kb, j: (0, 0)), pl.BlockSpec((N_PAIRS, 2, t), lambda kb, j: (0, 0, kb))],
        out_shape=[wide, wide, wide, jax.ShapeDtypeStruct((S, LANES), F32), jax.ShapeDtypeStruct((N_PAIRS, 2, S), F32)],
        scratch_shapes=[pltpu.VMEM((S, W), F32), pltpu.VMEM((t, W), F32), pltpu.VMEM((t, W), F32)],
        compiler_params=pltpu.CompilerParams(dimension_semantics=("arbitrary", "arbitrary"), vmem_limit_bytes=FOX_BWD_VMEM),
        name=name)(qkv, qkv, qkv, kbias, do, o, lse)


def _view_spec(tm, R, index=lambda i: (i, 0)):
    return pl.BlockSpec((tm // R, R * D_MODEL), index)


def _matmul_nt_views(name, a, w, dils, tm=512):
    S, K = a.shape

    def body(a_ref, w_ref, *rest):
        res = _dot(a_ref[...].astype(BF16), w_ref[...], NT)
        _write_views([res[:, b * LANES:(b + 1) * LANES] for b in range(N_PAIRS)], rest[-1], rest[:-1], dils, tm)

    return pl.pallas_call(
        body, grid=(S // tm,), in_specs=[pl.BlockSpec((tm, K), lambda i: (i, 0)), pl.BlockSpec((D_MODEL, K), lambda i: (0, 0))],
        out_specs=[_view_spec(tm, R) for R in dils],
        out_shape=[jax.ShapeDtypeStruct((S // R, R * D_MODEL), BF16) for R in dils],
        scratch_shapes=[pltpu.VMEM((N_PAIRS, tm, LANES), F32)], compiler_params=_params("parallel"), name=name)(a, w)


def _write_views(chunks, scr, out_refs, dils, tm):
    if any(R > 1 for R in dils):
        _stage_chunks(scr, chunks)
    for ref, R in zip(out_refs, dils):
        for b, x in enumerate(chunks):
            if R == 1:
                ref[:, b * LANES:(b + 1) * LANES] = x.astype(ref.dtype)
                continue
            for r in range(R):
                col = r * D_MODEL + b * LANES
                ref[:, col:col + LANES] = _strided_rows(scr, b, r, tm // R, R).astype(ref.dtype)


def _combine(name, os_, lses, dils, tm=256):
    S = os_[0].shape[0] * dils[0]
    G = len(dils)

    def body(*refs):
        o_refs, l_refs = refs[:G], refs[G:2 * G]
        o_outs, l_outs = refs[2 * G:3 * G], refs[3 * G:4 * G]
        stage = refs[4 * G:]
        for g, R in enumerate(dils):
            if R == 1:
                continue
            for src, dst in ((o_refs[g], stage[2 * g]), (l_refs[g], stage[2 * g + 1])):
                _unstride(lambda r, b, src=src: src[:, r * D_MODEL + b * LANES:r * D_MODEL + (b + 1) * LANES], R, stage[0], dst, tm)
        o_chunks, l_chunks = [], []
        for b in range(N_PAIRS):
            cols = slice(b * LANES, (b + 1) * LANES)
            os_b = [o_refs[g][:, cols] if R == 1 else stage[2 * g][b] for g, R in enumerate(dils)]
            ls = [l_refs[g][:, cols] if R == 1 else stage[2 * g + 1][b] for g, R in enumerate(dils)]
            m = functools.reduce(jnp.maximum, ls)
            ws = [jnp.exp(l - m) for l in ls]
            den = functools.reduce(jnp.add, ws)
            o_chunks.append(functools.reduce(jnp.add, [w * o for w, o in zip(ws, os_b)]) / den)
            l_chunks.append(m + jnp.log(den))
        _write_views(o_chunks, stage[0], o_outs, dils, tm)
        _write_views(l_chunks, stage[1], l_outs, dils, tm)

    specs = [_view_spec(tm, R) for R in dils]
    shapes = lambda dt: [jax.ShapeDtypeStruct((S // R, R * D_MODEL), dt) for R in dils]
    res = pl.pallas_call(
        body, grid=(S // tm,), in_specs=specs * 2, out_specs=specs * 2, out_shape=shapes(BF16) + shapes(F32),
        scratch_shapes=[pltpu.VMEM((N_PAIRS, tm, LANES), F32)] * (2 * G), compiler_params=_params("parallel"),
        name=name)(*os_, *lses)
    return res[:G], res[G:]


def _tri_matmul(tri, x):
    hi, mid, lo = _split3(x)
    return _dot(tri, hi, NN) + _dot(tri, mid, NN) + _dot(tri, lo, NN)


def _split3(x):
    hi = x.astype(BF16)
    r1 = x - hi.astype(F32)
    mid = r1.astype(BF16)
    return hi, mid, (r1 - mid.astype(F32)).astype(BF16)


def _gate_fwd(name, z, bf, tb=512):
    S = z.shape[0]

    def body(z_ref, b_ref, kb_ref, carry):
        @pl.when(pl.program_id(0) == 0)
        def _():
            carry[...] = jnp.zeros_like(carry)

        lf = jax.nn.log_sigmoid(z_ref[...] + b_ref[...])
        ri = lax.broadcasted_iota(jnp.int32, (tb, tb), 0)
        ci = lax.broadcasted_iota(jnp.int32, (tb, tb), 1)
        tri = (ci <= ri).astype(BF16)
        c = _tri_matmul(tri, lf) + carry[...]
        carry[...] = c[tb - 1:tb, :]
        head = lax.broadcasted_iota(jnp.int32, (LANES, D_MODEL), 0)
        col = lax.broadcasted_iota(jnp.int32, (LANES, D_MODEL), 1)
        base = (head >> 1) * LANES + jnp.where((head & 1) == 0, HEAD_DIM, 0)
        kb = jnp.zeros((tb, D_MODEL), F32)
        for i, piece in enumerate(_split3(-c)):
            place = ((col == base + i) & (head < N_HEADS)).astype(BF16)
            kb = kb + _dot(piece, place, NN)
        kb_ref[...] = kb.astype(BF16)

    row = pl.BlockSpec((tb, LANES), lambda i: (i, 0))
    return pl.pallas_call(
        body, grid=(S // tb,), in_specs=[row, pl.BlockSpec((1, LANES), lambda i: (0, 0))],
        out_specs=pl.BlockSpec((tb, D_MODEL), lambda i: (i, 0)), out_shape=jax.ShapeDtypeStruct((S, D_MODEL), BF16),
        scratch_shapes=[pltpu.VMEM((1, LANES), F32)], compiler_params=_params("arbitrary"), name=name)(z, bf)


def _gate_bwd(name, dc, z, bf, tb=512):
    S = z.shape[0]
    nb = S // tb

    def body(dc_ref, z_ref, b_ref, dz_ref, db_ref, carry):
        @pl.when(pl.program_id(0) == 0)
        def _():
            carry[...] = jnp.zeros_like(carry)
            db_ref[...] = jnp.zeros_like(db_ref)

        ri = lax.broadcasted_iota(jnp.int32, (tb, tb), 0)
        ci = lax.broadcasted_iota(jnp.int32, (tb, tb), 1)
        tri = (ci >= ri).astype(BF16)
        dlf = _tri_matmul(tri, dc_ref[...]) + carry[...]
        carry[...] = dlf[0:1, :]
        dz = dlf * jax.nn.sigmoid(-(z_ref[...] + b_ref[...]))
        dz_ref[...] = dz
        db_ref[...] += jnp.sum(dz, axis=0, keepdims=True)

    row = pl.BlockSpec((tb, LANES), lambda i: (nb - 1 - i, 0))
    vec = pl.BlockSpec((1, LANES), lambda i: (0, 0))
    return pl.pallas_call(
        body, grid=(nb,), in_specs=[row, row, vec], out_specs=[row, vec],
        out_shape=[jax.ShapeDtypeStruct((S, LANES), F32), jax.ShapeDtypeStruct((1, LANES), F32)],
        scratch_shapes=[pltpu.VMEM((1, LANES), F32)], compiler_params=_params("arbitrary"), name=name)(dc, z, bf)


def _ffn_gu(name, n, wgu, comm=None, tm=1024):
    S, D = n.shape
    nb = N_DEV // 2

    def body(n_ref, wg_ref, wu_ref, gu_ref, act_ref):
        x = n_ref[...]
        g = _dot(x, wg_ref[...], NN)
        u = _dot(x, wu_ref[...], NN)
        gu_ref[0] = g.astype(BF16)
        gu_ref[1] = u.astype(BF16)
        act_ref[...] = (g * jax.nn.sigmoid(g) * u).astype(BF16)

    return _call(
        name, body, (nb, S // tm),
        [pl.BlockSpec((tm, D), lambda j, i: (i, 0)), pl.BlockSpec((None, D, FF_BLK), lambda j, i: (j, 0, 0)),
         pl.BlockSpec((None, D, FF_BLK), lambda j, i: (j + nb, 0, 0))],
        [pl.BlockSpec((2, None, tm, FF_BLK), lambda j, i: (0, j, i, 0)), pl.BlockSpec((None, tm, FF_BLK), lambda j, i: (j, i, 0))],
        [jax.ShapeDtypeStruct((2, nb, S, FF_BLK), BF16), jax.ShapeDtypeStruct((nb, S, FF_BLK), BF16)], [],
        (n, wgu, wgu), ("parallel", "parallel"), comm)


def _ffn_down(name, act, wd, resid, comm=None, tm=1024):
    nb, S, _ = act.shape
    D = wd.shape[1]

    def epilogue(acc, ex, outs, j):
        outs[0][...] = acc + ex[0][...]

    o_spec = pl.BlockSpec((tm, D), lambda i, j, k: (i, 0))
    return _mm_call(name, (S // tm, 1, nb), act, pl.BlockSpec((None, tm, FF_BLK), lambda i, j, k: (k, i, 0)),
                    wd, pl.BlockSpec((FF_BLK, D), lambda i, j, k: (k, 0)), NN,
                    [jax.ShapeDtypeStruct((S, D), F32)], [o_spec], (tm, D), epilogue, (resid,), (o_spec,), comm=comm)


def _ffn_dact(name, dh, wd, gu, comm=None, tm=512):
    S, D = dh.shape
    nb = N_DEV // 2

    def epilogue(acc, ex, outs, j):
        g = ex[0][0].astype(F32)
        u = ex[0][1].astype(F32)
        sig = jax.nn.sigmoid(g)
        outs[0][0] = (acc * u * (sig * (1.0 + g * (1.0 - sig)))).astype(BF16)
        outs[0][1] = (acc * (g * sig)).astype(BF16)

    gu_spec = pl.BlockSpec((2, None, tm, FF_BLK), lambda j, i, k: (0, j, i, 0))
    return _mm_call(name, (nb, S // tm, 1), dh, pl.BlockSpec((tm, D), lambda j, i, k: (i, 0)),
                    wd, pl.BlockSpec((FF_BLK, D), lambda j, i, k: (j, 0)), NT,
                    [jax.ShapeDtypeStruct((2, nb, S, FF_BLK), BF16)], [gu_spec], (tm, FF_BLK), epilogue, (gu,), (gu_spec,),
                    col_axis=0, comm=comm)


def _ffn_dwgu(name, n, dgu, comm=None, tm=1024, tk=1024):
    S, D = n.shape
    dgu8 = dgu.reshape(N_DEV, S, FF_BLK)
    return _mm_call(name, (N_DEV, D // tm, S // tk), n, pl.BlockSpec((tk, tm), lambda d, i, k: (k, i)),
                    dgu8, pl.BlockSpec((None, tk, FF_BLK), lambda d, i, k: (d, k, 0)), TN,
                    [jax.ShapeDtypeStruct((N_DEV, D, FF_BLK), BF16)],
                    [pl.BlockSpec((None, tm, FF_BLK), lambda d, i, k: (d, i, 0))], (tm, FF_BLK), comm=comm)


def _ffn_dwd(name, act, dh, tk=1024):
    nb, S, _ = act.shape
    D = dh.shape[1]
    out = _mm_call(name, (nb, 1, S // tk), act, pl.BlockSpec((None, tk, FF_BLK), lambda b, j, k: (b, k, 0)),
                   dh, pl.BlockSpec((tk, D), lambda b, j, k: (k, 0)), TN,
                   [jax.ShapeDtypeStruct((nb, FF_BLK, D), BF16)],
                   [pl.BlockSpec((None, FF_BLK, D), lambda b, j, k: (b, 0, 0))], (FF_BLK, D))[0]
    return out.reshape(N_DEV, FF_BLK // 2, D)


def _ffn_dn(name, dgu, wgu, comm=None, tm=1024):
    S = dgu.shape[2]
    D = wgu.shape[1]
    dgu8 = dgu.reshape(N_DEV, S, FF_BLK)
    return _mm_call(name, (S // tm, 1, N_DEV), dgu8, pl.BlockSpec((None, tm, FF_BLK), lambda i, j, k: (k, i, 0)),
                    wgu, pl.BlockSpec((None, D, FF_BLK), lambda i, j, k: (k, 0, 0)), NT,
                    [jax.ShapeDtypeStruct((S, D), F32)], [pl.BlockSpec((tm, D), lambda i, j, k: (i, 0))], (tm, D), comm=comm)


def _adamw(name, parts, w, m, v, tr):
    rows, cols = w.shape
    n_parts = len(parts)
    c1 = 1.0 - ADAM_B1 ** ADAM_STEP
    c2 = 1.0 - ADAM_B2 ** ADAM_STEP

    def body(*refs):
        p_refs = refs[:n_parts]
        w_ref, m_ref, v_ref, g_ref, d_ref, nm_ref, nv_ref = refs[n_parts:]
        g = p_refs[0][...].astype(F32)
        for r in p_refs[1:]:
            g = g + r[...].astype(F32)
        mm = ADAM_B1 * m_ref[...] + (1.0 - ADAM_B1) * g
        vv = ADAM_B2 * v_ref[...] + (1.0 - ADAM_B2) * (g * g)
        g_ref[...] = g
        nm_ref[...] = mm
        nv_ref[...] = vv
        d_ref[...] = -ADAM_LR * ((mm / c1) / (jnp.sqrt(vv / c2) + ADAM_EPS) + ADAM_WD * w_ref[...])

    blk = pl.BlockSpec((tr, cols), lambda i: (i, 0))
    out = jax.ShapeDtypeStruct((rows, cols), F32)
    return pl.pallas_call(
        body, grid=(rows // tr,), in_specs=[blk] * (n_parts + 3), out_specs=[blk] * 4, out_shape=[out] * 4,
        compiler_params=_params("parallel"), name=name)(*parts, w, m, v)


def _position():
    return lax.axis_index("x"), lax.axis_index("y"), lax.axis_index("c")


def _other_chips():
    x, y, _ = _position()
    return [(1 - x, y), (x, 1 - y), (1 - x, 1 - y)]


def _remote(src, dst, send, recv, k, to):
    return pltpu.make_async_remote_copy(src_ref=src, dst_ref=dst, send_sem=send.at[k], recv_sem=recv.at[k],
                                        device_id=to, device_id_type=MESH)


def _ag_send(blocks, direct=False):
    n_peer = 7 if direct else 4

    def copies(ins, outs, send, recv, local, r0=0, l0=0):
        x, y, c = _position()
        me = 4 * x + 2 * y + c
        peers = [(x, y, 1 - c)] + [(px, py, c) for px, py in _other_chips()]
        if direct:
            peers += [(px, py, 1 - c) for px, py in _other_chips()]
        cps = []
        for t, (src, dst) in enumerate(zip(ins, outs)):
            cps.append(pltpu.make_async_copy(src, dst.at[me], local.at[l0 + t]))
            cps += [_remote(src, dst.at[me], send, recv, r0 + n_peer * t + k, to) for k, to in enumerate(peers)]
        return cps

    outs = tuple(jax.ShapeDtypeStruct((N_DEV,) + b.shape, b.dtype) for b in blocks)
    return _Comm(tuple(blocks), outs, {}, copies, n_peer * len(blocks), len(blocks))


def _ag_forward(bufs):
    def copies(ins, outs, send, recv, local, r0=0, l0=0):
        x, y, c = _position()
        cps = []
        for t, buf in enumerate(outs):
            for k, (px, py) in enumerate(_other_chips()):
                slot = buf.at[4 * px + 2 * py + c]
                cps.append(_remote(slot, slot, send, recv, r0 + 3 * t + k, (x, y, 1 - c)))
        return cps

    outs = tuple(jax.ShapeDtypeStruct(b.shape, b.dtype) for b in bufs)
    return _Comm(tuple(bufs), outs, {t: t for t in range(len(bufs))}, copies, 3 * len(bufs), 0)


def _rs_swap(shares):
    def copies(ins, outs, send, recv, local, r0=0, l0=0):
        x, y, c = _position()
        return [_remote(src.at[:, 1 - c], dst, send, recv, r0 + t, (x, y, 1 - c)) for t, (src, dst) in enumerate(zip(ins, outs))]

    ins = tuple(s.reshape((4, 2) + s.shape[1:]) for s in shares)
    outs = tuple(jax.ShapeDtypeStruct((4,) + s.shape[1:], s.dtype) for s in shares)
    return _Comm(ins, outs, {}, copies, len(shares), 0)


def _rs_exchange(sums):
    def copies(ins, outs, send, recv, local, r0=0, l0=0):
        _, _, c = _position()
        return [_remote(src.at[2 * px + py], dst.at[k], send, recv, r0 + 3 * t + k, (px, py, c))
                for t, (src, dst) in enumerate(zip(ins, outs)) for k, (px, py) in enumerate(_other_chips())]

    outs = tuple(jax.ShapeDtypeStruct((3,) + s.shape[1:], s.dtype) for s in sums)
    return _Comm(tuple(sums), outs, {}, copies, 3 * len(sums), 0)


def _comm_call(name, comm):
    return _call(name, lambda: None, (), [], [], [], [], (), (), comm)


def _pair_sum(name, share, got, core, tr):
    _, rows, cols = share.shape

    def body(c_ref, a_ref, b_ref, o_ref):
        o_ref[...] = (a_ref[...].astype(F32) + b_ref[...].astype(F32)).astype(o_ref.dtype)

    grid_spec = pltpu.PrefetchScalarGridSpec(
        num_scalar_prefetch=1, grid=(4, rows // tr),
        in_specs=[pl.BlockSpec((None, None, tr, cols), lambda q, i, c: (q, c[0], i, 0)),
                  pl.BlockSpec((None, tr, cols), lambda q, i, c: (q, i, 0))],
        out_specs=pl.BlockSpec((None, tr, cols), lambda q, i, c: (q, i, 0)))
    return pl.pallas_call(
        body, grid_spec=grid_spec, out_shape=jax.ShapeDtypeStruct((4, rows, cols), share.dtype),
        compiler_params=_params("parallel", "parallel"), name=name)(core, share.reshape(4, 2, rows, cols), got)


TENSORS = ("a_w_in", "a_w_out", "b_w_in", "b_w_out", "gu0", "gu1", "dn0", "dn1")
ROW_TILE = {"a_w_in": 256, "a_w_out": 128, "b_w_in": 256, "b_w_out": 128, "gu0": 256, "gu1": 256, "dn0": 176, "dn1": 176}
A_BLK = 9 * D_MODEL // N_DEV
B_BLK = 386
B_IN = 3 * D_MODEL + N_HEADS
B_IN_PAD = 3 * D_MODEL + LANES


def kernel(x, a_norm, a_w_in, a_w_out, b_norm, b_w_in, b_f, b_w_out, ffn_norm, ffn_w_gu, ffn_w_down, final_norm, loss_target, m_a_norm, m_a_w_in, m_a_w_out, m_b_norm, m_b_w_in, m_b_f, m_b_w_out, m_ffn_norm, m_ffn_w_gu, m_ffn_w_down, m_final_norm, v_a_norm, v_a_w_in, v_a_w_out, v_b_norm, v_b_w_in, v_b_f, v_b_w_out, v_ffn_norm, v_ffn_w_gu, v_ffn_w_down, v_final_norm):
    S = x.shape[1]
    xi, yi, ci = _position()
    dev = 4 * xi + 2 * yi + ci
    core = ci.reshape(1).astype(jnp.int32)
    h0, target = x.reshape(S, D_MODEL), loss_target.reshape(S, D_MODEL)

    def shards(a_in, a_out, b_in, b_out, gu, dn):
        return {"a_w_in": a_in[0], "a_w_out": a_out[0], "b_w_in": b_in[0], "b_w_out": b_out[0],
                "gu0": gu[0], "gu1": gu[1], "dn0": dn[0], "dn1": dn[1]}

    w_sh = shards(a_w_in, a_w_out, b_w_in, b_w_out, ffn_w_gu, ffn_w_down)
    m_sh = shards(m_a_w_in, m_a_w_out, m_b_w_in, m_b_w_out, m_ffn_w_gu, m_ffn_w_down)
    v_sh = shards(v_a_w_in, v_a_w_out, v_b_w_in, v_b_w_out, v_ffn_w_gu, v_ffn_w_down)
    wb = {n: w_sh[n].astype(BF16) for n in TENSORS}
    bf_pad = jnp.pad(b_f, ((0, 0), (0, LANES - N_HEADS)))
    tabs = _rope_tables(S)

    g_ain, g_aout = _comm_call("gather_a", _ag_send([wb["a_w_in"], wb["a_w_out"]]))
    g_ain, g_aout = _comm_call("forward_a", _ag_forward([g_ain, g_aout]))
    n0 = _rms_fwd("rms_a", h0, a_norm[0])
    later = [wb["b_w_in"], wb["b_w_out"], wb["gu0"], wb["dn0"], jnp.pad(b_norm, ((0, 7), (0, 0)))]
    w_a_in = g_ain.transpose(1, 0, 2).reshape(D_MODEL, 9 * D_MODEL)
    qkv_a, later = _a_proj("proj_a", n0, w_a_in, tabs, _ag_send(later))
    cols = [lambda r: r] * 3
    groups = [(g, dil, S // dil, qkv_a[g]) for g, (window, dil) in enumerate(DILATED_PATTERNS)]
    dils = [dil for _, dil in DILATED_PATTERNS]
    fwd = [_dil_fwd("dil_fwd%d" % g, view, *cols, dil, L) for g, dil, L, view in groups]
    o_views, lse_views = _combine("dil_combine", [f[0] for f in fwd], [f[1] for f in fwd], dils)
    o_a = o_views[0]
    w_a_out = g_aout.reshape(D_MODEL, D_MODEL)
    h1, (g_bin, g_bout, g_gu0, g_dn0, g_bnorm) = _matmul("out_a", o_a, w_a_out, "nn", F32, TM, 1024, 1024, resid=h0,
                                                         comm=_ag_forward(later))

    n1 = _rms_fwd("rms_f0", h1, ffn_norm[0])
    gu0, act0, g_gu1 = _ffn_gu("gu_f0", n1, g_gu0, _ag_send([wb["gu1"]]))
    w_dn0 = g_dn0.reshape(D_FF, D_MODEL)
    h2, g_dn1 = _ffn_down("down_f0", act0, w_dn0, h1, _ag_send([wb["dn1"]]))

    b_norm_full = g_bnorm[:, 0].reshape(D_MODEL)
    w_b_in = g_bin.transpose(1, 0, 2).reshape(D_MODEL, B_IN)
    w_b_gate = jnp.pad(w_b_in[:, 3 * D_MODEL:], ((0, 0), (0, LANES - N_HEADS)))
    w_b_cat = jnp.concatenate([w_b_in[:, :3 * D_MODEL], w_b_gate], axis=1)
    w_b_out = g_bout.reshape(D_MODEL, D_MODEL)
    n2 = _rms_fwd("rms_b", h2, b_norm_full)
    qkv, (g_gu1, g_dn1) = _matmul("proj_b", n2, w_b_in[:, :3 * D_MODEL], "nn", BF16, TM, 1024, 1024, col0_scale=SOFTMAX_SCALE,
                                  comm=_ag_forward([g_gu1, g_dn1]))
    z = _matmul("gate_b", n2, w_b_gate, "nn", F32, TM, LANES, 1024)
    kbias = _gate_fwd("gate_cumsum", z, bf_pad)
    tf = min(S, 512)
    o_b, lse_b = _fox_fwd("fox_fwd", qkv, kbias, tf)
    h3 = _matmul("out_b", o_b, w_b_out, "nn", F32, TM, 1024, 1024, resid=h2)

    w_dn1 = g_dn1.reshape(D_FF, D_MODEL)
    n3 = _rms_fwd("rms_f1", h3, ffn_norm[1])
    gu1, act1 = _ffn_gu("gu_f1", n3, g_gu1)
    h4 = _ffn_down("down_f1", act1, w_dn1, h3)[0]

    dh4, d_final, loss = _loss_head("loss_head", h4, final_norm, target)

    share, got, sums, others = {}, {}, {}, {}

    def pair_sums(*names):
        for n in names:
            sums[n] = _pair_sum("pair_" + n, share[n], got[n], core, ROW_TILE[n])

    dgu1 = _ffn_dact("dact_f1", dh4, w_dn1, gu1)[0]
    share["dn1"] = _ffn_dwd("dwd_f1", act1, dh4)
    share["gu1"] = _ffn_dwgu("dwgu_f1", n3, dgu1)[0]
    dn3, got["gu1"], got["dn1"] = _ffn_dn("dn_f1", dgu1, g_gu1, _rs_swap([share["gu1"], share["dn1"]]))
    dh3, d_ffn1 = _rms_bwd("rmsb_f1", dn3, h3, ffn_norm[1], dh4)
    pair_sums("gu1", "dn1")

    do_b = _matmul("dout_b", dh3, w_b_out, "nt", BF16, TM, 1024, 1024)
    share["b_w_out"] = _matmul("dwout_b", o_b, dh3, "tn", BF16, TM, 1024, 1024).reshape(N_DEV, 128, D_MODEL)
    dq_b, dk_b, dv_b, ds_rowsum, ds_colsum = _fox_bwd("fox_bwd", qkv, kbias, do_b, o_b, lse_b, tf)
    dc = ds_rowsum[:, :N_HEADS] - ds_colsum.reshape(N_HEADS, S).T
    dz, d_bf = _gate_bwd("gate_bwd", jnp.pad(dc, ((0, 0), (0, LANES - N_HEADS))), z, bf_pad)
    dproj_b = jnp.concatenate([dq_b, dk_b, dv_b, dz.astype(BF16)], axis=1)
    dw_b_in, (others["gu1"],) = _matmul("dwin_b", n2, dproj_b, "tn", BF16, TM, B_IN_PAD // 5, 1024, comm=_rs_exchange([sums["gu1"]]))
    dn2, (others["dn1"],) = _matmul("dn_b", dproj_b, w_b_cat, "nt", F32, TM, 1024, B_IN_PAD // 5, comm=_rs_exchange([sums["dn1"]]))
    dh2, d_bnorm = _rms_bwd("rmsb_b", dn2, h2, b_norm_full, dh3)
    share["b_w_in"] = dw_b_in[:, :B_IN].reshape(D_MODEL, N_DEV, B_BLK).transpose(1, 0, 2)

    dgu0, got["b_w_in"], got["b_w_out"] = _ffn_dact("dact_f0", dh2, w_dn0, gu0, _rs_swap([share["b_w_in"], share["b_w_out"]]))
    share["dn0"] = _ffn_dwd("dwd_f0", act0, dh2)
    pair_sums("b_w_in", "b_w_out")
    share["gu0"], others["b_w_in"], others["b_w_out"] = _ffn_dwgu(
        "dwgu_f0", n1, dgu0, _rs_exchange([sums["b_w_in"], sums["b_w_out"]]))
    dn1, got["gu0"], got["dn0"] = _ffn_dn("dn_f0", dgu0, g_gu0, _rs_swap([share["gu0"], share["dn0"]]))
    dh1, d_ffn0 = _rms_bwd("rmsb_f0", dn1, h1, ffn_norm[0], dh2)
    pair_sums("gu0", "dn0")

    do_views = _matmul_nt_views("dout_a", dh1, w_a_out, dils)
    share["a_w_out"] = _matmul("dwout_a", o_a, dh1, "tn", BF16, TM, 1024, 1024).reshape(N_DEV, 128, D_MODEL)
    pieces = []
    for g, dil, L, view in groups:
        rot = tuple(tb.reshape(L, dil * LANES) for tb in tabs)
        grads = _dil_bwd("dil_bwd%d" % g, view, do_views[g], o_views[g], lse_views[g], rot, dil, L)
        pieces += list(grads)
    dw_a_in, others["gu0"], others["dn0"] = _a_dw("dwin_a", n0, pieces, dils, _rs_exchange([sums["gu0"], sums["dn0"]]))
    share["a_w_in"] = dw_a_in.reshape(D_MODEL, N_DEV, A_BLK).transpose(1, 0, 2)
    got["a_w_in"], got["a_w_out"] = _comm_call("swap_a", _rs_swap([share["a_w_in"], share["a_w_out"]]))
    pair_sums("a_w_in", "a_w_out")
    dn0, others["a_w_in"], others["a_w_out"] = _a_dn("dn_a", pieces, dils, w_a_in, _rs_exchange([sums["a_w_in"], sums["a_w_out"]]))
    dx, d_anorm = _rms_bwd("rmsb_a", dn0, h0, a_norm[0], dh1)

    misc = jnp.concatenate([d_bf[:, :N_HEADS], loss[:, :1], jnp.zeros((1, D_MODEL - N_HEADS - 1), F32)], axis=1)
    small = jnp.concatenate([d_anorm, d_ffn0, d_ffn1, d_final, d_bnorm, misc, jnp.zeros((2, D_MODEL), F32)], axis=0)
    small_all, = _comm_call("gather_small", _ag_send([small], direct=True))

    outs = {}
    for n in TENSORS:
        mine = lax.dynamic_index_in_dim(sums[n], 2 * xi + yi, axis=0, keepdims=False)
        outs[n] = _adamw("adamw_" + n, [mine] + [others[n][k] for k in range(3)], w_sh[n], m_sh[n], v_sh[n], ROW_TILE[n])

    pad_vec = lambda a: jnp.pad(a, ((0, 0), (0, D_MODEL - a.shape[1])))

    def small_pack(an, fn, fin, bf):
        return jnp.concatenate([an, fn, fin.reshape(1, D_MODEL), jnp.zeros((1, D_MODEL), F32), pad_vec(bf),
                                jnp.zeros((2, D_MODEL), F32)], axis=0)

    sg, sd, sm, sv = _adamw("adamw_small", [small_all[d] for d in range(N_DEV)], small_pack(a_norm, ffn_norm, final_norm, b_f),
                            small_pack(m_a_norm, m_ffn_norm, m_final_norm, m_b_f),
                            small_pack(v_a_norm, v_ffn_norm, v_final_norm, v_b_f), 8)
    g_bn = lax.dynamic_slice(sg[4:5], (0, dev * LANES), (1, LANES))
    bn = _adamw("adamw_b_norm", [g_bn], b_norm, m_b_norm, v_b_norm, 1)

    def tree(i):
        full = lambda name, ref: outs[name][i].reshape(ref.shape)
        sml = (sg, sd, sm, sv)[i]
        return dict(
            a_norm=sml[0:1], a_w_in=full("a_w_in", a_w_in), a_w_out=full("a_w_out", a_w_out), b_norm=bn[i],
            b_w_in=full("b_w_in", b_w_in), b_f=sml[5:6, :N_HEADS], b_w_out=full("b_w_out", b_w_out), ffn_norm=sml[1:3],
            ffn_w_gu=jnp.stack([outs["gu0"][i], outs["gu1"][i]]).reshape(ffn_w_gu.shape),
            ffn_w_down=jnp.stack([outs["dn0"][i], outs["dn1"][i]]).reshape(ffn_w_down.shape), final_norm=sml[3])

    order = ("a_norm", "a_w_in", "a_w_out", "b_norm", "b_w_in", "b_f", "b_w_out", "ffn_norm", "ffn_w_gu", "ffn_w_down", "final_norm")
    result = [sg[5, N_HEADS], dx.reshape(x.shape)]
    for i in range(4):
        t = tree(i)
        result += [t[n] for n in order]
    return tuple(result)
```

```python
import functools
from typing import Callable, NamedTuple

import jax
import jax.numpy as jnp
from jax import lax
from jax.experimental import pallas as pl
from jax.experimental.pallas import tpu as pltpu

F32 = jnp.float32
BF16 = jnp.bfloat16

D_MODEL = 1024
N_HEADS = 16
HEAD_DIM = 64
N_PAIRS = N_HEADS // 2
LANES = 128
DILATED_PATTERNS = ((128, 1), (512, 4), (2048, 16))
BAND_STEPS = 128
ROT_DIM = HEAD_DIM // 4
ROPE_THETA = 500000.0
D_FF = 2816
RMS_EPS = 1e-6
NEG_INF = -1e30
SOFTMAX_SCALE = HEAD_DIM ** -0.5
N_DEV = 8
FF_BLK = 2 * D_FF // N_DEV
ADAM_LR, ADAM_B1, ADAM_B2, ADAM_EPS, ADAM_WD, ADAM_STEP = 0.001, 0.9, 0.999, 1e-08, 0.01, 10
VMEM_LIMIT = 52 * 1024 * 1024
FOX_BWD_VMEM = 60 * 1024 * 1024
TM = 1024
MESH = pl.DeviceIdType.MESH

NN = (((1,), (0,)), ((), ()))
NT = (((1,), (1,)), ((), ()))
TN = (((0,), (0,)), ((), ()))


def _params(*sem):
    return pltpu.CompilerParams(dimension_semantics=sem, vmem_limit_bytes=VMEM_LIMIT)


def _dot(a, b, dims):
    return lax.dot_general(a, b, dims, preferred_element_type=F32)


class _Comm(NamedTuple):
    ins: tuple
    outs: tuple
    aliases: dict
    copies: Callable
    n_remote: int
    n_local: int


def _call(name, body, grid, in_specs, out_specs, out_shape, scratch, args, sem, comm=None):
    if comm is None:
        return pl.pallas_call(body, grid=grid, in_specs=in_specs, out_specs=out_specs, out_shape=out_shape,
                              scratch_shapes=scratch, compiler_params=_params(*sem), name=name)(*args)
    n_in, n_out = len(in_specs), len(out_specs)
    n_ci, n_co = len(comm.ins), len(comm.outs)
    o0 = n_in + n_ci

    def hosted(*refs):
        c_ins, c_outs = refs[n_in:o0], refs[o0 + n_out:o0 + n_out + n_co]
        sems = refs[-3:]

        def start():
            for cp in comm.copies(c_ins, c_outs, *sems):
                cp.start()

        def wait():
            for cp in comm.copies(c_ins, c_outs, *sems):
                cp.wait()

        if not grid:
            start()
            body()
            wait()
            return
        ids = [pl.program_id(ax) for ax in range(len(grid))]
        pl.when(functools.reduce(jnp.logical_and, [i == 0 for i in ids]))(start)
        body(*refs[:n_in], *refs[o0:o0 + n_out], *refs[o0 + n_out + n_co:-3])
        pl.when(functools.reduce(jnp.logical_and, [i == g - 1 for i, g in zip(ids, grid)]))(wait)

    hbm = pl.BlockSpec(memory_space=pltpu.HBM)
    dma = pltpu.SemaphoreType.DMA
    return pl.pallas_call(
        hosted, grid=grid, in_specs=[*in_specs, *[hbm] * n_ci], out_specs=[*out_specs, *[hbm] * n_co],
        out_shape=[*out_shape, *comm.outs], input_output_aliases={n_in + i: n_out + o for i, o in comm.aliases.items()},
        scratch_shapes=[*scratch, dma((comm.n_remote,)), dma((comm.n_remote,)), dma((max(comm.n_local, 1),))],
        compiler_params=_params(*["arbitrary"] * len(grid)), name=name)(*args, *comm.ins)


def _mm_call(name, grid, a, a_spec, b, b_spec, dims, out_shapes, out_specs, acc_shape, epilogue=None,
             extras=(), extra_specs=(), col_axis=1, comm=None):
    nk = grid[2]
    n_extra = len(extras)
    n_out = len(out_shapes)

    def finish(res, ex, outs, j):
        if epilogue is None:
            outs[0][...] = res.astype(outs[0].dtype)
        else:
            epilogue(res, ex, outs, j)

    def body(*refs):
        a_ref, b_ref = refs[0], refs[1]
        ex = refs[2:2 + n_extra]
        outs = refs[2 + n_extra:2 + n_extra + n_out]
        j, k = pl.program_id(col_axis), pl.program_id(2)
        part = _dot(a_ref[...].astype(BF16), b_ref[...].astype(BF16), dims)
        if nk == 1:
            finish(part, ex, outs, j)
            return
        acc = refs[-1]

        @pl.when(k == 0)
        def _():
            acc[...] = part

        @pl.when((k > 0) & (k < nk - 1))
        def _():
            acc[...] += part

        @pl.when(k == nk - 1)
        def _():
            finish(acc[...] + part, ex, outs, j)

    return _call(name, body, grid, [a_spec, b_spec, *extra_specs], out_specs, out_shapes,
                 [] if nk == 1 else [pltpu.VMEM(acc_shape, F32)], (a, b, *extras), ("parallel", "parallel", "arbitrary"), comm)


def _matmul(name, a, b, mode, out_dtype, tm, tn, tk, resid=None, col0_scale=None, comm=None):
    if mode == "nn":
        (M, K), N = a.shape, b.shape[1]
        a_spec = pl.BlockSpec((tm, tk), lambda j, i, k: (i, k))
        b_spec = pl.BlockSpec((tk, tn), lambda j, i, k: (k, j))
        dims = NN
    elif mode == "nt":
        (M, K), N = a.shape, b.shape[0]
        a_spec = pl.BlockSpec((tm, tk), lambda j, i, k: (i, k))
        b_spec = pl.BlockSpec((tn, tk), lambda j, i, k: (j, k))
        dims = NT
    else:
        (K, M), N = a.shape, b.shape[1]
        a_spec = pl.BlockSpec((tk, tm), lambda j, i, k: (k, i))
        b_spec = pl.BlockSpec((tk, tn), lambda j, i, k: (k, j))
        dims = TN
    assert M % tm == 0 and N % tn == 0 and K % tk == 0, (name, M, N, K, tm, tn, tk)
    o_spec = pl.BlockSpec((tm, tn), lambda j, i, k: (i, j))
    extras, extra_specs, epilogue = (), (), None
    if resid is not None:
        extras, extra_specs = (resid,), (o_spec,)

        def epilogue(acc, ex, outs, j):
            outs[0][...] = (acc + ex[0][...]).astype(outs[0].dtype)

    elif col0_scale is not None:

        def epilogue(acc, ex, outs, j):
            outs[0][...] = (acc * jnp.where(j == 0, col0_scale, 1.0)).astype(outs[0].dtype)

    res = _mm_call(name, (N // tn, M // tm, K // tk), a, a_spec, b, b_spec, dims, [jax.ShapeDtypeStruct((M, N), out_dtype)],
                   [o_spec], (tm, tn), epilogue, extras, extra_specs, col_axis=0, comm=comm)
    return res[0] if comm is None else (res[0], res[1:])


def _rms_fwd(name, h, gain, dils=(1,), tm=512):
    S, D = h.shape

    def body(h_ref, g_ref, *rest):
        x = h_ref[...]
        rstd = lax.rsqrt(jnp.mean(x * x, axis=-1, keepdims=True) + RMS_EPS)
        y = x * rstd * g_ref[...]
        _write_views([y[:, b * LANES:(b + 1) * LANES] for b in range(N_PAIRS)], rest[-1], rest[:-1], dils, tm)

    res = pl.pallas_call(
        body, grid=(S // tm,), in_specs=[pl.BlockSpec((tm, D), lambda i: (i, 0)), pl.BlockSpec((1, D), lambda i: (0, 0))],
        out_specs=[_view_spec(tm, R) for R in dils], out_shape=[jax.ShapeDtypeStruct((S // R, R * D), BF16) for R in dils],
        scratch_shapes=[pltpu.VMEM((N_PAIRS, tm, LANES), F32)], compiler_params=_params("parallel"),
        name=name)(h, gain.reshape(1, D))
    return res[0] if len(dils) == 1 else res


def _rms_bwd(name, dn, h, gain, dres, tm=512):
    S, D = h.shape

    def body(dn_ref, h_ref, g_ref, r_ref, dh_ref, dg_ref):
        x = h_ref[...]
        rstd = lax.rsqrt(jnp.mean(x * x, axis=-1, keepdims=True) + RMS_EPS)
        xhat = x * rstd
        d = dn_ref[...]
        dxhat = d * g_ref[...]
        dh_ref[...] = rstd * (dxhat - xhat * jnp.mean(dxhat * xhat, axis=-1, keepdims=True)) + r_ref[...]

        @pl.when(pl.program_id(0) == 0)
        def _():
            dg_ref[...] = jnp.zeros_like(dg_ref)

        dg_ref[...] += jnp.sum(d * xhat, axis=0, keepdims=True)

    row = pl.BlockSpec((tm, D), lambda i: (i, 0))
    vec = pl.BlockSpec((1, D), lambda i: (0, 0))
    return pl.pallas_call(
        body, grid=(S // tm,), in_specs=[row, row, vec, row], out_specs=[row, vec],
        out_shape=[jax.ShapeDtypeStruct((S, D), F32), jax.ShapeDtypeStruct((1, D), F32)],
        compiler_params=_params("arbitrary"), name=name)(dn, h, gain.reshape(1, D), dres)


def _loss_head(name, h, gain, target, tm=512):
    S, D = h.shape

    def body(h_ref, g_ref, t_ref, dh_ref, dg_ref, loss_ref):
        x = h_ref[...]
        rstd = lax.rsqrt(jnp.mean(x * x, axis=-1, keepdims=True) + RMS_EPS)
        xhat = x * rstd
        err = xhat * g_ref[...] - t_ref[...]
        dy = err * (1.0 / D)
        dxhat = dy * g_ref[...]
        dh_ref[...] = rstd * (dxhat - xhat * jnp.mean(dxhat * xhat, axis=-1, keepdims=True))

        @pl.when(pl.program_id(0) == 0)
        def _():
            dg_ref[...] = jnp.zeros_like(dg_ref)
            loss_ref[...] = jnp.zeros_like(loss_ref)

        dg_ref[...] += jnp.sum(dy * xhat, axis=0, keepdims=True)
        part = 0.5 * jnp.sum(jnp.mean(err * err, axis=-1, keepdims=True), axis=0, keepdims=True)
        loss_ref[...] += jnp.broadcast_to(part, loss_ref.shape)

    row = pl.BlockSpec((tm, D), lambda i: (i, 0))
    vec = pl.BlockSpec((1, D), lambda i: (0, 0))
    return pl.pallas_call(
        body, grid=(S // tm,), in_specs=[row, vec, row], out_specs=[row, vec, pl.BlockSpec((1, LANES), lambda i: (0, 0))],
        out_shape=[jax.ShapeDtypeStruct((S, D), F32), jax.ShapeDtypeStruct((1, D), F32),
                   jax.ShapeDtypeStruct((1, LANES), F32)],
        compiler_params=_params("arbitrary"), name=name)(h, gain.reshape(1, D), target)


def _rope_tables(S):
    half = ROT_DIM // 2
    inv_freq = ROPE_THETA ** (-jnp.arange(half, dtype=F32) * 2.0 / ROT_DIM)
    ang = jnp.arange(S, dtype=F32)[:, None] * inv_freq[None, :]
    cos, sin = jnp.cos(ang), jnp.sin(ang)
    one = jnp.ones((S, HEAD_DIM - ROT_DIM), F32)
    zero = jnp.zeros((S, HEAD_DIM - ROT_DIM), F32)
    zh = jnp.zeros((S, half), F32)
    c = jnp.concatenate([cos, cos, one], axis=1)
    sa = jnp.concatenate([-sin, zh, zero], axis=1)
    sb = jnp.concatenate([zh, sin, zero], axis=1)
    return tuple(jnp.concatenate([t, t], axis=1) for t in (c, sa, sb))


def _rotate(x, c, sa, sb, sign):
    return x * c + sign * (pltpu.roll(x, LANES - ROT_DIM // 2, 1) * sa + pltpu.roll(x, ROT_DIM // 2, 1) * sb)


def _stage_chunks(scr, chunks):
    for c, x in enumerate(chunks):
        scr[c] = x


def _strided_rows(scr, c, r, n, R):
    return scr.at[c][pl.ds(r, n, stride=R), :]


def _a_proj(name, n, w, tabs, comm, tm=512):
    S, D = n.shape
    n_i = S // tm
    dils = [dil for _, dil in DILATED_PATTERNS]
    n_out = 3 * len(dils)

    def body(n_ref, w_ref, c_ref, sa_ref, sb_ref, *rest):
        outs, scr = rest[:n_out], rest[n_out]
        j = pl.program_id(0)
        acc = _dot(n_ref[...], w_ref[...], NN)
        c, sa, sb = c_ref[...], sa_ref[...], sb_ref[...]
        for J in range(n_out):
            R, kind = dils[J // 3], J % 3

            @pl.when(j == J)
            def _(J=J, R=R, kind=kind):
                chunks = [acc[:, b * LANES:(b + 1) * LANES] for b in range(N_PAIRS)]
                if kind < 2:
                    chunks = [_rotate(x, c, sa, sb, 1.0) * (SOFTMAX_SCALE if kind == 0 else 1.0) for x in chunks]
                if R == 1:
                    for b, x in enumerate(chunks):
                        outs[J][:, b * LANES:(b + 1) * LANES] = x.astype(BF16)
                    return
                _stage_chunks(scr, chunks)
                for r in range(R):
                    for b in range(N_PAIRS):
                        col = r * D_MODEL + b * LANES
                        outs[J][:, col:col + LANES] = _strided_rows(scr, b, r, tm // R, R).astype(BF16)

    def out_spec(J, R):
        return pl.BlockSpec((tm // R, R * D_MODEL), lambda j, i: (jnp.where(j == J, i, jnp.where(j < J, 0, n_i - 1)), 0))

    tab = pl.BlockSpec((tm, LANES), lambda j, i: (i, 0))
    res = _call(name, body, (n_out, n_i),
                [pl.BlockSpec((tm, D), lambda j, i: (i, 0)), pl.BlockSpec((D, D_MODEL), lambda j, i: (0, j)), tab, tab, tab],
                [out_spec(J, dils[J // 3]) for J in range(n_out)],
                [jax.ShapeDtypeStruct((S // dils[J // 3], dils[J // 3] * D_MODEL), BF16) for J in range(n_out)],
                [pltpu.VMEM((N_PAIRS, tm, LANES), F32)], (n, w, *tabs), ("arbitrary", "arbitrary"), comm)
    return [res[3 * g:3 * g + 3] for g in range(len(dils))], res[n_out:]


def _unstride(src_chunk, R, tok, rows):
    for r in range(R):
        for b in range(N_PAIRS):
            tok.at[b][pl.ds(r, rows // R, stride=R), :] = src_chunk(r, b).astype(F32)


def _by_residue(ref, R):
    return ref[...] if R == 1 else jnp.concatenate([ref[:, r * D_MODEL:(r + 1) * D_MODEL] for r in range(R)], axis=0)


def _a_dw(name, n_views, pieces, dils, comm, tk=512):
    D = D_MODEL
    S = n_views[0].shape[0] * dils[0]
    n_k = S // tk
    n_p, n_g = len(pieces), len(dils)

    def body(*refs):
        n_refs, p_refs, o_ref, acc = refs[:n_g], refs[n_g:n_g + n_p], refs[n_g + n_p], refs[n_g + n_p + 1]
        j, k = pl.program_id(0), pl.program_id(1)
        for J in range(n_p):

            @pl.when(j == J)
            def _(J=J):
                R = dils[J // 3]
                part = _dot(_by_residue(n_refs[J // 3], R), _by_residue(p_refs[J], R), TN)

                @pl.when(k == 0)
                def _():
                    acc[...] = part

                @pl.when(k > 0)
                def _():
                    acc[...] += part

        @pl.when(k == n_k - 1)
        def _():
            o_ref[...] = acc[...].astype(BF16)

    def piece_spec(J):
        R = dils[J // 3]
        return pl.BlockSpec((tk // R, R * D_MODEL), lambda j, k: (jnp.where(j == J, k, jnp.where(j < J, 0, n_k - 1)), 0))

    n_specs = [pl.BlockSpec((tk // R, R * D_MODEL), lambda j, k: (k, 0)) for R in dils]
    return _call(name, body, (n_p, n_k), n_specs + [piece_spec(J) for J in range(n_p)],
                 [pl.BlockSpec((D, D_MODEL), lambda j, k: (0, j))], [jax.ShapeDtypeStruct((D, n_p * D_MODEL), BF16)],
                 [pltpu.VMEM((D, D_MODEL), F32)], (*n_views, *pieces), ("arbitrary", "arbitrary"), comm)


def _a_dn(name, pieces, dils, w, comm, tm=512):
    D = w.shape[0]
    S = pieces[0].shape[0] * dils[0]
    n_p = len(pieces)

    def body(*refs):
        p_refs, w_ref, o_ref, acc, part_acc, tok = refs[:n_p], refs[n_p], refs[n_p + 1], refs[n_p + 2], refs[n_p + 3], refs[n_p + 4]
        j = pl.program_id(1)
        for J in range(n_p):

            @pl.when(j == J)
            def _(J=J):
                R, t = dils[J // 3], J % 3
                part = _dot(_by_residue(p_refs[J], R), w_ref[...], NT)
                if R == 1:
                    if J == 0:
                        acc[...] = part
                    else:
                        acc[...] += part
                    return
                if t == 0:
                    part_acc[...] = part
                    return
                if t == 1:
                    part_acc[...] += part
                    return
                n = tm // R
                _unstride(lambda r, b: part_acc[r * n:(r + 1) * n, b * LANES:(b + 1) * LANES]
                          + part[r * n:(r + 1) * n, b * LANES:(b + 1) * LANES], R, tok, tm)
                total = acc[...] + jnp.concatenate([tok[b] for b in range(N_PAIRS)], axis=1)
                if J == n_p - 1:
                    o_ref[...] = total
                else:
                    acc[...] = total

    specs = [pl.BlockSpec((tm // dils[J // 3], dils[J // 3] * D_MODEL), lambda i, j: (i, 0)) for J in range(n_p)]
    return _call(name, body, (S // tm, n_p), specs + [pl.BlockSpec((D, D_MODEL), lambda i, j: (0, j))],
                 [pl.BlockSpec((tm, D), lambda i, j: (i, 0))], [jax.ShapeDtypeStruct((S, D), F32)],
                 [pltpu.VMEM((tm, D), F32), pltpu.VMEM((tm, D), F32), pltpu.VMEM((N_PAIRS, tm, LANES), F32)], (*pieces, w),
                 ("arbitrary", "arbitrary"), comm)


def _lo_lanes():
    return lax.broadcasted_iota(jnp.int32, (1, LANES), 1) < HEAD_DIM


def _rep_rows(x2, lo):
    sw = pltpu.roll(x2, HEAD_DIM, 1)
    return jnp.where(lo, x2, sw), jnp.where(lo, sw, x2)


def _pair_cols(h):
    return slice((h // 2) * LANES, (h // 2 + 1) * LANES)


def _head_lanes(lo, h):
    return lo if h % 2 == 0 else jnp.logical_not(lo)


def _band_masks(t, first):
    ri = lax.broadcasted_iota(jnp.int32, (t, t), 0)
    ci = lax.broadcasted_iota(jnp.int32, (t, t), 1)
    neg_prev = jnp.where((ci >= ri) & jnp.logical_not(first), 0.0, NEG_INF)
    neg_cur = jnp.where(ci <= ri, 0.0, NEG_INF)
    return neg_prev, neg_cur


def _dil_specs(L, R, t, qcol, kcol, vcol):
    W = D_MODEL
    prev = lambda qi: jnp.maximum(qi - 1, 0)
    return dict(
        q=pl.BlockSpec((t, W), lambda r, qi: (qi, qcol(r))),
        kp=pl.BlockSpec((t, W), lambda r, qi: (prev(qi), kcol(r))), kc=pl.BlockSpec((t, W), lambda r, qi: (qi, kcol(r))),
        vp=pl.BlockSpec((t, W), lambda r, qi: (prev(qi), vcol(r))), vc=pl.BlockSpec((t, W), lambda r, qi: (qi, vcol(r))),
        own=pl.BlockSpec((t, W), lambda r, qi: (qi, r)), tab=pl.BlockSpec((t, LANES), lambda r, qi: (qi, r)))


def _dil_fwd(name, x, qcol, kcol, vcol, R, L):
    t = BAND_STEPS
    W = D_MODEL
    sp = _dil_specs(L, R, t, qcol, kcol, vcol)

    def body(q_ref, kp_ref, kc_ref, vp_ref, vc_ref, o_ref, lse_ref):
        lo = _lo_lanes()
        neg_p, neg_c = _band_masks(t, pl.program_id(1) == 0)
        s_p, s_c = [], []
        for h in range(N_HEADS):
            cols = _pair_cols(h)
            qh = jnp.where(_head_lanes(lo, h), q_ref[:, cols], 0)
            s_p.append(_dot(qh, kp_ref[:, cols], NT))
            s_c.append(_dot(qh, kc_ref[:, cols], NT))
        s_p = jnp.stack(s_p) + neg_p[None]
        s_c = jnp.stack(s_c) + neg_c[None]
        m = jnp.maximum(jnp.max(s_p, axis=2, keepdims=True), jnp.max(s_c, axis=2, keepdims=True))
        p_p, p_c = jnp.exp(s_p - m), jnp.exp(s_c - m)
        l = jnp.sum(p_p, axis=2, keepdims=True) + jnp.sum(p_c, axis=2, keepdims=True)
        inv, lse = 1.0 / l, m + jnp.log(l)
        p_p, p_c = p_p.astype(BF16), p_c.astype(BF16)
        for p in range(N_PAIRS):
            cols = _pair_cols(2 * p)
            o2 = jnp.zeros((t, LANES), F32)
            for h in (2 * p, 2 * p + 1):
                hm = _head_lanes(lo, h)
                pv = _dot(p_p[h], jnp.where(hm, vp_ref[:, cols], 0), NN) + _dot(p_c[h], jnp.where(hm, vc_ref[:, cols], 0), NN)
                o2 = o2 + pv * inv[h]
            o_ref[:, cols] = o2
            lse_ref[:, cols] = jnp.where(lo, lse[2 * p], lse[2 * p + 1])

    return pl.pallas_call(
        body, grid=(R, L // t), in_specs=[sp["q"], sp["kp"], sp["kc"], sp["vp"], sp["vc"]], out_specs=[sp["own"], sp["own"]],
        out_shape=[jax.ShapeDtypeStruct((L, R * W), F32), jax.ShapeDtypeStruct((L, R * W), F32)],
        compiler_params=_params("parallel", "parallel"), name=name)(x[0], x[1], x[1], x[2], x[2])


def _dil_scores(lo, q_ref, do_ref, o_ref, lse_ref, kv_refs):
    s = [[] for _ in kv_refs]
    dp = [[] for _ in kv_refs]
    lse, d = [], []
    for h in range(N_HEADS):
        cols = _pair_cols(h)
        hm = _head_lanes(lo, h)
        qh, doh = jnp.where(hm, q_ref[:, cols], 0), jnp.where(hm, do_ref[:, cols], 0)
        for i, (k_ref, v_ref) in enumerate(kv_refs):
            s[i].append(_dot(qh, k_ref[:, cols], NT))
            dp[i].append(_dot(doh, v_ref[:, cols], NT))
        lse.append(_rep_rows(lse_ref[:, cols], lo)[h % 2])
        dd = do_ref[:, cols].astype(F32) * o_ref[:, cols].astype(F32)
        d.append(jnp.sum(jnp.where(hm, dd, 0.0), axis=1, keepdims=True))
    return (*[jnp.stack(x) for x in s], *[jnp.stack(x) for x in dp], jnp.stack(lse), jnp.stack(d))


def _dil_bwd(name, x, do, o, lse, tabs, R, L):
    t = BAND_STEPS
    W = D_MODEL
    nq = L // t
    qb = lambda qi: jnp.minimum(qi, nq - 1)
    kb = lambda qi: jnp.maximum(qb(qi) - 1, 0)
    done = lambda qi: jnp.maximum(qi - 1, 0)
    at = lambda f, width: pl.BlockSpec((t, width), lambda r, qi: (f(qi), r))

    def body(q_ref, kp_ref, kc_ref, vp_ref, vc_ref, do_ref, o_ref, lse_ref, cq_ref, saq_ref, sbq_ref, ck_ref, sak_ref, sbk_ref,
             dq_ref, dk_ref, dv_ref, dk_scr, dv_scr):
        qi = pl.program_id(1)
        lo = _lo_lanes()
        rot_k = lambda x: _rotate(x, ck_ref[...], sak_ref[...], sbk_ref[...], -1.0).astype(BF16)

        @pl.when(qi == 0)
        def _():
            dk_scr[...] = jnp.zeros_like(dk_scr)
            dv_scr[...] = jnp.zeros_like(dv_scr)

        @pl.when(qi < nq)
        def _():
            neg_p, neg_c = _band_masks(t, qi == 0)
            s_p, s_c, dp_p, dp_c, lse_h, d = _dil_scores(lo, q_ref, do_ref, o_ref, lse_ref, ((kp_ref, vp_ref), (kc_ref, vc_ref)))
            p_p, p_c = jnp.exp(s_p + neg_p[None] - lse_h), jnp.exp(s_c + neg_c[None] - lse_h)
            ds_p, ds_c = (p_p * (dp_p - d)).astype(BF16), (p_c * (dp_c - d)).astype(BF16)
            p_p, p_c = p_p.astype(BF16), p_c.astype(BF16)
            for p in range(N_PAIRS):
                cols = _pair_cols(2 * p)
                dq2 = jnp.zeros((t, LANES), F32)
                dk_prev, dv_prev = dk_scr[:, cols], dv_scr[:, cols]
                dk_cur, dv_cur = jnp.zeros((t, LANES), F32), jnp.zeros((t, LANES), F32)
                for h in (2 * p, 2 * p + 1):
                    hm = _head_lanes(lo, h)
                    qh, doh = jnp.where(hm, q_ref[:, cols], 0), jnp.where(hm, do_ref[:, cols], 0)
                    dq2 = dq2 + _dot(ds_p[h], jnp.where(hm, kp_ref[:, cols], 0), NN) + _dot(ds_c[h], jnp.where(hm, kc_ref[:, cols], 0), NN)
                    dk_prev, dv_prev = dk_prev + _dot(ds_p[h], qh, TN), dv_prev + _dot(p_p[h], doh, TN)
                    dk_cur, dv_cur = dk_cur + _dot(ds_c[h], qh, TN), dv_cur + _dot(p_c[h], doh, TN)
                dq_ref[:, cols] = _rotate(dq2 * SOFTMAX_SCALE, cq_ref[...], saq_ref[...], sbq_ref[...], -1.0).astype(BF16)
                dk_ref[:, cols] = rot_k(dk_prev)
                dv_ref[:, cols] = dv_prev.astype(BF16)
                dk_scr[:, cols] = dk_cur
                dv_scr[:, cols] = dv_cur

        @pl.when(qi == nq)
        def _():
            for p in range(N_PAIRS):
                cols = _pair_cols(2 * p)
                dk_ref[:, cols] = rot_k(dk_scr[:, cols])
            dv_ref[...] = dv_scr[...].astype(BF16)

    wide = jax.ShapeDtypeStruct((L, R * W), BF16)
    return pl.pallas_call(
        body, grid=(R, nq + 1),
        in_specs=[at(qb, W), at(kb, W), at(qb, W), at(kb, W), at(qb, W), at(qb, W), at(qb, W), at(qb, W),
                  at(qb, LANES), at(qb, LANES), at(qb, LANES), at(done, LANES), at(done, LANES), at(done, LANES)],
        out_specs=[at(qb, W), at(done, W), at(done, W)], out_shape=[wide, wide, wide],
        scratch_shapes=[pltpu.VMEM((t, W), F32), pltpu.VMEM((t, W), F32)],
        compiler_params=_params("parallel", "arbitrary"), name=name)(x[0], x[1], x[1], x[2], x[2], do, o, lse, *tabs, *tabs)


def _fox_operands(q2, k2, kb2, lo, hh):
    lane = lax.broadcasted_iota(jnp.int32, (1, LANES), 1)
    if hh == 0:
        ones = ((lane >= HEAD_DIM) & (lane < HEAD_DIM + 3)).astype(BF16)
        return jnp.where(lo, q2, ones), jnp.where(lo, k2, kb2)
    ones = (lane < 3).astype(BF16)
    return jnp.where(lo, ones, q2), jnp.where(lo, kb2, k2)


def _causal_neg(t):
    ri = lax.broadcasted_iota(jnp.int32, (t, t), 0)
    ci = lax.broadcasted_iota(jnp.int32, (t, t), 1)
    return jnp.where(ci <= ri, 0.0, NEG_INF)


def _fox_fwd(name, qkv, kbias, t):
    S = qkv.shape[0]
    W = D_MODEL
    nq = S // t
    rep = t // LANES

    def body(q_ref, k_ref, v_ref, kb_ref, o_ref, lse_ref, m_scr, l_scr, acc_scr):
        qi, j = pl.program_id(0), pl.program_id(1)
        lo = _lo_lanes()

        @pl.when(j == 0)
        def _():
            m_scr[...] = jnp.full_like(m_scr, NEG_INF)
            l_scr[...] = jnp.zeros_like(l_scr)
            acc_scr[...] = jnp.zeros_like(acc_scr)

        def step(masked):
            neg = _causal_neg(t) if masked else None

            def pair(p, carry):
                cs = pl.ds(pl.multiple_of(p * LANES, LANES), LANES)
                q2, k2, v2, kb2 = q_ref[:, cs], k_ref[:, cs], v_ref[:, cs], kb_ref[:, cs]
                pvs, alphas = [], []
                for hh in range(2):
                    hm = lo if hh == 0 else jnp.logical_not(lo)
                    qh, kh = _fox_operands(q2, k2, kb2, lo, hh)
                    s = _dot(qh, kh, NT)
                    if masked:
                        s = s + neg
                    h = 2 * p + hh
                    m_prev = m_scr[h]
                    m_new = jnp.maximum(m_prev, jnp.max(s, axis=1, keepdims=True))
                    pe = jnp.exp(s - jnp.tile(m_new, (1, rep)))
                    alpha = jnp.exp(m_prev - m_new)
                    l_scr[h] = alpha * l_scr[h] + jnp.sum(pe, axis=1, keepdims=True)
                    m_scr[h] = m_new
                    pvs.append(_dot(pe.astype(BF16), jnp.where(hm, v2, 0), NN))
                    alphas.append(alpha)
                acc_scr[:, cs] = acc_scr[:, cs] * jnp.where(lo, alphas[0], alphas[1]) + pvs[0] + pvs[1]
                return carry

            lax.fori_loop(0, N_PAIRS, pair, 0, unroll=4)

        @pl.when(j < qi)
        def _():
            step(False)

        @pl.when(j == qi)
        def _():
            step(True)

        @pl.when(j == nq - 1)
        def _():
            for p in range(N_PAIRS):
                cols = slice(p * LANES, (p + 1) * LANES)
                l2 = jnp.where(lo, l_scr[2 * p], l_scr[2 * p + 1])
                m2 = jnp.where(lo, m_scr[2 * p], m_scr[2 * p + 1])
                o_ref[:, cols] = (acc_scr[:, cols] / l2).astype(BF16)
                lse_ref[:, cols] = m2 + jnp.log(l2)

    kv = lambda col: pl.BlockSpec((t, W), lambda qi, j: (jnp.minimum(j, qi), col))
    own = pl.BlockSpec((t, W), lambda qi, j: (qi, 0))
    return pl.pallas_call(
        body, grid=(nq, nq), in_specs=[own, kv(1), kv(2), kv(0)], out_specs=[own, own],
        out_shape=[jax.ShapeDtypeStruct((S, W), BF16), jax.ShapeDtypeStruct((S, W), F32)],
        scratch_shapes=[pltpu.VMEM((N_HEADS, t, LANES), F32), pltpu.VMEM((N_HEADS, t, LANES), F32), pltpu.VMEM((t, W), F32)],
        compiler_params=_params("parallel", "arbitrary"), name=name)(qkv, qkv, qkv, kbias)


def _fox_head_grads(qh, kh, v2, doh, neg, lse_h, d_h, rep):
    s = _dot(qh, kh, NT)
    if neg is not None:
        s = s + neg
    p = jnp.exp(s - jnp.tile(lse_h, (1, rep)))
    return p, p * (_dot(doh, v2, NT) - d_h)


def _fox_bwd(name, qkv, kbias, do, o, lse, t):
    S = qkv.shape[0]
    W = D_MODEL
    nq = S // t
    rep = t // LANES

    def body(q_ref, k_ref, v_ref, kb_ref, do_ref, o_ref, lse_ref, dq_ref, dk_ref, dv_ref, rs_ref, dc_ref, dq_scr, dk_scr, dv_scr):
        kb, j = pl.program_id(0), pl.program_id(1)
        lo = _lo_lanes()
        lane = lax.broadcasted_iota(jnp.int32, (1, LANES), 1)
        rows = pl.ds(pl.multiple_of(j * t, t), t)

        @pl.when((kb == 0) & (j == 0))
        def _():
            dq_scr[...] = jnp.zeros_like(dq_scr)
            rs_ref[...] = jnp.zeros_like(rs_ref)

        @pl.when(j == 0)
        def _():
            dk_scr[...] = jnp.zeros_like(dk_scr)
            dv_scr[...] = jnp.zeros_like(dv_scr)
            dc_ref[...] = jnp.zeros_like(dc_ref)

        def step(masked):
            neg = _causal_neg(t) if masked else None

            def pair(p, carry):
                cs = pl.ds(pl.multiple_of(p * LANES, LANES), LANES)
                q2, k2, v2, kb2, do2 = q_ref[:, cs], k_ref[:, cs], v_ref[:, cs], kb_ref[:, cs], do_ref[:, cs]
                dd = do2.astype(F32) * o_ref[:, cs].astype(F32)
                lse_h = _rep_rows(lse_ref[:, cs], lo)
                dq2 = jnp.zeros((t, LANES), F32)
                dv2 = jnp.zeros((t, LANES), F32)
                dk2 = jnp.zeros((t, LANES), F32)
                for hh in range(2):
                    hm = lo if hh == 0 else jnp.logical_not(lo)
                    qh, kh = _fox_operands(q2, k2, kb2, lo, hh)
                    doh = jnp.where(hm, do2, 0)
                    d_h = jnp.sum(jnp.where(hm, dd, 0.0), axis=1, keepdims=True)
                    pr, ds = _fox_head_grads(qh, kh, v2, doh, neg, lse_h[hh], d_h, rep)
                    rs_ref[rows, :] += jnp.where(lane == 2 * p + hh, jnp.sum(ds, axis=1, keepdims=True), 0.0)
                    dc_ref[p, hh:hh + 1, :] += jnp.sum(ds, axis=0, keepdims=True)
                    dsb = ds.astype(BF16)
                    dv2 = dv2 + _dot(pr.astype(BF16), doh, TN)
                    dk2 = dk2 + _dot(dsb, jnp.where(hm, q2, 0), TN)
                    dq2 = dq2 + _dot(dsb, jnp.where(hm, k2, 0), NN)
                dv_scr[:, cs] += dv2
                dk_scr[:, cs] += dk2
                dq_scr[rows, cs] += dq2
                return carry

            lax.fori_loop(0, N_PAIRS, pair, 0, unroll=4)
            if masked:
                dq_ref[...] = (dq_scr[rows, :] * SOFTMAX_SCALE).astype(BF16)

        @pl.when(j > kb)
        def _():
            step(False)

        @pl.when(j == kb)
        def _():
            step(True)

        @pl.when(j == nq - 1)
        def _():
            dv_ref[...] = dv_scr[...].astype(BF16)
            dk_ref[...] = dk_scr[...].astype(BF16)

    qrow = pl.BlockSpec((t, W), lambda kb, j: (jnp.maximum(j, kb), 0))
    krow = lambda col: pl.BlockSpec((t, W), lambda kb, j: (kb, col))
    own = pl.BlockSpec((t, W), lambda kb, j: (kb, 0))
    wide = jax.ShapeDtypeStruct((S, W), BF16)
    return pl.pallas_call(
        body, grid=(nq, nq), in_specs=[qrow, krow(1), krow(2), krow(0), qrow, qrow, qrow],
        out_specs=[own, own, own, pl.BlockSpec((S, LANES), lambda kb, j: (0, 0)), pl.BlockSpec((N_PAIRS, 2, t), lambda kb, j: (0, 0, kb))],
        out_shape=[wide, wide, wide, jax.ShapeDtypeStruct((S, LANES), F32), jax.ShapeDtypeStruct((N_PAIRS, 2, S), F32)],
        scratch_shapes=[pltpu.VMEM((S, W), F32), pltpu.VMEM((t, W), F32), pltpu.VMEM((t, W), F32)],
        compiler_params=pltpu.CompilerParams(dimension_semantics=("arbitrary", "arbitrary"), vmem_limit_bytes=FOX_BWD_VMEM),
        name=name)(qkv, qkv, qkv, kbias, do, o, lse)


def _view_spec(tm, R, index=lambda i: (i, 0)):
    return pl.BlockSpec((tm // R, R * D_MODEL), index)


def _matmul_nt_views(name, a, w, dils, tm=512):
    S, K = a.shape

    def body(a_ref, w_ref, *rest):
        res = _dot(a_ref[...].astype(BF16), w_ref[...], NT)
        _write_views([res[:, b * LANES:(b + 1) * LANES] for b in range(N_PAIRS)], rest[-1], rest[:-1], dils, tm)

    return pl.pallas_call(
        body, grid=(S // tm,), in_specs=[pl.BlockSpec((tm, K), lambda i: (i, 0)), pl.BlockSpec((D_MODEL, K), lambda i: (0, 0))],
        out_specs=[_view_spec(tm, R) for R in dils],
        out_shape=[jax.ShapeDtypeStruct((S // R, R * D_MODEL), BF16) for R in dils],
        scratch_shapes=[pltpu.VMEM((N_PAIRS, tm, LANES), F32)], compiler_params=_params("parallel"), name=name)(a, w)


def _write_views(chunks, scr, out_refs, dils, tm):
    if any(R > 1 for R in dils):
        _stage_chunks(scr, chunks)
    for ref, R in zip(out_refs, dils):
        for b, x in enumerate(chunks):
            if R == 1:
                ref[:, b * LANES:(b + 1) * LANES] = x.astype(ref.dtype)
                continue
            for r in range(R):
                col = r * D_MODEL + b * LANES
                ref[:, col:col + LANES] = _strided_rows(scr, b, r, tm // R, R).astype(ref.dtype)


def _combine(name, os_, lses, dils, tm=256):
    S = os_[0].shape[0] * dils[0]
    G = len(dils)

    def body(*refs):
        o_refs, l_refs = refs[:G], refs[G:2 * G]
        o_outs, l_outs = refs[2 * G:3 * G], refs[3 * G:4 * G]
        stage = refs[4 * G:]
        for g, R in enumerate(dils):
            if R == 1:
                continue
            for src, dst in ((o_refs[g], stage[2 * g]), (l_refs[g], stage[2 * g + 1])):
                _unstride(lambda r, b, src=src: src[:, r * D_MODEL + b * LANES:r * D_MODEL + (b + 1) * LANES], R, dst, tm)
        o_chunks, l_chunks = [], []
        for b in range(N_PAIRS):
            cols = slice(b * LANES, (b + 1) * LANES)
            os_b = [o_refs[g][:, cols] if R == 1 else stage[2 * g][b] for g, R in enumerate(dils)]
            ls = [l_refs[g][:, cols] if R == 1 else stage[2 * g + 1][b] for g, R in enumerate(dils)]
            m = functools.reduce(jnp.maximum, ls)
            ws = [jnp.exp(l - m) for l in ls]
            den = functools.reduce(jnp.add, ws)
            o_chunks.append(functools.reduce(jnp.add, [w * o for w, o in zip(ws, os_b)]) / den)
            l_chunks.append(m + jnp.log(den))
        _write_views(o_chunks, stage[0], o_outs, dils, tm)
        _write_views(l_chunks, stage[1], l_outs, dils, tm)

    specs = [_view_spec(tm, R) for R in dils]
    shapes = lambda dt: [jax.ShapeDtypeStruct((S // R, R * D_MODEL), dt) for R in dils]
    res = pl.pallas_call(
        body, grid=(S // tm,), in_specs=specs * 2, out_specs=specs * 2, out_shape=shapes(BF16) + shapes(F32),
        scratch_shapes=[pltpu.VMEM((N_PAIRS, tm, LANES), F32)] * (2 * G), compiler_params=_params("parallel"),
        name=name)(*os_, *lses)
    return res[:G], res[G:]


def _tri_matmul(tri, x):
    hi, mid, lo = _split3(x)
    return _dot(tri, hi, NN) + _dot(tri, mid, NN) + _dot(tri, lo, NN)


def _split3(x):
    hi = x.astype(BF16)
    r1 = x - hi.astype(F32)
    mid = r1.astype(BF16)
    return hi, mid, (r1 - mid.astype(F32)).astype(BF16)


def _gate_fwd(name, z, bf, tb=512):
    S = z.shape[0]

    def body(z_ref, b_ref, kb_ref, carry):
        @pl.when(pl.program_id(0) == 0)
        def _():
            carry[...] = jnp.zeros_like(carry)

        lf = jax.nn.log_sigmoid(z_ref[...] + b_ref[...])
        ri = lax.broadcasted_iota(jnp.int32, (tb, tb), 0)
        ci = lax.broadcasted_iota(jnp.int32, (tb, tb), 1)
        tri = (ci <= ri).astype(BF16)
        c = _tri_matmul(tri, lf) + carry[...]
        carry[...] = c[tb - 1:tb, :]
        head = lax.broadcasted_iota(jnp.int32, (LANES, D_MODEL), 0)
        col = lax.broadcasted_iota(jnp.int32, (LANES, D_MODEL), 1)
        base = (head >> 1) * LANES + jnp.where((head & 1) == 0, HEAD_DIM, 0)
        kb = jnp.zeros((tb, D_MODEL), F32)
        for i, piece in enumerate(_split3(-c)):
            place = ((col == base + i) & (head < N_HEADS)).astype(BF16)
            kb = kb + _dot(piece, place, NN)
        kb_ref[...] = kb.astype(BF16)

    row = pl.BlockSpec((tb, LANES), lambda i: (i, 0))
    return pl.pallas_call(
        body, grid=(S // tb,), in_specs=[row, pl.BlockSpec((1, LANES), lambda i: (0, 0))],
        out_specs=pl.BlockSpec((tb, D_MODEL), lambda i: (i, 0)), out_shape=jax.ShapeDtypeStruct((S, D_MODEL), BF16),
        scratch_shapes=[pltpu.VMEM((1, LANES), F32)], compiler_params=_params("arbitrary"), name=name)(z, bf)


def _gate_bwd(name, dc, z, bf, tb=512):
    S = z.shape[0]
    nb = S // tb

    def body(dc_ref, z_ref, b_ref, dz_ref, db_ref, carry):
        @pl.when(pl.program_id(0) == 0)
        def _():
            carry[...] = jnp.zeros_like(carry)
            db_ref[...] = jnp.zeros_like(db_ref)

        ri = lax.broadcasted_iota(jnp.int32, (tb, tb), 0)
        ci = lax.broadcasted_iota(jnp.int32, (tb, tb), 1)
        tri = (ci >= ri).astype(BF16)
        dlf = _tri_matmul(tri, dc_ref[...]) + carry[...]
        carry[...] = dlf[0:1, :]
        dz = dlf * jax.nn.sigmoid(-(z_ref[...] + b_ref[...]))
        dz_ref[...] = dz
        db_ref[...] += jnp.sum(dz, axis=0, keepdims=True)

    row = pl.BlockSpec((tb, LANES), lambda i: (nb - 1 - i, 0))
    vec = pl.BlockSpec((1, LANES), lambda i: (0, 0))
    return pl.pallas_call(
        body, grid=(nb,), in_specs=[row, row, vec], out_specs=[row, vec],
        out_shape=[jax.ShapeDtypeStruct((S, LANES), F32), jax.ShapeDtypeStruct((1, LANES), F32)],
        scratch_shapes=[pltpu.VMEM((1, LANES), F32)], compiler_params=_params("arbitrary"), name=name)(dc, z, bf)


def _ffn_gu(name, n, wgu, comm=None, tm=1024):
    S, D = n.shape
    nb = N_DEV // 2

    def body(n_ref, wg_ref, wu_ref, gu_ref, act_ref):
        x = n_ref[...]
        g = _dot(x, wg_ref[...], NN)
        u = _dot(x, wu_ref[...], NN)
        gu_ref[0] = g.astype(BF16)
        gu_ref[1] = u.astype(BF16)
        act_ref[...] = (g * jax.nn.sigmoid(g) * u).astype(BF16)

    return _call(
        name, body, (nb, S // tm),
        [pl.BlockSpec((tm, D), lambda j, i: (i, 0)), pl.BlockSpec((None, D, FF_BLK), lambda j, i: (j, 0, 0)),
         pl.BlockSpec((None, D, FF_BLK), lambda j, i: (j + nb, 0, 0))],
        [pl.BlockSpec((2, None, tm, FF_BLK), lambda j, i: (0, j, i, 0)), pl.BlockSpec((None, tm, FF_BLK), lambda j, i: (j, i, 0))],
        [jax.ShapeDtypeStruct((2, nb, S, FF_BLK), BF16), jax.ShapeDtypeStruct((nb, S, FF_BLK), BF16)], [],
        (n, wgu, wgu), ("parallel", "parallel"), comm)


def _ffn_down(name, act, wd, resid, comm=None, tm=1024):
    nb, S, _ = act.shape
    D = wd.shape[1]

    def epilogue(acc, ex, outs, j):
        outs[0][...] = acc + ex[0][...]

    o_spec = pl.BlockSpec((tm, D), lambda i, j, k: (i, 0))
    return _mm_call(name, (S // tm, 1, nb), act, pl.BlockSpec((None, tm, FF_BLK), lambda i, j, k: (k, i, 0)),
                    wd, pl.BlockSpec((FF_BLK, D), lambda i, j, k: (k, 0)), NN,
                    [jax.ShapeDtypeStruct((S, D), F32)], [o_spec], (tm, D), epilogue, (resid,), (o_spec,), comm=comm)


def _ffn_dact(name, dh, wd, gu, comm=None, tm=512):
    S, D = dh.shape
    nb = N_DEV // 2

    def epilogue(acc, ex, outs, j):
        g = ex[0][0].astype(F32)
        u = ex[0][1].astype(F32)
        sig = jax.nn.sigmoid(g)
        outs[0][0] = (acc * u * (sig * (1.0 + g * (1.0 - sig)))).astype(BF16)
        outs[0][1] = (acc * (g * sig)).astype(BF16)

    gu_spec = pl.BlockSpec((2, None, tm, FF_BLK), lambda j, i, k: (0, j, i, 0))
    return _mm_call(name, (nb, S // tm, 1), dh, pl.BlockSpec((tm, D), lambda j, i, k: (i, 0)),
                    wd, pl.BlockSpec((FF_BLK, D), lambda j, i, k: (j, 0)), NT,
                    [jax.ShapeDtypeStruct((2, nb, S, FF_BLK), BF16)], [gu_spec], (tm, FF_BLK), epilogue, (gu,), (gu_spec,),
                    col_axis=0, comm=comm)


def _ffn_dwgu(name, n, dgu, comm=None, tm=1024, tk=1024):
    S, D = n.shape
    dgu8 = dgu.reshape(N_DEV, S, FF_BLK)
    return _mm_call(name, (N_DEV, D // tm, S // tk), n, pl.BlockSpec((tk, tm), lambda d, i, k: (k, i)),
                    dgu8, pl.BlockSpec((None, tk, FF_BLK), lambda d, i, k: (d, k, 0)), TN,
                    [jax.ShapeDtypeStruct((N_DEV, D, FF_BLK), BF16)],
                    [pl.BlockSpec((None, tm, FF_BLK), lambda d, i, k: (d, i, 0))], (tm, FF_BLK), comm=comm)


def _ffn_dwd(name, act, dh, tk=1024):
    nb, S, _ = act.shape
    D = dh.shape[1]
    out = _mm_call(name, (nb, 1, S // tk), act, pl.BlockSpec((None, tk, FF_BLK), lambda b, j, k: (b, k, 0)),
                   dh, pl.BlockSpec((tk, D), lambda b, j, k: (k, 0)), TN,
                   [jax.ShapeDtypeStruct((nb, FF_BLK, D), BF16)],
                   [pl.BlockSpec((None, FF_BLK, D), lambda b, j, k: (b, 0, 0))], (FF_BLK, D))[0]
    return out.reshape(N_DEV, FF_BLK // 2, D)


def _ffn_dn(name, dgu, wgu, comm=None, tm=1024):
    S = dgu.shape[2]
    D = wgu.shape[1]
    dgu8 = dgu.reshape(N_DEV, S, FF_BLK)
    return _mm_call(name, (S // tm, 1, N_DEV), dgu8, pl.BlockSpec((None, tm, FF_BLK), lambda i, j, k: (k, i, 0)),
                    wgu, pl.BlockSpec((None, D, FF_BLK), lambda i, j, k: (k, 0, 0)), NT,
                    [jax.ShapeDtypeStruct((S, D), F32)], [pl.BlockSpec((tm, D), lambda i, j, k: (i, 0))], (tm, D), comm=comm)


def _adamw(name, parts, w, m, v, tr):
    rows, cols = w.shape
    n_parts = len(parts)
    c1 = 1.0 - ADAM_B1 ** ADAM_STEP
    c2 = 1.0 - ADAM_B2 ** ADAM_STEP

    def body(*refs):
        p_refs = refs[:n_parts]
        w_ref, m_ref, v_ref, g_ref, d_ref, nm_ref, nv_ref = refs[n_parts:]
        g = p_refs[0][...].astype(F32)
        for r in p_refs[1:]:
            g = g + r[...].astype(F32)
        mm = ADAM_B1 * m_ref[...] + (1.0 - ADAM_B1) * g
        vv = ADAM_B2 * v_ref[...] + (1.0 - ADAM_B2) * (g * g)
        g_ref[...] = g
        nm_ref[...] = mm
        nv_ref[...] = vv
        d_ref[...] = -ADAM_LR * ((mm / c1) / (jnp.sqrt(vv / c2) + ADAM_EPS) + ADAM_WD * w_ref[...])

    blk = pl.BlockSpec((tr, cols), lambda i: (i, 0))
    out = jax.ShapeDtypeStruct((rows, cols), F32)
    return pl.pallas_call(
        body, grid=(rows // tr,), in_specs=[blk] * (n_parts + 3), out_specs=[blk] * 4, out_shape=[out] * 4,
        compiler_params=_params("parallel"), name=name)(*parts, w, m, v)


def _position():
    return lax.axis_index("x"), lax.axis_index("y"), lax.axis_index("c")


def _other_chips():
    x, y, _ = _position()
    return [(1 - x, y), (x, 1 - y), (1 - x, 1 - y)]


def _remote(src, dst, send, recv, k, to):
    return pltpu.make_async_remote_copy(src_ref=src, dst_ref=dst, send_sem=send.at[k], recv_sem=recv.at[k],
                                        device_id=to, device_id_type=MESH)


def _ag_send(blocks, direct=False):
    n_peer = 7 if direct else 4

    def copies(ins, outs, send, recv, local, r0=0, l0=0):
        x, y, c = _position()
        me = 4 * x + 2 * y + c
        peers = [(x, y, 1 - c)] + [(px, py, c) for px, py in _other_chips()]
        if direct:
            peers += [(px, py, 1 - c) for px, py in _other_chips()]
        cps = []
        for t, (src, dst) in enumerate(zip(ins, outs)):
            cps.append(pltpu.make_async_copy(src, dst.at[me], local.at[l0 + t]))
            cps += [_remote(src, dst.at[me], send, recv, r0 + n_peer * t + k, to) for k, to in enumerate(peers)]
        return cps

    outs = tuple(jax.ShapeDtypeStruct((N_DEV,) + b.shape, b.dtype) for b in blocks)
    return _Comm(tuple(blocks), outs, {}, copies, n_peer * len(blocks), len(blocks))


def _ag_forward(bufs):
    def copies(ins, outs, send, recv, local, r0=0, l0=0):
        x, y, c = _position()
        cps = []
        for t, buf in enumerate(outs):
            for k, (px, py) in enumerate(_other_chips()):
                slot = buf.at[4 * px + 2 * py + c]
                cps.append(_remote(slot, slot, send, recv, r0 + 3 * t + k, (x, y, 1 - c)))
        return cps

    outs = tuple(jax.ShapeDtypeStruct(b.shape, b.dtype) for b in bufs)
    return _Comm(tuple(bufs), outs, {t: t for t in range(len(bufs))}, copies, 3 * len(bufs), 0)


def _rs_swap(shares):
    def copies(ins, outs, send, recv, local, r0=0, l0=0):
        x, y, c = _position()
        return [_remote(src.at[:, 1 - c], dst, send, recv, r0 + t, (x, y, 1 - c)) for t, (src, dst) in enumerate(zip(ins, outs))]

    ins = tuple(s.reshape((4, 2) + s.shape[1:]) for s in shares)
    outs = tuple(jax.ShapeDtypeStruct((4,) + s.shape[1:], s.dtype) for s in shares)
    return _Comm(ins, outs, {}, copies, len(shares), 0)


def _rs_exchange(sums):
    def copies(ins, outs, send, recv, local, r0=0, l0=0):
        _, _, c = _position()
        return [_remote(src.at[2 * px + py], dst.at[k], send, recv, r0 + 3 * t + k, (px, py, c))
                for t, (src, dst) in enumerate(zip(ins, outs)) for k, (px, py) in enumerate(_other_chips())]

    outs = tuple(jax.ShapeDtypeStruct((3,) + s.shape[1:], s.dtype) for s in sums)
    return _Comm(tuple(sums), outs, {}, copies, 3 * len(sums), 0)


def _comm_call(name, comm):
    return _call(name, lambda: None, (), [], [], [], [], (), (), comm)


def _pair_sum(name, share, got, core, tr):
    _, rows, cols = share.shape

    def body(c_ref, a_ref, b_ref, o_ref):
        o_ref[...] = (a_ref[...].astype(F32) + b_ref[...].astype(F32)).astype(o_ref.dtype)

    grid_spec = pltpu.PrefetchScalarGridSpec(
        num_scalar_prefetch=1, grid=(4, rows // tr),
        in_specs=[pl.BlockSpec((None, None, tr, cols), lambda q, i, c: (q, c[0], i, 0)),
                  pl.BlockSpec((None, tr, cols), lambda q, i, c: (q, i, 0))],
        out_specs=pl.BlockSpec((None, tr, cols), lambda q, i, c: (q, i, 0)))
    return pl.pallas_call(
        body, grid_spec=grid_spec, out_shape=jax.ShapeDtypeStruct((4, rows, cols), share.dtype),
        compiler_params=_params("parallel", "parallel"), name=name)(core, share.reshape(4, 2, rows, cols), got)


TENSORS = ("a_w_in", "a_w_out", "b_w_in", "b_w_out", "gu0", "gu1", "dn0", "dn1")
ROW_TILE = {"a_w_in": 256, "a_w_out": 128, "b_w_in": 256, "b_w_out": 128, "gu0": 256, "gu1": 256, "dn0": 176, "dn1": 176}
A_BLK = 9 * D_MODEL // N_DEV
B_BLK = 386
B_IN = 3 * D_MODEL + N_HEADS
B_IN_PAD = 3 * D_MODEL + LANES


def kernel(x, a_norm, a_w_in, a_w_out, b_norm, b_w_in, b_f, b_w_out, ffn_norm, ffn_w_gu, ffn_w_down, final_norm, loss_target, m_a_norm, m_a_w_in, m_a_w_out, m_b_norm, m_b_w_in, m_b_f, m_b_w_out, m_ffn_norm, m_ffn_w_gu, m_ffn_w_down, m_final_norm, v_a_norm, v_a_w_in, v_a_w_out, v_b_norm, v_b_w_in, v_b_f, v_b_w_out, v_ffn_norm, v_ffn_w_gu, v_ffn_w_down, v_final_norm):
    S = x.shape[1]
    xi, yi, ci = _position()
    dev = 4 * xi + 2 * yi + ci
    core = ci.reshape(1).astype(jnp.int32)
    h0, target = x.reshape(S, D_MODEL), loss_target.reshape(S, D_MODEL)

    def shards(a_in, a_out, b_in, b_out, gu, dn):
        return {"a_w_in": a_in[0], "a_w_out": a_out[0], "b_w_in": b_in[0], "b_w_out": b_out[0],
                "gu0": gu[0], "gu1": gu[1], "dn0": dn[0], "dn1": dn[1]}

    w_sh = shards(a_w_in, a_w_out, b_w_in, b_w_out, ffn_w_gu, ffn_w_down)
    m_sh = shards(m_a_w_in, m_a_w_out, m_b_w_in, m_b_w_out, m_ffn_w_gu, m_ffn_w_down)
    v_sh = shards(v_a_w_in, v_a_w_out, v_b_w_in, v_b_w_out, v_ffn_w_gu, v_ffn_w_down)
    wb = {n: w_sh[n].astype(BF16) for n in TENSORS}
    bf_pad = jnp.pad(b_f, ((0, 0), (0, LANES - N_HEADS)))
    tabs = _rope_tables(S)

    g_ain, g_aout = _comm_call("gather_a", _ag_send([wb["a_w_in"], wb["a_w_out"]]))
    g_ain, g_aout = _comm_call("forward_a", _ag_forward([g_ain, g_aout]))
    dils = [dil for _, dil in DILATED_PATTERNS]
    n0_views = _rms_fwd("rms_a", h0, a_norm[0], dils)
    n0 = n0_views[0]
    later =[wb["b_w_in"], wb["b_w_out"], wb["gu0"], wb["dn0"], jnp.pad(b_norm, ((0, 7), (0, 0)))]
    w_a_in = g_ain.transpose(1, 0, 2).reshape(D_MODEL, 9 * D_MODEL)
    qkv_a, later = _a_proj("proj_a", n0, w_a_in, tabs, _ag_send(later))
    cols = [lambda r: r] * 3
    groups = [(g, dil, S // dil, qkv_a[g]) for g, (window, dil) in enumerate(DILATED_PATTERNS)]
    fwd =[_dil_fwd("dil_fwd%d" % g, view, *cols, dil, L) for g, dil, L, view in groups]
    o_views, lse_views = _combine("dil_combine", [f[0] for f in fwd], [f[1] for f in fwd], dils)
    o_a = o_views[0]
    w_a_out = g_aout.reshape(D_MODEL, D_MODEL)
    h1, (g_bin, g_bout, g_gu0, g_dn0, g_bnorm) = _matmul("out_a", o_a, w_a_out, "nn", F32, TM, 1024, 1024, resid=h0,
                                                         comm=_ag_forward(later))

    n1 = _rms_fwd("rms_f0", h1, ffn_norm[0])
    gu0, act0, g_gu1 = _ffn_gu("gu_f0", n1, g_gu0, _ag_send([wb["gu1"]]))
    w_dn0 = g_dn0.reshape(D_FF, D_MODEL)
    h2, g_dn1 = _ffn_down("down_f0", act0, w_dn0, h1, _ag_send([wb["dn1"]]))

    b_norm_full = g_bnorm[:, 0].reshape(D_MODEL)
    w_b_in = g_bin.transpose(1, 0, 2).reshape(D_MODEL, B_IN)
    w_b_gate = jnp.pad(w_b_in[:, 3 * D_MODEL:], ((0, 0), (0, LANES - N_HEADS)))
    w_b_cat = jnp.concatenate([w_b_in[:, :3 * D_MODEL], w_b_gate], axis=1)
    w_b_out = g_bout.reshape(D_MODEL, D_MODEL)
    n2 = _rms_fwd("rms_b", h2, b_norm_full)
    qkv, (g_gu1, g_dn1) = _matmul("proj_b", n2, w_b_in[:, :3 * D_MODEL], "nn", BF16, TM, 1024, 1024, col0_scale=SOFTMAX_SCALE,
                                  comm=_ag_forward([g_gu1, g_dn1]))
    z = _matmul("gate_b", n2, w_b_gate, "nn", F32, TM, LANES, 1024)
    kbias = _gate_fwd("gate_cumsum", z, bf_pad)
    tf = min(S, 512)
    o_b, lse_b = _fox_fwd("fox_fwd", qkv, kbias, tf)
    h3 = _matmul("out_b", o_b, w_b_out, "nn", F32, TM, 1024, 1024, resid=h2)

    w_dn1 = g_dn1.reshape(D_FF, D_MODEL)
    n3 = _rms_fwd("rms_f1", h3, ffn_norm[1])
    gu1, act1 = _ffn_gu("gu_f1", n3, g_gu1)
    h4 = _ffn_down("down_f1", act1, w_dn1, h3)[0]

    dh4, d_final, loss = _loss_head("loss_head", h4, final_norm, target)

    share, got, sums, others = {}, {}, {}, {}

    def pair_sums(*names):
        for n in names:
            sums[n] = _pair_sum("pair_" + n, share[n], got[n], core, ROW_TILE[n])

    dgu1 = _ffn_dact("dact_f1", dh4, w_dn1, gu1)[0]
    share["dn1"] = _ffn_dwd("dwd_f1", act1, dh4)
    share["gu1"] = _ffn_dwgu("dwgu_f1", n3, dgu1)[0]
    dn3, got["gu1"], got["dn1"] = _ffn_dn("dn_f1", dgu1, g_gu1, _rs_swap([share["gu1"], share["dn1"]]))
    dh3, d_ffn1 = _rms_bwd("rmsb_f1", dn3, h3, ffn_norm[1], dh4)
    pair_sums("gu1", "dn1")

    do_b = _matmul("dout_b", dh3, w_b_out, "nt", BF16, TM, 1024, 1024)
    share["b_w_out"] = _matmul("dwout_b", o_b, dh3, "tn", BF16, TM, 1024, 1024).reshape(N_DEV, 128, D_MODEL)
    dq_b, dk_b, dv_b, ds_rowsum, ds_colsum = _fox_bwd("fox_bwd", qkv, kbias, do_b, o_b, lse_b, tf)
    dc = ds_rowsum[:, :N_HEADS] - ds_colsum.reshape(N_HEADS, S).T
    dz, d_bf = _gate_bwd("gate_bwd", jnp.pad(dc, ((0, 0), (0, LANES - N_HEADS))), z, bf_pad)
    dproj_b = jnp.concatenate([dq_b, dk_b, dv_b, dz.astype(BF16)], axis=1)
    dw_b_in, (others["gu1"],) = _matmul("dwin_b", n2, dproj_b, "tn", BF16, TM, B_IN_PAD // 5, 1024, comm=_rs_exchange([sums["gu1"]]))
    dn2, (others["dn1"],) = _matmul("dn_b", dproj_b, w_b_cat, "nt", F32, TM, 1024, B_IN_PAD // 5, comm=_rs_exchange([sums["dn1"]]))
    dh2, d_bnorm = _rms_bwd("rmsb_b", dn2, h2, b_norm_full, dh3)
    share["b_w_in"] = dw_b_in[:, :B_IN].reshape(D_MODEL, N_DEV, B_BLK).transpose(1, 0, 2)

    dgu0, got["b_w_in"], got["b_w_out"] = _ffn_dact("dact_f0", dh2, w_dn0, gu0, _rs_swap([share["b_w_in"], share["b_w_out"]]))
    share["dn0"] = _ffn_dwd("dwd_f0", act0, dh2)
    pair_sums("b_w_in", "b_w_out")
    share["gu0"], others["b_w_in"], others["b_w_out"] = _ffn_dwgu(
        "dwgu_f0", n1, dgu0, _rs_exchange([sums["b_w_in"], sums["b_w_out"]]))
    dn1, got["gu0"], got["dn0"] = _ffn_dn("dn_f0", dgu0, g_gu0, _rs_swap([share["gu0"], share["dn0"]]))
    dh1, d_ffn0 = _rms_bwd("rmsb_f0", dn1, h1, ffn_norm[0], dh2)
    pair_sums("gu0", "dn0")

    do_views = _matmul_nt_views("dout_a", dh1, w_a_out, dils)
    share["a_w_out"] = _matmul("dwout_a", o_a, dh1, "tn", BF16, TM, 1024, 1024).reshape(N_DEV, 128, D_MODEL)
    pieces = []
    for g, dil, L, view in groups:
        rot = tuple(tb.reshape(L, dil * LANES) for tb in tabs)
        grads = _dil_bwd("dil_bwd%d" % g, view, do_views[g], o_views[g], lse_views[g], rot, dil, L)
        pieces += list(grads)
    dw_a_in, others["gu0"], others["dn0"] = _a_dw("dwin_a", n0_views, pieces, dils, _rs_exchange([sums["gu0"], sums["dn0"]]))
    share["a_w_in"] = dw_a_in.reshape(D_MODEL, N_DEV, A_BLK).transpose(1, 0, 2)
    got["a_w_in"], got["a_w_out"] = _comm_call("swap_a", _rs_swap([share["a_w_in"], share["a_w_out"]]))
    pair_sums("a_w_in", "a_w_out")
    dn0, others["a_w_in"], others["a_w_out"] = _a_dn("dn_a", pieces, dils, w_a_in, _rs_exchange([sums["a_w_in"], sums["a_w_out"]]))
    dx, d_anorm = _rms_bwd("rmsb_a", dn0, h0, a_norm[0], dh1)

    misc = jnp.concatenate([d_bf[:, :N_HEADS], loss[:, :1], jnp.zeros((1, D_MODEL - N_HEADS - 1), F32)], axis=1)
    small = jnp.concatenate([d_anorm, d_ffn0, d_ffn1, d_final, d_bnorm, misc, jnp.zeros((2, D_MODEL), F32)], axis=0)
    small_all, = _comm_call("gather_small", _ag_send([small], direct=True))

    outs = {}
    for n in TENSORS:
        mine = lax.dynamic_index_in_dim(sums[n], 2 * xi + yi, axis=0, keepdims=False)
        outs[n] = _adamw("adamw_" + n, [mine] + [others[n][k] for k in range(3)], w_sh[n], m_sh[n], v_sh[n], ROW_TILE[n])

    pad_vec = lambda a: jnp.pad(a, ((0, 0), (0, D_MODEL - a.shape[1])))

    def small_pack(an, fn, fin, bf):
        return jnp.concatenate([an, fn, fin.reshape(1, D_MODEL), jnp.zeros((1, D_MODEL), F32), pad_vec(bf),
                                jnp.zeros((2, D_MODEL), F32)], axis=0)

    sg, sd, sm, sv = _adamw("adamw_small", [small_all[d] for d in range(N_DEV)], small_pack(a_norm, ffn_norm, final_norm, b_f),
                            small_pack(m_a_norm, m_ffn_norm, m_final_norm, m_b_f),
                            small_pack(v_a_norm, v_ffn_norm, v_final_norm, v_b_f), 8)
    g_bn = lax.dynamic_slice(sg[4:5], (0, dev * LANES), (1, LANES))
    bn = _adamw("adamw_b_norm", [g_bn], b_norm, m_b_norm, v_b_norm, 1)

    def tree(i):
        full = lambda name, ref: outs[name][i].reshape(ref.shape)
        sml = (sg, sd, sm, sv)[i]
        return dict(
            a_norm=sml[0:1], a_w_in=full("a_w_in", a_w_in), a_w_out=full("a_w_out", a_w_out), b_norm=bn[i],
            b_w_in=full("b_w_in", b_w_in), b_f=sml[5:6, :N_HEADS], b_w_out=full("b_w_out", b_w_out), ffn_norm=sml[1:3],
            ffn_w_gu=jnp.stack([outs["gu0"][i], outs["gu1"][i]]).reshape(ffn_w_gu.shape),
            ffn_w_down=jnp.stack([outs["dn0"][i], outs["dn1"][i]]).reshape(ffn_w_down.shape), final_norm=sml[3])

    order = ("a_norm", "a_w_in", "a_w_out", "b_norm", "b_w_in", "b_f", "b_w_out", "ffn_norm", "ffn_w_gu", "ffn_w_down", "final_norm")
    result = [sg[5, N_HEADS], dx.reshape(x.shape)]
    for i in range(4):
        t = tree(i)
        result += [t[n] for n in order]
    return tuple(result)
```

```python
import functools
from typing import Callable, NamedTuple

import jax
import jax.numpy as jnp
from jax import lax
from jax.experimental import pallas as pl
from jax.experimental.pallas import tpu as pltpu

F32 = jnp.float32
BF16 = jnp.bfloat16

D_MODEL = 1024
N_HEADS = 16
HEAD_DIM = 64
N_PAIRS = N_HEADS // 2
LANES = 128
DILATED_PATTERNS = ((128, 1), (512, 4), (2048, 16))
BAND_STEPS = 128
ROT_DIM = HEAD_DIM // 4
ROPE_THETA = 500000.0
D_FF = 2816
RMS_EPS = 1e-6
NEG_INF = -1e30
SOFTMAX_SCALE = HEAD_DIM ** -0.5
N_DEV = 8
FF_BLK = 2 * D_FF // N_DEV
ADAM_LR, ADAM_B1, ADAM_B2, ADAM_EPS, ADAM_WD, ADAM_STEP = 0.001, 0.9, 0.999, 1e-08, 0.01, 10
VMEM_LIMIT = 52 * 1024 * 1024
FOX_BWD_VMEM = 60 * 1024 * 1024
TM = 1024
MESH = pl.DeviceIdType.MESH

NN = (((1,), (0,)), ((), ()))
NT = (((1,), (1,)), ((), ()))
TN = (((0,), (0,)), ((), ()))


def _params(*sem):
    return pltpu.CompilerParams(dimension_semantics=sem, vmem_limit_bytes=VMEM_LIMIT)


def _dot(a, b, dims):
    return lax.dot_general(a, b, dims, preferred_element_type=F32)


class _Comm(NamedTuple):
    ins: tuple
    outs: tuple
    aliases: dict
    copies: Callable
    n_remote: int
    n_local: int


def _call(name, body, grid, in_specs, out_specs, out_shape, scratch, args, sem, comm=None):
    if comm is None:
        return pl.pallas_call(body, grid=grid, in_specs=in_specs, out_specs=out_specs, out_shape=out_shape,
                              scratch_shapes=scratch, compiler_params=_params(*sem), name=name)(*args)
    n_in, n_out = len(in_specs), len(out_specs)
    n_ci, n_co = len(comm.ins), len(comm.outs)
    o0 = n_in + n_ci

    def hosted(*refs):
        c_ins, c_outs = refs[n_in:o0], refs[o0 + n_out:o0 + n_out + n_co]
        sems = refs[-3:]

        def start():
            for cp in comm.copies(c_ins, c_outs, *sems):
                cp.start()

        def wait():
            for cp in comm.copies(c_ins, c_outs, *sems):
                cp.wait()

        if not grid:
            start()
            body()
            wait()
            return
        ids = [pl.program_id(ax) for ax in range(len(grid))]
        pl.when(functools.reduce(jnp.logical_and, [i == 0 for i in ids]))(start)
        body(*refs[:n_in], *refs[o0:o0 + n_out], *refs[o0 + n_out + n_co:-3])
        pl.when(functools.reduce(jnp.logical_and, [i == g - 1 for i, g in zip(ids, grid)]))(wait)

    hbm = pl.BlockSpec(memory_space=pltpu.HBM)
    dma = pltpu.SemaphoreType.DMA
    return pl.pallas_call(
        hosted, grid=grid, in_specs=[*in_specs, *[hbm] * n_ci], out_specs=[*out_specs, *[hbm] * n_co],
        out_shape=[*out_shape, *comm.outs], input_output_aliases={n_in + i: n_out + o for i, o in comm.aliases.items()},
        scratch_shapes=[*scratch, dma((comm.n_remote,)), dma((comm.n_remote,)), dma((max(comm.n_local, 1),))],
        compiler_params=_params(*["arbitrary"] * len(grid)), name=name)(*args, *comm.ins)


def _mm_call(name, grid, a, a_spec, b, b_spec, dims, out_shapes, out_specs, acc_shape, epilogue=None,
             extras=(), extra_specs=(), col_axis=1, comm=None):
    nk = grid[2]
    n_extra = len(extras)
    n_out = len(out_shapes)

    def finish(res, ex, outs, j):
        if epilogue is None:
            outs[0][...] = res.astype(outs[0].dtype)
        else:
            epilogue(res, ex, outs, j)

    def body(*refs):
        a_ref, b_ref = refs[0], refs[1]
        ex = refs[2:2 + n_extra]
        outs = refs[2 + n_extra:2 + n_extra + n_out]
        j, k = pl.program_id(col_axis), pl.program_id(2)
        part = _dot(a_ref[...].astype(BF16), b_ref[...].astype(BF16), dims)
        if nk == 1:
            finish(part, ex, outs, j)
            return
        acc = refs[-1]

        @pl.when(k == 0)
        def _():
            acc[...] = part

        @pl.when((k > 0) & (k < nk - 1))
        def _():
            acc[...] += part

        @pl.when(k == nk - 1)
        def _():
            finish(acc[...] + part, ex, outs, j)

    return _call(name, body, grid, [a_spec, b_spec, *extra_specs], out_specs, out_shapes,
                 [] if nk == 1 else [pltpu.VMEM(acc_shape, F32)], (a, b, *extras), ("parallel", "parallel", "arbitrary"), comm)


def _matmul(name, a, b, mode, out_dtype, tm, tn, tk, resid=None, col0_scale=None, comm=None):
    if mode == "nn":
        (M, K), N = a.shape, b.shape[1]
        a_spec = pl.BlockSpec((tm, tk), lambda j, i, k: (i, k))
        b_spec = pl.BlockSpec((tk, tn), lambda j, i, k: (k, j))
        dims = NN
    elif mode == "nt":
        (M, K), N = a.shape, b.shape[0]
        a_spec = pl.BlockSpec((tm, tk), lambda j, i, k: (i, k))
        b_spec = pl.BlockSpec((tn, tk), lambda j, i, k: (j, k))
        dims = NT
    else:
        (K, M), N = a.shape, b.shape[1]
        a_spec = pl.BlockSpec((tk, tm), lambda j, i, k: (k, i))
        b_spec = pl.BlockSpec((tk, tn), lambda j, i, k: (k, j))
        dims = TN
    assert M % tm == 0 and N % tn == 0 and K % tk == 0, (name, M, N, K, tm, tn, tk)
    o_spec = pl.BlockSpec((tm, tn), lambda j, i, k: (i, j))
    extras, extra_specs, epilogue = (), (), None
    if resid is not None:
        extras, extra_specs = (resid,), (o_spec,)

        def epilogue(acc, ex, outs, j):
            outs[0][...] = (acc + ex[0][...]).astype(outs[0].dtype)

    elif col0_scale is not None:

        def epilogue(acc, ex, outs, j):
            outs[0][...] = (acc * jnp.where(j == 0, col0_scale, 1.0)).astype(outs[0].dtype)

    res = _mm_call(name, (N // tn, M // tm, K // tk), a, a_spec, b, b_spec, dims, [jax.ShapeDtypeStruct((M, N), out_dtype)],
                   [o_spec], (tm, tn), epilogue, extras, extra_specs, col_axis=0, comm=comm)
    return res[0] if comm is None else (res[0], res[1:])


def _rms_fwd(name, h, gain, dils=(1,), tm=512):
    S, D = h.shape

    def body(h_ref, g_ref, *rest):
        x = h_ref[...]
        rstd = lax.rsqrt(jnp.mean(x * x, axis=-1, keepdims=True) + RMS_EPS)
        y = x * rstd * g_ref[...]
        _write_views([y[:, b * LANES:(b + 1) * LANES] for b in range(N_PAIRS)], rest[-1], rest[:-1], dils, tm)

    res = pl.pallas_call(
        body, grid=(S // tm,), in_specs=[pl.BlockSpec((tm, D), lambda i: (i, 0)), pl.BlockSpec((1, D), lambda i: (0, 0))],
        out_specs=[_view_spec(tm, R) for R in dils], out_shape=[jax.ShapeDtypeStruct((S // R, R * D), BF16) for R in dils],
        scratch_shapes=[pltpu.VMEM((N_PAIRS, tm, LANES), F32)], compiler_params=_params("parallel"),
        name=name)(h, gain.reshape(1, D))
    return res[0] if len(dils) == 1 else res


def _rms_bwd(name, dn, h, gain, dres, copy16=True, tm=512):
    S, D = h.shape

    def body(dn_ref, h_ref, g_ref, r_ref, dh_ref, dg_ref, *dh16_ref):
        x = h_ref[...]
        rstd = lax.rsqrt(jnp.mean(x * x, axis=-1, keepdims=True) + RMS_EPS)
        xhat = x * rstd
        d = dn_ref[...]
        dxhat = d * g_ref[...]
        dh = rstd * (dxhat - xhat * jnp.mean(dxhat * xhat, axis=-1, keepdims=True)) + r_ref[...]
        dh_ref[...] = dh
        if copy16:
            dh16_ref[0][...] = dh.astype(BF16)

        @pl.when(pl.program_id(0) == 0)
        def _():
            dg_ref[...] = jnp.zeros_like(dg_ref)

        dg_ref[...] += jnp.sum(d * xhat, axis=0, keepdims=True)

    row = pl.BlockSpec((tm, D), lambda i: (i, 0))
    vec = pl.BlockSpec((1, D), lambda i: (0, 0))
    return pl.pallas_call(
        body, grid=(S // tm,), in_specs=[row, row, vec, row], out_specs=[row, vec] + [row] * copy16,
        out_shape=[jax.ShapeDtypeStruct((S, D), F32), jax.ShapeDtypeStruct((1, D), F32)] + [jax.ShapeDtypeStruct((S, D), BF16)] * copy16,
        compiler_params=_params("arbitrary"), name=name)(dn, h, gain.reshape(1, D), dres)


def _loss_head(name, h, gain, target, tm=512):
    S, D = h.shape

    def body(h_ref, g_ref, t_ref, dh_ref, dg_ref, loss_ref, dh16_ref):
        x = h_ref[...]
        rstd = lax.rsqrt(jnp.mean(x * x, axis=-1, keepdims=True) + RMS_EPS)
        xhat = x * rstd
        err = xhat * g_ref[...] - t_ref[...]
        dy = err * (1.0 / D)
        dxhat = dy * g_ref[...]
        dh = rstd * (dxhat - xhat * jnp.mean(dxhat * xhat, axis=-1, keepdims=True))
        dh_ref[...] = dh
        dh16_ref[...] = dh.astype(BF16)

        @pl.when(pl.program_id(0) == 0)
        def _():
            dg_ref[...] = jnp.zeros_like(dg_ref)
            loss_ref[...] = jnp.zeros_like(loss_ref)

        dg_ref[...] += jnp.sum(dy * xhat, axis=0, keepdims=True)
        part = 0.5 * jnp.sum(jnp.mean(err * err, axis=-1, keepdims=True), axis=0, keepdims=True)
        loss_ref[...] += jnp.broadcast_to(part, loss_ref.shape)

    row = pl.BlockSpec((tm, D), lambda i: (i, 0))
    vec = pl.BlockSpec((1, D), lambda i: (0, 0))
    return pl.pallas_call(
        body, grid=(S // tm,), in_specs=[row, vec, row], out_specs=[row, vec, pl.BlockSpec((1, LANES), lambda i: (0, 0)), row],
        out_shape=[jax.ShapeDtypeStruct((S, D), F32), jax.ShapeDtypeStruct((1, D), F32),
                   jax.ShapeDtypeStruct((1, LANES), F32), jax.ShapeDtypeStruct((S, D), BF16)],
        compiler_params=_params("arbitrary"), name=name)(h, gain.reshape(1, D), target)


def _rope_tables(S):
    half = ROT_DIM // 2
    inv_freq = ROPE_THETA ** (-jnp.arange(half, dtype=F32) * 2.0 / ROT_DIM)
    ang = jnp.arange(S, dtype=F32)[:, None] * inv_freq[None, :]
    cos, sin = jnp.cos(ang), jnp.sin(ang)
    one = jnp.ones((S, HEAD_DIM - ROT_DIM), F32)
    zero = jnp.zeros((S, HEAD_DIM - ROT_DIM), F32)
    zh = jnp.zeros((S, half), F32)
    c = jnp.concatenate([cos, cos, one], axis=1)
    sa = jnp.concatenate([-sin, zh, zero], axis=1)
    sb = jnp.concatenate([zh, sin, zero], axis=1)
    return tuple(jnp.concatenate([t, t], axis=1) for t in (c, sa, sb))


def _rotate(x, c, sa, sb, sign):
    return x * c + sign * (pltpu.roll(x, LANES - ROT_DIM // 2, 1) * sa + pltpu.roll(x, ROT_DIM // 2, 1) * sb)


def _stage_chunks(scr, chunks):
    for c, x in enumerate(chunks):
        scr[c] = x


def _strided_rows(scr, c, r, n, R):
    return scr.at[c][pl.ds(r, n, stride=R), :]


def _a_proj(name, n, w, tabs, comm, tm=512):
    S, D = n.shape
    n_i = S // tm
    dils = [dil for _, dil in DILATED_PATTERNS]
    n_out = 3 * len(dils)

    def body(n_ref, w_ref, c_ref, sa_ref, sb_ref, *rest):
        outs, scr = rest[:n_out], rest[n_out]
        j = pl.program_id(0)
        acc = _dot(n_ref[...], w_ref[...], NN)
        c, sa, sb = c_ref[...], sa_ref[...], sb_ref[...]
        for J in range(n_out):
            R, kind = dils[J // 3], J % 3

            @pl.when(j == J)
            def _(J=J, R=R, kind=kind):
                chunks = [acc[:, b * LANES:(b + 1) * LANES] for b in range(N_PAIRS)]
                if kind < 2:
                    chunks = [_rotate(x, c, sa, sb, 1.0) * (SOFTMAX_SCALE if kind == 0 else 1.0) for x in chunks]
                if R == 1:
                    for b, x in enumerate(chunks):
                        outs[J][:, b * LANES:(b + 1) * LANES] = x.astype(BF16)
                    return
                _stage_chunks(scr, chunks)
                for r in range(R):
                    for b in range(N_PAIRS):
                        col = r * D_MODEL + b * LANES
                        outs[J][:, col:col + LANES] = _strided_rows(scr, b, r, tm // R, R).astype(BF16)

    def out_spec(J, R):
        return pl.BlockSpec((tm // R, R * D_MODEL), lambda j, i: (jnp.where(j == J, i, jnp.where(j < J, 0, n_i - 1)), 0))

    tab = pl.BlockSpec((tm, LANES), lambda j, i: (i, 0))
    res = _call(name, body, (n_out, n_i),
                [pl.BlockSpec((tm, D), lambda j, i: (i, 0)), pl.BlockSpec((D, D_MODEL), lambda j, i: (0, j)), tab, tab, tab],
                [out_spec(J, dils[J // 3]) for J in range(n_out)],
                [jax.ShapeDtypeStruct((S // dils[J // 3], dils[J // 3] * D_MODEL), BF16) for J in range(n_out)],
                [pltpu.VMEM((N_PAIRS, tm, LANES), F32)], (n, w, *tabs), ("arbitrary", "arbitrary"), comm)
    return [res[3 * g:3 * g + 3] for g in range(len(dils))], res[n_out:]


def _unstride(src_chunk, R, tok, rows):
    for r in range(R):
        for b in range(N_PAIRS):
            tok.at[b][pl.ds(r, rows // R, stride=R), :] = src_chunk(r, b).astype(F32)


def _by_residue(ref, R):
    return ref[...] if R == 1 else jnp.concatenate([ref[:, r * D_MODEL:(r + 1) * D_MODEL] for r in range(R)], axis=0)


def _a_dw(name, n_views, pieces, dils, comm, tk=512):
    D = D_MODEL
    S = n_views[0].shape[0] * dils[0]
    n_k = S // tk
    n_p, n_g = len(pieces), len(dils)

    def body(*refs):
        n_refs, p_refs, o_ref, acc = refs[:n_g], refs[n_g:n_g + n_p], refs[n_g + n_p], refs[n_g + n_p + 1]
        j, k = pl.program_id(0), pl.program_id(1)
        for J in range(n_p):

            @pl.when(j == J)
            def _(J=J):
                R = dils[J // 3]
                part = _dot(_by_residue(n_refs[J // 3], R), _by_residue(p_refs[J], R), TN)

                @pl.when(k == 0)
                def _():
                    acc[...] = part

                @pl.when(k > 0)
                def _():
                    acc[...] += part

        @pl.when(k == n_k - 1)
        def _():
            o_ref[...] = acc[...].astype(BF16)

    def piece_spec(J):
        R = dils[J // 3]
        return pl.BlockSpec((tk // R, R * D_MODEL), lambda j, k: (jnp.where(j == J, k, jnp.where(j < J, 0, n_k - 1)), 0))

    n_specs = [pl.BlockSpec((tk // R, R * D_MODEL), lambda j, k: (k, 0)) for R in dils]
    return _call(name, body, (n_p, n_k), n_specs + [piece_spec(J) for J in range(n_p)],
                 [pl.BlockSpec((D, D_MODEL), lambda j, k: (0, j))], [jax.ShapeDtypeStruct((D, n_p * D_MODEL), BF16)],
                 [pltpu.VMEM((D, D_MODEL), F32)], (*n_views, *pieces), ("arbitrary", "arbitrary"), comm)


def _a_dn(name, pieces, dils, w, comm, tm=512):
    D = w.shape[0]
    S = pieces[0].shape[0] * dils[0]
    n_p = len(pieces)

    def body(*refs):
        p_refs, w_ref, o_ref, acc, part_acc, tok = refs[:n_p], refs[n_p], refs[n_p + 1], refs[n_p + 2], refs[n_p + 3], refs[n_p + 4]
        j = pl.program_id(1)
        for J in range(n_p):

            @pl.when(j == J)
            def _(J=J):
                R, t = dils[J // 3], J % 3
                part = _dot(_by_residue(p_refs[J], R), w_ref[...], NT)
                if R == 1:
                    if J == 0:
                        acc[...] = part
                    else:
                        acc[...] += part
                    return
                if t == 0:
                    part_acc[...] = part
                    return
                if t == 1:
                    part_acc[...] += part
                    return
                n = tm // R
                _unstride(lambda r, b: part_acc[r * n:(r + 1) * n, b * LANES:(b + 1) * LANES]
                          + part[r * n:(r + 1) * n, b * LANES:(b + 1) * LANES], R, tok, tm)
                total = acc[...] + jnp.concatenate([tok[b] for b in range(N_PAIRS)], axis=1)
                if J == n_p - 1:
                    o_ref[...] = total
                else:
                    acc[...] = total

    specs = [pl.BlockSpec((tm // dils[J // 3], dils[J // 3] * D_MODEL), lambda i, j: (i, 0)) for J in range(n_p)]
    return _call(name, body, (S // tm, n_p), specs + [pl.BlockSpec((D, D_MODEL), lambda i, j: (0, j))],
                 [pl.BlockSpec((tm, D), lambda i, j: (i, 0))], [jax.ShapeDtypeStruct((S, D), F32)],
                 [pltpu.VMEM((tm, D), F32), pltpu.VMEM((tm, D), F32), pltpu.VMEM((N_PAIRS, tm, LANES), F32)], (*pieces, w),
                 ("arbitrary", "arbitrary"), comm)


def _lo_lanes():
    return lax.broadcasted_iota(jnp.int32, (1, LANES), 1) < HEAD_DIM


def _rep_rows(x2, lo):
    sw = pltpu.roll(x2, HEAD_DIM, 1)
    return jnp.where(lo, x2, sw), jnp.where(lo, sw, x2)


def _pair_cols(h):
    return slice((h // 2) * LANES, (h // 2 + 1) * LANES)


def _head_lanes(lo, h):
    return lo if h % 2 == 0 else jnp.logical_not(lo)


def _band_masks(t, first):
    ri = lax.broadcasted_iota(jnp.int32, (t, t), 0)
    ci = lax.broadcasted_iota(jnp.int32, (t, t), 1)
    neg_prev = jnp.where((ci >= ri) & jnp.logical_not(first), 0.0, NEG_INF)
    neg_cur = jnp.where(ci <= ri, 0.0, NEG_INF)
    return neg_prev, neg_cur


def _dil_specs(L, R, t, qcol, kcol, vcol):
    W = D_MODEL
    prev = lambda qi: jnp.maximum(qi - 1, 0)
    return dict(
        q=pl.BlockSpec((t, W), lambda r, qi: (qi, qcol(r))),
        kp=pl.BlockSpec((t, W), lambda r, qi: (prev(qi), kcol(r))), kc=pl.BlockSpec((t, W), lambda r, qi: (qi, kcol(r))),
        vp=pl.BlockSpec((t, W), lambda r, qi: (prev(qi), vcol(r))), vc=pl.BlockSpec((t, W), lambda r, qi: (qi, vcol(r))),
        own=pl.BlockSpec((t, W), lambda r, qi: (qi, r)), tab=pl.BlockSpec((t, LANES), lambda r, qi: (qi, r)))


def _dil_fwd(name, x, qcol, kcol, vcol, R, L):
    t = BAND_STEPS
    W = D_MODEL
    sp = _dil_specs(L, R, t, qcol, kcol, vcol)

    def body(q_ref, kp_ref, kc_ref, vp_ref, vc_ref, o_ref, lse_ref):
        lo = _lo_lanes()
        neg_p, neg_c = _band_masks(t, pl.program_id(1) == 0)
        s_p, s_c = [], []
        for h in range(N_HEADS):
            cols = _pair_cols(h)
            qh = jnp.where(_head_lanes(lo, h), q_ref[:, cols], 0)
            s_p.append(_dot(qh, kp_ref[:, cols], NT))
            s_c.append(_dot(qh, kc_ref[:, cols], NT))
        s_p = jnp.stack(s_p) + neg_p[None]
        s_c = jnp.stack(s_c) + neg_c[None]
        m = jnp.maximum(jnp.max(s_p, axis=2, keepdims=True), jnp.max(s_c, axis=2, keepdims=True))
        p_p, p_c = jnp.exp(s_p - m), jnp.exp(s_c - m)
        l = jnp.sum(p_p, axis=2, keepdims=True) + jnp.sum(p_c, axis=2, keepdims=True)
        inv, lse = 1.0 / l, m + jnp.log(l)
        p_p, p_c = p_p.astype(BF16), p_c.astype(BF16)
        for p in range(N_PAIRS):
            cols = _pair_cols(2 * p)
            o2 = jnp.zeros((t, LANES), F32)
            for h in (2 * p, 2 * p + 1):
                hm = _head_lanes(lo, h)
                pv = _dot(p_p[h], jnp.where(hm, vp_ref[:, cols], 0), NN) + _dot(p_c[h], jnp.where(hm, vc_ref[:, cols], 0), NN)
                o2 = o2 + pv * inv[h]
            o_ref[:, cols] = o2
            lse_ref[:, cols] = jnp.where(lo, lse[2 * p], lse[2 * p + 1])

    return pl.pallas_call(
        body, grid=(R, L // t), in_specs=[sp["q"], sp["kp"], sp["kc"], sp["vp"], sp["vc"]], out_specs=[sp["own"], sp["own"]],
        out_shape=[jax.ShapeDtypeStruct((L, R * W), F32), jax.ShapeDtypeStruct((L, R * W), F32)],
        compiler_params=_params("parallel", "parallel"), name=name)(x[0], x[1], x[1], x[2], x[2])


def _dil_scores(lo, q_ref, do_ref, o_ref, lse_ref, kv_refs):
    s = [[] for _ in kv_refs]
    dp = [[] for _ in kv_refs]
    lse, d = [], []
    for h in range(N_HEADS):
        cols = _pair_cols(h)
        hm = _head_lanes(lo, h)
        qh, doh = jnp.where(hm, q_ref[:, cols], 0), jnp.where(hm, do_ref[:, cols], 0)
        for i, (k_ref, v_ref) in enumerate(kv_refs):
            s[i].append(_dot(qh, k_ref[:, cols], NT))
            dp[i].append(_dot(doh, v_ref[:, cols], NT))
        lse.append(_rep_rows(lse_ref[:, cols], lo)[h % 2])
        dd = do_ref[:, cols].astype(F32) * o_ref[:, cols].astype(F32)
        d.append(jnp.sum(jnp.where(hm, dd, 0.0), axis=1, keepdims=True))
    return (*[jnp.stack(x) for x in s], *[jnp.stack(x) for x in dp], jnp.stack(lse), jnp.stack(d))


def _dil_bwd(name, x, do, o, lse, tabs, R, L):
    t = BAND_STEPS
    W = D_MODEL
    nq = L // t
    qb = lambda step: nq - 1 - step
    kb = lambda step: jnp.maximum(qb(step) - 1, 0)
    at = lambda f, width: pl.BlockSpec((t, width), lambda r, step: (f(step), r))

    def body(q_ref, kp_ref, kc_ref, vp_ref, vc_ref, do_ref, o_ref, lse_ref, c_ref, sa_ref, sb_ref, dq_ref, dk_ref, dv_ref,
             dk_scr, dv_scr):
        qi = nq - 1 - pl.program_id(1)
        lo = _lo_lanes()
        unrotate = lambda x: _rotate(x, c_ref[...], sa_ref[...], sb_ref[...], -1.0).astype(BF16)

        @pl.when(qi == nq - 1)
        def _():
            dk_scr[...] = jnp.zeros_like(dk_scr)
            dv_scr[...] = jnp.zeros_like(dv_scr)

        neg_p, neg_c = _band_masks(t, qi == 0)
        s_p, s_c, dp_p, dp_c, lse_h, d = _dil_scores(lo, q_ref, do_ref, o_ref, lse_ref, ((kp_ref, vp_ref), (kc_ref, vc_ref)))
        p_p, p_c = jnp.exp(s_p + neg_p[None] - lse_h), jnp.exp(s_c + neg_c[None] - lse_h)
        ds_p, ds_c = (p_p * (dp_p - d)).astype(BF16), (p_c * (dp_c - d)).astype(BF16)
        p_p, p_c = p_p.astype(BF16), p_c.astype(BF16)
        for p in range(N_PAIRS):
            cols = _pair_cols(2 * p)
            dq2 = jnp.zeros((t, LANES), F32)
            dk_cur, dv_cur = dk_scr[:, cols], dv_scr[:, cols]
            dk_prev, dv_prev = jnp.zeros((t, LANES), F32), jnp.zeros((t, LANES), F32)
            for h in (2 * p, 2 * p + 1):
                hm = _head_lanes(lo, h)
                qh, doh = jnp.where(hm, q_ref[:, cols], 0), jnp.where(hm, do_ref[:, cols], 0)
                dq2 = dq2 + _dot(ds_p[h], jnp.where(hm, kp_ref[:, cols], 0), NN) + _dot(ds_c[h], jnp.where(hm, kc_ref[:, cols], 0), NN)
                dk_prev, dv_prev = dk_prev + _dot(ds_p[h], qh, TN), dv_prev + _dot(p_p[h], doh, TN)
                dk_cur, dv_cur = dk_cur + _dot(ds_c[h], qh, TN), dv_cur + _dot(p_c[h], doh, TN)
            dq_ref[:, cols] = unrotate(dq2 * SOFTMAX_SCALE)
            dk_ref[:, cols] = unrotate(dk_cur)
            dv_ref[:, cols] = dv_cur.astype(BF16)
            dk_scr[:, cols] = dk_prev
            dv_scr[:, cols] = dv_prev

    wide = jax.ShapeDtypeStruct((L, R * W), BF16)
    return pl.pallas_call(
        body, grid=(R, nq),
        in_specs=[at(qb, W), at(kb, W), at(qb, W), at(kb, W), at(qb, W), at(qb, W), at(qb, W), at(qb, W),
                  at(qb, LANES), at(qb, LANES), at(qb, LANES)],
        out_specs=[at(qb, W), at(qb, W), at(qb, W)], out_shape=[wide, wide, wide],
        scratch_shapes=[pltpu.VMEM((t, W), F32), pltpu.VMEM((t, W), F32)],
        compiler_params=_params("parallel", "arbitrary"), name=name)(x[0], x[1], x[1], x[2], x[2], do, o, lse, *tabs)


def _fox_operands(q2, k2, kb2, lo, hh):
    lane = lax.broadcasted_iota(jnp.int32, (1, LANES), 1)
    if hh == 0:
        ones = ((lane >= HEAD_DIM) & (lane < HEAD_DIM + 3)).astype(BF16)
        return jnp.where(lo, q2, ones), jnp.where(lo, k2, kb2)
    ones = (lane < 3).astype(BF16)
    return jnp.where(lo, ones, q2), jnp.where(lo, kb2, k2)


def _causal_neg(t):
    ri = lax.broadcasted_iota(jnp.int32, (t, t), 0)
    ci = lax.broadcasted_iota(jnp.int32, (t, t), 1)
    return jnp.where(ci <= ri, 0.0, NEG_INF)


def _fox_fwd(name, qkv, kbias, t):
    S = qkv.shape[0]
    W = D_MODEL
    nq = S // t
    rep = t // LANES

    def body(q_ref, k_ref, v_ref, kb_ref, o_ref, lse_ref, m_scr, l_scr, acc_scr):
        qi, j = pl.program_id(0), pl.program_id(1)
        lo = _lo_lanes()

        @pl.when(j == 0)
        def _():
            m_scr[...] = jnp.full_like(m_scr, NEG_INF)
            l_scr[...] = jnp.zeros_like(l_scr)
            acc_scr[...] = jnp.zeros_like(acc_scr)

        def step(masked):
            neg = _causal_neg(t) if masked else None

            def pair(p, carry):
                cs = pl.ds(pl.multiple_of(p * LANES, LANES), LANES)
                q2, k2, v2, kb2 = q_ref[:, cs], k_ref[:, cs], v_ref[:, cs], kb_ref[:, cs]
                pvs, alphas = [], []
                for hh in range(2):
                    hm = lo if hh == 0 else jnp.logical_not(lo)
                    qh, kh = _fox_operands(q2, k2, kb2, lo, hh)
                    s = _dot(qh, kh, NT)
                    if masked:
                        s = s + neg
                    h = 2 * p + hh
                    m_prev = m_scr[h]
                    m_new = jnp.maximum(m_prev, jnp.max(s, axis=1, keepdims=True))
                    pe = jnp.exp(s - jnp.tile(m_new, (1, rep)))
                    alpha = jnp.exp(m_prev - m_new)
                    l_scr[h] = alpha * l_scr[h] + jnp.sum(pe, axis=1, keepdims=True)
                    m_scr[h] = m_new
                    pvs.append(_dot(pe.astype(BF16), jnp.where(hm, v2, 0), NN))
                    alphas.append(alpha)
                acc_scr[:, cs] = acc_scr[:, cs] * jnp.where(lo, alphas[0], alphas[1]) + pvs[0] + pvs[1]
                return carry

            lax.fori_loop(0, N_PAIRS, pair, 0, unroll=4)

        @pl.when(j < qi)
        def _():
            step(False)

        @pl.when(j == qi)
        def _():
            step(True)

        @pl.when(j == nq - 1)
        def _():
            for p in range(N_PAIRS):
                cols = slice(p * LANES, (p + 1) * LANES)
                l2 = jnp.where(lo, l_scr[2 * p], l_scr[2 * p + 1])
                m2 = jnp.where(lo, m_scr[2 * p], m_scr[2 * p + 1])
                o_ref[:, cols] = (acc_scr[:, cols] / l2).astype(BF16)
                lse_ref[:, cols] = m2 + jnp.log(l2)

    kv = lambda col: pl.BlockSpec((t, W), lambda qi, j: (jnp.minimum(j, qi), col))
    own = pl.BlockSpec((t, W), lambda qi, j: (qi, 0))
    return pl.pallas_call(
        body, grid=(nq, nq), in_specs=[own, kv(1), kv(2), kv(0)], out_specs=[own, own],
        out_shape=[jax.ShapeDtypeStruct((S, W), BF16), jax.ShapeDtypeStruct((S, W), F32)],
        scratch_shapes=[pltpu.VMEM((N_HEADS, t, LANES), F32), pltpu.VMEM((N_HEADS, t, LANES), F32), pltpu.VMEM((t, W), F32)],
        compiler_params=_params("parallel", "arbitrary"), name=name)(qkv, qkv, qkv, kbias)


def _fox_head_grads(qh, kh, v2, doh, neg, lse_h, d_h, rep):
    s = _dot(qh, kh, NT)
    if neg is not None:
        s = s + neg
    p = jnp.exp(s - jnp.tile(lse_h, (1, rep)))
    return p, p * (_dot(doh, v2, NT) - d_h)


def _fox_bwd(name, qkv, kbias, do, o, lse, t):
    S = qkv.shape[0]
    W = D_MODEL
    nq = S // t
    rep = t // LANES

    def body(q_ref, k_ref, v_ref, kb_ref, do_ref, o_ref, lse_ref, dq_ref, dk_ref, dv_ref, rs_ref, dc_ref, dq_scr, dk_scr, dv_scr):
        kb, j = pl.program_id(0), pl.program_id(1)
        lo = _lo_lanes()
        lane = lax.broadcasted_iota(jnp.int32, (1, LANES), 1)
        rows = pl.ds(pl.multiple_of(j * t, t), t)

        @pl.when((kb == 0) & (j == 0))
        def _():
            dq_scr[...] = jnp.zeros_like(dq_scr)
            rs_ref[...] = jnp.zeros_like(rs_ref)

        @pl.when(j == 0)
        def _():
            dk_scr[...] = jnp.zeros_like(dk_scr)
            dv_scr[...] = jnp.zeros_like(dv_scr)
            dc_ref[...] = jnp.zeros_like(dc_ref)

        def step(masked):
            neg = _causal_neg(t) if masked else None

            def pair(p, carry):
                cs = pl.ds(pl.multiple_of(p * LANES, LANES), LANES)
                q2, k2, v2, kb2, do2 = q_ref[:, cs], k_ref[:, cs], v_ref[:, cs], kb_ref[:, cs], do_ref[:, cs]
                dd = do2.astype(F32) * o_ref[:, cs].astype(F32)
                lse_h = _rep_rows(lse_ref[:, cs], lo)
                dq2 = jnp.zeros((t, LANES), F32)
                dv2 = jnp.zeros((t, LANES), F32)
                dk2 = jnp.zeros((t, LANES), F32)
                for hh in range(2):
                    hm = lo if hh == 0 else jnp.logical_not(lo)
                    qh, kh = _fox_operands(q2, k2, kb2, lo, hh)
                    doh = jnp.where(hm, do2, 0)
                    d_h = jnp.sum(jnp.where(hm, dd, 0.0), axis=1, keepdims=True)
                    pr, ds = _fox_head_grads(qh, kh, v2, doh, neg, lse_h[hh], d_h, rep)
                    rs_ref[rows, :] += jnp.where(lane == 2 * p + hh, jnp.sum(ds, axis=1, keepdims=True), 0.0)
                    dc_ref[p, hh:hh + 1, :] += jnp.sum(ds, axis=0, keepdims=True)
                    dsb = ds.astype(BF16)
                    dv2 = dv2 + _dot(pr.astype(BF16), doh, TN)
                    dk2 = dk2 + _dot(dsb, jnp.where(hm, q2, 0), TN)
                    dq2 = dq2 + _dot(dsb, jnp.where(hm, k2, 0), NN)
                dv_scr[:, cs] += dv2
                dk_scr[:, cs] += dk2
                dq_scr[rows, cs] += dq2
                return carry

            lax.fori_loop(0, N_PAIRS, pair, 0, unroll=4)
            if masked:
                dq_ref[...] = (dq_scr[rows, :] * SOFTMAX_SCALE).astype(BF16)

        @pl.when(j > kb)
        def _():
            step(False)

        @pl.when(j == kb)
        def _():
            step(True)

        @pl.when(j == nq - 1)
        def _():
            dv_ref[...] = dv_scr[...].astype(BF16)
            dk_ref[...] = dk_scr[...].astype(BF16)

    qrow = pl.BlockSpec((t, W), lambda kb, j: (jnp.maximum(j, kb), 0))
    krow = lambda col: pl.BlockSpec((t, W), lambda kb, j: (kb, col))
    own = pl.BlockSpec((t, W), lambda kb, j: (kb, 0))
    wide = jax.ShapeDtypeStruct((S, W), BF16)
    return pl.pallas_call(
        body, grid=(nq, nq), in_specs=[qrow, krow(1), krow(2), krow(0), qrow, qrow, qrow],
        out_specs=[own, own, own, pl.BlockSpec((S, LANES), lambda kb, j: (0, 0)), pl.BlockSpec((N_PAIRS, 2, t), lambda kb, j: (0, 0, kb))],
        out_shape=[wide, wide, wide, jax.ShapeDtypeStruct((S, LANES), F32), jax.ShapeDtypeStruct((N_PAIRS, 2, S), F32)],
        scratch_shapes=[pltpu.VMEM((S, W), F32), pltpu.VMEM((t, W), F32), pltpu.VMEM((t, W), F32)],
        compiler_params=pltpu.CompilerParams(dimension_semantics=("arbitrary", "arbitrary"), vmem_limit_bytes=FOX_BWD_VMEM),
        name=name)(qkv, qkv, qkv, kbias, do, o, lse)


def _view_spec(tm, R, index=lambda i: (i, 0)):
    return pl.BlockSpec((tm // R, R * D_MODEL), index)


def _matmul_nt_views(name, a, w, dils, tm=512):
    S, K = a.shape

    def body(a_ref, w_ref, *rest):
        res = _dot(a_ref[...].astype(BF16), w_ref[...], NT)
        _write_views([res[:, b * LANES:(b + 1) * LANES] for b in range(N_PAIRS)], rest[-1], rest[:-1], dils, tm)

    return pl.pallas_call(
        body, grid=(S // tm,), in_specs=[pl.BlockSpec((tm, K), lambda i: (i, 0)), pl.BlockSpec((D_MODEL, K), lambda i: (0, 0))],
        out_specs=[_view_spec(tm, R) for R in dils],
        out_shape=[jax.ShapeDtypeStruct((S // R, R * D_MODEL), BF16) for R in dils],
        scratch_shapes=[pltpu.VMEM((N_PAIRS, tm, LANES), F32)], compiler_params=_params("parallel"), name=name)(a, w)


def _write_views(chunks, scr, out_refs, dils, tm):
    if any(R > 1 for R in dils):
        _stage_chunks(scr, chunks)
    for ref, R in zip(out_refs, dils):
        for b, x in enumerate(chunks):
            if R == 1:
                ref[:, b * LANES:(b + 1) * LANES] = x.astype(ref.dtype)
                continue
            for r in range(R):
                col = r * D_MODEL + b * LANES
                ref[:, col:col + LANES] = _strided_rows(scr, b, r, tm // R, R).astype(ref.dtype)


def _combine(name, os_, lses, dils, tm=256):
    S = os_[0].shape[0] * dils[0]
    G = len(dils)

    def body(*refs):
        o_refs, l_refs = refs[:G], refs[G:2 * G]
        o_outs, l_outs = refs[2 * G:3 * G], refs[3 * G:4 * G]
        stage = refs[4 * G:]
        for g, R in enumerate(dils):
            if R == 1:
                continue
            for src, dst in ((o_refs[g], stage[2 * g]), (l_refs[g], stage[2 * g + 1])):
                _unstride(lambda r, b, src=src: src[:, r * D_MODEL + b * LANES:r * D_MODEL + (b + 1) * LANES], R, dst, tm)
        o_chunks, l_chunks = [], []
        for b in range(N_PAIRS):
            cols = slice(b * LANES, (b + 1) * LANES)
            os_b = [o_refs[g][:, cols] if R == 1 else stage[2 * g][b] for g, R in enumerate(dils)]
            ls = [l_refs[g][:, cols] if R == 1 else stage[2 * g + 1][b] for g, R in enumerate(dils)]
            m = functools.reduce(jnp.maximum, ls)
            ws = [jnp.exp(l - m) for l in ls]
            den = functools.reduce(jnp.add, ws)
            o_chunks.append(functools.reduce(jnp.add, [w * o for w, o in zip(ws, os_b)]) / den)
            l_chunks.append(m + jnp.log(den))
        _write_views(o_chunks, stage[0], o_outs, dils, tm)
        _write_views(l_chunks, stage[1], l_outs, dils, tm)

    specs = [_view_spec(tm, R) for R in dils]
    shapes = lambda dt: [jax.ShapeDtypeStruct((S // R, R * D_MODEL), dt) for R in dils]
    res = pl.pallas_call(
        body, grid=(S // tm,), in_specs=specs * 2, out_specs=specs * 2, out_shape=shapes(BF16) + shapes(F32),
        scratch_shapes=[pltpu.VMEM((N_PAIRS, tm, LANES), F32)] * (2 * G), compiler_params=_params("parallel"),
        name=name)(*os_, *lses)
    return res[:G], res[G:]


def _tri_matmul(tri, x):
    hi, mid, lo = _split3(x)
    return _dot(tri, hi, NN) + _dot(tri, mid, NN) + _dot(tri, lo, NN)


def _split3(x):
    hi = x.astype(BF16)
    r1 = x - hi.astype(F32)
    mid = r1.astype(BF16)
    return hi, mid, (r1 - mid.astype(F32)).astype(BF16)


def _gate_fwd(name, z, bf, tb=512):
    S = z.shape[0]

    def body(z_ref, b_ref, kb_ref, carry):
        @pl.when(pl.program_id(0) == 0)
        def _():
            carry[...] = jnp.zeros_like(carry)

        lf = jax.nn.log_sigmoid(z_ref[...] + b_ref[...])
        ri = lax.broadcasted_iota(jnp.int32, (tb, tb), 0)
        ci = lax.broadcasted_iota(jnp.int32, (tb, tb), 1)
        tri = (ci <= ri).astype(BF16)
        c = _tri_matmul(tri, lf) + carry[...]
        carry[...] = c[tb - 1:tb, :]
        head = lax.broadcasted_iota(jnp.int32, (LANES, D_MODEL), 0)
        col = lax.broadcasted_iota(jnp.int32, (LANES, D_MODEL), 1)
        base = (head >> 1) * LANES + jnp.where((head & 1) == 0, HEAD_DIM, 0)
        kb = jnp.zeros((tb, D_MODEL), F32)
        for i, piece in enumerate(_split3(-c)):
            place = ((col == base + i) & (head < N_HEADS)).astype(BF16)
            kb = kb + _dot(piece, place, NN)
        kb_ref[...] = kb.astype(BF16)

    row = pl.BlockSpec((tb, LANES), lambda i: (i, 0))
    return pl.pallas_call(
        body, grid=(S // tb,), in_specs=[row, pl.BlockSpec((1, LANES), lambda i: (0, 0))],
        out_specs=pl.BlockSpec((tb, D_MODEL), lambda i: (i, 0)), out_shape=jax.ShapeDtypeStruct((S, D_MODEL), BF16),
        scratch_shapes=[pltpu.VMEM((1, LANES), F32)], compiler_params=_params("arbitrary"), name=name)(z, bf)


def _gate_bwd(name, dc, z, bf, tb=512):
    S = z.shape[0]
    nb = S // tb

    def body(dc_ref, z_ref, b_ref, dz_ref, db_ref, carry):
        @pl.when(pl.program_id(0) == 0)
        def _():
            carry[...] = jnp.zeros_like(carry)
            db_ref[...] = jnp.zeros_like(db_ref)

        ri = lax.broadcasted_iota(jnp.int32, (tb, tb), 0)
        ci = lax.broadcasted_iota(jnp.int32, (tb, tb), 1)
        tri = (ci >= ri).astype(BF16)
        dlf = _tri_matmul(tri, dc_ref[...]) + carry[...]
        carry[...] = dlf[0:1, :]
        dz = dlf * jax.nn.sigmoid(-(z_ref[...] + b_ref[...]))
        dz_ref[...] = dz
        db_ref[...] += jnp.sum(dz, axis=0, keepdims=True)

    row = pl.BlockSpec((tb, LANES), lambda i: (nb - 1 - i, 0))
    vec = pl.BlockSpec((1, LANES), lambda i: (0, 0))
    return pl.pallas_call(
        body, grid=(nb,), in_specs=[row, row, vec], out_specs=[row, vec],
        out_shape=[jax.ShapeDtypeStruct((S, LANES), F32), jax.ShapeDtypeStruct((1, LANES), F32)],
        scratch_shapes=[pltpu.VMEM((1, LANES), F32)], compiler_params=_params("arbitrary"), name=name)(dc, z, bf)


def _ffn_gu(name, n, wgu, comm=None, tm=1024):
    S, D = n.shape
    nb = N_DEV // 2

    def body(n_ref, wg_ref, wu_ref, gu_ref, act_ref):
        x = n_ref[...]
        g = _dot(x, wg_ref[...], NN)
        u = _dot(x, wu_ref[...], NN)
        gu_ref[0] = g.astype(BF16)
        gu_ref[1] = u.astype(BF16)
        act_ref[...] = (g * jax.nn.sigmoid(g) * u).astype(BF16)

    return _call(
        name, body, (nb, S // tm),
        [pl.BlockSpec((tm, D), lambda j, i: (i, 0)), pl.BlockSpec((None, D, FF_BLK), lambda j, i: (j, 0, 0)),
         pl.BlockSpec((None, D, FF_BLK), lambda j, i: (j + nb, 0, 0))],
        [pl.BlockSpec((2, None, tm, FF_BLK), lambda j, i: (0, j, i, 0)), pl.BlockSpec((None, tm, FF_BLK), lambda j, i: (j, i, 0))],
        [jax.ShapeDtypeStruct((2, nb, S, FF_BLK), BF16), jax.ShapeDtypeStruct((nb, S, FF_BLK), BF16)], [],
        (n, wgu, wgu), ("parallel", "parallel"), comm)


def _ffn_down(name, act, wd, resid, comm=None, tm=1024):
    nb, S, _ = act.shape
    D = wd.shape[1]

    def epilogue(acc, ex, outs, j):
        outs[0][...] = acc + ex[0][...]

    o_spec = pl.BlockSpec((tm, D), lambda i, j, k: (i, 0))
    return _mm_call(name, (S // tm, 1, nb), act, pl.BlockSpec((None, tm, FF_BLK), lambda i, j, k: (k, i, 0)),
                    wd, pl.BlockSpec((FF_BLK, D), lambda i, j, k: (k, 0)), NN,
                    [jax.ShapeDtypeStruct((S, D), F32)], [o_spec], (tm, D), epilogue, (resid,), (o_spec,), comm=comm)


def _ffn_dact(name, dh, wd, gu, comm=None, tm=512):
    S, D = dh.shape
    nb = N_DEV // 2

    def epilogue(acc, ex, outs, j):
        g = ex[0][0].astype(F32)
        u = ex[0][1].astype(F32)
        sig = jax.nn.sigmoid(g)
        outs[0][0] = (acc * u * (sig * (1.0 + g * (1.0 - sig)))).astype(BF16)
        outs[0][1] = (acc * (g * sig)).astype(BF16)

    gu_spec = pl.BlockSpec((2, None, tm, FF_BLK), lambda j, i, k: (0, j, i, 0))
    return _mm_call(name, (nb, S // tm, 1), dh, pl.BlockSpec((tm, D), lambda j, i, k: (i, 0)),
                    wd, pl.BlockSpec((FF_BLK, D), lambda j, i, k: (j, 0)), NT,
                    [jax.ShapeDtypeStruct((2, nb, S, FF_BLK), BF16)], [gu_spec], (tm, FF_BLK), epilogue, (gu,), (gu_spec,),
                    col_axis=0, comm=comm)


def _ffn_dwgu(name, n, dgu, comm=None, tm=1024, tk=1024):
    S, D = n.shape
    dgu8 = dgu.reshape(N_DEV, S, FF_BLK)
    return _mm_call(name, (N_DEV, D // tm, S // tk), n, pl.BlockSpec((tk, tm), lambda d, i, k: (k, i)),
                    dgu8, pl.BlockSpec((None, tk, FF_BLK), lambda d, i, k: (d, k, 0)), TN,
                    [jax.ShapeDtypeStruct((N_DEV, D, FF_BLK), BF16)],
                    [pl.BlockSpec((None, tm, FF_BLK), lambda d, i, k: (d, i, 0))], (tm, FF_BLK), comm=comm)


def _ffn_dwd(name, act, dh, tk=1024):
    nb, S, _ = act.shape
    D = dh.shape[1]
    out = _mm_call(name, (nb, 1, S // tk), act, pl.BlockSpec((None, tk, FF_BLK), lambda b, j, k: (b, k, 0)),
                   dh, pl.BlockSpec((tk, D), lambda b, j, k: (k, 0)), TN,
                   [jax.ShapeDtypeStruct((nb, FF_BLK, D), BF16)],
                   [pl.BlockSpec((None, FF_BLK, D), lambda b, j, k: (b, 0, 0))], (FF_BLK, D))[0]
    return out.reshape(N_DEV, FF_BLK // 2, D)


def _ffn_dn(name, dgu, wgu, comm=None, tm=1024):
    S = dgu.shape[2]
    D = wgu.shape[1]
    dgu8 = dgu.reshape(N_DEV, S, FF_BLK)
    return _mm_call(name, (S // tm, 1, N_DEV), dgu8, pl.BlockSpec((None, tm, FF_BLK), lambda i, j, k: (k, i, 0)),
                    wgu, pl.BlockSpec((None, D, FF_BLK), lambda i, j, k: (k, 0, 0)), NT,
                    [jax.ShapeDtypeStruct((S, D), F32)], [pl.BlockSpec((tm, D), lambda i, j, k: (i, 0))], (tm, D), comm=comm)


def _adamw(name, parts, w, m, v, tr):
    rows, cols = w.shape
    n_parts = len(parts)
    c1 = 1.0 - ADAM_B1 ** ADAM_STEP
    c2 = 1.0 - ADAM_B2 ** ADAM_STEP

    def body(*refs):
        p_refs = refs[:n_parts]
        w_ref, m_ref, v_ref, g_ref, d_ref, nm_ref, nv_ref = refs[n_parts:]
        g = p_refs[0][...].astype(F32)
        for r in p_refs[1:]:
            g = g + r[...].astype(F32)
        mm = ADAM_B1 * m_ref[...] + (1.0 - ADAM_B1) * g
        vv = ADAM_B2 * v_ref[...] + (1.0 - ADAM_B2) * (g * g)
        g_ref[...] = g
        nm_ref[...] = mm
        nv_ref[...] = vv
        d_ref[...] = -ADAM_LR * ((mm / c1) / (jnp.sqrt(vv / c2) + ADAM_EPS) + ADAM_WD * w_ref[...])

    blk = pl.BlockSpec((tr, cols), lambda i: (i, 0))
    out = jax.ShapeDtypeStruct((rows, cols), F32)
    return pl.pallas_call(
        body, grid=(rows // tr,), in_specs=[blk] * (n_parts + 3), out_specs=[blk] * 4, out_shape=[out] * 4,
        compiler_params=_params("parallel"), name=name)(*parts, w, m, v)


def _position():
    return lax.axis_index("x"), lax.axis_index("y"), lax.axis_index("c")


def _other_chips():
    x, y, _ = _position()
    return [(1 - x, y), (x, 1 - y), (1 - x, 1 - y)]


def _remote(src, dst, send, recv, k, to):
    return pltpu.make_async_remote_copy(src_ref=src, dst_ref=dst, send_sem=send.at[k], recv_sem=recv.at[k],
                                        device_id=to, device_id_type=MESH)


def _ag_send(blocks, direct=False):
    n_peer = 7 if direct else 4

    def copies(ins, outs, send, recv, local, r0=0, l0=0):
        x, y, c = _position()
        me = 4 * x + 2 * y + c
        peers = [(x, y, 1 - c)] + [(px, py, c) for px, py in _other_chips()]
        if direct:
            peers += [(px, py, 1 - c) for px, py in _other_chips()]
        cps = []
        for t, (src, dst) in enumerate(zip(ins, outs)):
            cps.append(pltpu.make_async_copy(src, dst.at[me], local.at[l0 + t]))
            cps += [_remote(src, dst.at[me], send, recv, r0 + n_peer * t + k, to) for k, to in enumerate(peers)]
        return cps

    outs = tuple(jax.ShapeDtypeStruct((N_DEV,) + b.shape, b.dtype) for b in blocks)
    return _Comm(tuple(blocks), outs, {}, copies, n_peer * len(blocks), len(blocks))


def _ag_forward(bufs):
    def copies(ins, outs, send, recv, local, r0=0, l0=0):
        x, y, c = _position()
        cps = []
        for t, buf in enumerate(outs):
            for k, (px, py) in enumerate(_other_chips()):
                slot = buf.at[4 * px + 2 * py + c]
                cps.append(_remote(slot, slot, send, recv, r0 + 3 * t + k, (x, y, 1 - c)))
        return cps

    outs = tuple(jax.ShapeDtypeStruct(b.shape, b.dtype) for b in bufs)
    return _Comm(tuple(bufs), outs, {t: t for t in range(len(bufs))}, copies, 3 * len(bufs), 0)


def _rs_swap(shares):
    def copies(ins, outs, send, recv, local, r0=0, l0=0):
        x, y, c = _position()
        return [_remote(src.at[:, 1 - c], dst, send, recv, r0 + t, (x, y, 1 - c)) for t, (src, dst) in enumerate(zip(ins, outs))]

    ins = tuple(s.reshape((4, 2) + s.shape[1:]) for s in shares)
    outs = tuple(jax.ShapeDtypeStruct((4,) + s.shape[1:], s.dtype) for s in shares)
    return _Comm(ins, outs, {}, copies, len(shares), 0)


def _rs_exchange(sums):
    def copies(ins, outs, send, recv, local, r0=0, l0=0):
        _, _, c = _position()
        return [_remote(src.at[2 * px + py], dst.at[k], send, recv, r0 + 3 * t + k, (px, py, c))
                for t, (src, dst) in enumerate(zip(ins, outs)) for k, (px, py) in enumerate(_other_chips())]

    outs = tuple(jax.ShapeDtypeStruct((3,) + s.shape[1:], s.dtype) for s in sums)
    return _Comm(tuple(sums), outs, {}, copies, 3 * len(sums), 0)


def _comm_call(name, comm):
    return _call(name, lambda: None, (), [], [], [], [], (), (), comm)


def _pair_sum(name, share, got, core, tr):
    _, rows, cols = share.shape

    def body(c_ref, a_ref, b_ref, o_ref):
        o_ref[...] = (a_ref[...].astype(F32) + b_ref[...].astype(F32)).astype(o_ref.dtype)

    grid_spec = pltpu.PrefetchScalarGridSpec(
        num_scalar_prefetch=1, grid=(4, rows // tr),
        in_specs=[pl.BlockSpec((None, None, tr, cols), lambda q, i, c: (q, c[0], i, 0)),
                  pl.BlockSpec((None, tr, cols), lambda q, i, c: (q, i, 0))],
        out_specs=pl.BlockSpec((None, tr, cols), lambda q, i, c: (q, i, 0)))
    return pl.pallas_call(
        body, grid_spec=grid_spec, out_shape=jax.ShapeDtypeStruct((4, rows, cols), share.dtype),
        compiler_params=_params("parallel", "parallel"), name=name)(core, share.reshape(4, 2, rows, cols), got)


TENSORS = ("a_w_in", "a_w_out", "b_w_in", "b_w_out", "gu0", "gu1", "dn0", "dn1")
ROW_TILE = {"a_w_in": 256, "a_w_out": 128, "b_w_in": 256, "b_w_out": 128, "gu0": 256, "gu1": 256, "dn0": 176, "dn1": 176}
A_BLK = 9 * D_MODEL // N_DEV
B_BLK = 386
B_IN = 3 * D_MODEL + N_HEADS
B_IN_PAD = 3 * D_MODEL + LANES


def kernel(x, a_norm, a_w_in, a_w_out, b_norm, b_w_in, b_f, b_w_out, ffn_norm, ffn_w_gu, ffn_w_down, final_norm, loss_target, m_a_norm, m_a_w_in, m_a_w_out, m_b_norm, m_b_w_in, m_b_f, m_b_w_out, m_ffn_norm, m_ffn_w_gu, m_ffn_w_down, m_final_norm, v_a_norm, v_a_w_in, v_a_w_out, v_b_norm, v_b_w_in, v_b_f, v_b_w_out, v_ffn_norm, v_ffn_w_gu, v_ffn_w_down, v_final_norm):
    S = x.shape[1]
    xi, yi, ci = _position()
    dev = 4 * xi + 2 * yi + ci
    core = ci.reshape(1).astype(jnp.int32)
    h0, target = x.reshape(S, D_MODEL), loss_target.reshape(S, D_MODEL)

    def shards(a_in, a_out, b_in, b_out, gu, dn):
        return {"a_w_in": a_in[0], "a_w_out": a_out[0], "b_w_in": b_in[0], "b_w_out": b_out[0],
                "gu0": gu[0], "gu1": gu[1], "dn0": dn[0], "dn1": dn[1]}

    w_sh = shards(a_w_in, a_w_out, b_w_in, b_w_out, ffn_w_gu, ffn_w_down)
    m_sh = shards(m_a_w_in, m_a_w_out, m_b_w_in, m_b_w_out, m_ffn_w_gu, m_ffn_w_down)
    v_sh = shards(v_a_w_in, v_a_w_out, v_b_w_in, v_b_w_out, v_ffn_w_gu, v_ffn_w_down)
    wb = {n: w_sh[n].astype(BF16) for n in TENSORS}
    bf_pad = jnp.pad(b_f, ((0, 0), (0, LANES - N_HEADS)))
    tabs = _rope_tables(S)

    g_ain, g_aout = _comm_call("gather_a", _ag_send([wb["a_w_in"], wb["a_w_out"]]))
    g_ain, g_aout = _comm_call("forward_a", _ag_forward([g_ain, g_aout]))
    dils = [dil for _, dil in DILATED_PATTERNS]
    n0_views = _rms_fwd("rms_a", h0, a_norm[0], dils)
    n0 = n0_views[0]
    later =[wb["b_w_in"], wb["b_w_out"], wb["gu0"], wb["dn0"], jnp.pad(b_norm, ((0, 7), (0, 0)))]
    w_a_in = g_ain.transpose(1, 0, 2).reshape(D_MODEL, 9 * D_MODEL)
    qkv_a, later = _a_proj("proj_a", n0, w_a_in, tabs, _ag_send(later))
    cols = [lambda r: r] * 3
    groups = [(g, dil, S // dil, qkv_a[g]) for g, (window, dil) in enumerate(DILATED_PATTERNS)]
    fwd =[_dil_fwd("dil_fwd%d" % g, view, *cols, dil, L) for g, dil, L, view in groups]
    o_views, lse_views = _combine("dil_combine", [f[0] for f in fwd], [f[1] for f in fwd], dils)
    o_a = o_views[0]
    w_a_out = g_aout.reshape(D_MODEL, D_MODEL)
    h1, (g_bin, g_bout, g_gu0, g_dn0, g_bnorm) = _matmul("out_a", o_a, w_a_out, "nn", F32, TM, 1024, 1024, resid=h0,
                                                         comm=_ag_forward(later))

    n1 = _rms_fwd("rms_f0", h1, ffn_norm[0])
    gu0, act0, g_gu1 = _ffn_gu("gu_f0", n1, g_gu0, _ag_send([wb["gu1"]]))
    w_dn0 = g_dn0.reshape(D_FF, D_MODEL)
    h2, g_dn1 = _ffn_down("down_f0", act0, w_dn0, h1, _ag_send([wb["dn1"]]))

    b_norm_full = g_bnorm[:, 0].reshape(D_MODEL)
    w_b_in = g_bin.transpose(1, 0, 2).reshape(D_MODEL, B_IN)
    w_b_gate = jnp.pad(w_b_in[:, 3 * D_MODEL:], ((0, 0), (0, LANES - N_HEADS)))
    w_b_cat = jnp.concatenate([w_b_in[:, :3 * D_MODEL], w_b_gate], axis=1)
    w_b_out = g_bout.reshape(D_MODEL, D_MODEL)
    n2 = _rms_fwd("rms_b", h2, b_norm_full)
    qkv, (g_gu1, g_dn1) = _matmul("proj_b", n2, w_b_in[:, :3 * D_MODEL], "nn", BF16, TM, 1024, 1024, col0_scale=SOFTMAX_SCALE,
                                  comm=_ag_forward([g_gu1, g_dn1]))
    z = _matmul("gate_b", n2, w_b_gate, "nn", F32, TM, LANES, 1024)
    kbias = _gate_fwd("gate_cumsum", z, bf_pad)
    tf = min(S, 512)
    o_b, lse_b = _fox_fwd("fox_fwd", qkv, kbias, tf)
    h3 = _matmul("out_b", o_b, w_b_out, "nn", F32, TM, 1024, 1024, resid=h2)

    w_dn1 = g_dn1.reshape(D_FF, D_MODEL)
    n3 = _rms_fwd("rms_f1", h3, ffn_norm[1])
    gu1, act1 = _ffn_gu("gu_f1", n3, g_gu1)
    h4 = _ffn_down("down_f1", act1, w_dn1, h3)[0]

    dh4, d_final, loss, dh4_16 = _loss_head("loss_head", h4, final_norm, target)

    share, got, sums, others = {}, {}, {}, {}

    def pair_sums(*names):
        for n in names:
            sums[n] = _pair_sum("pair_" + n, share[n], got[n], core, ROW_TILE[n])

    dgu1 = _ffn_dact("dact_f1", dh4_16, w_dn1, gu1)[0]
    share["dn1"] = _ffn_dwd("dwd_f1", act1, dh4_16)
    share["gu1"] = _ffn_dwgu("dwgu_f1", n3, dgu1)[0]
    dn3, got["gu1"], got["dn1"] = _ffn_dn("dn_f1", dgu1, g_gu1, _rs_swap([share["gu1"], share["dn1"]]))
    dh3, d_ffn1, dh3_16 = _rms_bwd("rmsb_f1", dn3, h3, ffn_norm[1], dh4)
    pair_sums("gu1", "dn1")

    do_b = _matmul("dout_b", dh3_16, w_b_out, "nt", BF16, TM, 1024, 1024)
    share["b_w_out"] = _matmul("dwout_b", o_b, dh3_16, "tn", BF16, TM, 1024, 1024).reshape(N_DEV, 128, D_MODEL)
    dq_b, dk_b, dv_b, ds_rowsum, ds_colsum = _fox_bwd("fox_bwd", qkv, kbias, do_b, o_b, lse_b, tf)
    dc = ds_rowsum[:, :N_HEADS] - ds_colsum.reshape(N_HEADS, S).T
    dz, d_bf = _gate_bwd("gate_bwd", jnp.pad(dc, ((0, 0), (0, LANES - N_HEADS))), z, bf_pad)
    dproj_b = jnp.concatenate([dq_b, dk_b, dv_b, dz.astype(BF16)], axis=1)
    dw_b_in, (others["gu1"],) = _matmul("dwin_b", n2, dproj_b, "tn", BF16, TM, B_IN_PAD // 5, 1024, comm=_rs_exchange([sums["gu1"]]))
    dn2, (others["dn1"],) = _matmul("dn_b", dproj_b, w_b_cat, "nt", F32, TM, 1024, B_IN_PAD // 5, comm=_rs_exchange([sums["dn1"]]))
    dh2, d_bnorm, dh2_16 = _rms_bwd("rmsb_b", dn2, h2, b_norm_full, dh3)
    share["b_w_in"] = dw_b_in[:, :B_IN].reshape(D_MODEL, N_DEV, B_BLK).transpose(1, 0, 2)

    dgu0, got["b_w_in"], got["b_w_out"] = _ffn_dact("dact_f0", dh2_16, w_dn0, gu0, _rs_swap([share["b_w_in"], share["b_w_out"]]))
    share["dn0"] = _ffn_dwd("dwd_f0", act0, dh2_16)
    pair_sums("b_w_in", "b_w_out")
    share["gu0"], others["b_w_in"], others["b_w_out"] = _ffn_dwgu(
        "dwgu_f0", n1, dgu0, _rs_exchange([sums["b_w_in"], sums["b_w_out"]]))
    dn1, got["gu0"], got["dn0"] = _ffn_dn("dn_f0", dgu0, g_gu0, _rs_swap([share["gu0"], share["dn0"]]))
    dh1, d_ffn0, dh1_16 = _rms_bwd("rmsb_f0", dn1, h1, ffn_norm[0], dh2)
    pair_sums("gu0", "dn0")

    do_views = _matmul_nt_views("dout_a", dh1_16, w_a_out, dils)
    share["a_w_out"] = _matmul("dwout_a", o_a, dh1_16, "tn", BF16, TM, 1024, 1024).reshape(N_DEV, 128, D_MODEL)
    pieces = []
    for g, dil, L, view in groups:
        rot = tuple(tb.reshape(L, dil * LANES) for tb in tabs)
        grads = _dil_bwd("dil_bwd%d" % g, view, do_views[g], o_views[g], lse_views[g], rot, dil, L)
        pieces += list(grads)
    dw_a_in, others["gu0"], others["dn0"] = _a_dw("dwin_a", n0_views, pieces, dils, _rs_exchange([sums["gu0"], sums["dn0"]]))
    share["a_w_in"] = dw_a_in.reshape(D_MODEL, N_DEV, A_BLK).transpose(1, 0, 2)
    got["a_w_in"], got["a_w_out"] = _comm_call("swap_a", _rs_swap([share["a_w_in"], share["a_w_out"]]))
    pair_sums("a_w_in", "a_w_out")
    dn0, others["a_w_in"], others["a_w_out"] = _a_dn("dn_a", pieces, dils, w_a_in, _rs_exchange([sums["a_w_in"], sums["a_w_out"]]))
    dx, d_anorm = _rms_bwd("rmsb_a", dn0, h0, a_norm[0], dh1, copy16=False)

    misc = jnp.concatenate([d_bf[:, :N_HEADS], loss[:, :1], jnp.zeros((1, D_MODEL - N_HEADS - 1), F32)], axis=1)
    small = jnp.concatenate([d_anorm, d_ffn0, d_ffn1, d_final, d_bnorm, misc, jnp.zeros((2, D_MODEL), F32)], axis=0)
    small_all, = _comm_call("gather_small", _ag_send([small], direct=True))

    outs = {}
    for n in TENSORS:
        mine = lax.dynamic_index_in_dim(sums[n], 2 * xi + yi, axis=0, keepdims=False)
        outs[n] = _adamw("adamw_" + n, [mine] + [others[n][k] for k in range(3)], w_sh[n], m_sh[n], v_sh[n], ROW_TILE[n])

    pad_vec = lambda a: jnp.pad(a, ((0, 0), (0, D_MODEL - a.shape[1])))

    def small_pack(an, fn, fin, bf):
        return jnp.concatenate([an, fn, fin.reshape(1, D_MODEL), jnp.zeros((1, D_MODEL), F32), pad_vec(bf),
                                jnp.zeros((2, D_MODEL), F32)], axis=0)

    sg, sd, sm, sv = _adamw("adamw_small", [small_all[d] for d in range(N_DEV)], small_pack(a_norm, ffn_norm, final_norm, b_f),
                            small_pack(m_a_norm, m_ffn_norm, m_final_norm, m_b_f),
                            small_pack(v_a_norm, v_ffn_norm, v_final_norm, v_b_f), 8)
    g_bn = lax.dynamic_slice(sg[4:5], (0, dev * LANES), (1, LANES))
    bn = _adamw("adamw_b_norm", [g_bn], b_norm, m_b_norm, v_b_norm, 1)

    def tree(i):
        full = lambda name, ref: outs[name][i].reshape(ref.shape)
        sml = (sg, sd, sm, sv)[i]
        return dict(
            a_norm=sml[0:1], a_w_in=full("a_w_in", a_w_in), a_w_out=full("a_w_out", a_w_out), b_norm=bn[i],
            b_w_in=full("b_w_in", b_w_in), b_f=sml[5:6, :N_HEADS], b_w_out=full("b_w_out", b_w_out), ffn_norm=sml[1:3],
            ffn_w_gu=jnp.stack([outs["gu0"][i], outs["gu1"][i]]).reshape(ffn_w_gu.shape),
            ffn_w_down=jnp.stack([outs["dn0"][i], outs["dn1"][i]]).reshape(ffn_w_down.shape), final_norm=sml[3])

    order = ("a_norm", "a_w_in", "a_w_out", "b_norm", "b_w_in", "b_f", "b_w_out", "ffn_norm", "ffn_w_gu", "ffn_w_down", "final_norm")
    result = [sg[5, N_HEADS], dx.reshape(x.shape)]
    for i in range(4):
        t = tree(i)
        result += [t[n] for n in order]
    return tuple(result)
```

```python
import functools
from typing import Callable, NamedTuple

import jax
import jax.numpy as jnp
from jax import lax
from jax.experimental import pallas as pl
from jax.experimental.pallas import tpu as pltpu

F32 = jnp.float32
BF16 = jnp.bfloat16

D_MODEL = 1024
N_HEADS = 16
HEAD_DIM = 64
N_PAIRS = N_HEADS // 2
LANES = 128
DILATED_PATTERNS = ((128, 1), (512, 4), (2048, 16))
BAND_STEPS = 128
ROT_DIM = HEAD_DIM // 4
ROPE_THETA = 500000.0
D_FF = 2816
RMS_EPS = 1e-6
NEG_INF = -1e30
SOFTMAX_SCALE = HEAD_DIM ** -0.5
N_DEV = 8
FF_BLK = 2 * D_FF // N_DEV
ADAM_LR, ADAM_B1, ADAM_B2, ADAM_EPS, ADAM_WD, ADAM_STEP = 0.001, 0.9, 0.999, 1e-08, 0.01, 10
VMEM_LIMIT = 52 * 1024 * 1024
FOX_BWD_VMEM = 60 * 1024 * 1024
TM = 1024
MESH = pl.DeviceIdType.MESH

NN = (((1,), (0,)), ((), ()))
NT = (((1,), (1,)), ((), ()))
TN = (((0,), (0,)), ((), ()))


def _params(*sem):
    return pltpu.CompilerParams(dimension_semantics=sem, vmem_limit_bytes=VMEM_LIMIT)


def _dot(a, b, dims):
    return lax.dot_general(a, b, dims, preferred_element_type=F32)


class _Comm(NamedTuple):
    ins: tuple
    outs: tuple
    aliases: dict
    copies: Callable
    n_remote: int
    n_local: int


def _call(name, body, grid, in_specs, out_specs, out_shape, scratch, args, sem, comm=None):
    if comm is None:
        return pl.pallas_call(body, grid=grid, in_specs=in_specs, out_specs=out_specs, out_shape=out_shape,
                              scratch_shapes=scratch, compiler_params=_params(*sem), name=name)(*args)
    n_in, n_out = len(in_specs), len(out_specs)
    n_ci, n_co = len(comm.ins), len(comm.outs)
    o0 = n_in + n_ci

    def hosted(*refs):
        c_ins, c_outs = refs[n_in:o0], refs[o0 + n_out:o0 + n_out + n_co]
        sems = refs[-3:]

        def start():
            for cp in comm.copies(c_ins, c_outs, *sems):
                cp.start()

        def wait():
            for cp in comm.copies(c_ins, c_outs, *sems):
                cp.wait()

        if not grid:
            start()
            body()
            wait()
            return
        ids = [pl.program_id(ax) for ax in range(len(grid))]
        pl.when(functools.reduce(jnp.logical_and, [i == 0 for i in ids]))(start)
        body(*refs[:n_in], *refs[o0:o0 + n_out], *refs[o0 + n_out + n_co:-3])
        pl.when(functools.reduce(jnp.logical_and, [i == g - 1 for i, g in zip(ids, grid)]))(wait)

    hbm = pl.BlockSpec(memory_space=pltpu.HBM)
    dma = pltpu.SemaphoreType.DMA
    return pl.pallas_call(
        hosted, grid=grid, in_specs=[*in_specs, *[hbm] * n_ci], out_specs=[*out_specs, *[hbm] * n_co],
        out_shape=[*out_shape, *comm.outs], input_output_aliases={n_in + i: n_out + o for i, o in comm.aliases.items()},
        scratch_shapes=[*scratch, dma((comm.n_remote,)), dma((comm.n_remote,)), dma((max(comm.n_local, 1),))],
        compiler_params=_params(*["arbitrary"] * len(grid)), name=name)(*args, *comm.ins)


def _mm_call(name, grid, a, a_spec, b, b_spec, dims, out_shapes, out_specs, acc_shape, epilogue=None,
             extras=(), extra_specs=(), col_axis=1, comm=None):
    nk = grid[2]
    n_extra = len(extras)
    n_out = len(out_shapes)

    def finish(res, ex, outs, j):
        if epilogue is None:
            outs[0][...] = res.astype(outs[0].dtype)
        else:
            epilogue(res, ex, outs, j)

    def body(*refs):
        a_ref, b_ref = refs[0], refs[1]
        ex = refs[2:2 + n_extra]
        outs = refs[2 + n_extra:2 + n_extra + n_out]
        j, k = pl.program_id(col_axis), pl.program_id(2)
        part = _dot(a_ref[...].astype(BF16), b_ref[...].astype(BF16), dims)
        if nk == 1:
            finish(part, ex, outs, j)
            return
        acc = refs[-1]

        @pl.when(k == 0)
        def _():
            acc[...] = part

        @pl.when((k > 0) & (k < nk - 1))
        def _():
            acc[...] += part

        @pl.when(k == nk - 1)
        def _():
            finish(acc[...] + part, ex, outs, j)

    return _call(name, body, grid, [a_spec, b_spec, *extra_specs], out_specs, out_shapes,
                 [] if nk == 1 else [pltpu.VMEM(acc_shape, F32)], (a, b, *extras), ("parallel", "parallel", "arbitrary"), comm)


def _matmul(name, a, b, mode, out_dtype, tm, tn, tk, resid=None, col0_scale=None, comm=None):
    if mode == "nn":
        (M, K), N = a.shape, b.shape[1]
        a_spec = pl.BlockSpec((tm, tk), lambda j, i, k: (i, k))
        b_spec = pl.BlockSpec((tk, tn), lambda j, i, k: (k, j))
        dims = NN
    elif mode == "nt":
        (M, K), N = a.shape, b.shape[0]
        a_spec = pl.BlockSpec((tm, tk), lambda j, i, k: (i, k))
        b_spec = pl.BlockSpec((tn, tk), lambda j, i, k: (j, k))
        dims = NT
    else:
        (K, M), N = a.shape, b.shape[1]
        a_spec = pl.BlockSpec((tk, tm), lambda j, i, k: (k, i))
        b_spec = pl.BlockSpec((tk, tn), lambda j, i, k: (k, j))
        dims = TN
    assert M % tm == 0 and N % tn == 0 and K % tk == 0, (name, M, N, K, tm, tn, tk)
    o_spec = pl.BlockSpec((tm, tn), lambda j, i, k: (i, j))
    extras, extra_specs, epilogue = (), (), None
    if resid is not None:
        extras, extra_specs = (resid,), (o_spec,)

        def epilogue(acc, ex, outs, j):
            outs[0][...] = (acc + ex[0][...]).astype(outs[0].dtype)

    elif col0_scale is not None:

        def epilogue(acc, ex, outs, j):
            outs[0][...] = (acc * jnp.where(j == 0, col0_scale, 1.0)).astype(outs[0].dtype)

    res = _mm_call(name, (N // tn, M // tm, K // tk), a, a_spec, b, b_spec, dims, [jax.ShapeDtypeStruct((M, N), out_dtype)],
                   [o_spec], (tm, tn), epilogue, extras, extra_specs, col_axis=0, comm=comm)
    return res[0] if comm is None else (res[0], res[1:])


def _rms_fwd(name, h, gain, dils=(1,), comm=None, tm=512):
    S, D = h.shape

    def body(h_ref, g_ref, *rest):
        x = h_ref[...]
        rstd = lax.rsqrt(jnp.mean(x * x, axis=-1, keepdims=True) + RMS_EPS)
        y = x * rstd * g_ref[...]
        _write_views([y[:, b * LANES:(b + 1) * LANES] for b in range(N_PAIRS)], rest[-1], rest[:-1], dils, tm)

    res = _call(name, body, (S // tm,), [pl.BlockSpec((tm, D), lambda i: (i, 0)), pl.BlockSpec((1, D), lambda i: (0, 0))],
                [_view_spec(tm, R) for R in dils], [jax.ShapeDtypeStruct((S // R, R * D), BF16) for R in dils],
                [pltpu.VMEM((N_PAIRS, tm, LANES), F32)], (h, gain.reshape(1, D)), ("parallel",), comm)
    views = res[0] if len(dils) == 1 else res[:len(dils)]
    return views if comm is None else (views, res[len(dils):])


def _rms_bwd(name, dn, h, gain, dres, copy16=True, tm=512):
    S, D = h.shape

    def body(dn_ref, h_ref, g_ref, r_ref, dh_ref, dg_ref, *dh16_ref):
        x = h_ref[...]
        rstd = lax.rsqrt(jnp.mean(x * x, axis=-1, keepdims=True) + RMS_EPS)
        xhat = x * rstd
        d = dn_ref[...]
        dxhat = d * g_ref[...]
        dh = rstd * (dxhat - xhat * jnp.mean(dxhat * xhat, axis=-1, keepdims=True)) + r_ref[...]
        dh_ref[...] = dh
        if copy16:
            dh16_ref[0][...] = dh.astype(BF16)

        @pl.when(pl.program_id(0) == 0)
        def _():
            dg_ref[...] = jnp.zeros_like(dg_ref)

        dg_ref[...] += jnp.sum(d * xhat, axis=0, keepdims=True)

    row = pl.BlockSpec((tm, D), lambda i: (i, 0))
    vec = pl.BlockSpec((1, D), lambda i: (0, 0))
    return pl.pallas_call(
        body, grid=(S // tm,), in_specs=[row, row, vec, row], out_specs=[row, vec] + [row] * copy16,
        out_shape=[jax.ShapeDtypeStruct((S, D), F32), jax.ShapeDtypeStruct((1, D), F32)] + [jax.ShapeDtypeStruct((S, D), BF16)] * copy16,
        compiler_params=_params("arbitrary"), name=name)(dn, h, gain.reshape(1, D), dres)


def _loss_head(name, h, gain, target, tm=512):
    S, D = h.shape

    def body(h_ref, g_ref, t_ref, dh_ref, dg_ref, loss_ref, dh16_ref):
        x = h_ref[...]
        rstd = lax.rsqrt(jnp.mean(x * x, axis=-1, keepdims=True) + RMS_EPS)
        xhat = x * rstd
        err = xhat * g_ref[...] - t_ref[...]
        dy = err * (1.0 / D)
        dxhat = dy * g_ref[...]
        dh = rstd * (dxhat - xhat * jnp.mean(dxhat * xhat, axis=-1, keepdims=True))
        dh_ref[...] = dh
        dh16_ref[...] = dh.astype(BF16)

        @pl.when(pl.program_id(0) == 0)
        def _():
            dg_ref[...] = jnp.zeros_like(dg_ref)
            loss_ref[...] = jnp.zeros_like(loss_ref)

        dg_ref[...] += jnp.sum(dy * xhat, axis=0, keepdims=True)
        part = 0.5 * jnp.sum(jnp.mean(err * err, axis=-1, keepdims=True), axis=0, keepdims=True)
        loss_ref[...] += jnp.broadcast_to(part, loss_ref.shape)

    row = pl.BlockSpec((tm, D), lambda i: (i, 0))
    vec = pl.BlockSpec((1, D), lambda i: (0, 0))
    return pl.pallas_call(
        body, grid=(S // tm,), in_specs=[row, vec, row], out_specs=[row, vec, pl.BlockSpec((1, LANES), lambda i: (0, 0)), row],
        out_shape=[jax.ShapeDtypeStruct((S, D), F32), jax.ShapeDtypeStruct((1, D), F32),
                   jax.ShapeDtypeStruct((1, LANES), F32), jax.ShapeDtypeStruct((S, D), BF16)],
        compiler_params=_params("arbitrary"), name=name)(h, gain.reshape(1, D), target)


def _rope_tables(S):
    half = ROT_DIM // 2
    inv_freq = ROPE_THETA ** (-jnp.arange(half, dtype=F32) * 2.0 / ROT_DIM)
    ang = jnp.arange(S, dtype=F32)[:, None] * inv_freq[None, :]
    cos, sin = jnp.cos(ang), jnp.sin(ang)
    one = jnp.ones((S, HEAD_DIM - ROT_DIM), F32)
    zero = jnp.zeros((S, HEAD_DIM - ROT_DIM), F32)
    zh = jnp.zeros((S, half), F32)
    c = jnp.concatenate([cos, cos, one], axis=1)
    sa = jnp.concatenate([-sin, zh, zero], axis=1)
    sb = jnp.concatenate([zh, sin, zero], axis=1)
    return tuple(jnp.concatenate([t, t], axis=1) for t in (c, sa, sb))


def _rotate(x, c, sa, sb, sign):
    return x * c + sign * (pltpu.roll(x, LANES - ROT_DIM // 2, 1) * sa + pltpu.roll(x, ROT_DIM // 2, 1) * sb)


def _stage_chunks(scr, chunks):
    for c, x in enumerate(chunks):
        scr[c] = x


def _strided_rows(scr, c, r, n, R):
    return scr.at[c][pl.ds(r, n, stride=R), :]


def _a_proj(name, n, w, g, R, tabs, comm, tm=1024):
    S, D = n.shape
    n_i = S // tm
    n_out = 3

    def body(n_ref, w_ref, c_ref, sa_ref, sb_ref, *rest):
        outs, scr = rest[:n_out], rest[n_out]
        j = pl.program_id(0)
        acc = _dot(n_ref[...], w_ref[...], NN)
        c, sa, sb = c_ref[...], sa_ref[...], sb_ref[...]
        for J in range(n_out):
            kind = J

            @pl.when(j == J)
            def _(J=J, kind=kind):
                chunks = [acc[:, b * LANES:(b + 1) * LANES] for b in range(N_PAIRS)]
                if kind < 2:
                    chunks = [_rotate(x, c, sa, sb, 1.0) * (SOFTMAX_SCALE if kind == 0 else 1.0) for x in chunks]
                if R == 1:
                    for b, x in enumerate(chunks):
                        outs[J][:, b * LANES:(b + 1) * LANES] = x.astype(BF16)
                    return
                _stage_chunks(scr, chunks)
                for r in range(R):
                    for b in range(N_PAIRS):
                        col = r * D_MODEL + b * LANES
                        outs[J][:, col:col + LANES] = _strided_rows(scr, b, r, tm // R, R).astype(BF16)

    def out_spec(J):
        return pl.BlockSpec((tm // R, R * D_MODEL), lambda j, i: (jnp.where(j == J, i, jnp.where(j < J, 0, n_i - 1)), 0))

    tab = pl.BlockSpec((tm, LANES), lambda j, i: (i, 0))
    res = _call(name, body, (n_out, n_i),
                [pl.BlockSpec((tm, D), lambda j, i: (i, 0)), pl.BlockSpec((D, D_MODEL), lambda j, i: (0, 3 * g + j)), tab, tab, tab],
                [out_spec(J) for J in range(n_out)], [jax.ShapeDtypeStruct((S // R, R * D_MODEL), BF16)] * n_out,
                [pltpu.VMEM((N_PAIRS, tm, LANES), F32)], (n, w, *tabs), ("arbitrary", "arbitrary"), comm)
    return res[:n_out], res[n_out:]


def _unstride(src_chunk, R, tok, rows):
    for r in range(R):
        for b in range(N_PAIRS):
            tok.at[b][pl.ds(r, rows // R, stride=R), :] = src_chunk(r, b).astype(F32)


def _by_residue(ref, R):
    return ref[...] if R == 1 else jnp.concatenate([ref[:, r * D_MODEL:(r + 1) * D_MODEL] for r in range(R)], axis=0)


def _a_dw(name, n_views, pieces, dils, comm, tk=512):
    D = D_MODEL
    S = n_views[0].shape[0] * dils[0]
    n_k = S // tk
    n_p, n_g = len(pieces), len(dils)

    def body(*refs):
        n_refs, p_refs, o_ref, acc = refs[:n_g], refs[n_g:n_g + n_p], refs[n_g + n_p], refs[n_g + n_p + 1]
        j, k = pl.program_id(0), pl.program_id(1)
        for J in range(n_p):

            @pl.when(j == J)
            def _(J=J):
                R = dils[J // 3]
                part = _dot(_by_residue(n_refs[J // 3], R), _by_residue(p_refs[J], R), TN)

                @pl.when(k == 0)
                def _():
                    acc[...] = part

                @pl.when(k > 0)
                def _():
                    acc[...] += part

        @pl.when(k == n_k - 1)
        def _():
            o_ref[...] = acc[...].astype(BF16)

    def piece_spec(J):
        R = dils[J // 3]
        return pl.BlockSpec((tk // R, R * D_MODEL), lambda j, k: (jnp.where(j == J, k, jnp.where(j < J, 0, n_k - 1)), 0))

    n_specs = [pl.BlockSpec((tk // R, R * D_MODEL), lambda j, k: (k, 0)) for R in dils]
    return _call(name, body, (n_p, n_k), n_specs + [piece_spec(J) for J in range(n_p)],
                 [pl.BlockSpec((D, D_MODEL), lambda j, k: (0, j))], [jax.ShapeDtypeStruct((D, n_p * D_MODEL), BF16)],
                 [pltpu.VMEM((D, D_MODEL), F32)], (*n_views, *pieces), ("arbitrary", "arbitrary"), comm)


def _a_dn(name, pieces, dils, w, comm, tm=512):
    D = w.shape[0]
    S = pieces[0].shape[0] * dils[0]
    n_p = len(pieces)

    def body(*refs):
        p_refs, w_ref, o_ref, acc, part_acc, tok = refs[:n_p], refs[n_p], refs[n_p + 1], refs[n_p + 2], refs[n_p + 3], refs[n_p + 4]
        j = pl.program_id(1)
        for J in range(n_p):

            @pl.when(j == J)
            def _(J=J):
                R, t = dils[J // 3], J % 3
                part = _dot(_by_residue(p_refs[J], R), w_ref[...], NT)
                if R == 1:
                    if J == 0:
                        acc[...] = part
                    else:
                        acc[...] += part
                    return
                if t == 0:
                    part_acc[...] = part
                    return
                if t == 1:
                    part_acc[...] += part
                    return
                n = tm // R
                _unstride(lambda r, b: part_acc[r * n:(r + 1) * n, b * LANES:(b + 1) * LANES]
                          + part[r * n:(r + 1) * n, b * LANES:(b + 1) * LANES], R, tok, tm)
                total = acc[...] + jnp.concatenate([tok[b] for b in range(N_PAIRS)], axis=1)
                if J == n_p - 1:
                    o_ref[...] = total
                else:
                    acc[...] = total

    specs = [pl.BlockSpec((tm // dils[J // 3], dils[J // 3] * D_MODEL), lambda i, j: (i, 0)) for J in range(n_p)]
    return _call(name, body, (S // tm, n_p), specs + [pl.BlockSpec((D, D_MODEL), lambda i, j: (0, j))],
                 [pl.BlockSpec((tm, D), lambda i, j: (i, 0))], [jax.ShapeDtypeStruct((S, D), F32)],
                 [pltpu.VMEM((tm, D), F32), pltpu.VMEM((tm, D), F32), pltpu.VMEM((N_PAIRS, tm, LANES), F32)], (*pieces, w),
                 ("arbitrary", "arbitrary"), comm)


def _lo_lanes():
    return lax.broadcasted_iota(jnp.int32, (1, LANES), 1) < HEAD_DIM


def _rep_rows(x2, lo):
    sw = pltpu.roll(x2, HEAD_DIM, 1)
    return jnp.where(lo, x2, sw), jnp.where(lo, sw, x2)


def _pair_cols(h):
    return slice((h // 2) * LANES, (h // 2 + 1) * LANES)


def _head_lanes(lo, h):
    return lo if h % 2 == 0 else jnp.logical_not(lo)


def _band_masks(t, first):
    ri = lax.broadcasted_iota(jnp.int32, (t, t), 0)
    ci = lax.broadcasted_iota(jnp.int32, (t, t), 1)
    neg_prev = jnp.where((ci >= ri) & jnp.logical_not(first), 0.0, NEG_INF)
    neg_cur = jnp.where(ci <= ri, 0.0, NEG_INF)
    return neg_prev, neg_cur


def _dil_specs(L, R, t, qcol, kcol, vcol):
    W = D_MODEL
    prev = lambda qi: jnp.maximum(qi - 1, 0)
    return dict(
        q=pl.BlockSpec((t, W), lambda r, qi: (qi, qcol(r))),
        kp=pl.BlockSpec((t, W), lambda r, qi: (prev(qi), kcol(r))), kc=pl.BlockSpec((t, W), lambda r, qi: (qi, kcol(r))),
        vp=pl.BlockSpec((t, W), lambda r, qi: (prev(qi), vcol(r))), vc=pl.BlockSpec((t, W), lambda r, qi: (qi, vcol(r))),
        own=pl.BlockSpec((t, W), lambda r, qi: (qi, r)), tab=pl.BlockSpec((t, LANES), lambda r, qi: (qi, r)))


def _dil_fwd(name, x, qcol, kcol, vcol, R, L):
    t = BAND_STEPS
    W = D_MODEL
    sp = _dil_specs(L, R, t, qcol, kcol, vcol)

    def body(q_ref, kp_ref, kc_ref, vp_ref, vc_ref, o_ref, lse_ref):
        lo = _lo_lanes()
        neg_p, neg_c = _band_masks(t, pl.program_id(1) == 0)
        s_p, s_c = [], []
        for h in range(N_HEADS):
            cols = _pair_cols(h)
            qh = jnp.where(_head_lanes(lo, h), q_ref[:, cols], 0)
            s_p.append(_dot(qh, kp_ref[:, cols], NT))
            s_c.append(_dot(qh, kc_ref[:, cols], NT))
        s_p = jnp.stack(s_p) + neg_p[None]
        s_c = jnp.stack(s_c) + neg_c[None]
        m = jnp.maximum(jnp.max(s_p, axis=2, keepdims=True), jnp.max(s_c, axis=2, keepdims=True))
        p_p, p_c = jnp.exp(s_p - m), jnp.exp(s_c - m)
        l = jnp.sum(p_p, axis=2, keepdims=True) + jnp.sum(p_c, axis=2, keepdims=True)
        inv, lse = 1.0 / l, m + jnp.log(l)
        p_p, p_c = p_p.astype(BF16), p_c.astype(BF16)
        for p in range(N_PAIRS):
            cols = _pair_cols(2 * p)
            o2 = jnp.zeros((t, LANES), F32)
            for h in (2 * p, 2 * p + 1):
                hm = _head_lanes(lo, h)
                pv = _dot(p_p[h], jnp.where(hm, vp_ref[:, cols], 0), NN) + _dot(p_c[h], jnp.where(hm, vc_ref[:, cols], 0), NN)
                o2 = o2 + pv * inv[h]
            o_ref[:, cols] = o2
            lse_ref[:, cols] = jnp.where(lo, lse[2 * p], lse[2 * p + 1])

    return pl.pallas_call(
        body, grid=(R, L // t), in_specs=[sp["q"], sp["kp"], sp["kc"], sp["vp"], sp["vc"]], out_specs=[sp["own"], sp["own"]],
        out_shape=[jax.ShapeDtypeStruct((L, R * W), F32), jax.ShapeDtypeStruct((L, R * W), F32)],
        compiler_params=_params("parallel", "parallel"), name=name)(x[0], x[1], x[1], x[2], x[2])


def _dil_scores(lo, q_ref, do_ref, o_ref, lse_ref, kv_refs):
    s = [[] for _ in kv_refs]
    dp = [[] for _ in kv_refs]
    lse, d = [], []
    for h in range(N_HEADS):
        cols = _pair_cols(h)
        hm = _head_lanes(lo, h)
        qh, doh = jnp.where(hm, q_ref[:, cols], 0), jnp.where(hm, do_ref[:, cols], 0)
        for i, (k_ref, v_ref) in enumerate(kv_refs):
            s[i].append(_dot(qh, k_ref[:, cols], NT))
            dp[i].append(_dot(doh, v_ref[:, cols], NT))
        lse.append(_rep_rows(lse_ref[:, cols], lo)[h % 2])
        dd = do_ref[:, cols].astype(F32) * o_ref[:, cols].astype(F32)
        d.append(jnp.sum(jnp.where(hm, dd, 0.0), axis=1, keepdims=True))
    return (*[jnp.stack(x) for x in s], *[jnp.stack(x) for x in dp], jnp.stack(lse), jnp.stack(d))


def _dil_bwd(name, x, do, o, lse, tabs, R, L):
    t = BAND_STEPS
    W = D_MODEL
    nq = L // t
    qb = lambda step: nq - 1 - step
    kb = lambda step: jnp.maximum(qb(step) - 1, 0)
    at = lambda f, width: pl.BlockSpec((t, width), lambda r, step: (f(step), r))

    def body(q_ref, kp_ref, kc_ref, vp_ref, vc_ref, do_ref, o_ref, lse_ref, c_ref, sa_ref, sb_ref, dq_ref, dk_ref, dv_ref,
             dk_scr, dv_scr):
        qi = nq - 1 - pl.program_id(1)
        lo = _lo_lanes()
        unrotate = lambda x: _rotate(x, c_ref[...], sa_ref[...], sb_ref[...], -1.0).astype(BF16)

        @pl.when(qi == nq - 1)
        def _():
            dk_scr[...] = jnp.zeros_like(dk_scr)
            dv_scr[...] = jnp.zeros_like(dv_scr)

        neg_p, neg_c = _band_masks(t, qi == 0)
        s_p, s_c, dp_p, dp_c, lse_h, d = _dil_scores(lo, q_ref, do_ref, o_ref, lse_ref, ((kp_ref, vp_ref), (kc_ref, vc_ref)))
        p_p, p_c = jnp.exp(s_p + neg_p[None] - lse_h), jnp.exp(s_c + neg_c[None] - lse_h)
        ds_p, ds_c = (p_p * (dp_p - d)).astype(BF16), (p_c * (dp_c - d)).astype(BF16)
        p_p, p_c = p_p.astype(BF16), p_c.astype(BF16)
        for p in range(N_PAIRS):
            cols = _pair_cols(2 * p)
            dq2 = jnp.zeros((t, LANES), F32)
            dk_cur, dv_cur = dk_scr[:, cols], dv_scr[:, cols]
            dk_prev, dv_prev = jnp.zeros((t, LANES), F32), jnp.zeros((t, LANES), F32)
            for h in (2 * p, 2 * p + 1):
                hm = _head_lanes(lo, h)
                qh, doh = jnp.where(hm, q_ref[:, cols], 0), jnp.where(hm, do_ref[:, cols], 0)
                dq2 = dq2 + _dot(ds_p[h], jnp.where(hm, kp_ref[:, cols], 0), NN) + _dot(ds_c[h], jnp.where(hm, kc_ref[:, cols], 0), NN)
                dk_prev, dv_prev = dk_prev + _dot(ds_p[h], qh, TN), dv_prev + _dot(p_p[h], doh, TN)
                dk_cur, dv_cur = dk_cur + _dot(ds_c[h], qh, TN), dv_cur + _dot(p_c[h], doh, TN)
            dq_ref[:, cols] = unrotate(dq2 * SOFTMAX_SCALE)
            dk_ref[:, cols] = unrotate(dk_cur)
            dv_ref[:, cols] = dv_cur.astype(BF16)
            dk_scr[:, cols] = dk_prev
            dv_scr[:, cols] = dv_prev

    wide = jax.ShapeDtypeStruct((L, R * W), BF16)
    return pl.pallas_call(
        body, grid=(R, nq),
        in_specs=[at(qb, W), at(kb, W), at(qb, W), at(kb, W), at(qb, W), at(qb, W), at(qb, W), at(qb, W),
                  at(qb, LANES), at(qb, LANES), at(qb, LANES)],
        out_specs=[at(qb, W), at(qb, W), at(qb, W)], out_shape=[wide, wide, wide],
        scratch_shapes=[pltpu.VMEM((t, W), F32), pltpu.VMEM((t, W), F32)],
        compiler_params=_params("parallel", "arbitrary"), name=name)(x[0], x[1], x[1], x[2], x[2], do, o, lse, *tabs)


def _fox_operands(q2, k2, kb2, lo, hh):
    lane = lax.broadcasted_iota(jnp.int32, (1, LANES), 1)
    if hh == 0:
        ones = ((lane >= HEAD_DIM) & (lane < HEAD_DIM + 3)).astype(BF16)
        return jnp.where(lo, q2, ones), jnp.where(lo, k2, kb2)
    ones = (lane < 3).astype(BF16)
    return jnp.where(lo, ones, q2), jnp.where(lo, kb2, k2)


def _causal_neg(t):
    ri = lax.broadcasted_iota(jnp.int32, (t, t), 0)
    ci = lax.broadcasted_iota(jnp.int32, (t, t), 1)
    return jnp.where(ci <= ri, 0.0, NEG_INF)


def _fox_fwd(name, qkv, kbias, t):
    S = qkv.shape[0]
    W = D_MODEL
    nq = S // t
    rep = t // LANES

    def body(q_ref, k_ref, v_ref, kb_ref, o_ref, lse_ref, m_scr, l_scr, acc_scr):
        qi, j = pl.program_id(0), pl.program_id(1)
        lo = _lo_lanes()

        @pl.when(j == 0)
        def _():
            m_scr[...] = jnp.full_like(m_scr, NEG_INF)
            l_scr[...] = jnp.zeros_like(l_scr)
            acc_scr[...] = jnp.zeros_like(acc_scr)

        def step(masked):
            neg = _causal_neg(t) if masked else None

            def pair(p, carry):
                cs = pl.ds(pl.multiple_of(p * LANES, LANES), LANES)
                q2, k2, v2, kb2 = q_ref[:, cs], k_ref[:, cs], v_ref[:, cs], kb_ref[:, cs]
                pvs, alphas = [], []
                for hh in range(2):
                    hm = lo if hh == 0 else jnp.logical_not(lo)
                    qh, kh = _fox_operands(q2, k2, kb2, lo, hh)
                    s = _dot(qh, kh, NT)
                    if masked:
                        s = s + neg
                    h = 2 * p + hh
                    m_prev = m_scr[h]
                    m_new = jnp.maximum(m_prev, jnp.max(s, axis=1, keepdims=True))
                    pe = jnp.exp(s - jnp.tile(m_new, (1, rep)))
                    alpha = jnp.exp(m_prev - m_new)
                    l_scr[h] = alpha * l_scr[h] + jnp.sum(pe, axis=1, keepdims=True)
                    m_scr[h] = m_new
                    pvs.append(_dot(pe.astype(BF16), jnp.where(hm, v2, 0), NN))
                    alphas.append(alpha)
                acc_scr[:, cs] = acc_scr[:, cs] * jnp.where(lo, alphas[0], alphas[1]) + pvs[0] + pvs[1]
                return carry

            lax.fori_loop(0, N_PAIRS, pair, 0, unroll=4)

        @pl.when(j < qi)
        def _():
            step(False)

        @pl.when(j == qi)
        def _():
            step(True)

        @pl.when(j == nq - 1)
        def _():
            for p in range(N_PAIRS):
                cols = slice(p * LANES, (p + 1) * LANES)
                l2 = jnp.where(lo, l_scr[2 * p], l_scr[2 * p + 1])
                m2 = jnp.where(lo, m_scr[2 * p], m_scr[2 * p + 1])
                o_ref[:, cols] = (acc_scr[:, cols] / l2).astype(BF16)
                lse_ref[:, cols] = m2 + jnp.log(l2)

    kv = lambda col: pl.BlockSpec((t, W), lambda qi, j: (jnp.minimum(j, qi), col))
    own = pl.BlockSpec((t, W), lambda qi, j: (qi, 0))
    return pl.pallas_call(
        body, grid=(nq, nq), in_specs=[own, kv(1), kv(2), kv(0)], out_specs=[own, own],
        out_shape=[jax.ShapeDtypeStruct((S, W), BF16), jax.ShapeDtypeStruct((S, W), F32)],
        scratch_shapes=[pltpu.VMEM((N_HEADS, t, LANES), F32), pltpu.VMEM((N_HEADS, t, LANES), F32), pltpu.VMEM((t, W), F32)],
        compiler_params=_params("parallel", "arbitrary"), name=name)(qkv, qkv, qkv, kbias)


def _fox_head_grads(qh, kh, v2, doh, neg, lse_h, d_h, rep):
    s = _dot(qh, kh, NT)
    if neg is not None:
        s = s + neg
    p = jnp.exp(s - jnp.tile(lse_h, (1, rep)))
    return p, p * (_dot(doh, v2, NT) - d_h)


def _fox_bwd(name, qkv, kbias, do, o, lse, t):
    S = qkv.shape[0]
    W = D_MODEL
    nq = S // t
    rep = t // LANES

    def body(q_ref, k_ref, v_ref, kb_ref, do_ref, o_ref, lse_ref, dq_ref, dk_ref, dv_ref, rs_ref, dc_ref, dq_scr, dk_scr, dv_scr):
        kb, j = pl.program_id(0), pl.program_id(1)
        lo = _lo_lanes()
        lane = lax.broadcasted_iota(jnp.int32, (1, LANES), 1)
        rows = pl.ds(pl.multiple_of(j * t, t), t)

        @pl.when((kb == 0) & (j == 0))
        def _():
            dq_scr[...] = jnp.zeros_like(dq_scr)
            rs_ref[...] = jnp.zeros_like(rs_ref)

        @pl.when(j == 0)
        def _():
            dk_scr[...] = jnp.zeros_like(dk_scr)
            dv_scr[...] = jnp.zeros_like(dv_scr)
            dc_ref[...] = jnp.zeros_like(dc_ref)

        def step(masked):
            neg = _causal_neg(t) if masked else None

            def pair(p, carry):
                cs = pl.ds(pl.multiple_of(p * LANES, LANES), LANES)
                q2, k2, v2, kb2, do2 = q_ref[:, cs], k_ref[:, cs], v_ref[:, cs], kb_ref[:, cs], do_ref[:, cs]
                dd = do2.astype(F32) * o_ref[:, cs].astype(F32)
                lse_h = _rep_rows(lse_ref[:, cs], lo)
                dq2 = jnp.zeros((t, LANES), F32)
                dv2 = jnp.zeros((t, LANES), F32)
                dk2 = jnp.zeros((t, LANES), F32)
                for hh in range(2):
                    hm = lo if hh == 0 else jnp.logical_not(lo)
                    qh, kh = _fox_operands(q2, k2, kb2, lo, hh)
                    doh = jnp.where(hm, do2, 0)
                    d_h = jnp.sum(jnp.where(hm, dd, 0.0), axis=1, keepdims=True)
                    pr, ds = _fox_head_grads(qh, kh, v2, doh, neg, lse_h[hh], d_h, rep)
                    rs_ref[rows, :] += jnp.where(lane == 2 * p + hh, jnp.sum(ds, axis=1, keepdims=True), 0.0)
                    dc_ref[p, hh:hh + 1, :] += jnp.sum(ds, axis=0, keepdims=True)
                    dsb = ds.astype(BF16)
                    dv2 = dv2 + _dot(pr.astype(BF16), doh, TN)
                    dk2 = dk2 + _dot(dsb, jnp.where(hm, q2, 0), TN)
                    dq2 = dq2 + _dot(dsb, jnp.where(hm, k2, 0), NN)
                dv_scr[:, cs] += dv2
                dk_scr[:, cs] += dk2
                dq_scr[rows, cs] += dq2
                return carry

            lax.fori_loop(0, N_PAIRS, pair, 0, unroll=4)
            if masked:
                dq_ref[...] = (dq_scr[rows, :] * SOFTMAX_SCALE).astype(BF16)

        @pl.when(j > kb)
        def _():
            step(False)

        @pl.when(j == kb)
        def _():
            step(True)

        @pl.when(j == nq - 1)
        def _():
            dv_ref[...] = dv_scr[...].astype(BF16)
            dk_ref[...] = dk_scr[...].astype(BF16)

    qrow = pl.BlockSpec((t, W), lambda kb, j: (jnp.maximum(j, kb), 0))
    krow = lambda col: pl.BlockSpec((t, W), lambda kb, j: (kb, col))
    own = pl.BlockSpec((t, W), lambda kb, j: (kb, 0))
    wide = jax.ShapeDtypeStruct((S, W), BF16)
    return pl.pallas_call(
        body, grid=(nq, nq), in_specs=[qrow, krow(1), krow(2), krow(0), qrow, qrow, qrow],
        out_specs=[own, own, own, pl.BlockSpec((S, LANES), lambda kb, j: (0, 0)), pl.BlockSpec((N_PAIRS, 2, t), lambda kb, j: (0, 0, kb))],
        out_shape=[wide, wide, wide, jax.ShapeDtypeStruct((S, LANES), F32), jax.ShapeDtypeStruct((N_PAIRS, 2, S), F32)],
        scratch_shapes=[pltpu.VMEM((S, W), F32), pltpu.VMEM((t, W), F32), pltpu.VMEM((t, W), F32)],
        compiler_params=pltpu.CompilerParams(dimension_semantics=("arbitrary", "arbitrary"), vmem_limit_bytes=FOX_BWD_VMEM),
        name=name)(qkv, qkv, qkv, kbias, do, o, lse)


def _view_spec(tm, R, index=lambda i: (i, 0)):
    return pl.BlockSpec((tm // R, R * D_MODEL), index)


def _matmul_nt_views(name, a, w, dils, tm=512):
    S, K = a.shape

    def body(a_ref, w_ref, *rest):
        res = _dot(a_ref[...].astype(BF16), w_ref[...], NT)
        _write_views([res[:, b * LANES:(b + 1) * LANES] for b in range(N_PAIRS)], rest[-1], rest[:-1], dils, tm)

    return pl.pallas_call(
        body, grid=(S // tm,), in_specs=[pl.BlockSpec((tm, K), lambda i: (i, 0)), pl.BlockSpec((D_MODEL, K), lambda i: (0, 0))],
        out_specs=[_view_spec(tm, R) for R in dils],
        out_shape=[jax.ShapeDtypeStruct((S // R, R * D_MODEL), BF16) for R in dils],
        scratch_shapes=[pltpu.VMEM((N_PAIRS, tm, LANES), F32)], compiler_params=_params("parallel"), name=name)(a, w)


def _write_views(chunks, scr, out_refs, dils, tm):
    if any(R > 1 for R in dils):
        _stage_chunks(scr, chunks)
    for ref, R in zip(out_refs, dils):
        for b, x in enumerate(chunks):
            if R == 1:
                ref[:, b * LANES:(b + 1) * LANES] = x.astype(ref.dtype)
                continue
            for r in range(R):
                col = r * D_MODEL + b * LANES
                ref[:, col:col + LANES] = _strided_rows(scr, b, r, tm // R, R).astype(ref.dtype)


def _combine(name, os_, lses, dils, tm=256):
    S = os_[0].shape[0] * dils[0]
    G = len(dils)

    def body(*refs):
        o_refs, l_refs = refs[:G], refs[G:2 * G]
        o_outs, l_outs = refs[2 * G:3 * G], refs[3 * G:4 * G]
        stage = refs[4 * G:]
        for g, R in enumerate(dils):
            if R == 1:
                continue
            for src, dst in ((o_refs[g], stage[2 * g]), (l_refs[g], stage[2 * g + 1])):
                _unstride(lambda r, b, src=src: src[:, r * D_MODEL + b * LANES:r * D_MODEL + (b + 1) * LANES], R, dst, tm)
        o_chunks, l_chunks = [], []
        for b in range(N_PAIRS):
            cols = slice(b * LANES, (b + 1) * LANES)
            os_b = [o_refs[g][:, cols] if R == 1 else stage[2 * g][b] for g, R in enumerate(dils)]
            ls = [l_refs[g][:, cols] if R == 1 else stage[2 * g + 1][b] for g, R in enumerate(dils)]
            m = functools.reduce(jnp.maximum, ls)
            ws = [jnp.exp(l - m) for l in ls]
            den = functools.reduce(jnp.add, ws)
            o_chunks.append(functools.reduce(jnp.add, [w * o for w, o in zip(ws, os_b)]) / den)
            l_chunks.append(m + jnp.log(den))
        _write_views(o_chunks, stage[0], o_outs, dils, tm)
        _write_views(l_chunks, stage[1], l_outs, dils, tm)

    specs = [_view_spec(tm, R) for R in dils]
    shapes = lambda dt: [jax.ShapeDtypeStruct((S // R, R * D_MODEL), dt) for R in dils]
    res = pl.pallas_call(
        body, grid=(S // tm,), in_specs=specs * 2, out_specs=specs * 2, out_shape=shapes(BF16) + shapes(F32),
        scratch_shapes=[pltpu.VMEM((N_PAIRS, tm, LANES), F32)] * (2 * G), compiler_params=_params("parallel"),
        name=name)(*os_, *lses)
    return res[:G], res[G:]


def _tri_matmul(tri, x):
    hi, mid, lo = _split3(x)
    return _dot(tri, hi, NN) + _dot(tri, mid, NN) + _dot(tri, lo, NN)


def _split3(x):
    hi = x.astype(BF16)
    r1 = x - hi.astype(F32)
    mid = r1.astype(BF16)
    return hi, mid, (r1 - mid.astype(F32)).astype(BF16)


def _gate_fwd(name, z, bf, tb=512):
    S = z.shape[0]

    def body(z_ref, b_ref, kb_ref, carry):
        @pl.when(pl.program_id(0) == 0)
        def _():
            carry[...] = jnp.zeros_like(carry)

        lf = jax.nn.log_sigmoid(z_ref[...] + b_ref[...])
        ri = lax.broadcasted_iota(jnp.int32, (tb, tb), 0)
        ci = lax.broadcasted_iota(jnp.int32, (tb, tb), 1)
        tri = (ci <= ri).astype(BF16)
        c = _tri_matmul(tri, lf) + carry[...]
        carry[...] = c[tb - 1:tb, :]
        head = lax.broadcasted_iota(jnp.int32, (LANES, D_MODEL), 0)
        col = lax.broadcasted_iota(jnp.int32, (LANES, D_MODEL), 1)
        base = (head >> 1) * LANES + jnp.where((head & 1) == 0, HEAD_DIM, 0)
        kb = jnp.zeros((tb, D_MODEL), F32)
        for i, piece in enumerate(_split3(-c)):
            place = ((col == base + i) & (head < N_HEADS)).astype(BF16)
            kb = kb + _dot(piece, place, NN)
        kb_ref[...] = kb.astype(BF16)

    row = pl.BlockSpec((tb, LANES), lambda i: (i, 0))
    return pl.pallas_call(
        body, grid=(S // tb,), in_specs=[row, pl.BlockSpec((1, LANES), lambda i: (0, 0))],
        out_specs=pl.BlockSpec((tb, D_MODEL), lambda i: (i, 0)), out_shape=jax.ShapeDtypeStruct((S, D_MODEL), BF16),
        scratch_shapes=[pltpu.VMEM((1, LANES), F32)], compiler_params=_params("arbitrary"), name=name)(z, bf)


def _gate_bwd(name, dc, z, bf, tb=512):
    S = z.shape[0]
    nb = S // tb

    def body(dc_ref, z_ref, b_ref, dz_ref, db_ref, carry):
        @pl.when(pl.program_id(0) == 0)
        def _():
            carry[...] = jnp.zeros_like(carry)
            db_ref[...] = jnp.zeros_like(db_ref)

        ri = lax.broadcasted_iota(jnp.int32, (tb, tb), 0)
        ci = lax.broadcasted_iota(jnp.int32, (tb, tb), 1)
        tri = (ci >= ri).astype(BF16)
        dlf = _tri_matmul(tri, dc_ref[...]) + carry[...]
        carry[...] = dlf[0:1, :]
        dz = dlf * jax.nn.sigmoid(-(z_ref[...] + b_ref[...]))
        dz_ref[...] = dz
        db_ref[...] += jnp.sum(dz, axis=0, keepdims=True)

    row = pl.BlockSpec((tb, LANES), lambda i: (nb - 1 - i, 0))
    vec = pl.BlockSpec((1, LANES), lambda i: (0, 0))
    return pl.pallas_call(
        body, grid=(nb,), in_specs=[row, row, vec], out_specs=[row, vec],
        out_shape=[jax.ShapeDtypeStruct((S, LANES), F32), jax.ShapeDtypeStruct((1, LANES), F32)],
        scratch_shapes=[pltpu.VMEM((1, LANES), F32)], compiler_params=_params("arbitrary"), name=name)(dc, z, bf)


def _ffn_gu(name, n, wgu, comm=None, tm=1024):
    S, D = n.shape
    nb = N_DEV // 2

    def body(n_ref, wg_ref, wu_ref, gu_ref, act_ref):
        x = n_ref[...]
        g = _dot(x, wg_ref[...], NN)
        u = _dot(x, wu_ref[...], NN)
        gu_ref[0] = g.astype(BF16)
        gu_ref[1] = u.astype(BF16)
        act_ref[...] = (g * jax.nn.sigmoid(g) * u).astype(BF16)

    return _call(
        name, body, (nb, S // tm),
        [pl.BlockSpec((tm, D), lambda j, i: (i, 0)), pl.BlockSpec((None, D, FF_BLK), lambda j, i: (j, 0, 0)),
         pl.BlockSpec((None, D, FF_BLK), lambda j, i: (j + nb, 0, 0))],
        [pl.BlockSpec((2, None, tm, FF_BLK), lambda j, i: (0, j, i, 0)), pl.BlockSpec((None, tm, FF_BLK), lambda j, i: (j, i, 0))],
        [jax.ShapeDtypeStruct((2, nb, S, FF_BLK), BF16), jax.ShapeDtypeStruct((nb, S, FF_BLK), BF16)], [],
        (n, wgu, wgu), ("parallel", "parallel"), comm)


def _ffn_down(name, act, wd, resid, comm=None, tm=1024):
    nb, S, _ = act.shape
    D = wd.shape[1]

    def epilogue(acc, ex, outs, j):
        outs[0][...] = acc + ex[0][...]

    o_spec = pl.BlockSpec((tm, D), lambda i, j, k: (i, 0))
    return _mm_call(name, (S // tm, 1, nb), act, pl.BlockSpec((None, tm, FF_BLK), lambda i, j, k: (k, i, 0)),
                    wd, pl.BlockSpec((FF_BLK, D), lambda i, j, k: (k, 0)), NN,
                    [jax.ShapeDtypeStruct((S, D), F32)], [o_spec], (tm, D), epilogue, (resid,), (o_spec,), comm=comm)


def _ffn_dact(name, dh, wd, gu, comm=None, tm=512):
    S, D = dh.shape
    nb = N_DEV // 2

    def epilogue(acc, ex, outs, j):
        g = ex[0][0].astype(F32)
        u = ex[0][1].astype(F32)
        sig = jax.nn.sigmoid(g)
        outs[0][0] = (acc * u * (sig * (1.0 + g * (1.0 - sig)))).astype(BF16)
        outs[0][1] = (acc * (g * sig)).astype(BF16)

    gu_spec = pl.BlockSpec((2, None, tm, FF_BLK), lambda j, i, k: (0, j, i, 0))
    return _mm_call(name, (nb, S // tm, 1), dh, pl.BlockSpec((tm, D), lambda j, i, k: (i, 0)),
                    wd, pl.BlockSpec((FF_BLK, D), lambda j, i, k: (j, 0)), NT,
                    [jax.ShapeDtypeStruct((2, nb, S, FF_BLK), BF16)], [gu_spec], (tm, FF_BLK), epilogue, (gu,), (gu_spec,),
                    col_axis=0, comm=comm)


def _ffn_dwgu(name, n, dgu, comm=None, tm=1024, tk=1024):
    S, D = n.shape
    dgu8 = dgu.reshape(N_DEV, S, FF_BLK)
    return _mm_call(name, (N_DEV, D // tm, S // tk), n, pl.BlockSpec((tk, tm), lambda d, i, k: (k, i)),
                    dgu8, pl.BlockSpec((None, tk, FF_BLK), lambda d, i, k: (d, k, 0)), TN,
                    [jax.ShapeDtypeStruct((N_DEV, D, FF_BLK), BF16)],
                    [pl.BlockSpec((None, tm, FF_BLK), lambda d, i, k: (d, i, 0))], (tm, FF_BLK), comm=comm)


def _ffn_dwd(name, act, dh, tk=1024):
    nb, S, _ = act.shape
    D = dh.shape[1]
    out = _mm_call(name, (nb, 1, S // tk), act, pl.BlockSpec((None, tk, FF_BLK), lambda b, j, k: (b, k, 0)),
                   dh, pl.BlockSpec((tk, D), lambda b, j, k: (k, 0)), TN,
                   [jax.ShapeDtypeStruct((nb, FF_BLK, D), BF16)],
                   [pl.BlockSpec((None, FF_BLK, D), lambda b, j, k: (b, 0, 0))], (FF_BLK, D))[0]
    return out.reshape(N_DEV, FF_BLK // 2, D)


def _ffn_dn(name, dgu, wgu, comm=None, tm=1024):
    S = dgu.shape[2]
    D = wgu.shape[1]
    dgu8 = dgu.reshape(N_DEV, S, FF_BLK)
    return _mm_call(name, (S // tm, 1, N_DEV), dgu8, pl.BlockSpec((None, tm, FF_BLK), lambda i, j, k: (k, i, 0)),
                    wgu, pl.BlockSpec((None, D, FF_BLK), lambda i, j, k: (k, 0, 0)), NT,
                    [jax.ShapeDtypeStruct((S, D), F32)], [pl.BlockSpec((tm, D), lambda i, j, k: (i, 0))], (tm, D), comm=comm)


def _adamw(name, parts, w, m, v, tr):
    rows, cols = w.shape
    n_parts = len(parts)
    c1 = 1.0 - ADAM_B1 ** ADAM_STEP
    c2 = 1.0 - ADAM_B2 ** ADAM_STEP

    def body(*refs):
        p_refs = refs[:n_parts]
        w_ref, m_ref, v_ref, g_ref, d_ref, nm_ref, nv_ref = refs[n_parts:]
        g = p_refs[0][...].astype(F32)
        for r in p_refs[1:]:
            g = g + r[...].astype(F32)
        mm = ADAM_B1 * m_ref[...] + (1.0 - ADAM_B1) * g
        vv = ADAM_B2 * v_ref[...] + (1.0 - ADAM_B2) * (g * g)
        g_ref[...] = g
        nm_ref[...] = mm
        nv_ref[...] = vv
        d_ref[...] = -ADAM_LR * ((mm / c1) / (jnp.sqrt(vv / c2) + ADAM_EPS) + ADAM_WD * w_ref[...])

    blk = pl.BlockSpec((tr, cols), lambda i: (i, 0))
    out = jax.ShapeDtypeStruct((rows, cols), F32)
    return pl.pallas_call(
        body, grid=(rows // tr,), in_specs=[blk] * (n_parts + 3), out_specs=[blk] * 4, out_shape=[out] * 4,
        compiler_params=_params("parallel"), name=name)(*parts, w, m, v)


def _position():
    return lax.axis_index("x"), lax.axis_index("y"), lax.axis_index("c")


def _other_chips():
    x, y, _ = _position()
    return [(1 - x, y), (x, 1 - y), (1 - x, 1 - y)]


def _remote(src, dst, send, recv, k, to):
    return pltpu.make_async_remote_copy(src_ref=src, dst_ref=dst, send_sem=send.at[k], recv_sem=recv.at[k],
                                        device_id=to, device_id_type=MESH)


def _ag_send(blocks, direct=False):
    n_peer = 7 if direct else 4

    def copies(ins, outs, send, recv, local, r0=0, l0=0):
        x, y, c = _position()
        me = 4 * x + 2 * y + c
        peers = [(x, y, 1 - c)] + [(px, py, c) for px, py in _other_chips()]
        if direct:
            peers += [(px, py, 1 - c) for px, py in _other_chips()]
        cps = []
        for t, (src, dst) in enumerate(zip(ins, outs)):
            cps.append(pltpu.make_async_copy(src, dst.at[me], local.at[l0 + t]))
            cps += [_remote(src, dst.at[me], send, recv, r0 + n_peer * t + k, to) for k, to in enumerate(peers)]
        return cps

    outs = tuple(jax.ShapeDtypeStruct((N_DEV,) + b.shape, b.dtype) for b in blocks)
    return _Comm(tuple(blocks), outs, {}, copies, n_peer * len(blocks), len(blocks))


def _ag_forward(bufs):
    def copies(ins, outs, send, recv, local, r0=0, l0=0):
        x, y, c = _position()
        cps = []
        for t, buf in enumerate(outs):
            for k, (px, py) in enumerate(_other_chips()):
                slot = buf.at[4 * px + 2 * py + c]
                cps.append(_remote(slot, slot, send, recv, r0 + 3 * t + k, (x, y, 1 - c)))
        return cps

    outs = tuple(jax.ShapeDtypeStruct(b.shape, b.dtype) for b in bufs)
    return _Comm(tuple(bufs), outs, {t: t for t in range(len(bufs))}, copies, 3 * len(bufs), 0)


def _rs_swap(shares):
    def copies(ins, outs, send, recv, local, r0=0, l0=0):
        x, y, c = _position()
        return [_remote(src.at[:, 1 - c], dst, send, recv, r0 + t, (x, y, 1 - c)) for t, (src, dst) in enumerate(zip(ins, outs))]

    ins = tuple(s.reshape((4, 2) + s.shape[1:]) for s in shares)
    outs = tuple(jax.ShapeDtypeStruct((4,) + s.shape[1:], s.dtype) for s in shares)
    return _Comm(ins, outs, {}, copies, len(shares), 0)


def _rs_exchange(sums):
    def copies(ins, outs, send, recv, local, r0=0, l0=0):
        _, _, c = _position()
        return [_remote(src.at[2 * px + py], dst.at[k], send, recv, r0 + 3 * t + k, (px, py, c))
                for t, (src, dst) in enumerate(zip(ins, outs)) for k, (px, py) in enumerate(_other_chips())]

    outs = tuple(jax.ShapeDtypeStruct((3,) + s.shape[1:], s.dtype) for s in sums)
    return _Comm(tuple(sums), outs, {}, copies, 3 * len(sums), 0)


def _comm_call(name, comm):
    return _call(name, lambda: None, (), [], [], [], [], (), (), comm)


def _pair_sum(name, share, got, core, tr):
    _, rows, cols = share.shape

    def body(c_ref, a_ref, b_ref, o_ref):
        o_ref[...] = (a_ref[...].astype(F32) + b_ref[...].astype(F32)).astype(o_ref.dtype)

    grid_spec = pltpu.PrefetchScalarGridSpec(
        num_scalar_prefetch=1, grid=(4, rows // tr),
        in_specs=[pl.BlockSpec((None, None, tr, cols), lambda q, i, c: (q, c[0], i, 0)),
                  pl.BlockSpec((None, tr, cols), lambda q, i, c: (q, i, 0))],
        out_specs=pl.BlockSpec((None, tr, cols), lambda q, i, c: (q, i, 0)))
    return pl.pallas_call(
        body, grid_spec=grid_spec, out_shape=jax.ShapeDtypeStruct((4, rows, cols), share.dtype),
        compiler_params=_params("parallel", "parallel"), name=name)(core, share.reshape(4, 2, rows, cols), got)


TENSORS = ("a_w_in", "a_w_out", "b_w_in", "b_w_out", "gu0", "gu1", "dn0", "dn1")
ROW_TILE = {"a_w_in": 256, "a_w_out": 128, "b_w_in": 256, "b_w_out": 128, "gu0": 256, "gu1": 256, "dn0": 176, "dn1": 176}
A_BLK = 9 * D_MODEL // N_DEV
B_BLK = 386
B_IN = 3 * D_MODEL + N_HEADS
B_IN_PAD = 3 * D_MODEL + LANES


def kernel(x, a_norm, a_w_in, a_w_out, b_norm, b_w_in, b_f, b_w_out, ffn_norm, ffn_w_gu, ffn_w_down, final_norm, loss_target, m_a_norm, m_a_w_in, m_a_w_out, m_b_norm, m_b_w_in, m_b_f, m_b_w_out, m_ffn_norm, m_ffn_w_gu, m_ffn_w_down, m_final_norm, v_a_norm, v_a_w_in, v_a_w_out, v_b_norm, v_b_w_in, v_b_f, v_b_w_out, v_ffn_norm, v_ffn_w_gu, v_ffn_w_down, v_final_norm):
    S = x.shape[1]
    xi, yi, ci = _position()
    dev = 4 * xi + 2 * yi + ci
    core = ci.reshape(1).astype(jnp.int32)
    h0, target = x.reshape(S, D_MODEL), loss_target.reshape(S, D_MODEL)

    def shards(a_in, a_out, b_in, b_out, gu, dn):
        return {"a_w_in": a_in[0], "a_w_out": a_out[0], "b_w_in": b_in[0], "b_w_out": b_out[0],
                "gu0": gu[0], "gu1": gu[1], "dn0": dn[0], "dn1": dn[1]}

    w_sh = shards(a_w_in, a_w_out, b_w_in, b_w_out, ffn_w_gu, ffn_w_down)
    m_sh = shards(m_a_w_in, m_a_w_out, m_b_w_in, m_b_w_out, m_ffn_w_gu, m_ffn_w_down)
    v_sh = shards(v_a_w_in, v_a_w_out, v_b_w_in, v_b_w_out, v_ffn_w_gu, v_ffn_w_down)
    wb = {n: w_sh[n].astype(BF16) for n in TENSORS}
    bf_pad = jnp.pad(b_f, ((0, 0), (0, LANES - N_HEADS)))
    tabs = _rope_tables(S)

    g_ain, g_aout = _comm_call("gather_a", _ag_send([wb["a_w_in"], wb["a_w_out"]]))
    dils = [dil for _, dil in DILATED_PATTERNS]
    n0_views, (g_ain, g_aout) = _rms_fwd("rms_a", h0, a_norm[0], dils, _ag_forward([g_ain, g_aout]))
    n0 = n0_views[0]
    w_a_in = g_ain.transpose(1, 0, 2).reshape(D_MODEL, 9 * D_MODEL)
    sends = [[wb["b_w_in"], wb["b_w_out"]], [wb["gu0"]], [wb["dn0"], jnp.pad(b_norm, ((0, 7), (0, 0)))]]
    qkv_a, later = [], []
    for g, dil in enumerate(dils):
        qkv_g, sent = _a_proj("proj_a%d" % g, n0, w_a_in, g, dil, tabs, _ag_send(sends[g]))
        qkv_a.append(qkv_g)
        later += sent
    cols = [lambda r: r] * 3
    groups = [(g, dil, S // dil, qkv_a[g]) for g, (window, dil) in enumerate(DILATED_PATTERNS)]
    fwd =[_dil_fwd("dil_fwd%d" % g, view, *cols, dil, L) for g, dil, L, view in groups]
    o_views, lse_views = _combine("dil_combine", [f[0] for f in fwd], [f[1] for f in fwd], dils)
    o_a = o_views[0]
    w_a_out = g_aout.reshape(D_MODEL, D_MODEL)
    h1, (g_bin, g_bout, g_gu0, g_dn0, g_bnorm) = _matmul("out_a", o_a, w_a_out, "nn", F32, TM, 1024, 1024, resid=h0,
                                                         comm=_ag_forward(later))

    n1 = _rms_fwd("rms_f0", h1, ffn_norm[0])
    gu0, act0, g_gu1 = _ffn_gu("gu_f0", n1, g_gu0, _ag_send([wb["gu1"]]))
    w_dn0 = g_dn0.reshape(D_FF, D_MODEL)
    h2, g_dn1 = _ffn_down("down_f0", act0, w_dn0, h1, _ag_send([wb["dn1"]]))

    b_norm_full = g_bnorm[:, 0].reshape(D_MODEL)
    w_b_in = g_bin.transpose(1, 0, 2).reshape(D_MODEL, B_IN)
    w_b_gate = jnp.pad(w_b_in[:, 3 * D_MODEL:], ((0, 0), (0, LANES - N_HEADS)))
    w_b_cat = jnp.concatenate([w_b_in[:, :3 * D_MODEL], w_b_gate], axis=1)
    w_b_out = g_bout.reshape(D_MODEL, D_MODEL)
    n2 = _rms_fwd("rms_b", h2, b_norm_full)
    qkv, (g_gu1, g_dn1) = _matmul("proj_b", n2, w_b_in[:, :3 * D_MODEL], "nn", BF16, TM, 1024, 1024, col0_scale=SOFTMAX_SCALE,
                                  comm=_ag_forward([g_gu1, g_dn1]))
    z = _matmul("gate_b", n2, w_b_gate, "nn", F32, TM, LANES, 1024)
    kbias = _gate_fwd("gate_cumsum", z, bf_pad)
    tf = min(S, 512)
    o_b, lse_b = _fox_fwd("fox_fwd", qkv, kbias, tf)
    h3 = _matmul("out_b", o_b, w_b_out, "nn", F32, TM, 1024, 1024, resid=h2)

    w_dn1 = g_dn1.reshape(D_FF, D_MODEL)
    n3 = _rms_fwd("rms_f1", h3, ffn_norm[1])
    gu1, act1 = _ffn_gu("gu_f1", n3, g_gu1)
    h4 = _ffn_down("down_f1", act1, w_dn1, h3)[0]

    dh4, d_final, loss, dh4_16 = _loss_head("loss_head", h4, final_norm, target)

    share, got, sums, others = {}, {}, {}, {}

    def pair_sums(*names):
        for n in names:
            sums[n] = _pair_sum("pair_" + n, share[n], got[n], core, ROW_TILE[n])

    dgu1 = _ffn_dact("dact_f1", dh4_16, w_dn1, gu1)[0]
    share["dn1"] = _ffn_dwd("dwd_f1", act1, dh4_16)
    share["gu1"] = _ffn_dwgu("dwgu_f1", n3, dgu1)[0]
    dn3, got["gu1"], got["dn1"] = _ffn_dn("dn_f1", dgu1, g_gu1, _rs_swap([share["gu1"], share["dn1"]]))
    dh3, d_ffn1, dh3_16 = _rms_bwd("rmsb_f1", dn3, h3, ffn_norm[1], dh4)
    pair_sums("gu1", "dn1")

    do_b = _matmul("dout_b", dh3_16, w_b_out, "nt", BF16, TM, 1024, 1024)
    share["b_w_out"] = _matmul("dwout_b", o_b, dh3_16, "tn", BF16, TM, 1024, 1024).reshape(N_DEV, 128, D_MODEL)
    dq_b, dk_b, dv_b, ds_rowsum, ds_colsum = _fox_bwd("fox_bwd", qkv, kbias, do_b, o_b, lse_b, tf)
    dc = ds_rowsum[:, :N_HEADS] - ds_colsum.reshape(N_HEADS, S).T
    dz, d_bf = _gate_bwd("gate_bwd", jnp.pad(dc, ((0, 0), (0, LANES - N_HEADS))), z, bf_pad)
    dproj_b = jnp.concatenate([dq_b, dk_b, dv_b, dz.astype(BF16)], axis=1)
    dw_b_in, (others["gu1"],) = _matmul("dwin_b", n2, dproj_b, "tn", BF16, TM, B_IN_PAD // 5, 1024, comm=_rs_exchange([sums["gu1"]]))
    dn2, (others["dn1"],) = _matmul("dn_b", dproj_b, w_b_cat, "nt", F32, TM, 1024, B_IN_PAD // 5, comm=_rs_exchange([sums["dn1"]]))
    dh2, d_bnorm, dh2_16 = _rms_bwd("rmsb_b", dn2, h2, b_norm_full, dh3)
    share["b_w_in"] = dw_b_in[:, :B_IN].reshape(D_MODEL, N_DEV, B_BLK).transpose(1, 0, 2)

    dgu0, got["b_w_in"], got["b_w_out"] = _ffn_dact("dact_f0", dh2_16, w_dn0, gu0, _rs_swap([share["b_w_in"], share["b_w_out"]]))
    share["dn0"] = _ffn_dwd("dwd_f0", act0, dh2_16)
    pair_sums("b_w_in", "b_w_out")
    share["gu0"], others["b_w_in"], others["b_w_out"] = _ffn_dwgu(
        "dwgu_f0", n1, dgu0, _rs_exchange([sums["b_w_in"], sums["b_w_out"]]))
    dn1, got["gu0"], got["dn0"] = _ffn_dn("dn_f0", dgu0, g_gu0, _rs_swap([share["gu0"], share["dn0"]]))
    dh1, d_ffn0, dh1_16 = _rms_bwd("rmsb_f0", dn1, h1, ffn_norm[0], dh2)
    pair_sums("gu0", "dn0")

    do_views = _matmul_nt_views("dout_a", dh1_16, w_a_out, dils)
    share["a_w_out"] = _matmul("dwout_a", o_a, dh1_16, "tn", BF16, TM, 1024, 1024).reshape(N_DEV, 128, D_MODEL)
    pieces = []
    for g, dil, L, view in groups:
        rot = tuple(tb.reshape(L, dil * LANES) for tb in tabs)
        grads = _dil_bwd("dil_bwd%d" % g, view, do_views[g], o_views[g], lse_views[g], rot, dil, L)
        pieces += list(grads)
    dw_a_in, others["gu0"], others["dn0"] = _a_dw("dwin_a", n0_views, pieces, dils, _rs_exchange([sums["gu0"], sums["dn0"]]))
    share["a_w_in"] = dw_a_in.reshape(D_MODEL, N_DEV, A_BLK).transpose(1, 0, 2)
    got["a_w_in"], got["a_w_out"] = _comm_call("swap_a", _rs_swap([share["a_w_in"], share["a_w_out"]]))
    pair_sums("a_w_in", "a_w_out")
    dn0, others["a_w_in"], others["a_w_out"] = _a_dn("dn_a", pieces, dils, w_a_in, _rs_exchange([sums["a_w_in"], sums["a_w_out"]]))
    dx, d_anorm = _rms_bwd("rmsb_a", dn0, h0, a_norm[0], dh1, copy16=False)

    misc = jnp.concatenate([d_bf[:, :N_HEADS], loss[:, :1], jnp.zeros((1, D_MODEL - N_HEADS - 1), F32)], axis=1)
    small = jnp.concatenate([d_anorm, d_ffn0, d_ffn1, d_final, d_bnorm, misc, jnp.zeros((2, D_MODEL), F32)], axis=0)
    small_all, = _comm_call("gather_small", _ag_send([small], direct=True))

    outs = {}
    for n in TENSORS:
        mine = lax.dynamic_index_in_dim(sums[n], 2 * xi + yi, axis=0, keepdims=False)
        outs[n] = _adamw("adamw_" + n, [mine] + [others[n][k] for k in range(3)], w_sh[n], m_sh[n], v_sh[n], ROW_TILE[n])

    pad_vec = lambda a: jnp.pad(a, ((0, 0), (0, D_MODEL - a.shape[1])))

    def small_pack(an, fn, fin, bf):
        return jnp.concatenate([an, fn, fin.reshape(1, D_MODEL), jnp.zeros((1, D_MODEL), F32), pad_vec(bf),
                                jnp.zeros((2, D_MODEL), F32)], axis=0)

    sg, sd, sm, sv = _adamw("adamw_small", [small_all[d] for d in range(N_DEV)], small_pack(a_norm, ffn_norm, final_norm, b_f),
                            small_pack(m_a_norm, m_ffn_norm, m_final_norm, m_b_f),
                            small_pack(v_a_norm, v_ffn_norm, v_final_norm, v_b_f), 8)
    g_bn = lax.dynamic_slice(sg[4:5], (0, dev * LANES), (1, LANES))
    bn = _adamw("adamw_b_norm", [g_bn], b_norm, m_b_norm, v_b_norm, 1)

    def tree(i):
        full = lambda name, ref: outs[name][i].reshape(ref.shape)
        sml = (sg, sd, sm, sv)[i]
        return dict(
            a_norm=sml[0:1], a_w_in=full("a_w_in", a_w_in), a_w_out=full("a_w_out", a_w_out), b_norm=bn[i],
            b_w_in=full("b_w_in", b_w_in), b_f=sml[5:6, :N_HEADS], b_w_out=full("b_w_out", b_w_out), ffn_norm=sml[1:3],
            ffn_w_gu=jnp.stack([outs["gu0"][i], outs["gu1"][i]]).reshape(ffn_w_gu.shape),
            ffn_w_down=jnp.stack([outs["dn0"][i], outs["dn1"][i]]).reshape(ffn_w_down.shape), final_norm=sml[3])

    order = ("a_norm", "a_w_in", "a_w_out", "b_norm", "b_w_in", "b_f", "b_w_out", "ffn_norm", "ffn_w_gu", "ffn_w_down", "final_norm")
    result = [sg[5, N_HEADS], dx.reshape(x.shape)]
    for i in range(4):
        t = tree(i)
        result += [t[n] for n in order]
    return tuple(result)
```

```python
import functools
from typing import Callable, NamedTuple

import jax
import jax.numpy as jnp
from jax import lax
from jax.experimental import pallas as pl
from jax.experimental.pallas import tpu as pltpu

F32 = jnp.float32
BF16 = jnp.bfloat16

D_MODEL = 1024
N_HEADS = 16
HEAD_DIM = 64
N_PAIRS = N_HEADS // 2
LANES = 128
DILATED_PATTERNS = ((128, 1), (512, 4), (2048, 16))
BAND_STEPS = 128
ROT_DIM = HEAD_DIM // 4
ROPE_THETA = 500000.0
D_FF = 2816
RMS_EPS = 1e-6
NEG_INF = -1e30
SOFTMAX_SCALE = HEAD_DIM ** -0.5
N_DEV = 8
FF_BLK = 2 * D_FF // N_DEV
ADAM_LR, ADAM_B1, ADAM_B2, ADAM_EPS, ADAM_WD, ADAM_STEP = 0.001, 0.9, 0.999, 1e-08, 0.01, 10
VMEM_LIMIT = 52 * 1024 * 1024
FOX_BWD_VMEM = 60 * 1024 * 1024
TM = 1024
MESH = pl.DeviceIdType.MESH

NN = (((1,), (0,)), ((), ()))
NT = (((1,), (1,)), ((), ()))
TN = (((0,), (0,)), ((), ()))


def _params(*sem, vmem=VMEM_LIMIT):
    return pltpu.CompilerParams(dimension_semantics=sem, vmem_limit_bytes=vmem)


def _dot(a, b, dims):
    return lax.dot_general(a, b, dims, preferred_element_type=F32)


class _Comm(NamedTuple):
    ins: tuple
    outs: tuple
    aliases: dict
    copies: Callable
    n_remote: int
    n_local: int


def _call(name, body, grid, in_specs, out_specs, out_shape, scratch, args, sem, comm=None, vmem=VMEM_LIMIT):
    if comm is None:
        return pl.pallas_call(body, grid=grid, in_specs=in_specs, out_specs=out_specs, out_shape=out_shape,
                              scratch_shapes=scratch, compiler_params=_params(*sem, vmem=vmem), name=name)(*args)
    n_in, n_out = len(in_specs), len(out_specs)
    n_ci, n_co = len(comm.ins), len(comm.outs)
    o0 = n_in + n_ci

    def hosted(*refs):
        c_ins, c_outs = refs[n_in:o0], refs[o0 + n_out:o0 + n_out + n_co]
        sems = refs[-3:]

        def start():
            for cp in comm.copies(c_ins, c_outs, *sems):
                cp.start()

        def wait():
            for cp in comm.copies(c_ins, c_outs, *sems):
                cp.wait()

        if not grid:
            start()
            body()
            wait()
            return
        ids = [pl.program_id(ax) for ax in range(len(grid))]
        pl.when(functools.reduce(jnp.logical_and, [i == 0 for i in ids]))(start)
        body(*refs[:n_in], *refs[o0:o0 + n_out], *refs[o0 + n_out + n_co:-3])
        pl.when(functools.reduce(jnp.logical_and, [i == g - 1 for i, g in zip(ids, grid)]))(wait)

    hbm = pl.BlockSpec(memory_space=pltpu.HBM)
    dma = pltpu.SemaphoreType.DMA
    return pl.pallas_call(
        hosted, grid=grid, in_specs=[*in_specs, *[hbm] * n_ci], out_specs=[*out_specs, *[hbm] * n_co],
        out_shape=[*out_shape, *comm.outs], input_output_aliases={n_in + i: n_out + o for i, o in comm.aliases.items()},
        scratch_shapes=[*scratch, dma((comm.n_remote,)), dma((comm.n_remote,)), dma((max(comm.n_local, 1),))],
        compiler_params=_params(*["arbitrary"] * len(grid), vmem=vmem), name=name)(*args, *comm.ins)


def _mm_call(name, grid, a, a_spec, b, b_spec, dims, out_shapes, out_specs, acc_shape, epilogue=None,
             extras=(), extra_specs=(), col_axis=1, comm=None):
    nk = grid[2]
    n_extra = len(extras)
    n_out = len(out_shapes)

    def finish(res, ex, outs, j):
        if epilogue is None:
            outs[0][...] = res.astype(outs[0].dtype)
        else:
            epilogue(res, ex, outs, j)

    def body(*refs):
        a_ref, b_ref = refs[0], refs[1]
        ex = refs[2:2 + n_extra]
        outs = refs[2 + n_extra:2 + n_extra + n_out]
        j, k = pl.program_id(col_axis), pl.program_id(2)
        part = _dot(a_ref[...].astype(BF16), b_ref[...].astype(BF16), dims)
        if nk == 1:
            finish(part, ex, outs, j)
            return
        acc = refs[-1]

        @pl.when(k == 0)
        def _():
            acc[...] = part

        @pl.when((k > 0) & (k < nk - 1))
        def _():
            acc[...] += part

        @pl.when(k == nk - 1)
        def _():
            finish(acc[...] + part, ex, outs, j)

    return _call(name, body, grid, [a_spec, b_spec, *extra_specs], out_specs, out_shapes,
                 [] if nk == 1 else [pltpu.VMEM(acc_shape, F32)], (a, b, *extras), ("parallel", "parallel", "arbitrary"), comm)


def _matmul(name, a, b, mode, out_dtype, tm, tn, tk, resid=None, col0_scale=None, comm=None):
    if mode == "nn":
        (M, K), N = a.shape, b.shape[1]
        a_spec = pl.BlockSpec((tm, tk), lambda j, i, k: (i, k))
        b_spec = pl.BlockSpec((tk, tn), lambda j, i, k: (k, j))
        dims = NN
    elif mode == "nt":
        (M, K), N = a.shape, b.shape[0]
        a_spec = pl.BlockSpec((tm, tk), lambda j, i, k: (i, k))
        b_spec = pl.BlockSpec((tn, tk), lambda j, i, k: (j, k))
        dims = NT
    else:
        (K, M), N = a.shape, b.shape[1]
        a_spec = pl.BlockSpec((tk, tm), lambda j, i, k: (k, i))
        b_spec = pl.BlockSpec((tk, tn), lambda j, i, k: (k, j))
        dims = TN
    assert M % tm == 0 and N % tn == 0 and K % tk == 0, (name, M, N, K, tm, tn, tk)
    o_spec = pl.BlockSpec((tm, tn), lambda j, i, k: (i, j))
    extras, extra_specs, epilogue = (), (), None
    if resid is not None:
        extras, extra_specs = (resid,), (o_spec,)

        def epilogue(acc, ex, outs, j):
            outs[0][...] = (acc + ex[0][...]).astype(outs[0].dtype)

    elif col0_scale is not None:

        def epilogue(acc, ex, outs, j):
            outs[0][...] = (acc * jnp.where(j == 0, col0_scale, 1.0)).astype(outs[0].dtype)

    res = _mm_call(name, (N // tn, M // tm, K // tk), a, a_spec, b, b_spec, dims, [jax.ShapeDtypeStruct((M, N), out_dtype)],
                   [o_spec], (tm, tn), epilogue, extras, extra_specs, col_axis=0, comm=comm)
    return res[0] if comm is None else (res[0], res[1:])


def _rms_fwd(name, h, gain, dils=(1,), comm=None, tm=512):
    S, D = h.shape

    def body(h_ref, g_ref, *rest):
        x = h_ref[...]
        rstd = lax.rsqrt(jnp.mean(x * x, axis=-1, keepdims=True) + RMS_EPS)
        y = x * rstd * g_ref[...]
        _write_views([y[:, b * LANES:(b + 1) * LANES] for b in range(N_PAIRS)], rest[-1], rest[:-1], dils, tm)

    res = _call(name, body, (S // tm,), [pl.BlockSpec((tm, D), lambda i: (i, 0)), pl.BlockSpec((1, D), lambda i: (0, 0))],
                [_view_spec(tm, R) for R in dils], [jax.ShapeDtypeStruct((S // R, R * D), BF16) for R in dils],
                [pltpu.VMEM((N_PAIRS, tm, LANES), F32)], (h, gain.reshape(1, D)), ("parallel",), comm)
    views = res[0] if len(dils) == 1 else res[:len(dils)]
    return views if comm is None else (views, res[len(dils):])


def _rms_bwd(name, dn, h, gain, dres, copy16=True, tm=512):
    S, D = h.shape

    def body(dn_ref, h_ref, g_ref, r_ref, dh_ref, dg_ref, *dh16_ref):
        x = h_ref[...]
        rstd = lax.rsqrt(jnp.mean(x * x, axis=-1, keepdims=True) + RMS_EPS)
        xhat = x * rstd
        d = dn_ref[...]
        dxhat = d * g_ref[...]
        dh = rstd * (dxhat - xhat * jnp.mean(dxhat * xhat, axis=-1, keepdims=True)) + r_ref[...]
        dh_ref[...] = dh
        if copy16:
            dh16_ref[0][...] = dh.astype(BF16)

        @pl.when(pl.program_id(0) == 0)
        def _():
            dg_ref[...] = jnp.zeros_like(dg_ref)

        dg_ref[...] += jnp.sum(d * xhat, axis=0, keepdims=True)

    row = pl.BlockSpec((tm, D), lambda i: (i, 0))
    vec = pl.BlockSpec((1, D), lambda i: (0, 0))
    return pl.pallas_call(
        body, grid=(S // tm,), in_specs=[row, row, vec, row], out_specs=[row, vec] + [row] * copy16,
        out_shape=[jax.ShapeDtypeStruct((S, D), F32), jax.ShapeDtypeStruct((1, D), F32)] + [jax.ShapeDtypeStruct((S, D), BF16)] * copy16,
        compiler_params=_params("arbitrary"), name=name)(dn, h, gain.reshape(1, D), dres)


def _loss_head(name, h, gain, target, tm=512):
    S, D = h.shape

    def body(h_ref, g_ref, t_ref, dh_ref, dg_ref, loss_ref, dh16_ref):
        x = h_ref[...]
        rstd = lax.rsqrt(jnp.mean(x * x, axis=-1, keepdims=True) + RMS_EPS)
        xhat = x * rstd
        err = xhat * g_ref[...] - t_ref[...]
        dy = err * (1.0 / D)
        dxhat = dy * g_ref[...]
        dh = rstd * (dxhat - xhat * jnp.mean(dxhat * xhat, axis=-1, keepdims=True))
        dh_ref[...] = dh
        dh16_ref[...] = dh.astype(BF16)

        @pl.when(pl.program_id(0) == 0)
        def _():
            dg_ref[...] = jnp.zeros_like(dg_ref)
            loss_ref[...] = jnp.zeros_like(loss_ref)

        dg_ref[...] += jnp.sum(dy * xhat, axis=0, keepdims=True)
        part = 0.5 * jnp.sum(jnp.mean(err * err, axis=-1, keepdims=True), axis=0, keepdims=True)
        loss_ref[...] += jnp.broadcast_to(part, loss_ref.shape)

    row = pl.BlockSpec((tm, D), lambda i: (i, 0))
    vec = pl.BlockSpec((1, D), lambda i: (0, 0))
    return pl.pallas_call(
        body, grid=(S // tm,), in_specs=[row, vec, row], out_specs=[row, vec, pl.BlockSpec((1, LANES), lambda i: (0, 0)), row],
        out_shape=[jax.ShapeDtypeStruct((S, D), F32), jax.ShapeDtypeStruct((1, D), F32),
                   jax.ShapeDtypeStruct((1, LANES), F32), jax.ShapeDtypeStruct((S, D), BF16)],
        compiler_params=_params("arbitrary"), name=name)(h, gain.reshape(1, D), target)


def _rope_tables(S):
    half = ROT_DIM // 2
    inv_freq = ROPE_THETA ** (-jnp.arange(half, dtype=F32) * 2.0 / ROT_DIM)
    ang = jnp.arange(S, dtype=F32)[:, None] * inv_freq[None, :]
    cos, sin = jnp.cos(ang), jnp.sin(ang)
    one = jnp.ones((S, HEAD_DIM - ROT_DIM), F32)
    zero = jnp.zeros((S, HEAD_DIM - ROT_DIM), F32)
    zh = jnp.zeros((S, half), F32)
    c = jnp.concatenate([cos, cos, one], axis=1)
    sa = jnp.concatenate([-sin, zh, zero], axis=1)
    sb = jnp.concatenate([zh, sin, zero], axis=1)
    return tuple(jnp.concatenate([t, t], axis=1) for t in (c, sa, sb))


def _rotate(x, c, sa, sb, sign):
    return x * c + sign * (pltpu.roll(x, LANES - ROT_DIM // 2, 1) * sa + pltpu.roll(x, ROT_DIM // 2, 1) * sb)


def _stage_chunks(scr, chunks):
    for c, x in enumerate(chunks):
        scr[c] = x


def _strided_rows(scr, c, r, n, R):
    return scr.at[c][pl.ds(r, n, stride=R), :]


def _a_proj(name, n, w, g, R, tabs, comm, tm=1024):
    S, D = n.shape
    n_i = S // tm
    n_out = 3

    def body(n_ref, w_ref, c_ref, sa_ref, sb_ref, *rest):
        outs, scr = rest[:n_out], rest[n_out]
        j = pl.program_id(0)
        acc = _dot(n_ref[...], w_ref[...], NN)
        c, sa, sb = c_ref[...], sa_ref[...], sb_ref[...]
        for J in range(n_out):
            kind = J

            @pl.when(j == J)
            def _(J=J, kind=kind):
                chunks = [acc[:, b * LANES:(b + 1) * LANES] for b in range(N_PAIRS)]
                if kind < 2:
                    chunks = [_rotate(x, c, sa, sb, 1.0) * (SOFTMAX_SCALE if kind == 0 else 1.0) for x in chunks]
                if R == 1:
                    for b, x in enumerate(chunks):
                        outs[J][:, b * LANES:(b + 1) * LANES] = x.astype(BF16)
                    return
                _stage_chunks(scr, chunks)
                for r in range(R):
                    for b in range(N_PAIRS):
                        col = r * D_MODEL + b * LANES
                        outs[J][:, col:col + LANES] = _strided_rows(scr, b, r, tm // R, R).astype(BF16)

    def out_spec(J):
        return pl.BlockSpec((tm // R, R * D_MODEL), lambda j, i: (jnp.where(j == J, i, jnp.where(j < J, 0, n_i - 1)), 0))

    tab = pl.BlockSpec((tm, LANES), lambda j, i: (i, 0))
    res = _call(name, body, (n_out, n_i),
                [pl.BlockSpec((tm, D), lambda j, i: (i, 0)), pl.BlockSpec((D, D_MODEL), lambda j, i: (0, 3 * g + j)), tab, tab, tab],
                [out_spec(J) for J in range(n_out)], [jax.ShapeDtypeStruct((S // R, R * D_MODEL), BF16)] * n_out,
                [pltpu.VMEM((N_PAIRS, tm, LANES), F32)], (n, w, *tabs), ("arbitrary", "arbitrary"), comm)
    return res[:n_out], res[n_out:]


def _unstride(src_chunk, R, tok, rows):
    for r in range(R):
        for b in range(N_PAIRS):
            tok.at[b][pl.ds(r, rows // R, stride=R), :] = src_chunk(r, b).astype(F32)


def _by_residue(ref, R):
    return ref[...] if R == 1 else jnp.concatenate([ref[:, r * D_MODEL:(r + 1) * D_MODEL] for r in range(R)], axis=0)


def _a_dw(name, n_views, pieces, dils, comm, tk=512):
    D = D_MODEL
    S = n_views[0].shape[0] * dils[0]
    n_k = S // tk
    n_p, n_g = len(pieces), len(dils)

    def body(*refs):
        n_refs, p_refs, o_ref, acc = refs[:n_g], refs[n_g:n_g + n_p], refs[n_g + n_p], refs[n_g + n_p + 1]
        j, k = pl.program_id(0), pl.program_id(1)
        for J in range(n_p):

            @pl.when(j == J)
            def _(J=J):
                R = dils[J // 3]
                part = _dot(_by_residue(n_refs[J // 3], R), _by_residue(p_refs[J], R), TN)

                @pl.when(k == 0)
                def _():
                    acc[...] = part

                @pl.when(k > 0)
                def _():
                    acc[...] += part

        @pl.when(k == n_k - 1)
        def _():
            o_ref[...] = acc[...].astype(BF16)

    def piece_spec(J):
        R = dils[J // 3]
        return pl.BlockSpec((tk // R, R * D_MODEL), lambda j, k: (jnp.where(j == J, k, jnp.where(j < J, 0, n_k - 1)), 0))

    n_specs = [pl.BlockSpec((tk // R, R * D_MODEL), lambda j, k: (k, 0)) for R in dils]
    return _call(name, body, (n_p, n_k), n_specs + [piece_spec(J) for J in range(n_p)],
                 [pl.BlockSpec((D, D_MODEL), lambda j, k: (0, j))], [jax.ShapeDtypeStruct((D, n_p * D_MODEL), BF16)],
                 [pltpu.VMEM((D, D_MODEL), F32)], (*n_views, *pieces), ("arbitrary", "arbitrary"), comm)


def _a_dn(name, pieces, dils, w, comm, tm=512):
    D = w.shape[0]
    S = pieces[0].shape[0] * dils[0]
    n_p = len(pieces)

    def body(*refs):
        p_refs, w_ref, o_ref, acc, part_acc, tok = refs[:n_p], refs[n_p], refs[n_p + 1], refs[n_p + 2], refs[n_p + 3], refs[n_p + 4]
        j = pl.program_id(1)
        for J in range(n_p):

            @pl.when(j == J)
            def _(J=J):
                R, t = dils[J // 3], J % 3
                part = _dot(_by_residue(p_refs[J], R), w_ref[...], NT)
                if R == 1:
                    if J == 0:
                        acc[...] = part
                    else:
                        acc[...] += part
                    return
                if t == 0:
                    part_acc[...] = part
                    return
                if t == 1:
                    part_acc[...] += part
                    return
                n = tm // R
                _unstride(lambda r, b: part_acc[r * n:(r + 1) * n, b * LANES:(b + 1) * LANES]
                          + part[r * n:(r + 1) * n, b * LANES:(b + 1) * LANES], R, tok, tm)
                total = acc[...] + jnp.concatenate([tok[b] for b in range(N_PAIRS)], axis=1)
                if J == n_p - 1:
                    o_ref[...] = total
                else:
                    acc[...] = total

    specs = [pl.BlockSpec((tm // dils[J // 3], dils[J // 3] * D_MODEL), lambda i, j: (i, 0)) for J in range(n_p)]
    return _call(name, body, (S // tm, n_p), specs + [pl.BlockSpec((D, D_MODEL), lambda i, j: (0, j))],
                 [pl.BlockSpec((tm, D), lambda i, j: (i, 0))], [jax.ShapeDtypeStruct((S, D), F32)],
                 [pltpu.VMEM((tm, D), F32), pltpu.VMEM((tm, D), F32), pltpu.VMEM((N_PAIRS, tm, LANES), F32)], (*pieces, w),
                 ("arbitrary", "arbitrary"), comm)


def _lo_lanes():
    return lax.broadcasted_iota(jnp.int32, (1, LANES), 1) < HEAD_DIM


def _rep_rows(x2, lo):
    sw = pltpu.roll(x2, HEAD_DIM, 1)
    return jnp.where(lo, x2, sw), jnp.where(lo, sw, x2)


def _pair_cols(h):
    return slice((h // 2) * LANES, (h // 2 + 1) * LANES)


def _head_lanes(lo, h):
    return lo if h % 2 == 0 else jnp.logical_not(lo)


def _band_masks(t, first):
    ri = lax.broadcasted_iota(jnp.int32, (t, t), 0)
    ci = lax.broadcasted_iota(jnp.int32, (t, t), 1)
    neg_prev = jnp.where((ci >= ri) & jnp.logical_not(first), 0.0, NEG_INF)
    neg_cur = jnp.where(ci <= ri, 0.0, NEG_INF)
    return neg_prev, neg_cur


def _dil_specs(L, R, t, qcol, kcol, vcol):
    W = D_MODEL
    prev = lambda qi: jnp.maximum(qi - 1, 0)
    return dict(
        q=pl.BlockSpec((t, W), lambda r, qi: (qi, qcol(r))),
        kp=pl.BlockSpec((t, W), lambda r, qi: (prev(qi), kcol(r))), kc=pl.BlockSpec((t, W), lambda r, qi: (qi, kcol(r))),
        vp=pl.BlockSpec((t, W), lambda r, qi: (prev(qi), vcol(r))), vc=pl.BlockSpec((t, W), lambda r, qi: (qi, vcol(r))),
        own=pl.BlockSpec((t, W), lambda r, qi: (qi, r)), tab=pl.BlockSpec((t, LANES), lambda r, qi: (qi, r)))


def _dil_fwd(name, x, qcol, kcol, vcol, R, L, comm=None):
    t = BAND_STEPS
    W = D_MODEL
    sp = _dil_specs(L, R, t, qcol, kcol, vcol)

    def body(q_ref, kp_ref, kc_ref, vp_ref, vc_ref, o_ref, lse_ref):
        lo = _lo_lanes()
        neg_p, neg_c = _band_masks(t, pl.program_id(1) == 0)
        s_p, s_c = [], []
        for h in range(N_HEADS):
            cols = _pair_cols(h)
            qh = jnp.where(_head_lanes(lo, h), q_ref[:, cols], 0)
            s_p.append(_dot(qh, kp_ref[:, cols], NT))
            s_c.append(_dot(qh, kc_ref[:, cols], NT))
        s_p = jnp.stack(s_p) + neg_p[None]
        s_c = jnp.stack(s_c) + neg_c[None]
        m = jnp.maximum(jnp.max(s_p, axis=2, keepdims=True), jnp.max(s_c, axis=2, keepdims=True))
        p_p, p_c = jnp.exp(s_p - m), jnp.exp(s_c - m)
        l = jnp.sum(p_p, axis=2, keepdims=True) + jnp.sum(p_c, axis=2, keepdims=True)
        inv, lse = 1.0 / l, m + jnp.log(l)
        p_p, p_c = p_p.astype(BF16), p_c.astype(BF16)
        for p in range(N_PAIRS):
            cols = _pair_cols(2 * p)
            o2 = jnp.zeros((t, LANES), F32)
            for h in (2 * p, 2 * p + 1):
                hm = _head_lanes(lo, h)
                pv = _dot(p_p[h], jnp.where(hm, vp_ref[:, cols], 0), NN) + _dot(p_c[h], jnp.where(hm, vc_ref[:, cols], 0), NN)
                o2 = o2 + pv * inv[h]
            o_ref[:, cols] = o2
            lse_ref[:, cols] = jnp.where(lo, lse[2 * p], lse[2 * p + 1])

    return _call(name, body, (R, L // t), [sp["q"], sp["kp"], sp["kc"], sp["vp"], sp["vc"]], [sp["own"], sp["own"]],
                 [jax.ShapeDtypeStruct((L, R * W), F32), jax.ShapeDtypeStruct((L, R * W), F32)], [],
                 (x[0], x[1], x[1], x[2], x[2]), ("parallel", "parallel"), comm)


def _dil_scores(lo, q_ref, do_ref, o_ref, lse_ref, kv_refs):
    s = [[] for _ in kv_refs]
    dp = [[] for _ in kv_refs]
    lse, d = [], []
    for h in range(N_HEADS):
        cols = _pair_cols(h)
        hm = _head_lanes(lo, h)
        qh, doh = jnp.where(hm, q_ref[:, cols], 0), jnp.where(hm, do_ref[:, cols], 0)
        for i, (k_ref, v_ref) in enumerate(kv_refs):
            s[i].append(_dot(qh, k_ref[:, cols], NT))
            dp[i].append(_dot(doh, v_ref[:, cols], NT))
        lse.append(_rep_rows(lse_ref[:, cols], lo)[h % 2])
        dd = do_ref[:, cols].astype(F32) * o_ref[:, cols].astype(F32)
        d.append(jnp.sum(jnp.where(hm, dd, 0.0), axis=1, keepdims=True))
    return (*[jnp.stack(x) for x in s], *[jnp.stack(x) for x in dp], jnp.stack(lse), jnp.stack(d))


def _dil_bwd(name, x, do, o, lse, tabs, R, L):
    t = BAND_STEPS
    W = D_MODEL
    nq = L // t
    qb = lambda step: nq - 1 - step
    kb = lambda step: jnp.maximum(qb(step) - 1, 0)
    at = lambda f, width: pl.BlockSpec((t, width), lambda r, step: (f(step), r))

    def body(q_ref, kp_ref, kc_ref, vp_ref, vc_ref, do_ref, o_ref, lse_ref, c_ref, sa_ref, sb_ref, dq_ref, dk_ref, dv_ref,
             dk_scr, dv_scr):
        qi = nq - 1 - pl.program_id(1)
        lo = _lo_lanes()
        unrotate = lambda x: _rotate(x, c_ref[...], sa_ref[...], sb_ref[...], -1.0).astype(BF16)

        @pl.when(qi == nq - 1)
        def _():
            dk_scr[...] = jnp.zeros_like(dk_scr)
            dv_scr[...] = jnp.zeros_like(dv_scr)

        neg_p, neg_c = _band_masks(t, qi == 0)
        s_p, s_c, dp_p, dp_c, lse_h, d = _dil_scores(lo, q_ref, do_ref, o_ref, lse_ref, ((kp_ref, vp_ref), (kc_ref, vc_ref)))
        p_p, p_c = jnp.exp(s_p + neg_p[None] - lse_h), jnp.exp(s_c + neg_c[None] - lse_h)
        ds_p, ds_c = (p_p * (dp_p - d)).astype(BF16), (p_c * (dp_c - d)).astype(BF16)
        p_p, p_c = p_p.astype(BF16), p_c.astype(BF16)
        for p in range(N_PAIRS):
            cols = _pair_cols(2 * p)
            dq2 = jnp.zeros((t, LANES), F32)
            dk_cur, dv_cur = dk_scr[:, cols], dv_scr[:, cols]
            dk_prev, dv_prev = jnp.zeros((t, LANES), F32), jnp.zeros((t, LANES), F32)
            for h in (2 * p, 2 * p + 1):
                hm = _head_lanes(lo, h)
                qh, doh = jnp.where(hm, q_ref[:, cols], 0), jnp.where(hm, do_ref[:, cols], 0)
                dq2 = dq2 + _dot(ds_p[h], jnp.where(hm, kp_ref[:, cols], 0), NN) + _dot(ds_c[h], jnp.where(hm, kc_ref[:, cols], 0), NN)
                dk_prev, dv_prev = dk_prev + _dot(ds_p[h], qh, TN), dv_prev + _dot(p_p[h], doh, TN)
                dk_cur, dv_cur = dk_cur + _dot(ds_c[h], qh, TN), dv_cur + _dot(p_c[h], doh, TN)
            dq_ref[:, cols] = unrotate(dq2 * SOFTMAX_SCALE)
            dk_ref[:, cols] = unrotate(dk_cur)
            dv_ref[:, cols] = dv_cur.astype(BF16)
            dk_scr[:, cols] = dk_prev
            dv_scr[:, cols] = dv_prev

    wide = jax.ShapeDtypeStruct((L, R * W), BF16)
    return pl.pallas_call(
        body, grid=(R, nq),
        in_specs=[at(qb, W), at(kb, W), at(qb, W), at(kb, W), at(qb, W), at(qb, W), at(qb, W), at(qb, W),
                  at(qb, LANES), at(qb, LANES), at(qb, LANES)],
        out_specs=[at(qb, W), at(qb, W), at(qb, W)], out_shape=[wide, wide, wide],
        scratch_shapes=[pltpu.VMEM((t, W), F32), pltpu.VMEM((t, W), F32)],
        compiler_params=_params("parallel", "arbitrary"), name=name)(x[0], x[1], x[1], x[2], x[2], do, o, lse, *tabs)


def _fox_operands(q2, k2, kb2, lo, hh):
    lane = lax.broadcasted_iota(jnp.int32, (1, LANES), 1)
    if hh == 0:
        ones = ((lane >= HEAD_DIM) & (lane < HEAD_DIM + 3)).astype(BF16)
        return jnp.where(lo, q2, ones), jnp.where(lo, k2, kb2)
    ones = (lane < 3).astype(BF16)
    return jnp.where(lo, ones, q2), jnp.where(lo, kb2, k2)


def _causal_neg(t):
    ri = lax.broadcasted_iota(jnp.int32, (t, t), 0)
    ci = lax.broadcasted_iota(jnp.int32, (t, t), 1)
    return jnp.where(ci <= ri, 0.0, NEG_INF)


def _fox_fwd(name, qkv, kbias, t, comm=None):
    S = qkv.shape[0]
    W = D_MODEL
    nq = S // t
    rep = t // LANES

    def body(q_ref, k_ref, v_ref, kb_ref, o_ref, lse_ref, m_scr, l_scr, acc_scr):
        qi, j = pl.program_id(0), pl.program_id(1)
        lo = _lo_lanes()

        @pl.when(j == 0)
        def _():
            m_scr[...] = jnp.full_like(m_scr, NEG_INF)
            l_scr[...] = jnp.zeros_like(l_scr)
            acc_scr[...] = jnp.zeros_like(acc_scr)

        def step(masked):
            neg = _causal_neg(t) if masked else None

            def pair(p, carry):
                cs = pl.ds(pl.multiple_of(p * LANES, LANES), LANES)
                q2, k2, v2, kb2 = q_ref[:, cs], k_ref[:, cs], v_ref[:, cs], kb_ref[:, cs]
                pvs, alphas = [], []
                for hh in range(2):
                    hm = lo if hh == 0 else jnp.logical_not(lo)
                    qh, kh = _fox_operands(q2, k2, kb2, lo, hh)
                    s = _dot(qh, kh, NT)
                    if masked:
                        s = s + neg
                    h = 2 * p + hh
                    m_prev = m_scr[h]
                    m_new = jnp.maximum(m_prev, jnp.max(s, axis=1, keepdims=True))
                    pe = jnp.exp(s - jnp.tile(m_new, (1, rep)))
                    alpha = jnp.exp(m_prev - m_new)
                    l_scr[h] = alpha * l_scr[h] + jnp.sum(pe, axis=1, keepdims=True)
                    m_scr[h] = m_new
                    pvs.append(_dot(pe.astype(BF16), jnp.where(hm, v2, 0), NN))
                    alphas.append(alpha)
                acc_scr[:, cs] = acc_scr[:, cs] * jnp.where(lo, alphas[0], alphas[1]) + pvs[0] + pvs[1]
                return carry

            lax.fori_loop(0, N_PAIRS, pair, 0, unroll=4)

        @pl.when(j < qi)
        def _():
            step(False)

        @pl.when(j == qi)
        def _():
            step(True)

        @pl.when(j == nq - 1)
        def _():
            for p in range(N_PAIRS):
                cols = slice(p * LANES, (p + 1) * LANES)
                l2 = jnp.where(lo, l_scr[2 * p], l_scr[2 * p + 1])
                m2 = jnp.where(lo, m_scr[2 * p], m_scr[2 * p + 1])
                o_ref[:, cols] = (acc_scr[:, cols] / l2).astype(BF16)
                lse_ref[:, cols] = m2 + jnp.log(l2)

    kv = lambda col: pl.BlockSpec((t, W), lambda qi, j: (jnp.minimum(j, qi), col))
    own = pl.BlockSpec((t, W), lambda qi, j: (qi, 0))
    return _call(name, body, (nq, nq), [own, kv(1), kv(2), kv(0)], [own, own],
                 [jax.ShapeDtypeStruct((S, W), BF16), jax.ShapeDtypeStruct((S, W), F32)],
                 [pltpu.VMEM((N_HEADS, t, LANES), F32), pltpu.VMEM((N_HEADS, t, LANES), F32), pltpu.VMEM((t, W), F32)],
                 (qkv, qkv, qkv, kbias), ("parallel", "arbitrary"), comm)


def _fox_head_grads(qh, kh, v2, doh, neg, lse_h, d_h, rep):
    s = _dot(qh, kh, NT)
    if neg is not None:
        s = s + neg
    p = jnp.exp(s - jnp.tile(lse_h, (1, rep)))
    return p, p * (_dot(doh, v2, NT) - d_h)


def _fox_bwd(name, qkv, kbias, do, o, lse, t, comm=None):
    S = qkv.shape[0]
    W = D_MODEL
    nq = S // t
    rep = t // LANES

    def body(q_ref, k_ref, v_ref, kb_ref, do_ref, o_ref, lse_ref, dq_ref, dk_ref, dv_ref, rs_ref, dc_ref, dq_scr, dk_scr, dv_scr):
        kb, j = pl.program_id(0), pl.program_id(1)
        lo = _lo_lanes()
        lane = lax.broadcasted_iota(jnp.int32, (1, LANES), 1)
        rows = pl.ds(pl.multiple_of(j * t, t), t)

        @pl.when((kb == 0) & (j == 0))
        def _():
            dq_scr[...] = jnp.zeros_like(dq_scr)
            rs_ref[...] = jnp.zeros_like(rs_ref)

        @pl.when(j == 0)
        def _():
            dk_scr[...] = jnp.zeros_like(dk_scr)
            dv_scr[...] = jnp.zeros_like(dv_scr)
            dc_ref[...] = jnp.zeros_like(dc_ref)

        def step(masked):
            neg = _causal_neg(t) if masked else None

            def pair(p, carry):
                cs = pl.ds(pl.multiple_of(p * LANES, LANES), LANES)
                q2, k2, v2, kb2, do2 = q_ref[:, cs], k_ref[:, cs], v_ref[:, cs], kb_ref[:, cs], do_ref[:, cs]
                dd = do2.astype(F32) * o_ref[:, cs].astype(F32)
                lse_h = _rep_rows(lse_ref[:, cs], lo)
                dq2 = jnp.zeros((t, LANES), F32)
                dv2 = jnp.zeros((t, LANES), F32)
                dk2 = jnp.zeros((t, LANES), F32)
                for hh in range(2):
                    hm = lo if hh == 0 else jnp.logical_not(lo)
                    qh, kh = _fox_operands(q2, k2, kb2, lo, hh)
                    doh = jnp.where(hm, do2, 0)
                    d_h = jnp.sum(jnp.where(hm, dd, 0.0), axis=1, keepdims=True)
                    pr, ds = _fox_head_grads(qh, kh, v2, doh, neg, lse_h[hh], d_h, rep)
                    rs_ref[rows, :] += jnp.where(lane == 2 * p + hh, jnp.sum(ds, axis=1, keepdims=True), 0.0)
                    dc_ref[p, hh:hh + 1, :] += jnp.sum(ds, axis=0, keepdims=True)
                    dsb = ds.astype(BF16)
                    dv2 = dv2 + _dot(pr.astype(BF16), doh, TN)
                    dk2 = dk2 + _dot(dsb, jnp.where(hm, q2, 0), TN)
                    dq2 = dq2 + _dot(dsb, jnp.where(hm, k2, 0), NN)
                dv_scr[:, cs] += dv2
                dk_scr[:, cs] += dk2
                dq_scr[rows, cs] += dq2
                return carry

            lax.fori_loop(0, N_PAIRS, pair, 0, unroll=4)
            if masked:
                dq_ref[...] = (dq_scr[rows, :] * SOFTMAX_SCALE).astype(BF16)

        @pl.when(j > kb)
        def _():
            step(False)

        @pl.when(j == kb)
        def _():
            step(True)

        @pl.when(j == nq - 1)
        def _():
            dv_ref[...] = dv_scr[...].astype(BF16)
            dk_ref[...] = dk_scr[...].astype(BF16)

    qrow = pl.BlockSpec((t, W), lambda kb, j: (jnp.maximum(j, kb), 0))
    krow = lambda col: pl.BlockSpec((t, W), lambda kb, j: (kb, col))
    own = pl.BlockSpec((t, W), lambda kb, j: (kb, 0))
    wide = jax.ShapeDtypeStruct((S, W), BF16)
    return _call(name, body, (nq, nq), [qrow, krow(1), krow(2), krow(0), qrow, qrow, qrow],
                 [own, own, own, pl.BlockSpec((S, LANES), lambda kb, j: (0, 0)), pl.BlockSpec((N_PAIRS, 2, t), lambda kb, j: (0, 0, kb))],
                 [wide, wide, wide, jax.ShapeDtypeStruct((S, LANES), F32), jax.ShapeDtypeStruct((N_PAIRS, 2, S), F32)],
                 [pltpu.VMEM((S, W), F32), pltpu.VMEM((t, W), F32), pltpu.VMEM((t, W), F32)],
                 (qkv, qkv, qkv, kbias, do, o, lse), ("arbitrary", "arbitrary"), comm, vmem=FOX_BWD_VMEM)


def _view_spec(tm, R, index=lambda i: (i, 0)):
    return pl.BlockSpec((tm // R, R * D_MODEL), index)


def _matmul_nt_views(name, a, w, dils, tm=512):
    S, K = a.shape

    def body(a_ref, w_ref, *rest):
        res = _dot(a_ref[...].astype(BF16), w_ref[...], NT)
        _write_views([res[:, b * LANES:(b + 1) * LANES] for b in range(N_PAIRS)], rest[-1], rest[:-1], dils, tm)

    return pl.pallas_call(
        body, grid=(S // tm,), in_specs=[pl.BlockSpec((tm, K), lambda i: (i, 0)), pl.BlockSpec((D_MODEL, K), lambda i: (0, 0))],
        out_specs=[_view_spec(tm, R) for R in dils],
        out_shape=[jax.ShapeDtypeStruct((S // R, R * D_MODEL), BF16) for R in dils],
        scratch_shapes=[pltpu.VMEM((N_PAIRS, tm, LANES), F32)], compiler_params=_params("parallel"), name=name)(a, w)


def _write_views(chunks, scr, out_refs, dils, tm):
    if any(R > 1 for R in dils):
        _stage_chunks(scr, chunks)
    for ref, R in zip(out_refs, dils):
        for b, x in enumerate(chunks):
            if R == 1:
                ref[:, b * LANES:(b + 1) * LANES] = x.astype(ref.dtype)
                continue
            for r in range(R):
                col = r * D_MODEL + b * LANES
                ref[:, col:col + LANES] = _strided_rows(scr, b, r, tm // R, R).astype(ref.dtype)


def _combine(name, os_, lses, dils, tm=256):
    S = os_[0].shape[0] * dils[0]
    G = len(dils)

    def body(*refs):
        o_refs, l_refs = refs[:G], refs[G:2 * G]
        o_outs, l_outs = refs[2 * G:3 * G], refs[3 * G:4 * G]
        stage = refs[4 * G:]
        for g, R in enumerate(dils):
            if R == 1:
                continue
            for src, dst in ((o_refs[g], stage[2 * g]), (l_refs[g], stage[2 * g + 1])):
                _unstride(lambda r, b, src=src: src[:, r * D_MODEL + b * LANES:r * D_MODEL + (b + 1) * LANES], R, dst, tm)
        o_chunks, l_chunks = [], []
        for b in range(N_PAIRS):
            cols = slice(b * LANES, (b + 1) * LANES)
            os_b = [o_refs[g][:, cols] if R == 1 else stage[2 * g][b] for g, R in enumerate(dils)]
            ls = [l_refs[g][:, cols] if R == 1 else stage[2 * g + 1][b] for g, R in enumerate(dils)]
            m = functools.reduce(jnp.maximum, ls)
            ws = [jnp.exp(l - m) for l in ls]
            den = functools.reduce(jnp.add, ws)
            o_chunks.append(functools.reduce(jnp.add, [w * o for w, o in zip(ws, os_b)]) / den)
            l_chunks.append(m + jnp.log(den))
        _write_views(o_chunks, stage[0], o_outs, dils, tm)
        _write_views(l_chunks, stage[1], l_outs, dils, tm)

    specs = [_view_spec(tm, R) for R in dils]
    shapes = lambda dt: [jax.ShapeDtypeStruct((S // R, R * D_MODEL), dt) for R in dils]
    res = pl.pallas_call(
        body, grid=(S // tm,), in_specs=specs * 2, out_specs=specs * 2, out_shape=shapes(BF16) + shapes(F32),
        scratch_shapes=[pltpu.VMEM((N_PAIRS, tm, LANES), F32)] * (2 * G), compiler_params=_params("parallel"),
        name=name)(*os_, *lses)
    return res[:G], res[G:]


def _tri_matmul(tri, x):
    hi, mid, lo = _split3(x)
    return _dot(tri, hi, NN) + _dot(tri, mid, NN) + _dot(tri, lo, NN)


def _split3(x):
    hi = x.astype(BF16)
    r1 = x - hi.astype(F32)
    mid = r1.astype(BF16)
    return hi, mid, (r1 - mid.astype(F32)).astype(BF16)


def _gate_fwd(name, z, bf, tb=512):
    S = z.shape[0]

    def body(z_ref, b_ref, kb_ref, carry):
        @pl.when(pl.program_id(0) == 0)
        def _():
            carry[...] = jnp.zeros_like(carry)

        lf = jax.nn.log_sigmoid(z_ref[...] + b_ref[...])
        ri = lax.broadcasted_iota(jnp.int32, (tb, tb), 0)
        ci = lax.broadcasted_iota(jnp.int32, (tb, tb), 1)
        tri = (ci <= ri).astype(BF16)
        c = _tri_matmul(tri, lf) + carry[...]
        carry[...] = c[tb - 1:tb, :]
        head = lax.broadcasted_iota(jnp.int32, (LANES, D_MODEL), 0)
        col = lax.broadcasted_iota(jnp.int32, (LANES, D_MODEL), 1)
        base = (head >> 1) * LANES + jnp.where((head & 1) == 0, HEAD_DIM, 0)
        kb = jnp.zeros((tb, D_MODEL), F32)
        for i, piece in enumerate(_split3(-c)):
            place = ((col == base + i) & (head < N_HEADS)).astype(BF16)
            kb = kb + _dot(piece, place, NN)
        kb_ref[...] = kb.astype(BF16)

    row = pl.BlockSpec((tb, LANES), lambda i: (i, 0))
    return pl.pallas_call(
        body, grid=(S // tb,), in_specs=[row, pl.BlockSpec((1, LANES), lambda i: (0, 0))],
        out_specs=pl.BlockSpec((tb, D_MODEL), lambda i: (i, 0)), out_shape=jax.ShapeDtypeStruct((S, D_MODEL), BF16),
        scratch_shapes=[pltpu.VMEM((1, LANES), F32)], compiler_params=_params("arbitrary"), name=name)(z, bf)


def _gate_bwd(name, dc, z, bf, tb=512):
    S = z.shape[0]
    nb = S // tb

    def body(dc_ref, z_ref, b_ref, dz_ref, db_ref, carry):
        @pl.when(pl.program_id(0) == 0)
        def _():
            carry[...] = jnp.zeros_like(carry)
            db_ref[...] = jnp.zeros_like(db_ref)

        ri = lax.broadcasted_iota(jnp.int32, (tb, tb), 0)
        ci = lax.broadcasted_iota(jnp.int32, (tb, tb), 1)
        tri = (ci >= ri).astype(BF16)
        dlf = _tri_matmul(tri, dc_ref[...]) + carry[...]
        carry[...] = dlf[0:1, :]
        dz = dlf * jax.nn.sigmoid(-(z_ref[...] + b_ref[...]))
        dz_ref[...] = dz
        db_ref[...] += jnp.sum(dz, axis=0, keepdims=True)

    row = pl.BlockSpec((tb, LANES), lambda i: (nb - 1 - i, 0))
    vec = pl.BlockSpec((1, LANES), lambda i: (0, 0))
    return pl.pallas_call(
        body, grid=(nb,), in_specs=[row, row, vec], out_specs=[row, vec],
        out_shape=[jax.ShapeDtypeStruct((S, LANES), F32), jax.ShapeDtypeStruct((1, LANES), F32)],
        scratch_shapes=[pltpu.VMEM((1, LANES), F32)], compiler_params=_params("arbitrary"), name=name)(dc, z, bf)


def _ffn_gu(name, n, wgu, comm=None, tm=1024):
    S, D = n.shape
    nb = N_DEV // 2

    def body(n_ref, wg_ref, wu_ref, gu_ref, act_ref):
        x = n_ref[...]
        g = _dot(x, wg_ref[...], NN)
        u = _dot(x, wu_ref[...], NN)
        gu_ref[0] = g.astype(BF16)
        gu_ref[1] = u.astype(BF16)
        act_ref[...] = (g * jax.nn.sigmoid(g) * u).astype(BF16)

    return _call(
        name, body, (nb, S // tm),
        [pl.BlockSpec((tm, D), lambda j, i: (i, 0)), pl.BlockSpec((None, D, FF_BLK), lambda j, i: (j, 0, 0)),
         pl.BlockSpec((None, D, FF_BLK), lambda j, i: (j + nb, 0, 0))],
        [pl.BlockSpec((2, None, tm, FF_BLK), lambda j, i: (0, j, i, 0)), pl.BlockSpec((None, tm, FF_BLK), lambda j, i: (j, i, 0))],
        [jax.ShapeDtypeStruct((2, nb, S, FF_BLK), BF16), jax.ShapeDtypeStruct((nb, S, FF_BLK), BF16)], [],
        (n, wgu, wgu), ("parallel", "parallel"), comm)


def _ffn_down(name, act, wd, resid, comm=None, tm=1024):
    nb, S, _ = act.shape
    D = wd.shape[1]

    def epilogue(acc, ex, outs, j):
        outs[0][...] = acc + ex[0][...]

    o_spec = pl.BlockSpec((tm, D), lambda i, j, k: (i, 0))
    return _mm_call(name, (S // tm, 1, nb), act, pl.BlockSpec((None, tm, FF_BLK), lambda i, j, k: (k, i, 0)),
                    wd, pl.BlockSpec((FF_BLK, D), lambda i, j, k: (k, 0)), NN,
                    [jax.ShapeDtypeStruct((S, D), F32)], [o_spec], (tm, D), epilogue, (resid,), (o_spec,), comm=comm)


def _ffn_dact(name, dh, wd, gu, comm=None, tm=512):
    S, D = dh.shape
    nb = N_DEV // 2

    def epilogue(acc, ex, outs, j):
        g = ex[0][0].astype(F32)
        u = ex[0][1].astype(F32)
        sig = jax.nn.sigmoid(g)
        outs[0][0] = (acc * u * (sig * (1.0 + g * (1.0 - sig)))).astype(BF16)
        outs[0][1] = (acc * (g * sig)).astype(BF16)

    gu_spec = pl.BlockSpec((2, None, tm, FF_BLK), lambda j, i, k: (0, j, i, 0))
    return _mm_call(name, (nb, S // tm, 1), dh, pl.BlockSpec((tm, D), lambda j, i, k: (i, 0)),
                    wd, pl.BlockSpec((FF_BLK, D), lambda j, i, k: (j, 0)), NT,
                    [jax.ShapeDtypeStruct((2, nb, S, FF_BLK), BF16)], [gu_spec], (tm, FF_BLK), epilogue, (gu,), (gu_spec,),
                    col_axis=0, comm=comm)


def _ffn_dwgu(name, n, dgu, comm=None, tm=1024, tk=1024):
    S, D = n.shape
    dgu8 = dgu.reshape(N_DEV, S, FF_BLK)
    return _mm_call(name, (N_DEV, D // tm, S // tk), n, pl.BlockSpec((tk, tm), lambda d, i, k: (k, i)),
                    dgu8, pl.BlockSpec((None, tk, FF_BLK), lambda d, i, k: (d, k, 0)), TN,
                    [jax.ShapeDtypeStruct((N_DEV, D, FF_BLK), BF16)],
                    [pl.BlockSpec((None, tm, FF_BLK), lambda d, i, k: (d, i, 0))], (tm, FF_BLK), comm=comm)


def _ffn_dwd(name, act, dh, tk=1024):
    nb, S, _ = act.shape
    D = dh.shape[1]
    out = _mm_call(name, (nb, 1, S // tk), act, pl.BlockSpec((None, tk, FF_BLK), lambda b, j, k: (b, k, 0)),
                   dh, pl.BlockSpec((tk, D), lambda b, j, k: (k, 0)), TN,
                   [jax.ShapeDtypeStruct((nb, FF_BLK, D), BF16)],
                   [pl.BlockSpec((None, FF_BLK, D), lambda b, j, k: (b, 0, 0))], (FF_BLK, D))[0]
    return out.reshape(N_DEV, FF_BLK // 2, D)


def _ffn_dn(name, dgu, wgu, comm=None, tm=1024):
    S = dgu.shape[2]
    D = wgu.shape[1]
    dgu8 = dgu.reshape(N_DEV, S, FF_BLK)
    return _mm_call(name, (S // tm, 1, N_DEV), dgu8, pl.BlockSpec((None, tm, FF_BLK), lambda i, j, k: (k, i, 0)),
                    wgu, pl.BlockSpec((None, D, FF_BLK), lambda i, j, k: (k, 0, 0)), NT,
                    [jax.ShapeDtypeStruct((S, D), F32)], [pl.BlockSpec((tm, D), lambda i, j, k: (i, 0))], (tm, D), comm=comm)


def _adamw(name, parts, w, m, v, tr):
    rows, cols = w.shape
    n_parts = len(parts)
    c1 = 1.0 - ADAM_B1 ** ADAM_STEP
    c2 = 1.0 - ADAM_B2 ** ADAM_STEP

    def body(*refs):
        p_refs = refs[:n_parts]
        w_ref, m_ref, v_ref, g_ref, d_ref, nm_ref, nv_ref = refs[n_parts:]
        g = p_refs[0][...].astype(F32)
        for r in p_refs[1:]:
            g = g + r[...].astype(F32)
        mm = ADAM_B1 * m_ref[...] + (1.0 - ADAM_B1) * g
        vv = ADAM_B2 * v_ref[...] + (1.0 - ADAM_B2) * (g * g)
        g_ref[...] = g
        nm_ref[...] = mm
        nv_ref[...] = vv
        d_ref[...] = -ADAM_LR * ((mm / c1) / (jnp.sqrt(vv / c2) + ADAM_EPS) + ADAM_WD * w_ref[...])

    blk = pl.BlockSpec((tr, cols), lambda i: (i, 0))
    out = jax.ShapeDtypeStruct((rows, cols), F32)
    return pl.pallas_call(
        body, grid=(rows // tr,), in_specs=[blk] * (n_parts + 3), out_specs=[blk] * 4, out_shape=[out] * 4,
        compiler_params=_params("parallel"), name=name)(*parts, w, m, v)


def _position():
    return lax.axis_index("x"), lax.axis_index("y"), lax.axis_index("c")


def _other_chips():
    x, y, _ = _position()
    return [(1 - x, y), (x, 1 - y), (1 - x, 1 - y)]


def _remote(src, dst, send, recv, k, to):
    return pltpu.make_async_remote_copy(src_ref=src, dst_ref=dst, send_sem=send.at[k], recv_sem=recv.at[k],
                                        device_id=to, device_id_type=MESH)


def _ag_send(blocks, direct=False):
    n_peer = 7 if direct else 4

    def copies(ins, outs, send, recv, local, r0=0, l0=0):
        x, y, c = _position()
        me = 4 * x + 2 * y + c
        peers = [(x, y, 1 - c)] + [(px, py, c) for px, py in _other_chips()]
        if direct:
            peers += [(px, py, 1 - c) for px, py in _other_chips()]
        cps = []
        for t, (src, dst) in enumerate(zip(ins, outs)):
            cps.append(pltpu.make_async_copy(src, dst.at[me], local.at[l0 + t]))
            cps += [_remote(src, dst.at[me], send, recv, r0 + n_peer * t + k, to) for k, to in enumerate(peers)]
        return cps

    outs = tuple(jax.ShapeDtypeStruct((N_DEV,) + b.shape, b.dtype) for b in blocks)
    return _Comm(tuple(blocks), outs, {}, copies, n_peer * len(blocks), len(blocks))


def _ag_forward(bufs):
    def copies(ins, outs, send, recv, local, r0=0, l0=0):
        x, y, c = _position()
        cps = []
        for t, buf in enumerate(outs):
            for k, (px, py) in enumerate(_other_chips()):
                slot = buf.at[4 * px + 2 * py + c]
                cps.append(_remote(slot, slot, send, recv, r0 + 3 * t + k, (x, y, 1 - c)))
        return cps

    outs = tuple(jax.ShapeDtypeStruct(b.shape, b.dtype) for b in bufs)
    return _Comm(tuple(bufs), outs, {t: t for t in range(len(bufs))}, copies, 3 * len(bufs), 0)


def _rs_swap(shares):
    def copies(ins, outs, send, recv, local, r0=0, l0=0):
        x, y, c = _position()
        return [_remote(src.at[:, 1 - c], dst, send, recv, r0 + t, (x, y, 1 - c)) for t, (src, dst) in enumerate(zip(ins, outs))]

    ins = tuple(s.reshape((4, 2) + s.shape[1:]) for s in shares)
    outs = tuple(jax.ShapeDtypeStruct((4,) + s.shape[1:], s.dtype) for s in shares)
    return _Comm(ins, outs, {}, copies, len(shares), 0)


def _rs_exchange(sums):
    def copies(ins, outs, send, recv, local, r0=0, l0=0):
        _, _, c = _position()
        return [_remote(src.at[2 * px + py], dst.at[k], send, recv, r0 + 3 * t + k, (px, py, c))
                for t, (src, dst) in enumerate(zip(ins, outs)) for k, (px, py) in enumerate(_other_chips())]

    outs = tuple(jax.ShapeDtypeStruct((3,) + s.shape[1:], s.dtype) for s in sums)
    return _Comm(tuple(sums), outs, {}, copies, 3 * len(sums), 0)


def _comm_call(name, comm):
    return _call(name, lambda: None, (), [], [], [], [], (), (), comm)


def _pair_sum(name, share, got, core, tr):
    _, rows, cols = share.shape

    def body(c_ref, a_ref, b_ref, o_ref):
        o_ref[...] = (a_ref[...].astype(F32) + b_ref[...].astype(F32)).astype(o_ref.dtype)

    grid_spec = pltpu.PrefetchScalarGridSpec(
        num_scalar_prefetch=1, grid=(4, rows // tr),
        in_specs=[pl.BlockSpec((None, None, tr, cols), lambda q, i, c: (q, c[0], i, 0)),
                  pl.BlockSpec((None, tr, cols), lambda q, i, c: (q, i, 0))],
        out_specs=pl.BlockSpec((None, tr, cols), lambda q, i, c: (q, i, 0)))
    return pl.pallas_call(
        body, grid_spec=grid_spec, out_shape=jax.ShapeDtypeStruct((4, rows, cols), share.dtype),
        compiler_params=_params("parallel", "parallel"), name=name)(core, share.reshape(4, 2, rows, cols), got)


TENSORS = ("a_w_in", "a_w_out", "b_w_in", "b_w_out", "gu0", "gu1", "dn0", "dn1")
ROW_TILE = {"a_w_in": 256, "a_w_out": 128, "b_w_in": 256, "b_w_out": 128, "gu0": 256, "gu1": 256, "dn0": 176, "dn1": 176}
A_BLK = 9 * D_MODEL // N_DEV
B_BLK = 386
B_IN = 3 * D_MODEL + N_HEADS
B_IN_PAD = 3 * D_MODEL + LANES


def kernel(x, a_norm, a_w_in, a_w_out, b_norm, b_w_in, b_f, b_w_out, ffn_norm, ffn_w_gu, ffn_w_down, final_norm, loss_target, m_a_norm, m_a_w_in, m_a_w_out, m_b_norm, m_b_w_in, m_b_f, m_b_w_out, m_ffn_norm, m_ffn_w_gu, m_ffn_w_down, m_final_norm, v_a_norm, v_a_w_in, v_a_w_out, v_b_norm, v_b_w_in, v_b_f, v_b_w_out, v_ffn_norm, v_ffn_w_gu, v_ffn_w_down, v_final_norm):
    S = x.shape[1]
    xi, yi, ci = _position()
    dev = 4 * xi + 2 * yi + ci
    core = ci.reshape(1).astype(jnp.int32)
    h0, target = x.reshape(S, D_MODEL), loss_target.reshape(S, D_MODEL)

    def shards(a_in, a_out, b_in, b_out, gu, dn):
        return {"a_w_in": a_in[0], "a_w_out": a_out[0], "b_w_in": b_in[0], "b_w_out": b_out[0],
                "gu0": gu[0], "gu1": gu[1], "dn0": dn[0], "dn1": dn[1]}

    w_sh = shards(a_w_in, a_w_out, b_w_in, b_w_out, ffn_w_gu, ffn_w_down)
    m_sh = shards(m_a_w_in, m_a_w_out, m_b_w_in, m_b_w_out, m_ffn_w_gu, m_ffn_w_down)
    v_sh = shards(v_a_w_in, v_a_w_out, v_b_w_in, v_b_w_out, v_ffn_w_gu, v_ffn_w_down)
    wb = {n: w_sh[n].astype(BF16) for n in TENSORS}
    bf_pad = jnp.pad(b_f, ((0, 0), (0, LANES - N_HEADS)))
    tabs = _rope_tables(S)

    g_ain, g_aout = _comm_call("gather_a", _ag_send([wb["a_w_in"], wb["a_w_out"]]))
    dils = [dil for _, dil in DILATED_PATTERNS]
    n0_views, (g_ain, g_aout) = _rms_fwd("rms_a", h0, a_norm[0], dils, _ag_forward([g_ain, g_aout]))
    n0 = n0_views[0]
    w_a_in = g_ain.transpose(1, 0, 2).reshape(D_MODEL, 9 * D_MODEL)
    sends = [[wb["gu0"]], [wb["dn0"], jnp.pad(b_norm, ((0, 7), (0, 0)))], None]
    qkv_a, later = [], []
    for g, dil in enumerate(dils):
        qkv_g, sent = _a_proj("proj_a%d" % g, n0, w_a_in, g, dil, tabs, None if sends[g] is None else _ag_send(sends[g]))
        qkv_a.append(qkv_g)
        later += sent
    cols = [lambda r: r] * 3
    groups = [(g, dil, S // dil, qkv_a[g]) for g, (window, dil) in enumerate(DILATED_PATTERNS)]
    fwd = [_dil_fwd("dil_fwd%d" % g, view, *cols, dil, L, _ag_send([wb["b_w_in"], wb["b_w_out"]]) if g == 0 else None)
           for g, dil, L, view in groups]
    later = list(fwd[0][2:]) + later
    o_views, lse_views = _combine("dil_combine", [f[0] for f in fwd], [f[1] for f in fwd], dils)
    o_a = o_views[0]
    w_a_out = g_aout.reshape(D_MODEL, D_MODEL)
    h1, (g_bin, g_bout, g_gu0, g_dn0, g_bnorm) = _matmul("out_a", o_a, w_a_out, "nn", F32, TM, 1024, 1024, resid=h0,
                                                         comm=_ag_forward(later))

    n1 = _rms_fwd("rms_f0", h1, ffn_norm[0])
    gu0, act0 = _ffn_gu("gu_f0", n1, g_gu0)
    w_dn0 = g_dn0.reshape(D_FF, D_MODEL)
    h2 = _ffn_down("down_f0", act0, w_dn0, h1)[0]

    b_norm_full = g_bnorm[:, 0].reshape(D_MODEL)
    w_b_in = g_bin.transpose(1, 0, 2).reshape(D_MODEL, B_IN)
    w_b_gate = jnp.pad(w_b_in[:, 3 * D_MODEL:], ((0, 0), (0, LANES - N_HEADS)))
    w_b_cat = jnp.concatenate([w_b_in[:, :3 * D_MODEL], w_b_gate], axis=1)
    w_b_out = g_bout.reshape(D_MODEL, D_MODEL)
    n2 = _rms_fwd("rms_b", h2, b_norm_full)
    qkv = _matmul("proj_b", n2, w_b_in[:, :3 * D_MODEL], "nn", BF16, TM, 1024, 1024, col0_scale=SOFTMAX_SCALE)
    z = _matmul("gate_b", n2, w_b_gate, "nn", F32, TM, LANES, 1024)
    kbias = _gate_fwd("gate_cumsum", z, bf_pad)
    tf = min(S, 512)
    o_b, lse_b, g_gu1, g_dn1 = _fox_fwd("fox_fwd", qkv, kbias, tf, _ag_send([wb["gu1"], wb["dn1"]]))
    h3, (g_gu1, g_dn1) = _matmul("out_b", o_b, w_b_out, "nn", F32, TM, 1024, 1024, resid=h2, comm=_ag_forward([g_gu1, g_dn1]))

    w_dn1 = g_dn1.reshape(D_FF, D_MODEL)
    n3 = _rms_fwd("rms_f1", h3, ffn_norm[1])
    gu1, act1 = _ffn_gu("gu_f1", n3, g_gu1)
    h4 = _ffn_down("down_f1", act1, w_dn1, h3)[0]

    dh4, d_final, loss, dh4_16 = _loss_head("loss_head", h4, final_norm, target)

    share, got, sums, others = {}, {}, {}, {}

    def pair_sums(*names):
        for n in names:
            sums[n] = _pair_sum("pair_" + n, share[n], got[n], core, ROW_TILE[n])

    dgu1 = _ffn_dact("dact_f1", dh4_16, w_dn1, gu1)[0]
    share["dn1"] = _ffn_dwd("dwd_f1", act1, dh4_16)
    share["gu1"] = _ffn_dwgu("dwgu_f1", n3, dgu1)[0]
    dn3, got["gu1"], got["dn1"] = _ffn_dn("dn_f1", dgu1, g_gu1, _rs_swap([share["gu1"], share["dn1"]]))
    dh3, d_ffn1, dh3_16 = _rms_bwd("rmsb_f1", dn3, h3, ffn_norm[1], dh4)
    pair_sums("gu1", "dn1")

    do_b = _matmul("dout_b", dh3_16, w_b_out, "nt", BF16, TM, 1024, 1024)
    share["b_w_out"] = _matmul("dwout_b", o_b, dh3_16, "tn", BF16, TM, 1024, 1024).reshape(N_DEV, 128, D_MODEL)
    dq_b, dk_b, dv_b, ds_rowsum, ds_colsum, others["gu1"], others["dn1"] = _fox_bwd(
        "fox_bwd", qkv, kbias, do_b, o_b, lse_b, tf, _rs_exchange([sums["gu1"], sums["dn1"]]))
    dc = ds_rowsum[:, :N_HEADS] - ds_colsum.reshape(N_HEADS, S).T
    dz, d_bf = _gate_bwd("gate_bwd", jnp.pad(dc, ((0, 0), (0, LANES - N_HEADS))), z, bf_pad)
    dproj_b = jnp.concatenate([dq_b, dk_b, dv_b, dz.astype(BF16)], axis=1)
    dw_b_in = _matmul("dwin_b", n2, dproj_b, "tn", BF16, TM, B_IN_PAD // 5, 1024)
    dn2 = _matmul("dn_b", dproj_b, w_b_cat, "nt", F32, TM, 1024, B_IN_PAD // 5)
    dh2, d_bnorm, dh2_16 = _rms_bwd("rmsb_b", dn2, h2, b_norm_full, dh3)
    share["b_w_in"] = dw_b_in[:, :B_IN].reshape(D_MODEL, N_DEV, B_BLK).transpose(1, 0, 2)

    dgu0, got["b_w_in"], got["b_w_out"] = _ffn_dact("dact_f0", dh2_16, w_dn0, gu0, _rs_swap([share["b_w_in"], share["b_w_out"]]))
    share["dn0"] = _ffn_dwd("dwd_f0", act0, dh2_16)
    pair_sums("b_w_in", "b_w_out")
    share["gu0"], others["b_w_in"], others["b_w_out"] = _ffn_dwgu(
        "dwgu_f0", n1, dgu0, _rs_exchange([sums["b_w_in"], sums["b_w_out"]]))
    dn1, got["gu0"], got["dn0"] = _ffn_dn("dn_f0", dgu0, g_gu0, _rs_swap([share["gu0"], share["dn0"]]))
    dh1, d_ffn0, dh1_16 = _rms_bwd("rmsb_f0", dn1, h1, ffn_norm[0], dh2)
    pair_sums("gu0", "dn0")

    do_views = _matmul_nt_views("dout_a", dh1_16, w_a_out, dils)
    share["a_w_out"] = _matmul("dwout_a", o_a, dh1_16, "tn", BF16, TM, 1024, 1024).reshape(N_DEV, 128, D_MODEL)
    pieces = []
    for g, dil, L, view in groups:
        rot = tuple(tb.reshape(L, dil * LANES) for tb in tabs)
        grads = _dil_bwd("dil_bwd%d" % g, view, do_views[g], o_views[g], lse_views[g], rot, dil, L)
        pieces += list(grads)
    dw_a_in, others["gu0"], others["dn0"] = _a_dw("dwin_a", n0_views, pieces, dils, _rs_exchange([sums["gu0"], sums["dn0"]]))
    share["a_w_in"] = dw_a_in.reshape(D_MODEL, N_DEV, A_BLK).transpose(1, 0, 2)
    got["a_w_in"], got["a_w_out"] = _comm_call("swap_a", _rs_swap([share["a_w_in"], share["a_w_out"]]))
    pair_sums("a_w_in", "a_w_out")
    dn0, others["a_w_in"], others["a_w_out"] = _a_dn("dn_a", pieces, dils, w_a_in, _rs_exchange([sums["a_w_in"], sums["a_w_out"]]))
    dx, d_anorm = _rms_bwd("rmsb_a", dn0, h0, a_norm[0], dh1, copy16=False)

    misc = jnp.concatenate([d_bf[:, :N_HEADS], loss[:, :1], jnp.zeros((1, D_MODEL - N_HEADS - 1), F32)], axis=1)
    small = jnp.concatenate([d_anorm, d_ffn0, d_ffn1, d_final, d_bnorm, misc, jnp.zeros((2, D_MODEL), F32)], axis=0)
    small_all, = _comm_call("gather_small", _ag_send([small], direct=True))

    outs = {}
    for n in TENSORS:
        mine = lax.dynamic_index_in_dim(sums[n], 2 * xi + yi, axis=0, keepdims=False)
        outs[n] = _adamw("adamw_" + n, [mine] + [others[n][k] for k in range(3)], w_sh[n], m_sh[n], v_sh[n], ROW_TILE[n])

    pad_vec = lambda a: jnp.pad(a, ((0, 0), (0, D_MODEL - a.shape[1])))

    def small_pack(an, fn, fin, bf):
        return jnp.concatenate([an, fn, fin.reshape(1, D_MODEL), jnp.zeros((1, D_MODEL), F32), pad_vec(bf),
                                jnp.zeros((2, D_MODEL), F32)], axis=0)

    sg, sd, sm, sv = _adamw("adamw_small", [small_all[d] for d in range(N_DEV)], small_pack(a_norm, ffn_norm, final_norm, b_f),
                            small_pack(m_a_norm, m_ffn_norm, m_final_norm, m_b_f),
                            small_pack(v_a_norm, v_ffn_norm, v_final_norm, v_b_f), 8)
    g_bn = lax.dynamic_slice(sg[4:5], (0, dev * LANES), (1, LANES))
    bn = _adamw("adamw_b_norm", [g_bn], b_norm, m_b_norm, v_b_norm, 1)

    def tree(i):
        full = lambda name, ref: outs[name][i].reshape(ref.shape)
        sml = (sg, sd, sm, sv)[i]
        return dict(
            a_norm=sml[0:1], a_w_in=full("a_w_in", a_w_in), a_w_out=full("a_w_out", a_w_out), b_norm=bn[i],
            b_w_in=full("b_w_in", b_w_in), b_f=sml[5:6, :N_HEADS], b_w_out=full("b_w_out", b_w_out), ffn_norm=sml[1:3],
            ffn_w_gu=jnp.stack([outs["gu0"][i], outs["gu1"][i]]).reshape(ffn_w_gu.shape),
            ffn_w_down=jnp.stack([outs["dn0"][i], outs["dn1"][i]]).reshape(ffn_w_down.shape), final_norm=sml[3])

    order = ("a_norm", "a_w_in", "a_w_out", "b_norm", "b_w_in", "b_f", "b_w_out", "ffn_norm", "ffn_w_gu", "ffn_w_down", "final_norm")
    result = [sg[5, N_HEADS], dx.reshape(x.shape)]
    for i in range(4):
        t = tree(i)
        result += [t[n] for n in order]
    return tuple(result)
```

```python
import functools
from typing import Callable, NamedTuple

import jax
import jax.numpy as jnp
from jax import lax
from jax.experimental import pallas as pl
from jax.experimental.pallas import tpu as pltpu

F32 = jnp.float32
BF16 = jnp.bfloat16

D_MODEL = 1024
N_HEADS = 16
HEAD_DIM = 64
N_PAIRS = N_HEADS // 2
LANES = 128
DILATED_PATTERNS = ((128, 1), (512, 4), (2048, 16))
BAND_STEPS = 128
ROT_DIM = HEAD_DIM // 4
ROPE_THETA = 500000.0
D_FF = 2816
RMS_EPS = 1e-6
NEG_INF = -1e30
SOFTMAX_SCALE = HEAD_DIM ** -0.5
N_DEV = 8
FF_BLK = 2 * D_FF // N_DEV
ADAM_LR, ADAM_B1, ADAM_B2, ADAM_EPS, ADAM_WD, ADAM_STEP = 0.001, 0.9, 0.999, 1e-08, 0.01, 10
VMEM_LIMIT = 52 * 1024 * 1024
FOX_BWD_VMEM = 60 * 1024 * 1024
TM = 1024
MESH = pl.DeviceIdType.MESH

NN = (((1,), (0,)), ((), ()))
NT = (((1,), (1,)), ((), ()))
TN = (((0,), (0,)), ((), ()))


def _params(*sem, vmem=VMEM_LIMIT):
    return pltpu.CompilerParams(dimension_semantics=sem, vmem_limit_bytes=vmem)


def _dot(a, b, dims):
    return lax.dot_general(a, b, dims, preferred_element_type=F32)


class _Comm(NamedTuple):
    ins: tuple
    outs: tuple
    aliases: dict
    copies: Callable
    n_remote: int
    n_local: int


def _call(name, body, grid, in_specs, out_specs, out_shape, scratch, args, sem, comm=None, vmem=VMEM_LIMIT):
    if comm is None:
        return pl.pallas_call(body, grid=grid, in_specs=in_specs, out_specs=out_specs, out_shape=out_shape,
                              scratch_shapes=scratch, compiler_params=_params(*sem, vmem=vmem), name=name)(*args)
    n_in, n_out = len(in_specs), len(out_specs)
    n_ci, n_co = len(comm.ins), len(comm.outs)
    o0 = n_in + n_ci

    def hosted(*refs):
        c_ins, c_outs = refs[n_in:o0], refs[o0 + n_out:o0 + n_out + n_co]
        sems = refs[-3:]

        def start():
            for cp in comm.copies(c_ins, c_outs, *sems):
                cp.start()

        def wait():
            for cp in comm.copies(c_ins, c_outs, *sems):
                cp.wait()

        if not grid:
            start()
            body()
            wait()
            return
        ids = [pl.program_id(ax) for ax in range(len(grid))]
        pl.when(functools.reduce(jnp.logical_and, [i == 0 for i in ids]))(start)
        body(*refs[:n_in], *refs[o0:o0 + n_out], *refs[o0 + n_out + n_co:-3])
        pl.when(functools.reduce(jnp.logical_and, [i == g - 1 for i, g in zip(ids, grid)]))(wait)

    hbm = pl.BlockSpec(memory_space=pltpu.HBM)
    dma = pltpu.SemaphoreType.DMA
    return pl.pallas_call(
        hosted, grid=grid, in_specs=[*in_specs, *[hbm] * n_ci], out_specs=[*out_specs, *[hbm] * n_co],
        out_shape=[*out_shape, *comm.outs], input_output_aliases={n_in + i: n_out + o for i, o in comm.aliases.items()},
        scratch_shapes=[*scratch, dma((comm.n_remote,)), dma((comm.n_remote,)), dma((max(comm.n_local, 1),))],
        compiler_params=_params(*["arbitrary"] * len(grid), vmem=vmem), name=name)(*args, *comm.ins)


def _mm_call(name, grid, a, a_spec, b, b_spec, dims, out_shapes, out_specs, acc_shape, epilogue=None,
             extras=(), extra_specs=(), col_axis=1, comm=None):
    nk = grid[2]
    n_extra = len(extras)
    n_out = len(out_shapes)

    def finish(res, ex, outs, j):
        if epilogue is None:
            outs[0][...] = res.astype(outs[0].dtype)
        else:
            epilogue(res, ex, outs, j)

    def body(*refs):
        a_ref, b_ref = refs[0], refs[1]
        ex = refs[2:2 + n_extra]
        outs = refs[2 + n_extra:2 + n_extra + n_out]
        j, k = pl.program_id(col_axis), pl.program_id(2)
        part = _dot(a_ref[...].astype(BF16), b_ref[...].astype(BF16), dims)
        if nk == 1:
            finish(part, ex, outs, j)
            return
        acc = refs[-1]

        @pl.when(k == 0)
        def _():
            acc[...] = part

        @pl.when((k > 0) & (k < nk - 1))
        def _():
            acc[...] += part

        @pl.when(k == nk - 1)
        def _():
            finish(acc[...] + part, ex, outs, j)

    return _call(name, body, grid, [a_spec, b_spec, *extra_specs], out_specs, out_shapes,
                 [] if nk == 1 else [pltpu.VMEM(acc_shape, F32)], (a, b, *extras), ("parallel", "parallel", "arbitrary"), comm)


def _matmul(name, a, b, mode, out_dtype, tm, tn, tk, resid=None, col0_scale=None, comm=None):
    if mode == "nn":
        (M, K), N = a.shape, b.shape[1]
        a_spec = pl.BlockSpec((tm, tk), lambda j, i, k: (i, k))
        b_spec = pl.BlockSpec((tk, tn), lambda j, i, k: (k, j))
        dims = NN
    elif mode == "nt":
        (M, K), N = a.shape, b.shape[0]
        a_spec = pl.BlockSpec((tm, tk), lambda j, i, k: (i, k))
        b_spec = pl.BlockSpec((tn, tk), lambda j, i, k: (j, k))
        dims = NT
    else:
        (K, M), N = a.shape, b.shape[1]
        a_spec = pl.BlockSpec((tk, tm), lambda j, i, k: (k, i))
        b_spec = pl.BlockSpec((tk, tn), lambda j, i, k: (k, j))
        dims = TN
    assert M % tm == 0 and N % tn == 0 and K % tk == 0, (name, M, N, K, tm, tn, tk)
    o_spec = pl.BlockSpec((tm, tn), lambda j, i, k: (i, j))
    extras, extra_specs, epilogue = (), (), None
    if resid is not None:
        extras, extra_specs = (resid,), (o_spec,)

        def epilogue(acc, ex, outs, j):
            outs[0][...] = (acc + ex[0][...]).astype(outs[0].dtype)

    elif col0_scale is not None:

        def epilogue(acc, ex, outs, j):
            outs[0][...] = (acc * jnp.where(j == 0, col0_scale, 1.0)).astype(outs[0].dtype)

    res = _mm_call(name, (N // tn, M // tm, K // tk), a, a_spec, b, b_spec, dims, [jax.ShapeDtypeStruct((M, N), out_dtype)],
                   [o_spec], (tm, tn), epilogue, extras, extra_specs, col_axis=0, comm=comm)
    return res[0] if comm is None else (res[0], res[1:])


def _rms_fwd(name, h, gain, dils=(1,), comm=None, tm=512):
    S, D = h.shape

    def body(h_ref, g_ref, *rest):
        x = h_ref[...]
        rstd = lax.rsqrt(jnp.mean(x * x, axis=-1, keepdims=True) + RMS_EPS)
        y = x * rstd * g_ref[...]
        _write_views([y[:, b * LANES:(b + 1) * LANES] for b in range(N_PAIRS)], rest[-1], rest[:-1], dils, tm)

    res = _call(name, body, (S // tm,), [pl.BlockSpec((tm, D), lambda i: (i, 0)), pl.BlockSpec((1, D), lambda i: (0, 0))],
                [_view_spec(tm, R) for R in dils], [jax.ShapeDtypeStruct((S // R, R * D), BF16) for R in dils],
                [pltpu.VMEM((N_PAIRS, tm, LANES), F32)], (h, gain.reshape(1, D)), ("parallel",), comm)
    views = res[0] if len(dils) == 1 else res[:len(dils)]
    return views if comm is None else (views, res[len(dils):])


def _rms_bwd(name, dn, h, gain, dres, copy16=True, tm=512):
    S, D = h.shape

    def body(dn_ref, h_ref, g_ref, r_ref, dh_ref, dg_ref, *dh16_ref):
        x = h_ref[...]
        rstd = lax.rsqrt(jnp.mean(x * x, axis=-1, keepdims=True) + RMS_EPS)
        xhat = x * rstd
        d = dn_ref[...]
        dxhat = d * g_ref[...]
        dh = rstd * (dxhat - xhat * jnp.mean(dxhat * xhat, axis=-1, keepdims=True)) + r_ref[...]
        dh_ref[...] = dh
        if copy16:
            dh16_ref[0][...] = dh.astype(BF16)

        @pl.when(pl.program_id(0) == 0)
        def _():
            dg_ref[...] = jnp.zeros_like(dg_ref)

        dg_ref[...] += jnp.sum(d * xhat, axis=0, keepdims=True)

    row = pl.BlockSpec((tm, D), lambda i: (i, 0))
    vec = pl.BlockSpec((1, D), lambda i: (0, 0))
    return pl.pallas_call(
        body, grid=(S // tm,), in_specs=[row, row, vec, row], out_specs=[row, vec] + [row] * copy16,
        out_shape=[jax.ShapeDtypeStruct((S, D), F32), jax.ShapeDtypeStruct((1, D), F32)] + [jax.ShapeDtypeStruct((S, D), BF16)] * copy16,
        compiler_params=_params("arbitrary"), name=name)(dn, h, gain.reshape(1, D), dres)


def _loss_head(name, h, gain, target, tm=512):
    S, D = h.shape

    def body(h_ref, g_ref, t_ref, dh_ref, dg_ref, loss_ref, dh16_ref):
        x = h_ref[...]
        rstd = lax.rsqrt(jnp.mean(x * x, axis=-1, keepdims=True) + RMS_EPS)
        xhat = x * rstd
        err = xhat * g_ref[...] - t_ref[...]
        dy = err * (1.0 / D)
        dxhat = dy * g_ref[...]
        dh = rstd * (dxhat - xhat * jnp.mean(dxhat * xhat, axis=-1, keepdims=True))
        dh_ref[...] = dh
        dh16_ref[...] = dh.astype(BF16)

        @pl.when(pl.program_id(0) == 0)
        def _():
            dg_ref[...] = jnp.zeros_like(dg_ref)
            loss_ref[...] = jnp.zeros_like(loss_ref)

        dg_ref[...] += jnp.sum(dy * xhat, axis=0, keepdims=True)
        part = 0.5 * jnp.sum(jnp.mean(err * err, axis=-1, keepdims=True), axis=0, keepdims=True)
        loss_ref[...] += jnp.broadcast_to(part, loss_ref.shape)

    row = pl.BlockSpec((tm, D), lambda i: (i, 0))
    vec = pl.BlockSpec((1, D), lambda i: (0, 0))
    return pl.pallas_call(
        body, grid=(S // tm,), in_specs=[row, vec, row], out_specs=[row, vec, pl.BlockSpec((1, LANES), lambda i: (0, 0)), row],
        out_shape=[jax.ShapeDtypeStruct((S, D), F32), jax.ShapeDtypeStruct((1, D), F32),
                   jax.ShapeDtypeStruct((1, LANES), F32), jax.ShapeDtypeStruct((S, D), BF16)],
        compiler_params=_params("arbitrary"), name=name)(h, gain.reshape(1, D), target)


def _rope_tables(S):
    half = ROT_DIM // 2
    inv_freq = ROPE_THETA ** (-jnp.arange(half, dtype=F32) * 2.0 / ROT_DIM)
    ang = jnp.arange(S, dtype=F32)[:, None] * inv_freq[None, :]
    cos, sin = jnp.cos(ang), jnp.sin(ang)
    one = jnp.ones((S, HEAD_DIM - ROT_DIM), F32)
    zero = jnp.zeros((S, HEAD_DIM - ROT_DIM), F32)
    zh = jnp.zeros((S, half), F32)
    c = jnp.concatenate([cos, cos, one], axis=1)
    sa = jnp.concatenate([-sin, zh, zero], axis=1)
    sb = jnp.concatenate([zh, sin, zero], axis=1)
    return tuple(jnp.concatenate([t, t], axis=1) for t in (c, sa, sb))


def _rotate(x, c, sa, sb, sign):
    return x * c + sign * (pltpu.roll(x, LANES - ROT_DIM // 2, 1) * sa + pltpu.roll(x, ROT_DIM // 2, 1) * sb)


def _stage_chunks(scr, chunks):
    for c, x in enumerate(chunks):
        scr[c] = x


def _strided_rows(scr, c, r, n, R):
    return scr.at[c][pl.ds(r, n, stride=R), :]


def _a_proj(name, n, w, g, R, tabs, comm, tm=1024):
    S, D = n.shape
    n_i = S // tm
    n_out = 3

    def body(n_ref, w_ref, c_ref, sa_ref, sb_ref, *rest):
        outs, scr = rest[:n_out], rest[n_out]
        j = pl.program_id(0)
        acc = _dot(n_ref[...], w_ref[...], NN)
        c, sa, sb = c_ref[...], sa_ref[...], sb_ref[...]
        for J in range(n_out):
            kind = J

            @pl.when(j == J)
            def _(J=J, kind=kind):
                chunks = [acc[:, b * LANES:(b + 1) * LANES] for b in range(N_PAIRS)]
                if kind < 2:
                    chunks = [_rotate(x, c, sa, sb, 1.0) * (SOFTMAX_SCALE if kind == 0 else 1.0) for x in chunks]
                if R == 1:
                    for b, x in enumerate(chunks):
                        outs[J][:, b * LANES:(b + 1) * LANES] = x.astype(BF16)
                    return
                _stage_chunks(scr, chunks)
                for r in range(R):
                    for b in range(N_PAIRS):
                        col = r * D_MODEL + b * LANES
                        outs[J][:, col:col + LANES] = _strided_rows(scr, b, r, tm // R, R).astype(BF16)

    def out_spec(J):
        return pl.BlockSpec((tm // R, R * D_MODEL), lambda j, i: (jnp.where(j == J, i, jnp.where(j < J, 0, n_i - 1)), 0))

    tab = pl.BlockSpec((tm, LANES), lambda j, i: (i, 0))
    res = _call(name, body, (n_out, n_i),
                [pl.BlockSpec((tm, D), lambda j, i: (i, 0)), pl.BlockSpec((D, D_MODEL), lambda j, i: (0, 3 * g + j)), tab, tab, tab],
                [out_spec(J) for J in range(n_out)], [jax.ShapeDtypeStruct((S // R, R * D_MODEL), BF16)] * n_out,
                [pltpu.VMEM((N_PAIRS, tm, LANES), F32)], (n, w, *tabs), ("arbitrary", "arbitrary"), comm)
    return res[:n_out], res[n_out:]


def _unstride(src_chunk, R, tok, rows):
    for r in range(R):
        for b in range(N_PAIRS):
            tok.at[b][pl.ds(r, rows // R, stride=R), :] = src_chunk(r, b).astype(F32)


def _by_residue(ref, R):
    return ref[...] if R == 1 else jnp.concatenate([ref[:, r * D_MODEL:(r + 1) * D_MODEL] for r in range(R)], axis=0)


def _a_dw(name, n_view, pieces, R, comm, tk=1024):
    D = D_MODEL
    S = n_view.shape[0] * R
    n_k = S // tk
    n_p = len(pieces)

    def body(n_ref, *rest):
        p_refs, o_ref, acc = rest[:n_p], rest[n_p], rest[n_p + 1]
        j, k = pl.program_id(0), pl.program_id(1)
        for J in range(n_p):

            @pl.when(j == J)
            def _(J=J):
                part = _dot(_by_residue(n_ref, R), _by_residue(p_refs[J], R), TN)

                @pl.when(k == 0)
                def _():
                    acc[...] = part

                @pl.when((k > 0) & (k < n_k - 1))
                def _():
                    acc[...] += part

                @pl.when(k == n_k - 1)
                def _():
                    o_ref[...] = (acc[...] + part).astype(BF16)

    def piece_spec(J):
        return pl.BlockSpec((tk // R, R * D_MODEL), lambda j, k: (jnp.where(j == J, k, jnp.where(j < J, 0, n_k - 1)), 0))

    return _call(name, body, (n_p, n_k), [pl.BlockSpec((tk // R, R * D_MODEL), lambda j, k: (k, 0))] + [piece_spec(J) for J in range(n_p)],
                 [pl.BlockSpec((D, D_MODEL), lambda j, k: (0, j))], [jax.ShapeDtypeStruct((D, n_p * D_MODEL), BF16)],
                 [pltpu.VMEM((D, D_MODEL), F32)], (n_view, *pieces), ("arbitrary", "arbitrary"), comm)


def _a_dn(name, pieces, R, w, g, resid, comm, tm=1024):
    D = w.shape[0]
    S = pieces[0].shape[0] * R
    n_p = len(pieces)
    has_resid = resid is not None

    def body(*refs):
        p_refs, w_ref = refs[:n_p], refs[n_p]
        r_ref = refs[n_p + 1] if has_resid else None
        o_ref, part_acc, tok = refs[-3:]
        j = pl.program_id(1)
        for J in range(n_p):

            @pl.when(j == J)
            def _(J=J):
                part = _dot(_by_residue(p_refs[J], R), w_ref[...], NT)
                if J == 0:
                    part_acc[...] = part
                    return
                if J < n_p - 1:
                    part_acc[...] += part
                    return
                total = part_acc[...] + part
                if R > 1:
                    n = tm // R
                    _unstride(lambda r, b: total[r * n:(r + 1) * n, b * LANES:(b + 1) * LANES], R, tok, tm)
                    total = jnp.concatenate([tok[b] for b in range(N_PAIRS)], axis=1)
                o_ref[...] = total + r_ref[...] if has_resid else total

    row = pl.BlockSpec((tm, D), lambda i, j: (i, 0))
    specs = [pl.BlockSpec((tm // R, R * D_MODEL), lambda i, j: (i, 0))] * n_p + [pl.BlockSpec((D, D_MODEL), lambda i, j: (0, 3 * g + j))]
    args = (*pieces, w) + ((resid,) if has_resid else ())
    return _call(name, body, (S // tm, n_p), specs + [row] * has_resid, [row], [jax.ShapeDtypeStruct((S, D), F32)],
                 [pltpu.VMEM((tm, D), F32), pltpu.VMEM((N_PAIRS, tm, LANES), F32)], args, ("arbitrary", "arbitrary"), comm)


def _lo_lanes():
    return lax.broadcasted_iota(jnp.int32, (1, LANES), 1) < HEAD_DIM


def _rep_rows(x2, lo):
    sw = pltpu.roll(x2, HEAD_DIM, 1)
    return jnp.where(lo, x2, sw), jnp.where(lo, sw, x2)


def _pair_cols(h):
    return slice((h // 2) * LANES, (h // 2 + 1) * LANES)


def _head_lanes(lo, h):
    return lo if h % 2 == 0 else jnp.logical_not(lo)


def _band_masks(t, first):
    ri = lax.broadcasted_iota(jnp.int32, (t, t), 0)
    ci = lax.broadcasted_iota(jnp.int32, (t, t), 1)
    neg_prev = jnp.where((ci >= ri) & jnp.logical_not(first), 0.0, NEG_INF)
    neg_cur = jnp.where(ci <= ri, 0.0, NEG_INF)
    return neg_prev, neg_cur


def _dil_specs(L, R, t, qcol, kcol, vcol):
    W = D_MODEL
    prev = lambda qi: jnp.maximum(qi - 1, 0)
    return dict(
        q=pl.BlockSpec((t, W), lambda r, qi: (qi, qcol(r))),
        kp=pl.BlockSpec((t, W), lambda r, qi: (prev(qi), kcol(r))), kc=pl.BlockSpec((t, W), lambda r, qi: (qi, kcol(r))),
        vp=pl.BlockSpec((t, W), lambda r, qi: (prev(qi), vcol(r))), vc=pl.BlockSpec((t, W), lambda r, qi: (qi, vcol(r))),
        own=pl.BlockSpec((t, W), lambda r, qi: (qi, r)), tab=pl.BlockSpec((t, LANES), lambda r, qi: (qi, r)))


def _dil_fwd(name, x, qcol, kcol, vcol, R, L, comm=None):
    t = BAND_STEPS
    W = D_MODEL
    sp = _dil_specs(L, R, t, qcol, kcol, vcol)

    def body(q_ref, kp_ref, kc_ref, vp_ref, vc_ref, o_ref, lse_ref):
        lo = _lo_lanes()
        neg_p, neg_c = _band_masks(t, pl.program_id(1) == 0)
        s_p, s_c = [], []
        for h in range(N_HEADS):
            cols = _pair_cols(h)
            qh = jnp.where(_head_lanes(lo, h), q_ref[:, cols], 0)
            s_p.append(_dot(qh, kp_ref[:, cols], NT))
            s_c.append(_dot(qh, kc_ref[:, cols], NT))
        s_p = jnp.stack(s_p) + neg_p[None]
        s_c = jnp.stack(s_c) + neg_c[None]
        m = jnp.maximum(jnp.max(s_p, axis=2, keepdims=True), jnp.max(s_c, axis=2, keepdims=True))
        p_p, p_c = jnp.exp(s_p - m), jnp.exp(s_c - m)
        l = jnp.sum(p_p, axis=2, keepdims=True) + jnp.sum(p_c, axis=2, keepdims=True)
        inv, lse = 1.0 / l, m + jnp.log(l)
        p_p, p_c = p_p.astype(BF16), p_c.astype(BF16)
        for p in range(N_PAIRS):
            cols = _pair_cols(2 * p)
            o2 = jnp.zeros((t, LANES), F32)
            for h in (2 * p, 2 * p + 1):
                hm = _head_lanes(lo, h)
                pv = _dot(p_p[h], jnp.where(hm, vp_ref[:, cols], 0), NN) + _dot(p_c[h], jnp.where(hm, vc_ref[:, cols], 0), NN)
                o2 = o2 + pv * inv[h]
            o_ref[:, cols] = o2
            lse_ref[:, cols] = jnp.where(lo, lse[2 * p], lse[2 * p + 1])

    return _call(name, body, (R, L // t), [sp["q"], sp["kp"], sp["kc"], sp["vp"], sp["vc"]], [sp["own"], sp["own"]],
                 [jax.ShapeDtypeStruct((L, R * W), F32), jax.ShapeDtypeStruct((L, R * W), F32)], [],
                 (x[0], x[1], x[1], x[2], x[2]), ("parallel", "parallel"), comm)


def _dil_scores(lo, q_ref, do_ref, o_ref, lse_ref, kv_refs):
    s = [[] for _ in kv_refs]
    dp = [[] for _ in kv_refs]
    lse, d = [], []
    for h in range(N_HEADS):
        cols = _pair_cols(h)
        hm = _head_lanes(lo, h)
        qh, doh = jnp.where(hm, q_ref[:, cols], 0), jnp.where(hm, do_ref[:, cols], 0)
        for i, (k_ref, v_ref) in enumerate(kv_refs):
            s[i].append(_dot(qh, k_ref[:, cols], NT))
            dp[i].append(_dot(doh, v_ref[:, cols], NT))
        lse.append(_rep_rows(lse_ref[:, cols], lo)[h % 2])
        dd = do_ref[:, cols].astype(F32) * o_ref[:, cols].astype(F32)
        d.append(jnp.sum(jnp.where(hm, dd, 0.0), axis=1, keepdims=True))
    return (*[jnp.stack(x) for x in s], *[jnp.stack(x) for x in dp], jnp.stack(lse), jnp.stack(d))


def _dil_bwd(name, x, do, o, lse, tabs, R, L):
    t = BAND_STEPS
    W = D_MODEL
    nq = L // t
    qb = lambda step: nq - 1 - step
    kb = lambda step: jnp.maximum(qb(step) - 1, 0)
    at = lambda f, width: pl.BlockSpec((t, width), lambda r, step: (f(step), r))

    def body(q_ref, kp_ref, kc_ref, vp_ref, vc_ref, do_ref, o_ref, lse_ref, c_ref, sa_ref, sb_ref, dq_ref, dk_ref, dv_ref,
             dk_scr, dv_scr):
        qi = nq - 1 - pl.program_id(1)
        lo = _lo_lanes()
        unrotate = lambda x: _rotate(x, c_ref[...], sa_ref[...], sb_ref[...], -1.0).astype(BF16)

        @pl.when(qi == nq - 1)
        def _():
            dk_scr[...] = jnp.zeros_like(dk_scr)
            dv_scr[...] = jnp.zeros_like(dv_scr)

        neg_p, neg_c = _band_masks(t, qi == 0)
        s_p, s_c, dp_p, dp_c, lse_h, d = _dil_scores(lo, q_ref, do_ref, o_ref, lse_ref, ((kp_ref, vp_ref), (kc_ref, vc_ref)))
        p_p, p_c = jnp.exp(s_p + neg_p[None] - lse_h), jnp.exp(s_c + neg_c[None] - lse_h)
        ds_p, ds_c = (p_p * (dp_p - d)).astype(BF16), (p_c * (dp_c - d)).astype(BF16)
        p_p, p_c = p_p.astype(BF16), p_c.astype(BF16)
        for p in range(N_PAIRS):
            cols = _pair_cols(2 * p)
            dq2 = jnp.zeros((t, LANES), F32)
            dk_cur, dv_cur = dk_scr[:, cols], dv_scr[:, cols]
            dk_prev, dv_prev = jnp.zeros((t, LANES), F32), jnp.zeros((t, LANES), F32)
            for h in (2 * p, 2 * p + 1):
                hm = _head_lanes(lo, h)
                qh, doh = jnp.where(hm, q_ref[:, cols], 0), jnp.where(hm, do_ref[:, cols], 0)
                dq2 = dq2 + _dot(ds_p[h], jnp.where(hm, kp_ref[:, cols], 0), NN) + _dot(ds_c[h], jnp.where(hm, kc_ref[:, cols], 0), NN)
                dk_prev, dv_prev = dk_prev + _dot(ds_p[h], qh, TN), dv_prev + _dot(p_p[h], doh, TN)
                dk_cur, dv_cur = dk_cur + _dot(ds_c[h], qh, TN), dv_cur + _dot(p_c[h], doh, TN)
            dq_ref[:, cols] = unrotate(dq2 * SOFTMAX_SCALE)
            dk_ref[:, cols] = unrotate(dk_cur)
            dv_ref[:, cols] = dv_cur.astype(BF16)
            dk_scr[:, cols] = dk_prev
            dv_scr[:, cols] = dv_prev

    wide = jax.ShapeDtypeStruct((L, R * W), BF16)
    return pl.pallas_call(
        body, grid=(R, nq),
        in_specs=[at(qb, W), at(kb, W), at(qb, W), at(kb, W), at(qb, W), at(qb, W), at(qb, W), at(qb, W),
                  at(qb, LANES), at(qb, LANES), at(qb, LANES)],
        out_specs=[at(qb, W), at(qb, W), at(qb, W)], out_shape=[wide, wide, wide],
        scratch_shapes=[pltpu.VMEM((t, W), F32), pltpu.VMEM((t, W), F32)],
        compiler_params=_params("parallel", "arbitrary"), name=name)(x[0], x[1], x[1], x[2], x[2], do, o, lse, *tabs)


def _fox_operands(q2, k2, kb2, lo, hh):
    lane = lax.broadcasted_iota(jnp.int32, (1, LANES), 1)
    if hh == 0:
        ones = ((lane >= HEAD_DIM) & (lane < HEAD_DIM + 3)).astype(BF16)
        return jnp.where(lo, q2, ones), jnp.where(lo, k2, kb2)
    ones = (lane < 3).astype(BF16)
    return jnp.where(lo, ones, q2), jnp.where(lo, kb2, k2)


def _causal_neg(t):
    ri = lax.broadcasted_iota(jnp.int32, (t, t), 0)
    ci = lax.broadcasted_iota(jnp.int32, (t, t), 1)
    return jnp.where(ci <= ri, 0.0, NEG_INF)


def _fox_fwd(name, qkv, kbias, t, comm=None):
    S = qkv.shape[0]
    W = D_MODEL
    nq = S // t
    rep = t // LANES

    def body(q_ref, k_ref, v_ref, kb_ref, o_ref, lse_ref, m_scr, l_scr, acc_scr):
        qi, j = pl.program_id(0), pl.program_id(1)
        lo = _lo_lanes()

        @pl.when(j == 0)
        def _():
            m_scr[...] = jnp.full_like(m_scr, NEG_INF)
            l_scr[...] = jnp.zeros_like(l_scr)
            acc_scr[...] = jnp.zeros_like(acc_scr)

        def step(masked):
            neg = _causal_neg(t) if masked else None

            def pair(p, carry):
                cs = pl.ds(pl.multiple_of(p * LANES, LANES), LANES)
                q2, k2, v2, kb2 = q_ref[:, cs], k_ref[:, cs], v_ref[:, cs], kb_ref[:, cs]
                pvs, alphas = [], []
                for hh in range(2):
                    hm = lo if hh == 0 else jnp.logical_not(lo)
                    qh, kh = _fox_operands(q2, k2, kb2, lo, hh)
                    s = _dot(qh, kh, NT)
                    if masked:
                        s = s + neg
                    h = 2 * p + hh
                    m_prev = m_scr[h]
                    m_new = jnp.maximum(m_prev, jnp.max(s, axis=1, keepdims=True))
                    pe = jnp.exp(s - jnp.tile(m_new, (1, rep)))
                    alpha = jnp.exp(m_prev - m_new)
                    l_scr[h] = alpha * l_scr[h] + jnp.sum(pe, axis=1, keepdims=True)
                    m_scr[h] = m_new
                    pvs.append(_dot(pe.astype(BF16), jnp.where(hm, v2, 0), NN))
                    alphas.append(alpha)
                acc_scr[:, cs] = acc_scr[:, cs] * jnp.where(lo, alphas[0], alphas[1]) + pvs[0] + pvs[1]
                return carry

            lax.fori_loop(0, N_PAIRS, pair, 0, unroll=4)

        @pl.when(j < qi)
        def _():
            step(False)

        @pl.when(j == qi)
        def _():
            step(True)

        @pl.when(j == nq - 1)
        def _():
            for p in range(N_PAIRS):
                cols = slice(p * LANES, (p + 1) * LANES)
                l2 = jnp.where(lo, l_scr[2 * p], l_scr[2 * p + 1])
                m2 = jnp.where(lo, m_scr[2 * p], m_scr[2 * p + 1])
                o_ref[:, cols] = (acc_scr[:, cols] / l2).astype(BF16)
                lse_ref[:, cols] = m2 + jnp.log(l2)

    kv = lambda col: pl.BlockSpec((t, W), lambda qi, j: (jnp.minimum(j, qi), col))
    own = pl.BlockSpec((t, W), lambda qi, j: (qi, 0))
    return _call(name, body, (nq, nq), [own, kv(1), kv(2), kv(0)], [own, own],
                 [jax.ShapeDtypeStruct((S, W), BF16), jax.ShapeDtypeStruct((S, W), F32)],
                 [pltpu.VMEM((N_HEADS, t, LANES), F32), pltpu.VMEM((N_HEADS, t, LANES), F32), pltpu.VMEM((t, W), F32)],
                 (qkv, qkv, qkv, kbias), ("parallel", "arbitrary"), comm)


def _fox_head_grads(qh, kh, v2, doh, neg, lse_h, d_h, rep):
    s = _dot(qh, kh, NT)
    if neg is not None:
        s = s + neg
    p = jnp.exp(s - jnp.tile(lse_h, (1, rep)))
    return p, p * (_dot(doh, v2, NT) - d_h)


def _fox_bwd(name, qkv, kbias, do, o, lse, t, comm=None):
    S = qkv.shape[0]
    W = D_MODEL
    nq = S // t
    rep = t // LANES

    def body(q_ref, k_ref, v_ref, kb_ref, do_ref, o_ref, lse_ref, dq_ref, dk_ref, dv_ref, rs_ref, dc_ref, dq_scr, dk_scr, dv_scr):
        kb, j = pl.program_id(0), pl.program_id(1)
        lo = _lo_lanes()
        lane = lax.broadcasted_iota(jnp.int32, (1, LANES), 1)
        rows = pl.ds(pl.multiple_of(j * t, t), t)

        @pl.when((kb == 0) & (j == 0))
        def _():
            dq_scr[...] = jnp.zeros_like(dq_scr)
            rs_ref[...] = jnp.zeros_like(rs_ref)

        @pl.when(j == 0)
        def _():
            dk_scr[...] = jnp.zeros_like(dk_scr)
            dv_scr[...] = jnp.zeros_like(dv_scr)
            dc_ref[...] = jnp.zeros_like(dc_ref)

        def step(masked):
            neg = _causal_neg(t) if masked else None

            def pair(p, carry):
                cs = pl.ds(pl.multiple_of(p * LANES, LANES), LANES)
                q2, k2, v2, kb2, do2 = q_ref[:, cs], k_ref[:, cs], v_ref[:, cs], kb_ref[:, cs], do_ref[:, cs]
                dd = do2.astype(F32) * o_ref[:, cs].astype(F32)
                lse_h = _rep_rows(lse_ref[:, cs], lo)
                dq2 = jnp.zeros((t, LANES), F32)
                dv2 = jnp.zeros((t, LANES), F32)
                dk2 = jnp.zeros((t, LANES), F32)
                for hh in range(2):
                    hm = lo if hh == 0 else jnp.logical_not(lo)
                    qh, kh = _fox_operands(q2, k2, kb2, lo, hh)
                    doh = jnp.where(hm, do2, 0)
                    d_h = jnp.sum(jnp.where(hm, dd, 0.0), axis=1, keepdims=True)
                    pr, ds = _fox_head_grads(qh, kh, v2, doh, neg, lse_h[hh], d_h, rep)
                    rs_ref[rows, :] += jnp.where(lane == 2 * p + hh, jnp.sum(ds, axis=1, keepdims=True), 0.0)
                    dc_ref[p, hh:hh + 1, :] += jnp.sum(ds, axis=0, keepdims=True)
                    dsb = ds.astype(BF16)
                    dv2 = dv2 + _dot(pr.astype(BF16), doh, TN)
                    dk2 = dk2 + _dot(dsb, jnp.where(hm, q2, 0), TN)
                    dq2 = dq2 + _dot(dsb, jnp.where(hm, k2, 0), NN)
                dv_scr[:, cs] += dv2
                dk_scr[:, cs] += dk2
                dq_scr[rows, cs] += dq2
                return carry

            lax.fori_loop(0, N_PAIRS, pair, 0, unroll=4)
            if masked:
                dq_ref[...] = (dq_scr[rows, :] * SOFTMAX_SCALE).astype(BF16)

        @pl.when(j > kb)
        def _():
            step(False)

        @pl.when(j == kb)
        def _():
            step(True)

        @pl.when(j == nq - 1)
        def _():
            dv_ref[...] = dv_scr[...].astype(BF16)
            dk_ref[...] = dk_scr[...].astype(BF16)

    qrow = pl.BlockSpec((t, W), lambda kb, j: (jnp.maximum(j, kb), 0))
    krow = lambda col: pl.BlockSpec((t, W), lambda kb, j: (kb, col))
    own = pl.BlockSpec((t, W), lambda kb, j: (kb, 0))
    wide = jax.ShapeDtypeStruct((S, W), BF16)
    return _call(name, body, (nq, nq), [qrow, krow(1), krow(2), krow(0), qrow, qrow, qrow],
                 [own, own, own, pl.BlockSpec((S, LANES), lambda kb, j: (0, 0)), pl.BlockSpec((N_PAIRS, 2, t), lambda kb, j: (0, 0, kb))],
                 [wide, wide, wide, jax.ShapeDtypeStruct((S, LANES), F32), jax.ShapeDtypeStruct((N_PAIRS, 2, S), F32)],
                 [pltpu.VMEM((S, W), F32), pltpu.VMEM((t, W), F32), pltpu.VMEM((t, W), F32)],
                 (qkv, qkv, qkv, kbias, do, o, lse), ("arbitrary", "arbitrary"), comm, vmem=FOX_BWD_VMEM)


def _view_spec(tm, R, index=lambda i: (i, 0)):
    return pl.BlockSpec((tm // R, R * D_MODEL), index)


def _matmul_nt_views(name, a, w, dils, tm=512):
    S, K = a.shape

    def body(a_ref, w_ref, *rest):
        res = _dot(a_ref[...].astype(BF16), w_ref[...], NT)
        _write_views([res[:, b * LANES:(b + 1) * LANES] for b in range(N_PAIRS)], rest[-1], rest[:-1], dils, tm)

    return pl.pallas_call(
        body, grid=(S // tm,), in_specs=[pl.BlockSpec((tm, K), lambda i: (i, 0)), pl.BlockSpec((D_MODEL, K), lambda i: (0, 0))],
        out_specs=[_view_spec(tm, R) for R in dils],
        out_shape=[jax.ShapeDtypeStruct((S // R, R * D_MODEL), BF16) for R in dils],
        scratch_shapes=[pltpu.VMEM((N_PAIRS, tm, LANES), F32)], compiler_params=_params("parallel"), name=name)(a, w)


def _write_views(chunks, scr, out_refs, dils, tm):
    if any(R > 1 for R in dils):
        _stage_chunks(scr, chunks)
    for ref, R in zip(out_refs, dils):
        for b, x in enumerate(chunks):
            if R == 1:
                ref[:, b * LANES:(b + 1) * LANES] = x.astype(ref.dtype)
                continue
            for r in range(R):
                col = r * D_MODEL + b * LANES
                ref[:, col:col + LANES] = _strided_rows(scr, b, r, tm // R, R).astype(ref.dtype)


def _combine(name, os_, lses, dils, tm=256):
    S = os_[0].shape[0] * dils[0]
    G = len(dils)

    def body(*refs):
        o_refs, l_refs = refs[:G], refs[G:2 * G]
        o_outs, l_outs = refs[2 * G:3 * G], refs[3 * G:4 * G]
        stage = refs[4 * G:]
        for g, R in enumerate(dils):
            if R == 1:
                continue
            for src, dst in ((o_refs[g], stage[2 * g]), (l_refs[g], stage[2 * g + 1])):
                _unstride(lambda r, b, src=src: src[:, r * D_MODEL + b * LANES:r * D_MODEL + (b + 1) * LANES], R, dst, tm)
        o_chunks, l_chunks = [], []
        for b in range(N_PAIRS):
            cols = slice(b * LANES, (b + 1) * LANES)
            os_b = [o_refs[g][:, cols] if R == 1 else stage[2 * g][b] for g, R in enumerate(dils)]
            ls = [l_refs[g][:, cols] if R == 1 else stage[2 * g + 1][b] for g, R in enumerate(dils)]
            m = functools.reduce(jnp.maximum, ls)
            ws = [jnp.exp(l - m) for l in ls]
            den = functools.reduce(jnp.add, ws)
            o_chunks.append(functools.reduce(jnp.add, [w * o for w, o in zip(ws, os_b)]) / den)
            l_chunks.append(m + jnp.log(den))
        _write_views(o_chunks, stage[0], o_outs, dils, tm)
        _write_views(l_chunks, stage[1], l_outs, dils, tm)

    specs = [_view_spec(tm, R) for R in dils]
    shapes = lambda dt: [jax.ShapeDtypeStruct((S // R, R * D_MODEL), dt) for R in dils]
    res = pl.pallas_call(
        body, grid=(S // tm,), in_specs=specs * 2, out_specs=specs * 2, out_shape=shapes(BF16) + shapes(F32),
        scratch_shapes=[pltpu.VMEM((N_PAIRS, tm, LANES), F32)] * (2 * G), compiler_params=_params("parallel"),
        name=name)(*os_, *lses)
    return res[:G], res[G:]


def _tri_matmul(tri, x):
    hi, mid, lo = _split3(x)
    return _dot(tri, hi, NN) + _dot(tri, mid, NN) + _dot(tri, lo, NN)


def _split3(x):
    hi = x.astype(BF16)
    r1 = x - hi.astype(F32)
    mid = r1.astype(BF16)
    return hi, mid, (r1 - mid.astype(F32)).astype(BF16)


def _gate_fwd(name, z, bf, tb=512):
    S = z.shape[0]

    def body(z_ref, b_ref, kb_ref, carry):
        @pl.when(pl.program_id(0) == 0)
        def _():
            carry[...] = jnp.zeros_like(carry)

        lf = jax.nn.log_sigmoid(z_ref[...] + b_ref[...])
        ri = lax.broadcasted_iota(jnp.int32, (tb, tb), 0)
        ci = lax.broadcasted_iota(jnp.int32, (tb, tb), 1)
        tri = (ci <= ri).astype(BF16)
        c = _tri_matmul(tri, lf) + carry[...]
        carry[...] = c[tb - 1:tb, :]
        head = lax.broadcasted_iota(jnp.int32, (LANES, D_MODEL), 0)
        col = lax.broadcasted_iota(jnp.int32, (LANES, D_MODEL), 1)
        base = (head >> 1) * LANES + jnp.where((head & 1) == 0, HEAD_DIM, 0)
        kb = jnp.zeros((tb, D_MODEL), F32)
        for i, piece in enumerate(_split3(-c)):
            place = ((col == base + i) & (head < N_HEADS)).astype(BF16)
            kb = kb + _dot(piece, place, NN)
        kb_ref[...] = kb.astype(BF16)

    row = pl.BlockSpec((tb, LANES), lambda i: (i, 0))
    return pl.pallas_call(
        body, grid=(S // tb,), in_specs=[row, pl.BlockSpec((1, LANES), lambda i: (0, 0))],
        out_specs=pl.BlockSpec((tb, D_MODEL), lambda i: (i, 0)), out_shape=jax.ShapeDtypeStruct((S, D_MODEL), BF16),
        scratch_shapes=[pltpu.VMEM((1, LANES), F32)], compiler_params=_params("arbitrary"), name=name)(z, bf)


def _gate_bwd(name, dc, z, bf, tb=512):
    S = z.shape[0]
    nb = S // tb

    def body(dc_ref, z_ref, b_ref, dz_ref, db_ref, carry):
        @pl.when(pl.program_id(0) == 0)
        def _():
            carry[...] = jnp.zeros_like(carry)
            db_ref[...] = jnp.zeros_like(db_ref)

        ri = lax.broadcasted_iota(jnp.int32, (tb, tb), 0)
        ci = lax.broadcasted_iota(jnp.int32, (tb, tb), 1)
        tri = (ci >= ri).astype(BF16)
        dlf = _tri_matmul(tri, dc_ref[...]) + carry[...]
        carry[...] = dlf[0:1, :]
        dz = dlf * jax.nn.sigmoid(-(z_ref[...] + b_ref[...]))
        dz_ref[...] = dz
        db_ref[...] += jnp.sum(dz, axis=0, keepdims=True)

    row = pl.BlockSpec((tb, LANES), lambda i: (nb - 1 - i, 0))
    vec = pl.BlockSpec((1, LANES), lambda i: (0, 0))
    return pl.pallas_call(
        body, grid=(nb,), in_specs=[row, row, vec], out_specs=[row, vec],
        out_shape=[jax.ShapeDtypeStruct((S, LANES), F32), jax.ShapeDtypeStruct((1, LANES), F32)],
        scratch_shapes=[pltpu.VMEM((1, LANES), F32)], compiler_params=_params("arbitrary"), name=name)(dc, z, bf)


def _ffn_gu(name, n, wgu, comm=None, tm=1024):
    S, D = n.shape
    nb = N_DEV // 2

    def body(n_ref, wg_ref, wu_ref, gu_ref, act_ref):
        x = n_ref[...]
        g = _dot(x, wg_ref[...], NN)
        u = _dot(x, wu_ref[...], NN)
        gu_ref[0] = g.astype(BF16)
        gu_ref[1] = u.astype(BF16)
        act_ref[...] = (g * jax.nn.sigmoid(g) * u).astype(BF16)

    return _call(
        name, body, (nb, S // tm),
        [pl.BlockSpec((tm, D), lambda j, i: (i, 0)), pl.BlockSpec((None, D, FF_BLK), lambda j, i: (j, 0, 0)),
         pl.BlockSpec((None, D, FF_BLK), lambda j, i: (j + nb, 0, 0))],
        [pl.BlockSpec((2, None, tm, FF_BLK), lambda j, i: (0, j, i, 0)), pl.BlockSpec((None, tm, FF_BLK), lambda j, i: (j, i, 0))],
        [jax.ShapeDtypeStruct((2, nb, S, FF_BLK), BF16), jax.ShapeDtypeStruct((nb, S, FF_BLK), BF16)], [],
        (n, wgu, wgu), ("parallel", "parallel"), comm)


def _ffn_down(name, act, wd, resid, comm=None, tm=1024):
    nb, S, _ = act.shape
    D = wd.shape[1]

    def epilogue(acc, ex, outs, j):
        outs[0][...] = acc + ex[0][...]

    o_spec = pl.BlockSpec((tm, D), lambda i, j, k: (i, 0))
    return _mm_call(name, (S // tm, 1, nb), act, pl.BlockSpec((None, tm, FF_BLK), lambda i, j, k: (k, i, 0)),
                    wd, pl.BlockSpec((FF_BLK, D), lambda i, j, k: (k, 0)), NN,
                    [jax.ShapeDtypeStruct((S, D), F32)], [o_spec], (tm, D), epilogue, (resid,), (o_spec,), comm=comm)


def _ffn_dact(name, dh, wd, gu, comm=None, tm=512):
    S, D = dh.shape
    nb = N_DEV // 2

    def epilogue(acc, ex, outs, j):
        g = ex[0][0].astype(F32)
        u = ex[0][1].astype(F32)
        sig = jax.nn.sigmoid(g)
        outs[0][0] = (acc * u * (sig * (1.0 + g * (1.0 - sig)))).astype(BF16)
        outs[0][1] = (acc * (g * sig)).astype(BF16)

    gu_spec = pl.BlockSpec((2, None, tm, FF_BLK), lambda j, i, k: (0, j, i, 0))
    return _mm_call(name, (nb, S // tm, 1), dh, pl.BlockSpec((tm, D), lambda j, i, k: (i, 0)),
                    wd, pl.BlockSpec((FF_BLK, D), lambda j, i, k: (j, 0)), NT,
                    [jax.ShapeDtypeStruct((2, nb, S, FF_BLK), BF16)], [gu_spec], (tm, FF_BLK), epilogue, (gu,), (gu_spec,),
                    col_axis=0, comm=comm)


def _ffn_dwgu(name, n, dgu, comm=None, tm=1024, tk=1024):
    S, D = n.shape
    dgu8 = dgu.reshape(N_DEV, S, FF_BLK)
    return _mm_call(name, (N_DEV, D // tm, S // tk), n, pl.BlockSpec((tk, tm), lambda d, i, k: (k, i)),
                    dgu8, pl.BlockSpec((None, tk, FF_BLK), lambda d, i, k: (d, k, 0)), TN,
                    [jax.ShapeDtypeStruct((N_DEV, D, FF_BLK), BF16)],
                    [pl.BlockSpec((None, tm, FF_BLK), lambda d, i, k: (d, i, 0))], (tm, FF_BLK), comm=comm)


def _ffn_dwd(name, act, dh, tk=1024):
    nb, S, _ = act.shape
    D = dh.shape[1]
    out = _mm_call(name, (nb, 1, S // tk), act, pl.BlockSpec((None, tk, FF_BLK), lambda b, j, k: (b, k, 0)),
                   dh, pl.BlockSpec((tk, D), lambda b, j, k: (k, 0)), TN,
                   [jax.ShapeDtypeStruct((nb, FF_BLK, D), BF16)],
                   [pl.BlockSpec((None, FF_BLK, D), lambda b, j, k: (b, 0, 0))], (FF_BLK, D))[0]
    return out.reshape(N_DEV, FF_BLK // 2, D)


def _ffn_dn(name, dgu, wgu, comm=None, tm=1024):
    S = dgu.shape[2]
    D = wgu.shape[1]
    dgu8 = dgu.reshape(N_DEV, S, FF_BLK)
    return _mm_call(name, (S // tm, 1, N_DEV), dgu8, pl.BlockSpec((None, tm, FF_BLK), lambda i, j, k: (k, i, 0)),
                    wgu, pl.BlockSpec((None, D, FF_BLK), lambda i, j, k: (k, 0, 0)), NT,
                    [jax.ShapeDtypeStruct((S, D), F32)], [pl.BlockSpec((tm, D), lambda i, j, k: (i, 0))], (tm, D), comm=comm)


def _adamw(name, parts, w, m, v, tr):
    rows, cols = w.shape
    n_parts = len(parts)
    c1 = 1.0 - ADAM_B1 ** ADAM_STEP
    c2 = 1.0 - ADAM_B2 ** ADAM_STEP

    def body(*refs):
        p_refs = refs[:n_parts]
        w_ref, m_ref, v_ref, g_ref, d_ref, nm_ref, nv_ref = refs[n_parts:]
        g = p_refs[0][...].astype(F32)
        for r in p_refs[1:]:
            g = g + r[...].astype(F32)
        mm = ADAM_B1 * m_ref[...] + (1.0 - ADAM_B1) * g
        vv = ADAM_B2 * v_ref[...] + (1.0 - ADAM_B2) * (g * g)
        g_ref[...] = g
        nm_ref[...] = mm
        nv_ref[...] = vv
        d_ref[...] = -ADAM_LR * ((mm / c1) / (jnp.sqrt(vv / c2) + ADAM_EPS) + ADAM_WD * w_ref[...])

    blk = pl.BlockSpec((tr, cols), lambda i: (i, 0))
    out = jax.ShapeDtypeStruct((rows, cols), F32)
    return pl.pallas_call(
        body, grid=(rows // tr,), in_specs=[blk] * (n_parts + 3), out_specs=[blk] * 4, out_shape=[out] * 4,
        compiler_params=_params("parallel"), name=name)(*parts, w, m, v)


def _position():
    return lax.axis_index("x"), lax.axis_index("y"), lax.axis_index("c")


def _other_chips():
    x, y, _ = _position()
    return [(1 - x, y), (x, 1 - y), (1 - x, 1 - y)]


def _remote(src, dst, send, recv, k, to):
    return pltpu.make_async_remote_copy(src_ref=src, dst_ref=dst, send_sem=send.at[k], recv_sem=recv.at[k],
                                        device_id=to, device_id_type=MESH)


def _ag_send(blocks, direct=False):
    n_peer = 7 if direct else 4

    def copies(ins, outs, send, recv, local, r0=0, l0=0):
        x, y, c = _position()
        me = 4 * x + 2 * y + c
        peers = [(x, y, 1 - c)] + [(px, py, c) for px, py in _other_chips()]
        if direct:
            peers += [(px, py, 1 - c) for px, py in _other_chips()]
        cps = []
        for t, (src, dst) in enumerate(zip(ins, outs)):
            cps.append(pltpu.make_async_copy(src, dst.at[me], local.at[l0 + t]))
            cps += [_remote(src, dst.at[me], send, recv, r0 + n_peer * t + k, to) for k, to in enumerate(peers)]
        return cps

    outs = tuple(jax.ShapeDtypeStruct((N_DEV,) + b.shape, b.dtype) for b in blocks)
    return _Comm(tuple(blocks), outs, {}, copies, n_peer * len(blocks), len(blocks))


def _ag_forward(bufs):
    def copies(ins, outs, send, recv, local, r0=0, l0=0):
        x, y, c = _position()
        cps = []
        for t, buf in enumerate(outs):
            for k, (px, py) in enumerate(_other_chips()):
                slot = buf.at[4 * px + 2 * py + c]
                cps.append(_remote(slot, slot, send, recv, r0 + 3 * t + k, (x, y, 1 - c)))
        return cps

    outs = tuple(jax.ShapeDtypeStruct(b.shape, b.dtype) for b in bufs)
    return _Comm(tuple(bufs), outs, {t: t for t in range(len(bufs))}, copies, 3 * len(bufs), 0)


def _rs_swap(shares):
    def copies(ins, outs, send, recv, local, r0=0, l0=0):
        x, y, c = _position()
        return [_remote(src.at[:, 1 - c], dst, send, recv, r0 + t, (x, y, 1 - c)) for t, (src, dst) in enumerate(zip(ins, outs))]

    ins = tuple(s.reshape((4, 2) + s.shape[1:]) for s in shares)
    outs = tuple(jax.ShapeDtypeStruct((4,) + s.shape[1:], s.dtype) for s in shares)
    return _Comm(ins, outs, {}, copies, len(shares), 0)


def _rs_exchange(sums, only=None):
    ks = range(3) if only is None else (only,)

    def copies(ins, outs, send, recv, local, r0=0, l0=0):
        _, _, c = _position()
        chips = _other_chips()
        return [_remote(src.at[2 * chips[k][0] + chips[k][1]], dst.at[i], send, recv, r0 + len(ks) * t + i, (*chips[k], c))
                for t, (src, dst) in enumerate(zip(ins, outs)) for i, k in enumerate(ks)]

    outs = tuple(jax.ShapeDtypeStruct((len(ks),) + s.shape[1:], s.dtype) for s in sums)
    return _Comm(tuple(sums), outs, {}, copies, len(ks) * len(sums), 0)


def _comm_call(name, comm):
    return _call(name, lambda: None, (), [], [], [], [], (), (), comm)


def _pair_sum(name, share, got, core, tr):
    _, rows, cols = share.shape

    def body(c_ref, a_ref, b_ref, o_ref):
        o_ref[...] = (a_ref[...].astype(F32) + b_ref[...].astype(F32)).astype(o_ref.dtype)

    grid_spec = pltpu.PrefetchScalarGridSpec(
        num_scalar_prefetch=1, grid=(4, rows // tr),
        in_specs=[pl.BlockSpec((None, None, tr, cols), lambda q, i, c: (q, c[0], i, 0)),
                  pl.BlockSpec((None, tr, cols), lambda q, i, c: (q, i, 0))],
        out_specs=pl.BlockSpec((None, tr, cols), lambda q, i, c: (q, i, 0)))
    return pl.pallas_call(
        body, grid_spec=grid_spec, out_shape=jax.ShapeDtypeStruct((4, rows, cols), share.dtype),
        compiler_params=_params("parallel", "parallel"), name=name)(core, share.reshape(4, 2, rows, cols), got)


TENSORS = ("a_w_in", "a_w_out", "b_w_in", "b_w_out", "gu0", "gu1", "dn0", "dn1")
ROW_TILE = {"a_w_in": 256, "a_w_out": 128, "b_w_in": 256, "b_w_out": 128, "gu0": 256, "gu1": 256, "dn0": 176, "dn1": 176}
A_BLK = 9 * D_MODEL // N_DEV
B_BLK = 386
B_IN = 3 * D_MODEL + N_HEADS
B_IN_PAD = 3 * D_MODEL + LANES


def kernel(x, a_norm, a_w_in, a_w_out, b_norm, b_w_in, b_f, b_w_out, ffn_norm, ffn_w_gu, ffn_w_down, final_norm, loss_target, m_a_norm, m_a_w_in, m_a_w_out, m_b_norm, m_b_w_in, m_b_f, m_b_w_out, m_ffn_norm, m_ffn_w_gu, m_ffn_w_down, m_final_norm, v_a_norm, v_a_w_in, v_a_w_out, v_b_norm, v_b_w_in, v_b_f, v_b_w_out, v_ffn_norm, v_ffn_w_gu, v_ffn_w_down, v_final_norm):
    S = x.shape[1]
    xi, yi, ci = _position()
    dev = 4 * xi + 2 * yi + ci
    core = ci.reshape(1).astype(jnp.int32)
    h0, target = x.reshape(S, D_MODEL), loss_target.reshape(S, D_MODEL)

    def shards(a_in, a_out, b_in, b_out, gu, dn):
        return {"a_w_in": a_in[0], "a_w_out": a_out[0], "b_w_in": b_in[0], "b_w_out": b_out[0],
                "gu0": gu[0], "gu1": gu[1], "dn0": dn[0], "dn1": dn[1]}

    w_sh = shards(a_w_in, a_w_out, b_w_in, b_w_out, ffn_w_gu, ffn_w_down)
    m_sh = shards(m_a_w_in, m_a_w_out, m_b_w_in, m_b_w_out, m_ffn_w_gu, m_ffn_w_down)
    v_sh = shards(v_a_w_in, v_a_w_out, v_b_w_in, v_b_w_out, v_ffn_w_gu, v_ffn_w_down)
    wb = {n: w_sh[n].astype(BF16) for n in TENSORS}
    bf_pad = jnp.pad(b_f, ((0, 0), (0, LANES - N_HEADS)))
    tabs = _rope_tables(S)

    g_ain, g_aout = _comm_call("gather_a", _ag_send([wb["a_w_in"], wb["a_w_out"]]))
    dils = [dil for _, dil in DILATED_PATTERNS]
    n0_views, (g_ain, g_aout) = _rms_fwd("rms_a", h0, a_norm[0], dils, _ag_forward([g_ain, g_aout]))
    n0 = n0_views[0]
    w_a_in = g_ain.transpose(1, 0, 2).reshape(D_MODEL, 9 * D_MODEL)
    sends = [[wb["gu0"]], [wb["dn0"], jnp.pad(b_norm, ((0, 7), (0, 0)))], None]
    qkv_a, later = [], []
    for g, dil in enumerate(dils):
        qkv_g, sent = _a_proj("proj_a%d" % g, n0, w_a_in, g, dil, tabs, None if sends[g] is None else _ag_send(sends[g]))
        qkv_a.append(qkv_g)
        later += sent
    cols = [lambda r: r] * 3
    groups = [(g, dil, S // dil, qkv_a[g]) for g, (window, dil) in enumerate(DILATED_PATTERNS)]
    fwd = [_dil_fwd("dil_fwd%d" % g, view, *cols, dil, L, _ag_send([wb["b_w_in"], wb["b_w_out"]]) if g == 0 else None)
           for g, dil, L, view in groups]
    later = list(fwd[0][2:]) + later
    o_views, lse_views = _combine("dil_combine", [f[0] for f in fwd], [f[1] for f in fwd], dils)
    o_a = o_views[0]
    w_a_out = g_aout.reshape(D_MODEL, D_MODEL)
    h1, (g_bin, g_bout, g_gu0, g_dn0, g_bnorm) = _matmul("out_a", o_a, w_a_out, "nn", F32, TM, 1024, 1024, resid=h0,
                                                         comm=_ag_forward(later))

    n1 = _rms_fwd("rms_f0", h1, ffn_norm[0])
    gu0, act0 = _ffn_gu("gu_f0", n1, g_gu0)
    w_dn0 = g_dn0.reshape(D_FF, D_MODEL)
    h2 = _ffn_down("down_f0", act0, w_dn0, h1)[0]

    b_norm_full = g_bnorm[:, 0].reshape(D_MODEL)
    w_b_in = g_bin.transpose(1, 0, 2).reshape(D_MODEL, B_IN)
    w_b_gate = jnp.pad(w_b_in[:, 3 * D_MODEL:], ((0, 0), (0, LANES - N_HEADS)))
    w_b_cat = jnp.concatenate([w_b_in[:, :3 * D_MODEL], w_b_gate], axis=1)
    w_b_out = g_bout.reshape(D_MODEL, D_MODEL)
    n2 = _rms_fwd("rms_b", h2, b_norm_full)
    qkv = _matmul("proj_b", n2, w_b_in[:, :3 * D_MODEL], "nn", BF16, TM, 1024, 1024, col0_scale=SOFTMAX_SCALE)
    z = _matmul("gate_b", n2, w_b_gate, "nn", F32, TM, LANES, 1024)
    kbias = _gate_fwd("gate_cumsum", z, bf_pad)
    tf = min(S, 512)
    o_b, lse_b, g_gu1, g_dn1 = _fox_fwd("fox_fwd", qkv, kbias, tf, _ag_send([wb["gu1"], wb["dn1"]]))
    h3, (g_gu1, g_dn1) = _matmul("out_b", o_b, w_b_out, "nn", F32, TM, 1024, 1024, resid=h2, comm=_ag_forward([g_gu1, g_dn1]))

    w_dn1 = g_dn1.reshape(D_FF, D_MODEL)
    n3 = _rms_fwd("rms_f1", h3, ffn_norm[1])
    gu1, act1 = _ffn_gu("gu_f1", n3, g_gu1)
    h4 = _ffn_down("down_f1", act1, w_dn1, h3)[0]

    dh4, d_final, loss, dh4_16 = _loss_head("loss_head", h4, final_norm, target)

    share, got, sums, others = {}, {}, {}, {}

    def pair_sums(*names):
        for n in names:
            sums[n] = _pair_sum("pair_" + n, share[n], got[n], core, ROW_TILE[n])

    dgu1 = _ffn_dact("dact_f1", dh4_16, w_dn1, gu1)[0]
    share["dn1"] = _ffn_dwd("dwd_f1", act1, dh4_16)
    share["gu1"] = _ffn_dwgu("dwgu_f1", n3, dgu1)[0]
    dn3, got["gu1"], got["dn1"] = _ffn_dn("dn_f1", dgu1, g_gu1, _rs_swap([share["gu1"], share["dn1"]]))
    dh3, d_ffn1, dh3_16 = _rms_bwd("rmsb_f1", dn3, h3, ffn_norm[1], dh4)
    pair_sums("gu1", "dn1")

    do_b = _matmul("dout_b", dh3_16, w_b_out, "nt", BF16, TM, 1024, 1024)
    share["b_w_out"] = _matmul("dwout_b", o_b, dh3_16, "tn", BF16, TM, 1024, 1024).reshape(N_DEV, 128, D_MODEL)
    dq_b, dk_b, dv_b, ds_rowsum, ds_colsum, others["gu1"], others["dn1"] = _fox_bwd(
        "fox_bwd", qkv, kbias, do_b, o_b, lse_b, tf, _rs_exchange([sums["gu1"], sums["dn1"]]))
    dc = ds_rowsum[:, :N_HEADS] - ds_colsum.reshape(N_HEADS, S).T
    dz, d_bf = _gate_bwd("gate_bwd", jnp.pad(dc, ((0, 0), (0, LANES - N_HEADS))), z, bf_pad)
    dproj_b = jnp.concatenate([dq_b, dk_b, dv_b, dz.astype(BF16)], axis=1)
    dw_b_in = _matmul("dwin_b", n2, dproj_b, "tn", BF16, TM, B_IN_PAD // 5, 1024)
    dn2 = _matmul("dn_b", dproj_b, w_b_cat, "nt", F32, TM, 1024, B_IN_PAD // 5)
    dh2, d_bnorm, dh2_16 = _rms_bwd("rmsb_b", dn2, h2, b_norm_full, dh3)
    share["b_w_in"] = dw_b_in[:, :B_IN].reshape(D_MODEL, N_DEV, B_BLK).transpose(1, 0, 2)

    dgu0, got["b_w_in"], got["b_w_out"] = _ffn_dact("dact_f0", dh2_16, w_dn0, gu0, _rs_swap([share["b_w_in"], share["b_w_out"]]))
    share["dn0"] = _ffn_dwd("dwd_f0", act0, dh2_16)
    pair_sums("b_w_in", "b_w_out")
    share["gu0"], others["b_w_in"], others["b_w_out"] = _ffn_dwgu(
        "dwgu_f0", n1, dgu0, _rs_exchange([sums["b_w_in"], sums["b_w_out"]]))
    dn1, got["gu0"], got["dn0"] = _ffn_dn("dn_f0", dgu0, g_gu0, _rs_swap([share["gu0"], share["dn0"]]))
    dh1, d_ffn0, dh1_16 = _rms_bwd("rmsb_f0", dn1, h1, ffn_norm[0], dh2)
    pair_sums("gu0", "dn0")

    do_views = _matmul_nt_views("dout_a", dh1_16, w_a_out, dils)
    share["a_w_out"] = _matmul("dwout_a", o_a, dh1_16, "tn", BF16, TM, 1024, 1024).reshape(N_DEV, 128, D_MODEL)
    pieces = []
    for g, dil, L, view in groups:
        rot = tuple(tb.reshape(L, dil * LANES) for tb in tabs)
        pieces.append(_dil_bwd("dil_bwd%d" % g, view, do_views[g], o_views[g], lse_views[g], rot, dil, L))
    dw0, others["gu0"] = _a_dw("dwin_a0", n0_views[0], pieces[0], dils[0], _rs_exchange([sums["gu0"]]))
    dw1, others["dn0"] = _a_dw("dwin_a1", n0_views[1], pieces[1], dils[1], _rs_exchange([sums["dn0"]]))
    dw2, = _a_dw("dwin_a2", n0_views[2], pieces[2], dils[2], None)
    share["a_w_in"] = jnp.concatenate([dw0, dw1, dw2], axis=1).reshape(D_MODEL, N_DEV, A_BLK).transpose(1, 0, 2)
    got["a_w_in"], got["a_w_out"] = _comm_call("swap_a", _rs_swap([share["a_w_in"], share["a_w_out"]]))
    pair_sums("a_w_in", "a_w_out")
    dn0, parts_in, parts_out = None, [], []
    for g, dil in enumerate(dils):
        dn0, ex_in, ex_out = _a_dn("dn_a%d" % g, pieces[g], dil, w_a_in, g, dn0, _rs_exchange([sums["a_w_in"], sums["a_w_out"]], only=g))
        parts_in.append(ex_in[0])
        parts_out.append(ex_out[0])
    others["a_w_in"], others["a_w_out"] = parts_in, parts_out
    dx, d_anorm = _rms_bwd("rmsb_a", dn0, h0, a_norm[0], dh1, copy16=False)

    misc = jnp.concatenate([d_bf[:, :N_HEADS], loss[:, :1], jnp.zeros((1, D_MODEL - N_HEADS - 1), F32)], axis=1)
    small = jnp.concatenate([d_anorm, d_ffn0, d_ffn1, d_final, d_bnorm, misc, jnp.zeros((2, D_MODEL), F32)], axis=0)
    small_all, = _comm_call("gather_small", _ag_send([small], direct=True))

    outs = {}
    for n in TENSORS:
        mine = lax.dynamic_index_in_dim(sums[n], 2 * xi + yi, axis=0, keepdims=False)
        outs[n] = _adamw("adamw_" + n, [mine] + [others[n][k] for k in range(3)], w_sh[n], m_sh[n], v_sh[n], ROW_TILE[n])

    pad_vec = lambda a: jnp.pad(a, ((0, 0), (0, D_MODEL - a.shape[1])))

    def small_pack(an, fn, fin, bf):
        return jnp.concatenate([an, fn, fin.reshape(1, D_MODEL), jnp.zeros((1, D_MODEL), F32), pad_vec(bf),
                                jnp.zeros((2, D_MODEL), F32)], axis=0)

    sg, sd, sm, sv = _adamw("adamw_small", [small_all[d] for d in range(N_DEV)], small_pack(a_norm, ffn_norm, final_norm, b_f),
                            small_pack(m_a_norm, m_ffn_norm, m_final_norm, m_b_f),
                            small_pack(v_a_norm, v_ffn_norm, v_final_norm, v_b_f), 8)
    g_bn = lax.dynamic_slice(sg[4:5], (0, dev * LANES), (1, LANES))
    bn = _adamw("adamw_b_norm", [g_bn], b_norm, m_b_norm, v_b_norm, 1)

    def tree(i):
        full = lambda name, ref: outs[name][i].reshape(ref.shape)
        sml = (sg, sd, sm, sv)[i]
        return dict(
            a_norm=sml[0:1], a_w_in=full("a_w_in", a_w_in), a_w_out=full("a_w_out", a_w_out), b_norm=bn[i],
            b_w_in=full("b_w_in", b_w_in), b_f=sml[5:6, :N_HEADS], b_w_out=full("b_w_out", b_w_out), ffn_norm=sml[1:3],
            ffn_w_gu=jnp.stack([outs["gu0"][i], outs["gu1"][i]]).reshape(ffn_w_gu.shape),
            ffn_w_down=jnp.stack([outs["dn0"][i], outs["dn1"][i]]).reshape(ffn_w_down.shape), final_norm=sml[3])

    order = ("a_norm", "a_w_in", "a_w_out", "b_norm", "b_w_in", "b_f", "b_w_out", "ffn_norm", "ffn_w_gu", "ffn_w_down", "final_norm")
    result = [sg[5, N_HEADS], dx.reshape(x.shape)]
    for i in range(4):
        t = tree(i)
        result += [t[n] for n in order]
    return tuple(result)
```

```python
import functools
from typing import Callable, NamedTuple

import jax
import jax.numpy as jnp
from jax import lax
from jax.experimental import pallas as pl
from jax.experimental.pallas import tpu as pltpu

F32 = jnp.float32
BF16 = jnp.bfloat16

D_MODEL = 1024
N_HEADS = 16
HEAD_DIM = 64
N_PAIRS = N_HEADS // 2
LANES = 128
DILATED_PATTERNS = ((128, 1), (512, 4), (2048, 16))
BAND_STEPS = 128
ROT_DIM = HEAD_DIM // 4
ROPE_THETA = 500000.0
D_FF = 2816
RMS_EPS = 1e-6
NEG_INF = -1e30
SOFTMAX_SCALE = HEAD_DIM ** -0.5
N_DEV = 8
FF_BLK = 2 * D_FF // N_DEV
ADAM_LR, ADAM_B1, ADAM_B2, ADAM_EPS, ADAM_WD, ADAM_STEP = 0.001, 0.9, 0.999, 1e-08, 0.01, 10
VMEM_LIMIT = 52 * 1024 * 1024
FOX_BWD_VMEM = 60 * 1024 * 1024
TM = 1024
MESH = pl.DeviceIdType.MESH

NN = (((1,), (0,)), ((), ()))
NT = (((1,), (1,)), ((), ()))
TN = (((0,), (0,)), ((), ()))


def _params(*sem, vmem=VMEM_LIMIT):
    return pltpu.CompilerParams(dimension_semantics=sem, vmem_limit_bytes=vmem)


def _dot(a, b, dims):
    return lax.dot_general(a, b, dims, preferred_element_type=F32)


class _Comm(NamedTuple):
    ins: tuple
    outs: tuple
    aliases: dict
    copies: Callable
    n_remote: int
    n_local: int


def _call(name, body, grid, in_specs, out_specs, out_shape, scratch, args, sem, comm=None, vmem=VMEM_LIMIT):
    if comm is None:
        return pl.pallas_call(body, grid=grid, in_specs=in_specs, out_specs=out_specs, out_shape=out_shape,
                              scratch_shapes=scratch, compiler_params=_params(*sem, vmem=vmem), name=name)(*args)
    n_in, n_out = len(in_specs), len(out_specs)
    n_ci, n_co = len(comm.ins), len(comm.outs)
    o0 = n_in + n_ci

    def hosted(*refs):
        c_ins, c_outs = refs[n_in:o0], refs[o0 + n_out:o0 + n_out + n_co]
        sems = refs[-3:]

        def start():
            for cp in comm.copies(c_ins, c_outs, *sems):
                cp.start()

        def wait():
            for cp in comm.copies(c_ins, c_outs, *sems):
                cp.wait()

        if not grid:
            start()
            body()
            wait()
            return
        ids = [pl.program_id(ax) for ax in range(len(grid))]
        pl.when(functools.reduce(jnp.logical_and, [i == 0 for i in ids]))(start)
        body(*refs[:n_in], *refs[o0:o0 + n_out], *refs[o0 + n_out + n_co:-3])
        pl.when(functools.reduce(jnp.logical_and, [i == g - 1 for i, g in zip(ids, grid)]))(wait)

    hbm = pl.BlockSpec(memory_space=pltpu.HBM)
    dma = pltpu.SemaphoreType.DMA
    return pl.pallas_call(
        hosted, grid=grid, in_specs=[*in_specs, *[hbm] * n_ci], out_specs=[*out_specs, *[hbm] * n_co],
        out_shape=[*out_shape, *comm.outs], input_output_aliases={n_in + i: n_out + o for i, o in comm.aliases.items()},
        scratch_shapes=[*scratch, dma((comm.n_remote,)), dma((comm.n_remote,)), dma((max(comm.n_local, 1),))],
        compiler_params=_params(*["arbitrary"] * len(grid), vmem=vmem), name=name)(*args, *comm.ins)


def _mm_call(name, grid, a, a_spec, b, b_spec, dims, out_shapes, out_specs, acc_shape, epilogue=None,
             extras=(), extra_specs=(), col_axis=1, comm=None):
    nk = grid[2]
    n_extra = len(extras)
    n_out = len(out_shapes)

    def finish(res, ex, outs, j):
        if epilogue is None:
            outs[0][...] = res.astype(outs[0].dtype)
        else:
            epilogue(res, ex, outs, j)

    def body(*refs):
        a_ref, b_ref = refs[0], refs[1]
        ex = refs[2:2 + n_extra]
        outs = refs[2 + n_extra:2 + n_extra + n_out]
        j, k = pl.program_id(col_axis), pl.program_id(2)
        part = _dot(a_ref[...].astype(BF16), b_ref[...].astype(BF16), dims)
        if nk == 1:
            finish(part, ex, outs, j)
            return
        acc = refs[-1]

        @pl.when(k == 0)
        def _():
            acc[...] = part

        @pl.when((k > 0) & (k < nk - 1))
        def _():
            acc[...] += part

        @pl.when(k == nk - 1)
        def _():
            finish(acc[...] + part, ex, outs, j)

    return _call(name, body, grid, [a_spec, b_spec, *extra_specs], out_specs, out_shapes,
                 [] if nk == 1 else [pltpu.VMEM(acc_shape, F32)], (a, b, *extras), ("parallel", "parallel", "arbitrary"), comm)


def _matmul(name, a, b, mode, out_dtype, tm, tn, tk, resid=None, col0_scale=None, comm=None):
    if mode == "nn":
        (M, K), N = a.shape, b.shape[1]
        a_spec = pl.BlockSpec((tm, tk), lambda j, i, k: (i, k))
        b_spec = pl.BlockSpec((tk, tn), lambda j, i, k: (k, j))
        dims = NN
    elif mode == "nt":
        (M, K), N = a.shape, b.shape[0]
        a_spec = pl.BlockSpec((tm, tk), lambda j, i, k: (i, k))
        b_spec = pl.BlockSpec((tn, tk), lambda j, i, k: (j, k))
        dims = NT
    else:
        (K, M), N = a.shape, b.shape[1]
        a_spec = pl.BlockSpec((tk, tm), lambda j, i, k: (k, i))
        b_spec = pl.BlockSpec((tk, tn), lambda j, i, k: (k, j))
        dims = TN
    assert M % tm == 0 and N % tn == 0 and K % tk == 0, (name, M, N, K, tm, tn, tk)
    o_spec = pl.BlockSpec((tm, tn), lambda j, i, k: (i, j))
    extras, extra_specs, epilogue = (), (), None
    if resid is not None:
        extras, extra_specs = (resid,), (o_spec,)

        def epilogue(acc, ex, outs, j):
            outs[0][...] = (acc + ex[0][...]).astype(outs[0].dtype)

    elif col0_scale is not None:

        def epilogue(acc, ex, outs, j):
            outs[0][...] = (acc * jnp.where(j == 0, col0_scale, 1.0)).astype(outs[0].dtype)

    res = _mm_call(name, (N // tn, M // tm, K // tk), a, a_spec, b, b_spec, dims, [jax.ShapeDtypeStruct((M, N), out_dtype)],
                   [o_spec], (tm, tn), epilogue, extras, extra_specs, col_axis=0, comm=comm)
    return res[0] if comm is None else (res[0], res[1:])


def _rms_fwd(name, h, gain, dils=(1,), comm=None, tm=512):
    S, D = h.shape

    def body(h_ref, g_ref, *rest):
        x = h_ref[...]
        rstd = lax.rsqrt(jnp.mean(x * x, axis=-1, keepdims=True) + RMS_EPS)
        y = x * rstd * g_ref[...]
        _write_views([y[:, b * LANES:(b + 1) * LANES] for b in range(N_PAIRS)], rest[-1], rest[:-1], dils, tm)

    res = _call(name, body, (S // tm,), [pl.BlockSpec((tm, D), lambda i: (i, 0)), pl.BlockSpec((1, D), lambda i: (0, 0))],
                [_view_spec(tm, R) for R in dils], [jax.ShapeDtypeStruct((S // R, R * D), BF16) for R in dils],
                [pltpu.VMEM((N_PAIRS, tm, LANES), F32)], (h, gain.reshape(1, D)), ("parallel",), comm)
    views = res[0] if len(dils) == 1 else res[:len(dils)]
    return views if comm is None else (views, res[len(dils):])


def _rms_bwd(name, dn, h, gain, dres, copy16=True, tm=512):
    S, D = h.shape

    def body(dn_ref, h_ref, g_ref, r_ref, dh_ref, dg_ref, *dh16_ref):
        x = h_ref[...]
        rstd = lax.rsqrt(jnp.mean(x * x, axis=-1, keepdims=True) + RMS_EPS)
        xhat = x * rstd
        d = dn_ref[...]
        dxhat = d * g_ref[...]
        dh = rstd * (dxhat - xhat * jnp.mean(dxhat * xhat, axis=-1, keepdims=True)) + r_ref[...]
        dh_ref[...] = dh
        if copy16:
            dh16_ref[0][...] = dh.astype(BF16)

        @pl.when(pl.program_id(0) == 0)
        def _():
            dg_ref[...] = jnp.zeros_like(dg_ref)

        dg_ref[...] += jnp.sum(d * xhat, axis=0, keepdims=True)

    row = pl.BlockSpec((tm, D), lambda i: (i, 0))
    vec = pl.BlockSpec((1, D), lambda i: (0, 0))
    return pl.pallas_call(
        body, grid=(S // tm,), in_specs=[row, row, vec, row], out_specs=[row, vec] + [row] * copy16,
        out_shape=[jax.ShapeDtypeStruct((S, D), F32), jax.ShapeDtypeStruct((1, D), F32)] + [jax.ShapeDtypeStruct((S, D), BF16)] * copy16,
        compiler_params=_params("arbitrary"), name=name)(dn, h, gain.reshape(1, D), dres)


def _loss_head(name, h, gain, target, tm=512):
    S, D = h.shape

    def body(h_ref, g_ref, t_ref, dh_ref, dg_ref, loss_ref, dh16_ref):
        x = h_ref[...]
        rstd = lax.rsqrt(jnp.mean(x * x, axis=-1, keepdims=True) + RMS_EPS)
        xhat = x * rstd
        err = xhat * g_ref[...] - t_ref[...]
        dy = err * (1.0 / D)
        dxhat = dy * g_ref[...]
        dh = rstd * (dxhat - xhat * jnp.mean(dxhat * xhat, axis=-1, keepdims=True))
        dh_ref[...] = dh
        dh16_ref[...] = dh.astype(BF16)

        @pl.when(pl.program_id(0) == 0)
        def _():
            dg_ref[...] = jnp.zeros_like(dg_ref)
            loss_ref[...] = jnp.zeros_like(loss_ref)

        dg_ref[...] += jnp.sum(dy * xhat, axis=0, keepdims=True)
        part = 0.5 * jnp.sum(jnp.mean(err * err, axis=-1, keepdims=True), axis=0, keepdims=True)
        loss_ref[...] += jnp.broadcast_to(part, loss_ref.shape)

    row = pl.BlockSpec((tm, D), lambda i: (i, 0))
    vec = pl.BlockSpec((1, D), lambda i: (0, 0))
    return pl.pallas_call(
        body, grid=(S // tm,), in_specs=[row, vec, row], out_specs=[row, vec, pl.BlockSpec((1, LANES), lambda i: (0, 0)), row],
        out_shape=[jax.ShapeDtypeStruct((S, D), F32), jax.ShapeDtypeStruct((1, D), F32),
                   jax.ShapeDtypeStruct((1, LANES), F32), jax.ShapeDtypeStruct((S, D), BF16)],
        compiler_params=_params("arbitrary"), name=name)(h, gain.reshape(1, D), target)


def _rope_tables(S):
    half = ROT_DIM // 2
    inv_freq = ROPE_THETA ** (-jnp.arange(half, dtype=F32) * 2.0 / ROT_DIM)
    ang = jnp.arange(S, dtype=F32)[:, None] * inv_freq[None, :]
    cos, sin = jnp.cos(ang), jnp.sin(ang)
    one = jnp.ones((S, HEAD_DIM - ROT_DIM), F32)
    zero = jnp.zeros((S, HEAD_DIM - ROT_DIM), F32)
    zh = jnp.zeros((S, half), F32)
    c = jnp.concatenate([cos, cos, one], axis=1)
    sa = jnp.concatenate([-sin, zh, zero], axis=1)
    sb = jnp.concatenate([zh, sin, zero], axis=1)
    return tuple(jnp.concatenate([t, t], axis=1) for t in (c, sa, sb))


def _rotate(x, c, sa, sb, sign):
    return x * c + sign * (pltpu.roll(x, LANES - ROT_DIM // 2, 1) * sa + pltpu.roll(x, ROT_DIM // 2, 1) * sb)


def _stage_chunks(scr, chunks):
    for c, x in enumerate(chunks):
        scr[c] = x


def _strided_rows(scr, c, r, n, R):
    return scr.at[c][pl.ds(r, n, stride=R), :]


def _a_proj(name, n, w, g, R, tabs, comm, tm=1024):
    S, D = n.shape
    n_i = S // tm
    n_out = 3

    def body(n_ref, w_ref, c_ref, sa_ref, sb_ref, *rest):
        outs, scr = rest[:n_out], rest[n_out]
        j = pl.program_id(0)
        acc = _dot(n_ref[...], w_ref[...], NN)
        c, sa, sb = c_ref[...], sa_ref[...], sb_ref[...]
        for J in range(n_out):
            kind = J

            @pl.when(j == J)
            def _(J=J, kind=kind):
                chunks = [acc[:, b * LANES:(b + 1) * LANES] for b in range(N_PAIRS)]
                if kind < 2:
                    chunks = [_rotate(x, c, sa, sb, 1.0) * (SOFTMAX_SCALE if kind == 0 else 1.0) for x in chunks]
                if R == 1:
                    for b, x in enumerate(chunks):
                        outs[J][:, b * LANES:(b + 1) * LANES] = x.astype(BF16)
                    return
                _stage_chunks(scr, chunks)
                for r in range(R):
                    for b in range(N_PAIRS):
                        col = r * D_MODEL + b * LANES
                        outs[J][:, col:col + LANES] = _strided_rows(scr, b, r, tm // R, R).astype(BF16)

    def out_spec(J):
        return pl.BlockSpec((tm // R, R * D_MODEL), lambda j, i: (jnp.where(j == J, i, jnp.where(j < J, 0, n_i - 1)), 0))

    tab = pl.BlockSpec((tm, LANES), lambda j, i: (i, 0))
    res = _call(name, body, (n_out, n_i),
                [pl.BlockSpec((tm, D), lambda j, i: (i, 0)), pl.BlockSpec((D, D_MODEL), lambda j, i: (0, 3 * g + j)), tab, tab, tab],
                [out_spec(J) for J in range(n_out)], [jax.ShapeDtypeStruct((S // R, R * D_MODEL), BF16)] * n_out,
                [pltpu.VMEM((N_PAIRS, tm, LANES), F32)], (n, w, *tabs), ("arbitrary", "arbitrary"), comm)
    return res[:n_out], res[n_out:]


def _unstride(src_chunk, R, tok, rows):
    for r in range(R):
        for b in range(N_PAIRS):
            tok.at[b][pl.ds(r, rows // R, stride=R), :] = src_chunk(r, b).astype(F32)


def _by_residue(ref, R):
    return ref[...] if R == 1 else jnp.concatenate([ref[:, r * D_MODEL:(r + 1) * D_MODEL] for r in range(R)], axis=0)


def _a_dw(name, n_view, pieces, R, comm, tk=1024):
    D = D_MODEL
    S = n_view.shape[0] * R
    n_k = S // tk
    n_p = len(pieces)

    def body(n_ref, *rest):
        p_refs, o_ref, acc = rest[:n_p], rest[n_p], rest[n_p + 1]
        j, k = pl.program_id(0), pl.program_id(1)
        for J in range(n_p):

            @pl.when(j == J)
            def _(J=J):
                part = _dot(_by_residue(n_ref, R), _by_residue(p_refs[J], R), TN)

                @pl.when(k == 0)
                def _():
                    acc[...] = part

                @pl.when((k > 0) & (k < n_k - 1))
                def _():
                    acc[...] += part

                @pl.when(k == n_k - 1)
                def _():
                    o_ref[...] = (acc[...] + part).astype(BF16)

    def piece_spec(J):
        return pl.BlockSpec((tk // R, R * D_MODEL), lambda j, k: (jnp.where(j == J, k, jnp.where(j < J, 0, n_k - 1)), 0))

    return _call(name, body, (n_p, n_k), [pl.BlockSpec((tk // R, R * D_MODEL), lambda j, k: (k, 0))] + [piece_spec(J) for J in range(n_p)],
                 [pl.BlockSpec((D, D_MODEL), lambda j, k: (0, j))], [jax.ShapeDtypeStruct((D, n_p * D_MODEL), BF16)],
                 [pltpu.VMEM((D, D_MODEL), F32)], (n_view, *pieces), ("arbitrary", "arbitrary"), comm)


def _a_dn(name, pieces, dils, w, comm, tm=512):
    D = w.shape[0]
    S = pieces[0].shape[0] * dils[0]
    n_p = len(pieces)

    def body(*refs):
        p_refs, w_ref, o_ref, acc, part_acc, tok = refs[:n_p], refs[n_p], refs[n_p + 1], refs[n_p + 2], refs[n_p + 3], refs[n_p + 4]
        j = pl.program_id(1)
        for J in range(n_p):

            @pl.when(j == J)
            def _(J=J):
                R, t = dils[J // 3], J % 3
                part = _dot(_by_residue(p_refs[J], R), w_ref[...], NT)
                if R == 1:
                    if J == 0:
                        acc[...] = part
                    else:
                        acc[...] += part
                    return
                if t == 0:
                    part_acc[...] = part
                    return
                if t == 1:
                    part_acc[...] += part
                    return
                n = tm // R
                _unstride(lambda r, b: part_acc[r * n:(r + 1) * n, b * LANES:(b + 1) * LANES]
                          + part[r * n:(r + 1) * n, b * LANES:(b + 1) * LANES], R, tok, tm)
                total = acc[...] + jnp.concatenate([tok[b] for b in range(N_PAIRS)], axis=1)
                if J == n_p - 1:
                    o_ref[...] = total
                else:
                    acc[...] = total

    specs = [pl.BlockSpec((tm // dils[J // 3], dils[J // 3] * D_MODEL), lambda i, j: (i, 0)) for J in range(n_p)]
    return _call(name, body, (S // tm, n_p), specs + [pl.BlockSpec((D, D_MODEL), lambda i, j: (0, j))],
                 [pl.BlockSpec((tm, D), lambda i, j: (i, 0))], [jax.ShapeDtypeStruct((S, D), F32)],
                 [pltpu.VMEM((tm, D), F32), pltpu.VMEM((tm, D), F32), pltpu.VMEM((N_PAIRS, tm, LANES), F32)], (*pieces, w),
                 ("arbitrary", "arbitrary"), comm)


def _lo_lanes():
    return lax.broadcasted_iota(jnp.int32, (1, LANES), 1) < HEAD_DIM


def _rep_rows(x2, lo):
    sw = pltpu.roll(x2, HEAD_DIM, 1)
    return jnp.where(lo, x2, sw), jnp.where(lo, sw, x2)


def _pair_cols(h):
    return slice((h // 2) * LANES, (h // 2 + 1) * LANES)


def _head_lanes(lo, h):
    return lo if h % 2 == 0 else jnp.logical_not(lo)


def _band_masks(t, first):
    ri = lax.broadcasted_iota(jnp.int32, (t, t), 0)
    ci = lax.broadcasted_iota(jnp.int32, (t, t), 1)
    neg_prev = jnp.where((ci >= ri) & jnp.logical_not(first), 0.0, NEG_INF)
    neg_cur = jnp.where(ci <= ri, 0.0, NEG_INF)
    return neg_prev, neg_cur


def _dil_specs(L, R, t, qcol, kcol, vcol):
    W = D_MODEL
    prev = lambda qi: jnp.maximum(qi - 1, 0)
    return dict(
        q=pl.BlockSpec((t, W), lambda r, qi: (qi, qcol(r))),
        kp=pl.BlockSpec((t, W), lambda r, qi: (prev(qi), kcol(r))), kc=pl.BlockSpec((t, W), lambda r, qi: (qi, kcol(r))),
        vp=pl.BlockSpec((t, W), lambda r, qi: (prev(qi), vcol(r))), vc=pl.BlockSpec((t, W), lambda r, qi: (qi, vcol(r))),
        own=pl.BlockSpec((t, W), lambda r, qi: (qi, r)), tab=pl.BlockSpec((t, LANES), lambda r, qi: (qi, r)))


def _dil_fwd(name, x, qcol, kcol, vcol, R, L, comm=None):
    t = BAND_STEPS
    W = D_MODEL
    sp = _dil_specs(L, R, t, qcol, kcol, vcol)

    def body(q_ref, kp_ref, kc_ref, vp_ref, vc_ref, o_ref, lse_ref):
        lo = _lo_lanes()
        neg_p, neg_c = _band_masks(t, pl.program_id(1) == 0)
        s_p, s_c = [], []
        for h in range(N_HEADS):
            cols = _pair_cols(h)
            qh = jnp.where(_head_lanes(lo, h), q_ref[:, cols], 0)
            s_p.append(_dot(qh, kp_ref[:, cols], NT))
            s_c.append(_dot(qh, kc_ref[:, cols], NT))
        s_p = jnp.stack(s_p) + neg_p[None]
        s_c = jnp.stack(s_c) + neg_c[None]
        m = jnp.maximum(jnp.max(s_p, axis=2, keepdims=True), jnp.max(s_c, axis=2, keepdims=True))
        p_p, p_c = jnp.exp(s_p - m), jnp.exp(s_c - m)
        l = jnp.sum(p_p, axis=2, keepdims=True) + jnp.sum(p_c, axis=2, keepdims=True)
        inv, lse = 1.0 / l, m + jnp.log(l)
        p_p, p_c = p_p.astype(BF16), p_c.astype(BF16)
        for p in range(N_PAIRS):
            cols = _pair_cols(2 * p)
            o2 = jnp.zeros((t, LANES), F32)
            for h in (2 * p, 2 * p + 1):
                hm = _head_lanes(lo, h)
                pv = _dot(p_p[h], jnp.where(hm, vp_ref[:, cols], 0), NN) + _dot(p_c[h], jnp.where(hm, vc_ref[:, cols], 0), NN)
                o2 = o2 + pv * inv[h]
            o_ref[:, cols] = o2
            lse_ref[:, cols] = jnp.where(lo, lse[2 * p], lse[2 * p + 1])

    return _call(name, body, (R, L // t), [sp["q"], sp["kp"], sp["kc"], sp["vp"], sp["vc"]], [sp["own"], sp["own"]],
                 [jax.ShapeDtypeStruct((L, R * W), F32), jax.ShapeDtypeStruct((L, R * W), F32)], [],
                 (x[0], x[1], x[1], x[2], x[2]), ("parallel", "parallel"), comm)


def _dil_scores(lo, q_ref, do_ref, o_ref, lse_ref, kv_refs):
    s = [[] for _ in kv_refs]
    dp = [[] for _ in kv_refs]
    lse, d = [], []
    for h in range(N_HEADS):
        cols = _pair_cols(h)
        hm = _head_lanes(lo, h)
        qh, doh = jnp.where(hm, q_ref[:, cols], 0), jnp.where(hm, do_ref[:, cols], 0)
        for i, (k_ref, v_ref) in enumerate(kv_refs):
            s[i].append(_dot(qh, k_ref[:, cols], NT))
            dp[i].append(_dot(doh, v_ref[:, cols], NT))
        lse.append(_rep_rows(lse_ref[:, cols], lo)[h % 2])
        dd = do_ref[:, cols].astype(F32) * o_ref[:, cols].astype(F32)
        d.append(jnp.sum(jnp.where(hm, dd, 0.0), axis=1, keepdims=True))
    return (*[jnp.stack(x) for x in s], *[jnp.stack(x) for x in dp], jnp.stack(lse), jnp.stack(d))


def _dil_bwd(name, x, do, o, lse, tabs, R, L, comm=None):
    t = BAND_STEPS
    W = D_MODEL
    nq = L // t
    qb = lambda step: nq - 1 - step
    kb = lambda step: jnp.maximum(qb(step) - 1, 0)
    at = lambda f, width: pl.BlockSpec((t, width), lambda r, step: (f(step), r))

    def body(q_ref, kp_ref, kc_ref, vp_ref, vc_ref, do_ref, o_ref, lse_ref, c_ref, sa_ref, sb_ref, dq_ref, dk_ref, dv_ref,
             dk_scr, dv_scr):
        qi = nq - 1 - pl.program_id(1)
        lo = _lo_lanes()
        unrotate = lambda x: _rotate(x, c_ref[...], sa_ref[...], sb_ref[...], -1.0).astype(BF16)

        @pl.when(qi == nq - 1)
        def _():
            dk_scr[...] = jnp.zeros_like(dk_scr)
            dv_scr[...] = jnp.zeros_like(dv_scr)

        neg_p, neg_c = _band_masks(t, qi == 0)
        s_p, s_c, dp_p, dp_c, lse_h, d = _dil_scores(lo, q_ref, do_ref, o_ref, lse_ref, ((kp_ref, vp_ref), (kc_ref, vc_ref)))
        p_p, p_c = jnp.exp(s_p + neg_p[None] - lse_h), jnp.exp(s_c + neg_c[None] - lse_h)
        ds_p, ds_c = (p_p * (dp_p - d)).astype(BF16), (p_c * (dp_c - d)).astype(BF16)
        p_p, p_c = p_p.astype(BF16), p_c.astype(BF16)
        for p in range(N_PAIRS):
            cols = _pair_cols(2 * p)
            dq2 = jnp.zeros((t, LANES), F32)
            dk_cur, dv_cur = dk_scr[:, cols], dv_scr[:, cols]
            dk_prev, dv_prev = jnp.zeros((t, LANES), F32), jnp.zeros((t, LANES), F32)
            for h in (2 * p, 2 * p + 1):
                hm = _head_lanes(lo, h)
                qh, doh = jnp.where(hm, q_ref[:, cols], 0), jnp.where(hm, do_ref[:, cols], 0)
                dq2 = dq2 + _dot(ds_p[h], jnp.where(hm, kp_ref[:, cols], 0), NN) + _dot(ds_c[h], jnp.where(hm, kc_ref[:, cols], 0), NN)
                dk_prev, dv_prev = dk_prev + _dot(ds_p[h], qh, TN), dv_prev + _dot(p_p[h], doh, TN)
                dk_cur, dv_cur = dk_cur + _dot(ds_c[h], qh, TN), dv_cur + _dot(p_c[h], doh, TN)
            dq_ref[:, cols] = unrotate(dq2 * SOFTMAX_SCALE)
            dk_ref[:, cols] = unrotate(dk_cur)
            dv_ref[:, cols] = dv_cur.astype(BF16)
            dk_scr[:, cols] = dk_prev
            dv_scr[:, cols] = dv_prev

    wide = jax.ShapeDtypeStruct((L, R * W), BF16)
    return _call(name, body, (R, nq),
                 [at(qb, W), at(kb, W), at(qb, W), at(kb, W), at(qb, W), at(qb, W), at(qb, W), at(qb, W),
                  at(qb, LANES), at(qb, LANES), at(qb, LANES)],
                 [at(qb, W), at(qb, W), at(qb, W)], [wide, wide, wide], [pltpu.VMEM((t, W), F32), pltpu.VMEM((t, W), F32)],
                 (x[0], x[1], x[1], x[2], x[2], do, o, lse, *tabs), ("parallel", "arbitrary"), comm)


def _fox_operands(q2, k2, kb2, lo, hh):
    lane = lax.broadcasted_iota(jnp.int32, (1, LANES), 1)
    if hh == 0:
        ones = ((lane >= HEAD_DIM) & (lane < HEAD_DIM + 3)).astype(BF16)
        return jnp.where(lo, q2, ones), jnp.where(lo, k2, kb2)
    ones = (lane < 3).astype(BF16)
    return jnp.where(lo, ones, q2), jnp.where(lo, kb2, k2)


def _causal_neg(t):
    ri = lax.broadcasted_iota(jnp.int32, (t, t), 0)
    ci = lax.broadcasted_iota(jnp.int32, (t, t), 1)
    return jnp.where(ci <= ri, 0.0, NEG_INF)


def _fox_fwd(name, qkv, kbias, t, comm=None):
    S = qkv.shape[0]
    W = D_MODEL
    nq = S // t
    rep = t // LANES

    def body(q_ref, k_ref, v_ref, kb_ref, o_ref, lse_ref, m_scr, l_scr, acc_scr):
        qi, j = pl.program_id(0), pl.program_id(1)
        lo = _lo_lanes()

        @pl.when(j == 0)
        def _():
            m_scr[...] = jnp.full_like(m_scr, NEG_INF)
            l_scr[...] = jnp.zeros_like(l_scr)
            acc_scr[...] = jnp.zeros_like(acc_scr)

        def step(masked):
            neg = _causal_neg(t) if masked else None

            def pair(p, carry):
                cs = pl.ds(pl.multiple_of(p * LANES, LANES), LANES)
                q2, k2, v2, kb2 = q_ref[:, cs], k_ref[:, cs], v_ref[:, cs], kb_ref[:, cs]
                pvs, alphas = [], []
                for hh in range(2):
                    hm = lo if hh == 0 else jnp.logical_not(lo)
                    qh, kh = _fox_operands(q2, k2, kb2, lo, hh)
                    s = _dot(qh, kh, NT)
                    if masked:
                        s = s + neg
                    h = 2 * p + hh
                    m_prev = m_scr[h]
                    m_new = jnp.maximum(m_prev, jnp.max(s, axis=1, keepdims=True))
                    pe = jnp.exp(s - jnp.tile(m_new, (1, rep)))
                    alpha = jnp.exp(m_prev - m_new)
                    l_scr[h] = alpha * l_scr[h] + jnp.sum(pe, axis=1, keepdims=True)
                    m_scr[h] = m_new
                    pvs.append(_dot(pe.astype(BF16), jnp.where(hm, v2, 0), NN))
                    alphas.append(alpha)
                acc_scr[:, cs] = acc_scr[:, cs] * jnp.where(lo, alphas[0], alphas[1]) + pvs[0] + pvs[1]
                return carry

            lax.fori_loop(0, N_PAIRS, pair, 0, unroll=4)

        @pl.when(j < qi)
        def _():
            step(False)

        @pl.when(j == qi)
        def _():
            step(True)

        @pl.when(j == nq - 1)
        def _():
            for p in range(N_PAIRS):
                cols = slice(p * LANES, (p + 1) * LANES)
                l2 = jnp.where(lo, l_scr[2 * p], l_scr[2 * p + 1])
                m2 = jnp.where(lo, m_scr[2 * p], m_scr[2 * p + 1])
                o_ref[:, cols] = (acc_scr[:, cols] / l2).astype(BF16)
                lse_ref[:, cols] = m2 + jnp.log(l2)

    kv = lambda col: pl.BlockSpec((t, W), lambda qi, j: (jnp.minimum(j, qi), col))
    own = pl.BlockSpec((t, W), lambda qi, j: (qi, 0))
    return _call(name, body, (nq, nq), [own, kv(1), kv(2), kv(0)], [own, own],
                 [jax.ShapeDtypeStruct((S, W), BF16), jax.ShapeDtypeStruct((S, W), F32)],
                 [pltpu.VMEM((N_HEADS, t, LANES), F32), pltpu.VMEM((N_HEADS, t, LANES), F32), pltpu.VMEM((t, W), F32)],
                 (qkv, qkv, qkv, kbias), ("parallel", "arbitrary"), comm)


def _fox_head_grads(qh, kh, v2, doh, neg, lse_h, d_h, rep):
    s = _dot(qh, kh, NT)
    if neg is not None:
        s = s + neg
    p = jnp.exp(s - jnp.tile(lse_h, (1, rep)))
    return p, p * (_dot(doh, v2, NT) - d_h)


def _fox_bwd(name, qkv, kbias, do, o, lse, t, comm=None):
    S = qkv.shape[0]
    W = D_MODEL
    nq = S // t
    rep = t // LANES

    def body(q_ref, k_ref, v_ref, kb_ref, do_ref, o_ref, lse_ref, dq_ref, dk_ref, dv_ref, rs_ref, dc_ref, dq_scr, dk_scr, dv_scr):
        kb, j = pl.program_id(0), pl.program_id(1)
        lo = _lo_lanes()
        lane = lax.broadcasted_iota(jnp.int32, (1, LANES), 1)
        rows = pl.ds(pl.multiple_of(j * t, t), t)

        @pl.when((kb == 0) & (j == 0))
        def _():
            dq_scr[...] = jnp.zeros_like(dq_scr)
            rs_ref[...] = jnp.zeros_like(rs_ref)

        @pl.when(j == 0)
        def _():
            dk_scr[...] = jnp.zeros_like(dk_scr)
            dv_scr[...] = jnp.zeros_like(dv_scr)
            dc_ref[...] = jnp.zeros_like(dc_ref)

        def step(masked):
            neg = _causal_neg(t) if masked else None

            def pair(p, carry):
                cs = pl.ds(pl.multiple_of(p * LANES, LANES), LANES)
                q2, k2, v2, kb2, do2 = q_ref[:, cs], k_ref[:, cs], v_ref[:, cs], kb_ref[:, cs], do_ref[:, cs]
                dd = do2.astype(F32) * o_ref[:, cs].astype(F32)
                lse_h = _rep_rows(lse_ref[:, cs], lo)
                dq2 = jnp.zeros((t, LANES), F32)
                dv2 = jnp.zeros((t, LANES), F32)
                dk2 = jnp.zeros((t, LANES), F32)
                for hh in range(2):
                    hm = lo if hh == 0 else jnp.logical_not(lo)
                    qh, kh = _fox_operands(q2, k2, kb2, lo, hh)
                    doh = jnp.where(hm, do2, 0)
                    d_h = jnp.sum(jnp.where(hm, dd, 0.0), axis=1, keepdims=True)
                    pr, ds = _fox_head_grads(qh, kh, v2, doh, neg, lse_h[hh], d_h, rep)
                    rs_ref[rows, :] += jnp.where(lane == 2 * p + hh, jnp.sum(ds, axis=1, keepdims=True), 0.0)
                    dc_ref[p, hh:hh + 1, :] += jnp.sum(ds, axis=0, keepdims=True)
                    dsb = ds.astype(BF16)
                    dv2 = dv2 + _dot(pr.astype(BF16), doh, TN)
                    dk2 = dk2 + _dot(dsb, jnp.where(hm, q2, 0), TN)
                    dq2 = dq2 + _dot(dsb, jnp.where(hm, k2, 0), NN)
                dv_scr[:, cs] += dv2
                dk_scr[:, cs] += dk2
                dq_scr[rows, cs] += dq2
                return carry

            lax.fori_loop(0, N_PAIRS, pair, 0, unroll=4)
            if masked:
                dq_ref[...] = (dq_scr[rows, :] * SOFTMAX_SCALE).astype(BF16)

        @pl.when(j > kb)
        def _():
            step(False)

        @pl.when(j == kb)
        def _():
            step(True)

        @pl.when(j == nq - 1)
        def _():
            dv_ref[...] = dv_scr[...].astype(BF16)
            dk_ref[...] = dk_scr[...].astype(BF16)

    qrow = pl.BlockSpec((t, W), lambda kb, j: (jnp.maximum(j, kb), 0))
    krow = lambda col: pl.BlockSpec((t, W), lambda kb, j: (kb, col))
    own = pl.BlockSpec((t, W), lambda kb, j: (kb, 0))
    wide = jax.ShapeDtypeStruct((S, W), BF16)
    return _call(name, body, (nq, nq), [qrow, krow(1), krow(2), krow(0), qrow, qrow, qrow],
                 [own, own, own, pl.BlockSpec((S, LANES), lambda kb, j: (0, 0)), pl.BlockSpec((N_PAIRS, 2, t), lambda kb, j: (0, 0, kb))],
                 [wide, wide, wide, jax.ShapeDtypeStruct((S, LANES), F32), jax.ShapeDtypeStruct((N_PAIRS, 2, S), F32)],
                 [pltpu.VMEM((S, W), F32), pltpu.VMEM((t, W), F32), pltpu.VMEM((t, W), F32)],
                 (qkv, qkv, qkv, kbias, do, o, lse), ("arbitrary", "arbitrary"), comm, vmem=FOX_BWD_VMEM)


def _view_spec(tm, R, index=lambda i: (i, 0)):
    return pl.BlockSpec((tm // R, R * D_MODEL), index)


def _matmul_nt_views(name, a, w, dils, tm=512):
    S, K = a.shape

    def body(a_ref, w_ref, *rest):
        res = _dot(a_ref[...].astype(BF16), w_ref[...], NT)
        _write_views([res[:, b * LANES:(b + 1) * LANES] for b in range(N_PAIRS)], rest[-1], rest[:-1], dils, tm)

    return pl.pallas_call(
        body, grid=(S // tm,), in_specs=[pl.BlockSpec((tm, K), lambda i: (i, 0)), pl.BlockSpec((D_MODEL, K), lambda i: (0, 0))],
        out_specs=[_view_spec(tm, R) for R in dils],
        out_shape=[jax.ShapeDtypeStruct((S // R, R * D_MODEL), BF16) for R in dils],
        scratch_shapes=[pltpu.VMEM((N_PAIRS, tm, LANES), F32)], compiler_params=_params("parallel"), name=name)(a, w)


def _write_views(chunks, scr, out_refs, dils, tm):
    if any(R > 1 for R in dils):
        _stage_chunks(scr, chunks)
    for ref, R in zip(out_refs, dils):
        for b, x in enumerate(chunks):
            if R == 1:
                ref[:, b * LANES:(b + 1) * LANES] = x.astype(ref.dtype)
                continue
            for r in range(R):
                col = r * D_MODEL + b * LANES
                ref[:, col:col + LANES] = _strided_rows(scr, b, r, tm // R, R).astype(ref.dtype)


def _combine(name, os_, lses, dils, tm=256):
    S = os_[0].shape[0] * dils[0]
    G = len(dils)

    def body(*refs):
        o_refs, l_refs = refs[:G], refs[G:2 * G]
        o_outs, l_outs = refs[2 * G:3 * G], refs[3 * G:4 * G]
        stage = refs[4 * G:]
        for g, R in enumerate(dils):
            if R == 1:
                continue
            for src, dst in ((o_refs[g], stage[2 * g]), (l_refs[g], stage[2 * g + 1])):
                _unstride(lambda r, b, src=src: src[:, r * D_MODEL + b * LANES:r * D_MODEL + (b + 1) * LANES], R, dst, tm)
        o_chunks, l_chunks = [], []
        for b in range(N_PAIRS):
            cols = slice(b * LANES, (b + 1) * LANES)
            os_b = [o_refs[g][:, cols] if R == 1 else stage[2 * g][b] for g, R in enumerate(dils)]
            ls = [l_refs[g][:, cols] if R == 1 else stage[2 * g + 1][b] for g, R in enumerate(dils)]
            m = functools.reduce(jnp.maximum, ls)
            ws = [jnp.exp(l - m) for l in ls]
            den = functools.reduce(jnp.add, ws)
            o_chunks.append(functools.reduce(jnp.add, [w * o for w, o in zip(ws, os_b)]) / den)
            l_chunks.append(m + jnp.log(den))
        _write_views(o_chunks, stage[0], o_outs, dils, tm)
        _write_views(l_chunks, stage[1], l_outs, dils, tm)

    specs = [_view_spec(tm, R) for R in dils]
    shapes = lambda dt: [jax.ShapeDtypeStruct((S // R, R * D_MODEL), dt) for R in dils]
    res = pl.pallas_call(
        body, grid=(S // tm,), in_specs=specs * 2, out_specs=specs * 2, out_shape=shapes(BF16) + shapes(F32),
        scratch_shapes=[pltpu.VMEM((N_PAIRS, tm, LANES), F32)] * (2 * G), compiler_params=_params("parallel"),
        name=name)(*os_, *lses)
    return res[:G], res[G:]


def _tri_matmul(tri, x):
    hi, mid, lo = _split3(x)
    return _dot(tri, hi, NN) + _dot(tri, mid, NN) + _dot(tri, lo, NN)


def _split3(x):
    hi = x.astype(BF16)
    r1 = x - hi.astype(F32)
    mid = r1.astype(BF16)
    return hi, mid, (r1 - mid.astype(F32)).astype(BF16)


def _gate_fwd(name, z, bf, tb=512):
    S = z.shape[0]

    def body(z_ref, b_ref, kb_ref, carry):
        @pl.when(pl.program_id(0) == 0)
        def _():
            carry[...] = jnp.zeros_like(carry)

        lf = jax.nn.log_sigmoid(z_ref[...] + b_ref[...])
        ri = lax.broadcasted_iota(jnp.int32, (tb, tb), 0)
        ci = lax.broadcasted_iota(jnp.int32, (tb, tb), 1)
        tri = (ci <= ri).astype(BF16)
        c = _tri_matmul(tri, lf) + carry[...]
        carry[...] = c[tb - 1:tb, :]
        head = lax.broadcasted_iota(jnp.int32, (LANES, D_MODEL), 0)
        col = lax.broadcasted_iota(jnp.int32, (LANES, D_MODEL), 1)
        base = (head >> 1) * LANES + jnp.where((head & 1) == 0, HEAD_DIM, 0)
        kb = jnp.zeros((tb, D_MODEL), F32)
        for i, piece in enumerate(_split3(-c)):
            place = ((col == base + i) & (head < N_HEADS)).astype(BF16)
            kb = kb + _dot(piece, place, NN)
        kb_ref[...] = kb.astype(BF16)

    row = pl.BlockSpec((tb, LANES), lambda i: (i, 0))
    return pl.pallas_call(
        body, grid=(S // tb,), in_specs=[row, pl.BlockSpec((1, LANES), lambda i: (0, 0))],
        out_specs=pl.BlockSpec((tb, D_MODEL), lambda i: (i, 0)), out_shape=jax.ShapeDtypeStruct((S, D_MODEL), BF16),
        scratch_shapes=[pltpu.VMEM((1, LANES), F32)], compiler_params=_params("arbitrary"), name=name)(z, bf)


def _gate_bwd(name, dc, z, bf, tb=512):
    S = z.shape[0]
    nb = S // tb

    def body(dc_ref, z_ref, b_ref, dz_ref, db_ref, carry):
        @pl.when(pl.program_id(0) == 0)
        def _():
            carry[...] = jnp.zeros_like(carry)
            db_ref[...] = jnp.zeros_like(db_ref)

        ri = lax.broadcasted_iota(jnp.int32, (tb, tb), 0)
        ci = lax.broadcasted_iota(jnp.int32, (tb, tb), 1)
        tri = (ci >= ri).astype(BF16)
        dlf = _tri_matmul(tri, dc_ref[...]) + carry[...]
        carry[...] = dlf[0:1, :]
        dz = dlf * jax.nn.sigmoid(-(z_ref[...] + b_ref[...]))
        dz_ref[...] = dz
        db_ref[...] += jnp.sum(dz, axis=0, keepdims=True)

    row = pl.BlockSpec((tb, LANES), lambda i: (nb - 1 - i, 0))
    vec = pl.BlockSpec((1, LANES), lambda i: (0, 0))
    return pl.pallas_call(
        body, grid=(nb,), in_specs=[row, row, vec], out_specs=[row, vec],
        out_shape=[jax.ShapeDtypeStruct((S, LANES), F32), jax.ShapeDtypeStruct((1, LANES), F32)],
        scratch_shapes=[pltpu.VMEM((1, LANES), F32)], compiler_params=_params("arbitrary"), name=name)(dc, z, bf)


def _ffn_gu(name, n, wgu, comm=None, tm=1024):
    S, D = n.shape
    nb = N_DEV // 2

    def body(n_ref, wg_ref, wu_ref, gu_ref, act_ref):
        x = n_ref[...]
        g = _dot(x, wg_ref[...], NN)
        u = _dot(x, wu_ref[...], NN)
        gu_ref[0] = g.astype(BF16)
        gu_ref[1] = u.astype(BF16)
        act_ref[...] = (g * jax.nn.sigmoid(g) * u).astype(BF16)

    return _call(
        name, body, (nb, S // tm),
        [pl.BlockSpec((tm, D), lambda j, i: (i, 0)), pl.BlockSpec((None, D, FF_BLK), lambda j, i: (j, 0, 0)),
         pl.BlockSpec((None, D, FF_BLK), lambda j, i: (j + nb, 0, 0))],
        [pl.BlockSpec((2, None, tm, FF_BLK), lambda j, i: (0, j, i, 0)), pl.BlockSpec((None, tm, FF_BLK), lambda j, i: (j, i, 0))],
        [jax.ShapeDtypeStruct((2, nb, S, FF_BLK), BF16), jax.ShapeDtypeStruct((nb, S, FF_BLK), BF16)], [],
        (n, wgu, wgu), ("parallel", "parallel"), comm)


def _ffn_down(name, act, wd, resid, comm=None, tm=1024):
    nb, S, _ = act.shape
    D = wd.shape[1]

    def epilogue(acc, ex, outs, j):
        outs[0][...] = acc + ex[0][...]

    o_spec = pl.BlockSpec((tm, D), lambda i, j, k: (i, 0))
    return _mm_call(name, (S // tm, 1, nb), act, pl.BlockSpec((None, tm, FF_BLK), lambda i, j, k: (k, i, 0)),
                    wd, pl.BlockSpec((FF_BLK, D), lambda i, j, k: (k, 0)), NN,
                    [jax.ShapeDtypeStruct((S, D), F32)], [o_spec], (tm, D), epilogue, (resid,), (o_spec,), comm=comm)


def _ffn_dact(name, dh, wd, gu, comm=None, tm=512):
    S, D = dh.shape
    nb = N_DEV // 2

    def epilogue(acc, ex, outs, j):
        g = ex[0][0].astype(F32)
        u = ex[0][1].astype(F32)
        sig = jax.nn.sigmoid(g)
        outs[0][0] = (acc * u * (sig * (1.0 + g * (1.0 - sig)))).astype(BF16)
        outs[0][1] = (acc * (g * sig)).astype(BF16)

    gu_spec = pl.BlockSpec((2, None, tm, FF_BLK), lambda j, i, k: (0, j, i, 0))
    return _mm_call(name, (nb, S // tm, 1), dh, pl.BlockSpec((tm, D), lambda j, i, k: (i, 0)),
                    wd, pl.BlockSpec((FF_BLK, D), lambda j, i, k: (j, 0)), NT,
                    [jax.ShapeDtypeStruct((2, nb, S, FF_BLK), BF16)], [gu_spec], (tm, FF_BLK), epilogue, (gu,), (gu_spec,),
                    col_axis=0, comm=comm)


def _ffn_dwgu(name, n, dgu, comm=None, tm=1024, tk=1024):
    S, D = n.shape
    dgu8 = dgu.reshape(N_DEV, S, FF_BLK)
    return _mm_call(name, (N_DEV, D // tm, S // tk), n, pl.BlockSpec((tk, tm), lambda d, i, k: (k, i)),
                    dgu8, pl.BlockSpec((None, tk, FF_BLK), lambda d, i, k: (d, k, 0)), TN,
                    [jax.ShapeDtypeStruct((N_DEV, D, FF_BLK), BF16)],
                    [pl.BlockSpec((None, tm, FF_BLK), lambda d, i, k: (d, i, 0))], (tm, FF_BLK), comm=comm)


def _ffn_dwd(name, act, dh, tk=1024):
    nb, S, _ = act.shape
    D = dh.shape[1]
    out = _mm_call(name, (nb, 1, S // tk), act, pl.BlockSpec((None, tk, FF_BLK), lambda b, j, k: (b, k, 0)),
                   dh, pl.BlockSpec((tk, D), lambda b, j, k: (k, 0)), TN,
                   [jax.ShapeDtypeStruct((nb, FF_BLK, D), BF16)],
                   [pl.BlockSpec((None, FF_BLK, D), lambda b, j, k: (b, 0, 0))], (FF_BLK, D))[0]
    return out.reshape(N_DEV, FF_BLK // 2, D)


def _ffn_dn(name, dgu, wgu, comm=None, tm=1024):
    S = dgu.shape[2]
    D = wgu.shape[1]
    dgu8 = dgu.reshape(N_DEV, S, FF_BLK)
    return _mm_call(name, (S // tm, 1, N_DEV), dgu8, pl.BlockSpec((None, tm, FF_BLK), lambda i, j, k: (k, i, 0)),
                    wgu, pl.BlockSpec((None, D, FF_BLK), lambda i, j, k: (k, 0, 0)), NT,
                    [jax.ShapeDtypeStruct((S, D), F32)], [pl.BlockSpec((tm, D), lambda i, j, k: (i, 0))], (tm, D), comm=comm)


def _adamw(name, parts, w, m, v, tr):
    rows, cols = w.shape
    n_parts = len(parts)
    c1 = 1.0 - ADAM_B1 ** ADAM_STEP
    c2 = 1.0 - ADAM_B2 ** ADAM_STEP

    def body(*refs):
        p_refs = refs[:n_parts]
        w_ref, m_ref, v_ref, g_ref, d_ref, nm_ref, nv_ref = refs[n_parts:]
        g = p_refs[0][...].astype(F32)
        for r in p_refs[1:]:
            g = g + r[...].astype(F32)
        mm = ADAM_B1 * m_ref[...] + (1.0 - ADAM_B1) * g
        vv = ADAM_B2 * v_ref[...] + (1.0 - ADAM_B2) * (g * g)
        g_ref[...] = g
        nm_ref[...] = mm
        nv_ref[...] = vv
        d_ref[...] = -ADAM_LR * ((mm / c1) / (jnp.sqrt(vv / c2) + ADAM_EPS) + ADAM_WD * w_ref[...])

    blk = pl.BlockSpec((tr, cols), lambda i: (i, 0))
    out = jax.ShapeDtypeStruct((rows, cols), F32)
    return pl.pallas_call(
        body, grid=(rows // tr,), in_specs=[blk] * (n_parts + 3), out_specs=[blk] * 4, out_shape=[out] * 4,
        compiler_params=_params("parallel"), name=name)(*parts, w, m, v)


def _position():
    return lax.axis_index("x"), lax.axis_index("y"), lax.axis_index("c")


def _other_chips():
    x, y, _ = _position()
    return [(1 - x, y), (x, 1 - y), (1 - x, 1 - y)]


def _remote(src, dst, send, recv, k, to):
    return pltpu.make_async_remote_copy(src_ref=src, dst_ref=dst, send_sem=send.at[k], recv_sem=recv.at[k],
                                        device_id=to, device_id_type=MESH)


def _ag_send(blocks, direct=False):
    n_peer = 7 if direct else 4

    def copies(ins, outs, send, recv, local, r0=0, l0=0):
        x, y, c = _position()
        me = 4 * x + 2 * y + c
        peers = [(x, y, 1 - c)] + [(px, py, c) for px, py in _other_chips()]
        if direct:
            peers += [(px, py, 1 - c) for px, py in _other_chips()]
        cps = []
        for t, (src, dst) in enumerate(zip(ins, outs)):
            cps.append(pltpu.make_async_copy(src, dst.at[me], local.at[l0 + t]))
            cps += [_remote(src, dst.at[me], send, recv, r0 + n_peer * t + k, to) for k, to in enumerate(peers)]
        return cps

    outs = tuple(jax.ShapeDtypeStruct((N_DEV,) + b.shape, b.dtype) for b in blocks)
    return _Comm(tuple(blocks), outs, {}, copies, n_peer * len(blocks), len(blocks))


def _ag_forward(bufs):
    def copies(ins, outs, send, recv, local, r0=0, l0=0):
        x, y, c = _position()
        cps = []
        for t, buf in enumerate(outs):
            for k, (px, py) in enumerate(_other_chips()):
                slot = buf.at[4 * px + 2 * py + c]
                cps.append(_remote(slot, slot, send, recv, r0 + 3 * t + k, (x, y, 1 - c)))
        return cps

    outs = tuple(jax.ShapeDtypeStruct(b.shape, b.dtype) for b in bufs)
    return _Comm(tuple(bufs), outs, {t: t for t in range(len(bufs))}, copies, 3 * len(bufs), 0)


def _rs_swap(shares):
    def copies(ins, outs, send, recv, local, r0=0, l0=0):
        x, y, c = _position()
        return [_remote(src.at[:, 1 - c], dst, send, recv, r0 + t, (x, y, 1 - c)) for t, (src, dst) in enumerate(zip(ins, outs))]

    ins = tuple(s.reshape((4, 2) + s.shape[1:]) for s in shares)
    outs = tuple(jax.ShapeDtypeStruct((4,) + s.shape[1:], s.dtype) for s in shares)
    return _Comm(ins, outs, {}, copies, len(shares), 0)


def _rs_exchange(sums):
    def copies(ins, outs, send, recv, local, r0=0, l0=0):
        _, _, c = _position()
        return [_remote(src.at[2 * px + py], dst.at[k], send, recv, r0 + 3 * t + k, (px, py, c))
                for t, (src, dst) in enumerate(zip(ins, outs)) for k, (px, py) in enumerate(_other_chips())]

    outs = tuple(jax.ShapeDtypeStruct((3,) + s.shape[1:], s.dtype) for s in sums)
    return _Comm(tuple(sums), outs, {}, copies, 3 * len(sums), 0)


def _comm_call(name, comm):
    return _call(name, lambda: None, (), [], [], [], [], (), (), comm)


def _pair_sum(name, share, got, core, tr):
    _, rows, cols = share.shape

    def body(c_ref, a_ref, b_ref, o_ref):
        o_ref[...] = (a_ref[...].astype(F32) + b_ref[...].astype(F32)).astype(o_ref.dtype)

    grid_spec = pltpu.PrefetchScalarGridSpec(
        num_scalar_prefetch=1, grid=(4, rows // tr),
        in_specs=[pl.BlockSpec((None, None, tr, cols), lambda q, i, c: (q, c[0], i, 0)),
                  pl.BlockSpec((None, tr, cols), lambda q, i, c: (q, i, 0))],
        out_specs=pl.BlockSpec((None, tr, cols), lambda q, i, c: (q, i, 0)))
    return pl.pallas_call(
        body, grid_spec=grid_spec, out_shape=jax.ShapeDtypeStruct((4, rows, cols), share.dtype),
        compiler_params=_params("parallel", "parallel"), name=name)(core, share.reshape(4, 2, rows, cols), got)


TENSORS = ("a_w_in", "a_w_out", "b_w_in", "b_w_out", "gu0", "gu1", "dn0", "dn1")
ROW_TILE = {"a_w_in": 256, "a_w_out": 128, "b_w_in": 256, "b_w_out": 128, "gu0": 256, "gu1": 256, "dn0": 176, "dn1": 176}
A_BLK = 9 * D_MODEL // N_DEV
B_BLK = 386
B_IN = 3 * D_MODEL + N_HEADS
B_IN_PAD = 3 * D_MODEL + LANES


def kernel(x, a_norm, a_w_in, a_w_out, b_norm, b_w_in, b_f, b_w_out, ffn_norm, ffn_w_gu, ffn_w_down, final_norm, loss_target, m_a_norm, m_a_w_in, m_a_w_out, m_b_norm, m_b_w_in, m_b_f, m_b_w_out, m_ffn_norm, m_ffn_w_gu, m_ffn_w_down, m_final_norm, v_a_norm, v_a_w_in, v_a_w_out, v_b_norm, v_b_w_in, v_b_f, v_b_w_out, v_ffn_norm, v_ffn_w_gu, v_ffn_w_down, v_final_norm):
    S = x.shape[1]
    xi, yi, ci = _position()
    dev = 4 * xi + 2 * yi + ci
    core = ci.reshape(1).astype(jnp.int32)
    h0, target = x.reshape(S, D_MODEL), loss_target.reshape(S, D_MODEL)

    def shards(a_in, a_out, b_in, b_out, gu, dn):
        return {"a_w_in": a_in[0], "a_w_out": a_out[0], "b_w_in": b_in[0], "b_w_out": b_out[0],
                "gu0": gu[0], "gu1": gu[1], "dn0": dn[0], "dn1": dn[1]}

    w_sh = shards(a_w_in, a_w_out, b_w_in, b_w_out, ffn_w_gu, ffn_w_down)
    m_sh = shards(m_a_w_in, m_a_w_out, m_b_w_in, m_b_w_out, m_ffn_w_gu, m_ffn_w_down)
    v_sh = shards(v_a_w_in, v_a_w_out, v_b_w_in, v_b_w_out, v_ffn_w_gu, v_ffn_w_down)
    wb = {n: w_sh[n].astype(BF16) for n in TENSORS}
    bf_pad = jnp.pad(b_f, ((0, 0), (0, LANES - N_HEADS)))
    tabs = _rope_tables(S)

    g_ain, g_aout = _comm_call("gather_a", _ag_send([wb["a_w_in"], wb["a_w_out"]]))
    dils = [dil for _, dil in DILATED_PATTERNS]
    n0_views, (g_ain, g_aout) = _rms_fwd("rms_a", h0, a_norm[0], dils, _ag_forward([g_ain, g_aout]))
    n0 = n0_views[0]
    w_a_in = g_ain.transpose(1, 0, 2).reshape(D_MODEL, 9 * D_MODEL)
    sends = [[wb["gu0"]], [wb["dn0"], jnp.pad(b_norm, ((0, 7), (0, 0)))], None]
    qkv_a, later = [], []
    for g, dil in enumerate(dils):
        qkv_g, sent = _a_proj("proj_a%d" % g, n0, w_a_in, g, dil, tabs, None if sends[g] is None else _ag_send(sends[g]))
        qkv_a.append(qkv_g)
        later += sent
    cols = [lambda r: r] * 3
    groups = [(g, dil, S // dil, qkv_a[g]) for g, (window, dil) in enumerate(DILATED_PATTERNS)]
    fwd = [_dil_fwd("dil_fwd%d" % g, view, *cols, dil, L, _ag_send([wb["b_w_in"], wb["b_w_out"]]) if g == 0 else None)
           for g, dil, L, view in groups]
    later = list(fwd[0][2:]) + later
    o_views, lse_views = _combine("dil_combine", [f[0] for f in fwd], [f[1] for f in fwd], dils)
    o_a = o_views[0]
    w_a_out = g_aout.reshape(D_MODEL, D_MODEL)
    h1, (g_bin, g_bout, g_gu0, g_dn0, g_bnorm) = _matmul("out_a", o_a, w_a_out, "nn", F32, TM, 1024, 1024, resid=h0,
                                                         comm=_ag_forward(later))

    n1 = _rms_fwd("rms_f0", h1, ffn_norm[0])
    gu0, act0 = _ffn_gu("gu_f0", n1, g_gu0)
    w_dn0 = g_dn0.reshape(D_FF, D_MODEL)
    h2 = _ffn_down("down_f0", act0, w_dn0, h1)[0]

    b_norm_full = g_bnorm[:, 0].reshape(D_MODEL)
    w_b_in = g_bin.transpose(1, 0, 2).reshape(D_MODEL, B_IN)
    w_b_gate = jnp.pad(w_b_in[:, 3 * D_MODEL:], ((0, 0), (0, LANES - N_HEADS)))
    w_b_cat = jnp.concatenate([w_b_in[:, :3 * D_MODEL], w_b_gate], axis=1)
    w_b_out = g_bout.reshape(D_MODEL, D_MODEL)
    n2 = _rms_fwd("rms_b", h2, b_norm_full)
    qkv = _matmul("proj_b", n2, w_b_in[:, :3 * D_MODEL], "nn", BF16, TM, 1024, 1024, col0_scale=SOFTMAX_SCALE)
    z = _matmul("gate_b", n2, w_b_gate, "nn", F32, TM, LANES, 1024)
    kbias = _gate_fwd("gate_cumsum", z, bf_pad)
    tf = min(S, 512)
    o_b, lse_b, g_gu1, g_dn1 = _fox_fwd("fox_fwd", qkv, kbias, tf, _ag_send([wb["gu1"], wb["dn1"]]))
    h3, (g_gu1, g_dn1) = _matmul("out_b", o_b, w_b_out, "nn", F32, TM, 1024, 1024, resid=h2, comm=_ag_forward([g_gu1, g_dn1]))

    w_dn1 = g_dn1.reshape(D_FF, D_MODEL)
    n3 = _rms_fwd("rms_f1", h3, ffn_norm[1])
    gu1, act1 = _ffn_gu("gu_f1", n3, g_gu1)
    h4 = _ffn_down("down_f1", act1, w_dn1, h3)[0]

    dh4, d_final, loss, dh4_16 = _loss_head("loss_head", h4, final_norm, target)

    share, got, sums, others = {}, {}, {}, {}

    def pair_sums(*names):
        for n in names:
            sums[n] = _pair_sum("pair_" + n, share[n], got[n], core, ROW_TILE[n])

    dgu1 = _ffn_dact("dact_f1", dh4_16, w_dn1, gu1)[0]
    share["dn1"] = _ffn_dwd("dwd_f1", act1, dh4_16)
    share["gu1"] = _ffn_dwgu("dwgu_f1", n3, dgu1)[0]
    dn3, got["gu1"], got["dn1"] = _ffn_dn("dn_f1", dgu1, g_gu1, _rs_swap([share["gu1"], share["dn1"]]))
    dh3, d_ffn1, dh3_16 = _rms_bwd("rmsb_f1", dn3, h3, ffn_norm[1], dh4)
    pair_sums("gu1", "dn1")

    do_b = _matmul("dout_b", dh3_16, w_b_out, "nt", BF16, TM, 1024, 1024)
    share["b_w_out"] = _matmul("dwout_b", o_b, dh3_16, "tn", BF16, TM, 1024, 1024).reshape(N_DEV, 128, D_MODEL)
    dq_b, dk_b, dv_b, ds_rowsum, ds_colsum, others["gu1"], others["dn1"] = _fox_bwd(
        "fox_bwd", qkv, kbias, do_b, o_b, lse_b, tf, _rs_exchange([sums["gu1"], sums["dn1"]]))
    dc = ds_rowsum[:, :N_HEADS] - ds_colsum.reshape(N_HEADS, S).T
    dz, d_bf = _gate_bwd("gate_bwd", jnp.pad(dc, ((0, 0), (0, LANES - N_HEADS))), z, bf_pad)
    dproj_b = jnp.concatenate([dq_b, dk_b, dv_b, dz.astype(BF16)], axis=1)
    dw_b_in = _matmul("dwin_b", n2, dproj_b, "tn", BF16, TM, B_IN_PAD // 5, 1024)
    dn2 = _matmul("dn_b", dproj_b, w_b_cat, "nt", F32, TM, 1024, B_IN_PAD // 5)
    dh2, d_bnorm, dh2_16 = _rms_bwd("rmsb_b", dn2, h2, b_norm_full, dh3)
    share["b_w_in"] = dw_b_in[:, :B_IN].reshape(D_MODEL, N_DEV, B_BLK).transpose(1, 0, 2)

    dgu0, got["b_w_in"], got["b_w_out"] = _ffn_dact("dact_f0", dh2_16, w_dn0, gu0, _rs_swap([share["b_w_in"], share["b_w_out"]]))
    share["dn0"] = _ffn_dwd("dwd_f0", act0, dh2_16)
    pair_sums("b_w_in", "b_w_out")
    share["gu0"], others["b_w_in"], others["b_w_out"] = _ffn_dwgu(
        "dwgu_f0", n1, dgu0, _rs_exchange([sums["b_w_in"], sums["b_w_out"]]))
    dn1, got["gu0"], got["dn0"] = _ffn_dn("dn_f0", dgu0, g_gu0, _rs_swap([share["gu0"], share["dn0"]]))
    dh1, d_ffn0, dh1_16 = _rms_bwd("rmsb_f0", dn1, h1, ffn_norm[0], dh2)
    pair_sums("gu0", "dn0")

    do_views = _matmul_nt_views("dout_a", dh1_16, w_a_out, dils)
    share["a_w_out"] = _matmul("dwout_a", o_a, dh1_16, "tn", BF16, TM, 1024, 1024).reshape(N_DEV, 128, D_MODEL)
    pieces = []
    for g, dil, L, view in groups:
        rot = tuple(tb.reshape(L, dil * LANES) for tb in tabs)
        hosted = {0: "gu0", 1: "dn0"}.get(g)
        res = _dil_bwd("dil_bwd%d" % g, view, do_views[g], o_views[g], lse_views[g], rot, dil, L,
                       _rs_exchange([sums[hosted]]) if hosted else None)
        pieces.append(res[:3])
        if hosted:
            others[hosted] = res[3]
    dws = [_a_dw("dwin_a%d" % g, n0_views[g], pieces[g], dil, None)[0] for g, dil in enumerate(dils)]
    share["a_w_in"] = jnp.concatenate(dws, axis=1).reshape(D_MODEL, N_DEV, A_BLK).transpose(1, 0, 2)
    got["a_w_in"], got["a_w_out"] = _comm_call("swap_a", _rs_swap([share["a_w_in"], share["a_w_out"]]))
    pair_sums("a_w_in", "a_w_out")
    dn0, others["a_w_in"], others["a_w_out"] = _a_dn("dn_a", [p for ps in pieces for p in ps], dils, w_a_in,
                                                     _rs_exchange([sums["a_w_in"], sums["a_w_out"]]))
    dx, d_anorm = _rms_bwd("rmsb_a", dn0, h0, a_norm[0], dh1, copy16=False)

    misc = jnp.concatenate([d_bf[:, :N_HEADS], loss[:, :1], jnp.zeros((1, D_MODEL - N_HEADS - 1), F32)], axis=1)
    small = jnp.concatenate([d_anorm, d_ffn0, d_ffn1, d_final, d_bnorm, misc, jnp.zeros((2, D_MODEL), F32)], axis=0)
    small_all, = _comm_call("gather_small", _ag_send([small], direct=True))

    outs = {}
    for n in TENSORS:
        mine = lax.dynamic_index_in_dim(sums[n], 2 * xi + yi, axis=0, keepdims=False)
        outs[n] = _adamw("adamw_" + n, [mine] + [others[n][k] for k in range(3)], w_sh[n], m_sh[n], v_sh[n], ROW_TILE[n])

    pad_vec = lambda a: jnp.pad(a, ((0, 0), (0, D_MODEL - a.shape[1])))

    def small_pack(an, fn, fin, bf):
        return jnp.concatenate([an, fn, fin.reshape(1, D_MODEL), jnp.zeros((1, D_MODEL), F32), pad_vec(bf),
                                jnp.zeros((2, D_MODEL), F32)], axis=0)

    sg, sd, sm, sv = _adamw("adamw_small", [small_all[d] for d in range(N_DEV)], small_pack(a_norm, ffn_norm, final_norm, b_f),
                            small_pack(m_a_norm, m_ffn_norm, m_final_norm, m_b_f),
                            small_pack(v_a_norm, v_ffn_norm, v_final_norm, v_b_f), 8)
    g_bn = lax.dynamic_slice(sg[4:5], (0, dev * LANES), (1, LANES))
    bn = _adamw("adamw_b_norm", [g_bn], b_norm, m_b_norm, v_b_norm, 1)

    def tree(i):
        full = lambda name, ref: outs[name][i].reshape(ref.shape)
        sml = (sg, sd, sm, sv)[i]
        return dict(
            a_norm=sml[0:1], a_w_in=full("a_w_in", a_w_in), a_w_out=full("a_w_out", a_w_out), b_norm=bn[i],
            b_w_in=full("b_w_in", b_w_in), b_f=sml[5:6, :N_HEADS], b_w_out=full("b_w_out", b_w_out), ffn_norm=sml[1:3],
            ffn_w_gu=jnp.stack([outs["gu0"][i], outs["gu1"][i]]).reshape(ffn_w_gu.shape),
            ffn_w_down=jnp.stack([outs["dn0"][i], outs["dn1"][i]]).reshape(ffn_w_down.shape), final_norm=sml[3])

    order = ("a_norm", "a_w_in", "a_w_out", "b_norm", "b_w_in", "b_f", "b_w_out", "ffn_norm", "ffn_w_gu", "ffn_w_down", "final_norm")
    result = [sg[5, N_HEADS], dx.reshape(x.shape)]
    for i in range(4):
        t = tree(i)
        result += [t[n] for n in order]
    return tuple(result)
```

```python
import functools
from typing import Callable, NamedTuple

import jax
import jax.numpy as jnp
from jax import lax
from jax.experimental import pallas as pl
from jax.experimental.pallas import tpu as pltpu

F32 = jnp.float32
BF16 = jnp.bfloat16

D_MODEL = 1024
N_HEADS = 16
HEAD_DIM = 64
N_PAIRS = N_HEADS // 2
LANES = 128
DILATED_PATTERNS = ((128, 1), (512, 4), (2048, 16))
BAND_STEPS = 128
ROT_DIM = HEAD_DIM // 4
ROPE_THETA = 500000.0
D_FF = 2816
RMS_EPS = 1e-6
NEG_INF = -1e30
SOFTMAX_SCALE = HEAD_DIM ** -0.5
N_DEV = 8
FF_BLK = 2 * D_FF // N_DEV
ADAM_LR, ADAM_B1, ADAM_B2, ADAM_EPS, ADAM_WD, ADAM_STEP = 0.001, 0.9, 0.999, 1e-08, 0.01, 10
VMEM_LIMIT = 52 * 1024 * 1024
FOX_BWD_VMEM = 60 * 1024 * 1024
TM = 1024
MESH = pl.DeviceIdType.MESH

NN = (((1,), (0,)), ((), ()))
NT = (((1,), (1,)), ((), ()))
TN = (((0,), (0,)), ((), ()))


def _params(*sem, vmem=VMEM_LIMIT):
    return pltpu.CompilerParams(dimension_semantics=sem, vmem_limit_bytes=vmem)


def _dot(a, b, dims):
    return lax.dot_general(a, b, dims, preferred_element_type=F32)


class _Comm(NamedTuple):
    ins: tuple
    outs: tuple
    aliases: dict
    copies: Callable
    n_remote: int
    n_local: int


def _call(name, body, grid, in_specs, out_specs, out_shape, scratch, args, sem, comm=None, vmem=VMEM_LIMIT):
    if comm is None:
        return pl.pallas_call(body, grid=grid, in_specs=in_specs, out_specs=out_specs, out_shape=out_shape,
                              scratch_shapes=scratch, compiler_params=_params(*sem, vmem=vmem), name=name)(*args)
    n_in, n_out = len(in_specs), len(out_specs)
    n_ci, n_co = len(comm.ins), len(comm.outs)
    o0 = n_in + n_ci

    def hosted(*refs):
        c_ins, c_outs = refs[n_in:o0], refs[o0 + n_out:o0 + n_out + n_co]
        sems = refs[-3:]

        def start():
            for cp in comm.copies(c_ins, c_outs, *sems):
                cp.start()

        def wait():
            for cp in comm.copies(c_ins, c_outs, *sems):
                cp.wait()

        if not grid:
            start()
            body()
            wait()
            return
        ids = [pl.program_id(ax) for ax in range(len(grid))]
        pl.when(functools.reduce(jnp.logical_and, [i == 0 for i in ids]))(start)
        body(*refs[:n_in], *refs[o0:o0 + n_out], *refs[o0 + n_out + n_co:-3])
        pl.when(functools.reduce(jnp.logical_and, [i == g - 1 for i, g in zip(ids, grid)]))(wait)

    hbm = pl.BlockSpec(memory_space=pltpu.HBM)
    dma = pltpu.SemaphoreType.DMA
    return pl.pallas_call(
        hosted, grid=grid, in_specs=[*in_specs, *[hbm] * n_ci], out_specs=[*out_specs, *[hbm] * n_co],
        out_shape=[*out_shape, *comm.outs], input_output_aliases={n_in + i: n_out + o for i, o in comm.aliases.items()},
        scratch_shapes=[*scratch, dma((comm.n_remote,)), dma((comm.n_remote,)), dma((max(comm.n_local, 1),))],
        compiler_params=_params(*["arbitrary"] * len(grid), vmem=vmem), name=name)(*args, *comm.ins)


def _mm_call(name, grid, a, a_spec, b, b_spec, dims, out_shapes, out_specs, acc_shape, epilogue=None,
             extras=(), extra_specs=(), col_axis=1, comm=None):
    nk = grid[2]
    n_extra = len(extras)
    n_out = len(out_shapes)

    def finish(res, ex, outs, j):
        if epilogue is None:
            outs[0][...] = res.astype(outs[0].dtype)
        else:
            epilogue(res, ex, outs, j)

    def body(*refs):
        a_ref, b_ref = refs[0], refs[1]
        ex = refs[2:2 + n_extra]
        outs = refs[2 + n_extra:2 + n_extra + n_out]
        j, k = pl.program_id(col_axis), pl.program_id(2)
        part = _dot(a_ref[...].astype(BF16), b_ref[...].astype(BF16), dims)
        if nk == 1:
            finish(part, ex, outs, j)
            return
        acc = refs[-1]

        @pl.when(k == 0)
        def _():
            acc[...] = part

        @pl.when((k > 0) & (k < nk - 1))
        def _():
            acc[...] += part

        @pl.when(k == nk - 1)
        def _():
            finish(acc[...] + part, ex, outs, j)

    return _call(name, body, grid, [a_spec, b_spec, *extra_specs], out_specs, out_shapes,
                 [] if nk == 1 else [pltpu.VMEM(acc_shape, F32)], (a, b, *extras), ("parallel", "parallel", "arbitrary"), comm)


def _matmul(name, a, b, mode, out_dtype, tm, tn, tk, resid=None, col0_scale=None, comm=None):
    if mode == "nn":
        (M, K), N = a.shape, b.shape[1]
        a_spec = pl.BlockSpec((tm, tk), lambda j, i, k: (i, k))
        b_spec = pl.BlockSpec((tk, tn), lambda j, i, k: (k, j))
        dims = NN
    elif mode == "nt":
        (M, K), N = a.shape, b.shape[0]
        a_spec = pl.BlockSpec((tm, tk), lambda j, i, k: (i, k))
        b_spec = pl.BlockSpec((tn, tk), lambda j, i, k: (j, k))
        dims = NT
    else:
        (K, M), N = a.shape, b.shape[1]
        a_spec = pl.BlockSpec((tk, tm), lambda j, i, k: (k, i))
        b_spec = pl.BlockSpec((tk, tn), lambda j, i, k: (k, j))
        dims = TN
    assert M % tm == 0 and N % tn == 0 and K % tk == 0, (name, M, N, K, tm, tn, tk)
    o_spec = pl.BlockSpec((tm, tn), lambda j, i, k: (i, j))
    extras, extra_specs, epilogue = (), (), None
    if resid is not None:
        extras, extra_specs = (resid,), (o_spec,)

        def epilogue(acc, ex, outs, j):
            outs[0][...] = (acc + ex[0][...]).astype(outs[0].dtype)

    elif col0_scale is not None:

        def epilogue(acc, ex, outs, j):
            outs[0][...] = (acc * jnp.where(j == 0, col0_scale, 1.0)).astype(outs[0].dtype)

    res = _mm_call(name, (N // tn, M // tm, K // tk), a, a_spec, b, b_spec, dims, [jax.ShapeDtypeStruct((M, N), out_dtype)],
                   [o_spec], (tm, tn), epilogue, extras, extra_specs, col_axis=0, comm=comm)
    return res[0] if comm is None else (res[0], res[1:])


def _rms_fwd(name, h, gain, dils=(1,), comm=None, tm=512):
    S, D = h.shape

    def body(h_ref, g_ref, *rest):
        x = h_ref[...]
        rstd = lax.rsqrt(jnp.mean(x * x, axis=-1, keepdims=True) + RMS_EPS)
        y = x * rstd * g_ref[...]
        _write_views([y[:, b * LANES:(b + 1) * LANES] for b in range(N_PAIRS)], rest[-1], rest[:-1], dils, tm)

    res = _call(name, body, (S // tm,), [pl.BlockSpec((tm, D), lambda i: (i, 0)), pl.BlockSpec((1, D), lambda i: (0, 0))],
                [_view_spec(tm, R) for R in dils], [jax.ShapeDtypeStruct((S // R, R * D), BF16) for R in dils],
                [pltpu.VMEM((N_PAIRS, tm, LANES), F32)], (h, gain.reshape(1, D)), ("parallel",), comm)
    views = res[0] if len(dils) == 1 else res[:len(dils)]
    return views if comm is None else (views, res[len(dils):])


def _rms_bwd(name, dn, h, gain, dres, copy16=True, tm=512):
    S, D = h.shape

    def body(dn_ref, h_ref, g_ref, r_ref, dh_ref, dg_ref, *dh16_ref):
        x = h_ref[...]
        rstd = lax.rsqrt(jnp.mean(x * x, axis=-1, keepdims=True) + RMS_EPS)
        xhat = x * rstd
        d = dn_ref[...]
        dxhat = d * g_ref[...]
        dh = rstd * (dxhat - xhat * jnp.mean(dxhat * xhat, axis=-1, keepdims=True)) + r_ref[...]
        dh_ref[...] = dh
        if copy16:
            dh16_ref[0][...] = dh.astype(BF16)

        @pl.when(pl.program_id(0) == 0)
        def _():
            dg_ref[...] = jnp.zeros_like(dg_ref)

        dg_ref[...] += jnp.sum(d * xhat, axis=0, keepdims=True)

    row = pl.BlockSpec((tm, D), lambda i: (i, 0))
    vec = pl.BlockSpec((1, D), lambda i: (0, 0))
    return pl.pallas_call(
        body, grid=(S // tm,), in_specs=[row, row, vec, row], out_specs=[row, vec] + [row] * copy16,
        out_shape=[jax.ShapeDtypeStruct((S, D), F32), jax.ShapeDtypeStruct((1, D), F32)] + [jax.ShapeDtypeStruct((S, D), BF16)] * copy16,
        compiler_params=_params("arbitrary"), name=name)(dn, h, gain.reshape(1, D), dres)


def _loss_head(name, h, gain, target, tm=512):
    S, D = h.shape

    def body(h_ref, g_ref, t_ref, dh_ref, dg_ref, loss_ref, dh16_ref):
        x = h_ref[...]
        rstd = lax.rsqrt(jnp.mean(x * x, axis=-1, keepdims=True) + RMS_EPS)
        xhat = x * rstd
        err = xhat * g_ref[...] - t_ref[...]
        dy = err * (1.0 / D)
        dxhat = dy * g_ref[...]
        dh = rstd * (dxhat - xhat * jnp.mean(dxhat * xhat, axis=-1, keepdims=True))
        dh_ref[...] = dh
        dh16_ref[...] = dh.astype(BF16)

        @pl.when(pl.program_id(0) == 0)
        def _():
            dg_ref[...] = jnp.zeros_like(dg_ref)
            loss_ref[...] = jnp.zeros_like(loss_ref)

        dg_ref[...] += jnp.sum(dy * xhat, axis=0, keepdims=True)
        part = 0.5 * jnp.sum(jnp.mean(err * err, axis=-1, keepdims=True), axis=0, keepdims=True)
        loss_ref[...] += jnp.broadcast_to(part, loss_ref.shape)

    row = pl.BlockSpec((tm, D), lambda i: (i, 0))
    vec = pl.BlockSpec((1, D), lambda i: (0, 0))
    return pl.pallas_call(
        body, grid=(S // tm,), in_specs=[row, vec, row], out_specs=[row, vec, pl.BlockSpec((1, LANES), lambda i: (0, 0)), row],
        out_shape=[jax.ShapeDtypeStruct((S, D), F32), jax.ShapeDtypeStruct((1, D), F32),
                   jax.ShapeDtypeStruct((1, LANES), F32), jax.ShapeDtypeStruct((S, D), BF16)],
        compiler_params=_params("arbitrary"), name=name)(h, gain.reshape(1, D), target)


def _rope_tables(S):
    half = ROT_DIM // 2
    inv_freq = ROPE_THETA ** (-jnp.arange(half, dtype=F32) * 2.0 / ROT_DIM)
    ang = jnp.arange(S, dtype=F32)[:, None] * inv_freq[None, :]
    cos, sin = jnp.cos(ang), jnp.sin(ang)
    one = jnp.ones((S, HEAD_DIM - ROT_DIM), F32)
    zero = jnp.zeros((S, HEAD_DIM - ROT_DIM), F32)
    zh = jnp.zeros((S, half), F32)
    c = jnp.concatenate([cos, cos, one], axis=1)
    sa = jnp.concatenate([-sin, zh, zero], axis=1)
    sb = jnp.concatenate([zh, sin, zero], axis=1)
    return tuple(jnp.concatenate([t, t], axis=1) for t in (c, sa, sb))


def _rotate(x, c, sa, sb, sign):
    return x * c + sign * (pltpu.roll(x, LANES - ROT_DIM // 2, 1) * sa + pltpu.roll(x, ROT_DIM // 2, 1) * sb)


def _stage_chunks(scr, chunks):
    for c, x in enumerate(chunks):
        scr[c] = x


def _strided_rows(scr, c, r, n, R):
    return scr.at[c][pl.ds(r, n, stride=R), :]


def _a_proj(name, n, w, g, R, tabs, comm, tm=1024):
    S, D = n.shape
    n_i = S // tm
    n_out = 3

    def body(n_ref, w_ref, c_ref, sa_ref, sb_ref, *rest):
        outs, scr = rest[:n_out], rest[n_out]
        j = pl.program_id(0)
        acc = _dot(n_ref[...], w_ref[...], NN)
        c, sa, sb = c_ref[...], sa_ref[...], sb_ref[...]
        for J in range(n_out):
            kind = J

            @pl.when(j == J)
            def _(J=J, kind=kind):
                chunks = [acc[:, b * LANES:(b + 1) * LANES] for b in range(N_PAIRS)]
                if kind < 2:
                    chunks = [_rotate(x, c, sa, sb, 1.0) * (SOFTMAX_SCALE if kind == 0 else 1.0) for x in chunks]
                if R == 1:
                    for b, x in enumerate(chunks):
                        outs[J][:, b * LANES:(b + 1) * LANES] = x.astype(BF16)
                    return
                _stage_chunks(scr, chunks)
                for r in range(R):
                    for b in range(N_PAIRS):
                        col = r * D_MODEL + b * LANES
                        outs[J][:, col:col + LANES] = _strided_rows(scr, b, r, tm // R, R).astype(BF16)

    def out_spec(J):
        return pl.BlockSpec((tm // R, R * D_MODEL), lambda j, i: (jnp.where(j == J, i, jnp.where(j < J, 0, n_i - 1)), 0))

    tab = pl.BlockSpec((tm, LANES), lambda j, i: (i, 0))
    res = _call(name, body, (n_out, n_i),
                [pl.BlockSpec((tm, D), lambda j, i: (i, 0)), pl.BlockSpec((D, D_MODEL), lambda j, i: (0, 3 * g + j)), tab, tab, tab],
                [out_spec(J) for J in range(n_out)], [jax.ShapeDtypeStruct((S // R, R * D_MODEL), BF16)] * n_out,
                [pltpu.VMEM((N_PAIRS, tm, LANES), F32)], (n, w, *tabs), ("arbitrary", "arbitrary"), comm)
    return res[:n_out], res[n_out:]


def _unstride(src_chunk, R, tok, rows):
    for r in range(R):
        for b in range(N_PAIRS):
            tok.at[b][pl.ds(r, rows // R, stride=R), :] = src_chunk(r, b).astype(F32)


def _by_residue(ref, R):
    return ref[...] if R == 1 else jnp.concatenate([ref[:, r * D_MODEL:(r + 1) * D_MODEL] for r in range(R)], axis=0)


def _a_dw(name, n_view, pieces, R, comm, tk=1024):
    D = D_MODEL
    S = n_view.shape[0] * R
    n_k = S // tk
    n_p = len(pieces)

    def body(n_ref, *rest):
        p_refs, o_ref, acc = rest[:n_p], rest[n_p], rest[n_p + 1]
        j, k = pl.program_id(0), pl.program_id(1)
        for J in range(n_p):

            @pl.when(j == J)
            def _(J=J):
                part = _dot(_by_residue(n_ref, R), _by_residue(p_refs[J], R), TN)

                @pl.when(k == 0)
                def _():
                    acc[...] = part

                @pl.when((k > 0) & (k < n_k - 1))
                def _():
                    acc[...] += part

                @pl.when(k == n_k - 1)
                def _():
                    o_ref[...] = (acc[...] + part).astype(BF16)

    def piece_spec(J):
        return pl.BlockSpec((tk // R, R * D_MODEL), lambda j, k: (jnp.where(j == J, k, jnp.where(j < J, 0, n_k - 1)), 0))

    return _call(name, body, (n_p, n_k), [pl.BlockSpec((tk // R, R * D_MODEL), lambda j, k: (k, 0))] + [piece_spec(J) for J in range(n_p)],
                 [pl.BlockSpec((D, D_MODEL), lambda j, k: (0, j))], [jax.ShapeDtypeStruct((D, n_p * D_MODEL), BF16)],
                 [pltpu.VMEM((D, D_MODEL), F32)], (n_view, *pieces), ("arbitrary", "arbitrary"), comm)


def _a_dn(name, pieces, dils, w, comm, tm=512):
    D = w.shape[0]
    S = pieces[0].shape[0] * dils[0]
    n_p = len(pieces)

    def body(*refs):
        p_refs, w_ref, o_ref, acc, part_acc, tok = refs[:n_p], refs[n_p], refs[n_p + 1], refs[n_p + 2], refs[n_p + 3], refs[n_p + 4]
        j = pl.program_id(1)
        for J in range(n_p):

            @pl.when(j == J)
            def _(J=J):
                R, t = dils[J // 3], J % 3
                part = _dot(_by_residue(p_refs[J], R), w_ref[...], NT)
                if R == 1:
                    if J == 0:
                        acc[...] = part
                    else:
                        acc[...] += part
                    return
                if t == 0:
                    part_acc[...] = part
                    return
                if t == 1:
                    part_acc[...] += part
                    return
                n = tm // R
                _unstride(lambda r, b: part_acc[r * n:(r + 1) * n, b * LANES:(b + 1) * LANES]
                          + part[r * n:(r + 1) * n, b * LANES:(b + 1) * LANES], R, tok, tm)
                total = acc[...] + jnp.concatenate([tok[b] for b in range(N_PAIRS)], axis=1)
                if J == n_p - 1:
                    o_ref[...] = total
                else:
                    acc[...] = total

    specs = [pl.BlockSpec((tm // dils[J // 3], dils[J // 3] * D_MODEL), lambda i, j: (i, 0)) for J in range(n_p)]
    return _call(name, body, (S // tm, n_p), specs + [pl.BlockSpec((D, D_MODEL), lambda i, j: (0, j))],
                 [pl.BlockSpec((tm, D), lambda i, j: (i, 0))], [jax.ShapeDtypeStruct((S, D), F32)],
                 [pltpu.VMEM((tm, D), F32), pltpu.VMEM((tm, D), F32), pltpu.VMEM((N_PAIRS, tm, LANES), F32)], (*pieces, w),
                 ("arbitrary", "arbitrary"), comm)


def _lo_lanes():
    return lax.broadcasted_iota(jnp.int32, (1, LANES), 1) < HEAD_DIM


def _rep_rows(x2, lo):
    sw = pltpu.roll(x2, HEAD_DIM, 1)
    return jnp.where(lo, x2, sw), jnp.where(lo, sw, x2)


def _pair_cols(h):
    return slice((h // 2) * LANES, (h // 2 + 1) * LANES)


def _head_lanes(lo, h):
    return lo if h % 2 == 0 else jnp.logical_not(lo)


def _band_masks(t, first):
    ri = lax.broadcasted_iota(jnp.int32, (t, t), 0)
    ci = lax.broadcasted_iota(jnp.int32, (t, t), 1)
    neg_prev = jnp.where((ci >= ri) & jnp.logical_not(first), 0.0, NEG_INF)
    neg_cur = jnp.where(ci <= ri, 0.0, NEG_INF)
    return neg_prev, neg_cur


def _dil_specs(L, R, t, qcol, kcol, vcol):
    W = D_MODEL
    prev = lambda qi: jnp.maximum(qi - 1, 0)
    return dict(
        q=pl.BlockSpec((t, W), lambda r, qi: (qi, qcol(r))),
        kp=pl.BlockSpec((t, W), lambda r, qi: (prev(qi), kcol(r))), kc=pl.BlockSpec((t, W), lambda r, qi: (qi, kcol(r))),
        vp=pl.BlockSpec((t, W), lambda r, qi: (prev(qi), vcol(r))), vc=pl.BlockSpec((t, W), lambda r, qi: (qi, vcol(r))),
        own=pl.BlockSpec((t, W), lambda r, qi: (qi, r)), tab=pl.BlockSpec((t, LANES), lambda r, qi: (qi, r)))


def _dil_fwd(name, x, qcol, kcol, vcol, R, L, comm=None):
    t = BAND_STEPS
    W = D_MODEL
    sp = _dil_specs(L, R, t, qcol, kcol, vcol)

    def body(q_ref, kp_ref, kc_ref, vp_ref, vc_ref, o_ref, lse_ref):
        lo = _lo_lanes()
        neg_p, neg_c = _band_masks(t, pl.program_id(1) == 0)
        s_p, s_c = [], []
        for h in range(N_HEADS):
            cols = _pair_cols(h)
            qh = jnp.where(_head_lanes(lo, h), q_ref[:, cols], 0)
            s_p.append(_dot(qh, kp_ref[:, cols], NT))
            s_c.append(_dot(qh, kc_ref[:, cols], NT))
        s_p = jnp.stack(s_p) + neg_p[None]
        s_c = jnp.stack(s_c) + neg_c[None]
        m = jnp.maximum(jnp.max(s_p, axis=2, keepdims=True), jnp.max(s_c, axis=2, keepdims=True))
        p_p, p_c = jnp.exp(s_p - m), jnp.exp(s_c - m)
        l = jnp.sum(p_p, axis=2, keepdims=True) + jnp.sum(p_c, axis=2, keepdims=True)
        inv, lse = 1.0 / l, m + jnp.log(l)
        p_p, p_c = p_p.astype(BF16), p_c.astype(BF16)
        for p in range(N_PAIRS):
            cols = _pair_cols(2 * p)
            o2 = jnp.zeros((t, LANES), F32)
            for h in (2 * p, 2 * p + 1):
                hm = _head_lanes(lo, h)
                pv = _dot(p_p[h], jnp.where(hm, vp_ref[:, cols], 0), NN) + _dot(p_c[h], jnp.where(hm, vc_ref[:, cols], 0), NN)
                o2 = o2 + pv * inv[h]
            o_ref[:, cols] = o2
            lse_ref[:, cols] = jnp.where(lo, lse[2 * p], lse[2 * p + 1])

    return _call(name, body, (R, L // t), [sp["q"], sp["kp"], sp["kc"], sp["vp"], sp["vc"]], [sp["own"], sp["own"]],
                 [jax.ShapeDtypeStruct((L, R * W), F32), jax.ShapeDtypeStruct((L, R * W), F32)], [],
                 (x[0], x[1], x[1], x[2], x[2]), ("parallel", "parallel"), comm)


def _dil_scores(lo, q_ref, do_ref, o_ref, lse_ref, kv_refs):
    s = [[] for _ in kv_refs]
    dp = [[] for _ in kv_refs]
    lse, d = [], []
    for h in range(N_HEADS):
        cols = _pair_cols(h)
        hm = _head_lanes(lo, h)
        qh, doh = jnp.where(hm, q_ref[:, cols], 0), jnp.where(hm, do_ref[:, cols], 0)
        for i, (k_ref, v_ref) in enumerate(kv_refs):
            s[i].append(_dot(qh, k_ref[:, cols], NT))
            dp[i].append(_dot(doh, v_ref[:, cols], NT))
        lse.append(_rep_rows(lse_ref[:, cols], lo)[h % 2])
        dd = do_ref[:, cols].astype(F32) * o_ref[:, cols].astype(F32)
        d.append(jnp.sum(jnp.where(hm, dd, 0.0), axis=1, keepdims=True))
    return (*[jnp.stack(x) for x in s], *[jnp.stack(x) for x in dp], jnp.stack(lse), jnp.stack(d))


def _dil_bwd(name, x, do, o, lse, tabs, R, L, comm=None):
    t = BAND_STEPS
    W = D_MODEL
    nq = L // t
    qb = lambda step: nq - 1 - step
    kb = lambda step: jnp.maximum(qb(step) - 1, 0)
    at = lambda f, width: pl.BlockSpec((t, width), lambda r, step: (f(step), r))

    def body(q_ref, kp_ref, kc_ref, vp_ref, vc_ref, do_ref, o_ref, lse_ref, c_ref, sa_ref, sb_ref, dq_ref, dk_ref, dv_ref,
             dk_scr, dv_scr):
        qi = nq - 1 - pl.program_id(1)
        lo = _lo_lanes()
        unrotate = lambda x: _rotate(x, c_ref[...], sa_ref[...], sb_ref[...], -1.0).astype(BF16)

        @pl.when(qi == nq - 1)
        def _():
            dk_scr[...] = jnp.zeros_like(dk_scr)
            dv_scr[...] = jnp.zeros_like(dv_scr)

        neg_p, neg_c = _band_masks(t, qi == 0)
        s_p, s_c, dp_p, dp_c, lse_h, d = _dil_scores(lo, q_ref, do_ref, o_ref, lse_ref, ((kp_ref, vp_ref), (kc_ref, vc_ref)))
        p_p, p_c = jnp.exp(s_p + neg_p[None] - lse_h), jnp.exp(s_c + neg_c[None] - lse_h)
        ds_p, ds_c = (p_p * (dp_p - d)).astype(BF16), (p_c * (dp_c - d)).astype(BF16)
        p_p, p_c = p_p.astype(BF16), p_c.astype(BF16)
        for p in range(N_PAIRS):
            cols = _pair_cols(2 * p)
            dq2 = jnp.zeros((t, LANES), F32)
            dk_cur, dv_cur = dk_scr[:, cols], dv_scr[:, cols]
            dk_prev, dv_prev = jnp.zeros((t, LANES), F32), jnp.zeros((t, LANES), F32)
            for h in (2 * p, 2 * p + 1):
                hm = _head_lanes(lo, h)
                qh, doh = jnp.where(hm, q_ref[:, cols], 0), jnp.where(hm, do_ref[:, cols], 0)
                dq2 = dq2 + _dot(ds_p[h], jnp.where(hm, kp_ref[:, cols], 0), NN) + _dot(ds_c[h], jnp.where(hm, kc_ref[:, cols], 0), NN)
                dk_prev, dv_prev = dk_prev + _dot(ds_p[h], qh, TN), dv_prev + _dot(p_p[h], doh, TN)
                dk_cur, dv_cur = dk_cur + _dot(ds_c[h], qh, TN), dv_cur + _dot(p_c[h], doh, TN)
            dq_ref[:, cols] = unrotate(dq2 * SOFTMAX_SCALE)
            dk_ref[:, cols] = unrotate(dk_cur)
            dv_ref[:, cols] = dv_cur.astype(BF16)
            dk_scr[:, cols] = dk_prev
            dv_scr[:, cols] = dv_prev

    wide = jax.ShapeDtypeStruct((L, R * W), BF16)
    return _call(name, body, (R, nq),
                 [at(qb, W), at(kb, W), at(qb, W), at(kb, W), at(qb, W), at(qb, W), at(qb, W), at(qb, W),
                  at(qb, LANES), at(qb, LANES), at(qb, LANES)],
                 [at(qb, W), at(qb, W), at(qb, W)], [wide, wide, wide], [pltpu.VMEM((t, W), F32), pltpu.VMEM((t, W), F32)],
                 (x[0], x[1], x[1], x[2], x[2], do, o, lse, *tabs), ("parallel", "arbitrary"), comm)


def _fox_operands(q2, k2, kb2, lo, hh):
    lane = lax.broadcasted_iota(jnp.int32, (1, LANES), 1)
    if hh == 0:
        ones = ((lane >= HEAD_DIM) & (lane < HEAD_DIM + 3)).astype(BF16)
        return jnp.where(lo, q2, ones), jnp.where(lo, k2, kb2)
    ones = (lane < 3).astype(BF16)
    return jnp.where(lo, ones, q2), jnp.where(lo, kb2, k2)


def _causal_neg(t):
    ri = lax.broadcasted_iota(jnp.int32, (t, t), 0)
    ci = lax.broadcasted_iota(jnp.int32, (t, t), 1)
    return jnp.where(ci <= ri, 0.0, NEG_INF)


def _fox_fwd(name, qkv, kbias, t, comm=None):
    S = qkv.shape[0]
    W = D_MODEL
    nq = S // t
    rep = t // LANES

    def body(q_ref, k_ref, v_ref, kb_ref, o_ref, lse_ref, m_scr, l_scr, acc_scr):
        qi, j = pl.program_id(0), pl.program_id(1)
        lo = _lo_lanes()

        @pl.when(j == 0)
        def _():
            m_scr[...] = jnp.full_like(m_scr, NEG_INF)
            l_scr[...] = jnp.zeros_like(l_scr)
            acc_scr[...] = jnp.zeros_like(acc_scr)

        def step(masked):
            neg = _causal_neg(t) if masked else None

            def pair(p, carry):
                cs = pl.ds(pl.multiple_of(p * LANES, LANES), LANES)
                q2, k2, v2, kb2 = q_ref[:, cs], k_ref[:, cs], v_ref[:, cs], kb_ref[:, cs]
                pvs, alphas = [], []
                for hh in range(2):
                    hm = lo if hh == 0 else jnp.logical_not(lo)
                    qh, kh = _fox_operands(q2, k2, kb2, lo, hh)
                    s = _dot(qh, kh, NT)
                    if masked:
                        s = s + neg
                    h = 2 * p + hh
                    m_prev = m_scr[h]
                    m_new = jnp.maximum(m_prev, jnp.max(s, axis=1, keepdims=True))
                    pe = jnp.exp(s - jnp.tile(m_new, (1, rep)))
                    alpha = jnp.exp(m_prev - m_new)
                    l_scr[h] = alpha * l_scr[h] + jnp.sum(pe, axis=1, keepdims=True)
                    m_scr[h] = m_new
                    pvs.append(_dot(pe.astype(BF16), jnp.where(hm, v2, 0), NN))
                    alphas.append(alpha)
                acc_scr[:, cs] = acc_scr[:, cs] * jnp.where(lo, alphas[0], alphas[1]) + pvs[0] + pvs[1]
                return carry

            lax.fori_loop(0, N_PAIRS, pair, 0, unroll=4)

        @pl.when(j < qi)
        def _():
            step(False)

        @pl.when(j == qi)
        def _():
            step(True)

        @pl.when(j == nq - 1)
        def _():
            for p in range(N_PAIRS):
                cols = slice(p * LANES, (p + 1) * LANES)
                l2 = jnp.where(lo, l_scr[2 * p], l_scr[2 * p + 1])
                m2 = jnp.where(lo, m_scr[2 * p], m_scr[2 * p + 1])
                o_ref[:, cols] = (acc_scr[:, cols] / l2).astype(BF16)
                lse_ref[:, cols] = m2 + jnp.log(l2)

    kv = lambda col: pl.BlockSpec((t, W), lambda qi, j: (jnp.minimum(j, qi), col))
    own = pl.BlockSpec((t, W), lambda qi, j: (qi, 0))
    return _call(name, body, (nq, nq), [own, kv(1), kv(2), kv(0)], [own, own],
                 [jax.ShapeDtypeStruct((S, W), BF16), jax.ShapeDtypeStruct((S, W), F32)],
                 [pltpu.VMEM((N_HEADS, t, LANES), F32), pltpu.VMEM((N_HEADS, t, LANES), F32), pltpu.VMEM((t, W), F32)],
                 (qkv, qkv, qkv, kbias), ("parallel", "arbitrary"), comm)


def _fox_head_grads(qh, kh, v2, doh, neg, lse_h, d_h, rep):
    s = _dot(qh, kh, NT)
    if neg is not None:
        s = s + neg
    p = jnp.exp(s - jnp.tile(lse_h, (1, rep)))
    return p, p * (_dot(doh, v2, NT) - d_h)


def _fox_bwd(name, qkv, kbias, do, o, lse, t, comm=None):
    S = qkv.shape[0]
    W = D_MODEL
    nq = S // t
    rep = t // LANES

    def body(q_ref, k_ref, v_ref, kb_ref, do_ref, o_ref, lse_ref, dq_ref, dk_ref, dv_ref, rs_ref, dc_ref, dq_scr, dk_scr, dv_scr):
        kb, j = pl.program_id(0), pl.program_id(1)
        lo = _lo_lanes()
        lane = lax.broadcasted_iota(jnp.int32, (1, LANES), 1)
        rows = pl.ds(pl.multiple_of(j * t, t), t)

        @pl.when((kb == 0) & (j == 0))
        def _():
            dq_scr[...] = jnp.zeros_like(dq_scr)
            rs_ref[...] = jnp.zeros_like(rs_ref)

        @pl.when(j == 0)
        def _():
            dk_scr[...] = jnp.zeros_like(dk_scr)
            dv_scr[...] = jnp.zeros_like(dv_scr)
            dc_ref[...] = jnp.zeros_like(dc_ref)

        def step(masked):
            neg = _causal_neg(t) if masked else None

            def pair(p, carry):
                cs = pl.ds(pl.multiple_of(p * LANES, LANES), LANES)
                q2, k2, v2, kb2, do2 = q_ref[:, cs], k_ref[:, cs], v_ref[:, cs], kb_ref[:, cs], do_ref[:, cs]
                dd = do2.astype(F32) * o_ref[:, cs].astype(F32)
                lse_h = _rep_rows(lse_ref[:, cs], lo)
                dq2 = jnp.zeros((t, LANES), F32)
                dv2 = jnp.zeros((t, LANES), F32)
                dk2 = jnp.zeros((t, LANES), F32)
                for hh in range(2):
                    hm = lo if hh == 0 else jnp.logical_not(lo)
                    qh, kh = _fox_operands(q2, k2, kb2, lo, hh)
                    doh = jnp.where(hm, do2, 0)
                    d_h = jnp.sum(jnp.where(hm, dd, 0.0), axis=1, keepdims=True)
                    pr, ds = _fox_head_grads(qh, kh, v2, doh, neg, lse_h[hh], d_h, rep)
                    rs_ref[rows, :] += jnp.where(lane == 2 * p + hh, jnp.sum(ds, axis=1, keepdims=True), 0.0)
                    dc_ref[p, hh:hh + 1, :] += jnp.sum(ds, axis=0, keepdims=True)
                    dsb = ds.astype(BF16)
                    dv2 = dv2 + _dot(pr.astype(BF16), doh, TN)
                    dk2 = dk2 + _dot(dsb, jnp.where(hm, q2, 0), TN)
                    dq2 = dq2 + _dot(dsb, jnp.where(hm, k2, 0), NN)
                dv_scr[:, cs] += dv2
                dk_scr[:, cs] += dk2
                dq_scr[rows, cs] += dq2
                return carry

            lax.fori_loop(0, N_PAIRS, pair, 0, unroll=4)
            if masked:
                dq_ref[...] = (dq_scr[rows, :] * SOFTMAX_SCALE).astype(BF16)

        @pl.when(j > kb)
        def _():
            step(False)

        @pl.when(j == kb)
        def _():
            step(True)

        @pl.when(j == nq - 1)
        def _():
            dv_ref[...] = dv_scr[...].astype(BF16)
            dk_ref[...] = dk_scr[...].astype(BF16)

    qrow = pl.BlockSpec((t, W), lambda kb, j: (jnp.maximum(j, kb), 0))
    krow = lambda col: pl.BlockSpec((t, W), lambda kb, j: (kb, col))
    own = pl.BlockSpec((t, W), lambda kb, j: (kb, 0))
    wide = jax.ShapeDtypeStruct((S, W), BF16)
    return _call(name, body, (nq, nq), [qrow, krow(1), krow(2), krow(0), qrow, qrow, qrow],
                 [own, own, own, pl.BlockSpec((S, LANES), lambda kb, j: (0, 0)), pl.BlockSpec((N_PAIRS, 2, t), lambda kb, j: (0, 0, kb))],
                 [wide, wide, wide, jax.ShapeDtypeStruct((S, LANES), F32), jax.ShapeDtypeStruct((N_PAIRS, 2, S), F32)],
                 [pltpu.VMEM((S, W), F32), pltpu.VMEM((t, W), F32), pltpu.VMEM((t, W), F32)],
                 (qkv, qkv, qkv, kbias, do, o, lse), ("arbitrary", "arbitrary"), comm, vmem=FOX_BWD_VMEM)


def _view_spec(tm, R, index=lambda i: (i, 0)):
    return pl.BlockSpec((tm // R, R * D_MODEL), index)


def _matmul_nt_views(name, a, w, dils, tm=512):
    S, K = a.shape

    def body(a_ref, w_ref, *rest):
        res = _dot(a_ref[...].astype(BF16), w_ref[...], NT)
        _write_views([res[:, b * LANES:(b + 1) * LANES] for b in range(N_PAIRS)], rest[-1], rest[:-1], dils, tm)

    return pl.pallas_call(
        body, grid=(S // tm,), in_specs=[pl.BlockSpec((tm, K), lambda i: (i, 0)), pl.BlockSpec((D_MODEL, K), lambda i: (0, 0))],
        out_specs=[_view_spec(tm, R) for R in dils],
        out_shape=[jax.ShapeDtypeStruct((S // R, R * D_MODEL), BF16) for R in dils],
        scratch_shapes=[pltpu.VMEM((N_PAIRS, tm, LANES), F32)], compiler_params=_params("parallel"), name=name)(a, w)


def _write_views(chunks, scr, out_refs, dils, tm):
    if any(R > 1 for R in dils):
        _stage_chunks(scr, chunks)
    for ref, R in zip(out_refs, dils):
        for b, x in enumerate(chunks):
            if R == 1:
                ref[:, b * LANES:(b + 1) * LANES] = x.astype(ref.dtype)
                continue
            for r in range(R):
                col = r * D_MODEL + b * LANES
                ref[:, col:col + LANES] = _strided_rows(scr, b, r, tm // R, R).astype(ref.dtype)


def _combine(name, os_, lses, dils, tm=256):
    S = os_[0].shape[0] * dils[0]
    G = len(dils)

    def body(*refs):
        o_refs, l_refs = refs[:G], refs[G:2 * G]
        o_outs, l_outs = refs[2 * G:3 * G], refs[3 * G:4 * G]
        stage = refs[4 * G:]
        for g, R in enumerate(dils):
            if R == 1:
                continue
            for src, dst in ((o_refs[g], stage[2 * g]), (l_refs[g], stage[2 * g + 1])):
                _unstride(lambda r, b, src=src: src[:, r * D_MODEL + b * LANES:r * D_MODEL + (b + 1) * LANES], R, dst, tm)
        o_chunks, l_chunks = [], []
        for b in range(N_PAIRS):
            cols = slice(b * LANES, (b + 1) * LANES)
            os_b = [o_refs[g][:, cols] if R == 1 else stage[2 * g][b] for g, R in enumerate(dils)]
            ls = [l_refs[g][:, cols] if R == 1 else stage[2 * g + 1][b] for g, R in enumerate(dils)]
            m = functools.reduce(jnp.maximum, ls)
            ws = [jnp.exp(l - m) for l in ls]
            den = functools.reduce(jnp.add, ws)
            o_chunks.append(functools.reduce(jnp.add, [w * o for w, o in zip(ws, os_b)]) / den)
            l_chunks.append(m + jnp.log(den))
        _write_views(o_chunks, stage[0], o_outs, dils, tm)
        _write_views(l_chunks, stage[1], l_outs, dils, tm)

    specs = [_view_spec(tm, R) for R in dils]
    shapes = lambda dt: [jax.ShapeDtypeStruct((S // R, R * D_MODEL), dt) for R in dils]
    res = pl.pallas_call(
        body, grid=(S // tm,), in_specs=specs * 2, out_specs=specs * 2, out_shape=shapes(BF16) + shapes(F32),
        scratch_shapes=[pltpu.VMEM((N_PAIRS, tm, LANES), F32)] * (2 * G), compiler_params=_params("parallel"),
        name=name)(*os_, *lses)
    return res[:G], res[G:]


def _tri_matmul(tri, x):
    hi, mid, lo = _split3(x)
    return _dot(tri, hi, NN) + _dot(tri, mid, NN) + _dot(tri, lo, NN)


def _split3(x):
    hi = x.astype(BF16)
    r1 = x - hi.astype(F32)
    mid = r1.astype(BF16)
    return hi, mid, (r1 - mid.astype(F32)).astype(BF16)


def _gate_fwd(name, z, bf, tb=512):
    S = z.shape[0]

    def body(z_ref, b_ref, kb_ref, carry):
        @pl.when(pl.program_id(0) == 0)
        def _():
            carry[...] = jnp.zeros_like(carry)

        lf = jax.nn.log_sigmoid(z_ref[...] + b_ref[...])
        ri = lax.broadcasted_iota(jnp.int32, (tb, tb), 0)
        ci = lax.broadcasted_iota(jnp.int32, (tb, tb), 1)
        tri = (ci <= ri).astype(BF16)
        c = _tri_matmul(tri, lf) + carry[...]
        carry[...] = c[tb - 1:tb, :]
        head = lax.broadcasted_iota(jnp.int32, (LANES, D_MODEL), 0)
        col = lax.broadcasted_iota(jnp.int32, (LANES, D_MODEL), 1)
        base = (head >> 1) * LANES + jnp.where((head & 1) == 0, HEAD_DIM, 0)
        kb = jnp.zeros((tb, D_MODEL), F32)
        for i, piece in enumerate(_split3(-c)):
            place = ((col == base + i) & (head < N_HEADS)).astype(BF16)
            kb = kb + _dot(piece, place, NN)
        kb_ref[...] = kb.astype(BF16)

    row = pl.BlockSpec((tb, LANES), lambda i: (i, 0))
    return pl.pallas_call(
        body, grid=(S // tb,), in_specs=[row, pl.BlockSpec((1, LANES), lambda i: (0, 0))],
        out_specs=pl.BlockSpec((tb, D_MODEL), lambda i: (i, 0)), out_shape=jax.ShapeDtypeStruct((S, D_MODEL), BF16),
        scratch_shapes=[pltpu.VMEM((1, LANES), F32)], compiler_params=_params("arbitrary"), name=name)(z, bf)


def _gate_bwd(name, dc, z, bf, tb=512):
    S = z.shape[0]
    nb = S // tb

    def body(dc_ref, z_ref, b_ref, dz_ref, db_ref, carry):
        @pl.when(pl.program_id(0) == 0)
        def _():
            carry[...] = jnp.zeros_like(carry)
            db_ref[...] = jnp.zeros_like(db_ref)

        ri = lax.broadcasted_iota(jnp.int32, (tb, tb), 0)
        ci = lax.broadcasted_iota(jnp.int32, (tb, tb), 1)
        tri = (ci >= ri).astype(BF16)
        dlf = _tri_matmul(tri, dc_ref[...]) + carry[...]
        carry[...] = dlf[0:1, :]
        dz = dlf * jax.nn.sigmoid(-(z_ref[...] + b_ref[...]))
        dz_ref[...] = dz
        db_ref[...] += jnp.sum(dz, axis=0, keepdims=True)

    row = pl.BlockSpec((tb, LANES), lambda i: (nb - 1 - i, 0))
    vec = pl.BlockSpec((1, LANES), lambda i: (0, 0))
    return pl.pallas_call(
        body, grid=(nb,), in_specs=[row, row, vec], out_specs=[row, vec],
        out_shape=[jax.ShapeDtypeStruct((S, LANES), F32), jax.ShapeDtypeStruct((1, LANES), F32)],
        scratch_shapes=[pltpu.VMEM((1, LANES), F32)], compiler_params=_params("arbitrary"), name=name)(dc, z, bf)


def _ffn_gu(name, n, wgu, comm=None, tm=1024):
    S, D = n.shape
    nb = N_DEV // 2

    def body(n_ref, wg_ref, wu_ref, gu_ref, act_ref):
        x = n_ref[...]
        g = _dot(x, wg_ref[...], NN)
        u = _dot(x, wu_ref[...], NN)
        gu_ref[0] = g.astype(BF16)
        gu_ref[1] = u.astype(BF16)
        act_ref[...] = (g * jax.nn.sigmoid(g) * u).astype(BF16)

    return _call(
        name, body, (nb, S // tm),
        [pl.BlockSpec((tm, D), lambda j, i: (i, 0)), pl.BlockSpec((None, D, FF_BLK), lambda j, i: (j, 0, 0)),
         pl.BlockSpec((None, D, FF_BLK), lambda j, i: (j + nb, 0, 0))],
        [pl.BlockSpec((2, None, tm, FF_BLK), lambda j, i: (0, j, i, 0)), pl.BlockSpec((None, tm, FF_BLK), lambda j, i: (j, i, 0))],
        [jax.ShapeDtypeStruct((2, nb, S, FF_BLK), BF16), jax.ShapeDtypeStruct((nb, S, FF_BLK), BF16)], [],
        (n, wgu, wgu), ("parallel", "parallel"), comm)


def _ffn_down(name, act, wd, resid, comm=None, tm=1024):
    nb, S, _ = act.shape
    D = wd.shape[1]

    def epilogue(acc, ex, outs, j):
        outs[0][...] = acc + ex[0][...]

    o_spec = pl.BlockSpec((tm, D), lambda i, j, k: (i, 0))
    return _mm_call(name, (S // tm, 1, nb), act, pl.BlockSpec((None, tm, FF_BLK), lambda i, j, k: (k, i, 0)),
                    wd, pl.BlockSpec((FF_BLK, D), lambda i, j, k: (k, 0)), NN,
                    [jax.ShapeDtypeStruct((S, D), F32)], [o_spec], (tm, D), epilogue, (resid,), (o_spec,), comm=comm)


def _ffn_dact(name, dh, wd, gu, comm=None, tm=512):
    S, D = dh.shape
    nb = N_DEV // 2

    def epilogue(acc, ex, outs, j):
        g = ex[0][0].astype(F32)
        u = ex[0][1].astype(F32)
        sig = jax.nn.sigmoid(g)
        outs[0][0] = (acc * u * (sig * (1.0 + g * (1.0 - sig)))).astype(BF16)
        outs[0][1] = (acc * (g * sig)).astype(BF16)

    gu_spec = pl.BlockSpec((2, None, tm, FF_BLK), lambda j, i, k: (0, j, i, 0))
    return _mm_call(name, (nb, S // tm, 1), dh, pl.BlockSpec((tm, D), lambda j, i, k: (i, 0)),
                    wd, pl.BlockSpec((FF_BLK, D), lambda j, i, k: (j, 0)), NT,
                    [jax.ShapeDtypeStruct((2, nb, S, FF_BLK), BF16)], [gu_spec], (tm, FF_BLK), epilogue, (gu,), (gu_spec,),
                    col_axis=0, comm=comm)


def _ffn_dwgu(name, n, dgu, comm=None, tm=1024, tk=1024):
    S, D = n.shape
    dgu8 = dgu.reshape(N_DEV, S, FF_BLK)
    return _mm_call(name, (N_DEV, D // tm, S // tk), n, pl.BlockSpec((tk, tm), lambda d, i, k: (k, i)),
                    dgu8, pl.BlockSpec((None, tk, FF_BLK), lambda d, i, k: (d, k, 0)), TN,
                    [jax.ShapeDtypeStruct((N_DEV, D, FF_BLK), BF16)],
                    [pl.BlockSpec((None, tm, FF_BLK), lambda d, i, k: (d, i, 0))], (tm, FF_BLK), comm=comm)


def _ffn_dwd(name, act, dh, tk=1024):
    nb, S, _ = act.shape
    D = dh.shape[1]
    out = _mm_call(name, (nb, 1, S // tk), act, pl.BlockSpec((None, tk, FF_BLK), lambda b, j, k: (b, k, 0)),
                   dh, pl.BlockSpec((tk, D), lambda b, j, k: (k, 0)), TN,
                   [jax.ShapeDtypeStruct((nb, FF_BLK, D), BF16)],
                   [pl.BlockSpec((None, FF_BLK, D), lambda b, j, k: (b, 0, 0))], (FF_BLK, D))[0]
    return out.reshape(N_DEV, FF_BLK // 2, D)


def _ffn_dn(name, dgu, wgu, comm=None, tm=1024):
    S = dgu.shape[2]
    D = wgu.shape[1]
    dgu8 = dgu.reshape(N_DEV, S, FF_BLK)
    return _mm_call(name, (S // tm, 1, N_DEV), dgu8, pl.BlockSpec((None, tm, FF_BLK), lambda i, j, k: (k, i, 0)),
                    wgu, pl.BlockSpec((None, D, FF_BLK), lambda i, j, k: (k, 0, 0)), NT,
                    [jax.ShapeDtypeStruct((S, D), F32)], [pl.BlockSpec((tm, D), lambda i, j, k: (i, 0))], (tm, D), comm=comm)


def _adamw(name, parts, w, m, v, tr):
    rows, cols = w.shape
    n_parts = len(parts)
    c1 = 1.0 - ADAM_B1 ** ADAM_STEP
    c2 = 1.0 - ADAM_B2 ** ADAM_STEP

    def body(*refs):
        p_refs = refs[:n_parts]
        w_ref, m_ref, v_ref, g_ref, d_ref, nm_ref, nv_ref = refs[n_parts:]
        g = p_refs[0][...].astype(F32)
        for r in p_refs[1:]:
            g = g + r[...].astype(F32)
        mm = ADAM_B1 * m_ref[...] + (1.0 - ADAM_B1) * g
        vv = ADAM_B2 * v_ref[...] + (1.0 - ADAM_B2) * (g * g)
        g_ref[...] = g
        nm_ref[...] = mm
        nv_ref[...] = vv
        d_ref[...] = -ADAM_LR * ((mm / c1) / (jnp.sqrt(vv / c2) + ADAM_EPS) + ADAM_WD * w_ref[...])

    blk = pl.BlockSpec((tr, cols), lambda i: (i, 0))
    out = jax.ShapeDtypeStruct((rows, cols), F32)
    return pl.pallas_call(
        body, grid=(rows // tr,), in_specs=[blk] * (n_parts + 3), out_specs=[blk] * 4, out_shape=[out] * 4,
        compiler_params=_params("parallel"), name=name)(*parts, w, m, v)


def _position():
    return lax.axis_index("x"), lax.axis_index("y"), lax.axis_index("c")


def _other_chips():
    x, y, _ = _position()
    return [(1 - x, y), (x, 1 - y), (1 - x, 1 - y)]


def _remote(src, dst, send, recv, k, to):
    return pltpu.make_async_remote_copy(src_ref=src, dst_ref=dst, send_sem=send.at[k], recv_sem=recv.at[k],
                                        device_id=to, device_id_type=MESH)


def _ag_send(blocks, direct=False):
    n_peer = 7 if direct else 4

    def copies(ins, outs, send, recv, local, r0=0, l0=0):
        x, y, c = _position()
        me = 4 * x + 2 * y + c
        peers = [(x, y, 1 - c)] + [(px, py, c) for px, py in _other_chips()]
        if direct:
            peers += [(px, py, 1 - c) for px, py in _other_chips()]
        cps = []
        for t, (src, dst) in enumerate(zip(ins, outs)):
            cps.append(pltpu.make_async_copy(src, dst.at[me], local.at[l0 + t]))
            cps += [_remote(src, dst.at[me], send, recv, r0 + n_peer * t + k, to) for k, to in enumerate(peers)]
        return cps

    outs = tuple(jax.ShapeDtypeStruct((N_DEV,) + b.shape, b.dtype) for b in blocks)
    return _Comm(tuple(blocks), outs, {}, copies, n_peer * len(blocks), len(blocks))


def _ag_forward(bufs):
    def copies(ins, outs, send, recv, local, r0=0, l0=0):
        x, y, c = _position()
        cps = []
        for t, buf in enumerate(outs):
            for k, (px, py) in enumerate(_other_chips()):
                slot = buf.at[4 * px + 2 * py + c]
                cps.append(_remote(slot, slot, send, recv, r0 + 3 * t + k, (x, y, 1 - c)))
        return cps

    outs = tuple(jax.ShapeDtypeStruct(b.shape, b.dtype) for b in bufs)
    return _Comm(tuple(bufs), outs, {t: t for t in range(len(bufs))}, copies, 3 * len(bufs), 0)


def _rs_swap(shares):
    def copies(ins, outs, send, recv, local, r0=0, l0=0):
        x, y, c = _position()
        return [_remote(src.at[:, 1 - c], dst, send, recv, r0 + t, (x, y, 1 - c)) for t, (src, dst) in enumerate(zip(ins, outs))]

    ins = tuple(s.reshape((4, 2) + s.shape[1:]) for s in shares)
    outs = tuple(jax.ShapeDtypeStruct((4,) + s.shape[1:], s.dtype) for s in shares)
    return _Comm(ins, outs, {}, copies, len(shares), 0)


def _rs_exchange(sums):
    def copies(ins, outs, send, recv, local, r0=0, l0=0):
        _, _, c = _position()
        return [_remote(src.at[2 * px + py], dst.at[k], send, recv, r0 + 3 * t + k, (px, py, c))
                for t, (src, dst) in enumerate(zip(ins, outs)) for k, (px, py) in enumerate(_other_chips())]

    outs = tuple(jax.ShapeDtypeStruct((3,) + s.shape[1:], s.dtype) for s in sums)
    return _Comm(tuple(sums), outs, {}, copies, 3 * len(sums), 0)


def _comm_call(name, comm):
    return _call(name, lambda: None, (), [], [], [], [], (), (), comm)


def _pair_sum(name, share, got, core, tr):
    _, rows, cols = share.shape

    def body(c_ref, a_ref, b_ref, o_ref):
        o_ref[...] = (a_ref[...].astype(F32) + b_ref[...].astype(F32)).astype(o_ref.dtype)

    grid_spec = pltpu.PrefetchScalarGridSpec(
        num_scalar_prefetch=1, grid=(4, rows // tr),
        in_specs=[pl.BlockSpec((None, None, tr, cols), lambda q, i, c: (q, c[0], i, 0)),
                  pl.BlockSpec((None, tr, cols), lambda q, i, c: (q, i, 0))],
        out_specs=pl.BlockSpec((None, tr, cols), lambda q, i, c: (q, i, 0)))
    return pl.pallas_call(
        body, grid_spec=grid_spec, out_shape=jax.ShapeDtypeStruct((4, rows, cols), share.dtype),
        compiler_params=_params("parallel", "parallel"), name=name)(core, share.reshape(4, 2, rows, cols), got)


TENSORS = ("a_w_in", "a_w_out", "b_w_in", "b_w_out", "gu0", "gu1", "dn0", "dn1")
ROW_TILE = {"a_w_in": 256, "a_w_out": 128, "b_w_in": 256, "b_w_out": 128, "gu0": 256, "gu1": 256, "dn0": 176, "dn1": 176}
A_BLK = 9 * D_MODEL // N_DEV
B_BLK = 386
B_IN = 3 * D_MODEL + N_HEADS
B_IN_PAD = 3 * D_MODEL + LANES


def kernel(x, a_norm, a_w_in, a_w_out, b_norm, b_w_in, b_f, b_w_out, ffn_norm, ffn_w_gu, ffn_w_down, final_norm, loss_target, m_a_norm, m_a_w_in, m_a_w_out, m_b_norm, m_b_w_in, m_b_f, m_b_w_out, m_ffn_norm, m_ffn_w_gu, m_ffn_w_down, m_final_norm, v_a_norm, v_a_w_in, v_a_w_out, v_b_norm, v_b_w_in, v_b_f, v_b_w_out, v_ffn_norm, v_ffn_w_gu, v_ffn_w_down, v_final_norm):
    S = x.shape[1]
    xi, yi, ci = _position()
    dev = 4 * xi + 2 * yi + ci
    core = ci.reshape(1).astype(jnp.int32)
    h0, target = x.reshape(S, D_MODEL), loss_target.reshape(S, D_MODEL)

    def shards(a_in, a_out, b_in, b_out, gu, dn):
        return {"a_w_in": a_in[0], "a_w_out": a_out[0], "b_w_in": b_in[0], "b_w_out": b_out[0],
                "gu0": gu[0], "gu1": gu[1], "dn0": dn[0], "dn1": dn[1]}

    w_sh = shards(a_w_in, a_w_out, b_w_in, b_w_out, ffn_w_gu, ffn_w_down)
    m_sh = shards(m_a_w_in, m_a_w_out, m_b_w_in, m_b_w_out, m_ffn_w_gu, m_ffn_w_down)
    v_sh = shards(v_a_w_in, v_a_w_out, v_b_w_in, v_b_w_out, v_ffn_w_gu, v_ffn_w_down)
    wb = {n: w_sh[n].astype(BF16) for n in TENSORS}
    bf_pad = jnp.pad(b_f, ((0, 0), (0, LANES - N_HEADS)))
    tabs = _rope_tables(S)

    g_ain, g_aout = _comm_call("gather_a", _ag_send([wb["a_w_in"], wb["a_w_out"]]))
    dils = [dil for _, dil in DILATED_PATTERNS]
    n0_views, (g_ain, g_aout) = _rms_fwd("rms_a", h0, a_norm[0], dils, _ag_forward([g_ain, g_aout]))
    n0 = n0_views[0]
    w_a_in = g_ain.transpose(1, 0, 2).reshape(D_MODEL, 9 * D_MODEL)
    sends = [[wb["dn0"], jnp.pad(b_norm, ((0, 7), (0, 0)))], None, [wb["gu0"]]]
    qkv_a, sent = [], {}
    for g, dil in enumerate(dils):
        qkv_g, sent[g] = _a_proj("proj_a%d" % g, n0, w_a_in, g, dil, tabs, None if sends[g] is None else _ag_send(sends[g]))
        qkv_a.append(qkv_g)
    cols = [lambda r: r] * 3
    groups = [(g, dil, S // dil, qkv_a[g]) for g, (window, dil) in enumerate(DILATED_PATTERNS)]
    fwd = [_dil_fwd("dil_fwd%d" % g, view, *cols, dil, L, _ag_send([wb["b_w_in"], wb["b_w_out"]]) if g == 0 else None)
           for g, dil, L, view in groups]
    later = list(fwd[0][2:]) + [sent[2][0]] + list(sent[0])
    o_views, lse_views = _combine("dil_combine", [f[0] for f in fwd], [f[1] for f in fwd], dils)
    o_a = o_views[0]
    w_a_out = g_aout.reshape(D_MODEL, D_MODEL)
    h1, (g_bin, g_bout, g_gu0, g_dn0, g_bnorm) = _matmul("out_a", o_a, w_a_out, "nn", F32, TM, 1024, 1024, resid=h0,
                                                         comm=_ag_forward(later))

    n1 = _rms_fwd("rms_f0", h1, ffn_norm[0])
    gu0, act0 = _ffn_gu("gu_f0", n1, g_gu0)
    w_dn0 = g_dn0.reshape(D_FF, D_MODEL)
    h2 = _ffn_down("down_f0", act0, w_dn0, h1)[0]

    b_norm_full = g_bnorm[:, 0].reshape(D_MODEL)
    w_b_in = g_bin.transpose(1, 0, 2).reshape(D_MODEL, B_IN)
    w_b_gate = jnp.pad(w_b_in[:, 3 * D_MODEL:], ((0, 0), (0, LANES - N_HEADS)))
    w_b_cat = jnp.concatenate([w_b_in[:, :3 * D_MODEL], w_b_gate], axis=1)
    w_b_out = g_bout.reshape(D_MODEL, D_MODEL)
    n2 = _rms_fwd("rms_b", h2, b_norm_full)
    qkv = _matmul("proj_b", n2, w_b_in[:, :3 * D_MODEL], "nn", BF16, TM, 1024, 1024, col0_scale=SOFTMAX_SCALE)
    z = _matmul("gate_b", n2, w_b_gate, "nn", F32, TM, LANES, 1024)
    kbias = _gate_fwd("gate_cumsum", z, bf_pad)
    tf = min(S, 512)
    o_b, lse_b, g_gu1, g_dn1 = _fox_fwd("fox_fwd", qkv, kbias, tf, _ag_send([wb["gu1"], wb["dn1"]]))
    h3, (g_gu1, g_dn1) = _matmul("out_b", o_b, w_b_out, "nn", F32, TM, 1024, 1024, resid=h2, comm=_ag_forward([g_gu1, g_dn1]))

    w_dn1 = g_dn1.reshape(D_FF, D_MODEL)
    n3 = _rms_fwd("rms_f1", h3, ffn_norm[1])
    gu1, act1 = _ffn_gu("gu_f1", n3, g_gu1)
    h4 = _ffn_down("down_f1", act1, w_dn1, h3)[0]

    dh4, d_final, loss, dh4_16 = _loss_head("loss_head", h4, final_norm, target)

    share, got, sums, others = {}, {}, {}, {}

    def pair_sums(*names):
        for n in names:
            sums[n] = _pair_sum("pair_" + n, share[n], got[n], core, ROW_TILE[n])

    dgu1 = _ffn_dact("dact_f1", dh4_16, w_dn1, gu1)[0]
    share["dn1"] = _ffn_dwd("dwd_f1", act1, dh4_16)
    share["gu1"] = _ffn_dwgu("dwgu_f1", n3, dgu1)[0]
    dn3, got["gu1"], got["dn1"] = _ffn_dn("dn_f1", dgu1, g_gu1, _rs_swap([share["gu1"], share["dn1"]]))
    dh3, d_ffn1, dh3_16 = _rms_bwd("rmsb_f1", dn3, h3, ffn_norm[1], dh4)
    pair_sums("gu1", "dn1")

    do_b = _matmul("dout_b", dh3_16, w_b_out, "nt", BF16, TM, 1024, 1024)
    share["b_w_out"] = _matmul("dwout_b", o_b, dh3_16, "tn", BF16, TM, 1024, 1024).reshape(N_DEV, 128, D_MODEL)
    dq_b, dk_b, dv_b, ds_rowsum, ds_colsum, others["gu1"], others["dn1"] = _fox_bwd(
        "fox_bwd", qkv, kbias, do_b, o_b, lse_b, tf, _rs_exchange([sums["gu1"], sums["dn1"]]))
    dc = ds_rowsum[:, :N_HEADS] - ds_colsum.reshape(N_HEADS, S).T
    dz, d_bf = _gate_bwd("gate_bwd", jnp.pad(dc, ((0, 0), (0, LANES - N_HEADS))), z, bf_pad)
    dproj_b = jnp.concatenate([dq_b, dk_b, dv_b, dz.astype(BF16)], axis=1)
    dw_b_in = _matmul("dwin_b", n2, dproj_b, "tn", BF16, TM, B_IN_PAD // 5, 1024)
    dn2 = _matmul("dn_b", dproj_b, w_b_cat, "nt", F32, TM, 1024, B_IN_PAD // 5)
    dh2, d_bnorm, dh2_16 = _rms_bwd("rmsb_b", dn2, h2, b_norm_full, dh3)
    share["b_w_in"] = dw_b_in[:, :B_IN].reshape(D_MODEL, N_DEV, B_BLK).transpose(1, 0, 2)

    dgu0, got["b_w_in"], got["b_w_out"] = _ffn_dact("dact_f0", dh2_16, w_dn0, gu0, _rs_swap([share["b_w_in"], share["b_w_out"]]))
    share["dn0"] = _ffn_dwd("dwd_f0", act0, dh2_16)
    pair_sums("b_w_in", "b_w_out")
    share["gu0"], others["b_w_in"], others["b_w_out"] = _ffn_dwgu(
        "dwgu_f0", n1, dgu0, _rs_exchange([sums["b_w_in"], sums["b_w_out"]]))
    dn1, got["gu0"], got["dn0"] = _ffn_dn("dn_f0", dgu0, g_gu0, _rs_swap([share["gu0"], share["dn0"]]))
    dh1, d_ffn0, dh1_16 = _rms_bwd("rmsb_f0", dn1, h1, ffn_norm[0], dh2)
    pair_sums("gu0", "dn0")

    do_views = _matmul_nt_views("dout_a", dh1_16, w_a_out, dils)
    share["a_w_out"] = _matmul("dwout_a", o_a, dh1_16, "tn", BF16, TM, 1024, 1024).reshape(N_DEV, 128, D_MODEL)
    pieces = []
    for g, dil, L, view in groups:
        rot = tuple(tb.reshape(L, dil * LANES) for tb in tabs)
        res = _dil_bwd("dil_bwd%d" % g, view, do_views[g], o_views[g], lse_views[g], rot, dil, L,
                       _rs_exchange([sums["gu0"], sums["dn0"]]) if g == 0 else None)
        pieces.append(res[:3])
        if g == 0:
            others["gu0"], others["dn0"] = res[3:]
    dws = [_a_dw("dwin_a%d" % g, n0_views[g], pieces[g], dil, None)[0] for g, dil in enumerate(dils)]
    share["a_w_in"] = jnp.concatenate(dws, axis=1).reshape(D_MODEL, N_DEV, A_BLK).transpose(1, 0, 2)
    got["a_w_in"], got["a_w_out"] = _comm_call("swap_a", _rs_swap([share["a_w_in"], share["a_w_out"]]))
    pair_sums("a_w_in", "a_w_out")
    dn0, others["a_w_in"], others["a_w_out"] = _a_dn("dn_a", [p for ps in pieces for p in ps], dils, w_a_in,
                                                     _rs_exchange([sums["a_w_in"], sums["a_w_out"]]))
    dx, d_anorm = _rms_bwd("rmsb_a", dn0, h0, a_norm[0], dh1, copy16=False)

    misc = jnp.concatenate([d_bf[:, :N_HEADS], loss[:, :1], jnp.zeros((1, D_MODEL - N_HEADS - 1), F32)], axis=1)
    small = jnp.concatenate([d_anorm, d_ffn0, d_ffn1, d_final, d_bnorm, misc, jnp.zeros((2, D_MODEL), F32)], axis=0)
    small_all, = _comm_call("gather_small", _ag_send([small], direct=True))

    outs = {}
    for n in TENSORS:
        mine = lax.dynamic_index_in_dim(sums[n], 2 * xi + yi, axis=0, keepdims=False)
        outs[n] = _adamw("adamw_" + n, [mine] + [others[n][k] for k in range(3)], w_sh[n], m_sh[n], v_sh[n], ROW_TILE[n])

    pad_vec = lambda a: jnp.pad(a, ((0, 0), (0, D_MODEL - a.shape[1])))

    def small_pack(an, fn, fin, bf):
        return jnp.concatenate([an, fn, fin.reshape(1, D_MODEL), jnp.zeros((1, D_MODEL), F32), pad_vec(bf),
                                jnp.zeros((2, D_MODEL), F32)], axis=0)

    sg, sd, sm, sv = _adamw("adamw_small", [small_all[d] for d in range(N_DEV)], small_pack(a_norm, ffn_norm, final_norm, b_f),
                            small_pack(m_a_norm, m_ffn_norm, m_final_norm, m_b_f),
                            small_pack(v_a_norm, v_ffn_norm, v_final_norm, v_b_f), 8)
    g_bn = lax.dynamic_slice(sg[4:5], (0, dev * LANES), (1, LANES))
    bn = _adamw("adamw_b_norm", [g_bn], b_norm, m_b_norm, v_b_norm, 1)

    def tree(i):
        full = lambda name, ref: outs[name][i].reshape(ref.shape)
        sml = (sg, sd, sm, sv)[i]
        return dict(
            a_norm=sml[0:1], a_w_in=full("a_w_in", a_w_in), a_w_out=full("a_w_out", a_w_out), b_norm=bn[i],
            b_w_in=full("b_w_in", b_w_in), b_f=sml[5:6, :N_HEADS], b_w_out=full("b_w_out", b_w_out), ffn_norm=sml[1:3],
            ffn_w_gu=jnp.stack([outs["gu0"][i], outs["gu1"][i]]).reshape(ffn_w_gu.shape),
            ffn_w_down=jnp.stack([outs["dn0"][i], outs["dn1"][i]]).reshape(ffn_w_down.shape), final_norm=sml[3])

    order = ("a_norm", "a_w_in", "a_w_out", "b_norm", "b_w_in", "b_f", "b_w_out", "ffn_norm", "ffn_w_gu", "ffn_w_down", "final_norm")
    result = [sg[5, N_HEADS], dx.reshape(x.shape)]
    for i in range(4):
        t = tree(i)
        result += [t[n] for n in order]
    return tuple(result)
```

```python
import functools
from typing import Callable, NamedTuple

import jax
import jax.numpy as jnp
from jax import lax
from jax.experimental import pallas as pl
from jax.experimental.pallas import tpu as pltpu

F32 = jnp.float32
BF16 = jnp.bfloat16

D_MODEL = 1024
N_HEADS = 16
HEAD_DIM = 64
N_PAIRS = N_HEADS // 2
LANES = 128
DILATED_PATTERNS = ((128, 1), (512, 4), (2048, 16))
BAND_STEPS = 128
ROT_DIM = HEAD_DIM // 4
ROPE_THETA = 500000.0
D_FF = 2816
RMS_EPS = 1e-6
NEG_INF = -1e30
SOFTMAX_SCALE = HEAD_DIM ** -0.5
N_DEV = 8
FF_BLK = 2 * D_FF // N_DEV
ADAM_LR, ADAM_B1, ADAM_B2, ADAM_EPS, ADAM_WD, ADAM_STEP = 0.001, 0.9, 0.999, 1e-08, 0.01, 10
VMEM_LIMIT = 52 * 1024 * 1024
FOX_BWD_VMEM = 60 * 1024 * 1024
TM = 1024
MESH = pl.DeviceIdType.MESH

NN = (((1,), (0,)), ((), ()))
NT = (((1,), (1,)), ((), ()))
TN = (((0,), (0,)), ((), ()))


def _params(*sem, vmem=VMEM_LIMIT):
    return pltpu.CompilerParams(dimension_semantics=sem, vmem_limit_bytes=vmem)


def _dot(a, b, dims):
    return lax.dot_general(a, b, dims, preferred_element_type=F32)


class _Comm(NamedTuple):
    ins: tuple
    outs: tuple
    aliases: dict
    copies: Callable
    n_remote: int
    n_local: int


def _call(name, body, grid, in_specs, out_specs, out_shape, scratch, args, sem, comm=None, vmem=VMEM_LIMIT):
    if comm is None:
        return pl.pallas_call(body, grid=grid, in_specs=in_specs, out_specs=out_specs, out_shape=out_shape,
                              scratch_shapes=scratch, compiler_params=_params(*sem, vmem=vmem), name=name)(*args)
    n_in, n_out = len(in_specs), len(out_specs)
    n_ci, n_co = len(comm.ins), len(comm.outs)
    o0 = n_in + n_ci

    def hosted(*refs):
        c_ins, c_outs = refs[n_in:o0], refs[o0 + n_out:o0 + n_out + n_co]
        sems = refs[-3:]

        def start():
            for cp in comm.copies(c_ins, c_outs, *sems):
                cp.start()

        def wait():
            for cp in comm.copies(c_ins, c_outs, *sems):
                cp.wait()

        if not grid:
            start()
            body()
            wait()
            return
        ids = [pl.program_id(ax) for ax in range(len(grid))]
        pl.when(functools.reduce(jnp.logical_and, [i == 0 for i in ids]))(start)
        body(*refs[:n_in], *refs[o0:o0 + n_out], *refs[o0 + n_out + n_co:-3])
        pl.when(functools.reduce(jnp.logical_and, [i == g - 1 for i, g in zip(ids, grid)]))(wait)

    hbm = pl.BlockSpec(memory_space=pltpu.HBM)
    dma = pltpu.SemaphoreType.DMA
    return pl.pallas_call(
        hosted, grid=grid, in_specs=[*in_specs, *[hbm] * n_ci], out_specs=[*out_specs, *[hbm] * n_co],
        out_shape=[*out_shape, *comm.outs], input_output_aliases={n_in + i: n_out + o for i, o in comm.aliases.items()},
        scratch_shapes=[*scratch, dma((comm.n_remote,)), dma((comm.n_remote,)), dma((max(comm.n_local, 1),))],
        compiler_params=_params(*["arbitrary"] * len(grid), vmem=vmem), name=name)(*args, *comm.ins)


def _mm_call(name, grid, a, a_spec, b, b_spec, dims, out_shapes, out_specs, acc_shape, epilogue=None,
             extras=(), extra_specs=(), col_axis=1, comm=None):
    nk = grid[2]
    n_extra = len(extras)
    n_out = len(out_shapes)

    def finish(res, ex, outs, j):
        if epilogue is None:
            outs[0][...] = res.astype(outs[0].dtype)
        else:
            epilogue(res, ex, outs, j)

    def body(*refs):
        a_ref, b_ref = refs[0], refs[1]
        ex = refs[2:2 + n_extra]
        outs = refs[2 + n_extra:2 + n_extra + n_out]
        j, k = pl.program_id(col_axis), pl.program_id(2)
        part = _dot(a_ref[...].astype(BF16), b_ref[...].astype(BF16), dims)
        if nk == 1:
            finish(part, ex, outs, j)
            return
        acc = refs[-1]

        @pl.when(k == 0)
        def _():
            acc[...] = part

        @pl.when((k > 0) & (k < nk - 1))
        def _():
            acc[...] += part

        @pl.when(k == nk - 1)
        def _():
            finish(acc[...] + part, ex, outs, j)

    return _call(name, body, grid, [a_spec, b_spec, *extra_specs], out_specs, out_shapes,
                 [] if nk == 1 else [pltpu.VMEM(acc_shape, F32)], (a, b, *extras), ("parallel", "parallel", "arbitrary"), comm)


def _matmul(name, a, b, mode, out_dtype, tm, tn, tk, resid=None, col0_scale=None, comm=None):
    if mode == "nn":
        (M, K), N = a.shape, b.shape[1]
        a_spec = pl.BlockSpec((tm, tk), lambda j, i, k: (i, k))
        b_spec = pl.BlockSpec((tk, tn), lambda j, i, k: (k, j))
        dims = NN
    elif mode == "nt":
        (M, K), N = a.shape, b.shape[0]
        a_spec = pl.BlockSpec((tm, tk), lambda j, i, k: (i, k))
        b_spec = pl.BlockSpec((tn, tk), lambda j, i, k: (j, k))
        dims = NT
    else:
        (K, M), N = a.shape, b.shape[1]
        a_spec = pl.BlockSpec((tk, tm), lambda j, i, k: (k, i))
        b_spec = pl.BlockSpec((tk, tn), lambda j, i, k: (k, j))
        dims = TN
    assert M % tm == 0 and N % tn == 0 and K % tk == 0, (name, M, N, K, tm, tn, tk)
    o_spec = pl.BlockSpec((tm, tn), lambda j, i, k: (i, j))
    extras, extra_specs, epilogue = (), (), None
    if resid is not None:
        extras, extra_specs = (resid,), (o_spec,)

        def epilogue(acc, ex, outs, j):
            outs[0][...] = (acc + ex[0][...]).astype(outs[0].dtype)

    elif col0_scale is not None:

        def epilogue(acc, ex, outs, j):
            outs[0][...] = (acc * jnp.where(j == 0, col0_scale, 1.0)).astype(outs[0].dtype)

    res = _mm_call(name, (N // tn, M // tm, K // tk), a, a_spec, b, b_spec, dims, [jax.ShapeDtypeStruct((M, N), out_dtype)],
                   [o_spec], (tm, tn), epilogue, extras, extra_specs, col_axis=0, comm=comm)
    return res[0] if comm is None else (res[0], res[1:])


def _rms_fwd(name, h, gain, dils=(1,), comm=None, tm=512):
    S, D = h.shape

    def body(h_ref, g_ref, *rest):
        x = h_ref[...]
        rstd = lax.rsqrt(jnp.mean(x * x, axis=-1, keepdims=True) + RMS_EPS)
        y = x * rstd * g_ref[...]
        _write_views([y[:, b * LANES:(b + 1) * LANES] for b in range(N_PAIRS)], rest[-1], rest[:-1], dils, tm)

    res = _call(name, body, (S // tm,), [pl.BlockSpec((tm, D), lambda i: (i, 0)), pl.BlockSpec((1, D), lambda i: (0, 0))],
                [_view_spec(tm, R) for R in dils], [jax.ShapeDtypeStruct((S // R, R * D), BF16) for R in dils],
                [pltpu.VMEM((N_PAIRS, tm, LANES), F32)], (h, gain.reshape(1, D)), ("parallel",), comm)
    views = res[0] if len(dils) == 1 else res[:len(dils)]
    return views if comm is None else (views, res[len(dils):])


def _rms_bwd(name, dn, h, gain, dres, copy16=True, tm=512):
    S, D = h.shape

    def body(dn_ref, h_ref, g_ref, r_ref, dh_ref, dg_ref, *dh16_ref):
        x = h_ref[...]
        rstd = lax.rsqrt(jnp.mean(x * x, axis=-1, keepdims=True) + RMS_EPS)
        xhat = x * rstd
        d = dn_ref[...]
        dxhat = d * g_ref[...]
        dh = rstd * (dxhat - xhat * jnp.mean(dxhat * xhat, axis=-1, keepdims=True)) + r_ref[...]
        dh_ref[...] = dh
        if copy16:
            dh16_ref[0][...] = dh.astype(BF16)

        @pl.when(pl.program_id(0) == 0)
        def _():
            dg_ref[...] = jnp.zeros_like(dg_ref)

        dg_ref[...] += jnp.sum(d * xhat, axis=0, keepdims=True)

    row = pl.BlockSpec((tm, D), lambda i: (i, 0))
    vec = pl.BlockSpec((1, D), lambda i: (0, 0))
    return pl.pallas_call(
        body, grid=(S // tm,), in_specs=[row, row, vec, row], out_specs=[row, vec] + [row] * copy16,
        out_shape=[jax.ShapeDtypeStruct((S, D), F32), jax.ShapeDtypeStruct((1, D), F32)] + [jax.ShapeDtypeStruct((S, D), BF16)] * copy16,
        compiler_params=_params("arbitrary"), name=name)(dn, h, gain.reshape(1, D), dres)


def _loss_head(name, h, gain, target, tm=512):
    S, D = h.shape

    def body(h_ref, g_ref, t_ref, dh_ref, dg_ref, loss_ref, dh16_ref):
        x = h_ref[...]
        rstd = lax.rsqrt(jnp.mean(x * x, axis=-1, keepdims=True) + RMS_EPS)
        xhat = x * rstd
        err = xhat * g_ref[...] - t_ref[...]
        dy = err * (1.0 / D)
        dxhat = dy * g_ref[...]
        dh = rstd * (dxhat - xhat * jnp.mean(dxhat * xhat, axis=-1, keepdims=True))
        dh_ref[...] = dh
        dh16_ref[...] = dh.astype(BF16)

        @pl.when(pl.program_id(0) == 0)
        def _():
            dg_ref[...] = jnp.zeros_like(dg_ref)
            loss_ref[...] = jnp.zeros_like(loss_ref)

        dg_ref[...] += jnp.sum(dy * xhat, axis=0, keepdims=True)
        part = 0.5 * jnp.sum(jnp.mean(err * err, axis=-1, keepdims=True), axis=0, keepdims=True)
        loss_ref[...] += jnp.broadcast_to(part, loss_ref.shape)

    row = pl.BlockSpec((tm, D), lambda i: (i, 0))
    vec = pl.BlockSpec((1, D), lambda i: (0, 0))
    return pl.pallas_call(
        body, grid=(S // tm,), in_specs=[row, vec, row], out_specs=[row, vec, pl.BlockSpec((1, LANES), lambda i: (0, 0)), row],
        out_shape=[jax.ShapeDtypeStruct((S, D), F32), jax.ShapeDtypeStruct((1, D), F32),
                   jax.ShapeDtypeStruct((1, LANES), F32), jax.ShapeDtypeStruct((S, D), BF16)],
        compiler_params=_params("arbitrary"), name=name)(h, gain.reshape(1, D), target)


def _rope_tables(S):
    half = ROT_DIM // 2
    inv_freq = ROPE_THETA ** (-jnp.arange(half, dtype=F32) * 2.0 / ROT_DIM)
    ang = jnp.arange(S, dtype=F32)[:, None] * inv_freq[None, :]
    cos, sin = jnp.cos(ang), jnp.sin(ang)
    one = jnp.ones((S, HEAD_DIM - ROT_DIM), F32)
    zero = jnp.zeros((S, HEAD_DIM - ROT_DIM), F32)
    zh = jnp.zeros((S, half), F32)
    c = jnp.concatenate([cos, cos, one], axis=1)
    sa = jnp.concatenate([-sin, zh, zero], axis=1)
    sb = jnp.concatenate([zh, sin, zero], axis=1)
    return tuple(jnp.concatenate([t, t], axis=1) for t in (c, sa, sb))


def _rotate(x, c, sa, sb, sign):
    return x * c + sign * (pltpu.roll(x, LANES - ROT_DIM // 2, 1) * sa + pltpu.roll(x, ROT_DIM // 2, 1) * sb)


def _stage_chunks(scr, chunks):
    for c, x in enumerate(chunks):
        scr[c] = x


def _strided_rows(scr, c, r, n, R):
    return scr.at[c][pl.ds(r, n, stride=R), :]


def _a_proj(name, n, w, g, R, tabs, comm, tm=1024):
    S, D = n.shape
    n_i = S // tm
    n_out = 3

    def body(n_ref, w_ref, c_ref, sa_ref, sb_ref, *rest):
        outs, scr = rest[:n_out], rest[n_out]
        j = pl.program_id(0)
        acc = _dot(n_ref[...], w_ref[...], NN)
        c, sa, sb = c_ref[...], sa_ref[...], sb_ref[...]
        for J in range(n_out):
            kind = J

            @pl.when(j == J)
            def _(J=J, kind=kind):
                chunks = [acc[:, b * LANES:(b + 1) * LANES] for b in range(N_PAIRS)]
                if kind < 2:
                    chunks = [_rotate(x, c, sa, sb, 1.0) * (SOFTMAX_SCALE if kind == 0 else 1.0) for x in chunks]
                if R == 1:
                    for b, x in enumerate(chunks):
                        outs[J][:, b * LANES:(b + 1) * LANES] = x.astype(BF16)
                    return
                _stage_chunks(scr, chunks)
                for r in range(R):
                    for b in range(N_PAIRS):
                        col = r * D_MODEL + b * LANES
                        outs[J][:, col:col + LANES] = _strided_rows(scr, b, r, tm // R, R).astype(BF16)

    def out_spec(J):
        return pl.BlockSpec((tm // R, R * D_MODEL), lambda j, i: (jnp.where(j == J, i, jnp.where(j < J, 0, n_i - 1)), 0))

    tab = pl.BlockSpec((tm, LANES), lambda j, i: (i, 0))
    res = _call(name, body, (n_out, n_i),
                [pl.BlockSpec((tm, D), lambda j, i: (i, 0)), pl.BlockSpec((D, D_MODEL), lambda j, i: (0, 3 * g + j)), tab, tab, tab],
                [out_spec(J) for J in range(n_out)], [jax.ShapeDtypeStruct((S // R, R * D_MODEL), BF16)] * n_out,
                [pltpu.VMEM((N_PAIRS, tm, LANES), F32)], (n, w, *tabs), ("arbitrary", "arbitrary"), comm)
    return res[:n_out], res[n_out:]


def _unstride(src_chunk, R, tok, rows):
    for r in range(R):
        for b in range(N_PAIRS):
            tok.at[b][pl.ds(r, rows // R, stride=R), :] = src_chunk(r, b).astype(F32)


def _by_residue(ref, R):
    return ref[...] if R == 1 else jnp.concatenate([ref[:, r * D_MODEL:(r + 1) * D_MODEL] for r in range(R)], axis=0)


def _a_dw(name, n_view, pieces, R, comm, tk=1024):
    D = D_MODEL
    S = n_view.shape[0] * R
    n_k = S // tk
    n_p = len(pieces)

    def body(n_ref, *rest):
        p_refs, o_ref, acc = rest[:n_p], rest[n_p], rest[n_p + 1]
        j, k = pl.program_id(0), pl.program_id(1)
        for J in range(n_p):

            @pl.when(j == J)
            def _(J=J):
                part = _dot(_by_residue(n_ref, R), _by_residue(p_refs[J], R), TN)

                @pl.when(k == 0)
                def _():
                    acc[...] = part

                @pl.when((k > 0) & (k < n_k - 1))
                def _():
                    acc[...] += part

                @pl.when(k == n_k - 1)
                def _():
                    o_ref[...] = (acc[...] + part).astype(BF16)

    def piece_spec(J):
        return pl.BlockSpec((tk // R, R * D_MODEL), lambda j, k: (jnp.where(j == J, k, jnp.where(j < J, 0, n_k - 1)), 0))

    return _call(name, body, (n_p, n_k), [pl.BlockSpec((tk // R, R * D_MODEL), lambda j, k: (k, 0))] + [piece_spec(J) for J in range(n_p)],
                 [pl.BlockSpec((D, D_MODEL), lambda j, k: (0, j))], [jax.ShapeDtypeStruct((D, n_p * D_MODEL), BF16)],
                 [pltpu.VMEM((D, D_MODEL), F32)], (n_view, *pieces), ("arbitrary", "arbitrary"), comm)


def _a_dn(name, pieces, dils, w, comm, tm=512):
    D = w.shape[0]
    S = pieces[0].shape[0] * dils[0]
    n_p = len(pieces)

    def body(*refs):
        p_refs, w_ref, o_ref, acc, part_acc, tok = refs[:n_p], refs[n_p], refs[n_p + 1], refs[n_p + 2], refs[n_p + 3], refs[n_p + 4]
        j = pl.program_id(1)
        for J in range(n_p):

            @pl.when(j == J)
            def _(J=J):
                R, t = dils[J // 3], J % 3
                part = _dot(_by_residue(p_refs[J], R), w_ref[...], NT)
                if R == 1:
                    if J == 0:
                        acc[...] = part
                    else:
                        acc[...] += part
                    return
                if t == 0:
                    part_acc[...] = part
                    return
                if t == 1:
                    part_acc[...] += part
                    return
                n = tm // R
                _unstride(lambda r, b: part_acc[r * n:(r + 1) * n, b * LANES:(b + 1) * LANES]
                          + part[r * n:(r + 1) * n, b * LANES:(b + 1) * LANES], R, tok, tm)
                total = acc[...] + jnp.concatenate([tok[b] for b in range(N_PAIRS)], axis=1)
                if J == n_p - 1:
                    o_ref[...] = total
                else:
                    acc[...] = total

    specs = [pl.BlockSpec((tm // dils[J // 3], dils[J // 3] * D_MODEL), lambda i, j: (i, 0)) for J in range(n_p)]
    return _call(name, body, (S // tm, n_p), specs + [pl.BlockSpec((D, D_MODEL), lambda i, j: (0, j))],
                 [pl.BlockSpec((tm, D), lambda i, j: (i, 0))], [jax.ShapeDtypeStruct((S, D), F32)],
                 [pltpu.VMEM((tm, D), F32), pltpu.VMEM((tm, D), F32), pltpu.VMEM((N_PAIRS, tm, LANES), F32)], (*pieces, w),
                 ("arbitrary", "arbitrary"), comm)


def _lo_lanes():
    return lax.broadcasted_iota(jnp.int32, (1, LANES), 1) < HEAD_DIM


def _rep_rows(x2, lo):
    sw = pltpu.roll(x2, HEAD_DIM, 1)
    return jnp.where(lo, x2, sw), jnp.where(lo, sw, x2)


def _pair_cols(h):
    return slice((h // 2) * LANES, (h // 2 + 1) * LANES)


def _head_lanes(lo, h):
    return lo if h % 2 == 0 else jnp.logical_not(lo)


def _band_masks(t, first):
    ri = lax.broadcasted_iota(jnp.int32, (t, t), 0)
    ci = lax.broadcasted_iota(jnp.int32, (t, t), 1)
    neg_prev = jnp.where((ci >= ri) & jnp.logical_not(first), 0.0, NEG_INF)
    neg_cur = jnp.where(ci <= ri, 0.0, NEG_INF)
    return neg_prev, neg_cur


def _dil_specs(L, R, t, qcol, kcol, vcol):
    W = D_MODEL
    prev = lambda qi: jnp.maximum(qi - 1, 0)
    return dict(
        q=pl.BlockSpec((t, W), lambda r, qi: (qi, qcol(r))),
        kp=pl.BlockSpec((t, W), lambda r, qi: (prev(qi), kcol(r))), kc=pl.BlockSpec((t, W), lambda r, qi: (qi, kcol(r))),
        vp=pl.BlockSpec((t, W), lambda r, qi: (prev(qi), vcol(r))), vc=pl.BlockSpec((t, W), lambda r, qi: (qi, vcol(r))),
        own=pl.BlockSpec((t, W), lambda r, qi: (qi, r)), tab=pl.BlockSpec((t, LANES), lambda r, qi: (qi, r)))


def _dil_fwd(name, x, qcol, kcol, vcol, R, L, comm=None):
    t = BAND_STEPS
    W = D_MODEL
    sp = _dil_specs(L, R, t, qcol, kcol, vcol)

    def body(q_ref, kp_ref, kc_ref, vp_ref, vc_ref, o_ref, lse_ref):
        lo = _lo_lanes()
        neg_p, neg_c = _band_masks(t, pl.program_id(1) == 0)
        s_p, s_c = [], []
        for h in range(N_HEADS):
            cols = _pair_cols(h)
            qh = jnp.where(_head_lanes(lo, h), q_ref[:, cols], 0)
            s_p.append(_dot(qh, kp_ref[:, cols], NT))
            s_c.append(_dot(qh, kc_ref[:, cols], NT))
        s_p = jnp.stack(s_p) + neg_p[None]
        s_c = jnp.stack(s_c) + neg_c[None]
        m = jnp.maximum(jnp.max(s_p, axis=2, keepdims=True), jnp.max(s_c, axis=2, keepdims=True))
        p_p, p_c = jnp.exp(s_p - m), jnp.exp(s_c - m)
        l = jnp.sum(p_p, axis=2, keepdims=True) + jnp.sum(p_c, axis=2, keepdims=True)
        inv, lse = 1.0 / l, m + jnp.log(l)
        p_p, p_c = p_p.astype(BF16), p_c.astype(BF16)
        for p in range(N_PAIRS):
            cols = _pair_cols(2 * p)
            o2 = jnp.zeros((t, LANES), F32)
            for h in (2 * p, 2 * p + 1):
                hm = _head_lanes(lo, h)
                pv = _dot(p_p[h], jnp.where(hm, vp_ref[:, cols], 0), NN) + _dot(p_c[h], jnp.where(hm, vc_ref[:, cols], 0), NN)
                o2 = o2 + pv * inv[h]
            o_ref[:, cols] = o2.astype(BF16)
            lse_ref[:, cols] = jnp.where(lo, lse[2 * p], lse[2 * p + 1])

    return _call(name, body, (R, L // t), [sp["q"], sp["kp"], sp["kc"], sp["vp"], sp["vc"]], [sp["own"], sp["own"]],
                 [jax.ShapeDtypeStruct((L, R * W), BF16), jax.ShapeDtypeStruct((L, R * W), F32)], [],
                 (x[0], x[1], x[1], x[2], x[2]), ("parallel", "parallel"), comm)


def _dil_scores(lo, q_ref, do_ref, o_ref, lse_ref, kv_refs):
    s = [[] for _ in kv_refs]
    dp = [[] for _ in kv_refs]
    lse, d = [], []
    for h in range(N_HEADS):
        cols = _pair_cols(h)
        hm = _head_lanes(lo, h)
        qh, doh = jnp.where(hm, q_ref[:, cols], 0), jnp.where(hm, do_ref[:, cols], 0)
        for i, (k_ref, v_ref) in enumerate(kv_refs):
            s[i].append(_dot(qh, k_ref[:, cols], NT))
            dp[i].append(_dot(doh, v_ref[:, cols], NT))
        lse.append(_rep_rows(lse_ref[:, cols], lo)[h % 2])
        dd = do_ref[:, cols].astype(F32) * o_ref[:, cols].astype(F32)
        d.append(jnp.sum(jnp.where(hm, dd, 0.0), axis=1, keepdims=True))
    return (*[jnp.stack(x) for x in s], *[jnp.stack(x) for x in dp], jnp.stack(lse), jnp.stack(d))


def _dil_bwd(name, x, do, o, lse, tabs, R, L, comm=None):
    t = BAND_STEPS
    W = D_MODEL
    nq = L // t
    qb = lambda step: nq - 1 - step
    kb = lambda step: jnp.maximum(qb(step) - 1, 0)
    at = lambda f, width: pl.BlockSpec((t, width), lambda r, step: (f(step), r))

    def body(q_ref, kp_ref, kc_ref, vp_ref, vc_ref, do_ref, o_ref, lse_ref, c_ref, sa_ref, sb_ref, dq_ref, dk_ref, dv_ref,
             dk_scr, dv_scr):
        qi = nq - 1 - pl.program_id(1)
        lo = _lo_lanes()
        unrotate = lambda x: _rotate(x, c_ref[...], sa_ref[...], sb_ref[...], -1.0).astype(BF16)

        @pl.when(qi == nq - 1)
        def _():
            dk_scr[...] = jnp.zeros_like(dk_scr)
            dv_scr[...] = jnp.zeros_like(dv_scr)

        neg_p, neg_c = _band_masks(t, qi == 0)
        s_p, s_c, dp_p, dp_c, lse_h, d = _dil_scores(lo, q_ref, do_ref, o_ref, lse_ref, ((kp_ref, vp_ref), (kc_ref, vc_ref)))
        p_p, p_c = jnp.exp(s_p + neg_p[None] - lse_h), jnp.exp(s_c + neg_c[None] - lse_h)
        ds_p, ds_c = (p_p * (dp_p - d)).astype(BF16), (p_c * (dp_c - d)).astype(BF16)
        p_p, p_c = p_p.astype(BF16), p_c.astype(BF16)
        for p in range(N_PAIRS):
            cols = _pair_cols(2 * p)
            dq2 = jnp.zeros((t, LANES), F32)
            dk_cur, dv_cur = dk_scr[:, cols], dv_scr[:, cols]
            dk_prev, dv_prev = jnp.zeros((t, LANES), F32), jnp.zeros((t, LANES), F32)
            for h in (2 * p, 2 * p + 1):
                hm = _head_lanes(lo, h)
                qh, doh = jnp.where(hm, q_ref[:, cols], 0), jnp.where(hm, do_ref[:, cols], 0)
                dq2 = dq2 + _dot(ds_p[h], jnp.where(hm, kp_ref[:, cols], 0), NN) + _dot(ds_c[h], jnp.where(hm, kc_ref[:, cols], 0), NN)
                dk_prev, dv_prev = dk_prev + _dot(ds_p[h], qh, TN), dv_prev + _dot(p_p[h], doh, TN)
                dk_cur, dv_cur = dk_cur + _dot(ds_c[h], qh, TN), dv_cur + _dot(p_c[h], doh, TN)
            dq_ref[:, cols] = unrotate(dq2 * SOFTMAX_SCALE)
            dk_ref[:, cols] = unrotate(dk_cur)
            dv_ref[:, cols] = dv_cur.astype(BF16)
            dk_scr[:, cols] = dk_prev
            dv_scr[:, cols] = dv_prev

    wide = jax.ShapeDtypeStruct((L, R * W), BF16)
    return _call(name, body, (R, nq),
                 [at(qb, W), at(kb, W), at(qb, W), at(kb, W), at(qb, W), at(qb, W), at(qb, W), at(qb, W),
                  at(qb, LANES), at(qb, LANES), at(qb, LANES)],
                 [at(qb, W), at(qb, W), at(qb, W)], [wide, wide, wide], [pltpu.VMEM((t, W), F32), pltpu.VMEM((t, W), F32)],
                 (x[0], x[1], x[1], x[2], x[2], do, o, lse, *tabs), ("parallel", "arbitrary"), comm)


def _fox_operands(q2, k2, kb2, lo, hh):
    lane = lax.broadcasted_iota(jnp.int32, (1, LANES), 1)
    if hh == 0:
        ones = ((lane >= HEAD_DIM) & (lane < HEAD_DIM + 3)).astype(BF16)
        return jnp.where(lo, q2, ones), jnp.where(lo, k2, kb2)
    ones = (lane < 3).astype(BF16)
    return jnp.where(lo, ones, q2), jnp.where(lo, kb2, k2)


def _causal_neg(t):
    ri = lax.broadcasted_iota(jnp.int32, (t, t), 0)
    ci = lax.broadcasted_iota(jnp.int32, (t, t), 1)
    return jnp.where(ci <= ri, 0.0, NEG_INF)


def _fox_fwd(name, qkv, kbias, t, comm=None):
    S = qkv.shape[0]
    W = D_MODEL
    nq = S // t
    rep = t // LANES

    def body(q_ref, k_ref, v_ref, kb_ref, o_ref, lse_ref, m_scr, l_scr, acc_scr):
        qi, j = pl.program_id(0), pl.program_id(1)
        lo = _lo_lanes()

        @pl.when(j == 0)
        def _():
            m_scr[...] = jnp.full_like(m_scr, NEG_INF)
            l_scr[...] = jnp.zeros_like(l_scr)
            acc_scr[...] = jnp.zeros_like(acc_scr)

        def step(masked):
            neg = _causal_neg(t) if masked else None

            def pair(p, carry):
                cs = pl.ds(pl.multiple_of(p * LANES, LANES), LANES)
                q2, k2, v2, kb2 = q_ref[:, cs], k_ref[:, cs], v_ref[:, cs], kb_ref[:, cs]
                pvs, alphas = [], []
                for hh in range(2):
                    hm = lo if hh == 0 else jnp.logical_not(lo)
                    qh, kh = _fox_operands(q2, k2, kb2, lo, hh)
                    s = _dot(qh, kh, NT)
                    if masked:
                        s = s + neg
                    h = 2 * p + hh
                    m_prev = m_scr[h]
                    m_new = jnp.maximum(m_prev, jnp.max(s, axis=1, keepdims=True))
                    pe = jnp.exp(s - jnp.tile(m_new, (1, rep)))
                    alpha = jnp.exp(m_prev - m_new)
                    l_scr[h] = alpha * l_scr[h] + jnp.sum(pe, axis=1, keepdims=True)
                    m_scr[h] = m_new
                    pvs.append(_dot(pe.astype(BF16), jnp.where(hm, v2, 0), NN))
                    alphas.append(alpha)
                acc_scr[:, cs] = acc_scr[:, cs] * jnp.where(lo, alphas[0], alphas[1]) + pvs[0] + pvs[1]
                return carry

            lax.fori_loop(0, N_PAIRS, pair, 0, unroll=4)

        @pl.when(j < qi)
        def _():
            step(False)

        @pl.when(j == qi)
        def _():
            step(True)

        @pl.when(j == nq - 1)
        def _():
            for p in range(N_PAIRS):
                cols = slice(p * LANES, (p + 1) * LANES)
                l2 = jnp.where(lo, l_scr[2 * p], l_scr[2 * p + 1])
                m2 = jnp.where(lo, m_scr[2 * p], m_scr[2 * p + 1])
                o_ref[:, cols] = (acc_scr[:, cols] / l2).astype(BF16)
                lse_ref[:, cols] = m2 + jnp.log(l2)

    kv = lambda col: pl.BlockSpec((t, W), lambda qi, j: (jnp.minimum(j, qi), col))
    own = pl.BlockSpec((t, W), lambda qi, j: (qi, 0))
    return _call(name, body, (nq, nq), [own, kv(1), kv(2), kv(0)], [own, own],
                 [jax.ShapeDtypeStruct((S, W), BF16), jax.ShapeDtypeStruct((S, W), F32)],
                 [pltpu.VMEM((N_HEADS, t, LANES), F32), pltpu.VMEM((N_HEADS, t, LANES), F32), pltpu.VMEM((t, W), F32)],
                 (qkv, qkv, qkv, kbias), ("parallel", "arbitrary"), comm)


def _fox_head_grads(qh, kh, v2, doh, neg, lse_h, d_h, rep):
    s = _dot(qh, kh, NT)
    if neg is not None:
        s = s + neg
    p = jnp.exp(s - jnp.tile(lse_h, (1, rep)))
    return p, p * (_dot(doh, v2, NT) - d_h)


def _fox_bwd(name, qkv, kbias, do, o, lse, t, comm=None):
    S = qkv.shape[0]
    W = D_MODEL
    nq = S // t
    rep = t // LANES

    def body(q_ref, k_ref, v_ref, kb_ref, do_ref, o_ref, lse_ref, dq_ref, dk_ref, dv_ref, rs_ref, dc_ref, dq_scr, dk_scr, dv_scr):
        kb, j = pl.program_id(0), pl.program_id(1)
        lo = _lo_lanes()
        lane = lax.broadcasted_iota(jnp.int32, (1, LANES), 1)
        rows = pl.ds(pl.multiple_of(j * t, t), t)

        @pl.when((kb == 0) & (j == 0))
        def _():
            dq_scr[...] = jnp.zeros_like(dq_scr)
            rs_ref[...] = jnp.zeros_like(rs_ref)

        @pl.when(j == 0)
        def _():
            dk_scr[...] = jnp.zeros_like(dk_scr)
            dv_scr[...] = jnp.zeros_like(dv_scr)
            dc_ref[...] = jnp.zeros_like(dc_ref)

        def step(masked):
            neg = _causal_neg(t) if masked else None

            def pair(p, carry):
                cs = pl.ds(pl.multiple_of(p * LANES, LANES), LANES)
                q2, k2, v2, kb2, do2 = q_ref[:, cs], k_ref[:, cs], v_ref[:, cs], kb_ref[:, cs], do_ref[:, cs]
                dd = do2.astype(F32) * o_ref[:, cs].astype(F32)
                lse_h = _rep_rows(lse_ref[:, cs], lo)
                dq2 = jnp.zeros((t, LANES), F32)
                dv2 = jnp.zeros((t, LANES), F32)
                dk2 = jnp.zeros((t, LANES), F32)
                for hh in range(2):
                    hm = lo if hh == 0 else jnp.logical_not(lo)
                    qh, kh = _fox_operands(q2, k2, kb2, lo, hh)
                    doh = jnp.where(hm, do2, 0)
                    d_h = jnp.sum(jnp.where(hm, dd, 0.0), axis=1, keepdims=True)
                    pr, ds = _fox_head_grads(qh, kh, v2, doh, neg, lse_h[hh], d_h, rep)
                    rs_ref[rows, :] += jnp.where(lane == 2 * p + hh, jnp.sum(ds, axis=1, keepdims=True), 0.0)
                    dc_ref[p, hh:hh + 1, :] += jnp.sum(ds, axis=0, keepdims=True)
                    dsb = ds.astype(BF16)
                    dv2 = dv2 + _dot(pr.astype(BF16), doh, TN)
                    dk2 = dk2 + _dot(dsb, jnp.where(hm, q2, 0), TN)
                    dq2 = dq2 + _dot(dsb, jnp.where(hm, k2, 0), NN)
                dv_scr[:, cs] += dv2
                dk_scr[:, cs] += dk2
                dq_scr[rows, cs] += dq2
                return carry

            lax.fori_loop(0, N_PAIRS, pair, 0, unroll=4)
            if masked:
                dq_ref[...] = (dq_scr[rows, :] * SOFTMAX_SCALE).astype(BF16)

        @pl.when(j > kb)
        def _():
            step(False)

        @pl.when(j == kb)
        def _():
            step(True)

        @pl.when(j == nq - 1)
        def _():
            dv_ref[...] = dv_scr[...].astype(BF16)
            dk_ref[...] = dk_scr[...].astype(BF16)

    qrow = pl.BlockSpec((t, W), lambda kb, j: (jnp.maximum(j, kb), 0))
    krow = lambda col: pl.BlockSpec((t, W), lambda kb, j: (kb, col))
    own = pl.BlockSpec((t, W), lambda kb, j: (kb, 0))
    wide = jax.ShapeDtypeStruct((S, W), BF16)
    return _call(name, body, (nq, nq), [qrow, krow(1), krow(2), krow(0), qrow, qrow, qrow],
                 [own, own, own, pl.BlockSpec((S, LANES), lambda kb, j: (0, 0)), pl.BlockSpec((N_PAIRS, 2, t), lambda kb, j: (0, 0, kb))],
                 [wide, wide, wide, jax.ShapeDtypeStruct((S, LANES), F32), jax.ShapeDtypeStruct((N_PAIRS, 2, S), F32)],
                 [pltpu.VMEM((S, W), F32), pltpu.VMEM((t, W), F32), pltpu.VMEM((t, W), F32)],
                 (qkv, qkv, qkv, kbias, do, o, lse), ("arbitrary", "arbitrary"), comm, vmem=FOX_BWD_VMEM)


def _view_spec(tm, R, index=lambda i: (i, 0)):
    return pl.BlockSpec((tm // R, R * D_MODEL), index)


def _matmul_nt_views(name, a, w, dils, tm=512):
    S, K = a.shape

    def body(a_ref, w_ref, *rest):
        res = _dot(a_ref[...].astype(BF16), w_ref[...], NT)
        _write_views([res[:, b * LANES:(b + 1) * LANES] for b in range(N_PAIRS)], rest[-1], rest[:-1], dils, tm)

    return pl.pallas_call(
        body, grid=(S // tm,), in_specs=[pl.BlockSpec((tm, K), lambda i: (i, 0)), pl.BlockSpec((D_MODEL, K), lambda i: (0, 0))],
        out_specs=[_view_spec(tm, R) for R in dils],
        out_shape=[jax.ShapeDtypeStruct((S // R, R * D_MODEL), BF16) for R in dils],
        scratch_shapes=[pltpu.VMEM((N_PAIRS, tm, LANES), F32)], compiler_params=_params("parallel"), name=name)(a, w)


def _write_views(chunks, scr, out_refs, dils, tm):
    if any(R > 1 for R in dils):
        _stage_chunks(scr, chunks)
    for ref, R in zip(out_refs, dils):
        for b, x in enumerate(chunks):
            if R == 1:
                ref[:, b * LANES:(b + 1) * LANES] = x.astype(ref.dtype)
                continue
            for r in range(R):
                col = r * D_MODEL + b * LANES
                ref[:, col:col + LANES] = _strided_rows(scr, b, r, tm // R, R).astype(ref.dtype)


def _combine(name, os_, lses, dils, tm=256):
    S = os_[0].shape[0] * dils[0]
    G = len(dils)

    def body(*refs):
        o_refs, l_refs = refs[:G], refs[G:2 * G]
        o_outs, l_outs = refs[2 * G:3 * G], refs[3 * G:4 * G]
        stage = refs[4 * G:]
        for g, R in enumerate(dils):
            if R == 1:
                continue
            for src, dst in ((o_refs[g], stage[2 * g]), (l_refs[g], stage[2 * g + 1])):
                _unstride(lambda r, b, src=src: src[:, r * D_MODEL + b * LANES:r * D_MODEL + (b + 1) * LANES], R, dst, tm)
        o_chunks, l_chunks = [], []
        for b in range(N_PAIRS):
            cols = slice(b * LANES, (b + 1) * LANES)
            os_b = [o_refs[g][:, cols] if R == 1 else stage[2 * g][b] for g, R in enumerate(dils)]
            ls = [l_refs[g][:, cols] if R == 1 else stage[2 * g + 1][b] for g, R in enumerate(dils)]
            m = functools.reduce(jnp.maximum, ls)
            ws = [jnp.exp(l - m) for l in ls]
            den = functools.reduce(jnp.add, ws)
            o_chunks.append(functools.reduce(jnp.add, [w * o for w, o in zip(ws, os_b)]) / den)
            l_chunks.append(m + jnp.log(den))
        _write_views(o_chunks, stage[0], o_outs, dils, tm)
        _write_views(l_chunks, stage[1], l_outs, dils, tm)

    specs = [_view_spec(tm, R) for R in dils]
    shapes = lambda dt: [jax.ShapeDtypeStruct((S // R, R * D_MODEL), dt) for R in dils]
    res = pl.pallas_call(
        body, grid=(S // tm,), in_specs=specs * 2, out_specs=specs * 2, out_shape=shapes(BF16) + shapes(F32),
        scratch_shapes=[pltpu.VMEM((N_PAIRS, tm, LANES), F32)] * (2 * G), compiler_params=_params("parallel"),
        name=name)(*os_, *lses)
    return res[:G], res[G:]


def _tri_matmul(tri, x):
    hi, mid, lo = _split3(x)
    return _dot(tri, hi, NN) + _dot(tri, mid, NN) + _dot(tri, lo, NN)


def _split3(x):
    hi = x.astype(BF16)
    r1 = x - hi.astype(F32)
    mid = r1.astype(BF16)
    return hi, mid, (r1 - mid.astype(F32)).astype(BF16)


def _gate_fwd(name, z, bf, tb=512):
    S = z.shape[0]

    def body(z_ref, b_ref, kb_ref, carry):
        @pl.when(pl.program_id(0) == 0)
        def _():
            carry[...] = jnp.zeros_like(carry)

        lf = jax.nn.log_sigmoid(z_ref[...] + b_ref[...])
        ri = lax.broadcasted_iota(jnp.int32, (tb, tb), 0)
        ci = lax.broadcasted_iota(jnp.int32, (tb, tb), 1)
        tri = (ci <= ri).astype(BF16)
        c = _tri_matmul(tri, lf) + carry[...]
        carry[...] = c[tb - 1:tb, :]
        head = lax.broadcasted_iota(jnp.int32, (LANES, D_MODEL), 0)
        col = lax.broadcasted_iota(jnp.int32, (LANES, D_MODEL), 1)
        base = (head >> 1) * LANES + jnp.where((head & 1) == 0, HEAD_DIM, 0)
        kb = jnp.zeros((tb, D_MODEL), F32)
        for i, piece in enumerate(_split3(-c)):
            place = ((col == base + i) & (head < N_HEADS)).astype(BF16)
            kb = kb + _dot(piece, place, NN)
        kb_ref[...] = kb.astype(BF16)

    row = pl.BlockSpec((tb, LANES), lambda i: (i, 0))
    return pl.pallas_call(
        body, grid=(S // tb,), in_specs=[row, pl.BlockSpec((1, LANES), lambda i: (0, 0))],
        out_specs=pl.BlockSpec((tb, D_MODEL), lambda i: (i, 0)), out_shape=jax.ShapeDtypeStruct((S, D_MODEL), BF16),
        scratch_shapes=[pltpu.VMEM((1, LANES), F32)], compiler_params=_params("arbitrary"), name=name)(z, bf)


def _gate_bwd(name, dc, z, bf, tb=512):
    S = z.shape[0]
    nb = S // tb

    def body(dc_ref, z_ref, b_ref, dz_ref, db_ref, carry):
        @pl.when(pl.program_id(0) == 0)
        def _():
            carry[...] = jnp.zeros_like(carry)
            db_ref[...] = jnp.zeros_like(db_ref)

        ri = lax.broadcasted_iota(jnp.int32, (tb, tb), 0)
        ci = lax.broadcasted_iota(jnp.int32, (tb, tb), 1)
        tri = (ci >= ri).astype(BF16)
        dlf = _tri_matmul(tri, dc_ref[...]) + carry[...]
        carry[...] = dlf[0:1, :]
        dz = dlf * jax.nn.sigmoid(-(z_ref[...] + b_ref[...]))
        dz_ref[...] = dz
        db_ref[...] += jnp.sum(dz, axis=0, keepdims=True)

    row = pl.BlockSpec((tb, LANES), lambda i: (nb - 1 - i, 0))
    vec = pl.BlockSpec((1, LANES), lambda i: (0, 0))
    return pl.pallas_call(
        body, grid=(nb,), in_specs=[row, row, vec], out_specs=[row, vec],
        out_shape=[jax.ShapeDtypeStruct((S, LANES), F32), jax.ShapeDtypeStruct((1, LANES), F32)],
        scratch_shapes=[pltpu.VMEM((1, LANES), F32)], compiler_params=_params("arbitrary"), name=name)(dc, z, bf)


def _ffn_gu(name, n, wgu, comm=None, tm=1024):
    S, D = n.shape
    nb = N_DEV // 2

    def body(n_ref, wg_ref, wu_ref, gu_ref, act_ref):
        x = n_ref[...]
        g = _dot(x, wg_ref[...], NN)
        u = _dot(x, wu_ref[...], NN)
        gu_ref[0] = g.astype(BF16)
        gu_ref[1] = u.astype(BF16)
        act_ref[...] = (g * jax.nn.sigmoid(g) * u).astype(BF16)

    return _call(
        name, body, (nb, S // tm),
        [pl.BlockSpec((tm, D), lambda j, i: (i, 0)), pl.BlockSpec((None, D, FF_BLK), lambda j, i: (j, 0, 0)),
         pl.BlockSpec((None, D, FF_BLK), lambda j, i: (j + nb, 0, 0))],
        [pl.BlockSpec((2, None, tm, FF_BLK), lambda j, i: (0, j, i, 0)), pl.BlockSpec((None, tm, FF_BLK), lambda j, i: (j, i, 0))],
        [jax.ShapeDtypeStruct((2, nb, S, FF_BLK), BF16), jax.ShapeDtypeStruct((nb, S, FF_BLK), BF16)], [],
        (n, wgu, wgu), ("parallel", "parallel"), comm)


def _ffn_down(name, act, wd, resid, comm=None, tm=1024):
    nb, S, _ = act.shape
    D = wd.shape[1]

    def epilogue(acc, ex, outs, j):
        outs[0][...] = acc + ex[0][...]

    o_spec = pl.BlockSpec((tm, D), lambda i, j, k: (i, 0))
    return _mm_call(name, (S // tm, 1, nb), act, pl.BlockSpec((None, tm, FF_BLK), lambda i, j, k: (k, i, 0)),
                    wd, pl.BlockSpec((FF_BLK, D), lambda i, j, k: (k, 0)), NN,
                    [jax.ShapeDtypeStruct((S, D), F32)], [o_spec], (tm, D), epilogue, (resid,), (o_spec,), comm=comm)


def _ffn_dact(name, dh, wd, gu, comm=None, tm=512):
    S, D = dh.shape
    nb = N_DEV // 2

    def epilogue(acc, ex, outs, j):
        g = ex[0][0].astype(F32)
        u = ex[0][1].astype(F32)
        sig = jax.nn.sigmoid(g)
        outs[0][0] = (acc * u * (sig * (1.0 + g * (1.0 - sig)))).astype(BF16)
        outs[0][1] = (acc * (g * sig)).astype(BF16)

    gu_spec = pl.BlockSpec((2, None, tm, FF_BLK), lambda j, i, k: (0, j, i, 0))
    return _mm_call(name, (nb, S // tm, 1), dh, pl.BlockSpec((tm, D), lambda j, i, k: (i, 0)),
                    wd, pl.BlockSpec((FF_BLK, D), lambda j, i, k: (j, 0)), NT,
                    [jax.ShapeDtypeStruct((2, nb, S, FF_BLK), BF16)], [gu_spec], (tm, FF_BLK), epilogue, (gu,), (gu_spec,),
                    col_axis=0, comm=comm)


def _ffn_dwgu(name, n, dgu, comm=None, tm=1024, tk=1024):
    S, D = n.shape
    dgu8 = dgu.reshape(N_DEV, S, FF_BLK)
    return _mm_call(name, (N_DEV, D // tm, S // tk), n, pl.BlockSpec((tk, tm), lambda d, i, k: (k, i)),
                    dgu8, pl.BlockSpec((None, tk, FF_BLK), lambda d, i, k: (d, k, 0)), TN,
                    [jax.ShapeDtypeStruct((N_DEV, D, FF_BLK), BF16)],
                    [pl.BlockSpec((None, tm, FF_BLK), lambda d, i, k: (d, i, 0))], (tm, FF_BLK), comm=comm)


def _ffn_dwd(name, act, dh, tk=1024):
    nb, S, _ = act.shape
    D = dh.shape[1]
    out = _mm_call(name, (nb, 1, S // tk), act, pl.BlockSpec((None, tk, FF_BLK), lambda b, j, k: (b, k, 0)),
                   dh, pl.BlockSpec((tk, D), lambda b, j, k: (k, 0)), TN,
                   [jax.ShapeDtypeStruct((nb, FF_BLK, D), BF16)],
                   [pl.BlockSpec((None, FF_BLK, D), lambda b, j, k: (b, 0, 0))], (FF_BLK, D))[0]
    return out.reshape(N_DEV, FF_BLK // 2, D)


def _ffn_dn(name, dgu, wgu, comm=None, tm=1024):
    S = dgu.shape[2]
    D = wgu.shape[1]
    dgu8 = dgu.reshape(N_DEV, S, FF_BLK)
    return _mm_call(name, (S // tm, 1, N_DEV), dgu8, pl.BlockSpec((None, tm, FF_BLK), lambda i, j, k: (k, i, 0)),
                    wgu, pl.BlockSpec((None, D, FF_BLK), lambda i, j, k: (k, 0, 0)), NT,
                    [jax.ShapeDtypeStruct((S, D), F32)], [pl.BlockSpec((tm, D), lambda i, j, k: (i, 0))], (tm, D), comm=comm)


def _adamw(name, parts, w, m, v, tr):
    rows, cols = w.shape
    n_parts = len(parts)
    c1 = 1.0 - ADAM_B1 ** ADAM_STEP
    c2 = 1.0 - ADAM_B2 ** ADAM_STEP

    def body(*refs):
        p_refs = refs[:n_parts]
        w_ref, m_ref, v_ref, g_ref, d_ref, nm_ref, nv_ref = refs[n_parts:]
        g = p_refs[0][...].astype(F32)
        for r in p_refs[1:]:
            g = g + r[...].astype(F32)
        mm = ADAM_B1 * m_ref[...] + (1.0 - ADAM_B1) * g
        vv = ADAM_B2 * v_ref[...] + (1.0 - ADAM_B2) * (g * g)
        g_ref[...] = g
        nm_ref[...] = mm
        nv_ref[...] = vv
        d_ref[...] = -ADAM_LR * ((mm / c1) / (jnp.sqrt(vv / c2) + ADAM_EPS) + ADAM_WD * w_ref[...])

    blk = pl.BlockSpec((tr, cols), lambda i: (i, 0))
    out = jax.ShapeDtypeStruct((rows, cols), F32)
    return pl.pallas_call(
        body, grid=(rows // tr,), in_specs=[blk] * (n_parts + 3), out_specs=[blk] * 4, out_shape=[out] * 4,
        compiler_params=_params("parallel"), name=name)(*parts, w, m, v)


def _position():
    return lax.axis_index("x"), lax.axis_index("y"), lax.axis_index("c")


def _other_chips():
    x, y, _ = _position()
    return [(1 - x, y), (x, 1 - y), (1 - x, 1 - y)]


def _remote(src, dst, send, recv, k, to):
    return pltpu.make_async_remote_copy(src_ref=src, dst_ref=dst, send_sem=send.at[k], recv_sem=recv.at[k],
                                        device_id=to, device_id_type=MESH)


def _ag_send(blocks, direct=False):
    n_peer = 7 if direct else 4

    def copies(ins, outs, send, recv, local, r0=0, l0=0):
        x, y, c = _position()
        me = 4 * x + 2 * y + c
        peers = [(x, y, 1 - c)] + [(px, py, c) for px, py in _other_chips()]
        if direct:
            peers += [(px, py, 1 - c) for px, py in _other_chips()]
        cps = []
        for t, (src, dst) in enumerate(zip(ins, outs)):
            cps.append(pltpu.make_async_copy(src, dst.at[me], local.at[l0 + t]))
            cps += [_remote(src, dst.at[me], send, recv, r0 + n_peer * t + k, to) for k, to in enumerate(peers)]
        return cps

    outs = tuple(jax.ShapeDtypeStruct((N_DEV,) + b.shape, b.dtype) for b in blocks)
    return _Comm(tuple(blocks), outs, {}, copies, n_peer * len(blocks), len(blocks))


def _ag_forward(bufs):
    def copies(ins, outs, send, recv, local, r0=0, l0=0):
        x, y, c = _position()
        cps = []
        for t, buf in enumerate(outs):
            for k, (px, py) in enumerate(_other_chips()):
                slot = buf.at[4 * px + 2 * py + c]
                cps.append(_remote(slot, slot, send, recv, r0 + 3 * t + k, (x, y, 1 - c)))
        return cps

    outs = tuple(jax.ShapeDtypeStruct(b.shape, b.dtype) for b in bufs)
    return _Comm(tuple(bufs), outs, {t: t for t in range(len(bufs))}, copies, 3 * len(bufs), 0)


def _rs_swap(shares):
    def copies(ins, outs, send, recv, local, r0=0, l0=0):
        x, y, c = _position()
        return [_remote(src.at[:, 1 - c], dst, send, recv, r0 + t, (x, y, 1 - c)) for t, (src, dst) in enumerate(zip(ins, outs))]

    ins = tuple(s.reshape((4, 2) + s.shape[1:]) for s in shares)
    outs = tuple(jax.ShapeDtypeStruct((4,) + s.shape[1:], s.dtype) for s in shares)
    return _Comm(ins, outs, {}, copies, len(shares), 0)


def _rs_exchange(sums):
    def copies(ins, outs, send, recv, local, r0=0, l0=0):
        _, _, c = _position()
        return [_remote(src.at[2 * px + py], dst.at[k], send, recv, r0 + 3 * t + k, (px, py, c))
                for t, (src, dst) in enumerate(zip(ins, outs)) for k, (px, py) in enumerate(_other_chips())]

    outs = tuple(jax.ShapeDtypeStruct((3,) + s.shape[1:], s.dtype) for s in sums)
    return _Comm(tuple(sums), outs, {}, copies, 3 * len(sums), 0)


def _comm_call(name, comm):
    return _call(name, lambda: None, (), [], [], [], [], (), (), comm)


def _pair_sum(name, share, got, core, tr):
    _, rows, cols = share.shape

    def body(c_ref, a_ref, b_ref, o_ref):
        o_ref[...] = (a_ref[...].astype(F32) + b_ref[...].astype(F32)).astype(o_ref.dtype)

    grid_spec = pltpu.PrefetchScalarGridSpec(
        num_scalar_prefetch=1, grid=(4, rows // tr),
        in_specs=[pl.BlockSpec((None, None, tr, cols), lambda q, i, c: (q, c[0], i, 0)),
                  pl.BlockSpec((None, tr, cols), lambda q, i, c: (q, i, 0))],
        out_specs=pl.BlockSpec((None, tr, cols), lambda q, i, c: (q, i, 0)))
    return pl.pallas_call(
        body, grid_spec=grid_spec, out_shape=jax.ShapeDtypeStruct((4, rows, cols), share.dtype),
        compiler_params=_params("parallel", "parallel"), name=name)(core, share.reshape(4, 2, rows, cols), got)


TENSORS = ("a_w_in", "a_w_out", "b_w_in", "b_w_out", "gu0", "gu1", "dn0", "dn1")
ROW_TILE = {"a_w_in": 256, "a_w_out": 128, "b_w_in": 256, "b_w_out": 128, "gu0": 256, "gu1": 256, "dn0": 176, "dn1": 176}
A_BLK = 9 * D_MODEL // N_DEV
B_BLK = 386
B_IN = 3 * D_MODEL + N_HEADS
B_IN_PAD = 3 * D_MODEL + LANES


def kernel(x, a_norm, a_w_in, a_w_out, b_norm, b_w_in, b_f, b_w_out, ffn_norm, ffn_w_gu, ffn_w_down, final_norm, loss_target, m_a_norm, m_a_w_in, m_a_w_out, m_b_norm, m_b_w_in, m_b_f, m_b_w_out, m_ffn_norm, m_ffn_w_gu, m_ffn_w_down, m_final_norm, v_a_norm, v_a_w_in, v_a_w_out, v_b_norm, v_b_w_in, v_b_f, v_b_w_out, v_ffn_norm, v_ffn_w_gu, v_ffn_w_down, v_final_norm):
    S = x.shape[1]
    xi, yi, ci = _position()
    dev = 4 * xi + 2 * yi + ci
    core = ci.reshape(1).astype(jnp.int32)
    h0, target = x.reshape(S, D_MODEL), loss_target.reshape(S, D_MODEL)

    def shards(a_in, a_out, b_in, b_out, gu, dn):
        return {"a_w_in": a_in[0], "a_w_out": a_out[0], "b_w_in": b_in[0], "b_w_out": b_out[0],
                "gu0": gu[0], "gu1": gu[1], "dn0": dn[0], "dn1": dn[1]}

    w_sh = shards(a_w_in, a_w_out, b_w_in, b_w_out, ffn_w_gu, ffn_w_down)
    m_sh = shards(m_a_w_in, m_a_w_out, m_b_w_in, m_b_w_out, m_ffn_w_gu, m_ffn_w_down)
    v_sh = shards(v_a_w_in, v_a_w_out, v_b_w_in, v_b_w_out, v_ffn_w_gu, v_ffn_w_down)
    wb = {n: w_sh[n].astype(BF16) for n in TENSORS}
    bf_pad = jnp.pad(b_f, ((0, 0), (0, LANES - N_HEADS)))
    tabs = _rope_tables(S)

    g_ain, g_aout = _comm_call("gather_a", _ag_send([wb["a_w_in"], wb["a_w_out"]]))
    dils = [dil for _, dil in DILATED_PATTERNS]
    n0_views, (g_ain, g_aout) = _rms_fwd("rms_a", h0, a_norm[0], dils, _ag_forward([g_ain, g_aout]))
    n0 = n0_views[0]
    w_a_in = g_ain.transpose(1, 0, 2).reshape(D_MODEL, 9 * D_MODEL)
    sends = [[wb["dn0"], jnp.pad(b_norm, ((0, 7), (0, 0)))], None, [wb["gu0"]]]
    qkv_a, sent = [], {}
    for g, dil in enumerate(dils):
        qkv_g, sent[g] = _a_proj("proj_a%d" % g, n0, w_a_in, g, dil, tabs, None if sends[g] is None else _ag_send(sends[g]))
        qkv_a.append(qkv_g)
    cols = [lambda r: r] * 3
    groups = [(g, dil, S // dil, qkv_a[g]) for g, (window, dil) in enumerate(DILATED_PATTERNS)]
    fwd = [_dil_fwd("dil_fwd%d" % g, view, *cols, dil, L, _ag_send([wb["b_w_in"], wb["b_w_out"]]) if g == 0 else None)
           for g, dil, L, view in groups]
    later = list(fwd[0][2:]) + [sent[2][0]] + list(sent[0])
    o_views, lse_views = _combine("dil_combine", [f[0] for f in fwd], [f[1] for f in fwd], dils)
    o_a = o_views[0]
    w_a_out = g_aout.reshape(D_MODEL, D_MODEL)
    h1, (g_bin, g_bout, g_gu0, g_dn0, g_bnorm) = _matmul("out_a", o_a, w_a_out, "nn", F32, TM, 1024, 1024, resid=h0,
                                                         comm=_ag_forward(later))

    n1 = _rms_fwd("rms_f0", h1, ffn_norm[0])
    gu0, act0 = _ffn_gu("gu_f0", n1, g_gu0)
    w_dn0 = g_dn0.reshape(D_FF, D_MODEL)
    h2 = _ffn_down("down_f0", act0, w_dn0, h1)[0]

    b_norm_full = g_bnorm[:, 0].reshape(D_MODEL)
    w_b_in = g_bin.transpose(1, 0, 2).reshape(D_MODEL, B_IN)
    w_b_gate = jnp.pad(w_b_in[:, 3 * D_MODEL:], ((0, 0), (0, LANES - N_HEADS)))
    w_b_cat = jnp.concatenate([w_b_in[:, :3 * D_MODEL], w_b_gate], axis=1)
    w_b_out = g_bout.reshape(D_MODEL, D_MODEL)
    n2 = _rms_fwd("rms_b", h2, b_norm_full)
    qkv = _matmul("proj_b", n2, w_b_in[:, :3 * D_MODEL], "nn", BF16, TM, 1024, 1024, col0_scale=SOFTMAX_SCALE)
    z = _matmul("gate_b", n2, w_b_gate, "nn", F32, TM, LANES, 1024)
    kbias = _gate_fwd("gate_cumsum", z, bf_pad)
    tf = min(S, 512)
    o_b, lse_b, g_gu1, g_dn1 = _fox_fwd("fox_fwd", qkv, kbias, tf, _ag_send([wb["gu1"], wb["dn1"]]))
    h3, (g_gu1, g_dn1) = _matmul("out_b", o_b, w_b_out, "nn", F32, TM, 1024, 1024, resid=h2, comm=_ag_forward([g_gu1, g_dn1]))

    w_dn1 = g_dn1.reshape(D_FF, D_MODEL)
    n3 = _rms_fwd("rms_f1", h3, ffn_norm[1])
    gu1, act1 = _ffn_gu("gu_f1", n3, g_gu1)
    h4 = _ffn_down("down_f1", act1, w_dn1, h3)[0]

    dh4, d_final, loss, dh4_16 = _loss_head("loss_head", h4, final_norm, target)

    share, got, sums, others = {}, {}, {}, {}

    def pair_sums(*names):
        for n in names:
            sums[n] = _pair_sum("pair_" + n, share[n], got[n], core, ROW_TILE[n])

    dgu1 = _ffn_dact("dact_f1", dh4_16, w_dn1, gu1)[0]
    share["dn1"] = _ffn_dwd("dwd_f1", act1, dh4_16)
    share["gu1"] = _ffn_dwgu("dwgu_f1", n3, dgu1)[0]
    dn3, got["gu1"], got["dn1"] = _ffn_dn("dn_f1", dgu1, g_gu1, _rs_swap([share["gu1"], share["dn1"]]))
    dh3, d_ffn1, dh3_16 = _rms_bwd("rmsb_f1", dn3, h3, ffn_norm[1], dh4)
    pair_sums("gu1", "dn1")

    do_b = _matmul("dout_b", dh3_16, w_b_out, "nt", BF16, TM, 1024, 1024)
    share["b_w_out"] = _matmul("dwout_b", o_b, dh3_16, "tn", BF16, TM, 1024, 1024).reshape(N_DEV, 128, D_MODEL)
    dq_b, dk_b, dv_b, ds_rowsum, ds_colsum, others["gu1"], others["dn1"] = _fox_bwd(
        "fox_bwd", qkv, kbias, do_b, o_b, lse_b, tf, _rs_exchange([sums["gu1"], sums["dn1"]]))
    dc = ds_rowsum[:, :N_HEADS] - ds_colsum.reshape(N_HEADS, S).T
    dz, d_bf = _gate_bwd("gate_bwd", jnp.pad(dc, ((0, 0), (0, LANES - N_HEADS))), z, bf_pad)
    dproj_b = jnp.concatenate([dq_b, dk_b, dv_b, dz.astype(BF16)], axis=1)
    dw_b_in = _matmul("dwin_b", n2, dproj_b, "tn", BF16, TM, B_IN_PAD // 5, 1024)
    dn2 = _matmul("dn_b", dproj_b, w_b_cat, "nt", F32, TM, 1024, B_IN_PAD // 5)
    dh2, d_bnorm, dh2_16 = _rms_bwd("rmsb_b", dn2, h2, b_norm_full, dh3)
    share["b_w_in"] = dw_b_in[:, :B_IN].reshape(D_MODEL, N_DEV, B_BLK).transpose(1, 0, 2)

    dgu0, got["b_w_in"], got["b_w_out"] = _ffn_dact("dact_f0", dh2_16, w_dn0, gu0, _rs_swap([share["b_w_in"], share["b_w_out"]]))
    share["dn0"] = _ffn_dwd("dwd_f0", act0, dh2_16)
    pair_sums("b_w_in", "b_w_out")
    share["gu0"], others["b_w_in"], others["b_w_out"] = _ffn_dwgu(
        "dwgu_f0", n1, dgu0, _rs_exchange([sums["b_w_in"], sums["b_w_out"]]))
    dn1, got["gu0"], got["dn0"] = _ffn_dn("dn_f0", dgu0, g_gu0, _rs_swap([share["gu0"], share["dn0"]]))
    dh1, d_ffn0, dh1_16 = _rms_bwd("rmsb_f0", dn1, h1, ffn_norm[0], dh2)
    pair_sums("gu0", "dn0")

    do_views = _matmul_nt_views("dout_a", dh1_16, w_a_out, dils)
    share["a_w_out"] = _matmul("dwout_a", o_a, dh1_16, "tn", BF16, TM, 1024, 1024).reshape(N_DEV, 128, D_MODEL)
    pieces = []
    for g, dil, L, view in groups:
        rot = tuple(tb.reshape(L, dil * LANES) for tb in tabs)
        res = _dil_bwd("dil_bwd%d" % g, view, do_views[g], o_views[g], lse_views[g], rot, dil, L,
                       _rs_exchange([sums["gu0"], sums["dn0"]]) if g == 0 else None)
        pieces.append(res[:3])
        if g == 0:
            others["gu0"], others["dn0"] = res[3:]
    dws = [_a_dw("dwin_a%d" % g, n0_views[g], pieces[g], dil, None)[0] for g, dil in enumerate(dils)]
    share["a_w_in"] = jnp.concatenate(dws, axis=1).reshape(D_MODEL, N_DEV, A_BLK).transpose(1, 0, 2)
    got["a_w_in"], got["a_w_out"] = _comm_call("swap_a", _rs_swap([share["a_w_in"], share["a_w_out"]]))
    pair_sums("a_w_in", "a_w_out")
    dn0, others["a_w_in"], others["a_w_out"] = _a_dn("dn_a", [p for ps in pieces for p in ps], dils, w_a_in,
                                                     _rs_exchange([sums["a_w_in"], sums["a_w_out"]]))
    dx, d_anorm = _rms_bwd("rmsb_a", dn0, h0, a_norm[0], dh1, copy16=False)

    misc = jnp.concatenate([d_bf[:, :N_HEADS], loss[:, :1], jnp.zeros((1, D_MODEL - N_HEADS - 1), F32)], axis=1)
    small = jnp.concatenate([d_anorm, d_ffn0, d_ffn1, d_final, d_bnorm, misc, jnp.zeros((2, D_MODEL), F32)], axis=0)
    small_all, = _comm_call("gather_small", _ag_send([small], direct=True))

    outs = {}
    for n in TENSORS:
        mine = lax.dynamic_index_in_dim(sums[n], 2 * xi + yi, axis=0, keepdims=False)
        outs[n] = _adamw("adamw_" + n, [mine] + [others[n][k] for k in range(3)], w_sh[n], m_sh[n], v_sh[n], ROW_TILE[n])

    pad_vec = lambda a: jnp.pad(a, ((0, 0), (0, D_MODEL - a.shape[1])))

    def small_pack(an, fn, fin, bf):
        return jnp.concatenate([an, fn, fin.reshape(1, D_MODEL), jnp.zeros((1, D_MODEL), F32), pad_vec(bf),
                                jnp.zeros((2, D_MODEL), F32)], axis=0)

    sg, sd, sm, sv = _adamw("adamw_small", [small_all[d] for d in range(N_DEV)], small_pack(a_norm, ffn_norm, final_norm, b_f),
                            small_pack(m_a_norm, m_ffn_norm, m_final_norm, m_b_f),
                            small_pack(v_a_norm, v_ffn_norm, v_final_norm, v_b_f), 8)
    g_bn = lax.dynamic_slice(sg[4:5], (0, dev * LANES), (1, LANES))
    bn = _adamw("adamw_b_norm", [g_bn], b_norm, m_b_norm, v_b_norm, 1)

    def tree(i):
        full = lambda name, ref: outs[name][i].reshape(ref.shape)
        sml = (sg, sd, sm, sv)[i]
        return dict(
            a_norm=sml[0:1], a_w_in=full("a_w_in", a_w_in), a_w_out=full("a_w_out", a_w_out), b_norm=bn[i],
            b_w_in=full("b_w_in", b_w_in), b_f=sml[5:6, :N_HEADS], b_w_out=full("b_w_out", b_w_out), ffn_norm=sml[1:3],
            ffn_w_gu=jnp.stack([outs["gu0"][i], outs["gu1"][i]]).reshape(ffn_w_gu.shape),
            ffn_w_down=jnp.stack([outs["dn0"][i], outs["dn1"][i]]).reshape(ffn_w_down.shape), final_norm=sml[3])

    order = ("a_norm", "a_w_in", "a_w_out", "b_norm", "b_w_in", "b_f", "b_w_out", "ffn_norm", "ffn_w_gu", "ffn_w_down", "final_norm")
    result = [sg[5, N_HEADS], dx.reshape(x.shape)]
    for i in range(4):
        t = tree(i)
        result += [t[n] for n in order]
    return tuple(result)
```

```python
import functools
from typing import Callable, NamedTuple

import jax
import jax.numpy as jnp
from jax import lax
from jax.experimental import pallas as pl
from jax.experimental.pallas import tpu as pltpu

F32 = jnp.float32
BF16 = jnp.bfloat16

D_MODEL = 1024
N_HEADS = 16
HEAD_DIM = 64
N_PAIRS = N_HEADS // 2
LANES = 128
DILATED_PATTERNS = ((128, 1), (512, 4), (2048, 16))
BAND_STEPS = 128
ROT_DIM = HEAD_DIM // 4
ROPE_THETA = 500000.0
D_FF = 2816
RMS_EPS = 1e-6
NEG_INF = -1e30
SOFTMAX_SCALE = HEAD_DIM ** -0.5
N_DEV = 8
FF_BLK = 2 * D_FF // N_DEV
ADAM_LR, ADAM_B1, ADAM_B2, ADAM_EPS, ADAM_WD, ADAM_STEP = 0.001, 0.9, 0.999, 1e-08, 0.01, 10
VMEM_LIMIT = 52 * 1024 * 1024
FOX_BWD_VMEM = 60 * 1024 * 1024
TM = 1024
MESH = pl.DeviceIdType.MESH

NN = (((1,), (0,)), ((), ()))
NT = (((1,), (1,)), ((), ()))
TN = (((0,), (0,)), ((), ()))


def _params(*sem, vmem=VMEM_LIMIT):
    return pltpu.CompilerParams(dimension_semantics=sem, vmem_limit_bytes=vmem)


def _dot(a, b, dims):
    return lax.dot_general(a, b, dims, preferred_element_type=F32)


class _Comm(NamedTuple):
    ins: tuple
    outs: tuple
    aliases: dict
    copies: Callable
    n_remote: int
    n_local: int


def _call(name, body, grid, in_specs, out_specs, out_shape, scratch, args, sem, comm=None, vmem=VMEM_LIMIT):
    if comm is None:
        return pl.pallas_call(body, grid=grid, in_specs=in_specs, out_specs=out_specs, out_shape=out_shape,
                              scratch_shapes=scratch, compiler_params=_params(*sem, vmem=vmem), name=name)(*args)
    n_in, n_out = len(in_specs), len(out_specs)
    n_ci, n_co = len(comm.ins), len(comm.outs)
    o0 = n_in + n_ci

    def hosted(*refs):
        c_ins, c_outs = refs[n_in:o0], refs[o0 + n_out:o0 + n_out + n_co]
        sems = refs[-3:]

        def start():
            for cp in comm.copies(c_ins, c_outs, *sems):
                cp.start()

        def wait():
            for cp in comm.copies(c_ins, c_outs, *sems):
                cp.wait()

        if not grid:
            start()
            body()
            wait()
            return
        ids = [pl.program_id(ax) for ax in range(len(grid))]
        pl.when(functools.reduce(jnp.logical_and, [i == 0 for i in ids]))(start)
        body(*refs[:n_in], *refs[o0:o0 + n_out], *refs[o0 + n_out + n_co:-3])
        pl.when(functools.reduce(jnp.logical_and, [i == g - 1 for i, g in zip(ids, grid)]))(wait)

    hbm = pl.BlockSpec(memory_space=pltpu.HBM)
    dma = pltpu.SemaphoreType.DMA
    return pl.pallas_call(
        hosted, grid=grid, in_specs=[*in_specs, *[hbm] * n_ci], out_specs=[*out_specs, *[hbm] * n_co],
        out_shape=[*out_shape, *comm.outs], input_output_aliases={n_in + i: n_out + o for i, o in comm.aliases.items()},
        scratch_shapes=[*scratch, dma((comm.n_remote,)), dma((comm.n_remote,)), dma((max(comm.n_local, 1),))],
        compiler_params=_params(*["arbitrary"] * len(grid), vmem=vmem), name=name)(*args, *comm.ins)


def _mm_call(name, grid, a, a_spec, b, b_spec, dims, out_shapes, out_specs, acc_shape, epilogue=None,
             extras=(), extra_specs=(), col_axis=1, comm=None):
    nk = grid[2]
    n_extra = len(extras)
    n_out = len(out_shapes)

    def finish(res, ex, outs, j):
        if epilogue is None:
            outs[0][...] = res.astype(outs[0].dtype)
        else:
            epilogue(res, ex, outs, j)

    def body(*refs):
        a_ref, b_ref = refs[0], refs[1]
        ex = refs[2:2 + n_extra]
        outs = refs[2 + n_extra:2 + n_extra + n_out]
        j, k = pl.program_id(col_axis), pl.program_id(2)
        part = _dot(a_ref[...].astype(BF16), b_ref[...].astype(BF16), dims)
        if nk == 1:
            finish(part, ex, outs, j)
            return
        acc = refs[-1]

        @pl.when(k == 0)
        def _():
            acc[...] = part

        @pl.when((k > 0) & (k < nk - 1))
        def _():
            acc[...] += part

        @pl.when(k == nk - 1)
        def _():
            finish(acc[...] + part, ex, outs, j)

    return _call(name, body, grid, [a_spec, b_spec, *extra_specs], out_specs, out_shapes,
                 [] if nk == 1 else [pltpu.VMEM(acc_shape, F32)], (a, b, *extras), ("parallel", "parallel", "arbitrary"), comm)


def _matmul(name, a, b, mode, out_dtype, tm, tn, tk, resid=None, col0_scale=None, comm=None):
    if mode == "nn":
        (M, K), N = a.shape, b.shape[1]
        a_spec = pl.BlockSpec((tm, tk), lambda j, i, k: (i, k))
        b_spec = pl.BlockSpec((tk, tn), lambda j, i, k: (k, j))
        dims = NN
    elif mode == "nt":
        (M, K), N = a.shape, b.shape[0]
        a_spec = pl.BlockSpec((tm, tk), lambda j, i, k: (i, k))
        b_spec = pl.BlockSpec((tn, tk), lambda j, i, k: (j, k))
        dims = NT
    else:
        (K, M), N = a.shape, b.shape[1]
        a_spec = pl.BlockSpec((tk, tm), lambda j, i, k: (k, i))
        b_spec = pl.BlockSpec((tk, tn), lambda j, i, k: (k, j))
        dims = TN
    assert M % tm == 0 and N % tn == 0 and K % tk == 0, (name, M, N, K, tm, tn, tk)
    o_spec = pl.BlockSpec((tm, tn), lambda j, i, k: (i, j))
    extras, extra_specs, epilogue = (), (), None
    if resid is not None:
        extras, extra_specs = (resid,), (o_spec,)

        def epilogue(acc, ex, outs, j):
            outs[0][...] = (acc + ex[0][...]).astype(outs[0].dtype)

    elif col0_scale is not None:

        def epilogue(acc, ex, outs, j):
            outs[0][...] = (acc * jnp.where(j == 0, col0_scale, 1.0)).astype(outs[0].dtype)

    res = _mm_call(name, (N // tn, M // tm, K // tk), a, a_spec, b, b_spec, dims, [jax.ShapeDtypeStruct((M, N), out_dtype)],
                   [o_spec], (tm, tn), epilogue, extras, extra_specs, col_axis=0, comm=comm)
    return res[0] if comm is None else (res[0], res[1:])


def _rms_fwd(name, h, gain, dils=(1,), comm=None, tm=512):
    S, D = h.shape

    def body(h_ref, g_ref, *rest):
        x = h_ref[...]
        rstd = lax.rsqrt(jnp.mean(x * x, axis=-1, keepdims=True) + RMS_EPS)
        y = x * rstd * g_ref[...]
        _write_views([y[:, b * LANES:(b + 1) * LANES] for b in range(N_PAIRS)], rest[-1], rest[:-1], dils, tm)

    res = _call(name, body, (S // tm,), [pl.BlockSpec((tm, D), lambda i: (i, 0)), pl.BlockSpec((1, D), lambda i: (0, 0))],
                [_view_spec(tm, R) for R in dils], [jax.ShapeDtypeStruct((S // R, R * D), BF16) for R in dils],
                [pltpu.VMEM((N_PAIRS, tm, LANES), F32)], (h, gain.reshape(1, D)), ("parallel",), comm)
    views = res[0] if len(dils) == 1 else res[:len(dils)]
    return views if comm is None else (views, res[len(dils):])


def _rms_bwd(name, dn, h, gain, dres, copy16=True, tm=512):
    S, D = h.shape

    def body(dn_ref, h_ref, g_ref, r_ref, dh_ref, dg_ref, *dh16_ref):
        x = h_ref[...]
        rstd = lax.rsqrt(jnp.mean(x * x, axis=-1, keepdims=True) + RMS_EPS)
        xhat = x * rstd
        d = dn_ref[...]
        dxhat = d * g_ref[...]
        dh = rstd * (dxhat - xhat * jnp.mean(dxhat * xhat, axis=-1, keepdims=True)) + r_ref[...]
        dh_ref[...] = dh
        if copy16:
            dh16_ref[0][...] = dh.astype(BF16)

        @pl.when(pl.program_id(0) == 0)
        def _():
            dg_ref[...] = jnp.zeros_like(dg_ref)

        dg_ref[...] += jnp.sum(d * xhat, axis=0, keepdims=True)

    row = pl.BlockSpec((tm, D), lambda i: (i, 0))
    vec = pl.BlockSpec((1, D), lambda i: (0, 0))
    return pl.pallas_call(
        body, grid=(S // tm,), in_specs=[row, row, vec, row], out_specs=[row, vec] + [row] * copy16,
        out_shape=[jax.ShapeDtypeStruct((S, D), F32), jax.ShapeDtypeStruct((1, D), F32)] + [jax.ShapeDtypeStruct((S, D), BF16)] * copy16,
        compiler_params=_params("arbitrary"), name=name)(dn, h, gain.reshape(1, D), dres)


def _loss_head(name, h, gain, target, tm=512):
    S, D = h.shape

    def body(h_ref, g_ref, t_ref, dh_ref, dg_ref, loss_ref, dh16_ref):
        x = h_ref[...]
        rstd = lax.rsqrt(jnp.mean(x * x, axis=-1, keepdims=True) + RMS_EPS)
        xhat = x * rstd
        err = xhat * g_ref[...] - t_ref[...]
        dy = err * (1.0 / D)
        dxhat = dy * g_ref[...]
        dh = rstd * (dxhat - xhat * jnp.mean(dxhat * xhat, axis=-1, keepdims=True))
        dh_ref[...] = dh
        dh16_ref[...] = dh.astype(BF16)

        @pl.when(pl.program_id(0) == 0)
        def _():
            dg_ref[...] = jnp.zeros_like(dg_ref)
            loss_ref[...] = jnp.zeros_like(loss_ref)

        dg_ref[...] += jnp.sum(dy * xhat, axis=0, keepdims=True)
        part = 0.5 * jnp.sum(jnp.mean(err * err, axis=-1, keepdims=True), axis=0, keepdims=True)
        loss_ref[...] += jnp.broadcast_to(part, loss_ref.shape)

    row = pl.BlockSpec((tm, D), lambda i: (i, 0))
    vec = pl.BlockSpec((1, D), lambda i: (0, 0))
    return pl.pallas_call(
        body, grid=(S // tm,), in_specs=[row, vec, row], out_specs=[row, vec, pl.BlockSpec((1, LANES), lambda i: (0, 0)), row],
        out_shape=[jax.ShapeDtypeStruct((S, D), F32), jax.ShapeDtypeStruct((1, D), F32),
                   jax.ShapeDtypeStruct((1, LANES), F32), jax.ShapeDtypeStruct((S, D), BF16)],
        compiler_params=_params("arbitrary"), name=name)(h, gain.reshape(1, D), target)


def _rope_tables(S):
    half = ROT_DIM // 2
    inv_freq = ROPE_THETA ** (-jnp.arange(half, dtype=F32) * 2.0 / ROT_DIM)
    ang = jnp.arange(S, dtype=F32)[:, None] * inv_freq[None, :]
    cos, sin = jnp.cos(ang), jnp.sin(ang)
    one = jnp.ones((S, HEAD_DIM - ROT_DIM), F32)
    zero = jnp.zeros((S, HEAD_DIM - ROT_DIM), F32)
    zh = jnp.zeros((S, half), F32)
    c = jnp.concatenate([cos, cos, one], axis=1)
    sa = jnp.concatenate([-sin, zh, zero], axis=1)
    sb = jnp.concatenate([zh, sin, zero], axis=1)
    return tuple(jnp.concatenate([t, t], axis=1) for t in (c, sa, sb))


def _rotate(x, c, sa, sb, sign):
    return x * c + sign * (pltpu.roll(x, LANES - ROT_DIM // 2, 1) * sa + pltpu.roll(x, ROT_DIM // 2, 1) * sb)


def _stage_chunks(scr, chunks):
    for c, x in enumerate(chunks):
        scr[c] = x


def _strided_rows(scr, c, r, n, R):
    return scr.at[c][pl.ds(r, n, stride=R), :]


def _a_proj(name, n, w, g, R, tabs, comm, tm=1024):
    S, D = n.shape
    n_i = S // tm
    n_out = 3

    def body(n_ref, w_ref, c_ref, sa_ref, sb_ref, *rest):
        outs, scr = rest[:n_out], rest[n_out]
        j = pl.program_id(0)
        acc = _dot(n_ref[...], w_ref[...], NN)
        c, sa, sb = c_ref[...], sa_ref[...], sb_ref[...]
        for J in range(n_out):
            kind = J

            @pl.when(j == J)
            def _(J=J, kind=kind):
                chunks = [acc[:, b * LANES:(b + 1) * LANES] for b in range(N_PAIRS)]
                if kind < 2:
                    chunks = [_rotate(x, c, sa, sb, 1.0) * (SOFTMAX_SCALE if kind == 0 else 1.0) for x in chunks]
                if R == 1:
                    for b, x in enumerate(chunks):
                        outs[J][:, b * LANES:(b + 1) * LANES] = x.astype(BF16)
                    return
                _stage_chunks(scr, chunks)
                for r in range(R):
                    for b in range(N_PAIRS):
                        col = r * D_MODEL + b * LANES
                        outs[J][:, col:col + LANES] = _strided_rows(scr, b, r, tm // R, R).astype(BF16)

    def out_spec(J):
        return pl.BlockSpec((tm // R, R * D_MODEL), lambda j, i: (jnp.where(j == J, i, jnp.where(j < J, 0, n_i - 1)), 0))

    tab = pl.BlockSpec((tm, LANES), lambda j, i: (i, 0))
    res = _call(name, body, (n_out, n_i),
                [pl.BlockSpec((tm, D), lambda j, i: (i, 0)), pl.BlockSpec((D, D_MODEL), lambda j, i: (0, 3 * g + j)), tab, tab, tab],
                [out_spec(J) for J in range(n_out)], [jax.ShapeDtypeStruct((S // R, R * D_MODEL), BF16)] * n_out,
                [pltpu.VMEM((N_PAIRS, tm, LANES), F32)], (n, w, *tabs), ("arbitrary", "arbitrary"), comm)
    return res[:n_out], res[n_out:]


def _unstride(src_chunk, R, tok, rows):
    for r in range(R):
        for b in range(N_PAIRS):
            tok.at[b][pl.ds(r, rows // R, stride=R), :] = src_chunk(r, b).astype(F32)


def _by_residue(ref, R):
    return ref[...] if R == 1 else jnp.concatenate([ref[:, r * D_MODEL:(r + 1) * D_MODEL] for r in range(R)], axis=0)


def _a_dw(name, n_view, pieces, R, comm, tk=1024):
    D = D_MODEL
    S = n_view.shape[0] * R
    n_k = S // tk
    n_p = len(pieces)

    def body(n_ref, *rest):
        p_refs, o_ref, acc = rest[:n_p], rest[n_p], rest[n_p + 1]
        j, k = pl.program_id(0), pl.program_id(1)
        for J in range(n_p):

            @pl.when(j == J)
            def _(J=J):
                part = _dot(_by_residue(n_ref, R), _by_residue(p_refs[J], R), TN)

                @pl.when(k == 0)
                def _():
                    acc[...] = part

                @pl.when((k > 0) & (k < n_k - 1))
                def _():
                    acc[...] += part

                @pl.when(k == n_k - 1)
                def _():
                    o_ref[...] = (acc[...] + part).astype(BF16)

    def piece_spec(J):
        return pl.BlockSpec((tk // R, R * D_MODEL), lambda j, k: (jnp.where(j == J, k, jnp.where(j < J, 0, n_k - 1)), 0))

    return _call(name, body, (n_p, n_k), [pl.BlockSpec((tk // R, R * D_MODEL), lambda j, k: (k, 0))] + [piece_spec(J) for J in range(n_p)],
                 [pl.BlockSpec((D, D_MODEL), lambda j, k: (0, j))], [jax.ShapeDtypeStruct((D, n_p * D_MODEL), BF16)],
                 [pltpu.VMEM((D, D_MODEL), F32)], (n_view, *pieces), ("arbitrary", "arbitrary"), comm)


def _a_dn(name, pieces, dils, w, comm, tm=512):
    D = w.shape[0]
    S = pieces[0].shape[0] * dils[0]
    n_p = len(pieces)

    def body(*refs):
        p_refs, w_ref, o_ref, acc, part_acc, tok = refs[:n_p], refs[n_p], refs[n_p + 1], refs[n_p + 2], refs[n_p + 3], refs[n_p + 4]
        j = pl.program_id(1)
        for J in range(n_p):

            @pl.when(j == J)
            def _(J=J):
                R, t = dils[J // 3], J % 3
                part = _dot(_by_residue(p_refs[J], R), w_ref[...], NT)
                if R == 1:
                    if J == 0:
                        acc[...] = part
                    else:
                        acc[...] += part
                    return
                if t == 0:
                    part_acc[...] = part
                    return
                if t == 1:
                    part_acc[...] += part
                    return
                n = tm // R
                _unstride(lambda r, b: part_acc[r * n:(r + 1) * n, b * LANES:(b + 1) * LANES]
                          + part[r * n:(r + 1) * n, b * LANES:(b + 1) * LANES], R, tok, tm)
                total = acc[...] + jnp.concatenate([tok[b] for b in range(N_PAIRS)], axis=1)
                if J == n_p - 1:
                    o_ref[...] = total
                else:
                    acc[...] = total

    specs = [pl.BlockSpec((tm // dils[J // 3], dils[J // 3] * D_MODEL), lambda i, j: (i, 0)) for J in range(n_p)]
    return _call(name, body, (S // tm, n_p), specs + [pl.BlockSpec((D, D_MODEL), lambda i, j: (0, j))],
                 [pl.BlockSpec((tm, D), lambda i, j: (i, 0))], [jax.ShapeDtypeStruct((S, D), F32)],
                 [pltpu.VMEM((tm, D), F32), pltpu.VMEM((tm, D), F32), pltpu.VMEM((N_PAIRS, tm, LANES), F32)], (*pieces, w),
                 ("arbitrary", "arbitrary"), comm)


def _lo_lanes():
    return lax.broadcasted_iota(jnp.int32, (1, LANES), 1) < HEAD_DIM


def _rep_rows(x2, lo):
    sw = pltpu.roll(x2, HEAD_DIM, 1)
    return jnp.where(lo, x2, sw), jnp.where(lo, sw, x2)


def _pair_cols(h):
    return slice((h // 2) * LANES, (h // 2 + 1) * LANES)


def _head_lanes(lo, h):
    return lo if h % 2 == 0 else jnp.logical_not(lo)


def _band_masks(t, first):
    ri = lax.broadcasted_iota(jnp.int32, (t, t), 0)
    ci = lax.broadcasted_iota(jnp.int32, (t, t), 1)
    neg_prev = jnp.where((ci >= ri) & jnp.logical_not(first), 0.0, NEG_INF)
    neg_cur = jnp.where(ci <= ri, 0.0, NEG_INF)
    return neg_prev, neg_cur


def _dil_specs(L, R, t, qcol, kcol, vcol):
    W = D_MODEL
    prev = lambda qi: jnp.maximum(qi - 1, 0)
    return dict(
        q=pl.BlockSpec((t, W), lambda r, qi: (qi, qcol(r))),
        kp=pl.BlockSpec((t, W), lambda r, qi: (prev(qi), kcol(r))), kc=pl.BlockSpec((t, W), lambda r, qi: (qi, kcol(r))),
        vp=pl.BlockSpec((t, W), lambda r, qi: (prev(qi), vcol(r))), vc=pl.BlockSpec((t, W), lambda r, qi: (qi, vcol(r))),
        own=pl.BlockSpec((t, W), lambda r, qi: (qi, r)), tab=pl.BlockSpec((t, LANES), lambda r, qi: (qi, r)))


def _dil_fwd(name, x, qcol, kcol, vcol, R, L, comm=None):
    t = BAND_STEPS
    W = D_MODEL
    sp = _dil_specs(L, R, t, qcol, kcol, vcol)

    def body(q_ref, kp_ref, kc_ref, vp_ref, vc_ref, o_ref, lse_ref):
        lo = _lo_lanes()
        neg_p, neg_c = _band_masks(t, pl.program_id(1) == 0)
        s_p, s_c = [], []
        for h in range(N_HEADS):
            cols = _pair_cols(h)
            qh = jnp.where(_head_lanes(lo, h), q_ref[:, cols], 0)
            s_p.append(_dot(qh, kp_ref[:, cols], NT))
            s_c.append(_dot(qh, kc_ref[:, cols], NT))
        s_p = jnp.stack(s_p) + neg_p[None]
        s_c = jnp.stack(s_c) + neg_c[None]
        m = jnp.maximum(jnp.max(s_p, axis=2, keepdims=True), jnp.max(s_c, axis=2, keepdims=True))
        p_p, p_c = jnp.exp(s_p - m), jnp.exp(s_c - m)
        l = jnp.sum(p_p, axis=2, keepdims=True) + jnp.sum(p_c, axis=2, keepdims=True)
        inv, lse = 1.0 / l, m + jnp.log(l)
        p_p, p_c = p_p.astype(BF16), p_c.astype(BF16)
        for p in range(N_PAIRS):
            cols = _pair_cols(2 * p)
            o2 = jnp.zeros((t, LANES), F32)
            for h in (2 * p, 2 * p + 1):
                hm = _head_lanes(lo, h)
                pv = _dot(p_p[h], jnp.where(hm, vp_ref[:, cols], 0), NN) + _dot(p_c[h], jnp.where(hm, vc_ref[:, cols], 0), NN)
                o2 = o2 + pv * inv[h]
            o_ref[:, cols] = o2
            lse_ref[:, cols] = jnp.where(lo, lse[2 * p], lse[2 * p + 1])

    return _call(name, body, (R, L // t), [sp["q"], sp["kp"], sp["kc"], sp["vp"], sp["vc"]], [sp["own"], sp["own"]],
                 [jax.ShapeDtypeStruct((L, R * W), F32), jax.ShapeDtypeStruct((L, R * W), F32)], [],
                 (x[0], x[1], x[1], x[2], x[2]), ("parallel", "parallel"), comm)


def _dil_scores(lo, q_ref, do_ref, o_ref, lse_ref, kv_refs):
    s = [[] for _ in kv_refs]
    dp = [[] for _ in kv_refs]
    lse, d = [], []
    for h in range(N_HEADS):
        cols = _pair_cols(h)
        hm = _head_lanes(lo, h)
        qh, doh = jnp.where(hm, q_ref[:, cols], 0), jnp.where(hm, do_ref[:, cols], 0)
        for i, (k_ref, v_ref) in enumerate(kv_refs):
            s[i].append(_dot(qh, k_ref[:, cols], NT))
            dp[i].append(_dot(doh, v_ref[:, cols], NT))
        lse.append(_rep_rows(lse_ref[:, cols], lo)[h % 2])
        dd = do_ref[:, cols].astype(F32) * o_ref[:, cols].astype(F32)
        d.append(jnp.sum(jnp.where(hm, dd, 0.0), axis=1, keepdims=True))
    return (*[jnp.stack(x) for x in s], *[jnp.stack(x) for x in dp], jnp.stack(lse), jnp.stack(d))


def _dil_bwd(name, x, do, o, lse, tabs, R, L, comm=None):
    t = BAND_STEPS
    W = D_MODEL
    nq = L // t
    qb = lambda step: nq - 1 - step
    kb = lambda step: jnp.maximum(qb(step) - 1, 0)
    at = lambda f, width: pl.BlockSpec((t, width), lambda r, step: (f(step), r))

    def body(q_ref, kp_ref, kc_ref, vp_ref, vc_ref, do_ref, o_ref, lse_ref, c_ref, sa_ref, sb_ref, dq_ref, dk_ref, dv_ref,
             dk_scr, dv_scr):
        qi = nq - 1 - pl.program_id(1)
        lo = _lo_lanes()
        unrotate = lambda x: _rotate(x, c_ref[...], sa_ref[...], sb_ref[...], -1.0).astype(BF16)

        @pl.when(qi == nq - 1)
        def _():
            dk_scr[...] = jnp.zeros_like(dk_scr)
            dv_scr[...] = jnp.zeros_like(dv_scr)

        neg_p, neg_c = _band_masks(t, qi == 0)
        s_p, s_c, dp_p, dp_c, lse_h, d = _dil_scores(lo, q_ref, do_ref, o_ref, lse_ref, ((kp_ref, vp_ref), (kc_ref, vc_ref)))
        p_p, p_c = jnp.exp(s_p + neg_p[None] - lse_h), jnp.exp(s_c + neg_c[None] - lse_h)
        ds_p, ds_c = (p_p * (dp_p - d)).astype(BF16), (p_c * (dp_c - d)).astype(BF16)
        p_p, p_c = p_p.astype(BF16), p_c.astype(BF16)
        for p in range(N_PAIRS):
            cols = _pair_cols(2 * p)
            dq2 = jnp.zeros((t, LANES), F32)
            dk_cur, dv_cur = dk_scr[:, cols], dv_scr[:, cols]
            dk_prev, dv_prev = jnp.zeros((t, LANES), F32), jnp.zeros((t, LANES), F32)
            for h in (2 * p, 2 * p + 1):
                hm = _head_lanes(lo, h)
                qh, doh = jnp.where(hm, q_ref[:, cols], 0), jnp.where(hm, do_ref[:, cols], 0)
                dq2 = dq2 + _dot(ds_p[h], jnp.where(hm, kp_ref[:, cols], 0), NN) + _dot(ds_c[h], jnp.where(hm, kc_ref[:, cols], 0), NN)
                dk_prev, dv_prev = dk_prev + _dot(ds_p[h], qh, TN), dv_prev + _dot(p_p[h], doh, TN)
                dk_cur, dv_cur = dk_cur + _dot(ds_c[h], qh, TN), dv_cur + _dot(p_c[h], doh, TN)
            dq_ref[:, cols] = unrotate(dq2 * SOFTMAX_SCALE)
            dk_ref[:, cols] = unrotate(dk_cur)
            dv_ref[:, cols] = dv_cur.astype(BF16)
            dk_scr[:, cols] = dk_prev
            dv_scr[:, cols] = dv_prev

    wide = jax.ShapeDtypeStruct((L, R * W), BF16)
    return _call(name, body, (R, nq),
                 [at(qb, W), at(kb, W), at(qb, W), at(kb, W), at(qb, W), at(qb, W), at(qb, W), at(qb, W),
                  at(qb, LANES), at(qb, LANES), at(qb, LANES)],
                 [at(qb, W), at(qb, W), at(qb, W)], [wide, wide, wide], [pltpu.VMEM((t, W), F32), pltpu.VMEM((t, W), F32)],
                 (x[0], x[1], x[1], x[2], x[2], do, o, lse, *tabs), ("parallel", "arbitrary"), comm)


def _fox_operands(q2, k2, kb2, lo, hh):
    lane = lax.broadcasted_iota(jnp.int32, (1, LANES), 1)
    if hh == 0:
        ones = ((lane >= HEAD_DIM) & (lane < HEAD_DIM + 3)).astype(BF16)
        return jnp.where(lo, q2, ones), jnp.where(lo, k2, kb2)
    ones = (lane < 3).astype(BF16)
    return jnp.where(lo, ones, q2), jnp.where(lo, kb2, k2)


def _causal_neg(t):
    ri = lax.broadcasted_iota(jnp.int32, (t, t), 0)
    ci = lax.broadcasted_iota(jnp.int32, (t, t), 1)
    return jnp.where(ci <= ri, 0.0, NEG_INF)


def _fox_fwd(name, qkv, kbias, t, comm=None):
    S = qkv.shape[0]
    W = D_MODEL
    nq = S // t
    rep = t // LANES

    def body(q_ref, k_ref, v_ref, kb_ref, o_ref, lse_ref, m_scr, l_scr, acc_scr):
        qi, j = pl.program_id(0), pl.program_id(1)
        lo = _lo_lanes()

        @pl.when(j == 0)
        def _():
            m_scr[...] = jnp.full_like(m_scr, NEG_INF)
            l_scr[...] = jnp.zeros_like(l_scr)
            acc_scr[...] = jnp.zeros_like(acc_scr)

        def step(masked):
            neg = _causal_neg(t) if masked else None

            def pair(p, carry):
                cs = pl.ds(pl.multiple_of(p * LANES, LANES), LANES)
                q2, k2, v2, kb2 = q_ref[:, cs], k_ref[:, cs], v_ref[:, cs], kb_ref[:, cs]
                pvs, alphas = [], []
                for hh in range(2):
                    hm = lo if hh == 0 else jnp.logical_not(lo)
                    qh, kh = _fox_operands(q2, k2, kb2, lo, hh)
                    s = _dot(qh, kh, NT)
                    if masked:
                        s = s + neg
                    h = 2 * p + hh
                    m_prev = m_scr[h]
                    m_new = jnp.maximum(m_prev, jnp.max(s, axis=1, keepdims=True))
                    pe = jnp.exp(s - jnp.tile(m_new, (1, rep)))
                    alpha = jnp.exp(m_prev - m_new)
                    l_scr[h] = alpha * l_scr[h] + jnp.sum(pe, axis=1, keepdims=True)
                    m_scr[h] = m_new
                    pvs.append(_dot(pe.astype(BF16), jnp.where(hm, v2, 0), NN))
                    alphas.append(alpha)
                acc_scr[:, cs] = acc_scr[:, cs] * jnp.where(lo, alphas[0], alphas[1]) + pvs[0] + pvs[1]
                return carry

            lax.fori_loop(0, N_PAIRS, pair, 0, unroll=4)

        @pl.when(j < qi)
        def _():
            step(False)

        @pl.when(j == qi)
        def _():
            step(True)

        @pl.when(j == nq - 1)
        def _():
            for p in range(N_PAIRS):
                cols = slice(p * LANES, (p + 1) * LANES)
                l2 = jnp.where(lo, l_scr[2 * p], l_scr[2 * p + 1])
                m2 = jnp.where(lo, m_scr[2 * p], m_scr[2 * p + 1])
                o_ref[:, cols] = (acc_scr[:, cols] / l2).astype(BF16)
                lse_ref[:, cols] = m2 + jnp.log(l2)

    kv = lambda col: pl.BlockSpec((t, W), lambda qi, j: (jnp.minimum(j, qi), col))
    own = pl.BlockSpec((t, W), lambda qi, j: (qi, 0))
    return _call(name, body, (nq, nq), [own, kv(1), kv(2), kv(0)], [own, own],
                 [jax.ShapeDtypeStruct((S, W), BF16), jax.ShapeDtypeStruct((S, W), F32)],
                 [pltpu.VMEM((N_HEADS, t, LANES), F32), pltpu.VMEM((N_HEADS, t, LANES), F32), pltpu.VMEM((t, W), F32)],
                 (qkv, qkv, qkv, kbias), ("parallel", "arbitrary"), comm)


def _fox_head_grads(qh, kh, v2, doh, neg, lse_h, d_h, rep):
    s = _dot(qh, kh, NT)
    if neg is not None:
        s = s + neg
    p = jnp.exp(s - jnp.tile(lse_h, (1, rep)))
    return p, p * (_dot(doh, v2, NT) - d_h)


def _fox_bwd(name, qkv, kbias, do, o, lse, t, comm=None):
    S = qkv.shape[0]
    W = D_MODEL
    nq = S // t
    rep = t // LANES

    def body(q_ref, k_ref, v_ref, kb_ref, do_ref, o_ref, lse_ref, dq_ref, dk_ref, dv_ref, rs_ref, dc_ref, dq_scr, dk_scr, dv_scr):
        kb, j = pl.program_id(0), pl.program_id(1)
        lo = _lo_lanes()
        lane = lax.broadcasted_iota(jnp.int32, (1, LANES), 1)
        rows = pl.ds(pl.multiple_of(j * t, t), t)

        @pl.when((kb == 0) & (j == 0))
        def _():
            dq_scr[...] = jnp.zeros_like(dq_scr)
            rs_ref[...] = jnp.zeros_like(rs_ref)

        @pl.when(j == 0)
        def _():
            dk_scr[...] = jnp.zeros_like(dk_scr)
            dv_scr[...] = jnp.zeros_like(dv_scr)
            dc_ref[...] = jnp.zeros_like(dc_ref)

        def step(masked):
            neg = _causal_neg(t) if masked else None

            def pair(p, carry):
                cs = pl.ds(pl.multiple_of(p * LANES, LANES), LANES)
                q2, k2, v2, kb2, do2 = q_ref[:, cs], k_ref[:, cs], v_ref[:, cs], kb_ref[:, cs], do_ref[:, cs]
                dd = do2.astype(F32) * o_ref[:, cs].astype(F32)
                lse_h = _rep_rows(lse_ref[:, cs], lo)
                dq2 = jnp.zeros((t, LANES), F32)
                dv2 = jnp.zeros((t, LANES), F32)
                dk2 = jnp.zeros((t, LANES), F32)
                for hh in range(2):
                    hm = lo if hh == 0 else jnp.logical_not(lo)
                    qh, kh = _fox_operands(q2, k2, kb2, lo, hh)
                    doh = jnp.where(hm, do2, 0)
                    d_h = jnp.sum(jnp.where(hm, dd, 0.0), axis=1, keepdims=True)
                    pr, ds = _fox_head_grads(qh, kh, v2, doh, neg, lse_h[hh], d_h, rep)
                    rs_ref[rows, :] += jnp.where(lane == 2 * p + hh, jnp.sum(ds, axis=1, keepdims=True), 0.0)
                    dc_ref[p, hh:hh + 1, :] += jnp.sum(ds, axis=0, keepdims=True)
                    dsb = ds.astype(BF16)
                    dv2 = dv2 + _dot(pr.astype(BF16), doh, TN)
                    dk2 = dk2 + _dot(dsb, jnp.where(hm, q2, 0), TN)
                    dq2 = dq2 + _dot(dsb, jnp.where(hm, k2, 0), NN)
                dv_scr[:, cs] += dv2
                dk_scr[:, cs] += dk2
                dq_scr[rows, cs] += dq2
                return carry

            lax.fori_loop(0, N_PAIRS, pair, 0, unroll=4)
            if masked:
                dq_ref[...] = (dq_scr[rows, :] * SOFTMAX_SCALE).astype(BF16)

        @pl.when(j > kb)
        def _():
            step(False)

        @pl.when(j == kb)
        def _():
            step(True)

        @pl.when(j == nq - 1)
        def _():
            dv_ref[...] = dv_scr[...].astype(BF16)
            dk_ref[...] = dk_scr[...].astype(BF16)

    qrow = pl.BlockSpec((t, W), lambda kb, j: (jnp.maximum(j, kb), 0))
    krow = lambda col: pl.BlockSpec((t, W), lambda kb, j: (kb, col))
    own = pl.BlockSpec((t, W), lambda kb, j: (kb, 0))
    wide = jax.ShapeDtypeStruct((S, W), BF16)
    return _call(name, body, (nq, nq), [qrow, krow(1), krow(2), krow(0), qrow, qrow, qrow],
                 [own, own, own, pl.BlockSpec((S, LANES), lambda kb, j: (0, 0)), pl.BlockSpec((N_PAIRS, 2, t), lambda kb, j: (0, 0, kb))],
                 [wide, wide, wide, jax.ShapeDtypeStruct((S, LANES), F32), jax.ShapeDtypeStruct((N_PAIRS, 2, S), F32)],
                 [pltpu.VMEM((S, W), F32), pltpu.VMEM((t, W), F32), pltpu.VMEM((t, W), F32)],
                 (qkv, qkv, qkv, kbias, do, o, lse), ("arbitrary", "arbitrary"), comm, vmem=FOX_BWD_VMEM)


def _view_spec(tm, R, index=lambda i: (i, 0)):
    return pl.BlockSpec((tm // R, R * D_MODEL), index)


def _matmul_nt_views(name, a, w, dils, tm=512):
    S, K = a.shape

    def body(a_ref, w_ref, *rest):
        res = _dot(a_ref[...].astype(BF16), w_ref[...], NT)
        _write_views([res[:, b * LANES:(b + 1) * LANES] for b in range(N_PAIRS)], rest[-1], rest[:-1], dils, tm)

    return pl.pallas_call(
        body, grid=(S // tm,), in_specs=[pl.BlockSpec((tm, K), lambda i: (i, 0)), pl.BlockSpec((D_MODEL, K), lambda i: (0, 0))],
        out_specs=[_view_spec(tm, R) for R in dils],
        out_shape=[jax.ShapeDtypeStruct((S // R, R * D_MODEL), BF16) for R in dils],
        scratch_shapes=[pltpu.VMEM((N_PAIRS, tm, LANES), F32)], compiler_params=_params("parallel"), name=name)(a, w)


def _write_views(chunks, scr, out_refs, dils, tm):
    if any(R > 1 for R in dils):
        _stage_chunks(scr, chunks)
    for ref, R in zip(out_refs, dils):
        for b, x in enumerate(chunks):
            if R == 1:
                ref[:, b * LANES:(b + 1) * LANES] = x.astype(ref.dtype)
                continue
            for r in range(R):
                col = r * D_MODEL + b * LANES
                ref[:, col:col + LANES] = _strided_rows(scr, b, r, tm // R, R).astype(ref.dtype)


def _combine(name, os_, lses, dils, tm=256):
    S = os_[0].shape[0] * dils[0]
    G = len(dils)

    def body(*refs):
        o_refs, l_refs = refs[:G], refs[G:2 * G]
        o_outs, l_outs = refs[2 * G:3 * G], refs[3 * G:4 * G]
        stage = refs[4 * G:]
        for g, R in enumerate(dils):
            if R == 1:
                continue
            for src, dst in ((o_refs[g], stage[2 * g]), (l_refs[g], stage[2 * g + 1])):
                _unstride(lambda r, b, src=src: src[:, r * D_MODEL + b * LANES:r * D_MODEL + (b + 1) * LANES], R, dst, tm)
        o_chunks, l_chunks = [], []
        for b in range(N_PAIRS):
            cols = slice(b * LANES, (b + 1) * LANES)
            os_b = [o_refs[g][:, cols] if R == 1 else stage[2 * g][b] for g, R in enumerate(dils)]
            ls = [l_refs[g][:, cols] if R == 1 else stage[2 * g + 1][b] for g, R in enumerate(dils)]
            m = functools.reduce(jnp.maximum, ls)
            ws = [jnp.exp(l - m) for l in ls]
            den = functools.reduce(jnp.add, ws)
            o_chunks.append(functools.reduce(jnp.add, [w * o for w, o in zip(ws, os_b)]) / den)
            l_chunks.append(m + jnp.log(den))
        _write_views(o_chunks, stage[0], o_outs, dils, tm)
        _write_views(l_chunks, stage[1], l_outs, dils, tm)

    specs = [_view_spec(tm, R) for R in dils]
    shapes = lambda dt: [jax.ShapeDtypeStruct((S // R, R * D_MODEL), dt) for R in dils]
    res = pl.pallas_call(
        body, grid=(S // tm,), in_specs=specs * 2, out_specs=specs * 2, out_shape=shapes(BF16) + shapes(F32),
        scratch_shapes=[pltpu.VMEM((N_PAIRS, tm, LANES), F32)] * (2 * G), compiler_params=_params("parallel"),
        name=name)(*os_, *lses)
    return res[:G], res[G:]


def _tri_matmul(tri, x):
    hi, mid, lo = _split3(x)
    return _dot(tri, hi, NN) + _dot(tri, mid, NN) + _dot(tri, lo, NN)


def _split3(x):
    hi = x.astype(BF16)
    r1 = x - hi.astype(F32)
    mid = r1.astype(BF16)
    return hi, mid, (r1 - mid.astype(F32)).astype(BF16)


def _gate_fwd(name, z, bf, tb=512):
    S = z.shape[0]

    def body(z_ref, b_ref, kb_ref, carry):
        @pl.when(pl.program_id(0) == 0)
        def _():
            carry[...] = jnp.zeros_like(carry)

        lf = jax.nn.log_sigmoid(z_ref[...] + b_ref[...])
        ri = lax.broadcasted_iota(jnp.int32, (tb, tb), 0)
        ci = lax.broadcasted_iota(jnp.int32, (tb, tb), 1)
        tri = (ci <= ri).astype(BF16)
        c = _tri_matmul(tri, lf) + carry[...]
        carry[...] = c[tb - 1:tb, :]
        head = lax.broadcasted_iota(jnp.int32, (LANES, D_MODEL), 0)
        col = lax.broadcasted_iota(jnp.int32, (LANES, D_MODEL), 1)
        base = (head >> 1) * LANES + jnp.where((head & 1) == 0, HEAD_DIM, 0)
        kb = jnp.zeros((tb, D_MODEL), F32)
        for i, piece in enumerate(_split3(-c)):
            place = ((col == base + i) & (head < N_HEADS)).astype(BF16)
            kb = kb + _dot(piece, place, NN)
        kb_ref[...] = kb.astype(BF16)

    row = pl.BlockSpec((tb, LANES), lambda i: (i, 0))
    return pl.pallas_call(
        body, grid=(S // tb,), in_specs=[row, pl.BlockSpec((1, LANES), lambda i: (0, 0))],
        out_specs=pl.BlockSpec((tb, D_MODEL), lambda i: (i, 0)), out_shape=jax.ShapeDtypeStruct((S, D_MODEL), BF16),
        scratch_shapes=[pltpu.VMEM((1, LANES), F32)], compiler_params=_params("arbitrary"), name=name)(z, bf)


def _gate_bwd(name, dc, z, bf, tb=512):
    S = z.shape[0]
    nb = S // tb

    def body(dc_ref, z_ref, b_ref, dz_ref, db_ref, carry):
        @pl.when(pl.program_id(0) == 0)
        def _():
            carry[...] = jnp.zeros_like(carry)
            db_ref[...] = jnp.zeros_like(db_ref)

        ri = lax.broadcasted_iota(jnp.int32, (tb, tb), 0)
        ci = lax.broadcasted_iota(jnp.int32, (tb, tb), 1)
        tri = (ci >= ri).astype(BF16)
        dlf = _tri_matmul(tri, dc_ref[...]) + carry[...]
        carry[...] = dlf[0:1, :]
        dz = dlf * jax.nn.sigmoid(-(z_ref[...] + b_ref[...]))
        dz_ref[...] = dz
        db_ref[...] += jnp.sum(dz, axis=0, keepdims=True)

    row = pl.BlockSpec((tb, LANES), lambda i: (nb - 1 - i, 0))
    vec = pl.BlockSpec((1, LANES), lambda i: (0, 0))
    return pl.pallas_call(
        body, grid=(nb,), in_specs=[row, row, vec], out_specs=[row, vec],
        out_shape=[jax.ShapeDtypeStruct((S, LANES), F32), jax.ShapeDtypeStruct((1, LANES), F32)],
        scratch_shapes=[pltpu.VMEM((1, LANES), F32)], compiler_params=_params("arbitrary"), name=name)(dc, z, bf)


def _ffn_gu(name, n, wgu, comm=None, tm=1024):
    S, D = n.shape
    nb = N_DEV // 2

    def body(n_ref, wg_ref, wu_ref, gu_ref, act_ref):
        x = n_ref[...]
        g = _dot(x, wg_ref[...], NN)
        u = _dot(x, wu_ref[...], NN)
        gu_ref[0] = g.astype(BF16)
        gu_ref[1] = u.astype(BF16)
        act_ref[...] = (g * jax.nn.sigmoid(g) * u).astype(BF16)

    return _call(
        name, body, (nb, S // tm),
        [pl.BlockSpec((tm, D), lambda j, i: (i, 0)), pl.BlockSpec((None, D, FF_BLK), lambda j, i: (j, 0, 0)),
         pl.BlockSpec((None, D, FF_BLK), lambda j, i: (j + nb, 0, 0))],
        [pl.BlockSpec((2, None, tm, FF_BLK), lambda j, i: (0, j, i, 0)), pl.BlockSpec((None, tm, FF_BLK), lambda j, i: (j, i, 0))],
        [jax.ShapeDtypeStruct((2, nb, S, FF_BLK), BF16), jax.ShapeDtypeStruct((nb, S, FF_BLK), BF16)], [],
        (n, wgu, wgu), ("parallel", "parallel"), comm)


def _ffn_down(name, act, wd, resid, comm=None, tm=1024):
    nb, S, _ = act.shape
    D = wd.shape[1]

    def epilogue(acc, ex, outs, j):
        outs[0][...] = acc + ex[0][...]

    o_spec = pl.BlockSpec((tm, D), lambda i, j, k: (i, 0))
    return _mm_call(name, (S // tm, 1, nb), act, pl.BlockSpec((None, tm, FF_BLK), lambda i, j, k: (k, i, 0)),
                    wd, pl.BlockSpec((FF_BLK, D), lambda i, j, k: (k, 0)), NN,
                    [jax.ShapeDtypeStruct((S, D), F32)], [o_spec], (tm, D), epilogue, (resid,), (o_spec,), comm=comm)


def _ffn_dact(name, dh, wd, gu, comm=None, tm=512):
    S, D = dh.shape
    nb = N_DEV // 2

    def epilogue(acc, ex, outs, j):
        g = ex[0][0].astype(F32)
        u = ex[0][1].astype(F32)
        sig = jax.nn.sigmoid(g)
        outs[0][0] = (acc * u * (sig * (1.0 + g * (1.0 - sig)))).astype(BF16)
        outs[0][1] = (acc * (g * sig)).astype(BF16)

    gu_spec = pl.BlockSpec((2, None, tm, FF_BLK), lambda j, i, k: (0, j, i, 0))
    return _mm_call(name, (nb, S // tm, 1), dh, pl.BlockSpec((tm, D), lambda j, i, k: (i, 0)),
                    wd, pl.BlockSpec((FF_BLK, D), lambda j, i, k: (j, 0)), NT,
                    [jax.ShapeDtypeStruct((2, nb, S, FF_BLK), BF16)], [gu_spec], (tm, FF_BLK), epilogue, (gu,), (gu_spec,),
                    col_axis=0, comm=comm)


def _ffn_dwgu(name, n, dgu, comm=None, tm=1024, tk=1024):
    S, D = n.shape
    dgu8 = dgu.reshape(N_DEV, S, FF_BLK)
    return _mm_call(name, (N_DEV, D // tm, S // tk), n, pl.BlockSpec((tk, tm), lambda d, i, k: (k, i)),
                    dgu8, pl.BlockSpec((None, tk, FF_BLK), lambda d, i, k: (d, k, 0)), TN,
                    [jax.ShapeDtypeStruct((N_DEV, D, FF_BLK), BF16)],
                    [pl.BlockSpec((None, tm, FF_BLK), lambda d, i, k: (d, i, 0))], (tm, FF_BLK), comm=comm)


def _ffn_dwd(name, act, dh, tk=1024):
    nb, S, _ = act.shape
    D = dh.shape[1]
    out = _mm_call(name, (nb, 1, S // tk), act, pl.BlockSpec((None, tk, FF_BLK), lambda b, j, k: (b, k, 0)),
                   dh, pl.BlockSpec((tk, D), lambda b, j, k: (k, 0)), TN,
                   [jax.ShapeDtypeStruct((nb, FF_BLK, D), BF16)],
                   [pl.BlockSpec((None, FF_BLK, D), lambda b, j, k: (b, 0, 0))], (FF_BLK, D))[0]
    return out.reshape(N_DEV, FF_BLK // 2, D)


def _ffn_dn(name, dgu, wgu, comm=None, tm=1024):
    S = dgu.shape[2]
    D = wgu.shape[1]
    dgu8 = dgu.reshape(N_DEV, S, FF_BLK)
    return _mm_call(name, (S // tm, 1, N_DEV), dgu8, pl.BlockSpec((None, tm, FF_BLK), lambda i, j, k: (k, i, 0)),
                    wgu, pl.BlockSpec((None, D, FF_BLK), lambda i, j, k: (k, 0, 0)), NT,
                    [jax.ShapeDtypeStruct((S, D), F32)], [pl.BlockSpec((tm, D), lambda i, j, k: (i, 0))], (tm, D), comm=comm)


def _adamw(name, parts, w, m, v, tr):
    stacked = w.ndim == 3
    by_layer = parts if stacked else [parts]
    n_l, n_parts = len(by_layer), len(by_layer[0])
    rows, cols = w.shape[-2:]
    n_i = rows // tr
    c1 = 1.0 - ADAM_B1 ** ADAM_STEP
    c2 = 1.0 - ADAM_B2 ** ADAM_STEP

    def body(*refs):
        w_ref, m_ref, v_ref, g_ref, d_ref, nm_ref, nv_ref = refs[n_l * n_parts:]
        layer = pl.program_id(0)
        for L in range(n_l):

            @pl.when(layer == L)
            def _(L=L):
                p_refs = refs[L * n_parts:(L + 1) * n_parts]
                g = p_refs[0][...].astype(F32)
                for r in p_refs[1:]:
                    g = g + r[...].astype(F32)
                mm = ADAM_B1 * m_ref[...] + (1.0 - ADAM_B1) * g
                vv = ADAM_B2 * v_ref[...] + (1.0 - ADAM_B2) * (g * g)
                g_ref[...] = g
                nm_ref[...] = mm
                nv_ref[...] = vv
                d_ref[...] = -ADAM_LR * ((mm / c1) / (jnp.sqrt(vv / c2) + ADAM_EPS) + ADAM_WD * w_ref[...])

    def part_spec(L):
        return pl.BlockSpec((tr, cols), lambda l, i: (jnp.where(l == L, i, jnp.where(l < L, 0, n_i - 1)), 0))

    blk = pl.BlockSpec((None, tr, cols), lambda l, i: (l, i, 0)) if stacked else pl.BlockSpec((tr, cols), lambda l, i: (i, 0))
    out = jax.ShapeDtypeStruct(w.shape, F32)
    return pl.pallas_call(
        body, grid=(n_l, n_i), in_specs=[part_spec(L) for L in range(n_l) for _ in range(n_parts)] + [blk] * 3,
        out_specs=[blk] * 4, out_shape=[out] * 4, compiler_params=_params("arbitrary", "parallel"),
        name=name)(*[p for ps in by_layer for p in ps], w, m, v)


def _position():
    return lax.axis_index("x"), lax.axis_index("y"), lax.axis_index("c")


def _other_chips():
    x, y, _ = _position()
    return [(1 - x, y), (x, 1 - y), (1 - x, 1 - y)]


def _remote(src, dst, send, recv, k, to):
    return pltpu.make_async_remote_copy(src_ref=src, dst_ref=dst, send_sem=send.at[k], recv_sem=recv.at[k],
                                        device_id=to, device_id_type=MESH)


def _ag_send(blocks, direct=False):
    n_peer = 7 if direct else 4

    def copies(ins, outs, send, recv, local, r0=0, l0=0):
        x, y, c = _position()
        me = 4 * x + 2 * y + c
        peers = [(x, y, 1 - c)] + [(px, py, c) for px, py in _other_chips()]
        if direct:
            peers += [(px, py, 1 - c) for px, py in _other_chips()]
        cps = []
        for t, (src, dst) in enumerate(zip(ins, outs)):
            cps.append(pltpu.make_async_copy(src, dst.at[me], local.at[l0 + t]))
            cps += [_remote(src, dst.at[me], send, recv, r0 + n_peer * t + k, to) for k, to in enumerate(peers)]
        return cps

    outs = tuple(jax.ShapeDtypeStruct((N_DEV,) + b.shape, b.dtype) for b in blocks)
    return _Comm(tuple(blocks), outs, {}, copies, n_peer * len(blocks), len(blocks))


def _ag_forward(bufs):
    def copies(ins, outs, send, recv, local, r0=0, l0=0):
        x, y, c = _position()
        cps = []
        for t, buf in enumerate(outs):
            for k, (px, py) in enumerate(_other_chips()):
                slot = buf.at[4 * px + 2 * py + c]
                cps.append(_remote(slot, slot, send, recv, r0 + 3 * t + k, (x, y, 1 - c)))
        return cps

    outs = tuple(jax.ShapeDtypeStruct(b.shape, b.dtype) for b in bufs)
    return _Comm(tuple(bufs), outs, {t: t for t in range(len(bufs))}, copies, 3 * len(bufs), 0)


def _rs_swap(shares):
    def copies(ins, outs, send, recv, local, r0=0, l0=0):
        x, y, c = _position()
        return [_remote(src.at[:, 1 - c], dst, send, recv, r0 + t, (x, y, 1 - c)) for t, (src, dst) in enumerate(zip(ins, outs))]

    ins = tuple(s.reshape((4, 2) + s.shape[1:]) for s in shares)
    outs = tuple(jax.ShapeDtypeStruct((4,) + s.shape[1:], s.dtype) for s in shares)
    return _Comm(ins, outs, {}, copies, len(shares), 0)


def _rs_exchange(sums):
    def copies(ins, outs, send, recv, local, r0=0, l0=0):
        _, _, c = _position()
        return [_remote(src.at[2 * px + py], dst.at[k], send, recv, r0 + 3 * t + k, (px, py, c))
                for t, (src, dst) in enumerate(zip(ins, outs)) for k, (px, py) in enumerate(_other_chips())]

    outs = tuple(jax.ShapeDtypeStruct((3,) + s.shape[1:], s.dtype) for s in sums)
    return _Comm(tuple(sums), outs, {}, copies, 3 * len(sums), 0)


def _comm_call(name, comm):
    return _call(name, lambda: None, (), [], [], [], [], (), (), comm)


def _pair_sum(name, share, got, core, tr):
    _, rows, cols = share.shape

    def body(c_ref, a_ref, b_ref, o_ref):
        o_ref[...] = (a_ref[...].astype(F32) + b_ref[...].astype(F32)).astype(o_ref.dtype)

    grid_spec = pltpu.PrefetchScalarGridSpec(
        num_scalar_prefetch=1, grid=(4, rows // tr),
        in_specs=[pl.BlockSpec((None, None, tr, cols), lambda q, i, c: (q, c[0], i, 0)),
                  pl.BlockSpec((None, tr, cols), lambda q, i, c: (q, i, 0))],
        out_specs=pl.BlockSpec((None, tr, cols), lambda q, i, c: (q, i, 0)))
    return pl.pallas_call(
        body, grid_spec=grid_spec, out_shape=jax.ShapeDtypeStruct((4, rows, cols), share.dtype),
        compiler_params=_params("parallel", "parallel"), name=name)(core, share.reshape(4, 2, rows, cols), got)


TENSORS = ("a_w_in", "a_w_out", "b_w_in", "b_w_out", "gu0", "gu1", "dn0", "dn1")
ROW_TILE = {"a_w_in": 256, "a_w_out": 128, "b_w_in": 256, "b_w_out": 128, "gu0": 256, "gu1": 256, "dn0": 176, "dn1": 176}
A_BLK = 9 * D_MODEL // N_DEV
B_BLK = 386
B_IN = 3 * D_MODEL + N_HEADS
B_IN_PAD = 3 * D_MODEL + LANES


def kernel(x, a_norm, a_w_in, a_w_out, b_norm, b_w_in, b_f, b_w_out, ffn_norm, ffn_w_gu, ffn_w_down, final_norm, loss_target, m_a_norm, m_a_w_in, m_a_w_out, m_b_norm, m_b_w_in, m_b_f, m_b_w_out, m_ffn_norm, m_ffn_w_gu, m_ffn_w_down, m_final_norm, v_a_norm, v_a_w_in, v_a_w_out, v_b_norm, v_b_w_in, v_b_f, v_b_w_out, v_ffn_norm, v_ffn_w_gu, v_ffn_w_down, v_final_norm):
    S = x.shape[1]
    xi, yi, ci = _position()
    dev = 4 * xi + 2 * yi + ci
    core = ci.reshape(1).astype(jnp.int32)
    h0, target = x.reshape(S, D_MODEL), loss_target.reshape(S, D_MODEL)

    def shards(a_in, a_out, b_in, b_out, gu, dn):
        return {"a_w_in": a_in[0], "a_w_out": a_out[0], "b_w_in": b_in[0], "b_w_out": b_out[0],
                "gu0": gu[0], "gu1": gu[1], "dn0": dn[0], "dn1": dn[1]}

    w_sh = shards(a_w_in, a_w_out, b_w_in, b_w_out, ffn_w_gu, ffn_w_down)
    m_sh = shards(m_a_w_in, m_a_w_out, m_b_w_in, m_b_w_out, m_ffn_w_gu, m_ffn_w_down)
    v_sh = shards(v_a_w_in, v_a_w_out, v_b_w_in, v_b_w_out, v_ffn_w_gu, v_ffn_w_down)
    wb = {n: w_sh[n].astype(BF16) for n in TENSORS}
    bf_pad = jnp.pad(b_f, ((0, 0), (0, LANES - N_HEADS)))
    tabs = _rope_tables(S)

    g_ain, g_aout = _comm_call("gather_a", _ag_send([wb["a_w_in"], wb["a_w_out"]]))
    dils = [dil for _, dil in DILATED_PATTERNS]
    n0_views, (g_ain, g_aout) = _rms_fwd("rms_a", h0, a_norm[0], dils, _ag_forward([g_ain, g_aout]))
    n0 = n0_views[0]
    w_a_in = g_ain.transpose(1, 0, 2).reshape(D_MODEL, 9 * D_MODEL)
    sends = [[wb["dn0"], jnp.pad(b_norm, ((0, 7), (0, 0)))], None, [wb["gu0"]]]
    qkv_a, sent = [], {}
    for g, dil in enumerate(dils):
        qkv_g, sent[g] = _a_proj("proj_a%d" % g, n0, w_a_in, g, dil, tabs, None if sends[g] is None else _ag_send(sends[g]))
        qkv_a.append(qkv_g)
    cols = [lambda r: r] * 3
    groups = [(g, dil, S // dil, qkv_a[g]) for g, (window, dil) in enumerate(DILATED_PATTERNS)]
    fwd = [_dil_fwd("dil_fwd%d" % g, view, *cols, dil, L, _ag_send([wb["b_w_in"], wb["b_w_out"]]) if g == 0 else None)
           for g, dil, L, view in groups]
    later = list(fwd[0][2:]) + [sent[2][0]] + list(sent[0])
    o_views, lse_views = _combine("dil_combine", [f[0] for f in fwd], [f[1] for f in fwd], dils)
    o_a = o_views[0]
    w_a_out = g_aout.reshape(D_MODEL, D_MODEL)
    h1, (g_bin, g_bout, g_gu0, g_dn0, g_bnorm) = _matmul("out_a", o_a, w_a_out, "nn", F32, TM, 1024, 1024, resid=h0,
                                                         comm=_ag_forward(later))

    n1 = _rms_fwd("rms_f0", h1, ffn_norm[0])
    gu0, act0 = _ffn_gu("gu_f0", n1, g_gu0)
    w_dn0 = g_dn0.reshape(D_FF, D_MODEL)
    h2 = _ffn_down("down_f0", act0, w_dn0, h1)[0]

    b_norm_full = g_bnorm[:, 0].reshape(D_MODEL)
    w_b_in = g_bin.transpose(1, 0, 2).reshape(D_MODEL, B_IN)
    w_b_gate = jnp.pad(w_b_in[:, 3 * D_MODEL:], ((0, 0), (0, LANES - N_HEADS)))
    w_b_cat = jnp.concatenate([w_b_in[:, :3 * D_MODEL], w_b_gate], axis=1)
    w_b_out = g_bout.reshape(D_MODEL, D_MODEL)
    n2 = _rms_fwd("rms_b", h2, b_norm_full)
    qkv = _matmul("proj_b", n2, w_b_in[:, :3 * D_MODEL], "nn", BF16, TM, 1024, 1024, col0_scale=SOFTMAX_SCALE)
    z = _matmul("gate_b", n2, w_b_gate, "nn", F32, TM, LANES, 1024)
    kbias = _gate_fwd("gate_cumsum", z, bf_pad)
    tf = min(S, 512)
    o_b, lse_b, g_gu1, g_dn1 = _fox_fwd("fox_fwd", qkv, kbias, tf, _ag_send([wb["gu1"], wb["dn1"]]))
    h3, (g_gu1, g_dn1) = _matmul("out_b", o_b, w_b_out, "nn", F32, TM, 1024, 1024, resid=h2, comm=_ag_forward([g_gu1, g_dn1]))

    w_dn1 = g_dn1.reshape(D_FF, D_MODEL)
    n3 = _rms_fwd("rms_f1", h3, ffn_norm[1])
    gu1, act1 = _ffn_gu("gu_f1", n3, g_gu1)
    h4 = _ffn_down("down_f1", act1, w_dn1, h3)[0]

    dh4, d_final, loss, dh4_16 = _loss_head("loss_head", h4, final_norm, target)

    share, got, sums, others = {}, {}, {}, {}

    def pair_sums(*names):
        for n in names:
            sums[n] = _pair_sum("pair_" + n, share[n], got[n], core, ROW_TILE[n])

    dgu1 = _ffn_dact("dact_f1", dh4_16, w_dn1, gu1)[0]
    share["dn1"] = _ffn_dwd("dwd_f1", act1, dh4_16)
    share["gu1"] = _ffn_dwgu("dwgu_f1", n3, dgu1)[0]
    dn3, got["gu1"], got["dn1"] = _ffn_dn("dn_f1", dgu1, g_gu1, _rs_swap([share["gu1"], share["dn1"]]))
    dh3, d_ffn1, dh3_16 = _rms_bwd("rmsb_f1", dn3, h3, ffn_norm[1], dh4)
    pair_sums("gu1", "dn1")

    do_b = _matmul("dout_b", dh3_16, w_b_out, "nt", BF16, TM, 1024, 1024)
    share["b_w_out"] = _matmul("dwout_b", o_b, dh3_16, "tn", BF16, TM, 1024, 1024).reshape(N_DEV, 128, D_MODEL)
    dq_b, dk_b, dv_b, ds_rowsum, ds_colsum, others["gu1"], others["dn1"] = _fox_bwd(
        "fox_bwd", qkv, kbias, do_b, o_b, lse_b, tf, _rs_exchange([sums["gu1"], sums["dn1"]]))
    dc = ds_rowsum[:, :N_HEADS] - ds_colsum.reshape(N_HEADS, S).T
    dz, d_bf = _gate_bwd("gate_bwd", jnp.pad(dc, ((0, 0), (0, LANES - N_HEADS))), z, bf_pad)
    dproj_b = jnp.concatenate([dq_b, dk_b, dv_b, dz.astype(BF16)], axis=1)
    dw_b_in = _matmul("dwin_b", n2, dproj_b, "tn", BF16, TM, B_IN_PAD // 5, 1024)
    dn2 = _matmul("dn_b", dproj_b, w_b_cat, "nt", F32, TM, 1024, B_IN_PAD // 5)
    dh2, d_bnorm, dh2_16 = _rms_bwd("rmsb_b", dn2, h2, b_norm_full, dh3)
    share["b_w_in"] = dw_b_in[:, :B_IN].reshape(D_MODEL, N_DEV, B_BLK).transpose(1, 0, 2)

    dgu0, got["b_w_in"], got["b_w_out"] = _ffn_dact("dact_f0", dh2_16, w_dn0, gu0, _rs_swap([share["b_w_in"], share["b_w_out"]]))
    share["dn0"] = _ffn_dwd("dwd_f0", act0, dh2_16)
    pair_sums("b_w_in", "b_w_out")
    share["gu0"], others["b_w_in"], others["b_w_out"] = _ffn_dwgu(
        "dwgu_f0", n1, dgu0, _rs_exchange([sums["b_w_in"], sums["b_w_out"]]))
    dn1, got["gu0"], got["dn0"] = _ffn_dn("dn_f0", dgu0, g_gu0, _rs_swap([share["gu0"], share["dn0"]]))
    dh1, d_ffn0, dh1_16 = _rms_bwd("rmsb_f0", dn1, h1, ffn_norm[0], dh2)
    pair_sums("gu0", "dn0")

    do_views = _matmul_nt_views("dout_a", dh1_16, w_a_out, dils)
    share["a_w_out"] = _matmul("dwout_a", o_a, dh1_16, "tn", BF16, TM, 1024, 1024).reshape(N_DEV, 128, D_MODEL)
    pieces = []
    for g, dil, L, view in groups:
        rot = tuple(tb.reshape(L, dil * LANES) for tb in tabs)
        res = _dil_bwd("dil_bwd%d" % g, view, do_views[g], o_views[g], lse_views[g], rot, dil, L,
                       _rs_exchange([sums["gu0"], sums["dn0"]]) if g == 0 else None)
        pieces.append(res[:3])
        if g == 0:
            others["gu0"], others["dn0"] = res[3:]
    dws = [_a_dw("dwin_a%d" % g, n0_views[g], pieces[g], dil, None)[0] for g, dil in enumerate(dils)]
    share["a_w_in"] = jnp.concatenate(dws, axis=1).reshape(D_MODEL, N_DEV, A_BLK).transpose(1, 0, 2)
    got["a_w_in"], got["a_w_out"] = _comm_call("swap_a", _rs_swap([share["a_w_in"], share["a_w_out"]]))
    pair_sums("a_w_in", "a_w_out")
    dn0, others["a_w_in"], others["a_w_out"] = _a_dn("dn_a", [p for ps in pieces for p in ps], dils, w_a_in,
                                                     _rs_exchange([sums["a_w_in"], sums["a_w_out"]]))
    dx, d_anorm = _rms_bwd("rmsb_a", dn0, h0, a_norm[0], dh1, copy16=False)

    misc = jnp.concatenate([d_bf[:, :N_HEADS], loss[:, :1], jnp.zeros((1, D_MODEL - N_HEADS - 1), F32)], axis=1)
    small = jnp.concatenate([d_anorm, d_ffn0, d_ffn1, d_final, d_bnorm, misc, jnp.zeros((2, D_MODEL), F32)], axis=0)
    small_all, = _comm_call("gather_small", _ag_send([small], direct=True))

    def grad_parts(n):
        return [lax.dynamic_index_in_dim(sums[n], 2 * xi + yi, axis=0, keepdims=False)] + [others[n][k] for k in range(3)]

    outs = {n: _adamw("adamw_" + n, grad_parts(n), w_sh[n], m_sh[n], v_sh[n], ROW_TILE[n])
            for n in ("a_w_in", "a_w_out", "b_w_in", "b_w_out")}
    outs["gu"] = _adamw("adamw_gu", [grad_parts("gu0"), grad_parts("gu1")], ffn_w_gu, m_ffn_w_gu, v_ffn_w_gu, ROW_TILE["gu0"])
    outs["dn"] = _adamw("adamw_dn", [grad_parts("dn0"), grad_parts("dn1")], ffn_w_down, m_ffn_w_down, v_ffn_w_down, ROW_TILE["dn0"])

    pad_vec = lambda a: jnp.pad(a, ((0, 0), (0, D_MODEL - a.shape[1])))

    def small_pack(an, fn, fin, bf):
        return jnp.concatenate([an, fn, fin.reshape(1, D_MODEL), jnp.zeros((1, D_MODEL), F32), pad_vec(bf),
                                jnp.zeros((2, D_MODEL), F32)], axis=0)

    sg, sd, sm, sv = _adamw("adamw_small", [small_all[d] for d in range(N_DEV)], small_pack(a_norm, ffn_norm, final_norm, b_f),
                            small_pack(m_a_norm, m_ffn_norm, m_final_norm, m_b_f),
                            small_pack(v_a_norm, v_ffn_norm, v_final_norm, v_b_f), 8)
    g_bn = lax.dynamic_slice(sg[4:5], (0, dev * LANES), (1, LANES))
    bn = _adamw("adamw_b_norm", [g_bn], b_norm, m_b_norm, v_b_norm, 1)

    def tree(i):
        full = lambda name, ref: outs[name][i].reshape(ref.shape)
        sml = (sg, sd, sm, sv)[i]
        return dict(
            a_norm=sml[0:1], a_w_in=full("a_w_in", a_w_in), a_w_out=full("a_w_out", a_w_out), b_norm=bn[i],
            b_w_in=full("b_w_in", b_w_in), b_f=sml[5:6, :N_HEADS], b_w_out=full("b_w_out", b_w_out), ffn_norm=sml[1:3],
            ffn_w_gu=outs["gu"][i], ffn_w_down=outs["dn"][i], final_norm=sml[3])

    order = ("a_norm", "a_w_in", "a_w_out", "b_norm", "b_w_in", "b_f", "b_w_out", "ffn_norm", "ffn_w_gu", "ffn_w_down", "final_norm")
    result = [sg[5, N_HEADS], dx.reshape(x.shape)]
    for i in range(4):
        t = tree(i)
        result += [t[n] for n in order]
    return tuple(result)
```

```python
import functools
from typing import Callable, NamedTuple

import jax
import jax.numpy as jnp
from jax import lax
from jax.experimental import pallas as pl
from jax.experimental.pallas import tpu as pltpu

F32 = jnp.float32
BF16 = jnp.bfloat16

D_MODEL = 1024
N_HEADS = 16
HEAD_DIM = 64
N_PAIRS = N_HEADS // 2
LANES = 128
DILATED_PATTERNS = ((128, 1), (512, 4), (2048, 16))
BAND_STEPS = 128
ROT_DIM = HEAD_DIM // 4
ROPE_THETA = 500000.0
D_FF = 2816
RMS_EPS = 1e-6
NEG_INF = -1e30
SOFTMAX_SCALE = HEAD_DIM ** -0.5
N_DEV = 8
FF_BLK = 2 * D_FF // N_DEV
ADAM_LR, ADAM_B1, ADAM_B2, ADAM_EPS, ADAM_WD, ADAM_STEP = 0.001, 0.9, 0.999, 1e-08, 0.01, 10
VMEM_LIMIT = 52 * 1024 * 1024
FOX_BWD_VMEM = 60 * 1024 * 1024
TM = 1024
MESH = pl.DeviceIdType.MESH

NN = (((1,), (0,)), ((), ()))
NT = (((1,), (1,)), ((), ()))
TN = (((0,), (0,)), ((), ()))


def _params(*sem, vmem=VMEM_LIMIT):
    return pltpu.CompilerParams(dimension_semantics=sem, vmem_limit_bytes=vmem)


def _dot(a, b, dims):
    return lax.dot_general(a, b, dims, preferred_element_type=F32)


class _Comm(NamedTuple):
    ins: tuple
    outs: tuple
    aliases: dict
    copies: Callable
    n_remote: int
    n_local: int


def _call(name, body, grid, in_specs, out_specs, out_shape, scratch, args, sem, comm=None, vmem=VMEM_LIMIT):
    if comm is None:
        return pl.pallas_call(body, grid=grid, in_specs=in_specs, out_specs=out_specs, out_shape=out_shape,
                              scratch_shapes=scratch, compiler_params=_params(*sem, vmem=vmem), name=name)(*args)
    n_in, n_out = len(in_specs), len(out_specs)
    n_ci, n_co = len(comm.ins), len(comm.outs)
    o0 = n_in + n_ci

    def hosted(*refs):
        c_ins, c_outs = refs[n_in:o0], refs[o0 + n_out:o0 + n_out + n_co]
        sems = refs[-3:]

        def start():
            for cp in comm.copies(c_ins, c_outs, *sems):
                cp.start()

        def wait():
            for cp in comm.copies(c_ins, c_outs, *sems):
                cp.wait()

        if not grid:
            start()
            body()
            wait()
            return
        ids = [pl.program_id(ax) for ax in range(len(grid))]
        pl.when(functools.reduce(jnp.logical_and, [i == 0 for i in ids]))(start)
        body(*refs[:n_in], *refs[o0:o0 + n_out], *refs[o0 + n_out + n_co:-3])
        pl.when(functools.reduce(jnp.logical_and, [i == g - 1 for i, g in zip(ids, grid)]))(wait)

    hbm = pl.BlockSpec(memory_space=pltpu.HBM)
    dma = pltpu.SemaphoreType.DMA
    return pl.pallas_call(
        hosted, grid=grid, in_specs=[*in_specs, *[hbm] * n_ci], out_specs=[*out_specs, *[hbm] * n_co],
        out_shape=[*out_shape, *comm.outs], input_output_aliases={n_in + i: n_out + o for i, o in comm.aliases.items()},
        scratch_shapes=[*scratch, dma((comm.n_remote,)), dma((comm.n_remote,)), dma((max(comm.n_local, 1),))],
        compiler_params=_params(*["arbitrary"] * len(grid), vmem=vmem), name=name)(*args, *comm.ins)


def _mm_call(name, grid, a, a_spec, b, b_spec, dims, out_shapes, out_specs, acc_shape, epilogue=None,
             extras=(), extra_specs=(), col_axis=1, comm=None):
    nk = grid[2]
    n_extra = len(extras)
    n_out = len(out_shapes)

    def finish(res, ex, outs, j):
        if epilogue is None:
            outs[0][...] = res.astype(outs[0].dtype)
        else:
            epilogue(res, ex, outs, j)

    def body(*refs):
        a_ref, b_ref = refs[0], refs[1]
        ex = refs[2:2 + n_extra]
        outs = refs[2 + n_extra:2 + n_extra + n_out]
        j, k = pl.program_id(col_axis), pl.program_id(2)
        part = _dot(a_ref[...].astype(BF16), b_ref[...].astype(BF16), dims)
        if nk == 1:
            finish(part, ex, outs, j)
            return
        acc = refs[-1]

        @pl.when(k == 0)
        def _():
            acc[...] = part

        @pl.when((k > 0) & (k < nk - 1))
        def _():
            acc[...] += part

        @pl.when(k == nk - 1)
        def _():
            finish(acc[...] + part, ex, outs, j)

    return _call(name, body, grid, [a_spec, b_spec, *extra_specs], out_specs, out_shapes,
                 [] if nk == 1 else [pltpu.VMEM(acc_shape, F32)], (a, b, *extras), ("parallel", "parallel", "arbitrary"), comm)


def _matmul(name, a, b, mode, out_dtype, tm, tn, tk, resid=None, col0_scale=None, comm=None):
    if mode == "nn":
        (M, K), N = a.shape, b.shape[1]
        a_spec = pl.BlockSpec((tm, tk), lambda j, i, k: (i, k))
        b_spec = pl.BlockSpec((tk, tn), lambda j, i, k: (k, j))
        dims = NN
    elif mode == "nt":
        (M, K), N = a.shape, b.shape[0]
        a_spec = pl.BlockSpec((tm, tk), lambda j, i, k: (i, k))
        b_spec = pl.BlockSpec((tn, tk), lambda j, i, k: (j, k))
        dims = NT
    else:
        (K, M), N = a.shape, b.shape[1]
        a_spec = pl.BlockSpec((tk, tm), lambda j, i, k: (k, i))
        b_spec = pl.BlockSpec((tk, tn), lambda j, i, k: (k, j))
        dims = TN
    assert M % tm == 0 and N % tn == 0 and K % tk == 0, (name, M, N, K, tm, tn, tk)
    o_spec = pl.BlockSpec((tm, tn), lambda j, i, k: (i, j))
    extras, extra_specs, epilogue = (), (), None
    if resid is not None:
        extras, extra_specs = (resid,), (o_spec,)

        def epilogue(acc, ex, outs, j):
            outs[0][...] = (acc + ex[0][...]).astype(outs[0].dtype)

    elif col0_scale is not None:

        def epilogue(acc, ex, outs, j):
            outs[0][...] = (acc * jnp.where(j == 0, col0_scale, 1.0)).astype(outs[0].dtype)

    res = _mm_call(name, (N // tn, M // tm, K // tk), a, a_spec, b, b_spec, dims, [jax.ShapeDtypeStruct((M, N), out_dtype)],
                   [o_spec], (tm, tn), epilogue, extras, extra_specs, col_axis=0, comm=comm)
    return res[0] if comm is None else (res[0], res[1:])


def _rms_fwd(name, h, gain, dils=(1,), comm=None, tm=512):
    S, D = h.shape

    def body(h_ref, g_ref, *rest):
        x = h_ref[...]
        rstd = lax.rsqrt(jnp.mean(x * x, axis=-1, keepdims=True) + RMS_EPS)
        y = x * rstd * g_ref[...]
        _write_views([y[:, b * LANES:(b + 1) * LANES] for b in range(N_PAIRS)], rest[-1], rest[:-1], dils, tm)

    res = _call(name, body, (S // tm,), [pl.BlockSpec((tm, D), lambda i: (i, 0)), pl.BlockSpec((1, D), lambda i: (0, 0))],
                [_view_spec(tm, R) for R in dils], [jax.ShapeDtypeStruct((S // R, R * D), BF16) for R in dils],
                [pltpu.VMEM((N_PAIRS, tm, LANES), F32)], (h, gain.reshape(1, D)), ("parallel",), comm)
    views = res[0] if len(dils) == 1 else res[:len(dils)]
    return views if comm is None else (views, res[len(dils):])


def _rms_bwd(name, dn, h, gain, dres, copy16=True, tm=512):
    S, D = h.shape

    def body(dn_ref, h_ref, g_ref, r_ref, dh_ref, dg_ref, *dh16_ref):
        x = h_ref[...]
        rstd = lax.rsqrt(jnp.mean(x * x, axis=-1, keepdims=True) + RMS_EPS)
        xhat = x * rstd
        d = dn_ref[...]
        dxhat = d * g_ref[...]
        dh = rstd * (dxhat - xhat * jnp.mean(dxhat * xhat, axis=-1, keepdims=True)) + r_ref[...]
        dh_ref[...] = dh
        if copy16:
            dh16_ref[0][...] = dh.astype(BF16)

        @pl.when(pl.program_id(0) == 0)
        def _():
            dg_ref[...] = jnp.zeros_like(dg_ref)

        dg_ref[...] += jnp.sum(d * xhat, axis=0, keepdims=True)

    row = pl.BlockSpec((tm, D), lambda i: (i, 0))
    vec = pl.BlockSpec((1, D), lambda i: (0, 0))
    return pl.pallas_call(
        body, grid=(S // tm,), in_specs=[row, row, vec, row], out_specs=[row, vec] + [row] * copy16,
        out_shape=[jax.ShapeDtypeStruct((S, D), F32), jax.ShapeDtypeStruct((1, D), F32)] + [jax.ShapeDtypeStruct((S, D), BF16)] * copy16,
        compiler_params=_params("arbitrary"), name=name)(dn, h, gain.reshape(1, D), dres)


def _loss_head(name, h, gain, target, tm=512):
    S, D = h.shape

    def body(h_ref, g_ref, t_ref, dh_ref, dg_ref, loss_ref, dh16_ref):
        x = h_ref[...]
        rstd = lax.rsqrt(jnp.mean(x * x, axis=-1, keepdims=True) + RMS_EPS)
        xhat = x * rstd
        err = xhat * g_ref[...] - t_ref[...]
        dy = err * (1.0 / D)
        dxhat = dy * g_ref[...]
        dh = rstd * (dxhat - xhat * jnp.mean(dxhat * xhat, axis=-1, keepdims=True))
        dh_ref[...] = dh
        dh16_ref[...] = dh.astype(BF16)

        @pl.when(pl.program_id(0) == 0)
        def _():
            dg_ref[...] = jnp.zeros_like(dg_ref)
            loss_ref[...] = jnp.zeros_like(loss_ref)

        dg_ref[...] += jnp.sum(dy * xhat, axis=0, keepdims=True)
        part = 0.5 * jnp.sum(jnp.mean(err * err, axis=-1, keepdims=True), axis=0, keepdims=True)
        loss_ref[...] += jnp.broadcast_to(part, loss_ref.shape)

    row = pl.BlockSpec((tm, D), lambda i: (i, 0))
    vec = pl.BlockSpec((1, D), lambda i: (0, 0))
    return pl.pallas_call(
        body, grid=(S // tm,), in_specs=[row, vec, row], out_specs=[row, vec, pl.BlockSpec((1, LANES), lambda i: (0, 0)), row],
        out_shape=[jax.ShapeDtypeStruct((S, D), F32), jax.ShapeDtypeStruct((1, D), F32),
                   jax.ShapeDtypeStruct((1, LANES), F32), jax.ShapeDtypeStruct((S, D), BF16)],
        compiler_params=_params("arbitrary"), name=name)(h, gain.reshape(1, D), target)


def _rope_tables(S):
    half = ROT_DIM // 2
    inv_freq = ROPE_THETA ** (-jnp.arange(half, dtype=F32) * 2.0 / ROT_DIM)
    inv_head = jnp.concatenate([inv_freq, inv_freq, jnp.zeros((HEAD_DIM - ROT_DIM,), F32)])
    ang = jnp.arange(S, dtype=F32)[:, None] * jnp.concatenate([inv_head, inv_head])[None, :]
    lane = (jnp.arange(LANES) % HEAD_DIM)[None, :]
    cos, sin = jnp.cos(ang), jnp.sin(ang)
    c = jnp.where(lane < ROT_DIM, cos, 1.0)
    sa = jnp.where(lane < half, -sin, 0.0)
    sb = jnp.where((lane >= half) & (lane < ROT_DIM), sin, 0.0)
    return c, sa, sb


def _rotate(x, c, sa, sb, sign):
    return x * c + sign * (pltpu.roll(x, LANES - ROT_DIM // 2, 1) * sa + pltpu.roll(x, ROT_DIM // 2, 1) * sb)


def _stage_chunks(scr, chunks):
    for c, x in enumerate(chunks):
        scr[c] = x


def _strided_rows(scr, c, r, n, R):
    return scr.at[c][pl.ds(r, n, stride=R), :]


def _a_proj(name, n, w, g, R, tabs, comm, tm=1024):
    S, D = n.shape
    n_i = S // tm
    n_out = 3

    def body(n_ref, w_ref, c_ref, sa_ref, sb_ref, *rest):
        outs, scr = rest[:n_out], rest[n_out]
        j = pl.program_id(0)
        acc = _dot(n_ref[...], w_ref[...], NN)
        c, sa, sb = c_ref[...], sa_ref[...], sb_ref[...]
        for J in range(n_out):
            kind = J

            @pl.when(j == J)
            def _(J=J, kind=kind):
                chunks = [acc[:, b * LANES:(b + 1) * LANES] for b in range(N_PAIRS)]
                if kind < 2:
                    chunks = [_rotate(x, c, sa, sb, 1.0) * (SOFTMAX_SCALE if kind == 0 else 1.0) for x in chunks]
                if R == 1:
                    for b, x in enumerate(chunks):
                        outs[J][:, b * LANES:(b + 1) * LANES] = x.astype(BF16)
                    return
                _stage_chunks(scr, chunks)
                for r in range(R):
                    for b in range(N_PAIRS):
                        col = r * D_MODEL + b * LANES
                        outs[J][:, col:col + LANES] = _strided_rows(scr, b, r, tm // R, R).astype(BF16)

    def out_spec(J):
        return pl.BlockSpec((tm // R, R * D_MODEL), lambda j, i: (jnp.where(j == J, i, jnp.where(j < J, 0, n_i - 1)), 0))

    tab = pl.BlockSpec((tm, LANES), lambda j, i: (i, 0))
    res = _call(name, body, (n_out, n_i),
                [pl.BlockSpec((tm, D), lambda j, i: (i, 0)), pl.BlockSpec((D, D_MODEL), lambda j, i: (0, 3 * g + j)), tab, tab, tab],
                [out_spec(J) for J in range(n_out)], [jax.ShapeDtypeStruct((S // R, R * D_MODEL), BF16)] * n_out,
                [pltpu.VMEM((N_PAIRS, tm, LANES), F32)], (n, w, *tabs), ("arbitrary", "arbitrary"), comm)
    return res[:n_out], res[n_out:]


def _unstride(src_chunk, R, tok, rows):
    for r in range(R):
        for b in range(N_PAIRS):
            tok.at[b][pl.ds(r, rows // R, stride=R), :] = src_chunk(r, b).astype(F32)


def _by_residue(ref, R):
    return ref[...] if R == 1 else jnp.concatenate([ref[:, r * D_MODEL:(r + 1) * D_MODEL] for r in range(R)], axis=0)


def _a_dw(name, n_view, pieces, R, comm, tk=1024):
    D = D_MODEL
    S = n_view.shape[0] * R
    n_k = S // tk
    n_p = len(pieces)

    def body(n_ref, *rest):
        p_refs, o_ref, acc = rest[:n_p], rest[n_p], rest[n_p + 1]
        j, k = pl.program_id(0), pl.program_id(1)
        for J in range(n_p):

            @pl.when(j == J)
            def _(J=J):
                part = _dot(_by_residue(n_ref, R), _by_residue(p_refs[J], R), TN)

                @pl.when(k == 0)
                def _():
                    acc[...] = part

                @pl.when((k > 0) & (k < n_k - 1))
                def _():
                    acc[...] += part

                @pl.when(k == n_k - 1)
                def _():
                    o_ref[...] = (acc[...] + part).astype(BF16)

    def piece_spec(J):
        return pl.BlockSpec((tk // R, R * D_MODEL), lambda j, k: (jnp.where(j == J, k, jnp.where(j < J, 0, n_k - 1)), 0))

    return _call(name, body, (n_p, n_k), [pl.BlockSpec((tk // R, R * D_MODEL), lambda j, k: (k, 0))] + [piece_spec(J) for J in range(n_p)],
                 [pl.BlockSpec((D, D_MODEL), lambda j, k: (0, j))], [jax.ShapeDtypeStruct((D, n_p * D_MODEL), BF16)],
                 [pltpu.VMEM((D, D_MODEL), F32)], (n_view, *pieces), ("arbitrary", "arbitrary"), comm)


def _a_dn(name, pieces, dils, w, comm, tm=512):
    D = w.shape[0]
    S = pieces[0].shape[0] * dils[0]
    n_p = len(pieces)

    def body(*refs):
        p_refs, w_ref, o_ref, acc, part_acc, tok = refs[:n_p], refs[n_p], refs[n_p + 1], refs[n_p + 2], refs[n_p + 3], refs[n_p + 4]
        j = pl.program_id(1)
        for J in range(n_p):

            @pl.when(j == J)
            def _(J=J):
                R, t = dils[J // 3], J % 3
                part = _dot(_by_residue(p_refs[J], R), w_ref[...], NT)
                if R == 1:
                    if J == 0:
                        acc[...] = part
                    else:
                        acc[...] += part
                    return
                if t == 0:
                    part_acc[...] = part
                    return
                if t == 1:
                    part_acc[...] += part
                    return
                n = tm // R
                _unstride(lambda r, b: part_acc[r * n:(r + 1) * n, b * LANES:(b + 1) * LANES]
                          + part[r * n:(r + 1) * n, b * LANES:(b + 1) * LANES], R, tok, tm)
                total = acc[...] + jnp.concatenate([tok[b] for b in range(N_PAIRS)], axis=1)
                if J == n_p - 1:
                    o_ref[...] = total
                else:
                    acc[...] = total

    specs = [pl.BlockSpec((tm // dils[J // 3], dils[J // 3] * D_MODEL), lambda i, j: (i, 0)) for J in range(n_p)]
    return _call(name, body, (S // tm, n_p), specs + [pl.BlockSpec((D, D_MODEL), lambda i, j: (0, j))],
                 [pl.BlockSpec((tm, D), lambda i, j: (i, 0))], [jax.ShapeDtypeStruct((S, D), F32)],
                 [pltpu.VMEM((tm, D), F32), pltpu.VMEM((tm, D), F32), pltpu.VMEM((N_PAIRS, tm, LANES), F32)], (*pieces, w),
                 ("arbitrary", "arbitrary"), comm)


def _lo_lanes():
    return lax.broadcasted_iota(jnp.int32, (1, LANES), 1) < HEAD_DIM


def _rep_rows(x2, lo):
    sw = pltpu.roll(x2, HEAD_DIM, 1)
    return jnp.where(lo, x2, sw), jnp.where(lo, sw, x2)


def _pair_cols(h):
    return slice((h // 2) * LANES, (h // 2 + 1) * LANES)


def _head_lanes(lo, h):
    return lo if h % 2 == 0 else jnp.logical_not(lo)


def _band_masks(t, first):
    ri = lax.broadcasted_iota(jnp.int32, (t, t), 0)
    ci = lax.broadcasted_iota(jnp.int32, (t, t), 1)
    neg_prev = jnp.where((ci >= ri) & jnp.logical_not(first), 0.0, NEG_INF)
    neg_cur = jnp.where(ci <= ri, 0.0, NEG_INF)
    return neg_prev, neg_cur


def _dil_specs(L, R, t, qcol, kcol, vcol):
    W = D_MODEL
    prev = lambda qi: jnp.maximum(qi - 1, 0)
    return dict(
        q=pl.BlockSpec((t, W), lambda r, qi: (qi, qcol(r))),
        kp=pl.BlockSpec((t, W), lambda r, qi: (prev(qi), kcol(r))), kc=pl.BlockSpec((t, W), lambda r, qi: (qi, kcol(r))),
        vp=pl.BlockSpec((t, W), lambda r, qi: (prev(qi), vcol(r))), vc=pl.BlockSpec((t, W), lambda r, qi: (qi, vcol(r))),
        own=pl.BlockSpec((t, W), lambda r, qi: (qi, r)), tab=pl.BlockSpec((t, LANES), lambda r, qi: (qi, r)))


def _dil_fwd(name, x, qcol, kcol, vcol, R, L, comm=None):
    t = BAND_STEPS
    W = D_MODEL
    sp = _dil_specs(L, R, t, qcol, kcol, vcol)

    def body(q_ref, kp_ref, kc_ref, vp_ref, vc_ref, o_ref, lse_ref):
        lo = _lo_lanes()
        neg_p, neg_c = _band_masks(t, pl.program_id(1) == 0)
        s_p, s_c = [], []
        for h in range(N_HEADS):
            cols = _pair_cols(h)
            qh = jnp.where(_head_lanes(lo, h), q_ref[:, cols], 0)
            s_p.append(_dot(qh, kp_ref[:, cols], NT))
            s_c.append(_dot(qh, kc_ref[:, cols], NT))
        s_p = jnp.stack(s_p) + neg_p[None]
        s_c = jnp.stack(s_c) + neg_c[None]
        m = jnp.maximum(jnp.max(s_p, axis=2, keepdims=True), jnp.max(s_c, axis=2, keepdims=True))
        p_p, p_c = jnp.exp(s_p - m), jnp.exp(s_c - m)
        l = jnp.sum(p_p, axis=2, keepdims=True) + jnp.sum(p_c, axis=2, keepdims=True)
        inv, lse = 1.0 / l, m + jnp.log(l)
        p_p, p_c = p_p.astype(BF16), p_c.astype(BF16)
        for p in range(N_PAIRS):
            cols = _pair_cols(2 * p)
            o2 = jnp.zeros((t, LANES), F32)
            for h in (2 * p, 2 * p + 1):
                hm = _head_lanes(lo, h)
                pv = _dot(p_p[h], jnp.where(hm, vp_ref[:, cols], 0), NN) + _dot(p_c[h], jnp.where(hm, vc_ref[:, cols], 0), NN)
                o2 = o2 + pv * inv[h]
            o_ref[:, cols] = o2
            lse_ref[:, cols] = jnp.where(lo, lse[2 * p], lse[2 * p + 1])

    return _call(name, body, (R, L // t), [sp["q"], sp["kp"], sp["kc"], sp["vp"], sp["vc"]], [sp["own"], sp["own"]],
                 [jax.ShapeDtypeStruct((L, R * W), F32), jax.ShapeDtypeStruct((L, R * W), F32)], [],
                 (x[0], x[1], x[1], x[2], x[2]), ("parallel", "parallel"), comm)


def _dil_scores(lo, q_ref, do_ref, o_ref, lse_ref, kv_refs):
    s = [[] for _ in kv_refs]
    dp = [[] for _ in kv_refs]
    lse, d = [], []
    for h in range(N_HEADS):
        cols = _pair_cols(h)
        hm = _head_lanes(lo, h)
        qh, doh = jnp.where(hm, q_ref[:, cols], 0), jnp.where(hm, do_ref[:, cols], 0)
        for i, (k_ref, v_ref) in enumerate(kv_refs):
            s[i].append(_dot(qh, k_ref[:, cols], NT))
            dp[i].append(_dot(doh, v_ref[:, cols], NT))
        lse.append(_rep_rows(lse_ref[:, cols], lo)[h % 2])
        dd = do_ref[:, cols].astype(F32) * o_ref[:, cols].astype(F32)
        d.append(jnp.sum(jnp.where(hm, dd, 0.0), axis=1, keepdims=True))
    return (*[jnp.stack(x) for x in s], *[jnp.stack(x) for x in dp], jnp.stack(lse), jnp.stack(d))


def _dil_bwd(name, x, do, o, lse, tabs, R, L, comm=None):
    t = BAND_STEPS
    W = D_MODEL
    nq = L // t
    qb = lambda step: nq - 1 - step
    kb = lambda step: jnp.maximum(qb(step) - 1, 0)
    at = lambda f, width: pl.BlockSpec((t, width), lambda r, step: (f(step), r))

    def body(q_ref, kp_ref, kc_ref, vp_ref, vc_ref, do_ref, o_ref, lse_ref, c_ref, sa_ref, sb_ref, dq_ref, dk_ref, dv_ref,
             dk_scr, dv_scr):
        qi = nq - 1 - pl.program_id(1)
        lo = _lo_lanes()
        unrotate = lambda x: _rotate(x, c_ref[...], sa_ref[...], sb_ref[...], -1.0).astype(BF16)

        @pl.when(qi == nq - 1)
        def _():
            dk_scr[...] = jnp.zeros_like(dk_scr)
            dv_scr[...] = jnp.zeros_like(dv_scr)

        neg_p, neg_c = _band_masks(t, qi == 0)
        s_p, s_c, dp_p, dp_c, lse_h, d = _dil_scores(lo, q_ref, do_ref, o_ref, lse_ref, ((kp_ref, vp_ref), (kc_ref, vc_ref)))
        p_p, p_c = jnp.exp(s_p + neg_p[None] - lse_h), jnp.exp(s_c + neg_c[None] - lse_h)
        ds_p, ds_c = (p_p * (dp_p - d)).astype(BF16), (p_c * (dp_c - d)).astype(BF16)
        p_p, p_c = p_p.astype(BF16), p_c.astype(BF16)
        for p in range(N_PAIRS):
            cols = _pair_cols(2 * p)
            dq2 = jnp.zeros((t, LANES), F32)
            dk_cur, dv_cur = dk_scr[:, cols], dv_scr[:, cols]
            dk_prev, dv_prev = jnp.zeros((t, LANES), F32), jnp.zeros((t, LANES), F32)
            for h in (2 * p, 2 * p + 1):
                hm = _head_lanes(lo, h)
                qh, doh = jnp.where(hm, q_ref[:, cols], 0), jnp.where(hm, do_ref[:, cols], 0)
                dq2 = dq2 + _dot(ds_p[h], jnp.where(hm, kp_ref[:, cols], 0), NN) + _dot(ds_c[h], jnp.where(hm, kc_ref[:, cols], 0), NN)
                dk_prev, dv_prev = dk_prev + _dot(ds_p[h], qh, TN), dv_prev + _dot(p_p[h], doh, TN)
                dk_cur, dv_cur = dk_cur + _dot(ds_c[h], qh, TN), dv_cur + _dot(p_c[h], doh, TN)
            dq_ref[:, cols] = unrotate(dq2 * SOFTMAX_SCALE)
            dk_ref[:, cols] = unrotate(dk_cur)
            dv_ref[:, cols] = dv_cur.astype(BF16)
            dk_scr[:, cols] = dk_prev
            dv_scr[:, cols] = dv_prev

    wide = jax.ShapeDtypeStruct((L, R * W), BF16)
    return _call(name, body, (R, nq),
                 [at(qb, W), at(kb, W), at(qb, W), at(kb, W), at(qb, W), at(qb, W), at(qb, W), at(qb, W),
                  at(qb, LANES), at(qb, LANES), at(qb, LANES)],
                 [at(qb, W), at(qb, W), at(qb, W)], [wide, wide, wide], [pltpu.VMEM((t, W), F32), pltpu.VMEM((t, W), F32)],
                 (x[0], x[1], x[1], x[2], x[2], do, o, lse, *tabs), ("parallel", "arbitrary"), comm)


def _fox_operands(q2, k2, kb2, lo, hh):
    lane = lax.broadcasted_iota(jnp.int32, (1, LANES), 1)
    if hh == 0:
        ones = ((lane >= HEAD_DIM) & (lane < HEAD_DIM + 3)).astype(BF16)
        return jnp.where(lo, q2, ones), jnp.where(lo, k2, kb2)
    ones = (lane < 3).astype(BF16)
    return jnp.where(lo, ones, q2), jnp.where(lo, kb2, k2)


def _causal_neg(t):
    ri = lax.broadcasted_iota(jnp.int32, (t, t), 0)
    ci = lax.broadcasted_iota(jnp.int32, (t, t), 1)
    return jnp.where(ci <= ri, 0.0, NEG_INF)


def _fox_fwd(name, qkv, kbias, t, comm=None):
    S = qkv.shape[0]
    W = D_MODEL
    nq = S // t
    rep = t // LANES

    def body(q_ref, k_ref, v_ref, kb_ref, o_ref, lse_ref, m_scr, l_scr, acc_scr):
        qi, j = pl.program_id(0), pl.program_id(1)
        lo = _lo_lanes()

        @pl.when(j == 0)
        def _():
            m_scr[...] = jnp.full_like(m_scr, NEG_INF)
            l_scr[...] = jnp.zeros_like(l_scr)
            acc_scr[...] = jnp.zeros_like(acc_scr)

        def step(masked):
            neg = _causal_neg(t) if masked else None

            def pair(p, carry):
                cs = pl.ds(pl.multiple_of(p * LANES, LANES), LANES)
                q2, k2, v2, kb2 = q_ref[:, cs], k_ref[:, cs], v_ref[:, cs], kb_ref[:, cs]
                pvs, alphas = [], []
                for hh in range(2):
                    hm = lo if hh == 0 else jnp.logical_not(lo)
                    qh, kh = _fox_operands(q2, k2, kb2, lo, hh)
                    s = _dot(qh, kh, NT)
                    if masked:
                        s = s + neg
                    h = 2 * p + hh
                    m_prev = m_scr[h]
                    m_new = jnp.maximum(m_prev, jnp.max(s, axis=1, keepdims=True))
                    pe = jnp.exp(s - jnp.tile(m_new, (1, rep)))
                    alpha = jnp.exp(m_prev - m_new)
                    l_scr[h] = alpha * l_scr[h] + jnp.sum(pe, axis=1, keepdims=True)
                    m_scr[h] = m_new
                    pvs.append(_dot(pe.astype(BF16), jnp.where(hm, v2, 0), NN))
                    alphas.append(alpha)
                acc_scr[:, cs] = acc_scr[:, cs] * jnp.where(lo, alphas[0], alphas[1]) + pvs[0] + pvs[1]
                return carry

            lax.fori_loop(0, N_PAIRS, pair, 0, unroll=4)

        @pl.when(j < qi)
        def _():
            step(False)

        @pl.when(j == qi)
        def _():
            step(True)

        @pl.when(j == nq - 1)
        def _():
            for p in range(N_PAIRS):
                cols = slice(p * LANES, (p + 1) * LANES)
                l2 = jnp.where(lo, l_scr[2 * p], l_scr[2 * p + 1])
                m2 = jnp.where(lo, m_scr[2 * p], m_scr[2 * p + 1])
                o_ref[:, cols] = (acc_scr[:, cols] / l2).astype(BF16)
                lse_ref[:, cols] = m2 + jnp.log(l2)

    kv = lambda col: pl.BlockSpec((t, W), lambda qi, j: (jnp.minimum(j, qi), col))
    own = pl.BlockSpec((t, W), lambda qi, j: (qi, 0))
    return _call(name, body, (nq, nq), [own, kv(1), kv(2), kv(0)], [own, own],
                 [jax.ShapeDtypeStruct((S, W), BF16), jax.ShapeDtypeStruct((S, W), F32)],
                 [pltpu.VMEM((N_HEADS, t, LANES), F32), pltpu.VMEM((N_HEADS, t, LANES), F32), pltpu.VMEM((t, W), F32)],
                 (qkv, qkv, qkv, kbias), ("parallel", "arbitrary"), comm)


def _fox_head_grads(qh, kh, v2, doh, neg, lse_h, d_h, rep):
    s = _dot(qh, kh, NT)
    if neg is not None:
        s = s + neg
    p = jnp.exp(s - jnp.tile(lse_h, (1, rep)))
    return p, p * (_dot(doh, v2, NT) - d_h)


def _fox_bwd(name, qkv, kbias, do, o, lse, t, comm=None):
    S = qkv.shape[0]
    W = D_MODEL
    nq = S // t
    rep = t // LANES

    def body(q_ref, k_ref, v_ref, kb_ref, do_ref, o_ref, lse_ref, dq_ref, dk_ref, dv_ref, rs_ref, dc_ref, dq_scr, dk_scr, dv_scr):
        kb, j = pl.program_id(0), pl.program_id(1)
        lo = _lo_lanes()
        lane = lax.broadcasted_iota(jnp.int32, (1, LANES), 1)
        rows = pl.ds(pl.multiple_of(j * t, t), t)

        @pl.when((kb == 0) & (j == 0))
        def _():
            dq_scr[...] = jnp.zeros_like(dq_scr)
            rs_ref[...] = jnp.zeros_like(rs_ref)

        @pl.when(j == 0)
        def _():
            dk_scr[...] = jnp.zeros_like(dk_scr)
            dv_scr[...] = jnp.zeros_like(dv_scr)
            dc_ref[...] = jnp.zeros_like(dc_ref)

        def step(masked):
            neg = _causal_neg(t) if masked else None

            def pair(p, carry):
                cs = pl.ds(pl.multiple_of(p * LANES, LANES), LANES)
                q2, k2, v2, kb2, do2 = q_ref[:, cs], k_ref[:, cs], v_ref[:, cs], kb_ref[:, cs], do_ref[:, cs]
                dd = do2.astype(F32) * o_ref[:, cs].astype(F32)
                lse_h = _rep_rows(lse_ref[:, cs], lo)
                dq2 = jnp.zeros((t, LANES), F32)
                dv2 = jnp.zeros((t, LANES), F32)
                dk2 = jnp.zeros((t, LANES), F32)
                for hh in range(2):
                    hm = lo if hh == 0 else jnp.logical_not(lo)
                    qh, kh = _fox_operands(q2, k2, kb2, lo, hh)
                    doh = jnp.where(hm, do2, 0)
                    d_h = jnp.sum(jnp.where(hm, dd, 0.0), axis=1, keepdims=True)
                    pr, ds = _fox_head_grads(qh, kh, v2, doh, neg, lse_h[hh], d_h, rep)
                    rs_ref[rows, :] += jnp.where(lane == 2 * p + hh, jnp.sum(ds, axis=1, keepdims=True), 0.0)
                    dc_ref[p, hh:hh + 1, :] += jnp.sum(ds, axis=0, keepdims=True)
                    dsb = ds.astype(BF16)
                    dv2 = dv2 + _dot(pr.astype(BF16), doh, TN)
                    dk2 = dk2 + _dot(dsb, jnp.where(hm, q2, 0), TN)
                    dq2 = dq2 + _dot(dsb, jnp.where(hm, k2, 0), NN)
                dv_scr[:, cs] += dv2
                dk_scr[:, cs] += dk2
                dq_scr[rows, cs] += dq2
                return carry

            lax.fori_loop(0, N_PAIRS, pair, 0, unroll=4)
            if masked:
                dq_ref[...] = (dq_scr[rows, :] * SOFTMAX_SCALE).astype(BF16)

        @pl.when(j > kb)
        def _():
            step(False)

        @pl.when(j == kb)
        def _():
            step(True)

        @pl.when(j == nq - 1)
        def _():
            dv_ref[...] = dv_scr[...].astype(BF16)
            dk_ref[...] = dk_scr[...].astype(BF16)

    qrow = pl.BlockSpec((t, W), lambda kb, j: (jnp.maximum(j, kb), 0))
    krow = lambda col: pl.BlockSpec((t, W), lambda kb, j: (kb, col))
    own = pl.BlockSpec((t, W), lambda kb, j: (kb, 0))
    wide = jax.ShapeDtypeStruct((S, W), BF16)
    return _call(name, body, (nq, nq), [qrow, krow(1), krow(2), krow(0), qrow, qrow, qrow],
                 [own, own, own, pl.BlockSpec((S, LANES), lambda kb, j: (0, 0)), pl.BlockSpec((N_PAIRS, 2, t), lambda kb, j: (0, 0, kb))],
                 [wide, wide, wide, jax.ShapeDtypeStruct((S, LANES), F32), jax.ShapeDtypeStruct((N_PAIRS, 2, S), F32)],
                 [pltpu.VMEM((S, W), F32), pltpu.VMEM((t, W), F32), pltpu.VMEM((t, W), F32)],
                 (qkv, qkv, qkv, kbias, do, o, lse), ("arbitrary", "arbitrary"), comm, vmem=FOX_BWD_VMEM)


def _view_spec(tm, R, index=lambda i: (i, 0)):
    return pl.BlockSpec((tm // R, R * D_MODEL), index)


def _matmul_nt_views(name, a, w, dils, tm=512):
    S, K = a.shape

    def body(a_ref, w_ref, *rest):
        res = _dot(a_ref[...].astype(BF16), w_ref[...], NT)
        _write_views([res[:, b * LANES:(b + 1) * LANES] for b in range(N_PAIRS)], rest[-1], rest[:-1], dils, tm)

    return pl.pallas_call(
        body, grid=(S // tm,), in_specs=[pl.BlockSpec((tm, K), lambda i: (i, 0)), pl.BlockSpec((D_MODEL, K), lambda i: (0, 0))],
        out_specs=[_view_spec(tm, R) for R in dils],
        out_shape=[jax.ShapeDtypeStruct((S // R, R * D_MODEL), BF16) for R in dils],
        scratch_shapes=[pltpu.VMEM((N_PAIRS, tm, LANES), F32)], compiler_params=_params("parallel"), name=name)(a, w)


def _write_views(chunks, scr, out_refs, dils, tm):
    if any(R > 1 for R in dils):
        _stage_chunks(scr, chunks)
    for ref, R in zip(out_refs, dils):
        for b, x in enumerate(chunks):
            if R == 1:
                ref[:, b * LANES:(b + 1) * LANES] = x.astype(ref.dtype)
                continue
            for r in range(R):
                col = r * D_MODEL + b * LANES
                ref[:, col:col + LANES] = _strided_rows(scr, b, r, tm // R, R).astype(ref.dtype)


def _combine(name, os_, lses, dils, tm=256):
    S = os_[0].shape[0] * dils[0]
    G = len(dils)

    def body(*refs):
        o_refs, l_refs = refs[:G], refs[G:2 * G]
        o_outs, l_outs = refs[2 * G:3 * G], refs[3 * G:4 * G]
        stage = refs[4 * G:]
        for g, R in enumerate(dils):
            if R == 1:
                continue
            for src, dst in ((o_refs[g], stage[2 * g]), (l_refs[g], stage[2 * g + 1])):
                _unstride(lambda r, b, src=src: src[:, r * D_MODEL + b * LANES:r * D_MODEL + (b + 1) * LANES], R, dst, tm)
        o_chunks, l_chunks = [], []
        for b in range(N_PAIRS):
            cols = slice(b * LANES, (b + 1) * LANES)
            os_b = [o_refs[g][:, cols] if R == 1 else stage[2 * g][b] for g, R in enumerate(dils)]
            ls = [l_refs[g][:, cols] if R == 1 else stage[2 * g + 1][b] for g, R in enumerate(dils)]
            m = functools.reduce(jnp.maximum, ls)
            ws = [jnp.exp(l - m) for l in ls]
            den = functools.reduce(jnp.add, ws)
            o_chunks.append(functools.reduce(jnp.add, [w * o for w, o in zip(ws, os_b)]) / den)
            l_chunks.append(m + jnp.log(den))
        _write_views(o_chunks, stage[0], o_outs, dils, tm)
        _write_views(l_chunks, stage[1], l_outs, dils, tm)

    specs = [_view_spec(tm, R) for R in dils]
    shapes = lambda dt: [jax.ShapeDtypeStruct((S // R, R * D_MODEL), dt) for R in dils]
    res = pl.pallas_call(
        body, grid=(S // tm,), in_specs=specs * 2, out_specs=specs * 2, out_shape=shapes(BF16) + shapes(F32),
        scratch_shapes=[pltpu.VMEM((N_PAIRS, tm, LANES), F32)] * (2 * G), compiler_params=_params("parallel"),
        name=name)(*os_, *lses)
    return res[:G], res[G:]


def _tri_matmul(tri, x):
    hi, mid, lo = _split3(x)
    return _dot(tri, hi, NN) + _dot(tri, mid, NN) + _dot(tri, lo, NN)


def _split3(x):
    hi = x.astype(BF16)
    r1 = x - hi.astype(F32)
    mid = r1.astype(BF16)
    return hi, mid, (r1 - mid.astype(F32)).astype(BF16)


def _gate_fwd(name, z, bf, tb=512):
    S = z.shape[0]

    def body(z_ref, b_ref, kb_ref, carry):
        @pl.when(pl.program_id(0) == 0)
        def _():
            carry[...] = jnp.zeros_like(carry)

        lf = jax.nn.log_sigmoid(z_ref[...] + b_ref[...])
        ri = lax.broadcasted_iota(jnp.int32, (tb, tb), 0)
        ci = lax.broadcasted_iota(jnp.int32, (tb, tb), 1)
        tri = (ci <= ri).astype(BF16)
        c = _tri_matmul(tri, lf) + carry[...]
        carry[...] = c[tb - 1:tb, :]
        head = lax.broadcasted_iota(jnp.int32, (LANES, D_MODEL), 0)
        col = lax.broadcasted_iota(jnp.int32, (LANES, D_MODEL), 1)
        base = (head >> 1) * LANES + jnp.where((head & 1) == 0, HEAD_DIM, 0)
        kb = jnp.zeros((tb, D_MODEL), F32)
        for i, piece in enumerate(_split3(-c)):
            place = ((col == base + i) & (head < N_HEADS)).astype(BF16)
            kb = kb + _dot(piece, place, NN)
        kb_ref[...] = kb.astype(BF16)

    row = pl.BlockSpec((tb, LANES), lambda i: (i, 0))
    return pl.pallas_call(
        body, grid=(S // tb,), in_specs=[row, pl.BlockSpec((1, LANES), lambda i: (0, 0))],
        out_specs=pl.BlockSpec((tb, D_MODEL), lambda i: (i, 0)), out_shape=jax.ShapeDtypeStruct((S, D_MODEL), BF16),
        scratch_shapes=[pltpu.VMEM((1, LANES), F32)], compiler_params=_params("arbitrary"), name=name)(z, bf)


def _gate_bwd(name, dc, z, bf, tb=512):
    S = z.shape[0]
    nb = S // tb

    def body(dc_ref, z_ref, b_ref, dz_ref, db_ref, carry):
        @pl.when(pl.program_id(0) == 0)
        def _():
            carry[...] = jnp.zeros_like(carry)
            db_ref[...] = jnp.zeros_like(db_ref)

        ri = lax.broadcasted_iota(jnp.int32, (tb, tb), 0)
        ci = lax.broadcasted_iota(jnp.int32, (tb, tb), 1)
        tri = (ci >= ri).astype(BF16)
        dlf = _tri_matmul(tri, dc_ref[...]) + carry[...]
        carry[...] = dlf[0:1, :]
        dz = dlf * jax.nn.sigmoid(-(z_ref[...] + b_ref[...]))
        dz_ref[...] = dz
        db_ref[...] += jnp.sum(dz, axis=0, keepdims=True)

    row = pl.BlockSpec((tb, LANES), lambda i: (nb - 1 - i, 0))
    vec = pl.BlockSpec((1, LANES), lambda i: (0, 0))
    return pl.pallas_call(
        body, grid=(nb,), in_specs=[row, row, vec], out_specs=[row, vec],
        out_shape=[jax.ShapeDtypeStruct((S, LANES), F32), jax.ShapeDtypeStruct((1, LANES), F32)],
        scratch_shapes=[pltpu.VMEM((1, LANES), F32)], compiler_params=_params("arbitrary"), name=name)(dc, z, bf)


def _ffn_gu(name, n, wgu, comm=None, tm=1024):
    S, D = n.shape
    nb = N_DEV // 2

    def body(n_ref, wg_ref, wu_ref, gu_ref, act_ref):
        x = n_ref[...]
        g = _dot(x, wg_ref[...], NN)
        u = _dot(x, wu_ref[...], NN)
        gu_ref[0] = g.astype(BF16)
        gu_ref[1] = u.astype(BF16)
        act_ref[...] = (g * jax.nn.sigmoid(g) * u).astype(BF16)

    return _call(
        name, body, (nb, S // tm),
        [pl.BlockSpec((tm, D), lambda j, i: (i, 0)), pl.BlockSpec((None, D, FF_BLK), lambda j, i: (j, 0, 0)),
         pl.BlockSpec((None, D, FF_BLK), lambda j, i: (j + nb, 0, 0))],
        [pl.BlockSpec((2, None, tm, FF_BLK), lambda j, i: (0, j, i, 0)), pl.BlockSpec((None, tm, FF_BLK), lambda j, i: (j, i, 0))],
        [jax.ShapeDtypeStruct((2, nb, S, FF_BLK), BF16), jax.ShapeDtypeStruct((nb, S, FF_BLK), BF16)], [],
        (n, wgu, wgu), ("parallel", "parallel"), comm)


def _ffn_down(name, act, wd, resid, comm=None, tm=1024):
    nb, S, _ = act.shape
    D = wd.shape[1]

    def epilogue(acc, ex, outs, j):
        outs[0][...] = acc + ex[0][...]

    o_spec = pl.BlockSpec((tm, D), lambda i, j, k: (i, 0))
    return _mm_call(name, (S // tm, 1, nb), act, pl.BlockSpec((None, tm, FF_BLK), lambda i, j, k: (k, i, 0)),
                    wd, pl.BlockSpec((FF_BLK, D), lambda i, j, k: (k, 0)), NN,
                    [jax.ShapeDtypeStruct((S, D), F32)], [o_spec], (tm, D), epilogue, (resid,), (o_spec,), comm=comm)


def _ffn_dact(name, dh, wd, gu, comm=None, tm=512):
    S, D = dh.shape
    nb = N_DEV // 2

    def epilogue(acc, ex, outs, j):
        g = ex[0][0].astype(F32)
        u = ex[0][1].astype(F32)
        sig = jax.nn.sigmoid(g)
        outs[0][0] = (acc * u * (sig * (1.0 + g * (1.0 - sig)))).astype(BF16)
        outs[0][1] = (acc * (g * sig)).astype(BF16)

    gu_spec = pl.BlockSpec((2, None, tm, FF_BLK), lambda j, i, k: (0, j, i, 0))
    return _mm_call(name, (nb, S // tm, 1), dh, pl.BlockSpec((tm, D), lambda j, i, k: (i, 0)),
                    wd, pl.BlockSpec((FF_BLK, D), lambda j, i, k: (j, 0)), NT,
                    [jax.ShapeDtypeStruct((2, nb, S, FF_BLK), BF16)], [gu_spec], (tm, FF_BLK), epilogue, (gu,), (gu_spec,),
                    col_axis=0, comm=comm)


def _ffn_dwgu(name, n, dgu, comm=None, tm=1024, tk=1024):
    S, D = n.shape
    dgu8 = dgu.reshape(N_DEV, S, FF_BLK)
    return _mm_call(name, (N_DEV, D // tm, S // tk), n, pl.BlockSpec((tk, tm), lambda d, i, k: (k, i)),
                    dgu8, pl.BlockSpec((None, tk, FF_BLK), lambda d, i, k: (d, k, 0)), TN,
                    [jax.ShapeDtypeStruct((N_DEV, D, FF_BLK), BF16)],
                    [pl.BlockSpec((None, tm, FF_BLK), lambda d, i, k: (d, i, 0))], (tm, FF_BLK), comm=comm)


def _ffn_dwd(name, act, dh, tk=1024):
    nb, S, _ = act.shape
    D = dh.shape[1]
    out = _mm_call(name, (nb, 1, S // tk), act, pl.BlockSpec((None, tk, FF_BLK), lambda b, j, k: (b, k, 0)),
                   dh, pl.BlockSpec((tk, D), lambda b, j, k: (k, 0)), TN,
                   [jax.ShapeDtypeStruct((nb, FF_BLK, D), BF16)],
                   [pl.BlockSpec((None, FF_BLK, D), lambda b, j, k: (b, 0, 0))], (FF_BLK, D))[0]
    return out.reshape(N_DEV, FF_BLK // 2, D)


def _ffn_dn(name, dgu, wgu, comm=None, tm=1024):
    S = dgu.shape[2]
    D = wgu.shape[1]
    dgu8 = dgu.reshape(N_DEV, S, FF_BLK)
    return _mm_call(name, (S // tm, 1, N_DEV), dgu8, pl.BlockSpec((None, tm, FF_BLK), lambda i, j, k: (k, i, 0)),
                    wgu, pl.BlockSpec((None, D, FF_BLK), lambda i, j, k: (k, 0, 0)), NT,
                    [jax.ShapeDtypeStruct((S, D), F32)], [pl.BlockSpec((tm, D), lambda i, j, k: (i, 0))], (tm, D), comm=comm)


def _adamw(name, parts, w, m, v, tr):
    stacked = w.ndim == 3
    by_layer = parts if stacked else [parts]
    n_l, n_parts = len(by_layer), len(by_layer[0])
    rows, cols = w.shape[-2:]
    n_i = rows // tr
    c1 = 1.0 - ADAM_B1 ** ADAM_STEP
    c2 = 1.0 - ADAM_B2 ** ADAM_STEP

    def body(*refs):
        w_ref, m_ref, v_ref, g_ref, d_ref, nm_ref, nv_ref = refs[n_l * n_parts:]
        layer = pl.program_id(0)
        for L in range(n_l):

            @pl.when(layer == L)
            def _(L=L):
                p_refs = refs[L * n_parts:(L + 1) * n_parts]
                g = p_refs[0][...].astype(F32)
                for r in p_refs[1:]:
                    g = g + r[...].astype(F32)
                mm = ADAM_B1 * m_ref[...] + (1.0 - ADAM_B1) * g
                vv = ADAM_B2 * v_ref[...] + (1.0 - ADAM_B2) * (g * g)
                g_ref[...] = g
                nm_ref[...] = mm
                nv_ref[...] = vv
                d_ref[...] = -ADAM_LR * ((mm / c1) / (jnp.sqrt(vv / c2) + ADAM_EPS) + ADAM_WD * w_ref[...])

    def part_spec(L):
        return pl.BlockSpec((tr, cols), lambda l, i: (jnp.where(l == L, i, jnp.where(l < L, 0, n_i - 1)), 0))

    blk = pl.BlockSpec((None, tr, cols), lambda l, i: (l, i, 0)) if stacked else pl.BlockSpec((tr, cols), lambda l, i: (i, 0))
    out = jax.ShapeDtypeStruct(w.shape, F32)
    return pl.pallas_call(
        body, grid=(n_l, n_i), in_specs=[part_spec(L) for L in range(n_l) for _ in range(n_parts)] + [blk] * 3,
        out_specs=[blk] * 4, out_shape=[out] * 4, compiler_params=_params("arbitrary", "parallel"),
        name=name)(*[p for ps in by_layer for p in ps], w, m, v)


def _position():
    return lax.axis_index("x"), lax.axis_index("y"), lax.axis_index("c")


def _other_chips():
    x, y, _ = _position()
    return [(1 - x, y), (x, 1 - y), (1 - x, 1 - y)]


def _remote(src, dst, send, recv, k, to):
    return pltpu.make_async_remote_copy(src_ref=src, dst_ref=dst, send_sem=send.at[k], recv_sem=recv.at[k],
                                        device_id=to, device_id_type=MESH)


def _ag_send(blocks, direct=False):
    n_peer = 7 if direct else 4

    def copies(ins, outs, send, recv, local, r0=0, l0=0):
        x, y, c = _position()
        me = 4 * x + 2 * y + c
        peers = [(x, y, 1 - c)] + [(px, py, c) for px, py in _other_chips()]
        if direct:
            peers += [(px, py, 1 - c) for px, py in _other_chips()]
        cps = []
        for t, (src, dst) in enumerate(zip(ins, outs)):
            cps.append(pltpu.make_async_copy(src, dst.at[me], local.at[l0 + t]))
            cps += [_remote(src, dst.at[me], send, recv, r0 + n_peer * t + k, to) for k, to in enumerate(peers)]
        return cps

    outs = tuple(jax.ShapeDtypeStruct((N_DEV,) + b.shape, b.dtype) for b in blocks)
    return _Comm(tuple(blocks), outs, {}, copies, n_peer * len(blocks), len(blocks))


def _ag_forward(bufs):
    def copies(ins, outs, send, recv, local, r0=0, l0=0):
        x, y, c = _position()
        cps = []
        for t, buf in enumerate(outs):
            for k, (px, py) in enumerate(_other_chips()):
                slot = buf.at[4 * px + 2 * py + c]
                cps.append(_remote(slot, slot, send, recv, r0 + 3 * t + k, (x, y, 1 - c)))
        return cps

    outs = tuple(jax.ShapeDtypeStruct(b.shape, b.dtype) for b in bufs)
    return _Comm(tuple(bufs), outs, {t: t for t in range(len(bufs))}, copies, 3 * len(bufs), 0)


def _rs_swap(shares):
    def copies(ins, outs, send, recv, local, r0=0, l0=0):
        x, y, c = _position()
        return [_remote(src.at[:, 1 - c], dst, send, recv, r0 + t, (x, y, 1 - c)) for t, (src, dst) in enumerate(zip(ins, outs))]

    ins = tuple(s.reshape((4, 2) + s.shape[1:]) for s in shares)
    outs = tuple(jax.ShapeDtypeStruct((4,) + s.shape[1:], s.dtype) for s in shares)
    return _Comm(ins, outs, {}, copies, len(shares), 0)


def _rs_exchange(sums):
    def copies(ins, outs, send, recv, local, r0=0, l0=0):
        _, _, c = _position()
        return [_remote(src.at[2 * px + py], dst.at[k], send, recv, r0 + 3 * t + k, (px, py, c))
                for t, (src, dst) in enumerate(zip(ins, outs)) for k, (px, py) in enumerate(_other_chips())]

    outs = tuple(jax.ShapeDtypeStruct((3,) + s.shape[1:], s.dtype) for s in sums)
    return _Comm(tuple(sums), outs, {}, copies, 3 * len(sums), 0)


def _comm_call(name, comm):
    return _call(name, lambda: None, (), [], [], [], [], (), (), comm)


def _pair_sum(name, share, got, core, tr):
    _, rows, cols = share.shape

    def body(c_ref, a_ref, b_ref, o_ref):
        o_ref[...] = (a_ref[...].astype(F32) + b_ref[...].astype(F32)).astype(o_ref.dtype)

    grid_spec = pltpu.PrefetchScalarGridSpec(
        num_scalar_prefetch=1, grid=(4, rows // tr),
        in_specs=[pl.BlockSpec((None, None, tr, cols), lambda q, i, c: (q, c[0], i, 0)),
                  pl.BlockSpec((None, tr, cols), lambda q, i, c: (q, i, 0))],
        out_specs=pl.BlockSpec((None, tr, cols), lambda q, i, c: (q, i, 0)))
    return pl.pallas_call(
        body, grid_spec=grid_spec, out_shape=jax.ShapeDtypeStruct((4, rows, cols), share.dtype),
        compiler_params=_params("parallel", "parallel"), name=name)(core, share.reshape(4, 2, rows, cols), got)


TENSORS = ("a_w_in", "a_w_out", "b_w_in", "b_w_out", "gu0", "gu1", "dn0", "dn1")
ROW_TILE = {"a_w_in": 256, "a_w_out": 128, "b_w_in": 256, "b_w_out": 128, "gu0": 256, "gu1": 256, "dn0": 176, "dn1": 176}
A_BLK = 9 * D_MODEL // N_DEV
B_BLK = 386
B_IN = 3 * D_MODEL + N_HEADS
B_IN_PAD = 3 * D_MODEL + LANES


def kernel(x, a_norm, a_w_in, a_w_out, b_norm, b_w_in, b_f, b_w_out, ffn_norm, ffn_w_gu, ffn_w_down, final_norm, loss_target, m_a_norm, m_a_w_in, m_a_w_out, m_b_norm, m_b_w_in, m_b_f, m_b_w_out, m_ffn_norm, m_ffn_w_gu, m_ffn_w_down, m_final_norm, v_a_norm, v_a_w_in, v_a_w_out, v_b_norm, v_b_w_in, v_b_f, v_b_w_out, v_ffn_norm, v_ffn_w_gu, v_ffn_w_down, v_final_norm):
    S = x.shape[1]
    xi, yi, ci = _position()
    dev = 4 * xi + 2 * yi + ci
    core = ci.reshape(1).astype(jnp.int32)
    h0, target = x.reshape(S, D_MODEL), loss_target.reshape(S, D_MODEL)

    def shards(a_in, a_out, b_in, b_out, gu, dn):
        return {"a_w_in": a_in[0], "a_w_out": a_out[0], "b_w_in": b_in[0], "b_w_out": b_out[0],
                "gu0": gu[0], "gu1": gu[1], "dn0": dn[0], "dn1": dn[1]}

    w_sh = shards(a_w_in, a_w_out, b_w_in, b_w_out, ffn_w_gu, ffn_w_down)
    m_sh = shards(m_a_w_in, m_a_w_out, m_b_w_in, m_b_w_out, m_ffn_w_gu, m_ffn_w_down)
    v_sh = shards(v_a_w_in, v_a_w_out, v_b_w_in, v_b_w_out, v_ffn_w_gu, v_ffn_w_down)
    wb = {n: w_sh[n].astype(BF16) for n in TENSORS}
    bf_pad = jnp.pad(b_f, ((0, 0), (0, LANES - N_HEADS)))
    tabs = _rope_tables(S)

    g_ain, g_aout = _comm_call("gather_a", _ag_send([wb["a_w_in"], wb["a_w_out"]]))
    dils = [dil for _, dil in DILATED_PATTERNS]
    n0_views, (g_ain, g_aout) = _rms_fwd("rms_a", h0, a_norm[0], dils, _ag_forward([g_ain, g_aout]))
    n0 = n0_views[0]
    w_a_in = g_ain.transpose(1, 0, 2).reshape(D_MODEL, 9 * D_MODEL)
    sends = [[wb["dn0"], jnp.pad(b_norm, ((0, 7), (0, 0)))], None, [wb["gu0"]]]
    qkv_a, sent = [], {}
    for g, dil in enumerate(dils):
        qkv_g, sent[g] = _a_proj("proj_a%d" % g, n0, w_a_in, g, dil, tabs, None if sends[g] is None else _ag_send(sends[g]))
        qkv_a.append(qkv_g)
    cols = [lambda r: r] * 3
    groups = [(g, dil, S // dil, qkv_a[g]) for g, (window, dil) in enumerate(DILATED_PATTERNS)]
    fwd = [_dil_fwd("dil_fwd%d" % g, view, *cols, dil, L, _ag_send([wb["b_w_in"], wb["b_w_out"]]) if g == 0 else None)
           for g, dil, L, view in groups]
    later = list(fwd[0][2:]) + [sent[2][0]] + list(sent[0])
    o_views, lse_views = _combine("dil_combine", [f[0] for f in fwd], [f[1] for f in fwd], dils)
    o_a = o_views[0]
    w_a_out = g_aout.reshape(D_MODEL, D_MODEL)
    h1, (g_bin, g_bout, g_gu0, g_dn0, g_bnorm) = _matmul("out_a", o_a, w_a_out, "nn", F32, TM, 1024, 1024, resid=h0,
                                                         comm=_ag_forward(later))

    n1 = _rms_fwd("rms_f0", h1, ffn_norm[0])
    gu0, act0 = _ffn_gu("gu_f0", n1, g_gu0)
    w_dn0 = g_dn0.reshape(D_FF, D_MODEL)
    h2 = _ffn_down("down_f0", act0, w_dn0, h1)[0]

    b_norm_full = g_bnorm[:, 0].reshape(D_MODEL)
    w_b_in = g_bin.transpose(1, 0, 2).reshape(D_MODEL, B_IN)
    w_b_gate = jnp.pad(w_b_in[:, 3 * D_MODEL:], ((0, 0), (0, LANES - N_HEADS)))
    w_b_cat = jnp.concatenate([w_b_in[:, :3 * D_MODEL], w_b_gate], axis=1)
    w_b_out = g_bout.reshape(D_MODEL, D_MODEL)
    n2 = _rms_fwd("rms_b", h2, b_norm_full)
    qkv = _matmul("proj_b", n2, w_b_in[:, :3 * D_MODEL], "nn", BF16, TM, 1024, 1024, col0_scale=SOFTMAX_SCALE)
    z = _matmul("gate_b", n2, w_b_gate, "nn", F32, TM, LANES, 1024)
    kbias = _gate_fwd("gate_cumsum", z, bf_pad)
    tf = min(S, 512)
    o_b, lse_b, g_gu1, g_dn1 = _fox_fwd("fox_fwd", qkv, kbias, tf, _ag_send([wb["gu1"], wb["dn1"]]))
    h3, (g_gu1, g_dn1) = _matmul("out_b", o_b, w_b_out, "nn", F32, TM, 1024, 1024, resid=h2, comm=_ag_forward([g_gu1, g_dn1]))

    w_dn1 = g_dn1.reshape(D_FF, D_MODEL)
    n3 = _rms_fwd("rms_f1", h3, ffn_norm[1])
    gu1, act1 = _ffn_gu("gu_f1", n3, g_gu1)
    h4 = _ffn_down("down_f1", act1, w_dn1, h3)[0]

    dh4, d_final, loss, dh4_16 = _loss_head("loss_head", h4, final_norm, target)

    share, got, sums, others = {}, {}, {}, {}

    def pair_sums(*names):
        for n in names:
            sums[n] = _pair_sum("pair_" + n, share[n], got[n], core, ROW_TILE[n])

    dgu1 = _ffn_dact("dact_f1", dh4_16, w_dn1, gu1)[0]
    share["dn1"] = _ffn_dwd("dwd_f1", act1, dh4_16)
    share["gu1"] = _ffn_dwgu("dwgu_f1", n3, dgu1)[0]
    dn3, got["gu1"], got["dn1"] = _ffn_dn("dn_f1", dgu1, g_gu1, _rs_swap([share["gu1"], share["dn1"]]))
    dh3, d_ffn1, dh3_16 = _rms_bwd("rmsb_f1", dn3, h3, ffn_norm[1], dh4)
    pair_sums("gu1", "dn1")

    do_b = _matmul("dout_b", dh3_16, w_b_out, "nt", BF16, TM, 1024, 1024)
    share["b_w_out"] = _matmul("dwout_b", o_b, dh3_16, "tn", BF16, TM, 1024, 1024).reshape(N_DEV, 128, D_MODEL)
    dq_b, dk_b, dv_b, ds_rowsum, ds_colsum, others["gu1"], others["dn1"] = _fox_bwd(
        "fox_bwd", qkv, kbias, do_b, o_b, lse_b, tf, _rs_exchange([sums["gu1"], sums["dn1"]]))
    dc = ds_rowsum[:, :N_HEADS] - ds_colsum.reshape(N_HEADS, S).T
    dz, d_bf = _gate_bwd("gate_bwd", jnp.pad(dc, ((0, 0), (0, LANES - N_HEADS))), z, bf_pad)
    dproj_b = jnp.concatenate([dq_b, dk_b, dv_b, dz.astype(BF16)], axis=1)
    dw_b_in = _matmul("dwin_b", n2, dproj_b, "tn", BF16, TM, B_IN_PAD // 5, 1024)
    dn2 = _matmul("dn_b", dproj_b, w_b_cat, "nt", F32, TM, 1024, B_IN_PAD // 5)
    dh2, d_bnorm, dh2_16 = _rms_bwd("rmsb_b", dn2, h2, b_norm_full, dh3)
    share["b_w_in"] = dw_b_in[:, :B_IN].reshape(D_MODEL, N_DEV, B_BLK).transpose(1, 0, 2)

    dgu0, got["b_w_in"], got["b_w_out"] = _ffn_dact("dact_f0", dh2_16, w_dn0, gu0, _rs_swap([share["b_w_in"], share["b_w_out"]]))
    share["dn0"] = _ffn_dwd("dwd_f0", act0, dh2_16)
    pair_sums("b_w_in", "b_w_out")
    share["gu0"], others["b_w_in"], others["b_w_out"] = _ffn_dwgu(
        "dwgu_f0", n1, dgu0, _rs_exchange([sums["b_w_in"], sums["b_w_out"]]))
    dn1, got["gu0"], got["dn0"] = _ffn_dn("dn_f0", dgu0, g_gu0, _rs_swap([share["gu0"], share["dn0"]]))
    dh1, d_ffn0, dh1_16 = _rms_bwd("rmsb_f0", dn1, h1, ffn_norm[0], dh2)
    pair_sums("gu0", "dn0")

    do_views = _matmul_nt_views("dout_a", dh1_16, w_a_out, dils)
    share["a_w_out"] = _matmul("dwout_a", o_a, dh1_16, "tn", BF16, TM, 1024, 1024).reshape(N_DEV, 128, D_MODEL)
    pieces = []
    for g, dil, L, view in groups:
        rot = tuple(tb.reshape(L, dil * LANES) for tb in tabs)
        res = _dil_bwd("dil_bwd%d" % g, view, do_views[g], o_views[g], lse_views[g], rot, dil, L,
                       _rs_exchange([sums["gu0"], sums["dn0"]]) if g == 0 else None)
        pieces.append(res[:3])
        if g == 0:
            others["gu0"], others["dn0"] = res[3:]
    dws = [_a_dw("dwin_a%d" % g, n0_views[g], pieces[g], dil, None)[0] for g, dil in enumerate(dils)]
    share["a_w_in"] = jnp.concatenate(dws, axis=1).reshape(D_MODEL, N_DEV, A_BLK).transpose(1, 0, 2)
    got["a_w_in"], got["a_w_out"] = _comm_call("swap_a", _rs_swap([share["a_w_in"], share["a_w_out"]]))
    pair_sums("a_w_in", "a_w_out")
    dn0, others["a_w_in"], others["a_w_out"] = _a_dn("dn_a", [p for ps in pieces for p in ps], dils, w_a_in,
                                                     _rs_exchange([sums["a_w_in"], sums["a_w_out"]]))
    dx, d_anorm = _rms_bwd("rmsb_a", dn0, h0, a_norm[0], dh1, copy16=False)

    misc = jnp.concatenate([d_bf[:, :N_HEADS], loss[:, :1], jnp.zeros((1, D_MODEL - N_HEADS - 1), F32)], axis=1)
    small = jnp.concatenate([d_anorm, d_ffn0, d_ffn1, d_final, d_bnorm, misc, jnp.zeros((2, D_MODEL), F32)], axis=0)
    small_all, = _comm_call("gather_small", _ag_send([small], direct=True))

    def grad_parts(n):
        return [lax.dynamic_index_in_dim(sums[n], 2 * xi + yi, axis=0, keepdims=False)] + [others[n][k] for k in range(3)]

    outs = {n: _adamw("adamw_" + n, grad_parts(n), w_sh[n], m_sh[n], v_sh[n], ROW_TILE[n])
            for n in ("a_w_in", "a_w_out", "b_w_in", "b_w_out")}
    outs["gu"] = _adamw("adamw_gu", [grad_parts("gu0"), grad_parts("gu1")], ffn_w_gu, m_ffn_w_gu, v_ffn_w_gu, ROW_TILE["gu0"])
    outs["dn"] = _adamw("adamw_dn", [grad_parts("dn0"), grad_parts("dn1")], ffn_w_down, m_ffn_w_down, v_ffn_w_down, ROW_TILE["dn0"])

    pad_vec = lambda a: jnp.pad(a, ((0, 0), (0, D_MODEL - a.shape[1])))

    def small_pack(an, fn, fin, bf):
        return jnp.concatenate([an, fn, fin.reshape(1, D_MODEL), jnp.zeros((1, D_MODEL), F32), pad_vec(bf),
                                jnp.zeros((2, D_MODEL), F32)], axis=0)

    sg, sd, sm, sv = _adamw("adamw_small", [small_all[d] for d in range(N_DEV)], small_pack(a_norm, ffn_norm, final_norm, b_f),
                            small_pack(m_a_norm, m_ffn_norm, m_final_norm, m_b_f),
                            small_pack(v_a_norm, v_ffn_norm, v_final_norm, v_b_f), 8)
    g_bn = lax.dynamic_slice(sg[4:5], (0, dev * LANES), (1, LANES))
    bn = _adamw("adamw_b_norm", [g_bn], b_norm, m_b_norm, v_b_norm, 1)

    def tree(i):
        full = lambda name, ref: outs[name][i].reshape(ref.shape)
        sml = (sg, sd, sm, sv)[i]
        return dict(
            a_norm=sml[0:1], a_w_in=full("a_w_in", a_w_in), a_w_out=full("a_w_out", a_w_out), b_norm=bn[i],
            b_w_in=full("b_w_in", b_w_in), b_f=sml[5:6, :N_HEADS], b_w_out=full("b_w_out", b_w_out), ffn_norm=sml[1:3],
            ffn_w_gu=outs["gu"][i], ffn_w_down=outs["dn"][i], final_norm=sml[3])

    order = ("a_norm", "a_w_in", "a_w_out", "b_norm", "b_w_in", "b_f", "b_w_out", "ffn_norm", "ffn_w_gu", "ffn_w_down", "final_norm")
    result = [sg[5, N_HEADS], dx.reshape(x.shape)]
    for i in range(4):
        t = tree(i)
        result += [t[n] for n in order]
    return tuple(result)
```

```python
import functools
from typing import Callable, NamedTuple

import jax
import jax.numpy as jnp
from jax import lax
from jax.experimental import pallas as pl
from jax.experimental.pallas import tpu as pltpu

F32 = jnp.float32
BF16 = jnp.bfloat16

D_MODEL = 1024
N_HEADS = 16
HEAD_DIM = 64
N_PAIRS = N_HEADS // 2
LANES = 128
DILATED_PATTERNS = ((128, 1), (512, 4), (2048, 16))
BAND_STEPS = 128
ROT_DIM = HEAD_DIM // 4
ROPE_THETA = 500000.0
D_FF = 2816
RMS_EPS = 1e-6
NEG_INF = -1e30
SOFTMAX_SCALE = HEAD_DIM ** -0.5
N_DEV = 8
FF_BLK = 2 * D_FF // N_DEV
ADAM_LR, ADAM_B1, ADAM_B2, ADAM_EPS, ADAM_WD, ADAM_STEP = 0.001, 0.9, 0.999, 1e-08, 0.01, 10
VMEM_LIMIT = 52 * 1024 * 1024
FOX_BWD_VMEM = 60 * 1024 * 1024
TM = 1024
MESH = pl.DeviceIdType.MESH

NN = (((1,), (0,)), ((), ()))
NT = (((1,), (1,)), ((), ()))
TN = (((0,), (0,)), ((), ()))


def _params(*sem, vmem=VMEM_LIMIT):
    return pltpu.CompilerParams(dimension_semantics=sem, vmem_limit_bytes=vmem)


def _dot(a, b, dims):
    return lax.dot_general(a, b, dims, preferred_element_type=F32)


class _Comm(NamedTuple):
    ins: tuple
    outs: tuple
    aliases: dict
    copies: Callable
    n_remote: int
    n_local: int


def _call(name, body, grid, in_specs, out_specs, out_shape, scratch, args, sem, comm=None, vmem=VMEM_LIMIT):
    if comm is None:
        return pl.pallas_call(body, grid=grid, in_specs=in_specs, out_specs=out_specs, out_shape=out_shape,
                              scratch_shapes=scratch, compiler_params=_params(*sem, vmem=vmem), name=name)(*args)
    n_in, n_out = len(in_specs), len(out_specs)
    n_ci, n_co = len(comm.ins), len(comm.outs)
    o0 = n_in + n_ci

    def hosted(*refs):
        c_ins, c_outs = refs[n_in:o0], refs[o0 + n_out:o0 + n_out + n_co]
        sems = refs[-3:]

        def start():
            for cp in comm.copies(c_ins, c_outs, *sems):
                cp.start()

        def wait():
            for cp in comm.copies(c_ins, c_outs, *sems):
                cp.wait()

        if not grid:
            start()
            body()
            wait()
            return
        ids = [pl.program_id(ax) for ax in range(len(grid))]
        pl.when(functools.reduce(jnp.logical_and, [i == 0 for i in ids]))(start)
        body(*refs[:n_in], *refs[o0:o0 + n_out], *refs[o0 + n_out + n_co:-3])
        pl.when(functools.reduce(jnp.logical_and, [i == g - 1 for i, g in zip(ids, grid)]))(wait)

    hbm = pl.BlockSpec(memory_space=pltpu.HBM)
    dma = pltpu.SemaphoreType.DMA
    return pl.pallas_call(
        hosted, grid=grid, in_specs=[*in_specs, *[hbm] * n_ci], out_specs=[*out_specs, *[hbm] * n_co],
        out_shape=[*out_shape, *comm.outs], input_output_aliases={n_in + i: n_out + o for i, o in comm.aliases.items()},
        scratch_shapes=[*scratch, dma((comm.n_remote,)), dma((comm.n_remote,)), dma((max(comm.n_local, 1),))],
        compiler_params=_params(*["arbitrary"] * len(grid), vmem=vmem), name=name)(*args, *comm.ins)


def _mm_call(name, grid, a, a_spec, b, b_spec, dims, out_shapes, out_specs, acc_shape, epilogue=None,
             extras=(), extra_specs=(), col_axis=1, comm=None):
    nk = grid[2]
    n_extra = len(extras)
    n_out = len(out_shapes)

    def finish(res, ex, outs, j):
        if epilogue is None:
            outs[0][...] = res.astype(outs[0].dtype)
        else:
            epilogue(res, ex, outs, j)

    def body(*refs):
        a_ref, b_ref = refs[0], refs[1]
        ex = refs[2:2 + n_extra]
        outs = refs[2 + n_extra:2 + n_extra + n_out]
        j, k = pl.program_id(col_axis), pl.program_id(2)
        part = _dot(a_ref[...].astype(BF16), b_ref[...].astype(BF16), dims)
        if nk == 1:
            finish(part, ex, outs, j)
            return
        acc = refs[-1]

        @pl.when(k == 0)
        def _():
            acc[...] = part

        @pl.when((k > 0) & (k < nk - 1))
        def _():
            acc[...] += part

        @pl.when(k == nk - 1)
        def _():
            finish(acc[...] + part, ex, outs, j)

    return _call(name, body, grid, [a_spec, b_spec, *extra_specs], out_specs, out_shapes,
                 [] if nk == 1 else [pltpu.VMEM(acc_shape, F32)], (a, b, *extras), ("parallel", "parallel", "arbitrary"), comm)


def _matmul(name, a, b, mode, out_dtype, tm, tn, tk, resid=None, col0_scale=None, comm=None):
    if mode == "nn":
        (M, K), N = a.shape, b.shape[1]
        a_spec = pl.BlockSpec((tm, tk), lambda j, i, k: (i, k))
        b_spec = pl.BlockSpec((tk, tn), lambda j, i, k: (k, j))
        dims = NN
    elif mode == "nt":
        (M, K), N = a.shape, b.shape[0]
        a_spec = pl.BlockSpec((tm, tk), lambda j, i, k: (i, k))
        b_spec = pl.BlockSpec((tn, tk), lambda j, i, k: (j, k))
        dims = NT
    else:
        (K, M), N = a.shape, b.shape[1]
        a_spec = pl.BlockSpec((tk, tm), lambda j, i, k: (k, i))
        b_spec = pl.BlockSpec((tk, tn), lambda j, i, k: (k, j))
        dims = TN
    assert M % tm == 0 and N % tn == 0 and K % tk == 0, (name, M, N, K, tm, tn, tk)
    o_spec = pl.BlockSpec((tm, tn), lambda j, i, k: (i, j))
    extras, extra_specs, epilogue = (), (), None
    if resid is not None:
        extras, extra_specs = (resid,), (o_spec,)

        def epilogue(acc, ex, outs, j):
            outs[0][...] = (acc + ex[0][...]).astype(outs[0].dtype)

    elif col0_scale is not None:

        def epilogue(acc, ex, outs, j):
            outs[0][...] = (acc * jnp.where(j == 0, col0_scale, 1.0)).astype(outs[0].dtype)

    res = _mm_call(name, (N // tn, M // tm, K // tk), a, a_spec, b, b_spec, dims, [jax.ShapeDtypeStruct((M, N), out_dtype)],
                   [o_spec], (tm, tn), epilogue, extras, extra_specs, col_axis=0, comm=comm)
    return res[0] if comm is None else (res[0], res[1:])


def _rms_fwd(name, h, gain, dils=(1,), comm=None, tm=512):
    S, D = h.shape

    def body(h_ref, g_ref, *rest):
        x = h_ref[...]
        rstd = lax.rsqrt(jnp.mean(x * x, axis=-1, keepdims=True) + RMS_EPS)
        y = x * rstd * g_ref[...]
        _write_views([y[:, b * LANES:(b + 1) * LANES] for b in range(N_PAIRS)], rest[-1], rest[:-1], dils, tm)

    res = _call(name, body, (S // tm,), [pl.BlockSpec((tm, D), lambda i: (i, 0)), pl.BlockSpec((1, D), lambda i: (0, 0))],
                [_view_spec(tm, R) for R in dils], [jax.ShapeDtypeStruct((S // R, R * D), BF16) for R in dils],
                [pltpu.VMEM((N_PAIRS, tm, LANES), F32)], (h, gain.reshape(1, D)), ("parallel",), comm)
    views = res[0] if len(dils) == 1 else res[:len(dils)]
    return views if comm is None else (views, res[len(dils):])


def _rms_bwd(name, dn, h, gain, dres, copy16=True, tm=512):
    S, D = h.shape

    def body(dn_ref, h_ref, g_ref, r_ref, dh_ref, dg_ref, *dh16_ref):
        x = h_ref[...]
        rstd = lax.rsqrt(jnp.mean(x * x, axis=-1, keepdims=True) + RMS_EPS)
        xhat = x * rstd
        d = dn_ref[...]
        dxhat = d * g_ref[...]
        dh = rstd * (dxhat - xhat * jnp.mean(dxhat * xhat, axis=-1, keepdims=True)) + r_ref[...]
        dh_ref[...] = dh
        if copy16:
            dh16_ref[0][...] = dh.astype(BF16)

        @pl.when(pl.program_id(0) == 0)
        def _():
            dg_ref[...] = jnp.zeros_like(dg_ref)

        dg_ref[...] += jnp.sum(d * xhat, axis=0, keepdims=True)

    row = pl.BlockSpec((tm, D), lambda i: (i, 0))
    vec = pl.BlockSpec((1, D), lambda i: (0, 0))
    return pl.pallas_call(
        body, grid=(S // tm,), in_specs=[row, row, vec, row], out_specs=[row, vec] + [row] * copy16,
        out_shape=[jax.ShapeDtypeStruct((S, D), F32), jax.ShapeDtypeStruct((1, D), F32)] + [jax.ShapeDtypeStruct((S, D), BF16)] * copy16,
        compiler_params=_params("arbitrary"), name=name)(dn, h, gain.reshape(1, D), dres)


def _loss_head(name, h, gain, target, tm=512):
    S, D = h.shape

    def body(h_ref, g_ref, t_ref, dh_ref, dg_ref, loss_ref, dh16_ref):
        x = h_ref[...]
        rstd = lax.rsqrt(jnp.mean(x * x, axis=-1, keepdims=True) + RMS_EPS)
        xhat = x * rstd
        err = xhat * g_ref[...] - t_ref[...]
        dy = err * (1.0 / D)
        dxhat = dy * g_ref[...]
        dh = rstd * (dxhat - xhat * jnp.mean(dxhat * xhat, axis=-1, keepdims=True))
        dh_ref[...] = dh
        dh16_ref[...] = dh.astype(BF16)

        @pl.when(pl.program_id(0) == 0)
        def _():
            dg_ref[...] = jnp.zeros_like(dg_ref)
            loss_ref[...] = jnp.zeros_like(loss_ref)

        dg_ref[...] += jnp.sum(dy * xhat, axis=0, keepdims=True)
        part = 0.5 * jnp.sum(jnp.mean(err * err, axis=-1, keepdims=True), axis=0, keepdims=True)
        loss_ref[...] += jnp.broadcast_to(part, loss_ref.shape)

    row = pl.BlockSpec((tm, D), lambda i: (i, 0))
    vec = pl.BlockSpec((1, D), lambda i: (0, 0))
    return pl.pallas_call(
        body, grid=(S // tm,), in_specs=[row, vec, row], out_specs=[row, vec, pl.BlockSpec((1, LANES), lambda i: (0, 0)), row],
        out_shape=[jax.ShapeDtypeStruct((S, D), F32), jax.ShapeDtypeStruct((1, D), F32),
                   jax.ShapeDtypeStruct((1, LANES), F32), jax.ShapeDtypeStruct((S, D), BF16)],
        compiler_params=_params("arbitrary"), name=name)(h, gain.reshape(1, D), target)


def _rope_tables(S):
    half = ROT_DIM // 2
    inv_freq = ROPE_THETA ** (-jnp.arange(half, dtype=F32) * 2.0 / ROT_DIM)
    inv_head = jnp.concatenate([inv_freq, inv_freq, jnp.zeros((HEAD_DIM - ROT_DIM,), F32)])
    ang = jnp.arange(S, dtype=F32)[:, None] * jnp.concatenate([inv_head, inv_head])[None, :]
    lane = (jnp.arange(LANES) % HEAD_DIM)[None, :]
    cos, sin = jnp.cos(ang), jnp.sin(ang)
    c = jnp.where(lane < ROT_DIM, cos, 1.0)
    sa = jnp.where(lane < half, -sin, 0.0)
    sb = jnp.where((lane >= half) & (lane < ROT_DIM), sin, 0.0)
    return c, sa, sb


def _rotate(x, c, sa, sb, sign):
    return x * c + sign * (pltpu.roll(x, LANES - ROT_DIM // 2, 1) * sa + pltpu.roll(x, ROT_DIM // 2, 1) * sb)


def _stage_chunks(scr, chunks):
    for c, x in enumerate(chunks):
        scr[c] = x


def _strided_rows(scr, c, r, n, R):
    return scr.at[c][pl.ds(r, n, stride=R), :]


def _a_proj(name, n, w, g, R, tabs, comm, tm=1024):
    S, D = n.shape
    n_i = S // tm
    n_out = 3

    def body(n_ref, w_ref, c_ref, sa_ref, sb_ref, *rest):
        outs, scr = rest[:n_out], rest[n_out]
        j = pl.program_id(0)
        acc = _dot(n_ref[...], w_ref[...], NN)
        c, sa, sb = c_ref[...], sa_ref[...], sb_ref[...]
        for J in range(n_out):
            kind = J

            @pl.when(j == J)
            def _(J=J, kind=kind):
                chunks = [acc[:, b * LANES:(b + 1) * LANES] for b in range(N_PAIRS)]
                if kind < 2:
                    chunks = [_rotate(x, c, sa, sb, 1.0) * (SOFTMAX_SCALE if kind == 0 else 1.0) for x in chunks]
                if R == 1:
                    for b, x in enumerate(chunks):
                        outs[J][:, b * LANES:(b + 1) * LANES] = x.astype(BF16)
                    return
                _stage_chunks(scr, chunks)
                for r in range(R):
                    for b in range(N_PAIRS):
                        col = r * D_MODEL + b * LANES
                        outs[J][:, col:col + LANES] = _strided_rows(scr, b, r, tm // R, R).astype(BF16)

    def out_spec(J):
        return pl.BlockSpec((tm // R, R * D_MODEL), lambda j, i: (jnp.where(j == J, i, jnp.where(j < J, 0, n_i - 1)), 0))

    tab = pl.BlockSpec((tm, LANES), lambda j, i: (i, 0))
    res = _call(name, body, (n_out, n_i),
                [pl.BlockSpec((tm, D), lambda j, i: (i, 0)), pl.BlockSpec((D, D_MODEL), lambda j, i: (0, 3 * g + j)), tab, tab, tab],
                [out_spec(J) for J in range(n_out)], [jax.ShapeDtypeStruct((S // R, R * D_MODEL), BF16)] * n_out,
                [pltpu.VMEM((N_PAIRS, tm, LANES), F32)], (n, w, *tabs), ("arbitrary", "arbitrary"), comm)
    return res[:n_out], res[n_out:]


def _unstride(src_chunk, R, tok, rows):
    for r in range(R):
        for b in range(N_PAIRS):
            tok.at[b][pl.ds(r, rows // R, stride=R), :] = src_chunk(r, b).astype(F32)


def _by_residue(ref, R):
    return ref[...] if R == 1 else jnp.concatenate([ref[:, r * D_MODEL:(r + 1) * D_MODEL] for r in range(R)], axis=0)


def _a_dw(name, n_view, pieces, R, comm, tk=1024):
    D = D_MODEL
    S = n_view.shape[0] * R
    n_k = S // tk
    n_p = len(pieces)

    def body(n_ref, *rest):
        p_refs, o_ref, acc = rest[:n_p], rest[n_p], rest[n_p + 1]
        j, k = pl.program_id(0), pl.program_id(1)
        for J in range(n_p):

            @pl.when(j == J)
            def _(J=J):
                part = _dot(_by_residue(n_ref, R), _by_residue(p_refs[J], R), TN)

                @pl.when(k == 0)
                def _():
                    acc[...] = part

                @pl.when((k > 0) & (k < n_k - 1))
                def _():
                    acc[...] += part

                @pl.when(k == n_k - 1)
                def _():
                    o_ref[...] = (acc[...] + part).astype(BF16)

    def piece_spec(J):
        return pl.BlockSpec((tk // R, R * D_MODEL), lambda j, k: (jnp.where(j == J, k, jnp.where(j < J, 0, n_k - 1)), 0))

    return _call(name, body, (n_p, n_k), [pl.BlockSpec((tk // R, R * D_MODEL), lambda j, k: (k, 0))] + [piece_spec(J) for J in range(n_p)],
                 [pl.BlockSpec((D, D_MODEL), lambda j, k: (0, j))], [jax.ShapeDtypeStruct((D, n_p * D_MODEL), BF16)],
                 [pltpu.VMEM((D, D_MODEL), F32)], (n_view, *pieces), ("arbitrary", "arbitrary"), comm)


def _a_dn(name, pieces, dils, w, comm, tm=512):
    D = w.shape[0]
    S = pieces[0].shape[0] * dils[0]
    n_p = len(pieces)

    def body(*refs):
        p_refs, w_ref, o_ref, acc, part_acc, tok = refs[:n_p], refs[n_p], refs[n_p + 1], refs[n_p + 2], refs[n_p + 3], refs[n_p + 4]
        j = pl.program_id(1)
        for J in range(n_p):

            @pl.when(j == J)
            def _(J=J):
                R, t = dils[J // 3], J % 3
                part = _dot(_by_residue(p_refs[J], R), w_ref[...], NT)
                if R == 1:
                    if J == 0:
                        acc[...] = part
                    else:
                        acc[...] += part
                    return
                if t == 0:
                    part_acc[...] = part
                    return
                if t == 1:
                    part_acc[...] += part
                    return
                n = tm // R
                _unstride(lambda r, b: part_acc[r * n:(r + 1) * n, b * LANES:(b + 1) * LANES]
                          + part[r * n:(r + 1) * n, b * LANES:(b + 1) * LANES], R, tok, tm)
                total = acc[...] + jnp.concatenate([tok[b] for b in range(N_PAIRS)], axis=1)
                if J == n_p - 1:
                    o_ref[...] = total
                else:
                    acc[...] = total

    specs = [pl.BlockSpec((tm // dils[J // 3], dils[J // 3] * D_MODEL), lambda i, j: (i, 0)) for J in range(n_p)]
    return _call(name, body, (S // tm, n_p), specs + [pl.BlockSpec((D, D_MODEL), lambda i, j: (0, j))],
                 [pl.BlockSpec((tm, D), lambda i, j: (i, 0))], [jax.ShapeDtypeStruct((S, D), F32)],
                 [pltpu.VMEM((tm, D), F32), pltpu.VMEM((tm, D), F32), pltpu.VMEM((N_PAIRS, tm, LANES), F32)], (*pieces, w),
                 ("arbitrary", "arbitrary"), comm)


def _lo_lanes():
    return lax.broadcasted_iota(jnp.int32, (1, LANES), 1) < HEAD_DIM


def _rep_rows(x2, lo):
    sw = pltpu.roll(x2, HEAD_DIM, 1)
    return jnp.where(lo, x2, sw), jnp.where(lo, sw, x2)


def _pair_cols(h):
    return slice((h // 2) * LANES, (h // 2 + 1) * LANES)


def _head_lanes(lo, h):
    return lo if h % 2 == 0 else jnp.logical_not(lo)


def _band_masks(t, first):
    ri = lax.broadcasted_iota(jnp.int32, (t, t), 0)
    ci = lax.broadcasted_iota(jnp.int32, (t, t), 1)
    neg_prev = jnp.where((ci >= ri) & jnp.logical_not(first), 0.0, NEG_INF)
    neg_cur = jnp.where(ci <= ri, 0.0, NEG_INF)
    return neg_prev, neg_cur


def _dil_specs(L, R, t, qcol, kcol, vcol):
    W = D_MODEL
    prev = lambda qi: jnp.maximum(qi - 1, 0)
    return dict(
        q=pl.BlockSpec((t, W), lambda r, qi: (qi, qcol(r))),
        kp=pl.BlockSpec((t, W), lambda r, qi: (prev(qi), kcol(r))), kc=pl.BlockSpec((t, W), lambda r, qi: (qi, kcol(r))),
        vp=pl.BlockSpec((t, W), lambda r, qi: (prev(qi), vcol(r))), vc=pl.BlockSpec((t, W), lambda r, qi: (qi, vcol(r))),
        own=pl.BlockSpec((t, W), lambda r, qi: (qi, r)), tab=pl.BlockSpec((t, LANES), lambda r, qi: (qi, r)))


def _dil_fwd(name, x, qcol, kcol, vcol, R, L, comm=None):
    t = BAND_STEPS
    W = D_MODEL
    sp = _dil_specs(L, R, t, qcol, kcol, vcol)

    def body(q_ref, kp_ref, kc_ref, vp_ref, vc_ref, o_ref, lse_ref):
        lo = _lo_lanes()
        neg_p, neg_c = _band_masks(t, pl.program_id(1) == 0)
        s_p, s_c = [], []
        for h in range(N_HEADS):
            cols = _pair_cols(h)
            qh = jnp.where(_head_lanes(lo, h), q_ref[:, cols], 0)
            s_p.append(_dot(qh, kp_ref[:, cols], NT))
            s_c.append(_dot(qh, kc_ref[:, cols], NT))
        s_p = jnp.stack(s_p) + neg_p[None]
        s_c = jnp.stack(s_c) + neg_c[None]
        m = jnp.maximum(jnp.max(s_p, axis=2, keepdims=True), jnp.max(s_c, axis=2, keepdims=True))
        p_p, p_c = jnp.exp(s_p - m), jnp.exp(s_c - m)
        l = jnp.sum(p_p, axis=2, keepdims=True) + jnp.sum(p_c, axis=2, keepdims=True)
        inv, lse = 1.0 / l, m + jnp.log(l)
        p_p, p_c = p_p.astype(BF16), p_c.astype(BF16)
        for p in range(N_PAIRS):
            cols = _pair_cols(2 * p)
            o2 = jnp.zeros((t, LANES), F32)
            for h in (2 * p, 2 * p + 1):
                hm = _head_lanes(lo, h)
                pv = _dot(p_p[h], jnp.where(hm, vp_ref[:, cols], 0), NN) + _dot(p_c[h], jnp.where(hm, vc_ref[:, cols], 0), NN)
                o2 = o2 + pv * inv[h]
            o_ref[:, cols] = o2
            lse_ref[:, cols] = jnp.where(lo, lse[2 * p], lse[2 * p + 1])

    return _call(name, body, (R, L // t), [sp["q"], sp["kp"], sp["kc"], sp["vp"], sp["vc"]], [sp["own"], sp["own"]],
                 [jax.ShapeDtypeStruct((L, R * W), F32), jax.ShapeDtypeStruct((L, R * W), F32)], [],
                 (x[0], x[1], x[1], x[2], x[2]), ("parallel", "parallel"), comm)


def _dil_scores(lo, q_ref, do_ref, o_ref, lse_ref, kv_refs):
    s = [[] for _ in kv_refs]
    dp = [[] for _ in kv_refs]
    lse, d = [], []
    for h in range(N_HEADS):
        cols = _pair_cols(h)
        hm = _head_lanes(lo, h)
        qh, doh = jnp.where(hm, q_ref[:, cols], 0), jnp.where(hm, do_ref[:, cols], 0)
        for i, (k_ref, v_ref) in enumerate(kv_refs):
            s[i].append(_dot(qh, k_ref[:, cols], NT))
            dp[i].append(_dot(doh, v_ref[:, cols], NT))
        lse.append(_rep_rows(lse_ref[:, cols], lo)[h % 2])
        dd = do_ref[:, cols].astype(F32) * o_ref[:, cols].astype(F32)
        d.append(jnp.sum(jnp.where(hm, dd, 0.0), axis=1, keepdims=True))
    return (*[jnp.stack(x) for x in s], *[jnp.stack(x) for x in dp], jnp.stack(lse), jnp.stack(d))


def _dil_bwd(name, x, do, o, lse, tabs, R, L, comm=None):
    t = BAND_STEPS
    W = D_MODEL
    nq = L // t
    qb = lambda step: nq - 1 - step
    kb = lambda step: jnp.maximum(qb(step) - 1, 0)
    at = lambda f, width: pl.BlockSpec((t, width), lambda r, step: (f(step), r))

    def body(q_ref, kp_ref, kc_ref, vp_ref, vc_ref, do_ref, o_ref, lse_ref, c_ref, sa_ref, sb_ref, dq_ref, dk_ref, dv_ref,
             dk_scr, dv_scr):
        qi = nq - 1 - pl.program_id(1)
        lo = _lo_lanes()
        unrotate = lambda x: _rotate(x, c_ref[...], sa_ref[...], sb_ref[...], -1.0).astype(BF16)

        @pl.when(qi == nq - 1)
        def _():
            dk_scr[...] = jnp.zeros_like(dk_scr)
            dv_scr[...] = jnp.zeros_like(dv_scr)

        neg_p, neg_c = _band_masks(t, qi == 0)
        s_p, s_c, dp_p, dp_c, lse_h, d = _dil_scores(lo, q_ref, do_ref, o_ref, lse_ref, ((kp_ref, vp_ref), (kc_ref, vc_ref)))
        p_p, p_c = jnp.exp(s_p + neg_p[None] - lse_h), jnp.exp(s_c + neg_c[None] - lse_h)
        ds_p, ds_c = (p_p * (dp_p - d)).astype(BF16), (p_c * (dp_c - d)).astype(BF16)
        p_p, p_c = p_p.astype(BF16), p_c.astype(BF16)
        for p in range(N_PAIRS):
            cols = _pair_cols(2 * p)
            dq2 = jnp.zeros((t, LANES), F32)
            dk_cur, dv_cur = dk_scr[:, cols], dv_scr[:, cols]
            dk_prev, dv_prev = jnp.zeros((t, LANES), F32), jnp.zeros((t, LANES), F32)
            for h in (2 * p, 2 * p + 1):
                hm = _head_lanes(lo, h)
                qh, doh = jnp.where(hm, q_ref[:, cols], 0), jnp.where(hm, do_ref[:, cols], 0)
                dq2 = dq2 + _dot(ds_p[h], jnp.where(hm, kp_ref[:, cols], 0), NN) + _dot(ds_c[h], jnp.where(hm, kc_ref[:, cols], 0), NN)
                dk_prev, dv_prev = dk_prev + _dot(ds_p[h], qh, TN), dv_prev + _dot(p_p[h], doh, TN)
                dk_cur, dv_cur = dk_cur + _dot(ds_c[h], qh, TN), dv_cur + _dot(p_c[h], doh, TN)
            dq_ref[:, cols] = unrotate(dq2 * SOFTMAX_SCALE)
            dk_ref[:, cols] = unrotate(dk_cur)
            dv_ref[:, cols] = dv_cur.astype(BF16)
            dk_scr[:, cols] = dk_prev
            dv_scr[:, cols] = dv_prev

    wide = jax.ShapeDtypeStruct((L, R * W), BF16)
    return _call(name, body, (R, nq),
                 [at(qb, W), at(kb, W), at(qb, W), at(kb, W), at(qb, W), at(qb, W), at(qb, W), at(qb, W),
                  at(qb, LANES), at(qb, LANES), at(qb, LANES)],
                 [at(qb, W), at(qb, W), at(qb, W)], [wide, wide, wide], [pltpu.VMEM((t, W), F32), pltpu.VMEM((t, W), F32)],
                 (x[0], x[1], x[1], x[2], x[2], do, o, lse, *tabs), ("parallel", "arbitrary"), comm)


def _fox_operands(q2, k2, kb2, lo, hh):
    lane = lax.broadcasted_iota(jnp.int32, (1, LANES), 1)
    if hh == 0:
        ones = ((lane >= HEAD_DIM) & (lane < HEAD_DIM + 3)).astype(BF16)
        return jnp.where(lo, q2, ones), jnp.where(lo, k2, kb2)
    ones = (lane < 3).astype(BF16)
    return jnp.where(lo, ones, q2), jnp.where(lo, kb2, k2)


def _causal_neg(t):
    ri = lax.broadcasted_iota(jnp.int32, (t, t), 0)
    ci = lax.broadcasted_iota(jnp.int32, (t, t), 1)
    return jnp.where(ci <= ri, 0.0, NEG_INF)


def _fox_fwd(name, qkv, kbias, t, comm=None):
    S = qkv.shape[0]
    W = D_MODEL
    nq = S // t
    rep = t // LANES

    def body(q_ref, k_ref, v_ref, kb_ref, o_ref, lse_ref, m_scr, l_scr, acc_scr):
        qi, j = pl.program_id(0), pl.program_id(1)
        lo = _lo_lanes()

        @pl.when(j == 0)
        def _():
            m_scr[...] = jnp.full_like(m_scr, NEG_INF)
            l_scr[...] = jnp.zeros_like(l_scr)
            acc_scr[...] = jnp.zeros_like(acc_scr)

        def step(masked):
            neg = _causal_neg(t) if masked else None

            def pair(p, carry):
                cs = pl.ds(pl.multiple_of(p * LANES, LANES), LANES)
                q2, k2, v2, kb2 = q_ref[:, cs], k_ref[:, cs], v_ref[:, cs], kb_ref[:, cs]
                pvs, alphas = [], []
                for hh in range(2):
                    hm = lo if hh == 0 else jnp.logical_not(lo)
                    qh, kh = _fox_operands(q2, k2, kb2, lo, hh)
                    s = _dot(qh, kh, NT)
                    if masked:
                        s = s + neg
                    h = 2 * p + hh
                    m_prev = m_scr[h]
                    m_new = jnp.maximum(m_prev, jnp.max(s, axis=1, keepdims=True))
                    pe = jnp.exp(s - jnp.tile(m_new, (1, rep)))
                    alpha = jnp.exp(m_prev - m_new)
                    l_scr[h] = alpha * l_scr[h] + jnp.sum(pe, axis=1, keepdims=True)
                    m_scr[h] = m_new
                    pvs.append(_dot(pe.astype(BF16), jnp.where(hm, v2, 0), NN))
                    alphas.append(alpha)
                acc_scr[:, cs] = acc_scr[:, cs] * jnp.where(lo, alphas[0], alphas[1]) + pvs[0] + pvs[1]
                return carry

            lax.fori_loop(0, N_PAIRS, pair, 0, unroll=4)

        @pl.when(j < qi)
        def _():
            step(False)

        @pl.when(j == qi)
        def _():
            step(True)

        @pl.when(j == nq - 1)
        def _():
            for p in range(N_PAIRS):
                cols = slice(p * LANES, (p + 1) * LANES)
                l2 = jnp.where(lo, l_scr[2 * p], l_scr[2 * p + 1])
                m2 = jnp.where(lo, m_scr[2 * p], m_scr[2 * p + 1])
                o_ref[:, cols] = (acc_scr[:, cols] / l2).astype(BF16)
                lse_ref[:, cols] = m2 + jnp.log(l2)

    kv = lambda col: pl.BlockSpec((t, W), lambda qi, j: (jnp.minimum(j, qi), col))
    own = pl.BlockSpec((t, W), lambda qi, j: (qi, 0))
    return _call(name, body, (nq, nq), [own, kv(1), kv(2), kv(0)], [own, own],
                 [jax.ShapeDtypeStruct((S, W), BF16), jax.ShapeDtypeStruct((S, W), F32)],
                 [pltpu.VMEM((N_HEADS, t, LANES), F32), pltpu.VMEM((N_HEADS, t, LANES), F32), pltpu.VMEM((t, W), F32)],
                 (qkv, qkv, qkv, kbias), ("parallel", "arbitrary"), comm)


def _fox_head_grads(qh, kh, v2, doh, neg, lse_h, d_h, rep):
    s = _dot(qh, kh, NT)
    if neg is not None:
        s = s + neg
    p = jnp.exp(s - jnp.tile(lse_h, (1, rep)))
    return p, p * (_dot(doh, v2, NT) - d_h)


def _fox_bwd(name, qkv, kbias, do, o, lse, t, comm=None):
    S = qkv.shape[0]
    W = D_MODEL
    nq = S // t
    rep = t // LANES

    def body(q_ref, k_ref, v_ref, kb_ref, do_ref, o_ref, lse_ref, g_ref, rs_ref, dc_ref, dq_scr, dk_scr, dv_scr):
        kb, j = pl.program_id(0), pl.program_id(1)
        lo = _lo_lanes()
        lane = lax.broadcasted_iota(jnp.int32, (1, LANES), 1)
        rows = pl.ds(pl.multiple_of(j * t, t), t)

        @pl.when((kb == 0) & (j == 0))
        def _():
            dq_scr[...] = jnp.zeros_like(dq_scr)
            rs_ref[...] = jnp.zeros_like(rs_ref)

        @pl.when(j == 0)
        def _():
            dk_scr[...] = jnp.zeros_like(dk_scr)
            dv_scr[...] = jnp.zeros_like(dv_scr)
            dc_ref[...] = jnp.zeros_like(dc_ref)

        def step(masked):
            neg = _causal_neg(t) if masked else None

            def pair(p, carry):
                cs = pl.ds(pl.multiple_of(p * LANES, LANES), LANES)
                q2, k2, v2, kb2, do2 = q_ref[:, cs], k_ref[:, cs], v_ref[:, cs], kb_ref[:, cs], do_ref[:, cs]
                dd = do2.astype(F32) * o_ref[:, cs].astype(F32)
                lse_h = _rep_rows(lse_ref[:, cs], lo)
                dq2 = jnp.zeros((t, LANES), F32)
                dv2 = jnp.zeros((t, LANES), F32)
                dk2 = jnp.zeros((t, LANES), F32)
                for hh in range(2):
                    hm = lo if hh == 0 else jnp.logical_not(lo)
                    qh, kh = _fox_operands(q2, k2, kb2, lo, hh)
                    doh = jnp.where(hm, do2, 0)
                    d_h = jnp.sum(jnp.where(hm, dd, 0.0), axis=1, keepdims=True)
                    pr, ds = _fox_head_grads(qh, kh, v2, doh, neg, lse_h[hh], d_h, rep)
                    rs_ref[rows, :] += jnp.where(lane == 2 * p + hh, jnp.sum(ds, axis=1, keepdims=True), 0.0)
                    dc_ref[p, hh:hh + 1, :] += jnp.sum(ds, axis=0, keepdims=True)
                    dsb = ds.astype(BF16)
                    dv2 = dv2 + _dot(pr.astype(BF16), doh, TN)
                    dk2 = dk2 + _dot(dsb, jnp.where(hm, q2, 0), TN)
                    dq2 = dq2 + _dot(dsb, jnp.where(hm, k2, 0), NN)
                dv_scr[:, cs] += dv2
                dk_scr[:, cs] += dk2
                dq_scr[rows, cs] += dq2
                return carry

            lax.fori_loop(0, N_PAIRS, pair, 0, unroll=4)
            if masked:
                g_ref[:, 0:W] = (dq_scr[rows, :] * SOFTMAX_SCALE).astype(BF16)

        @pl.when(j > kb)
        def _():
            step(False)

        @pl.when(j == kb)
        def _():
            step(True)

        @pl.when(j == nq - 1)
        def _():
            g_ref[:, 2 * W:3 * W] = dv_scr[...].astype(BF16)
            g_ref[:, W:2 * W] = dk_scr[...].astype(BF16)

    qrow = pl.BlockSpec((t, W), lambda kb, j: (jnp.maximum(j, kb), 0))
    krow = lambda col: pl.BlockSpec((t, W), lambda kb, j: (kb, col))
    own = pl.BlockSpec((t, W), lambda kb, j: (kb, 0))
    return _call(name, body, (nq, nq), [qrow, krow(1), krow(2), krow(0), qrow, qrow, qrow],
                 [pl.BlockSpec((t, B_IN_PAD), lambda kb, j: (kb, 0)), pl.BlockSpec((S, LANES), lambda kb, j: (0, 0)),
                  pl.BlockSpec((N_PAIRS, 2, t), lambda kb, j: (0, 0, kb))],
                 [jax.ShapeDtypeStruct((S, B_IN_PAD), BF16), jax.ShapeDtypeStruct((S, LANES), F32), jax.ShapeDtypeStruct((N_PAIRS, 2, S), F32)],
                 [pltpu.VMEM((S, W), F32), pltpu.VMEM((t, W), F32), pltpu.VMEM((t, W), F32)],
                 (qkv, qkv, qkv, kbias, do, o, lse), ("arbitrary", "arbitrary"), comm, vmem=FOX_BWD_VMEM)


def _view_spec(tm, R, index=lambda i: (i, 0)):
    return pl.BlockSpec((tm // R, R * D_MODEL), index)


def _matmul_nt_views(name, a, w, dils, tm=512):
    S, K = a.shape

    def body(a_ref, w_ref, *rest):
        res = _dot(a_ref[...].astype(BF16), w_ref[...], NT)
        _write_views([res[:, b * LANES:(b + 1) * LANES] for b in range(N_PAIRS)], rest[-1], rest[:-1], dils, tm)

    return pl.pallas_call(
        body, grid=(S // tm,), in_specs=[pl.BlockSpec((tm, K), lambda i: (i, 0)), pl.BlockSpec((D_MODEL, K), lambda i: (0, 0))],
        out_specs=[_view_spec(tm, R) for R in dils],
        out_shape=[jax.ShapeDtypeStruct((S // R, R * D_MODEL), BF16) for R in dils],
        scratch_shapes=[pltpu.VMEM((N_PAIRS, tm, LANES), F32)], compiler_params=_params("parallel"), name=name)(a, w)


def _write_views(chunks, scr, out_refs, dils, tm):
    if any(R > 1 for R in dils):
        _stage_chunks(scr, chunks)
    for ref, R in zip(out_refs, dils):
        for b, x in enumerate(chunks):
            if R == 1:
                ref[:, b * LANES:(b + 1) * LANES] = x.astype(ref.dtype)
                continue
            for r in range(R):
                col = r * D_MODEL + b * LANES
                ref[:, col:col + LANES] = _strided_rows(scr, b, r, tm // R, R).astype(ref.dtype)


def _combine(name, os_, lses, dils, tm=256):
    S = os_[0].shape[0] * dils[0]
    G = len(dils)

    def body(*refs):
        o_refs, l_refs = refs[:G], refs[G:2 * G]
        o_outs, l_outs = refs[2 * G:3 * G], refs[3 * G:4 * G]
        stage = refs[4 * G:]
        for g, R in enumerate(dils):
            if R == 1:
                continue
            for src, dst in ((o_refs[g], stage[2 * g]), (l_refs[g], stage[2 * g + 1])):
                _unstride(lambda r, b, src=src: src[:, r * D_MODEL + b * LANES:r * D_MODEL + (b + 1) * LANES], R, dst, tm)
        o_chunks, l_chunks = [], []
        for b in range(N_PAIRS):
            cols = slice(b * LANES, (b + 1) * LANES)
            os_b = [o_refs[g][:, cols] if R == 1 else stage[2 * g][b] for g, R in enumerate(dils)]
            ls = [l_refs[g][:, cols] if R == 1 else stage[2 * g + 1][b] for g, R in enumerate(dils)]
            m = functools.reduce(jnp.maximum, ls)
            ws = [jnp.exp(l - m) for l in ls]
            den = functools.reduce(jnp.add, ws)
            o_chunks.append(functools.reduce(jnp.add, [w * o for w, o in zip(ws, os_b)]) / den)
            l_chunks.append(m + jnp.log(den))
        _write_views(o_chunks, stage[0], o_outs, dils, tm)
        _write_views(l_chunks, stage[1], l_outs, dils, tm)

    specs = [_view_spec(tm, R) for R in dils]
    shapes = lambda dt: [jax.ShapeDtypeStruct((S // R, R * D_MODEL), dt) for R in dils]
    res = pl.pallas_call(
        body, grid=(S // tm,), in_specs=specs * 2, out_specs=specs * 2, out_shape=shapes(BF16) + shapes(F32),
        scratch_shapes=[pltpu.VMEM((N_PAIRS, tm, LANES), F32)] * (2 * G), compiler_params=_params("parallel"),
        name=name)(*os_, *lses)
    return res[:G], res[G:]


def _tri_matmul(tri, x):
    hi, mid, lo = _split3(x)
    return _dot(tri, hi, NN) + _dot(tri, mid, NN) + _dot(tri, lo, NN)


def _split3(x):
    hi = x.astype(BF16)
    r1 = x - hi.astype(F32)
    mid = r1.astype(BF16)
    return hi, mid, (r1 - mid.astype(F32)).astype(BF16)


def _gate_fwd(name, z, bf, tb=512):
    S = z.shape[0]

    def body(z_ref, b_ref, kb_ref, carry):
        @pl.when(pl.program_id(0) == 0)
        def _():
            carry[...] = jnp.zeros_like(carry)

        lf = jax.nn.log_sigmoid(z_ref[...] + b_ref[...])
        ri = lax.broadcasted_iota(jnp.int32, (tb, tb), 0)
        ci = lax.broadcasted_iota(jnp.int32, (tb, tb), 1)
        tri = (ci <= ri).astype(BF16)
        c = _tri_matmul(tri, lf) + carry[...]
        carry[...] = c[tb - 1:tb, :]
        head = lax.broadcasted_iota(jnp.int32, (LANES, D_MODEL), 0)
        col = lax.broadcasted_iota(jnp.int32, (LANES, D_MODEL), 1)
        base = (head >> 1) * LANES + jnp.where((head & 1) == 0, HEAD_DIM, 0)
        kb = jnp.zeros((tb, D_MODEL), F32)
        for i, piece in enumerate(_split3(-c)):
            place = ((col == base + i) & (head < N_HEADS)).astype(BF16)
            kb = kb + _dot(piece, place, NN)
        kb_ref[...] = kb.astype(BF16)

    row = pl.BlockSpec((tb, LANES), lambda i: (i, 0))
    return pl.pallas_call(
        body, grid=(S // tb,), in_specs=[row, pl.BlockSpec((1, LANES), lambda i: (0, 0))],
        out_specs=pl.BlockSpec((tb, D_MODEL), lambda i: (i, 0)), out_shape=jax.ShapeDtypeStruct((S, D_MODEL), BF16),
        scratch_shapes=[pltpu.VMEM((1, LANES), F32)], compiler_params=_params("arbitrary"), name=name)(z, bf)


def _gate_bwd(name, dc, z, bf, tb=512):
    S = z.shape[0]
    nb = S // tb

    def body(dc_ref, z_ref, b_ref, dz_ref, db_ref, carry):
        @pl.when(pl.program_id(0) == 0)
        def _():
            carry[...] = jnp.zeros_like(carry)
            db_ref[...] = jnp.zeros_like(db_ref)

        ri = lax.broadcasted_iota(jnp.int32, (tb, tb), 0)
        ci = lax.broadcasted_iota(jnp.int32, (tb, tb), 1)
        tri = (ci >= ri).astype(BF16)
        dlf = _tri_matmul(tri, dc_ref[...]) + carry[...]
        carry[...] = dlf[0:1, :]
        dz = dlf * jax.nn.sigmoid(-(z_ref[...] + b_ref[...]))
        dz_ref[...] = dz
        db_ref[...] += jnp.sum(dz, axis=0, keepdims=True)

    row = pl.BlockSpec((tb, LANES), lambda i: (nb - 1 - i, 0))
    vec = pl.BlockSpec((1, LANES), lambda i: (0, 0))
    return pl.pallas_call(
        body, grid=(nb,), in_specs=[row, row, vec], out_specs=[row, vec],
        out_shape=[jax.ShapeDtypeStruct((S, LANES), F32), jax.ShapeDtypeStruct((1, LANES), F32)],
        scratch_shapes=[pltpu.VMEM((1, LANES), F32)], compiler_params=_params("arbitrary"), name=name)(dc, z, bf)


def _ffn_gu(name, n, wgu, comm=None, tm=1024):
    S, D = n.shape
    nb = N_DEV // 2

    def body(n_ref, wg_ref, wu_ref, gu_ref, act_ref):
        x = n_ref[...]
        g = _dot(x, wg_ref[...], NN)
        u = _dot(x, wu_ref[...], NN)
        gu_ref[0] = g.astype(BF16)
        gu_ref[1] = u.astype(BF16)
        act_ref[...] = (g * jax.nn.sigmoid(g) * u).astype(BF16)

    return _call(
        name, body, (nb, S // tm),
        [pl.BlockSpec((tm, D), lambda j, i: (i, 0)), pl.BlockSpec((None, D, FF_BLK), lambda j, i: (j, 0, 0)),
         pl.BlockSpec((None, D, FF_BLK), lambda j, i: (j + nb, 0, 0))],
        [pl.BlockSpec((2, None, tm, FF_BLK), lambda j, i: (0, j, i, 0)), pl.BlockSpec((None, tm, FF_BLK), lambda j, i: (j, i, 0))],
        [jax.ShapeDtypeStruct((2, nb, S, FF_BLK), BF16), jax.ShapeDtypeStruct((nb, S, FF_BLK), BF16)], [],
        (n, wgu, wgu), ("parallel", "parallel"), comm)


def _ffn_down(name, act, wd, resid, comm=None, tm=1024):
    nb, S, _ = act.shape
    D = wd.shape[1]

    def epilogue(acc, ex, outs, j):
        outs[0][...] = acc + ex[0][...]

    o_spec = pl.BlockSpec((tm, D), lambda i, j, k: (i, 0))
    return _mm_call(name, (S // tm, 1, nb), act, pl.BlockSpec((None, tm, FF_BLK), lambda i, j, k: (k, i, 0)),
                    wd, pl.BlockSpec((FF_BLK, D), lambda i, j, k: (k, 0)), NN,
                    [jax.ShapeDtypeStruct((S, D), F32)], [o_spec], (tm, D), epilogue, (resid,), (o_spec,), comm=comm)


def _ffn_dact(name, dh, wd, gu, comm=None, tm=512):
    S, D = dh.shape
    nb = N_DEV // 2

    def epilogue(acc, ex, outs, j):
        g = ex[0][0].astype(F32)
        u = ex[0][1].astype(F32)
        sig = jax.nn.sigmoid(g)
        outs[0][0] = (acc * u * (sig * (1.0 + g * (1.0 - sig)))).astype(BF16)
        outs[0][1] = (acc * (g * sig)).astype(BF16)

    gu_spec = pl.BlockSpec((2, None, tm, FF_BLK), lambda j, i, k: (0, j, i, 0))
    return _mm_call(name, (nb, S // tm, 1), dh, pl.BlockSpec((tm, D), lambda j, i, k: (i, 0)),
                    wd, pl.BlockSpec((FF_BLK, D), lambda j, i, k: (j, 0)), NT,
                    [jax.ShapeDtypeStruct((2, nb, S, FF_BLK), BF16)], [gu_spec], (tm, FF_BLK), epilogue, (gu,), (gu_spec,),
                    col_axis=0, comm=comm)


def _ffn_dwgu(name, n, dgu, comm=None, tm=1024, tk=1024):
    S, D = n.shape
    dgu8 = dgu.reshape(N_DEV, S, FF_BLK)
    return _mm_call(name, (N_DEV, D // tm, S // tk), n, pl.BlockSpec((tk, tm), lambda d, i, k: (k, i)),
                    dgu8, pl.BlockSpec((None, tk, FF_BLK), lambda d, i, k: (d, k, 0)), TN,
                    [jax.ShapeDtypeStruct((N_DEV, D, FF_BLK), BF16)],
                    [pl.BlockSpec((None, tm, FF_BLK), lambda d, i, k: (d, i, 0))], (tm, FF_BLK), comm=comm)


def _ffn_dwd(name, act, dh, tk=1024):
    nb, S, _ = act.shape
    D = dh.shape[1]
    out = _mm_call(name, (nb, 1, S // tk), act, pl.BlockSpec((None, tk, FF_BLK), lambda b, j, k: (b, k, 0)),
                   dh, pl.BlockSpec((tk, D), lambda b, j, k: (k, 0)), TN,
                   [jax.ShapeDtypeStruct((nb, FF_BLK, D), BF16)],
                   [pl.BlockSpec((None, FF_BLK, D), lambda b, j, k: (b, 0, 0))], (FF_BLK, D))[0]
    return out.reshape(N_DEV, FF_BLK // 2, D)


def _ffn_dn(name, dgu, wgu, comm=None, tm=1024):
    S = dgu.shape[2]
    D = wgu.shape[1]
    dgu8 = dgu.reshape(N_DEV, S, FF_BLK)
    return _mm_call(name, (S // tm, 1, N_DEV), dgu8, pl.BlockSpec((None, tm, FF_BLK), lambda i, j, k: (k, i, 0)),
                    wgu, pl.BlockSpec((None, D, FF_BLK), lambda i, j, k: (k, 0, 0)), NT,
                    [jax.ShapeDtypeStruct((S, D), F32)], [pl.BlockSpec((tm, D), lambda i, j, k: (i, 0))], (tm, D), comm=comm)


def _adamw(name, parts, w, m, v, tr):
    stacked = w.ndim == 3
    by_layer = parts if stacked else [parts]
    n_l, n_parts = len(by_layer), len(by_layer[0])
    rows, cols = w.shape[-2:]
    n_i = rows // tr
    c1 = 1.0 - ADAM_B1 ** ADAM_STEP
    c2 = 1.0 - ADAM_B2 ** ADAM_STEP

    def body(*refs):
        w_ref, m_ref, v_ref, g_ref, d_ref, nm_ref, nv_ref = refs[n_l * n_parts:]
        layer = pl.program_id(0)
        for L in range(n_l):

            @pl.when(layer == L)
            def _(L=L):
                p_refs = refs[L * n_parts:(L + 1) * n_parts]
                g = p_refs[0][...].astype(F32)
                for r in p_refs[1:]:
                    g = g + r[...].astype(F32)
                mm = ADAM_B1 * m_ref[...] + (1.0 - ADAM_B1) * g
                vv = ADAM_B2 * v_ref[...] + (1.0 - ADAM_B2) * (g * g)
                g_ref[...] = g
                nm_ref[...] = mm
                nv_ref[...] = vv
                d_ref[...] = -ADAM_LR * ((mm / c1) / (jnp.sqrt(vv / c2) + ADAM_EPS) + ADAM_WD * w_ref[...])

    def part_spec(L):
        return pl.BlockSpec((tr, cols), lambda l, i: (jnp.where(l == L, i, jnp.where(l < L, 0, n_i - 1)), 0))

    blk = pl.BlockSpec((None, tr, cols), lambda l, i: (l, i, 0)) if stacked else pl.BlockSpec((tr, cols), lambda l, i: (i, 0))
    out = jax.ShapeDtypeStruct(w.shape, F32)
    return pl.pallas_call(
        body, grid=(n_l, n_i), in_specs=[part_spec(L) for L in range(n_l) for _ in range(n_parts)] + [blk] * 3,
        out_specs=[blk] * 4, out_shape=[out] * 4, compiler_params=_params("arbitrary", "parallel"),
        name=name)(*[p for ps in by_layer for p in ps], w, m, v)


def _position():
    return lax.axis_index("x"), lax.axis_index("y"), lax.axis_index("c")


def _other_chips():
    x, y, _ = _position()
    return [(1 - x, y), (x, 1 - y), (1 - x, 1 - y)]


def _remote(src, dst, send, recv, k, to):
    return pltpu.make_async_remote_copy(src_ref=src, dst_ref=dst, send_sem=send.at[k], recv_sem=recv.at[k],
                                        device_id=to, device_id_type=MESH)


def _ag_send(blocks, direct=False):
    n_peer = 7 if direct else 4

    def copies(ins, outs, send, recv, local, r0=0, l0=0):
        x, y, c = _position()
        me = 4 * x + 2 * y + c
        peers = [(x, y, 1 - c)] + [(px, py, c) for px, py in _other_chips()]
        if direct:
            peers += [(px, py, 1 - c) for px, py in _other_chips()]
        cps = []
        for t, (src, dst) in enumerate(zip(ins, outs)):
            cps.append(pltpu.make_async_copy(src, dst.at[me], local.at[l0 + t]))
            cps += [_remote(src, dst.at[me], send, recv, r0 + n_peer * t + k, to) for k, to in enumerate(peers)]
        return cps

    outs = tuple(jax.ShapeDtypeStruct((N_DEV,) + b.shape, b.dtype) for b in blocks)
    return _Comm(tuple(blocks), outs, {}, copies, n_peer * len(blocks), len(blocks))


def _ag_forward(bufs):
    def copies(ins, outs, send, recv, local, r0=0, l0=0):
        x, y, c = _position()
        cps = []
        for t, buf in enumerate(outs):
            for k, (px, py) in enumerate(_other_chips()):
                slot = buf.at[4 * px + 2 * py + c]
                cps.append(_remote(slot, slot, send, recv, r0 + 3 * t + k, (x, y, 1 - c)))
        return cps

    outs = tuple(jax.ShapeDtypeStruct(b.shape, b.dtype) for b in bufs)
    return _Comm(tuple(bufs), outs, {t: t for t in range(len(bufs))}, copies, 3 * len(bufs), 0)


def _rs_swap(shares):
    def copies(ins, outs, send, recv, local, r0=0, l0=0):
        x, y, c = _position()
        return [_remote(src.at[:, 1 - c], dst, send, recv, r0 + t, (x, y, 1 - c)) for t, (src, dst) in enumerate(zip(ins, outs))]

    ins = tuple(s.reshape((4, 2) + s.shape[1:]) for s in shares)
    outs = tuple(jax.ShapeDtypeStruct((4,) + s.shape[1:], s.dtype) for s in shares)
    return _Comm(ins, outs, {}, copies, len(shares), 0)


def _rs_exchange(sums):
    def copies(ins, outs, send, recv, local, r0=0, l0=0):
        _, _, c = _position()
        return [_remote(src.at[2 * px + py], dst.at[k], send, recv, r0 + 3 * t + k, (px, py, c))
                for t, (src, dst) in enumerate(zip(ins, outs)) for k, (px, py) in enumerate(_other_chips())]

    outs = tuple(jax.ShapeDtypeStruct((3,) + s.shape[1:], s.dtype) for s in sums)
    return _Comm(tuple(sums), outs, {}, copies, 3 * len(sums), 0)


def _comm_call(name, comm):
    return _call(name, lambda: None, (), [], [], [], [], (), (), comm)


def _pair_sum(name, share, got, core, tr):
    _, rows, cols = share.shape

    def body(c_ref, a_ref, b_ref, o_ref):
        o_ref[...] = (a_ref[...].astype(F32) + b_ref[...].astype(F32)).astype(o_ref.dtype)

    grid_spec = pltpu.PrefetchScalarGridSpec(
        num_scalar_prefetch=1, grid=(4, rows // tr),
        in_specs=[pl.BlockSpec((None, None, tr, cols), lambda q, i, c: (q, c[0], i, 0)),
                  pl.BlockSpec((None, tr, cols), lambda q, i, c: (q, i, 0))],
        out_specs=pl.BlockSpec((None, tr, cols), lambda q, i, c: (q, i, 0)))
    return pl.pallas_call(
        body, grid_spec=grid_spec, out_shape=jax.ShapeDtypeStruct((4, rows, cols), share.dtype),
        compiler_params=_params("parallel", "parallel"), name=name)(core, share.reshape(4, 2, rows, cols), got)


TENSORS = ("a_w_in", "a_w_out", "b_w_in", "b_w_out", "gu0", "gu1", "dn0", "dn1")
ROW_TILE = {"a_w_in": 256, "a_w_out": 128, "b_w_in": 256, "b_w_out": 128, "gu0": 256, "gu1": 256, "dn0": 176, "dn1": 176}
A_BLK = 9 * D_MODEL // N_DEV
B_BLK = 386
B_IN = 3 * D_MODEL + N_HEADS
B_IN_PAD = 3 * D_MODEL + LANES


def kernel(x, a_norm, a_w_in, a_w_out, b_norm, b_w_in, b_f, b_w_out, ffn_norm, ffn_w_gu, ffn_w_down, final_norm, loss_target, m_a_norm, m_a_w_in, m_a_w_out, m_b_norm, m_b_w_in, m_b_f, m_b_w_out, m_ffn_norm, m_ffn_w_gu, m_ffn_w_down, m_final_norm, v_a_norm, v_a_w_in, v_a_w_out, v_b_norm, v_b_w_in, v_b_f, v_b_w_out, v_ffn_norm, v_ffn_w_gu, v_ffn_w_down, v_final_norm):
    S = x.shape[1]
    xi, yi, ci = _position()
    dev = 4 * xi + 2 * yi + ci
    core = ci.reshape(1).astype(jnp.int32)
    h0, target = x.reshape(S, D_MODEL), loss_target.reshape(S, D_MODEL)

    def shards(a_in, a_out, b_in, b_out, gu, dn):
        return {"a_w_in": a_in[0], "a_w_out": a_out[0], "b_w_in": b_in[0], "b_w_out": b_out[0],
                "gu0": gu[0], "gu1": gu[1], "dn0": dn[0], "dn1": dn[1]}

    w_sh = shards(a_w_in, a_w_out, b_w_in, b_w_out, ffn_w_gu, ffn_w_down)
    m_sh = shards(m_a_w_in, m_a_w_out, m_b_w_in, m_b_w_out, m_ffn_w_gu, m_ffn_w_down)
    v_sh = shards(v_a_w_in, v_a_w_out, v_b_w_in, v_b_w_out, v_ffn_w_gu, v_ffn_w_down)
    wb = {n: w_sh[n].astype(BF16) for n in TENSORS}
    bf_pad = jnp.pad(b_f, ((0, 0), (0, LANES - N_HEADS)))
    tabs = _rope_tables(S)

    g_ain, g_aout = _comm_call("gather_a", _ag_send([wb["a_w_in"], wb["a_w_out"]]))
    dils = [dil for _, dil in DILATED_PATTERNS]
    n0_views, (g_ain, g_aout) = _rms_fwd("rms_a", h0, a_norm[0], dils, _ag_forward([g_ain, g_aout]))
    n0 = n0_views[0]
    w_a_in = g_ain.transpose(1, 0, 2).reshape(D_MODEL, 9 * D_MODEL)
    sends = [[wb["dn0"], jnp.pad(b_norm, ((0, 7), (0, 0)))], None, [wb["gu0"]]]
    qkv_a, sent = [], {}
    for g, dil in enumerate(dils):
        qkv_g, sent[g] = _a_proj("proj_a%d" % g, n0, w_a_in, g, dil, tabs, None if sends[g] is None else _ag_send(sends[g]))
        qkv_a.append(qkv_g)
    cols = [lambda r: r] * 3
    groups = [(g, dil, S // dil, qkv_a[g]) for g, (window, dil) in enumerate(DILATED_PATTERNS)]
    fwd = [_dil_fwd("dil_fwd%d" % g, view, *cols, dil, L, _ag_send([wb["b_w_in"], wb["b_w_out"]]) if g == 0 else None)
           for g, dil, L, view in groups]
    later = list(fwd[0][2:]) + [sent[2][0]] + list(sent[0])
    o_views, lse_views = _combine("dil_combine", [f[0] for f in fwd], [f[1] for f in fwd], dils)
    o_a = o_views[0]
    w_a_out = g_aout.reshape(D_MODEL, D_MODEL)
    h1, (g_bin, g_bout, g_gu0, g_dn0, g_bnorm) = _matmul("out_a", o_a, w_a_out, "nn", F32, TM, 1024, 1024, resid=h0,
                                                         comm=_ag_forward(later))

    n1 = _rms_fwd("rms_f0", h1, ffn_norm[0])
    gu0, act0 = _ffn_gu("gu_f0", n1, g_gu0)
    w_dn0 = g_dn0.reshape(D_FF, D_MODEL)
    h2 = _ffn_down("down_f0", act0, w_dn0, h1)[0]

    b_norm_full = g_bnorm[:, 0].reshape(D_MODEL)
    w_b_in = g_bin.transpose(1, 0, 2).reshape(D_MODEL, B_IN)
    w_b_gate = jnp.pad(w_b_in[:, 3 * D_MODEL:], ((0, 0), (0, LANES - N_HEADS)))
    w_b_cat = jnp.concatenate([w_b_in[:, :3 * D_MODEL], w_b_gate], axis=1)
    w_b_out = g_bout.reshape(D_MODEL, D_MODEL)
    n2 = _rms_fwd("rms_b", h2, b_norm_full)
    qkv = _matmul("proj_b", n2, w_b_in[:, :3 * D_MODEL], "nn", BF16, TM, 1024, 1024, col0_scale=SOFTMAX_SCALE)
    z = _matmul("gate_b", n2, w_b_gate, "nn", F32, TM, LANES, 1024)
    kbias = _gate_fwd("gate_cumsum", z, bf_pad)
    tf = min(S, 512)
    o_b, lse_b, g_gu1, g_dn1 = _fox_fwd("fox_fwd", qkv, kbias, tf, _ag_send([wb["gu1"], wb["dn1"]]))
    h3, (g_gu1, g_dn1) = _matmul("out_b", o_b, w_b_out, "nn", F32, TM, 1024, 1024, resid=h2, comm=_ag_forward([g_gu1, g_dn1]))

    w_dn1 = g_dn1.reshape(D_FF, D_MODEL)
    n3 = _rms_fwd("rms_f1", h3, ffn_norm[1])
    gu1, act1 = _ffn_gu("gu_f1", n3, g_gu1)
    h4 = _ffn_down("down_f1", act1, w_dn1, h3)[0]

    dh4, d_final, loss, dh4_16 = _loss_head("loss_head", h4, final_norm, target)

    share, got, sums, others = {}, {}, {}, {}

    def pair_sums(*names):
        for n in names:
            sums[n] = _pair_sum("pair_" + n, share[n], got[n], core, ROW_TILE[n])

    dgu1 = _ffn_dact("dact_f1", dh4_16, w_dn1, gu1)[0]
    share["dn1"] = _ffn_dwd("dwd_f1", act1, dh4_16)
    share["gu1"] = _ffn_dwgu("dwgu_f1", n3, dgu1)[0]
    dn3, got["gu1"], got["dn1"] = _ffn_dn("dn_f1", dgu1, g_gu1, _rs_swap([share["gu1"], share["dn1"]]))
    dh3, d_ffn1, dh3_16 = _rms_bwd("rmsb_f1", dn3, h3, ffn_norm[1], dh4)
    pair_sums("gu1", "dn1")

    do_b = _matmul("dout_b", dh3_16, w_b_out, "nt", BF16, TM, 1024, 1024)
    share["b_w_out"] = _matmul("dwout_b", o_b, dh3_16, "tn", BF16, TM, 1024, 1024).reshape(N_DEV, 128, D_MODEL)
    dqkv_b, ds_rowsum, ds_colsum, others["gu1"], others["dn1"] = _fox_bwd(
        "fox_bwd", qkv, kbias, do_b, o_b, lse_b, tf, _rs_exchange([sums["gu1"], sums["dn1"]]))
    dc = ds_rowsum[:, :N_HEADS] - ds_colsum.reshape(N_HEADS, S).T
    dz, d_bf = _gate_bwd("gate_bwd", jnp.pad(dc, ((0, 0), (0, LANES - N_HEADS))), z, bf_pad)
    dproj_b = lax.dynamic_update_slice(dqkv_b, dz.astype(BF16), (0, 3 * D_MODEL))
    dw_b_in = _matmul("dwin_b", n2, dproj_b, "tn", BF16, TM, B_IN_PAD // 5, 1024)
    dn2 = _matmul("dn_b", dproj_b, w_b_cat, "nt", F32, TM, 1024, B_IN_PAD // 5)
    dh2, d_bnorm, dh2_16 = _rms_bwd("rmsb_b", dn2, h2, b_norm_full, dh3)
    share["b_w_in"] = dw_b_in[:, :B_IN].reshape(D_MODEL, N_DEV, B_BLK).transpose(1, 0, 2)

    dgu0, got["b_w_in"], got["b_w_out"] = _ffn_dact("dact_f0", dh2_16, w_dn0, gu0, _rs_swap([share["b_w_in"], share["b_w_out"]]))
    share["dn0"] = _ffn_dwd("dwd_f0", act0, dh2_16)
    pair_sums("b_w_in", "b_w_out")
    share["gu0"], others["b_w_in"], others["b_w_out"] = _ffn_dwgu(
        "dwgu_f0", n1, dgu0, _rs_exchange([sums["b_w_in"], sums["b_w_out"]]))
    dn1, got["gu0"], got["dn0"] = _ffn_dn("dn_f0", dgu0, g_gu0, _rs_swap([share["gu0"], share["dn0"]]))
    dh1, d_ffn0, dh1_16 = _rms_bwd("rmsb_f0", dn1, h1, ffn_norm[0], dh2)
    pair_sums("gu0", "dn0")

    do_views = _matmul_nt_views("dout_a", dh1_16, w_a_out, dils)
    share["a_w_out"] = _matmul("dwout_a", o_a, dh1_16, "tn", BF16, TM, 1024, 1024).reshape(N_DEV, 128, D_MODEL)
    pieces = []
    for g, dil, L, view in groups:
        rot = tuple(tb.reshape(L, dil * LANES) for tb in tabs)
        res = _dil_bwd("dil_bwd%d" % g, view, do_views[g], o_views[g], lse_views[g], rot, dil, L,
                       _rs_exchange([sums["gu0"], sums["dn0"]]) if g == 0 else None)
        pieces.append(res[:3])
        if g == 0:
            others["gu0"], others["dn0"] = res[3:]
    dws = [_a_dw("dwin_a%d" % g, n0_views[g], pieces[g], dil, None)[0] for g, dil in enumerate(dils)]
    share["a_w_in"] = jnp.concatenate(dws, axis=1).reshape(D_MODEL, N_DEV, A_BLK).transpose(1, 0, 2)
    got["a_w_in"], got["a_w_out"] = _comm_call("swap_a", _rs_swap([share["a_w_in"], share["a_w_out"]]))
    pair_sums("a_w_in", "a_w_out")
    dn0, others["a_w_in"], others["a_w_out"] = _a_dn("dn_a", [p for ps in pieces for p in ps], dils, w_a_in,
                                                     _rs_exchange([sums["a_w_in"], sums["a_w_out"]]))
    dx, d_anorm = _rms_bwd("rmsb_a", dn0, h0, a_norm[0], dh1, copy16=False)

    misc = jnp.concatenate([d_bf[:, :N_HEADS], loss[:, :1], jnp.zeros((1, D_MODEL - N_HEADS - 1), F32)], axis=1)
    small = jnp.concatenate([d_anorm, d_ffn0, d_ffn1, d_final, d_bnorm, misc, jnp.zeros((2, D_MODEL), F32)], axis=0)
    small_all, = _comm_call("gather_small", _ag_send([small], direct=True))

    def grad_parts(n):
        return [lax.dynamic_index_in_dim(sums[n], 2 * xi + yi, axis=0, keepdims=False)] + [others[n][k] for k in range(3)]

    outs = {n: _adamw("adamw_" + n, grad_parts(n), w_sh[n], m_sh[n], v_sh[n], ROW_TILE[n])
            for n in ("a_w_in", "a_w_out", "b_w_in", "b_w_out")}
    outs["gu"] = _adamw("adamw_gu", [grad_parts("gu0"), grad_parts("gu1")], ffn_w_gu, m_ffn_w_gu, v_ffn_w_gu, ROW_TILE["gu0"])
    outs["dn"] = _adamw("adamw_dn", [grad_parts("dn0"), grad_parts("dn1")], ffn_w_down, m_ffn_w_down, v_ffn_w_down, ROW_TILE["dn0"])

    pad_vec = lambda a: jnp.pad(a, ((0, 0), (0, D_MODEL - a.shape[1])))

    def small_pack(an, fn, fin, bf):
        return jnp.concatenate([an, fn, fin.reshape(1, D_MODEL), jnp.zeros((1, D_MODEL), F32), pad_vec(bf),
                                jnp.zeros((2, D_MODEL), F32)], axis=0)

    sg, sd, sm, sv = _adamw("adamw_small", [small_all[d] for d in range(N_DEV)], small_pack(a_norm, ffn_norm, final_norm, b_f),
                            small_pack(m_a_norm, m_ffn_norm, m_final_norm, m_b_f),
                            small_pack(v_a_norm, v_ffn_norm, v_final_norm, v_b_f), 8)
    g_bn = lax.dynamic_slice(sg[4:5], (0, dev * LANES), (1, LANES))
    bn = _adamw("adamw_b_norm", [g_bn], b_norm, m_b_norm, v_b_norm, 1)

    def tree(i):
        full = lambda name, ref: outs[name][i].reshape(ref.shape)
        sml = (sg, sd, sm, sv)[i]
        return dict(
            a_norm=sml[0:1], a_w_in=full("a_w_in", a_w_in), a_w_out=full("a_w_out", a_w_out), b_norm=bn[i],
            b_w_in=full("b_w_in", b_w_in), b_f=sml[5:6, :N_HEADS], b_w_out=full("b_w_out", b_w_out), ffn_norm=sml[1:3],
            ffn_w_gu=outs["gu"][i], ffn_w_down=outs["dn"][i], final_norm=sml[3])

    order = ("a_norm", "a_w_in", "a_w_out", "b_norm", "b_w_in", "b_f", "b_w_out", "ffn_norm", "ffn_w_gu", "ffn_w_down", "final_norm")
    result = [sg[5, N_HEADS], dx.reshape(x.shape)]
    for i in range(4):
        t = tree(i)
        result += [t[n] for n in order]
    return tuple(result)
```
